```python
import jax, jax.numpy as jnp
from jax import lax
import numpy as np

D_MODEL = 1024
BATCH = 8
SEQ = 8192
DEPTH = 2

GRID_W = 64
CTX_LEN = 256
EPS = 1e-6
D_FF = 4 * D_MODEL
SGU_HEADS = 4
SGU_HEAD_DIM = 64
SGU_WIDTH = SGU_HEADS * SGU_HEAD_DIM
SGU_CHUNK = 128
GLA_HEADS = 4
GLA_DK = 32
GLA_DV = 64
GLA_K_WIDTH = GLA_HEADS * GLA_DK
GLA_V_WIDTH = GLA_HEADS * GLA_DV
GLA_GATE_RANK = 16
GLA_TAU = 16.0
GLA_CHUNK = 64
MLA_HEADS = 4
MLA_Q_RANK = 256
MLA_KV_RANK = 256
MLA_NOPE = 128
MLA_ROPE = 64
MLA_V = 128
MLA_WIDTH = MLA_HEADS * MLA_V
ROPE_BASE = 10000.0
ATTN_BLOCK = 128
MIX_WIDTH = SGU_WIDTH + GLA_V_WIDTH + MLA_WIDTH
KV_SPLITS = (GLA_K_WIDTH, GLA_V_WIDTH, GLA_GATE_RANK, GLA_GATE_RANK, MLA_KV_RANK, MLA_ROPE)
Q_SPLITS = (SGU_WIDTH, SGU_WIDTH, GLA_K_WIDTH, GLA_V_WIDTH, MLA_Q_RANK)
KV_COLS = sum(KV_SPLITS)
IN_COLS = KV_COLS + sum(Q_SPLITS)

kernel_name = "hybrid_sgu_gla_mla_prefix_dit_block"


def split_cols(t, widths):
    idx = np.cumsum(widths)[:-1].tolist()
    return jnp.split(t, idx, axis=-1)


def rmsnorm(x, w):
    xf = x.astype(jnp.float32)
    y = xf * lax.rsqrt(jnp.mean(xf * xf, axis=-1, keepdims=True) + EPS)
    return (y * w.astype(jnp.float32)).astype(x.dtype)


def layernorm(x, w, b):
    xf = x.astype(jnp.float32)
    mu = jnp.mean(xf, axis=-1, keepdims=True)
    var = jnp.mean(jnp.square(xf - mu), axis=-1, keepdims=True)
    y = (xf - mu) * lax.rsqrt(var + EPS)
    return (y * w.astype(jnp.float32) + b.astype(jnp.float32)).astype(x.dtype)


def modulate(xn, shift, scale):
    return xn * (1.0 + scale) + shift


def axial_angles(n):
    rows = n // GRID_W
    row = jnp.repeat(jnp.arange(rows), GRID_W).astype(jnp.float32)
    col = jnp.tile(jnp.arange(GRID_W), rows).astype(jnp.float32)
    half = MLA_ROPE // 2
    freq = ROPE_BASE ** (-jnp.arange(half // 2, dtype=jnp.float32) * 2.0 / half)
    return row[:, None] * freq[None, :], col[:, None] * freq[None, :]


def rotate(x, ang):
    m = x.shape[-1] // 2
    xf = x.astype(jnp.float32)
    x1, x2 = xf[..., :m], xf[..., m:]
    cos, sin = jnp.cos(ang), jnp.sin(ang)
    return jnp.concatenate([x1 * cos - x2 * sin, x1 * sin + x2 * cos], axis=-1).astype(x.dtype)


def axial_rope(x, ang_row, ang_col):
    half = x.shape[-1] // 2
    return jnp.concatenate([rotate(x[..., :half], ang_row), rotate(x[..., half:], ang_col)], axis=-1)


def to_heads(t, d):
    bsz, n, _ = t.shape
    return t.reshape(bsz, n, -1, d).transpose(0, 2, 1, 3)


def sgu(p_u, p_v, norm_w, norm_b, w_s, b_s):
    bsz, n, _ = p_u.shape
    u = jax.nn.gelu(p_u)
    v = layernorm(jax.nn.gelu(p_v), norm_w, norm_b)
    v = v.reshape(bsz, n // SGU_CHUNK, SGU_CHUNK, SGU_HEADS, SGU_HEAD_DIM)
    s = jnp.einsum('hij,bcjhe->bcihe', w_s, v) + b_s.T[:, :, None]
    return u * s.reshape(bsz, n, SGU_WIDTH)


def gla_log_gate(p_g, w, b):
    z = (p_g @ w + b).astype(jnp.float32)
    return jax.nn.log_sigmoid(z) / GLA_TAU


def gla_scan(q, k, v, g, s0):
    bsz, h, n, _ = q.shape
    nc = n // GLA_CHUNK

    def to_chunks(t):
        return jnp.moveaxis(t.reshape(bsz, h, nc, GLA_CHUNK, t.shape[-1]), 2, 0)

    mask = jnp.tril(jnp.ones((GLA_CHUNK, GLA_CHUNK), bool))[:, :, None]

    def step(s, inp):
        qc, kc, vc, gc = inp
        b = jnp.cumsum(gc, axis=2)
        diff = b[:, :, :, None, :] - b[:, :, None, :, :]
        decay = jnp.where(mask, jnp.exp(jnp.where(mask, diff, 0.0)), 0.0)
        att = jnp.einsum('bhid,bhjd,bhijd->bhij', qc, kc, decay)
        o = jnp.einsum('bhij,bhje->bhie', att, vc) + jnp.einsum('bhid,bhde->bhie', qc * jnp.exp(b), s)
        b_last = b[:, :, -1:, :]
        s_new = jnp.exp(b_last)[:, :, 0, :, None] * s + jnp.einsum('bhjd,bhje->bhde', kc * jnp.exp(b_last - b), vc)
        return s_new, o

    s_fin, o = lax.scan(step, s0, (to_chunks(q), to_chunks(k), to_chunks(v), to_chunks(g)))
    o = jnp.moveaxis(o, 0, 2).reshape(bsz, h, n, v.shape[-1])
    return o, s_fin


def gla_final_state(k, v, g):
    b = jnp.cumsum(g, axis=2)
    return jnp.einsum('bhjd,bhje->bhde', k * jnp.exp(b[:, :, -1:, :] - b), v)


def gla_output(o, r, norm_w, dtype):
    bsz, h, n, dv = o.shape
    o = rmsnorm(o.transpose(0, 2, 1, 3), norm_w)
    o = o * jax.nn.silu(r.reshape(bsz, n, h, dv).astype(jnp.float32))
    return o.reshape(bsz, n, h * dv).astype(dtype)


def mla_kv(p_ckv, p_krope, kv_norm_w, w_ukv, ang):
    bsz, n, _ = p_ckv.shape
    kv = (rmsnorm(p_ckv, kv_norm_w) @ w_ukv).reshape(bsz, n, MLA_HEADS, MLA_NOPE + MLA_V)
    k_nope, v = kv[..., :MLA_NOPE], kv[..., MLA_NOPE:]
    k_rope = p_krope if ang is None else axial_rope(p_krope, ang[0], ang[1])
    return k_nope, k_rope, v


def mla_q(p_dq, q_norm_w, w_uq, ang):
    bsz, n, _ = p_dq.shape
    q = (rmsnorm(p_dq, q_norm_w) @ w_uq).reshape(bsz, n, MLA_HEADS, MLA_NOPE + MLA_ROPE)
    q_nope, q_rope = q[..., :MLA_NOPE], q[..., MLA_NOPE:]
    if ang is not None:
        q_rope = axial_rope(q_rope, ang[0][:, None, :], ang[1][:, None, :])
    return q_nope, q_rope


def mla_attend(q_nope, q_rope, k_nope, k_rope, v):
    scale = (MLA_NOPE + MLA_ROPE) ** -0.5
    s = jnp.einsum('bqhd,bkhd->bhqk', q_nope, k_nope) + jnp.einsum('bqhr,bkr->bhqk', q_rope, k_rope)
    p = jax.nn.softmax(s.astype(jnp.float32) * scale, axis=-1).astype(v.dtype)
    return jnp.einsum('bhqk,bkhd->bqhd', p, v)


def mla_attend_blocked(q_nope, q_rope, k_nope, k_rope, v):
    bsz, n = q_nope.shape[:2]
    nb = n // ATTN_BLOCK
    qn = jnp.moveaxis(q_nope.reshape(bsz, nb, ATTN_BLOCK, MLA_HEADS, MLA_NOPE), 1, 0)
    qr = jnp.moveaxis(q_rope.reshape(bsz, nb, ATTN_BLOCK, MLA_HEADS, MLA_ROPE), 1, 0)
    out = lax.map(lambda qs: mla_attend(qs[0], qs[1], k_nope, k_rope, v), (qn, qr))
    return jnp.moveaxis(out, 0, 1).reshape(bsz, n, MLA_WIDTH)


def mixing(h, hc, with_ctx_out, w_in, sgu_norm_w, sgu_norm_b, sgu_w, sgu_b, gla_wg_fwd, gla_bg_fwd,
           gla_wg_bwd, gla_bg_bwd, gla_norm_w, mla_q_norm_w, mla_w_uq, mla_kv_norm_w, mla_w_ukv, ang):
    bsz, n, _ = h.shape
    p = h @ w_in
    gk, gv, ggf, ggb, ckv, kr = split_cols(p[..., :KV_COLS], KV_SPLITS)
    su, sv, gq, gr, dq = split_cols(p[..., KV_COLS:], Q_SPLITS)
    pc = hc @ (w_in if with_ctx_out else w_in[:, :KV_COLS])
    gk_c, gv_c, ggf_c, ggb_c, ckv_c, kr_c = split_cols(pc[..., :KV_COLS], KV_SPLITS)

    y_sgu = sgu(su, sv, sgu_norm_w, sgu_norm_b, sgu_w, sgu_b)

    def kvg(k_, v_, gf_, gb_):
        return (to_heads(k_, GLA_DK).astype(jnp.float32), to_heads(v_, GLA_DV).astype(jnp.float32),
                to_heads(gla_log_gate(gf_, gla_wg_fwd, gla_bg_fwd), GLA_DK),
                to_heads(gla_log_gate(gb_, gla_wg_bwd, gla_bg_bwd), GLA_DK))

    def gq_heads(q_):
        return to_heads(q_, GLA_DK).astype(jnp.float32) * (GLA_DK ** -0.5)

    flip = lambda t: jnp.flip(t, axis=2)
    k_l, v_l, gf_l, gb_l = kvg(gk, gv, ggf, ggb)
    k_c, v_c, gf_c, gb_c = kvg(gk_c, gv_c, ggf_c, ggb_c)
    q_l = gq_heads(gq)
    if with_ctx_out:
        su_c, sv_c, gq_c, gr_c, dq_c = split_cols(pc[..., KV_COLS:], Q_SPLITS)
        q_c = gq_heads(gq_c)
        zero = jnp.zeros((bsz, GLA_HEADS, GLA_DK, GLA_DV), jnp.float32)
        o_cf, s_f = gla_scan(q_c, k_c, v_c, gf_c, zero)
        o_cb, s_b = gla_scan(flip(q_c), flip(k_c), flip(v_c), flip(gb_c), zero)
        y_gla_c = gla_output(o_cf + flip(o_cb), gr_c, gla_norm_w, hc.dtype)
    else:
        s_f = gla_final_state(k_c, v_c, gf_c)
        s_b = gla_final_state(flip(k_c), flip(v_c), flip(gb_c))
    o_lf, _ = gla_scan(q_l, k_l, v_l, gf_l, s_f)
    o_lb, _ = gla_scan(flip(q_l), flip(k_l), flip(v_l), flip(gb_l), s_b)
    y_gla = gla_output(o_lf + flip(o_lb), gr, gla_norm_w, h.dtype)

    kn_c, krp_c, vv_c = mla_kv(ckv_c, kr_c, mla_kv_norm_w, mla_w_ukv, None)
    kn_l, krp_l, vv_l = mla_kv(ckv, kr, mla_kv_norm_w, mla_w_ukv, ang)
    qn_l, qr_l = mla_q(dq, mla_q_norm_w, mla_w_uq, ang)
    y_mla = mla_attend_blocked(qn_l, qr_l, jnp.concatenate([kn_c, kn_l], axis=1),
                               jnp.concatenate([krp_c, krp_l], axis=1), jnp.concatenate([vv_c, vv_l], axis=1))
    y = jnp.concatenate([y_sgu, y_gla, y_mla], axis=-1)

    yc = None
    if with_ctx_out:
        y_sgu_c = sgu(su_c, sv_c, sgu_norm_w, sgu_norm_b, sgu_w, sgu_b)
        qn_c, qr_c = mla_q(dq_c, mla_q_norm_w, mla_w_uq, None)
        y_mla_c = mla_attend(qn_c, qr_c, kn_c, krp_c, vv_c).reshape(bsz, -1, MLA_WIDTH)
        yc = jnp.concatenate([y_sgu_c, y_gla_c, y_mla_c], axis=-1)
    return y, yc


def ffn(h, w1, w2):
    a = jax.nn.relu(h @ w1)
    return (a * a) @ w2


def _fwd_setup_inputs(seed: int = 0) -> dict:
    key = jax.random.key(seed)
    ks = jax.random.split(key, 32)
    L, D = DEPTH, D_MODEL

    def nrm(k, shape, scale):
        return jax.random.normal(k, shape, jnp.float32) * scale

    return {
        "x": nrm(ks[0], (BATCH, SEQ, D), 1.0),
        "c": nrm(ks[1], (BATCH, D), 1.0),
        "ctx": nrm(ks[2], (BATCH, CTX_LEN, D), 1.0),
        "c_ctx": nrm(ks[3], (D,), 1.0),
        "w_mod": nrm(ks[4], (L, D, 6 * D), 0.5 * D ** -0.5),
        "b_mod": nrm(ks[5], (L, 6 * D), 0.02),
        "norm1_w": 1.0 + nrm(ks[6], (L, D), 0.02),
        "w_in": nrm(ks[7], (L, D, IN_COLS), D ** -0.5),
        "w_out": nrm(ks[8], (L, MIX_WIDTH, D), MIX_WIDTH ** -0.5),
        "sgu_norm_w": 1.0 + nrm(ks[9], (L, SGU_WIDTH), 0.02),
        "sgu_norm_b": nrm(ks[10], (L, SGU_WIDTH), 0.02),
        "sgu_w": nrm(ks[11], (L, SGU_HEADS, SGU_CHUNK, SGU_CHUNK), SGU_CHUNK ** -0.5),
        "sgu_b": 1.0 + nrm(ks[12], (L, SGU_HEADS, SGU_CHUNK), 0.02),
        "gla_wg_fwd": nrm(ks[13], (L, GLA_GATE_RANK, GLA_K_WIDTH), GLA_GATE_RANK ** -0.5),
        "gla_bg_fwd": nrm(ks[14], (L, GLA_K_WIDTH), 0.1),
        "gla_wg_bwd": nrm(ks[15], (L, GLA_GATE_RANK, GLA_K_WIDTH), GLA_GATE_RANK ** -0.5),
        "gla_bg_bwd": nrm(ks[16], (L, GLA_K_WIDTH), 0.1),
        "gla_norm_w": 1.0 + nrm(ks[17], (L, GLA_DV), 0.02),
        "mla_q_norm_w": 1.0 + nrm(ks[18], (L, MLA_Q_RANK), 0.02),
        "mla_w_uq": nrm(ks[19], (L, MLA_Q_RANK, MLA_HEADS * (MLA_NOPE + MLA_ROPE)), MLA_Q_RANK ** -0.5),
        "mla_kv_norm_w": 1.0 + nrm(ks[20], (L, MLA_KV_RANK), 0.02),
        "mla_w_ukv": nrm(ks[21], (L, MLA_KV_RANK, MLA_HEADS * (MLA_NOPE + MLA_V)), MLA_KV_RANK ** -0.5),
        "norm2_w": 1.0 + nrm(ks[22], (L, D), 0.02),
        "w_ff1": nrm(ks[23], (L, D, D_FF), D ** -0.5),
        "w_ff2": nrm(ks[24], (L, D_FF, D), D_FF ** -0.5),
        "final_norm_w": 1.0 + nrm(ks[25], (D,), 0.02),
    }


def _fwd_reference(x, c, ctx, c_ctx, w_mod, b_mod, norm1_w, w_in, w_out, sgu_norm_w, sgu_norm_b, sgu_w, sgu_b,
              gla_wg_fwd, gla_bg_fwd, gla_wg_bwd, gla_bg_bwd, gla_norm_w, mla_q_norm_w, mla_w_uq,
              mla_kv_norm_w, mla_w_ukv, norm2_w, w_ff1, w_ff2, final_norm_w):
    n = x.shape[1]
    ang = axial_angles(n)
    silu_c = jax.nn.silu(c)
    silu_cc = jax.nn.silu(c_ctx)
    xc = ctx
    for l in range(DEPTH):
        last = l == DEPTH - 1
        mod = (silu_c @ w_mod[l] + b_mod[l])[:, None, :]
        sh1, sc1, g1, sh2, sc2, g2 = jnp.split(mod, 6, axis=-1)
        if last:
            mod_c = silu_cc @ w_mod[l][:, :2 * D_MODEL] + b_mod[l][:2 * D_MODEL]
            sh1c, sc1c = jnp.split(mod_c, 2, axis=-1)
        else:
            mod_c = silu_cc @ w_mod[l] + b_mod[l]
            sh1c, sc1c, g1c, sh2c, sc2c, g2c = jnp.split(mod_c, 6, axis=-1)
        h = modulate(rmsnorm(x, norm1_w[l]), sh1, sc1)
        hc = modulate(rmsnorm(xc, norm1_w[l]), sh1c, sc1c)
        y, yc = mixing(h, hc, not last, w_in[l], sgu_norm_w[l], sgu_norm_b[l], sgu_w[l], sgu_b[l],
                       gla_wg_fwd[l], gla_bg_fwd[l], gla_wg_bwd[l], gla_bg_bwd[l], gla_norm_w[l],
                       mla_q_norm_w[l], mla_w_uq[l], mla_kv_norm_w[l], mla_w_ukv[l], ang)
        x = x + g1 * (y @ w_out[l])
        x = x + g2 * ffn(modulate(rmsnorm(x, norm2_w[l]), sh2, sc2), w_ff1[l], w_ff2[l])
        if not last:
            xc = xc + g1c * (yc @ w_out[l])
            xc = xc + g2c * ffn(modulate(rmsnorm(xc, norm2_w[l]), sh2c, sc2c), w_ff1[l], w_ff2[l])
    return rmsnorm(x, final_norm_w)


import jax as _jax
import jax.numpy as _jnp

TWIN_FORMAT = 'train_step'
FWD_PARAMS = ['x', 'c', 'ctx', 'c_ctx', 'w_mod', 'b_mod', 'norm1_w', 'w_in', 'w_out', 'sgu_norm_w', 'sgu_norm_b', 'sgu_w', 'sgu_b', 'gla_wg_fwd', 'gla_bg_fwd', 'gla_wg_bwd', 'gla_bg_bwd', 'gla_norm_w', 'mla_q_norm_w', 'mla_w_uq', 'mla_kv_norm_w', 'mla_w_ukv', 'norm2_w', 'w_ff1', 'w_ff2', 'final_norm_w']
TWIN_WEIGHTS = ['c_ctx', 'w_mod', 'b_mod', 'norm1_w', 'w_in', 'w_out', 'sgu_norm_w', 'sgu_norm_b', 'sgu_w', 'sgu_b', 'gla_wg_fwd', 'gla_bg_fwd', 'gla_wg_bwd', 'gla_bg_bwd', 'gla_norm_w', 'mla_q_norm_w', 'mla_w_uq', 'mla_kv_norm_w', 'mla_w_ukv', 'norm2_w', 'w_ff1', 'w_ff2', 'final_norm_w']
TWIN_DIFF_INPUT = 'x'
TWIN_INPUTS = ['x', 'c', 'ctx', 'c_ctx', 'w_mod', 'b_mod', 'norm1_w', 'w_in', 'w_out', 'sgu_norm_w', 'sgu_norm_b', 'sgu_w', 'sgu_b', 'gla_wg_fwd', 'gla_bg_fwd', 'gla_wg_bwd', 'gla_bg_bwd', 'gla_norm_w', 'mla_q_norm_w', 'mla_w_uq', 'mla_kv_norm_w', 'mla_w_ukv', 'norm2_w', 'w_ff1', 'w_ff2', 'final_norm_w', 'loss_target', 'm_c_ctx', 'm_w_mod', 'm_b_mod', 'm_norm1_w', 'm_w_in', 'm_w_out', 'm_sgu_norm_w', 'm_sgu_norm_b', 'm_sgu_w', 'm_sgu_b', 'm_gla_wg_fwd', 'm_gla_bg_fwd', 'm_gla_wg_bwd', 'm_gla_bg_bwd', 'm_gla_norm_w', 'm_mla_q_norm_w', 'm_mla_w_uq', 'm_mla_kv_norm_w', 'm_mla_w_ukv', 'm_norm2_w', 'm_w_ff1', 'm_w_ff2', 'm_final_norm_w', 'v_c_ctx', 'v_w_mod', 'v_b_mod', 'v_norm1_w', 'v_w_in', 'v_w_out', 'v_sgu_norm_w', 'v_sgu_norm_b', 'v_sgu_w', 'v_sgu_b', 'v_gla_wg_fwd', 'v_gla_bg_fwd', 'v_gla_wg_bwd', 'v_gla_bg_bwd', 'v_gla_norm_w', 'v_mla_q_norm_w', 'v_mla_w_uq', 'v_mla_kv_norm_w', 'v_mla_w_ukv', 'v_norm2_w', 'v_w_ff1', 'v_w_ff2', 'v_final_norm_w']
TWIN_OUTPUTS = ['loss', 'grad_x', 'grad_c_ctx', 'grad_w_mod', 'grad_b_mod', 'grad_norm1_w', 'grad_w_in', 'grad_w_out', 'grad_sgu_norm_w', 'grad_sgu_norm_b', 'grad_sgu_w', 'grad_sgu_b', 'grad_gla_wg_fwd', 'grad_gla_bg_fwd', 'grad_gla_wg_bwd', 'grad_gla_bg_bwd', 'grad_gla_norm_w', 'grad_mla_q_norm_w', 'grad_mla_w_uq', 'grad_mla_kv_norm_w', 'grad_mla_w_ukv', 'grad_norm2_w', 'grad_w_ff1', 'grad_w_ff2', 'grad_final_norm_w', 'delta_c_ctx', 'delta_w_mod', 'delta_b_mod', 'delta_norm1_w', 'delta_w_in', 'delta_w_out', 'delta_sgu_norm_w', 'delta_sgu_norm_b', 'delta_sgu_w', 'delta_sgu_b', 'delta_gla_wg_fwd', 'delta_gla_bg_fwd', 'delta_gla_wg_bwd', 'delta_gla_bg_bwd', 'delta_gla_norm_w', 'delta_mla_q_norm_w', 'delta_mla_w_uq', 'delta_mla_kv_norm_w', 'delta_mla_w_ukv', 'delta_norm2_w', 'delta_w_ff1', 'delta_w_ff2', 'delta_final_norm_w', 'new_m_c_ctx', 'new_m_w_mod', 'new_m_b_mod', 'new_m_norm1_w', 'new_m_w_in', 'new_m_w_out', 'new_m_sgu_norm_w', 'new_m_sgu_norm_b', 'new_m_sgu_w', 'new_m_sgu_b', 'new_m_gla_wg_fwd', 'new_m_gla_bg_fwd', 'new_m_gla_wg_bwd', 'new_m_gla_bg_bwd', 'new_m_gla_norm_w', 'new_m_mla_q_norm_w', 'new_m_mla_w_uq', 'new_m_mla_kv_norm_w', 'new_m_mla_w_ukv', 'new_m_norm2_w', 'new_m_w_ff1', 'new_m_w_ff2', 'new_m_final_norm_w', 'new_v_c_ctx', 'new_v_w_mod', 'new_v_b_mod', 'new_v_norm1_w', 'new_v_w_in', 'new_v_w_out', 'new_v_sgu_norm_w', 'new_v_sgu_norm_b', 'new_v_sgu_w', 'new_v_sgu_b', 'new_v_gla_wg_fwd', 'new_v_gla_bg_fwd', 'new_v_gla_wg_bwd', 'new_v_gla_bg_bwd', 'new_v_gla_norm_w', 'new_v_mla_q_norm_w', 'new_v_mla_w_uq', 'new_v_mla_kv_norm_w', 'new_v_mla_w_ukv', 'new_v_norm2_w', 'new_v_w_ff1', 'new_v_w_ff2', 'new_v_final_norm_w']
TWIN_LEAF_KINDS = {'loss': 'loss', 'grad_x': 'grad_x', 'grad_c_ctx': 'grad_w', 'grad_w_mod': 'grad_w', 'grad_b_mod': 'grad_w', 'grad_norm1_w': 'grad_w', 'grad_w_in': 'grad_w', 'grad_w_out': 'grad_w', 'grad_sgu_norm_w': 'grad_w', 'grad_sgu_norm_b': 'grad_w', 'grad_sgu_w': 'grad_w', 'grad_sgu_b': 'grad_w', 'grad_gla_wg_fwd': 'grad_w', 'grad_gla_bg_fwd': 'grad_w', 'grad_gla_wg_bwd': 'grad_w', 'grad_gla_bg_bwd': 'grad_w', 'grad_gla_norm_w': 'grad_w', 'grad_mla_q_norm_w': 'grad_w', 'grad_mla_w_uq': 'grad_w', 'grad_mla_kv_norm_w': 'grad_w', 'grad_mla_w_ukv': 'grad_w', 'grad_norm2_w': 'grad_w', 'grad_w_ff1': 'grad_w', 'grad_w_ff2': 'grad_w', 'grad_final_norm_w': 'grad_w', 'delta_c_ctx': 'delta_w', 'delta_w_mod': 'delta_w', 'delta_b_mod': 'delta_w', 'delta_norm1_w': 'delta_w', 'delta_w_in': 'delta_w', 'delta_w_out': 'delta_w', 'delta_sgu_norm_w': 'delta_w', 'delta_sgu_norm_b': 'delta_w', 'delta_sgu_w': 'delta_w', 'delta_sgu_b': 'delta_w', 'delta_gla_wg_fwd': 'delta_w', 'delta_gla_bg_fwd': 'delta_w', 'delta_gla_wg_bwd': 'delta_w', 'delta_gla_bg_bwd': 'delta_w', 'delta_gla_norm_w': 'delta_w', 'delta_mla_q_norm_w': 'delta_w', 'delta_mla_w_uq': 'delta_w', 'delta_mla_kv_norm_w': 'delta_w', 'delta_mla_w_ukv': 'delta_w', 'delta_norm2_w': 'delta_w', 'delta_w_ff1': 'delta_w', 'delta_w_ff2': 'delta_w', 'delta_final_norm_w': 'delta_w', 'new_m_c_ctx': 'new_m', 'new_m_w_mod': 'new_m', 'new_m_b_mod': 'new_m', 'new_m_norm1_w': 'new_m', 'new_m_w_in': 'new_m', 'new_m_w_out': 'new_m', 'new_m_sgu_norm_w': 'new_m', 'new_m_sgu_norm_b': 'new_m', 'new_m_sgu_w': 'new_m', 'new_m_sgu_b': 'new_m', 'new_m_gla_wg_fwd': 'new_m', 'new_m_gla_bg_fwd': 'new_m', 'new_m_gla_wg_bwd': 'new_m', 'new_m_gla_bg_bwd': 'new_m', 'new_m_gla_norm_w': 'new_m', 'new_m_mla_q_norm_w': 'new_m', 'new_m_mla_w_uq': 'new_m', 'new_m_mla_kv_norm_w': 'new_m', 'new_m_mla_w_ukv': 'new_m', 'new_m_norm2_w': 'new_m', 'new_m_w_ff1': 'new_m', 'new_m_w_ff2': 'new_m', 'new_m_final_norm_w': 'new_m', 'new_v_c_ctx': 'new_v', 'new_v_w_mod': 'new_v', 'new_v_b_mod': 'new_v', 'new_v_norm1_w': 'new_v', 'new_v_w_in': 'new_v', 'new_v_w_out': 'new_v', 'new_v_sgu_norm_w': 'new_v', 'new_v_sgu_norm_b': 'new_v', 'new_v_sgu_w': 'new_v', 'new_v_sgu_b': 'new_v', 'new_v_gla_wg_fwd': 'new_v', 'new_v_gla_bg_fwd': 'new_v', 'new_v_gla_wg_bwd': 'new_v', 'new_v_gla_bg_bwd': 'new_v', 'new_v_gla_norm_w': 'new_v', 'new_v_mla_q_norm_w': 'new_v', 'new_v_mla_w_uq': 'new_v', 'new_v_mla_kv_norm_w': 'new_v', 'new_v_mla_w_ukv': 'new_v', 'new_v_norm2_w': 'new_v', 'new_v_w_ff1': 'new_v', 'new_v_w_ff2': 'new_v', 'new_v_final_norm_w': 'new_v'}


def _forward(args):
    return _fwd_reference(*[args[k] for k in FWD_PARAMS])


def _output_shape():
    def fwd():
        inp = _fwd_setup_inputs(0)
        return _fwd_reference(*[inp[k] for k in FWD_PARAMS])
    out = _jax.eval_shape(fwd)
    return out.shape, out.dtype

N_MICROBATCH = 1
ADAM_LR = 0.001
ADAM_B1 = 0.9
ADAM_B2 = 0.999
ADAM_EPS = 1e-08
ADAM_WD = 0.01
ADAM_STEP = 10
PER_EXAMPLE_BATCH_AXIS = {'x': 0, 'c': 0, 'ctx': 0, 'loss_target': 0}
SHARED_INPUTS = []
_WEIGHT_DTYPES = {'c_ctx': _jnp.float32, 'w_mod': _jnp.float32, 'b_mod': _jnp.float32, 'norm1_w': _jnp.float32, 'w_in': _jnp.float32, 'w_out': _jnp.float32, 'sgu_norm_w': _jnp.float32, 'sgu_norm_b': _jnp.float32, 'sgu_w': _jnp.float32, 'sgu_b': _jnp.float32, 'gla_wg_fwd': _jnp.float32, 'gla_bg_fwd': _jnp.float32, 'gla_wg_bwd': _jnp.float32, 'gla_bg_bwd': _jnp.float32, 'gla_norm_w': _jnp.float32, 'mla_q_norm_w': _jnp.float32, 'mla_w_uq': _jnp.float32, 'mla_kv_norm_w': _jnp.float32, 'mla_w_ukv': _jnp.float32, 'norm2_w': _jnp.float32, 'w_ff1': _jnp.float32, 'w_ff2': _jnp.float32, 'final_norm_w': _jnp.float32}
MOMENT_SCALE = {'c_ctx': 1.386345e-02, 'w_mod': 1.177119e-01, 'b_mod': 2.192760e-01, 'norm1_w': 6.792487e-02, 'w_in': 5.384414e-02, 'w_out': 4.805524e-02, 'sgu_norm_w': 5.506524e-02, 'sgu_norm_b': 5.605755e-02, 'sgu_w': 3.698103e-02, 'sgu_b': 3.804730e-02, 'gla_wg_fwd': 9.884117e-03, 'gla_bg_fwd': 2.630056e-02, 'gla_wg_bwd': 8.629677e-03, 'gla_bg_bwd': 2.465834e-02, 'gla_norm_w': 9.696116e-02, 'mla_q_norm_w': 1.039463e-02, 'mla_w_uq': 5.852879e-03, 'mla_kv_norm_w': 3.456481e-02, 'mla_w_ukv': 1.767098e-02, 'norm2_w': 1.096913e-01, 'w_ff1': 5.645631e-02, 'w_ff2': 1.040280e-01, 'final_norm_w': 6.440045e+01}


def _to_microbatches(a, axis):
    t = _jnp.moveaxis(a, axis, 0)
    t = t.reshape((N_MICROBATCH, t.shape[0] // N_MICROBATCH) + t.shape[1:])
    return _jnp.moveaxis(t, 1, axis + 1)


def setup_inputs(seed: int = 0) -> dict:
    inp = _fwd_setup_inputs(seed)
    key = _jax.random.fold_in(_jax.random.key(seed), 7919)
    shape, _ = _output_shape()
    out = dict(inp)
    out["loss_target"] = _jax.random.normal(_jax.random.fold_in(key, 0), shape, _jnp.float32)
    for i, name in enumerate(TWIN_WEIGHTS):
        w = inp[name].astype(_jnp.float32)
        if MOMENT_SCALE is None:
            s = _jnp.sqrt(_jnp.mean(_jnp.square(w)) + 1e-30)
        else:
            s = MOMENT_SCALE[name]
        km, kv = _jax.random.split(_jax.random.fold_in(key, i + 1))
        out[name] = w
        out["m_" + name] = s * _jax.random.normal(km, w.shape, _jnp.float32)
        out["v_" + name] = (s * s) * _jax.random.uniform(kv, w.shape, _jnp.float32, 0.5, 1.5)
    if N_MICROBATCH > 1:
        for name, axis in PER_EXAMPLE_BATCH_AXIS.items():
            out[name] = _to_microbatches(out[name], axis)
    return {'x': out['x'], 'c': out['c'], 'ctx': out['ctx'], 'c_ctx': out['c_ctx'], 'w_mod': out['w_mod'], 'b_mod': out['b_mod'], 'norm1_w': out['norm1_w'], 'w_in': out['w_in'], 'w_out': out['w_out'], 'sgu_norm_w': out['sgu_norm_w'], 'sgu_norm_b': out['sgu_norm_b'], 'sgu_w': out['sgu_w'], 'sgu_b': out['sgu_b'], 'gla_wg_fwd': out['gla_wg_fwd'], 'gla_bg_fwd': out['gla_bg_fwd'], 'gla_wg_bwd': out['gla_wg_bwd'], 'gla_bg_bwd': out['gla_bg_bwd'], 'gla_norm_w': out['gla_norm_w'], 'mla_q_norm_w': out['mla_q_norm_w'], 'mla_w_uq': out['mla_w_uq'], 'mla_kv_norm_w': out['mla_kv_norm_w'], 'mla_w_ukv': out['mla_w_ukv'], 'norm2_w': out['norm2_w'], 'w_ff1': out['w_ff1'], 'w_ff2': out['w_ff2'], 'final_norm_w': out['final_norm_w'], 'loss_target': out['loss_target'], 'm_c_ctx': out['m_c_ctx'], 'm_w_mod': out['m_w_mod'], 'm_b_mod': out['m_b_mod'], 'm_norm1_w': out['m_norm1_w'], 'm_w_in': out['m_w_in'], 'm_w_out': out['m_w_out'], 'm_sgu_norm_w': out['m_sgu_norm_w'], 'm_sgu_norm_b': out['m_sgu_norm_b'], 'm_sgu_w': out['m_sgu_w'], 'm_sgu_b': out['m_sgu_b'], 'm_gla_wg_fwd': out['m_gla_wg_fwd'], 'm_gla_bg_fwd': out['m_gla_bg_fwd'], 'm_gla_wg_bwd': out['m_gla_wg_bwd'], 'm_gla_bg_bwd': out['m_gla_bg_bwd'], 'm_gla_norm_w': out['m_gla_norm_w'], 'm_mla_q_norm_w': out['m_mla_q_norm_w'], 'm_mla_w_uq': out['m_mla_w_uq'], 'm_mla_kv_norm_w': out['m_mla_kv_norm_w'], 'm_mla_w_ukv': out['m_mla_w_ukv'], 'm_norm2_w': out['m_norm2_w'], 'm_w_ff1': out['m_w_ff1'], 'm_w_ff2': out['m_w_ff2'], 'm_final_norm_w': out['m_final_norm_w'], 'v_c_ctx': out['v_c_ctx'], 'v_w_mod': out['v_w_mod'], 'v_b_mod': out['v_b_mod'], 'v_norm1_w': out['v_norm1_w'], 'v_w_in': out['v_w_in'], 'v_w_out': out['v_w_out'], 'v_sgu_norm_w': out['v_sgu_norm_w'], 'v_sgu_norm_b': out['v_sgu_norm_b'], 'v_sgu_w': out['v_sgu_w'], 'v_sgu_b': out['v_sgu_b'], 'v_gla_wg_fwd': out['v_gla_wg_fwd'], 'v_gla_bg_fwd': out['v_gla_bg_fwd'], 'v_gla_wg_bwd': out['v_gla_wg_bwd'], 'v_gla_bg_bwd': out['v_gla_bg_bwd'], 'v_gla_norm_w': out['v_gla_norm_w'], 'v_mla_q_norm_w': out['v_mla_q_norm_w'], 'v_mla_w_uq': out['v_mla_w_uq'], 'v_mla_kv_norm_w': out['v_mla_kv_norm_w'], 'v_mla_w_ukv': out['v_mla_w_ukv'], 'v_norm2_w': out['v_norm2_w'], 'v_w_ff1': out['v_w_ff1'], 'v_w_ff2': out['v_w_ff2'], 'v_final_norm_w': out['v_final_norm_w']}


def _loss(weights, diff, rest, loss_target):
    with _jax.named_scope("forward"):
        args = {**rest, TWIN_DIFF_INPUT: diff, **{k: w.astype(_WEIGHT_DTYPES[k]) for k, w in weights.items()}}
        y = _forward(args)
    with _jax.named_scope("loss_head"):
        err = _jnp.square(y.astype(_jnp.float32) - loss_target)
        return 0.5 * _jnp.sum(_jnp.mean(err, axis=-1)) if err.ndim else 0.5 * err


def _adamw(w, g, m, v):
    m = ADAM_B1 * m + (1.0 - ADAM_B1) * g
    v = ADAM_B2 * v + (1.0 - ADAM_B2) * _jnp.square(g)
    m_hat = m / (1.0 - ADAM_B1 ** ADAM_STEP)
    v_hat = v / (1.0 - ADAM_B2 ** ADAM_STEP)
    delta = -ADAM_LR * (m_hat / (_jnp.sqrt(v_hat) + ADAM_EPS) + ADAM_WD * w)
    return delta, m, v


def reference(x, c, ctx, c_ctx, w_mod, b_mod, norm1_w, w_in, w_out, sgu_norm_w, sgu_norm_b, sgu_w, sgu_b, gla_wg_fwd, gla_bg_fwd, gla_wg_bwd, gla_bg_bwd, gla_norm_w, mla_q_norm_w, mla_w_uq, mla_kv_norm_w, mla_w_ukv, norm2_w, w_ff1, w_ff2, final_norm_w, loss_target, m_c_ctx, m_w_mod, m_b_mod, m_norm1_w, m_w_in, m_w_out, m_sgu_norm_w, m_sgu_norm_b, m_sgu_w, m_sgu_b, m_gla_wg_fwd, m_gla_bg_fwd, m_gla_wg_bwd, m_gla_bg_bwd, m_gla_norm_w, m_mla_q_norm_w, m_mla_w_uq, m_mla_kv_norm_w, m_mla_w_ukv, m_norm2_w, m_w_ff1, m_w_ff2, m_final_norm_w, v_c_ctx, v_w_mod, v_b_mod, v_norm1_w, v_w_in, v_w_out, v_sgu_norm_w, v_sgu_norm_b, v_sgu_w, v_sgu_b, v_gla_wg_fwd, v_gla_bg_fwd, v_gla_wg_bwd, v_gla_bg_bwd, v_gla_norm_w, v_mla_q_norm_w, v_mla_w_uq, v_mla_kv_norm_w, v_mla_w_ukv, v_norm2_w, v_w_ff1, v_w_ff2, v_final_norm_w):
    given = dict(x=x, c=c, ctx=ctx, c_ctx=c_ctx, w_mod=w_mod, b_mod=b_mod, norm1_w=norm1_w, w_in=w_in, w_out=w_out, sgu_norm_w=sgu_norm_w, sgu_norm_b=sgu_norm_b, sgu_w=sgu_w, sgu_b=sgu_b, gla_wg_fwd=gla_wg_fwd, gla_bg_fwd=gla_bg_fwd, gla_wg_bwd=gla_wg_bwd, gla_bg_bwd=gla_bg_bwd, gla_norm_w=gla_norm_w, mla_q_norm_w=mla_q_norm_w, mla_w_uq=mla_w_uq, mla_kv_norm_w=mla_kv_norm_w, mla_w_ukv=mla_w_ukv, norm2_w=norm2_w, w_ff1=w_ff1, w_ff2=w_ff2, final_norm_w=final_norm_w, loss_target=loss_target, m_c_ctx=m_c_ctx, m_w_mod=m_w_mod, m_b_mod=m_b_mod, m_norm1_w=m_norm1_w, m_w_in=m_w_in, m_w_out=m_w_out, m_sgu_norm_w=m_sgu_norm_w, m_sgu_norm_b=m_sgu_norm_b, m_sgu_w=m_sgu_w, m_sgu_b=m_sgu_b, m_gla_wg_fwd=m_gla_wg_fwd, m_gla_bg_fwd=m_gla_bg_fwd, m_gla_wg_bwd=m_gla_wg_bwd, m_gla_bg_bwd=m_gla_bg_bwd, m_gla_norm_w=m_gla_norm_w, m_mla_q_norm_w=m_mla_q_norm_w, m_mla_w_uq=m_mla_w_uq, m_mla_kv_norm_w=m_mla_kv_norm_w, m_mla_w_ukv=m_mla_w_ukv, m_norm2_w=m_norm2_w, m_w_ff1=m_w_ff1, m_w_ff2=m_w_ff2, m_final_norm_w=m_final_norm_w, v_c_ctx=v_c_ctx, v_w_mod=v_w_mod, v_b_mod=v_b_mod, v_norm1_w=v_norm1_w, v_w_in=v_w_in, v_w_out=v_w_out, v_sgu_norm_w=v_sgu_norm_w, v_sgu_norm_b=v_sgu_norm_b, v_sgu_w=v_sgu_w, v_sgu_b=v_sgu_b, v_gla_wg_fwd=v_gla_wg_fwd, v_gla_bg_fwd=v_gla_bg_fwd, v_gla_wg_bwd=v_gla_wg_bwd, v_gla_bg_bwd=v_gla_bg_bwd, v_gla_norm_w=v_gla_norm_w, v_mla_q_norm_w=v_mla_q_norm_w, v_mla_w_uq=v_mla_w_uq, v_mla_kv_norm_w=v_mla_kv_norm_w, v_mla_w_ukv=v_mla_w_ukv, v_norm2_w=v_norm2_w, v_w_ff1=v_w_ff1, v_w_ff2=v_w_ff2, v_final_norm_w=v_final_norm_w)
    weights = {n: given[n] for n in TWIN_WEIGHTS}
    shared = {n: given[n] for n in SHARED_INPUTS}
    per_example = {n: given[n] for n in ['x', 'c', 'ctx']}
    grad_fn = _jax.value_and_grad(_loss, argnums=(0, 1))

    def one_microbatch(ex, loss_target):
        ex = dict(ex)
        diff = ex.pop(TWIN_DIFF_INPUT)
        return grad_fn(weights, diff, {**shared, **ex}, loss_target)

    if N_MICROBATCH == 1:
        loss, (grad_w, grad_x) = one_microbatch(per_example, given["loss_target"])
    else:
        def body(carry, xs):
            loss_sum, grad_sum = carry
            l_k, (gw_k, gx_k) = one_microbatch(xs[0], xs[1])
            with _jax.named_scope("update"):
                return (loss_sum + l_k, _jax.tree.map(_jnp.add, grad_sum, gw_k)), gx_k

        init = (_jnp.zeros((), _jnp.float32), _jax.tree.map(_jnp.zeros_like, weights))
        (loss, grad_w), grad_x = _jax.lax.scan(body, init, (per_example, given["loss_target"]))
    with _jax.named_scope("update"):
        delta_w, new_m, new_v = {}, {}, {}
        for n in TWIN_WEIGHTS:
            delta_w[n], new_m[n], new_v[n] = _adamw(weights[n], grad_w[n], given["m_" + n], given["v_" + n])
    return (loss, grad_x, *[grad_w[n] for n in TWIN_WEIGHTS], *[delta_w[n] for n in TWIN_WEIGHTS],
            *[new_m[n] for n in TWIN_WEIGHTS], *[new_v[n] for n in TWIN_WEIGHTS])
```

```python
import functools
import math

import numpy as np
import jax
import jax.numpy as jnp
from jax import lax
from jax.experimental import pallas as pl
from jax.experimental.pallas import tpu as pltpu

F32 = jnp.float32
BF16 = jnp.bfloat16
HI = lax.Precision.HIGHEST
EPS = 1e-6
VMEM_LIMIT_BYTES = 56 * 1024 * 1024
LANES = 128

D = 1024
D_FF = 4096
CTX = 256
GRID_W = 64
SGU_CHUNK = 128
GLA_CHUNK = 64
GLA_TAU = 16.0
GLA_DK = 32
MLA_SCALE = (128 + 64) ** -0.5
ROPE_BASE = 10000.0
TM = 256
NCTXB = CTX // TM
P_GV, P_CKV, P_SU, P_SV, P_GR, P_DQ, P_GK, P_GATE, P_KR, P_GQ = 0, 256, 512, 768, 1024, 1280, 1536, 1664, 1792, 1920
P_COLS = 2048
IN_GROUPS = ((0, 128, P_GK), (128, 256, P_GV), (384, 32, P_GATE), (416, 256, P_CKV), (672, 64, P_KR),
             (736, 256, P_SU), (992, 256, P_SV), (1248, 128, P_GQ), (1376, 256, P_GR), (1632, 256, P_DQ))
ADAM_LR, ADAM_B1, ADAM_B2, ADAM_EPS, ADAM_WD, ADAM_STEP = 0.001, 0.9, 0.999, 1e-08, 0.01, 10
MESH = pl.DeviceIdType.MESH


def _params(sem):
    return pltpu.CompilerParams(dimension_semantics=sem, vmem_limit_bytes=VMEM_LIMIT_BYTES)


def _pick(n, cands):
    for c in cands:
        if n % c == 0:
            return c
    return n


class Op:
    def __init__(self, arr, blk, idx, gshape, gidx, acc):
        self.arr, self.blk, self.idx, self.gshape, self.gidx, self.acc = arr, blk, idx, gshape, gidx, acc

    def spec(self):
        return pl.BlockSpec(self.blk, self.idx)


def rows(arr, width=None, cb=0, off=0, tm=TM):
    w = arr.shape[1] if width is None else width
    n = arr.shape[0] - off * tm
    return Op(arr, (tm, w), lambda i: (i + off, cb), (n, w), lambda i: (i, 0), False)


def chunks(arr, per_tile):
    z = (0,) * (arr.ndim - 1)
    return Op(arr, (per_tile,) + arr.shape[1:], lambda i: (i,) + z, arr.shape, lambda i: (i,) + z, False)


def const(arr):
    z = (0,) * arr.ndim
    return Op(arr, arr.shape, lambda i: z, arr.shape, lambda i: z, True)


def rw(name, fn, ins, outs, grid):
    nin = len(ins)

    def body(*refs):
        vals = [r[...] for r in refs[:nin]]
        res = fn(pl.program_id(0), *vals)
        for o, r in zip(refs[nin:], res):
            o[...] = r.astype(o.dtype)

    return pl.pallas_call(
        body, name=name, grid=(grid,),
        in_specs=[o.spec() for o in ins],
        out_specs=[pl.BlockSpec(b, ix) for (_, _, b, ix) in outs],
        out_shape=[jax.ShapeDtypeStruct(s, d) for (s, d, _, _) in outs],
        compiler_params=_params(("arbitrary",)),
    )(*[o.arr for o in ins])


def rowout(n, w, dtype, tm=TM):
    return ((n, w), dtype, (tm, w), lambda i: (i, 0))


def chunkout(shape, dtype, per_tile):
    z = (0,) * (len(shape) - 1)
    return (shape, dtype, (per_tile,) + tuple(shape[1:]), lambda i: (i,) + z)


def rw_vjp(name, fn, ins, cots, wrt, grid, gdt=None, adds=None):
    nin = len(ins)
    cot_ops = [c for c in cots if c is not None]
    add_items = sorted((adds or {}).items())
    gdt = gdt or [F32] * len(wrt)
    ncot, nadd = len(cot_ops), len(add_items)

    def body(*refs):
        i = pl.program_id(0)
        vals = [r[...] for r in refs[:nin]]
        cvals = [r[...] for r in refs[nin:nin + ncot]]
        avals = [r[...] for r in refs[nin + ncot:nin + ncot + nadd]]
        grefs = refs[nin + ncot + nadd:]

        def f(*d):
            a = list(vals)
            for k, dv in zip(wrt, d):
                a[k] = dv
            return tuple(fn(i, *a))

        outs, vf = jax.vjp(f, *[vals[k] for k in wrt])
        it = iter(cvals)
        ct = tuple(jnp.zeros_like(o) if c is None else next(it).astype(o.dtype) for c, o in zip(cots, outs))
        gs = list(vf(ct))
        for (pos, _), av in zip(add_items, avals):
            gs[pos] = gs[pos].astype(F32) + av.astype(F32)
        for pos, (k, g, gref) in enumerate(zip(wrt, gs, grefs)):
            if ins[k].acc:
                @pl.when(i == 0)
                def _():
                    gref[...] = jnp.zeros_like(gref)
                gref[...] += g.astype(gref.dtype)
            else:
                gref[...] = g.astype(gref.dtype)

    all_in = list(ins) + cot_ops + [op for _, op in add_items]
    return pl.pallas_call(
        body, name=name, grid=(grid,),
        in_specs=[o.spec() for o in all_in],
        out_specs=[pl.BlockSpec(ins[k].blk, ins[k].gidx) for k in wrt],
        out_shape=[jax.ShapeDtypeStruct(ins[k].gshape, dt) for k, dt in zip(wrt, gdt)],
        compiler_params=_params(("arbitrary",)),
    )(*[o.arr for o in all_in])


def mm(name, a, b, out_dtype):
    m, k = a.shape
    _, n = b.shape
    tm = _pick(m, (768, 512, 256))
    tn = _pick(n, (512, 256, 128))

    def body(a_ref, b_ref, o_ref):
        o_ref[...] = lax.dot_general(a_ref[...], b_ref[...], (((1,), (0,)), ((), ())),
                                     preferred_element_type=F32).astype(o_ref.dtype)

    return pl.pallas_call(
        body, name=name, grid=(m // tm, n // tn),
        in_specs=[pl.BlockSpec((tm, k), lambda i, j: (i, 0)), pl.BlockSpec((k, tn), lambda i, j: (0, j))],
        out_specs=pl.BlockSpec((tm, tn), lambda i, j: (i, j)),
        out_shape=jax.ShapeDtypeStruct((m, n), out_dtype),
        compiler_params=_params(("arbitrary", "arbitrary")),
    )(a, b)


def mm_tn(name, a, b):
    m, ka = a.shape
    _, nb = b.shape
    tm = _pick(m, (768, 512, 256))
    ta = _pick(ka, (512, 256, 128))
    tb = _pick(nb, (1024, 512, 256, 128))

    def body(a_ref, b_ref, o_ref):
        @pl.when(pl.program_id(2) == 0)
        def _():
            o_ref[...] = jnp.zeros_like(o_ref)
        o_ref[...] += lax.dot_general(a_ref[...], b_ref[...], (((0,), (0,)), ((), ())), preferred_element_type=F32)

    return pl.pallas_call(
        body, name=name, grid=(ka // ta, nb // tb, m // tm),
        in_specs=[pl.BlockSpec((tm, ta), lambda i, j, k: (k, i)), pl.BlockSpec((tm, tb), lambda i, j, k: (k, j))],
        out_specs=pl.BlockSpec((ta, tb), lambda i, j, k: (i, j)),
        out_shape=jax.ShapeDtypeStruct((ka, nb), F32),
        compiler_params=_params(("arbitrary", "arbitrary", "arbitrary")),
    )(a, b)


def _rms(x, w):
    return x * lax.rsqrt(jnp.mean(x * x, axis=-1, keepdims=True) + EPS) * w


def _mod_of(blk, m):
    return jnp.where(blk < NCTXB, m[0], m[1])


def _gelu(x):
    return x * (0.5 * (1.0 + jnp.tanh(math.sqrt(2.0 / math.pi) * (x + 0.044715 * (x * x * x)))))


def _sigmoid(x):
    return 1.0 / (1.0 + jnp.exp(-x))


def _log_sigmoid(z):
    return jnp.minimum(z, 0.0) - jnp.log(1.0 + jnp.exp(-jnp.abs(z)))


def _dot(a, b, dims=((1,), (0,)), precision=None):
    return lax.dot_general(a, b, (dims, ((), ())), precision=precision, preferred_element_type=F32)


def _lane_group_mask(width, group, h):
    lane = lax.broadcasted_iota(jnp.int32, (1, width), 1)
    return (lane >= h * group) & (lane < (h + 1) * group)


def fn_norm1(blk, x, m, nw):
    mv = _mod_of(blk, m)
    return ((_rms(x, nw) * (1.0 + mv[:, D:2 * D]) + mv[:, 0:D]),)


def fn_res_norm2(blk, x, yo, m, nw):
    mv = _mod_of(blk, m)
    x1 = x + mv[:, 2 * D:3 * D] * yo
    return x1, _rms(x1, nw) * (1.0 + mv[:, 4 * D:5 * D]) + mv[:, 3 * D:4 * D]


def fn_res2(blk, x1, f, m):
    mv = _mod_of(blk, m)
    return (x1 + mv[:, 5 * D:6 * D] * f,)


def fn_act(blk, u):
    a = jnp.maximum(u, 0.0)
    return (a * a,)


def fn_sgu(blk, su, sv, nw, nb, ws, bm):
    u = _gelu(su)
    g = _gelu(sv)
    mu = jnp.mean(g, axis=-1, keepdims=True)
    var = jnp.mean(jnp.square(g - mu), axis=-1, keepdims=True)
    v = (g - mu) * lax.rsqrt(var + EPS) * nw + nb
    out = []
    for c in range(su.shape[0] // SGU_CHUNK):
        vc = v[c * SGU_CHUNK:(c + 1) * SGU_CHUNK]
        s = bm
        for h in range(4):
            vh = jnp.where(_lane_group_mask(256, 64, h), vc, 0.0)
            s = s + _dot(ws[h].astype(BF16), vh.astype(BF16))
        out.append(u[c * SGU_CHUNK:(c + 1) * SGU_CHUNK] * s)
    return (jnp.concatenate(out, axis=0),)


def fn_gates(blk, pg, wg, bg):
    z = _dot(pg.astype(BF16), wg.astype(BF16)) + bg
    g = _log_sigmoid(z) * (1.0 / GLA_TAU)
    return g[:, :128], g[:, 128:]


def _gla_chunk_terms(g, rev):
    r = lax.broadcasted_iota(jnp.int32, (GLA_CHUNK, GLA_CHUNK), 0)
    c = lax.broadcasted_iota(jnp.int32, (GLA_CHUNK, GLA_CHUNK), 1)
    tri = jnp.where((c >= r) if rev else (c <= r), 1.0, 0.0).astype(F32)
    b = _dot(tri, g, precision=HI)
    return b, jnp.sum(g, axis=0, keepdims=True), ((c >= r) if rev else (c <= r))


def _bd_mask():
    r = lax.broadcasted_iota(jnp.int32, (128, 256), 0)
    c = lax.broadcasted_iota(jnp.int32, (128, 256), 1)
    return (r // GLA_DK) == (c // 64)


def _gla_kv_chunk(k, v, g, rev):
    b, tot, _ = _gla_chunk_terms(g, rev)
    kd = k * jnp.exp(tot - b)
    u = jnp.where(_bd_mask(), _dot(kd, v, ((0,), (0,)), precision=HI), 0.0)
    return u, jnp.exp(tot)


def _gla_o_chunk(q, k, v, g, s, rev):
    b, _, causal = _gla_chunk_terms(g, rev)
    qe = q * jnp.exp(b) * (GLA_DK ** -0.5)
    ke = k * jnp.exp(-b)
    o = _dot(qe, jnp.where(_bd_mask(), s, 0.0), precision=HI)
    for h in range(4):
        a = _dot(jnp.where(_lane_group_mask(128, GLA_DK, h), qe, 0.0), ke, ((1,), (1,)), precision=HI)
        a = jnp.where(causal, a, 0.0)
        o = o + _dot(a, jnp.where(_lane_group_mask(256, 64, h), v, 0.0), precision=HI)
    return o


def fn_gla_kv(blk, k, v, gf, gb):
    uf, ef, ub, eb = [], [], [], []
    for c in range(k.shape[0] // GLA_CHUNK):
        sl = slice(c * GLA_CHUNK, (c + 1) * GLA_CHUNK)
        u, e = _gla_kv_chunk(k[sl], v[sl], gf[sl], False)
        uf.append(u[None]); ef.append(e[None])
        u, e = _gla_kv_chunk(k[sl], v[sl], gb[sl], True)
        ub.append(u[None]); eb.append(e[None])
    cat = lambda t: jnp.concatenate(t, axis=0)
    return cat(uf), cat(ef), cat(ub), cat(eb)


def fn_gla_o(blk, q, k, v, gf, gb, gr, sf, sb, nwt):
    out = []
    for c in range(q.shape[0] // GLA_CHUNK):
        sl = slice(c * GLA_CHUNK, (c + 1) * GLA_CHUNK)
        out.append(_gla_o_chunk(q[sl], k[sl], v[sl], gf[sl], sf[c], False)
                   + _gla_o_chunk(q[sl], k[sl], v[sl], gb[sl], sb[c], True))
    o = jnp.concatenate(out, axis=0)
    r = lax.broadcasted_iota(jnp.int32, (256, 256), 0)
    c = lax.broadcasted_iota(jnp.int32, (256, 256), 1)
    head_mean = jnp.where((r // 64) == (c // 64), 1.0 / 64.0, 0.0).astype(F32)
    ms = _dot(o * o, head_mean, precision=HI)
    on = o * lax.rsqrt(ms + EPS) * nwt
    return (on * (gr * _sigmoid(gr)),)


def _rope_partner(x):
    lane = lax.broadcasted_iota(jnp.int32, x.shape, 1)
    return jnp.where((lane // 16) % 2 == 0, pltpu.roll(x, LANES - 16, 1), pltpu.roll(x, 16, 1))


@jax.custom_vjp
def _rope(x, cs, sn):
    return x * cs + _rope_partner(x) * sn


def _rope_fwd(x, cs, sn):
    return _rope(x, cs, sn), (cs, sn)


def _rope_bwd(res, dy):
    cs, sn = res
    return dy * cs + _rope_partner(dy * sn), jnp.zeros_like(cs), jnp.zeros_like(sn)


_rope.defvjp(_rope_fwd, _rope_bwd)


def fn_mla_pre(blk, ckv, dq, kvw, qw):
    return _rms(ckv, kvw), _rms(dq, qw)


def fn_mla_post(blk, kk, qu, kr, cs, sn):
    kro = _rope(kr, cs, sn)
    kcat, q = [], []
    for h in range(4):
        kcat += [kk[:, 128 * h:128 * (h + 1)].astype(F32), kro]
        q += [qu[:, 256 * h:256 * h + 128], _rope(qu[:, 256 * h + 128:256 * (h + 1)], cs, sn)]
    return jnp.concatenate(kcat, axis=1), jnp.concatenate(q, axis=1)


def fn_delta(blk, do, o):
    out = []
    for h in range(4):
        s = jnp.sum(do[:, 128 * h:128 * (h + 1)] * o[:, 128 * h:128 * (h + 1)], axis=-1, keepdims=True)
        out.append(jnp.broadcast_to(s, (do.shape[0], 128)))
    return (jnp.concatenate(out, axis=1),)


def _attn_scores(q, k, qi, ki, tq, tk):
    s = _dot(q, k, ((1,), (1,))) * MLA_SCALE
    row = qi * tq + lax.broadcasted_iota(jnp.int32, (tq, tk), 0)
    colk = ki * tk + lax.broadcasted_iota(jnp.int32, (tq, tk), 1)
    return jnp.where((row < CTX) & (colk >= CTX), -1e30, s)


def flash_fwd(q, kcat, kvu):
    t = q.shape[0]
    tq = tk = _pick(t, (768, 512, 256))

    def body(q_ref, k_ref, v_ref, o_ref, lse_ref, m_sc, l_sc, acc_sc):
        qi, ki = pl.program_id(1), pl.program_id(2)

        @pl.when(ki == 0)
        def _():
            m_sc[...] = jnp.full_like(m_sc, -1e30)
            l_sc[...] = jnp.zeros_like(l_sc)
            acc_sc[...] = jnp.zeros_like(acc_sc)

        s = _attn_scores(q_ref[...], k_ref[...], qi, ki, tq, tk)
        m_new = jnp.maximum(m_sc[...], jnp.max(s, axis=-1, keepdims=True))
        alpha = jnp.exp(m_sc[...] - m_new)
        p = jnp.exp(s - m_new)
        l_sc[...] = alpha * l_sc[...] + jnp.sum(p, axis=-1, keepdims=True)
        acc_sc[...] = alpha * acc_sc[...] + _dot(p.astype(BF16), v_ref[...])
        m_sc[...] = m_new

        @pl.when(ki == pl.num_programs(2) - 1)
        def _():
            o_ref[...] = acc_sc[...] / l_sc[...]
            lse_ref[...] = jnp.broadcast_to(m_sc[...] + jnp.log(l_sc[...]), lse_ref.shape)

    return pl.pallas_call(
        body, name="mla_flash_fwd", grid=(4, t // tq, t // tk),
        in_specs=[pl.BlockSpec((tq, 256), lambda h, i, j: (i, h)), pl.BlockSpec((tk, 256), lambda h, i, j: (j, h)),
                  pl.BlockSpec((tk, 128), lambda h, i, j: (j, 4 + h))],
        out_specs=[pl.BlockSpec((tq, 128), lambda h, i, j: (i, h)), pl.BlockSpec((tq, 128), lambda h, i, j: (i, h))],
        out_shape=[jax.ShapeDtypeStruct((t, 512), F32), jax.ShapeDtypeStruct((t, 512), F32)],
        scratch_shapes=[pltpu.VMEM((tq, 1), F32), pltpu.VMEM((tq, 1), F32), pltpu.VMEM((tq, 128), F32)],
        compiler_params=_params(("arbitrary", "arbitrary", "arbitrary")),
    )(q, kcat, kvu)


def _attn_bwd_terms(q, k, v, do, lse, delta, qi, ki, tq, tk):
    s = _attn_scores(q, k, qi, ki, tq, tk)
    p = jnp.exp(s - lse[:, 0:1])
    dp = _dot(do, v, ((1,), (1,)))
    ds = p * (dp - delta[:, 0:1]) * MLA_SCALE
    return p.astype(BF16), ds.astype(BF16)


def flash_bwd_kv(q, kcat, kvu, dy, lse, delta):
    t = q.shape[0]
    tq = tk = _pick(t, (768, 512, 256))

    def body(q_ref, k_ref, v_ref, do_ref, lse_ref, dl_ref, dk_ref, dv_ref):
        ki, qi = pl.program_id(1), pl.program_id(2)

        @pl.when(qi == 0)
        def _():
            dk_ref[...] = jnp.zeros_like(dk_ref)
            dv_ref[...] = jnp.zeros_like(dv_ref)

        do = do_ref[...].astype(BF16)
        p, ds = _attn_bwd_terms(q_ref[...], k_ref[...], v_ref[...], do, lse_ref[...], dl_ref[...], qi, ki, tq, tk)
        dv_ref[...] += _dot(p, do, ((0,), (0,)))
        dk_ref[...] += _dot(ds, q_ref[...], ((0,), (0,)))

    return pl.pallas_call(
        body, name="mla_flash_bwd_kv", grid=(4, t // tk, t // tq),
        in_specs=[pl.BlockSpec((tq, 256), lambda h, j, i: (i, h)), pl.BlockSpec((tk, 256), lambda h, j, i: (j, h)),
                  pl.BlockSpec((tk, 128), lambda h, j, i: (j, 4 + h)), pl.BlockSpec((tq, 128), lambda h, j, i: (i, 4 + h)),
                  pl.BlockSpec((tq, 128), lambda h, j, i: (i, h)), pl.BlockSpec((tq, 128), lambda h, j, i: (i, h))],
        out_specs=[pl.BlockSpec((tk, 256), lambda h, j, i: (j, h)), pl.BlockSpec((tk, 128), lambda h, j, i: (j, h))],
        out_shape=[jax.ShapeDtypeStruct((t, 1024), F32), jax.ShapeDtypeStruct((t, 512), F32)],
        compiler_params=_params(("arbitrary", "arbitrary", "arbitrary")),
    )(q, kcat, kvu, dy, lse, delta)


def flash_bwd_q(q, kcat, kvu, dy, lse, delta):
    t = q.shape[0]
    tq = tk = _pick(t, (768, 512, 256))

    def body(q_ref, k_ref, v_ref, do_ref, lse_ref, dl_ref, dq_ref):
        qi, ki = pl.program_id(1), pl.program_id(2)

        @pl.when(ki == 0)
        def _():
            dq_ref[...] = jnp.zeros_like(dq_ref)

        do = do_ref[...].astype(BF16)
        _, ds = _attn_bwd_terms(q_ref[...], k_ref[...], v_ref[...], do, lse_ref[...], dl_ref[...], qi, ki, tq, tk)
        dq_ref[...] += _dot(ds, k_ref[...])

    return pl.pallas_call(
        body, name="mla_flash_bwd_q", grid=(4, t // tq, t // tk),
        in_specs=[pl.BlockSpec((tq, 256), lambda h, i, j: (i, h)), pl.BlockSpec((tk, 256), lambda h, i, j: (j, h)),
                  pl.BlockSpec((tk, 128), lambda h, i, j: (j, 4 + h)), pl.BlockSpec((tq, 128), lambda h, i, j: (i, 4 + h)),
                  pl.BlockSpec((tq, 128), lambda h, i, j: (i, h)), pl.BlockSpec((tq, 128), lambda h, i, j: (i, h))],
        out_specs=pl.BlockSpec((tq, 256), lambda h, i, j: (i, h)),
        out_shape=jax.ShapeDtypeStruct((t, 1024), F32),
        compiler_params=_params(("arbitrary", "arbitrary", "arbitrary")),
    )(q, kcat, kvu, dy, lse, delta)


def _scan_chunk(t, nc, rev):
    nctx = CTX // GLA_CHUNK
    if not rev:
        return t
    return jnp.where(t < nctx, nctx - 1 - t, nc - 1 - (t - nctx))


def _diag(e):
    r = lax.broadcasted_iota(jnp.int32, (128, 128), 0)
    c = lax.broadcasted_iota(jnp.int32, (128, 128), 1)
    return jnp.where(r == c, jnp.broadcast_to(e, (128, 128)), 0.0)


def gla_states(uf, ef, ub, eb):
    nc = uf.shape[0]

    def body(uf_ref, ef_ref, ub_ref, eb_ref, sf_ref, sb_ref, sf_sc, sb_sc):
        @pl.when(pl.program_id(0) == 0)
        def _():
            sf_sc[...] = jnp.zeros_like(sf_sc)
            sb_sc[...] = jnp.zeros_like(sb_sc)

        for u_ref, e_ref, s_ref, sc in ((uf_ref, ef_ref, sf_ref, sf_sc), (ub_ref, eb_ref, sb_ref, sb_sc)):
            s_ref[0] = sc[...]
            sc[...] = _dot(_diag(e_ref[0]), sc[...], precision=HI) + u_ref[0]

    big = lambda rev: pl.BlockSpec((1, 128, 256), lambda t: (_scan_chunk(t, nc, rev), 0, 0))
    small = lambda rev: pl.BlockSpec((1, 1, 128), lambda t: (_scan_chunk(t, nc, rev), 0, 0))
    return pl.pallas_call(
        body, name="gla_states", grid=(nc,),
        in_specs=[big(False), small(False), big(True), small(True)],
        out_specs=[big(False), big(True)],
        out_shape=[jax.ShapeDtypeStruct((nc, 128, 256), F32)] * 2,
        scratch_shapes=[pltpu.VMEM((128, 256), F32)] * 2,
        compiler_params=_params(("arbitrary",)),
    )(uf, ef, ub, eb)


def gla_states_bwd(ef, eb, sf, sb, dsf, dsb):
    nc = ef.shape[0]

    def body(ef_ref, eb_ref, sf_ref, sb_ref, dsf_ref, dsb_ref, duf_ref, def_ref, dub_ref, deb_ref, gf_sc, gb_sc):
        @pl.when(pl.program_id(0) == 0)
        def _():
            gf_sc[...] = jnp.zeros_like(gf_sc)
            gb_sc[...] = jnp.zeros_like(gb_sc)

        for e_ref, s_ref, ds_ref, du_ref, de_ref, g in ((ef_ref, sf_ref, dsf_ref, duf_ref, def_ref, gf_sc),
                                                         (eb_ref, sb_ref, dsb_ref, dub_ref, deb_ref, gb_sc)):
            gs = g[...]
            du_ref[0] = gs
            r = lax.broadcasted_iota(jnp.int32, (128, 128), 0)
            c = lax.broadcasted_iota(jnp.int32, (128, 128), 1)
            m = _dot(s_ref[0], gs, ((1,), (1,)), precision=HI)
            de_ref[0] = jnp.sum(jnp.where(r == c, m, 0.0), axis=0, keepdims=True)
            g[...] = _dot(_diag(e_ref[0]), gs, precision=HI) + ds_ref[0]

    big = lambda rev: pl.BlockSpec((1, 128, 256), lambda t: (_scan_chunk(nc - 1 - t, nc, rev), 0, 0))
    small = lambda rev: pl.BlockSpec((1, 1, 128), lambda t: (_scan_chunk(nc - 1 - t, nc, rev), 0, 0))
    return pl.pallas_call(
        body, name="gla_states_bwd", grid=(nc,),
        in_specs=[small(False), small(True), big(False), big(True), big(False), big(True)],
        out_specs=[big(False), small(False), big(True), small(True)],
        out_shape=[jax.ShapeDtypeStruct((nc, 128, 256), F32), jax.ShapeDtypeStruct((nc, 1, 128), F32)] * 2,
        scratch_shapes=[pltpu.VMEM((128, 256), F32)] * 2,
        compiler_params=_params(("arbitrary",)),
    )(ef, eb, sf, sb, dsf, dsb)


def loss_head(xt, target, fnw):
    n = target.shape[0]

    def f(x, t, w):
        y = _rms(x, w)
        return 0.5 * jnp.sum(jnp.square(y - t)) * (1.0 / D)

    def body(x_ref, t_ref, w_ref, loss_ref, dx_ref, dw_ref):
        @pl.when(pl.program_id(0) == 0)
        def _():
            loss_ref[...] = jnp.zeros_like(loss_ref)
            dw_ref[...] = jnp.zeros_like(dw_ref)

        val, (dx, dw) = jax.value_and_grad(f, argnums=(0, 2))(x_ref[...], t_ref[...], w_ref[...])
        loss_ref[...] += jnp.broadcast_to(val, loss_ref.shape)
        dx_ref[...] = dx
        dw_ref[...] += dw

    return pl.pallas_call(
        body, name="loss_head", grid=(n // TM,),
        in_specs=[pl.BlockSpec((TM, D), lambda i: (i + NCTXB, 0)), pl.BlockSpec((TM, D), lambda i: (i, 0)),
                  pl.BlockSpec((1, D), lambda i: (0, 0))],
        out_specs=[pl.BlockSpec((1, 128), lambda i: (0, 0)), pl.BlockSpec((TM, D), lambda i: (i, 0)),
                   pl.BlockSpec((1, D), lambda i: (0, 0))],
        out_shape=[jax.ShapeDtypeStruct((1, 128), F32), jax.ShapeDtypeStruct((n, D), F32), jax.ShapeDtypeStruct((1, D), F32)],
        compiler_params=_params(("arbitrary",)),
    )(xt, target, fnw)


def _in_to_padded(w):
    out, pos = [], 0
    for src, wd, dst in sorted(IN_GROUPS, key=lambda g: g[2]):
        if dst > pos:
            out.append(jnp.zeros((w.shape[0], dst - pos), w.dtype))
        out.append(w[:, src:src + wd])
        pos = dst + wd
    if pos < P_COLS:
        out.append(jnp.zeros((w.shape[0], P_COLS - pos), w.dtype))
    return jnp.concatenate(out, axis=1)


def _in_from_padded(g):
    return jnp.concatenate([g[:, dst:dst + wd] for _, wd, dst in IN_GROUPS], axis=1)


def _uq_to_padded(w):
    return jnp.pad(w.reshape(256, 4, 192), ((0, 0), (0, 0), (0, 64))).reshape(256, 1024)


def _uq_from_padded(g):
    return g.reshape(256, 4, 256)[:, :, :192].reshape(256, 768)


def _ukv_to_padded(w):
    return w.reshape(256, 4, 2, 128).transpose(0, 2, 1, 3).reshape(256, 1024)


def _ukv_from_padded(g):
    return g.reshape(256, 2, 4, 128).transpose(0, 2, 1, 3).reshape(256, 1024)


def _rope_tables(n):
    pos = jnp.arange(n)
    freq = ROPE_BASE ** (-jnp.arange(16, dtype=F32) * 2.0 / 32.0)
    ar = (pos // GRID_W).astype(F32)[:, None] * freq[None, :]
    ac = (pos % GRID_W).astype(F32)[:, None] * freq[None, :]
    z = jnp.zeros((n, 64), F32)
    cs = jnp.concatenate([jnp.cos(ar), jnp.cos(ar), jnp.cos(ac), jnp.cos(ac), z], axis=1)
    sn = jnp.concatenate([-jnp.sin(ar), jnp.sin(ar), -jnp.sin(ac), jnp.sin(ac), z], axis=1)
    cs_c = jnp.concatenate([jnp.ones((CTX, 64), F32), jnp.zeros((CTX, 64), F32)], axis=1)
    return jnp.concatenate([cs_c, cs], axis=0), jnp.concatenate([jnp.zeros((CTX, 128), F32), sn], axis=0)


def _small_views(sp):
    wg = jnp.concatenate([jnp.pad(sp["gla_wg_fwd"], ((0, 112), (0, 0))), jnp.pad(sp["gla_wg_bwd"], ((16, 96), (0, 0)))], axis=1)
    return dict(
        n1w=sp["norm1_w"][None], n2w=sp["norm2_w"][None],
        sgu_nw=sp["sgu_norm_w"][None], sgu_nb=sp["sgu_norm_b"][None], sgu_w=sp["sgu_w"],
        sgu_bm=jnp.repeat(sp["sgu_b"].T, 64, axis=1),
        wg=wg, bg=jnp.concatenate([sp["gla_bg_fwd"], sp["gla_bg_bwd"]])[None],
        gla_nwt=jnp.tile(sp["gla_norm_w"], 4)[None],
        kvw=sp["mla_kv_norm_w"][None], qw=sp["mla_q_norm_w"][None])


def _small_grads(g):
    return dict(
        norm1_w=g["n1w"][0], norm2_w=g["n2w"][0],
        sgu_norm_w=g["sgu_nw"][0], sgu_norm_b=g["sgu_nb"][0], sgu_w=g["sgu_w"],
        sgu_b=g["sgu_bm"].reshape(128, 4, 64).sum(-1).T,
        gla_wg_fwd=g["wg"][0:16, 0:128], gla_wg_bwd=g["wg"][16:32, 128:256],
        gla_bg_fwd=g["bg"][0, 0:128], gla_bg_bwd=g["bg"][0, 128:256],
        gla_norm_w=g["gla_nwt"].reshape(4, 64).sum(0),
        mla_kv_norm_w=g["kvw"][0], mla_q_norm_w=g["qw"][0])


def _big_views(w_in, w_out, w_uq, w_ukv, w_ff1, w_ff2):
    win, wuq, wukv = _in_to_padded(w_in), _uq_to_padded(w_uq), _ukv_to_padded(w_ukv)
    return dict(win=win, win_t=win.T, wuq=wuq, wuq_t=wuq.T, wukv=wukv, wukv_t=wukv.T,
                wout=w_out, wout_t=w_out.T, w1=w_ff1, w1_t=w_ff1.T, w2=w_ff2, w2_t=w_ff2.T)


def _layer_ops(p, sv, a):
    pc = lambda off, w: rows(p, w, off // w)
    return dict(
        sgu=[pc(P_SU, 256), pc(P_SV, 256), const(sv["sgu_nw"]), const(sv["sgu_nb"]), const(sv["sgu_w"]), const(sv["sgu_bm"])],
        gates=[pc(P_GATE, 128), const(sv["wg"]), const(sv["bg"])],
        mla_pre=[pc(P_CKV, 256), pc(P_DQ, 256), const(sv["kvw"]), const(sv["qw"])],
        gla_kv=lambda: [pc(P_GK, 128), pc(P_GV, 256), rows(a["gf"]), rows(a["gb"])],
        gla_o=lambda: [pc(P_GQ, 128), pc(P_GK, 128), pc(P_GV, 256), rows(a["gf"]), rows(a["gb"]), pc(P_GR, 256),
                       chunks(a["sf"], TM // GLA_CHUNK), chunks(a["sb"], TM // GLA_CHUNK), const(sv["gla_nwt"])],
        mla_post=lambda: [rows(a["kvu"], 512, 0), rows(a["qu"]), pc(P_KR, 128), rows(a["cs"]), rows(a["sn"])])


def layer_fwd(l, xt, modl, bw, sv, tabs):
    t = xt.shape[0]
    g, nc, cpt = t // TM, t // GLA_CHUNK, TM // GLA_CHUNK
    nm = lambda s: f"l{l}_{s}"
    a = dict(x=xt, cs=tabs[0], sn=tabs[1])
    a["h"], = rw(nm("norm1"), fn_norm1, [rows(xt), const(modl), const(sv["n1w"])], [rowout(t, D, BF16)], g)
    p = a["p"] = mm(nm("in_proj"), a["h"], bw["win"], F32)
    ops = _layer_ops(p, sv, a)
    y_sgu, = rw(nm("sgu"), fn_sgu, ops["sgu"], [rowout(t, 256, BF16)], g)
    a["gf"], a["gb"] = rw(nm("gates"), fn_gates, ops["gates"], [rowout(t, 128, F32)] * 2, g)
    a["uf"], a["ef"], a["ub"], a["eb"] = rw(nm("gla_kv"), fn_gla_kv, ops["gla_kv"](),
                                           [chunkout((nc, 128, 256), F32, cpt), chunkout((nc, 1, 128), F32, cpt)] * 2, g)
    a["sf"], a["sb"] = gla_states(a["uf"], a["ef"], a["ub"], a["eb"])
    y_gla, = rw(nm("gla_o"), fn_gla_o, ops["gla_o"](), [rowout(t, 256, BF16)], g)
    a["ckvn"], a["dqn"] = rw(nm("mla_pre"), fn_mla_pre, ops["mla_pre"], [rowout(t, 256, BF16)] * 2, g)
    a["kvu"] = mm(nm("kv_up"), a["ckvn"], bw["wukv"], BF16)
    a["qu"] = mm(nm("q_up"), a["dqn"], bw["wuq"], F32)
    a["kcat"], a["q"] = rw(nm("mla_post"), fn_mla_post, ops["mla_post"](), [rowout(t, 1024, BF16)] * 2, g)
    a["o"], a["lse"] = flash_fwd(a["q"], a["kcat"], a["kvu"])
    a["y"] = jnp.concatenate([y_sgu, y_gla, a["o"].astype(BF16)], axis=1)
    a["yo"] = mm(nm("out_proj"), a["y"], bw["wout"], F32)
    a["x1"], a["h2"] = rw(nm("res_norm2"), fn_res_norm2, [rows(xt), rows(a["yo"]), const(modl), const(sv["n2w"])],
                          [rowout(t, D, F32), rowout(t, D, BF16)], g)
    a["u"] = mm(nm("ff1"), a["h2"], bw["w1"], F32)
    a["a2"], = rw(nm("act"), fn_act, [rows(a["u"])], [rowout(t, D_FF, BF16)], g)
    a["f"] = mm(nm("ff2"), a["a2"], bw["w2"], F32)
    x2, = rw(nm("res2"), fn_res2, [rows(a["x1"]), rows(a["f"]), const(modl)], [rowout(t, D, F32)], g)
    return x2, a


def fn_assemble(blk, gv1, gv2, ckv, su, sv_, gr, dq, gk1, gk2, pg, kr, gq):
    return (jnp.concatenate([gv1 + gv2, ckv, su, sv_, gr, dq, gk1 + gk2, pg, kr, gq], axis=1),)


def layer_bwd(l, dx2, a, modl, bw, sv):
    t = dx2.shape[0]
    g, nc, cpt = t // TM, t // GLA_CHUNK, TM // GLA_CHUNK
    nm = lambda s: f"l{l}_{s}_bwd"
    p = a["p"]
    ops = _layer_ops(p, sv, a)
    gw, gs = {}, {}
    dx1a, df, dm_a = rw_vjp(nm("res2"), fn_res2, [rows(a["x1"]), rows(a["f"]), const(modl)], [rows(dx2)], [0, 1, 2], g,
                            gdt=[F32, BF16, F32])
    gw["w2"] = mm_tn(nm("ff2_w"), a["a2"], df)
    da2 = mm(nm("ff2_x"), df, bw["w2_t"], F32)
    du, = rw_vjp(nm("act"), fn_act, [rows(a["u"])], [rows(da2)], [0], g, gdt=[BF16])
    gw["w1"] = mm_tn(nm("ff1_w"), a["h2"], du)
    dh2 = mm(nm("ff1_x"), du, bw["w1_t"], F32)
    dxa, dyo, dm_b, gs["n2w"] = rw_vjp(nm("res_norm2"), fn_res_norm2,
                                       [rows(a["x"]), rows(a["yo"]), const(modl), const(sv["n2w"])],
                                       [rows(dx1a), rows(dh2)], [0, 1, 2, 3], g, gdt=[F32, BF16, F32, F32])
    gw["wout"] = mm_tn(nm("out_w"), a["y"], dyo)
    dy = mm(nm("out_x"), dyo, bw["wout_t"], F32)
    dsu, dsv, gs["sgu_nw"], gs["sgu_nb"], gs["sgu_w"], gs["sgu_bm"] = rw_vjp(
        nm("sgu"), fn_sgu, ops["sgu"], [rows(dy, 256, 0)], [0, 1, 2, 3, 4, 5], g)
    dgq, dgk1, dgv1, dgf1, dgb1, dgr, dsf, dsb, gs["gla_nwt"] = rw_vjp(
        nm("gla_o"), fn_gla_o, ops["gla_o"](), [rows(dy, 256, 1)], list(range(9)), g)
    duf, def_, dub, deb = gla_states_bwd(a["ef"], a["eb"], a["sf"], a["sb"], dsf, dsb)
    dgk2, dgv2, dgf, dgb = rw_vjp(nm("gla_kv"), fn_gla_kv, ops["gla_kv"](),
                                  [chunks(duf, cpt), chunks(def_, cpt), chunks(dub, cpt), chunks(deb, cpt)], [0, 1, 2, 3], g,
                                  adds={2: rows(dgf1), 3: rows(dgb1)})
    dpg, gs["wg"], gs["bg"] = rw_vjp(nm("gates"), fn_gates, ops["gates"], [rows(dgf), rows(dgb)], [0, 1, 2], g)
    delta, = rw(nm("attn_delta"), fn_delta, [rows(dy, 512, 1), rows(a["o"])], [rowout(t, 512, F32)], g)
    dkcat, dv = flash_bwd_kv(a["q"], a["kcat"], a["kvu"], dy, a["lse"], delta)
    dq = flash_bwd_q(a["q"], a["kcat"], a["kvu"], dy, a["lse"], delta)
    dkk, dqu, dkr = rw_vjp(nm("mla_post"), fn_mla_post, ops["mla_post"](), [rows(dkcat), rows(dq)], [0, 1, 2], g,
                           gdt=[BF16, BF16, F32])
    dkvu = jnp.concatenate([dkk, dv.astype(BF16)], axis=1)
    gw["wukv"] = mm_tn(nm("kv_up_w"), a["ckvn"], dkvu)
    gw["wuq"] = mm_tn(nm("q_up_w"), a["dqn"], dqu)
    dckvn = mm(nm("kv_up_x"), dkvu, bw["wukv_t"], F32)
    ddqn = mm(nm("q_up_x"), dqu, bw["wuq_t"], F32)
    dckv, ddq, gs["kvw"], gs["qw"] = rw_vjp(nm("mla_pre"), fn_mla_pre, ops["mla_pre"], [rows(dckvn), rows(ddqn)],
                                            [0, 1, 2, 3], g)
    dp, = rw(nm("assemble"), fn_assemble,
             [rows(x_) for x_ in (dgv1, dgv2, dckv, dsu, dsv, dgr, ddq, dgk1, dgk2, dpg, dkr, dgq)],
             [rowout(t, P_COLS, BF16)], g)
    gw["win"] = mm_tn(nm("in_w"), a["h"], dp)
    dh = mm(nm("in_x"), dp, bw["win_t"], F32)
    dx, dm_c, gs["n1w"] = rw_vjp(nm("norm1"), fn_norm1, [rows(a["x"]), const(modl), const(sv["n1w"])], [rows(dh)],
                                 [0, 1, 2], g, adds={0: rows(dxa)})
    big = dict(w_in=_in_from_padded(gw["win"]), w_out=gw["wout"], mla_w_uq=_uq_from_padded(gw["wuq"]),
               mla_w_ukv=_ukv_from_padded(gw["wukv"]), w_ff1=gw["w1"], w_ff2=gw["w2"])
    return dx, dm_a + dm_b + dm_c, big, _small_grads(gs)


SMALL_NAMES = ("norm1_w", "sgu_norm_w", "sgu_norm_b", "sgu_w", "sgu_b", "gla_wg_fwd", "gla_bg_fwd", "gla_wg_bwd",
               "gla_bg_bwd", "gla_norm_w", "mla_q_norm_w", "mla_kv_norm_w", "norm2_w")
BIG_NAMES = ("w_in", "w_out", "mla_w_uq", "mla_w_ukv", "w_ff1", "w_ff2")


def local_step(x, ctx, target, mods, big, small, final_norm_w):
    n = x.shape[0]
    xt = jnp.concatenate([ctx, x], axis=0)
    tabs = _rope_tables(n)
    depth = len(mods)
    bws = [_big_views(*[big[l][k] for k in BIG_NAMES]) for l in range(depth)]
    svs = [_small_views(small[l]) for l in range(depth)]
    acts = []
    for l in range(depth):
        xt, a = layer_fwd(l, xt, mods[l], bws[l], svs[l], tabs)
        acts.append(a)
    loss, dxl, dfnw = loss_head(xt, target, final_norm_w[None])
    dxt = jnp.concatenate([jnp.zeros((CTX, D), F32), dxl], axis=0)
    dmods, gbig, gsmall = [None] * depth, [None] * depth, [None] * depth
    for l in reversed(range(depth)):
        dxt, dmods[l], gbig[l], gsmall[l] = layer_bwd(l, dxt, acts[l], mods[l], bws[l], svs[l])
    return loss, dxt[CTX:], dmods, gbig, gsmall, dfnw


def _group(group):
    x, y, c = lax.axis_index("x"), lax.axis_index("y"), lax.axis_index("c")
    if group == "sib":
        return 2, c, [((x, y, 1 - c), 1 - c)]
    if group == "chip":
        flips = [(1, 0), (0, 1), (1, 1)]
        return 4, 2 * x + y, [((x ^ fx, y ^ fy, c), 2 * (x ^ fx) + (y ^ fy)) for fx, fy in flips]
    flips = [(fx, fy, fc) for fx in (0, 1) for fy in (0, 1) for fc in (0, 1)][1:]
    return 8, 4 * x + 2 * y + c, [((x ^ fx, y ^ fy, c ^ fc), 4 * (x ^ fx) + 2 * (y ^ fy) + (c ^ fc)) for fx, fy, fc in flips]


def _group_size(group):
    return {"sib": 2, "chip": 4, "all": 8}[group]


def xchg(name, entries):
    n_in = sum(len(arrs) for _, _, arrs in entries)
    n_remote = sum(_group_size(g) - 1 if k != "swap" else 1 for k, g, _ in entries)
    n_local = sum(1 for k, _, _ in entries if k != "swap")
    out_shape = []
    for kind, group, arrs in entries:
        a = arrs[0]
        if kind == "gather":
            out_shape.append(jax.ShapeDtypeStruct((_group_size(group),) + a.shape, a.dtype))
        else:
            out_shape.append(jax.ShapeDtypeStruct(a.shape, a.dtype))

    def body(*refs):
        in_refs, out_refs = refs[:n_in], refs[n_in:n_in + len(entries)]
        send_sems, recv_sems, local_sems = refs[n_in + len(entries):]
        c = lax.axis_index("c")
        pos, k, kl = 0, 0, 0
        waits = []
        for (kind, group, arrs), out in zip(entries, out_refs):
            srcs = in_refs[pos:pos + len(arrs)]
            pos += len(arrs)
            _, mine, peers = _group(group)
            if kind == "swap":
                (dev, _), = peers
                for core, src in ((0, srcs[1]), (1, srcs[0])):
                    @pl.when(c == core)
                    def _(src=src, k=k):
                        pltpu.make_async_remote_copy(src_ref=src, dst_ref=out, send_sem=send_sems.at[k],
                                                     recv_sem=recv_sems.at[k], device_id=dev, device_id_type=MESH).start()
                waits.append(pltpu.make_async_remote_copy(src_ref=srcs[0], dst_ref=out, send_sem=send_sems.at[k],
                                                          recv_sem=recv_sems.at[k], device_id=dev, device_id_type=MESH))
                k += 1
                continue
            src = srcs[0]
            own = pltpu.make_async_copy(src if kind == "gather" else src.at[mine], out.at[mine], local_sems.at[kl])
            own.start()
            kl += 1
            for dev, slot in peers:
                piece = src if kind == "gather" else src.at[slot]
                pltpu.make_async_remote_copy(src_ref=piece, dst_ref=out.at[mine], send_sem=send_sems.at[k],
                                             recv_sem=recv_sems.at[k], device_id=dev, device_id_type=MESH).start()
                waits.append(pltpu.make_async_remote_copy(src_ref=piece, dst_ref=out.at[slot], send_sem=send_sems.at[k],
                                                          recv_sem=recv_sems.at[k], device_id=dev, device_id_type=MESH))
                k += 1
            waits.append(own)
        for w in waits:
            w.wait()

    any_spec = pl.BlockSpec(memory_space=pl.ANY)
    return pl.pallas_call(
        body, name=name,
        in_specs=[any_spec] * n_in, out_specs=[any_spec] * len(entries), out_shape=out_shape,
        scratch_shapes=[pltpu.SemaphoreType.DMA((n_remote,)), pltpu.SemaphoreType.DMA((n_remote,)),
                        pltpu.SemaphoreType.DMA((max(n_local, 1),))],
    )(*[a for _, _, arrs in entries for a in arrs])


def _block_rows(r, c, budget=131072):
    tr = 8
    while tr * 2 * c <= budget and r % (tr * 2) == 0:
        tr *= 2
    return tr if r % tr == 0 else r


def tree_sum(name, parts):
    g, r, c = parts.shape
    tr = _block_rows(r, c)

    def body(p_ref, o_ref):
        p = [p_ref[i] for i in range(g)]
        while len(p) > 1:
            p = [p[i] + p[i + 1] for i in range(0, len(p), 2)]
        o_ref[...] = p[0]

    return pl.pallas_call(
        body, name=name, grid=(r // tr,),
        in_specs=[pl.BlockSpec((g, tr, c), lambda i: (0, i, 0))], out_specs=pl.BlockSpec((tr, c), lambda i: (i, 0)),
        out_shape=jax.ShapeDtypeStruct((r, c), parts.dtype), compiler_params=_params(("arbitrary",)),
    )(parts)


def pair_sum(name, g0, g1, recv, core):
    r, c = recv.shape
    tr = _block_rows(r, c)

    def body(a_ref, b_ref, r_ref, k_ref, o_ref):
        o_ref[...] = jnp.where(k_ref[...] > 0.5, b_ref[...], a_ref[...]) + r_ref[...]

    blk = pl.BlockSpec((tr, c), lambda i: (i, 0))
    return pl.pallas_call(
        body, name=name, grid=(r // tr,), in_specs=[blk, blk, blk, pl.BlockSpec((1, 1), lambda i: (0, 0))],
        out_specs=blk, out_shape=jax.ShapeDtypeStruct((r, c), F32), compiler_params=_params(("arbitrary",)),
    )(g0, g1, recv, core)


def adamw(name, w, g, m, v):
    r, c = w.shape
    tr = _block_rows(r, c)

    def body(w_ref, g_ref, m_ref, v_ref, d_ref, nm_ref, nv_ref):
        gg = g_ref[...]
        nm = ADAM_B1 * m_ref[...] + (1.0 - ADAM_B1) * gg
        nv = ADAM_B2 * v_ref[...] + (1.0 - ADAM_B2) * jnp.square(gg)
        m_hat = nm / (1.0 - ADAM_B1 ** ADAM_STEP)
        v_hat = nv / (1.0 - ADAM_B2 ** ADAM_STEP)
        d_ref[...] = -ADAM_LR * (m_hat / (jnp.sqrt(v_hat) + ADAM_EPS) + ADAM_WD * w_ref[...])
        nm_ref[...] = nm
        nv_ref[...] = nv

    blk = pl.BlockSpec((tr, c), lambda i: (i, 0))
    return pl.pallas_call(
        body, name=name, grid=(r // tr,), in_specs=[blk] * 4, out_specs=[blk] * 3,
        out_shape=[jax.ShapeDtypeStruct((r, c), F32)] * 3, compiler_params=_params(("arbitrary",)),
    )(w, g, m, v)


W_MOD_COLS = 6 * D // 4
MOD_TN = 512


def mod_project(c16, w_mod, b_loc):
    def body(c_ref, w_ref, b_ref, o_ref):
        cv = c_ref[...]
        s = (cv * _sigmoid(cv)).astype(BF16)
        o_ref[0] = _dot(s, w_ref[0].astype(BF16)) + b_ref[0]

    return pl.pallas_call(
        body, name="mod_project", grid=(2, W_MOD_COLS // MOD_TN),
        in_specs=[pl.BlockSpec((16, D), lambda l, j: (0, 0)), pl.BlockSpec((1, D, MOD_TN), lambda l, j: (l, 0, j)),
                  pl.BlockSpec((1, 1, MOD_TN), lambda l, j: (l, 0, j))],
        out_specs=pl.BlockSpec((1, 16, MOD_TN), lambda l, j: (l, 0, j)),
        out_shape=jax.ShapeDtypeStruct((2, 16, W_MOD_COLS), F32), compiler_params=_params(("arbitrary", "arbitrary")),
    )(c16, w_mod, b_loc)


def mod_weight_grad(c16, dm16):
    def body(c_ref, d_ref, o_ref):
        cv = c_ref[...]
        o_ref[0] = _dot(cv * _sigmoid(cv), d_ref[0], ((0,), (0,)), precision=HI)

    return pl.pallas_call(
        body, name="mod_weight_grad", grid=(2, W_MOD_COLS // MOD_TN),
        in_specs=[pl.BlockSpec((16, D), lambda l, j: (0, 0)), pl.BlockSpec((1, 16, MOD_TN), lambda l, j: (l, 0, j))],
        out_specs=pl.BlockSpec((1, D, MOD_TN), lambda l, j: (l, 0, j)),
        out_shape=jax.ShapeDtypeStruct((2, D, W_MOD_COLS), F32), compiler_params=_params(("arbitrary", "arbitrary")),
    )(c16, dm16)


def cctx_partial(dmc, w_mod):
    def body(d_ref, w_ref, o_ref):
        @pl.when(pl.program_id(0) == 0)
        def _():
            o_ref[...] = jnp.zeros_like(o_ref)
        o_ref[...] += _dot(d_ref[0], w_ref[0], ((1,), (1,)), precision=HI)

    return pl.pallas_call(
        body, name="cctx_partial", grid=(2,),
        in_specs=[pl.BlockSpec((1, 8, W_MOD_COLS), lambda l: (l, 0, 0)), pl.BlockSpec((1, D, W_MOD_COLS), lambda l: (l, 0, 0))],
        out_specs=pl.BlockSpec((8, D), lambda l: (0, 0)),
        out_shape=jax.ShapeDtypeStruct((8, D), F32), compiler_params=_params(("arbitrary",)),
    )(dmc, w_mod)


def cctx_grad(parts, c_ctx8):
    def body(p_ref, c_ref, o_ref):
        ds = (p_ref[0] + p_ref[1]) + (p_ref[2] + p_ref[3])
        _, vf = jax.vjp(lambda z: z * _sigmoid(z), c_ref[...])
        o_ref[...] = vf(ds)[0]

    return pl.pallas_call(
        body, name="cctx_grad", out_shape=jax.ShapeDtypeStruct((8, D), F32),
    )(parts, c_ctx8)


ARG_NAMES = ("x", "c", "ctx", "c_ctx", "w_mod", "b_mod", "norm1_w", "w_in", "w_out", "sgu_norm_w", "sgu_norm_b", "sgu_w",
             "sgu_b", "gla_wg_fwd", "gla_bg_fwd", "gla_wg_bwd", "gla_bg_bwd", "gla_norm_w", "mla_q_norm_w", "mla_w_uq",
             "mla_kv_norm_w", "mla_w_ukv", "norm2_w", "w_ff1", "w_ff2", "final_norm_w")
WEIGHT_NAMES = ARG_NAMES[3:]
PACKED = ("c_ctx", "b_mod") + SMALL_NAMES + ("final_norm_w",)
ROW_SHARDED = ("w_out", "w_ff2")
PACK_ROWS = 256


def _pack(vectors):
    flat = jnp.concatenate([v.reshape(-1) for v in vectors])
    n = flat.shape[0]
    total = -(-n // (PACK_ROWS * LANES)) * PACK_ROWS * LANES
    return jnp.pad(flat, (0, total - n)).reshape(-1, LANES)


def _unpack(buf, shapes):
    flat, out, pos = buf.reshape(-1), [], 0
    for shp in shapes:
        n = int(np.prod(shp))
        out.append(flat[pos:pos + n].reshape(shp))
        pos += n
    return out


def _full_weight(name, gathered, l):
    g = gathered[:, l]
    if name in ROW_SHARDED:
        return g.reshape(-1, g.shape[-1])
    return g.transpose(1, 0, 2).reshape(g.shape[1], -1)


def _chip_chunks(name, a):
    if name in ROW_SHARDED:
        return a.reshape(4, a.shape[0] // 4, a.shape[1])
    return a.reshape(a.shape[0], 4, a.shape[1] // 4).transpose(1, 0, 2)


def kernel(x, c, ctx, c_ctx, w_mod, b_mod, norm1_w, w_in, w_out, sgu_norm_w, sgu_norm_b, sgu_w, sgu_b, gla_wg_fwd, gla_bg_fwd, gla_wg_bwd, gla_bg_bwd, gla_norm_w, mla_q_norm_w, mla_w_uq, mla_kv_norm_w, mla_w_ukv, norm2_w, w_ff1, w_ff2, final_norm_w, loss_target, m_c_ctx, m_w_mod, m_b_mod, m_norm1_w, m_w_in, m_w_out, m_sgu_norm_w, m_sgu_norm_b, m_sgu_w, m_sgu_b, m_gla_wg_fwd, m_gla_bg_fwd, m_gla_wg_bwd, m_gla_bg_bwd, m_gla_norm_w, m_mla_q_norm_w, m_mla_w_uq, m_mla_kv_norm_w, m_mla_w_ukv, m_norm2_w, m_w_ff1, m_w_ff2, m_final_norm_w, v_c_ctx, v_w_mod, v_b_mod, v_norm1_w, v_w_in, v_w_out, v_sgu_norm_w, v_sgu_norm_b, v_sgu_w, v_sgu_b, v_gla_wg_fwd, v_gla_bg_fwd, v_gla_wg_bwd, v_gla_bg_bwd, v_gla_norm_w, v_mla_q_norm_w, v_mla_w_uq, v_mla_kv_norm_w, v_mla_w_ukv, v_norm2_w, v_w_ff1, v_w_ff2, v_final_norm_w):
    args = (x, c, ctx, c_ctx, w_mod, b_mod, norm1_w, w_in, w_out, sgu_norm_w, sgu_norm_b, sgu_w, sgu_b, gla_wg_fwd, gla_bg_fwd, gla_wg_bwd, gla_bg_bwd, gla_norm_w, mla_q_norm_w, mla_w_uq, mla_kv_norm_w, mla_w_ukv, norm2_w, w_ff1, w_ff2, final_norm_w)
    w = dict(zip(ARG_NAMES, args))
    moms = (m_c_ctx, m_w_mod, m_b_mod, m_norm1_w, m_w_in, m_w_out, m_sgu_norm_w, m_sgu_norm_b, m_sgu_w, m_sgu_b, m_gla_wg_fwd, m_gla_bg_fwd, m_gla_wg_bwd, m_gla_bg_bwd, m_gla_norm_w, m_mla_q_norm_w, m_mla_w_uq, m_mla_kv_norm_w, m_mla_w_ukv, m_norm2_w, m_w_ff1, m_w_ff2, m_final_norm_w)
    vars_ = (v_c_ctx, v_w_mod, v_b_mod, v_norm1_w, v_w_in, v_w_out, v_sgu_norm_w, v_sgu_norm_b, v_sgu_w, v_sgu_b, v_gla_wg_fwd, v_gla_bg_fwd, v_gla_wg_bwd, v_gla_bg_bwd, v_gla_norm_w, v_mla_q_norm_w, v_mla_w_uq, v_mla_kv_norm_w, v_mla_w_ukv, v_norm2_w, v_w_ff1, v_w_ff2, v_final_norm_w)
    m1 = dict(zip(WEIGHT_NAMES, moms))
    m2 = dict(zip(WEIGHT_NAMES, vars_))
    xi, yi, ci = lax.axis_index("x"), lax.axis_index("y"), lax.axis_index("c")
    chip, dev = 2 * xi + yi, 4 * xi + 2 * yi + ci
    depth = w_mod.shape[0]

    got = xchg("gather_inputs", [("gather", "all", [c])] + [("gather", "chip", [w[k].astype(BF16)]) for k in BIG_NAMES])
    c_all, shards = got[0], dict(zip(BIG_NAMES, got[1:]))
    c16 = jnp.concatenate([c_all.reshape(8, D), c_ctx[None], jnp.zeros((7, D), F32)], axis=0)
    b_loc = lax.dynamic_slice_in_dim(b_mod, chip * W_MOD_COLS, W_MOD_COLS, axis=1)[:, None, :]
    mod_part = mod_project(c16, w_mod, b_loc)
    mod_all, = xchg("gather_mod", [("gather", "chip", [mod_part])])
    mod_full = mod_all.transpose(1, 2, 0, 3).reshape(depth, 16, 6 * D)
    mods = [jnp.stack([mod_full[l, 8], lax.dynamic_index_in_dim(mod_full[l], dev, 0, keepdims=False)])[:, None, :]
            for l in range(depth)]

    big = [{k: _full_weight(k, shards[k], l) for k in BIG_NAMES} for l in range(depth)]
    small = [{k: w[k][l] for k in SMALL_NAMES} for l in range(depth)]
    loss, grad_x, dmods, gbig, gsmall, dfnw = local_step(x[0], ctx[0], loss_target[0], mods, big, small, final_norm_w)
    loss = lax.psum(loss[0, 0], ("x", "y", "c"))

    dm_lat = jnp.stack([dmods[l][1, 0] for l in range(depth)])
    dm_ctx = jnp.stack([dmods[l][0, 0] for l in range(depth)])
    small_pack = _pack([dm_lat, dm_ctx] + [jnp.stack([gsmall[l][k] for l in range(depth)]) for k in SMALL_NAMES] + [dfnw])
    got = xchg("exchange_grads", [("gather", "all", [small_pack])] + [("swap", "sib", [gbig[0][k], gbig[1][k]]) for k in BIG_NAMES])
    small_all, from_sib = got[0], dict(zip(BIG_NAMES, got[1:]))
    small_sum = tree_sum("small_grad_sum", small_all)
    core = ci.astype(F32).reshape(1, 1)
    mine = {k: pair_sum(f"pair_sum_{k}", gbig[0][k], gbig[1][k], from_sib[k], core) for k in BIG_NAMES}

    n_dm = depth * 6 * D
    dm_rows = n_dm // LANES
    dm_lat_all = small_all[:, :dm_rows].reshape(8, depth, 6 * D)
    dm_ctx_sum = small_sum[dm_rows:2 * dm_rows].reshape(depth, 6 * D)
    take = lambda a: lax.dynamic_slice_in_dim(a, chip * W_MOD_COLS, W_MOD_COLS, axis=-1)
    dmc_loc = take(dm_ctx_sum)
    cc_part = cctx_partial(jnp.pad(dmc_loc[:, None, :], ((0, 0), (0, 7), (0, 0))), w_mod)
    got = xchg("scatter_grads", [("gather", "chip", [cc_part])] + [("scatter", "chip", [_chip_chunks(k, mine[k])]) for k in BIG_NAMES])
    cc_parts, chunks_in = got[0], dict(zip(BIG_NAMES, got[1:]))
    reduced = {k: tree_sum(f"chip_sum_{k}", chunks_in[k]) for k in BIG_NAMES}
    g_c_ctx = cctx_grad(cc_parts, jnp.broadcast_to(c_ctx[None], (8, D)))[0]

    got = xchg("share_layers", [("gather", "sib", [reduced[k]]) for k in BIG_NAMES])
    grads = dict(zip(BIG_NAMES, got))

    dm16 = jnp.concatenate([take(dm_lat_all).transpose(1, 0, 2), dmc_loc[:, None, :], jnp.zeros((depth, 7, W_MOD_COLS), F32)], axis=1)
    grads["w_mod"] = mod_weight_grad(c16, dm16)
    flat_sum = small_sum.reshape(-1)
    g_b_mod = (flat_sum[:n_dm] + flat_sum[n_dm:2 * n_dm]).reshape(depth, 6 * D)
    rest_shapes = [w[k].shape for k in PACKED[2:]]
    n_rest = sum(int(np.prod(s)) for s in rest_shapes)
    for k, g in zip(PACKED, [g_c_ctx, g_b_mod] + _unpack(flat_sum[2 * n_dm:2 * n_dm + n_rest], rest_shapes)):
        grads[k] = g

    delta, new_m, new_v = {}, {}, {}
    for k in BIG_NAMES + ("w_mod",):
        view = lambda a: a.reshape(-1, a.shape[-1])
        d_, m_, v_ = adamw(f"adamw_{k}", view(w[k]), view(grads[k]), view(m1[k]), view(m2[k]))
        delta[k], new_m[k], new_v[k] = d_.reshape(w[k].shape), m_.reshape(w[k].shape), v_.reshape(w[k].shape)
    shapes = [w[k].shape for k in PACKED]
    d_, m_, v_ = adamw("adamw_small", _pack([w[k] for k in PACKED]), _pack([grads[k] for k in PACKED]),
                       _pack([m1[k] for k in PACKED]), _pack([m2[k] for k in PACKED]))
    for k, dk, mk, vk in zip(PACKED, _unpack(d_, shapes), _unpack(m_, shapes), _unpack(v_, shapes)):
        delta[k], new_m[k], new_v[k] = dk, mk, vk
    return (loss, grad_x[None], *[grads[k] for k in WEIGHT_NAMES], *[delta[k] for k in WEIGHT_NAMES],
            *[new_m[k] for k in WEIGHT_NAMES], *[new_v[k] for k in WEIGHT_NAMES])
```

```python
import functools
import math

import numpy as np
import jax
import jax.numpy as jnp
from jax import lax
from jax.experimental import pallas as pl
from jax.experimental.pallas import tpu as pltpu

F32 = jnp.float32
BF16 = jnp.bfloat16
HI = lax.Precision.HIGHEST
EPS = 1e-6
VMEM_LIMIT_BYTES = 56 * 1024 * 1024
LANES = 128

D = 1024
D_FF = 4096
CTX = 256
GRID_W = 64
SGU_CHUNK = 128
GLA_CHUNK = 64
GLA_TAU = 16.0
GLA_DK = 32
MLA_SCALE = (128 + 64) ** -0.5
ROPE_BASE = 10000.0
TM = 256
NCTXB = CTX // TM
P_GV, P_CKV, P_SU, P_SV, P_GR, P_DQ, P_GK, P_GATE, P_KR, P_GQ = 0, 256, 512, 768, 1024, 1280, 1536, 1664, 1792, 1920
P_COLS = 2048
IN_GROUPS = ((0, 128, P_GK), (128, 256, P_GV), (384, 32, P_GATE), (416, 256, P_CKV), (672, 64, P_KR),
             (736, 256, P_SU), (992, 256, P_SV), (1248, 128, P_GQ), (1376, 256, P_GR), (1632, 256, P_DQ))
ADAM_LR, ADAM_B1, ADAM_B2, ADAM_EPS, ADAM_WD, ADAM_STEP = 0.001, 0.9, 0.999, 1e-08, 0.01, 10
MESH = pl.DeviceIdType.MESH


def _params(sem):
    return pltpu.CompilerParams(dimension_semantics=sem, vmem_limit_bytes=VMEM_LIMIT_BYTES)


def _pick(n, cands):
    for c in cands:
        if n % c == 0:
            return c
    return n


class Op:
    def __init__(self, arr, blk, idx, gshape, gidx, acc):
        self.arr, self.blk, self.idx, self.gshape, self.gidx, self.acc = arr, blk, idx, gshape, gidx, acc

    def spec(self):
        return pl.BlockSpec(self.blk, self.idx)


def rows(arr, width=None, cb=0, off=0, tm=TM):
    w = arr.shape[1] if width is None else width
    n = arr.shape[0] - off * tm
    return Op(arr, (tm, w), lambda i: (i + off, cb), (n, w), lambda i: (i, 0), False)


def chunks(arr, per_tile):
    z = (0,) * (arr.ndim - 1)
    return Op(arr, (per_tile,) + arr.shape[1:], lambda i: (i,) + z, arr.shape, lambda i: (i,) + z, False)


def const(arr):
    z = (0,) * arr.ndim
    return Op(arr, arr.shape, lambda i: z, arr.shape, lambda i: z, True)


def rw(name, fn, ins, outs, grid):
    nin = len(ins)

    def body(*refs):
        vals = [r[...] for r in refs[:nin]]
        res = fn(pl.program_id(0), *vals)
        for o, r in zip(refs[nin:], res):
            o[...] = r.astype(o.dtype)

    return pl.pallas_call(
        body, name=name, grid=(grid,),
        in_specs=[o.spec() for o in ins],
        out_specs=[pl.BlockSpec(b, ix) for (_, _, b, ix) in outs],
        out_shape=[jax.ShapeDtypeStruct(s, d) for (s, d, _, _) in outs],
        compiler_params=_params(("arbitrary",)),
    )(*[o.arr for o in ins])


def rowout(n, w, dtype, tm=TM):
    return ((n, w), dtype, (tm, w), lambda i: (i, 0))


def chunkout(shape, dtype, per_tile):
    z = (0,) * (len(shape) - 1)
    return (shape, dtype, (per_tile,) + tuple(shape[1:]), lambda i: (i,) + z)


def rw_vjp(name, fn, ins, cots, wrt, grid, gdt=None, adds=None):
    nin = len(ins)
    cot_ops = [c for c in cots if c is not None]
    add_items = sorted((adds or {}).items())
    gdt = gdt or [F32] * len(wrt)
    ncot, nadd = len(cot_ops), len(add_items)

    def body(*refs):
        i = pl.program_id(0)
        vals = [r[...] for r in refs[:nin]]
        cvals = [r[...] for r in refs[nin:nin + ncot]]
        avals = [r[...] for r in refs[nin + ncot:nin + ncot + nadd]]
        grefs = refs[nin + ncot + nadd:]

        def f(*d):
            a = list(vals)
            for k, dv in zip(wrt, d):
                a[k] = dv
            return tuple(fn(i, *a))

        outs, vf = jax.vjp(f, *[vals[k] for k in wrt])
        it = iter(cvals)
        ct = tuple(jnp.zeros_like(o) if c is None else next(it).astype(o.dtype) for c, o in zip(cots, outs))
        gs = list(vf(ct))
        for (pos, _), av in zip(add_items, avals):
            gs[pos] = gs[pos].astype(F32) + av.astype(F32)
        for pos, (k, g, gref) in enumerate(zip(wrt, gs, grefs)):
            if ins[k].acc:
                @pl.when(i == 0)
                def _():
                    gref[...] = jnp.zeros_like(gref)
                gref[...] += g.astype(gref.dtype)
            else:
                gref[...] = g.astype(gref.dtype)

    all_in = list(ins) + cot_ops + [op for _, op in add_items]
    return pl.pallas_call(
        body, name=name, grid=(grid,),
        in_specs=[o.spec() for o in all_in],
        out_specs=[pl.BlockSpec(ins[k].blk, ins[k].gidx) for k in wrt],
        out_shape=[jax.ShapeDtypeStruct(ins[k].gshape, dt) for k, dt in zip(wrt, gdt)],
        compiler_params=_params(("arbitrary",)),
    )(*[o.arr for o in all_in])


def mm(name, a, b, out_dtype):
    m, k = a.shape
    _, n = b.shape
    tm = _pick(m, (768, 512, 256))
    tn = _pick(n, (512, 256, 128))

    def body(a_ref, b_ref, o_ref):
        o_ref[...] = lax.dot_general(a_ref[...], b_ref[...], (((1,), (0,)), ((), ())),
                                     preferred_element_type=F32).astype(o_ref.dtype)

    return pl.pallas_call(
        body, name=name, grid=(m // tm, n // tn),
        in_specs=[pl.BlockSpec((tm, k), lambda i, j: (i, 0)), pl.BlockSpec((k, tn), lambda i, j: (0, j))],
        out_specs=pl.BlockSpec((tm, tn), lambda i, j: (i, j)),
        out_shape=jax.ShapeDtypeStruct((m, n), out_dtype),
        compiler_params=_params(("arbitrary", "arbitrary")),
    )(a, b)


def mm_tn(name, a, b):
    m, ka = a.shape
    _, nb = b.shape
    tm = _pick(m, (768, 512, 256))
    ta = _pick(ka, (512, 256, 128))
    tb = _pick(nb, (1024, 512, 256, 128))

    def body(a_ref, b_ref, o_ref):
        @pl.when(pl.program_id(2) == 0)
        def _():
            o_ref[...] = jnp.zeros_like(o_ref)
        o_ref[...] += lax.dot_general(a_ref[...], b_ref[...], (((0,), (0,)), ((), ())), preferred_element_type=F32)

    return pl.pallas_call(
        body, name=name, grid=(ka // ta, nb // tb, m // tm),
        in_specs=[pl.BlockSpec((tm, ta), lambda i, j, k: (k, i)), pl.BlockSpec((tm, tb), lambda i, j, k: (k, j))],
        out_specs=pl.BlockSpec((ta, tb), lambda i, j, k: (i, j)),
        out_shape=jax.ShapeDtypeStruct((ka, nb), F32),
        compiler_params=_params(("arbitrary", "arbitrary", "arbitrary")),
    )(a, b)


def _rms(x, w):
    return x * lax.rsqrt(jnp.mean(x * x, axis=-1, keepdims=True) + EPS) * w


def _mod_of(blk, m):
    return jnp.where(blk < NCTXB, m[0], m[1])


def _gelu(x):
    return x * (0.5 * (1.0 + jnp.tanh(math.sqrt(2.0 / math.pi) * (x + 0.044715 * (x * x * x)))))


def _sigmoid(x):
    return 1.0 / (1.0 + jnp.exp(-x))


def _log_sigmoid(z):
    return jnp.minimum(z, 0.0) - jnp.log(1.0 + jnp.exp(-jnp.abs(z)))


def _dot(a, b, dims=((1,), (0,)), precision=None):
    return lax.dot_general(a, b, (dims, ((), ())), precision=precision, preferred_element_type=F32)


def _lane_group_mask(width, group, h):
    lane = lax.broadcasted_iota(jnp.int32, (1, width), 1)
    return (lane >= h * group) & (lane < (h + 1) * group)


def fn_norm1(blk, x, m, nw):
    mv = _mod_of(blk, m)
    return ((_rms(x, nw) * (1.0 + mv[:, D:2 * D]) + mv[:, 0:D]),)


def fn_res_norm2(blk, x, yo, m, nw):
    mv = _mod_of(blk, m)
    x1 = x + mv[:, 2 * D:3 * D] * yo
    return x1, _rms(x1, nw) * (1.0 + mv[:, 4 * D:5 * D]) + mv[:, 3 * D:4 * D]


def fn_res2(blk, x1, f, m):
    mv = _mod_of(blk, m)
    return (x1 + mv[:, 5 * D:6 * D] * f,)


def fn_act(blk, u):
    a = jnp.maximum(u, 0.0)
    return (a * a,)


def fn_sgu(blk, su, sv, nw, nb, ws, bm):
    u = _gelu(su)
    g = _gelu(sv)
    mu = jnp.mean(g, axis=-1, keepdims=True)
    var = jnp.mean(jnp.square(g - mu), axis=-1, keepdims=True)
    v = (g - mu) * lax.rsqrt(var + EPS) * nw + nb
    out = []
    for c in range(su.shape[0] // SGU_CHUNK):
        vc = v[c * SGU_CHUNK:(c + 1) * SGU_CHUNK]
        s = bm
        for h in range(4):
            vh = jnp.where(_lane_group_mask(256, 64, h), vc, 0.0)
            s = s + _dot(ws[h].astype(BF16), vh.astype(BF16))
        out.append(u[c * SGU_CHUNK:(c + 1) * SGU_CHUNK] * s)
    return (jnp.concatenate(out, axis=0),)


def fn_gates(blk, pg, wg, bg):
    z = _dot(pg.astype(BF16), wg.astype(BF16)) + bg
    g = _log_sigmoid(z) * (1.0 / GLA_TAU)
    return g[:, :128], g[:, 128:]


def _gla_chunk_terms(g, rev):
    r = lax.broadcasted_iota(jnp.int32, (GLA_CHUNK, GLA_CHUNK), 0)
    c = lax.broadcasted_iota(jnp.int32, (GLA_CHUNK, GLA_CHUNK), 1)
    tri = jnp.where((c >= r) if rev else (c <= r), 1.0, 0.0).astype(F32)
    b = _dot(tri, g, precision=HI)
    return b, jnp.sum(g, axis=0, keepdims=True)


def _bd_mask():
    r = lax.broadcasted_iota(jnp.int32, (128, 256), 0)
    c = lax.broadcasted_iota(jnp.int32, (128, 256), 1)
    return (r // GLA_DK) == (c // 64)


def _gla_kv_chunk(k, v, g, rev):
    b, tot = _gla_chunk_terms(g, rev)
    kd = k * jnp.exp(tot - b)
    u = jnp.where(_bd_mask(), _dot(kd.astype(BF16), v.astype(BF16), ((0,), (0,))), 0.0)
    return u, jnp.exp(tot)


def _gla_o_chunk(q, k, v, g, s, rev):
    b, _ = _gla_chunk_terms(g, rev)
    qe = q * jnp.exp(b) * (GLA_DK ** -0.5)
    ke = k * jnp.exp(-b)
    o = _dot(qe.astype(BF16), jnp.where(_bd_mask(), s, 0.0).astype(BF16))
    qs = jnp.concatenate([jnp.where(_lane_group_mask(128, GLA_DK, h), qe, 0.0) for h in range(4)], axis=0)
    a = _dot(qs.astype(BF16), ke.astype(BF16), ((1,), (1,)))
    i = lax.broadcasted_iota(jnp.int32, a.shape, 0) % GLA_CHUNK
    j = lax.broadcasted_iota(jnp.int32, a.shape, 1)
    a = jnp.where((j >= i) if rev else (j <= i), a, 0.0)
    av = _dot(a.astype(BF16), v.astype(BF16))
    for h in range(4):
        o = o + jnp.where(_lane_group_mask(256, 64, h), av[GLA_CHUNK * h:GLA_CHUNK * (h + 1)], 0.0)
    return o


def fn_gla_kv(blk, k, v, gf, gb):
    uf, ef, ub, eb = [], [], [], []
    for c in range(k.shape[0] // GLA_CHUNK):
        sl = slice(c * GLA_CHUNK, (c + 1) * GLA_CHUNK)
        u, e = _gla_kv_chunk(k[sl], v[sl], gf[sl], False)
        uf.append(u[None]); ef.append(e[None])
        u, e = _gla_kv_chunk(k[sl], v[sl], gb[sl], True)
        ub.append(u[None]); eb.append(e[None])
    cat = lambda t: jnp.concatenate(t, axis=0)
    return cat(uf), cat(ef), cat(ub), cat(eb)


def fn_gla_o(blk, q, k, v, gf, gb, gr, sf, sb, nwt):
    out = []
    for c in range(q.shape[0] // GLA_CHUNK):
        sl = slice(c * GLA_CHUNK, (c + 1) * GLA_CHUNK)
        out.append(_gla_o_chunk(q[sl], k[sl], v[sl], gf[sl], sf[c], False)
                   + _gla_o_chunk(q[sl], k[sl], v[sl], gb[sl], sb[c], True))
    o = jnp.concatenate(out, axis=0)
    r = lax.broadcasted_iota(jnp.int32, (256, 256), 0)
    c = lax.broadcasted_iota(jnp.int32, (256, 256), 1)
    head_mean = jnp.where((r // 64) == (c // 64), 1.0 / 64.0, 0.0).astype(F32)
    ms = _dot(o * o, head_mean, precision=HI)
    on = o * lax.rsqrt(ms + EPS) * nwt
    return (on * (gr * _sigmoid(gr)),)


def _rope_partner(x):
    lane = lax.broadcasted_iota(jnp.int32, x.shape, 1)
    return jnp.where((lane // 16) % 2 == 0, pltpu.roll(x, LANES - 16, 1), pltpu.roll(x, 16, 1))


@jax.custom_vjp
def _rope(x, cs, sn):
    return x * cs + _rope_partner(x) * sn


def _rope_fwd(x, cs, sn):
    return _rope(x, cs, sn), (cs, sn)


def _rope_bwd(res, dy):
    cs, sn = res
    return dy * cs + _rope_partner(dy * sn), jnp.zeros_like(cs), jnp.zeros_like(sn)


_rope.defvjp(_rope_fwd, _rope_bwd)


def fn_mla_pre(blk, ckv, dq, kvw, qw):
    return _rms(ckv, kvw), _rms(dq, qw)


def fn_mla_post(blk, kk, qu, kr, cs, sn):
    kro = _rope(kr, cs, sn)
    kcat, q = [], []
    for h in range(4):
        kcat += [kk[:, 128 * h:128 * (h + 1)].astype(F32), kro]
        q += [qu[:, 256 * h:256 * h + 128], _rope(qu[:, 256 * h + 128:256 * (h + 1)], cs, sn)]
    return jnp.concatenate(kcat, axis=1), jnp.concatenate(q, axis=1)


def fn_delta(blk, do, o):
    out = []
    for h in range(4):
        s = jnp.sum(do[:, 128 * h:128 * (h + 1)] * o[:, 128 * h:128 * (h + 1)], axis=-1, keepdims=True)
        out.append(jnp.broadcast_to(s, (do.shape[0], 128)))
    return (jnp.concatenate(out, axis=1),)


SCORE_SCALE = MLA_SCALE * math.log2(math.e)
ATTN_ROWS = 256
NEG = -1e30


def _scores(q, k, k0, masked):
    s = _dot(q, k, ((1,), (1,))) * SCORE_SCALE
    if masked:
        col = k0 + lax.broadcasted_iota(jnp.int32, s.shape, 1)
        s = jnp.where(col >= CTX, NEG, s)
    return s


def flash_fwd(q, kcat, kvu):
    t = q.shape[0]
    tq = _pick(t, (768, 512, 256))
    tk = _pick(t, (2816, 1536, 768, 512, 256))
    nsub = tq // ATTN_ROWS

    def body(q_ref, k_ref, v_ref, o_ref, lse_ref, m_sc, l_sc, acc_sc):
        qi, ki = pl.program_id(1), pl.program_id(2)

        @pl.when(ki == 0)
        def _():
            m_sc[...] = jnp.full_like(m_sc, NEG)
            l_sc[...] = jnp.zeros_like(l_sc)
            acc_sc[...] = jnp.zeros_like(acc_sc)

        def step(first_tile):
            k, v = k_ref[...], v_ref[...]
            for r in range(nsub):
                rs = pl.ds(r * ATTN_ROWS, ATTN_ROWS)
                s = _scores(q_ref[rs, :], k, ki * tk, first_tile and r == 0)
                m_old = m_sc[rs, :]
                m_new = jnp.maximum(m_old, jnp.max(s, axis=-1, keepdims=True))
                alpha = jnp.exp2(m_old - m_new)
                p = jnp.exp2(s - m_new)
                l_sc[rs, :] = alpha * l_sc[rs, :] + jnp.sum(p, axis=-1, keepdims=True)
                acc_sc[rs, :] = alpha * acc_sc[rs, :] + _dot(p.astype(BF16), v)
                m_sc[rs, :] = m_new

        pl.when(qi == 0)(lambda: step(True))
        pl.when(qi != 0)(lambda: step(False))

        @pl.when(ki == pl.num_programs(2) - 1)
        def _():
            o_ref[...] = acc_sc[...] / l_sc[...]
            lse_ref[...] = jnp.broadcast_to(m_sc[...] + jnp.log2(l_sc[...]), lse_ref.shape)

    return pl.pallas_call(
        body, name="mla_flash_fwd", grid=(4, t // tq, t // tk),
        in_specs=[pl.BlockSpec((tq, 256), lambda h, i, j: (i, h)), pl.BlockSpec((tk, 256), lambda h, i, j: (j, h)),
                  pl.BlockSpec((tk, 128), lambda h, i, j: (j, 4 + h))],
        out_specs=[pl.BlockSpec((tq, 128), lambda h, i, j: (i, h)), pl.BlockSpec((tq, 128), lambda h, i, j: (i, h))],
        out_shape=[jax.ShapeDtypeStruct((t, 512), F32), jax.ShapeDtypeStruct((t, 512), F32)],
        scratch_shapes=[pltpu.VMEM((tq, 1), F32), pltpu.VMEM((tq, 1), F32), pltpu.VMEM((tq, 128), F32)],
        compiler_params=_params(("arbitrary", "arbitrary", "arbitrary")),
    )(q, kcat, kvu)


def flash_bwd(q, kcat, kvu, dy, lse, delta):
    t = q.shape[0]
    tq = tk = _pick(t, (768, 512, 256))
    nsub = tq // ATTN_ROWS

    def body(q_ref, k_ref, v_ref, do_ref, lse_ref, dl_ref, dq_ref, dk_ref, dv_ref):
        kj, qi = pl.program_id(1), pl.program_id(2)

        @pl.when(qi == 0)
        def _():
            dk_ref[...] = jnp.zeros_like(dk_ref)
            dv_ref[...] = jnp.zeros_like(dv_ref)

        def step(first_tile):
            k, v = k_ref[...], v_ref[...]
            for r in range(nsub):
                rs = pl.ds(r * ATTN_ROWS, ATTN_ROWS)
                qr, do = q_ref[rs, :], do_ref[rs, :].astype(BF16)
                s = _scores(qr, k, kj * tk, first_tile and r == 0)
                p = jnp.exp2(s - lse_ref[rs, 0:1])
                dp = _dot(do, v, ((1,), (1,)))
                ds = (p * (dp - dl_ref[rs, 0:1]) * MLA_SCALE).astype(BF16)
                dv_ref[...] += _dot(p.astype(BF16), do, ((0,), (0,)))
                dk_ref[...] += _dot(ds, qr, ((0,), (0,)))
                rows_ = pl.ds(pl.multiple_of(qi * tq + r * ATTN_ROWS, ATTN_ROWS), ATTN_ROWS)
                dqr = _dot(ds, k)

                @pl.when(kj == 0)
                def _():
                    dq_ref[rows_, :] = dqr

                @pl.when(kj != 0)
                def _():
                    dq_ref[rows_, :] += dqr

        pl.when(qi == 0)(lambda: step(True))
        pl.when(qi != 0)(lambda: step(False))

    return pl.pallas_call(
        body, name="mla_flash_bwd", grid=(4, t // tk, t // tq),
        in_specs=[pl.BlockSpec((tq, 256), lambda h, j, i: (i, h)), pl.BlockSpec((tk, 256), lambda h, j, i: (j, h)),
                  pl.BlockSpec((tk, 128), lambda h, j, i: (j, 4 + h)), pl.BlockSpec((tq, 128), lambda h, j, i: (i, 4 + h)),
                  pl.BlockSpec((tq, 128), lambda h, j, i: (i, h)), pl.BlockSpec((tq, 128), lambda h, j, i: (i, h))],
        out_specs=[pl.BlockSpec((t, 256), lambda h, j, i: (0, h)), pl.BlockSpec((tk, 256), lambda h, j, i: (j, h)),
                   pl.BlockSpec((tk, 128), lambda h, j, i: (j, h))],
        out_shape=[jax.ShapeDtypeStruct((t, 1024), F32), jax.ShapeDtypeStruct((t, 1024), F32),
                   jax.ShapeDtypeStruct((t, 512), F32)],
        compiler_params=_params(("arbitrary", "arbitrary", "arbitrary")),
    )(q, kcat, kvu, dy, lse, delta)


SCAN_BLOCK = CTX // GLA_CHUNK


def _scan_block(t, nb, rev):
    if not rev:
        return t
    return jnp.where(t < 1, 0, nb - t)


def _scan_order(rev):
    return tuple(reversed(range(SCAN_BLOCK))) if rev else tuple(range(SCAN_BLOCK))


def _diag(e):
    r = lax.broadcasted_iota(jnp.int32, (128, 128), 0)
    c = lax.broadcasted_iota(jnp.int32, (128, 128), 1)
    return jnp.where(r == c, jnp.broadcast_to(e, (128, 128)), 0.0)


def gla_states(uf, ef, ub, eb):
    nb = uf.shape[0] // SCAN_BLOCK

    def body(uf_ref, ef_ref, ub_ref, eb_ref, sf_ref, sb_ref, sf_sc, sb_sc):
        @pl.when(pl.program_id(0) == 0)
        def _():
            sf_sc[...] = jnp.zeros_like(sf_sc)
            sb_sc[...] = jnp.zeros_like(sb_sc)

        for u_ref, e_ref, s_ref, sc, rev in ((uf_ref, ef_ref, sf_ref, sf_sc, False), (ub_ref, eb_ref, sb_ref, sb_sc, True)):
            s = sc[...]
            for c in _scan_order(rev):
                s_ref[c] = s
                s = _dot(_diag(e_ref[c]), s, precision=HI) + u_ref[c]
            sc[...] = s

    big = lambda rev: pl.BlockSpec((SCAN_BLOCK, 128, 256), lambda t: (_scan_block(t, nb, rev), 0, 0))
    small = lambda rev: pl.BlockSpec((SCAN_BLOCK, 1, 128), lambda t: (_scan_block(t, nb, rev), 0, 0))
    return pl.pallas_call(
        body, name="gla_states", grid=(nb,),
        in_specs=[big(False), small(False), big(True), small(True)],
        out_specs=[big(False), big(True)],
        out_shape=[jax.ShapeDtypeStruct(uf.shape, F32)] * 2,
        scratch_shapes=[pltpu.VMEM((128, 256), F32)] * 2,
        compiler_params=_params(("arbitrary",)),
    )(uf, ef, ub, eb)


def gla_states_bwd(ef, eb, sf, sb, dsf, dsb):
    nb = ef.shape[0] // SCAN_BLOCK

    def body(ef_ref, eb_ref, sf_ref, sb_ref, dsf_ref, dsb_ref, duf_ref, def_ref, dub_ref, deb_ref, gf_sc, gb_sc):
        @pl.when(pl.program_id(0) == 0)
        def _():
            gf_sc[...] = jnp.zeros_like(gf_sc)
            gb_sc[...] = jnp.zeros_like(gb_sc)

        r = lax.broadcasted_iota(jnp.int32, (128, 128), 0)
        c = lax.broadcasted_iota(jnp.int32, (128, 128), 1)
        for e_ref, s_ref, ds_ref, du_ref, de_ref, g_sc, rev in ((ef_ref, sf_ref, dsf_ref, duf_ref, def_ref, gf_sc, False),
                                                                 (eb_ref, sb_ref, dsb_ref, dub_ref, deb_ref, gb_sc, True)):
            g = g_sc[...]
            for k in reversed(_scan_order(rev)):
                du_ref[k] = g
                m = _dot(s_ref[k], g, ((1,), (1,)), precision=HI)
                de_ref[k] = jnp.sum(jnp.where(r == c, m, 0.0), axis=0, keepdims=True)
                g = _dot(_diag(e_ref[k]), g, precision=HI) + ds_ref[k]
            g_sc[...] = g

    big = lambda rev: pl.BlockSpec((SCAN_BLOCK, 128, 256), lambda t: (_scan_block(nb - 1 - t, nb, rev), 0, 0))
    small = lambda rev: pl.BlockSpec((SCAN_BLOCK, 1, 128), lambda t: (_scan_block(nb - 1 - t, nb, rev), 0, 0))
    return pl.pallas_call(
        body, name="gla_states_bwd", grid=(nb,),
        in_specs=[small(False), small(True), big(False), big(True), big(False), big(True)],
        out_specs=[big(False), small(False), big(True), small(True)],
        out_shape=[jax.ShapeDtypeStruct(sf.shape, F32), jax.ShapeDtypeStruct(ef.shape, F32)] * 2,
        scratch_shapes=[pltpu.VMEM((128, 256), F32)] * 2,
        compiler_params=_params(("arbitrary",)),
    )(ef, eb, sf, sb, dsf, dsb)


def loss_head(xt, target, fnw):
    n = target.shape[0]

    def f(x, t, w):
        y = _rms(x, w)
        return 0.5 * jnp.sum(jnp.square(y - t)) * (1.0 / D)

    def body(x_ref, t_ref, w_ref, loss_ref, dx_ref, dw_ref):
        @pl.when(pl.program_id(0) == 0)
        def _():
            loss_ref[...] = jnp.zeros_like(loss_ref)
            dw_ref[...] = jnp.zeros_like(dw_ref)

        val, (dx, dw) = jax.value_and_grad(f, argnums=(0, 2))(x_ref[...], t_ref[...], w_ref[...])
        loss_ref[...] += jnp.broadcast_to(val, loss_ref.shape)
        dx_ref[...] = dx
        dw_ref[...] += dw

    return pl.pallas_call(
        body, name="loss_head", grid=(n // TM,),
        in_specs=[pl.BlockSpec((TM, D), lambda i: (i + NCTXB, 0)), pl.BlockSpec((TM, D), lambda i: (i, 0)),
                  pl.BlockSpec((1, D), lambda i: (0, 0))],
        out_specs=[pl.BlockSpec((1, 128), lambda i: (0, 0)), pl.BlockSpec((TM, D), lambda i: (i, 0)),
                   pl.BlockSpec((1, D), lambda i: (0, 0))],
        out_shape=[jax.ShapeDtypeStruct((1, 128), F32), jax.ShapeDtypeStruct((n, D), F32), jax.ShapeDtypeStruct((1, D), F32)],
        compiler_params=_params(("arbitrary",)),
    )(xt, target, fnw)


def _in_to_padded(w):
    out, pos = [], 0
    for src, wd, dst in sorted(IN_GROUPS, key=lambda g: g[2]):
        if dst > pos:
            out.append(jnp.zeros((w.shape[0], dst - pos), w.dtype))
        out.append(w[:, src:src + wd])
        pos = dst + wd
    if pos < P_COLS:
        out.append(jnp.zeros((w.shape[0], P_COLS - pos), w.dtype))
    return jnp.concatenate(out, axis=1)


def _in_from_padded(g):
    return jnp.concatenate([g[:, dst:dst + wd] for _, wd, dst in IN_GROUPS], axis=1)


def _uq_to_padded(w):
    return jnp.pad(w.reshape(256, 4, 192), ((0, 0), (0, 0), (0, 64))).reshape(256, 1024)


def _uq_from_padded(g):
    return g.reshape(256, 4, 256)[:, :, :192].reshape(256, 768)


def _ukv_to_padded(w):
    return w.reshape(256, 4, 2, 128).transpose(0, 2, 1, 3).reshape(256, 1024)


def _ukv_from_padded(g):
    return g.reshape(256, 2, 4, 128).transpose(0, 2, 1, 3).reshape(256, 1024)


def _rope_tables(n):
    pos = jnp.arange(n)
    freq = ROPE_BASE ** (-jnp.arange(16, dtype=F32) * 2.0 / 32.0)
    ar = (pos // GRID_W).astype(F32)[:, None] * freq[None, :]
    ac = (pos % GRID_W).astype(F32)[:, None] * freq[None, :]
    z = jnp.zeros((n, 64), F32)
    cs = jnp.concatenate([jnp.cos(ar), jnp.cos(ar), jnp.cos(ac), jnp.cos(ac), z], axis=1)
    sn = jnp.concatenate([-jnp.sin(ar), jnp.sin(ar), -jnp.sin(ac), jnp.sin(ac), z], axis=1)
    cs_c = jnp.concatenate([jnp.ones((CTX, 64), F32), jnp.zeros((CTX, 64), F32)], axis=1)
    return jnp.concatenate([cs_c, cs], axis=0), jnp.concatenate([jnp.zeros((CTX, 128), F32), sn], axis=0)


def _small_views(sp):
    wg = jnp.concatenate([jnp.pad(sp["gla_wg_fwd"], ((0, 112), (0, 0))), jnp.pad(sp["gla_wg_bwd"], ((16, 96), (0, 0)))], axis=1)
    return dict(
        n1w=sp["norm1_w"][None], n2w=sp["norm2_w"][None],
        sgu_nw=sp["sgu_norm_w"][None], sgu_nb=sp["sgu_norm_b"][None], sgu_w=sp["sgu_w"],
        sgu_bm=jnp.repeat(sp["sgu_b"].T, 64, axis=1),
        wg=wg, bg=jnp.concatenate([sp["gla_bg_fwd"], sp["gla_bg_bwd"]])[None],
        gla_nwt=jnp.tile(sp["gla_norm_w"], 4)[None],
        kvw=sp["mla_kv_norm_w"][None], qw=sp["mla_q_norm_w"][None])


def _small_grads(g):
    return dict(
        norm1_w=g["n1w"][0], norm2_w=g["n2w"][0],
        sgu_norm_w=g["sgu_nw"][0], sgu_norm_b=g["sgu_nb"][0], sgu_w=g["sgu_w"],
        sgu_b=g["sgu_bm"].reshape(128, 4, 64).sum(-1).T,
        gla_wg_fwd=g["wg"][0:16, 0:128], gla_wg_bwd=g["wg"][16:32, 128:256],
        gla_bg_fwd=g["bg"][0, 0:128], gla_bg_bwd=g["bg"][0, 128:256],
        gla_norm_w=g["gla_nwt"].reshape(4, 64).sum(0),
        mla_kv_norm_w=g["kvw"][0], mla_q_norm_w=g["qw"][0])


def _big_views(w_in, w_out, w_uq, w_ukv, w_ff1, w_ff2):
    win, wuq, wukv = _in_to_padded(w_in), _uq_to_padded(w_uq), _ukv_to_padded(w_ukv)
    return dict(win=win, win_t=win.T, wuq=wuq, wuq_t=wuq.T, wukv=wukv, wukv_t=wukv.T,
                wout=w_out, wout_t=w_out.T, w1=w_ff1, w1_t=w_ff1.T, w2=w_ff2, w2_t=w_ff2.T)


def _layer_ops(p, sv, a):
    pc = lambda off, w: rows(p, w, off // w)
    return dict(
        sgu=[pc(P_SU, 256), pc(P_SV, 256), const(sv["sgu_nw"]), const(sv["sgu_nb"]), const(sv["sgu_w"]), const(sv["sgu_bm"])],
        gates=[pc(P_GATE, 128), const(sv["wg"]), const(sv["bg"])],
        mla_pre=[pc(P_CKV, 256), pc(P_DQ, 256), const(sv["kvw"]), const(sv["qw"])],
        gla_kv=lambda: [pc(P_GK, 128), pc(P_GV, 256), rows(a["gf"]), rows(a["gb"])],
        gla_o=lambda: [pc(P_GQ, 128), pc(P_GK, 128), pc(P_GV, 256), rows(a["gf"]), rows(a["gb"]), pc(P_GR, 256),
                       chunks(a["sf"], TM // GLA_CHUNK), chunks(a["sb"], TM // GLA_CHUNK), const(sv["gla_nwt"])],
        mla_post=lambda: [rows(a["kvu"], 512, 0), rows(a["qu"]), pc(P_KR, 128), rows(a["cs"]), rows(a["sn"])])


def layer_fwd(l, xt, modl, bw, sv, tabs):
    t = xt.shape[0]
    g, nc, cpt = t // TM, t // GLA_CHUNK, TM // GLA_CHUNK
    nm = lambda s: f"l{l}_{s}"
    a = dict(x=xt, cs=tabs[0], sn=tabs[1])
    a["h"], = rw(nm("norm1"), fn_norm1, [rows(xt), const(modl), const(sv["n1w"])], [rowout(t, D, BF16)], g)
    p = a["p"] = mm(nm("in_proj"), a["h"], bw["win"], F32)
    ops = _layer_ops(p, sv, a)
    y_sgu, = rw(nm("sgu"), fn_sgu, ops["sgu"], [rowout(t, 256, BF16)], g)
    a["gf"], a["gb"] = rw(nm("gates"), fn_gates, ops["gates"], [rowout(t, 128, F32)] * 2, g)
    a["uf"], a["ef"], a["ub"], a["eb"] = rw(nm("gla_kv"), fn_gla_kv, ops["gla_kv"](),
                                           [chunkout((nc, 128, 256), F32, cpt), chunkout((nc, 1, 128), F32, cpt)] * 2, g)
    a["sf"], a["sb"] = gla_states(a["uf"], a["ef"], a["ub"], a["eb"])
    y_gla, = rw(nm("gla_o"), fn_gla_o, ops["gla_o"](), [rowout(t, 256, BF16)], g)
    a["ckvn"], a["dqn"] = rw(nm("mla_pre"), fn_mla_pre, ops["mla_pre"], [rowout(t, 256, BF16)] * 2, g)
    a["kvu"] = mm(nm("kv_up"), a["ckvn"], bw["wukv"], BF16)
    a["qu"] = mm(nm("q_up"), a["dqn"], bw["wuq"], F32)
    a["kcat"], a["q"] = rw(nm("mla_post"), fn_mla_post, ops["mla_post"](), [rowout(t, 1024, BF16)] * 2, g)
    a["o"], a["lse"] = flash_fwd(a["q"], a["kcat"], a["kvu"])
    a["y"] = jnp.concatenate([y_sgu, y_gla, a["o"].astype(BF16)], axis=1)
    a["yo"] = mm(nm("out_proj"), a["y"], bw["wout"], F32)
    a["x1"], a["h2"] = rw(nm("res_norm2"), fn_res_norm2, [rows(xt), rows(a["yo"]), const(modl), const(sv["n2w"])],
                          [rowout(t, D, F32), rowout(t, D, BF16)], g)
    a["u"] = mm(nm("ff1"), a["h2"], bw["w1"], F32)
    a["a2"], = rw(nm("act"), fn_act, [rows(a["u"])], [rowout(t, D_FF, BF16)], g)
    a["f"] = mm(nm("ff2"), a["a2"], bw["w2"], F32)
    x2, = rw(nm("res2"), fn_res2, [rows(a["x1"]), rows(a["f"]), const(modl)], [rowout(t, D, F32)], g)
    return x2, a


def fn_assemble(blk, gv1, gv2, ckv, su, sv_, gr, dq, gk1, gk2, pg, kr, gq):
    return (jnp.concatenate([gv1 + gv2, ckv, su, sv_, gr, dq, gk1 + gk2, pg, kr, gq], axis=1),)


def layer_bwd(l, dx2, a, modl, bw, sv):
    t = dx2.shape[0]
    g, nc, cpt = t // TM, t // GLA_CHUNK, TM // GLA_CHUNK
    nm = lambda s: f"l{l}_{s}_bwd"
    p = a["p"]
    ops = _layer_ops(p, sv, a)
    gw, gs = {}, {}
    dx1a, df, dm_a = rw_vjp(nm("res2"), fn_res2, [rows(a["x1"]), rows(a["f"]), const(modl)], [rows(dx2)], [0, 1, 2], g,
                            gdt=[F32, BF16, F32])
    gw["w2"] = mm_tn(nm("ff2_w"), a["a2"], df)
    da2 = mm(nm("ff2_x"), df, bw["w2_t"], F32)
    du, = rw_vjp(nm("act"), fn_act, [rows(a["u"])], [rows(da2)], [0], g, gdt=[BF16])
    gw["w1"] = mm_tn(nm("ff1_w"), a["h2"], du)
    dh2 = mm(nm("ff1_x"), du, bw["w1_t"], F32)
    dxa, dyo, dm_b, gs["n2w"] = rw_vjp(nm("res_norm2"), fn_res_norm2,
                                       [rows(a["x"]), rows(a["yo"]), const(modl), const(sv["n2w"])],
                                       [rows(dx1a), rows(dh2)], [0, 1, 2, 3], g, gdt=[F32, BF16, F32, F32])
    gw["wout"] = mm_tn(nm("out_w"), a["y"], dyo)
    dy = mm(nm("out_x"), dyo, bw["wout_t"], F32)
    dsu, dsv, gs["sgu_nw"], gs["sgu_nb"], gs["sgu_w"], gs["sgu_bm"] = rw_vjp(
        nm("sgu"), fn_sgu, ops["sgu"], [rows(dy, 256, 0)], [0, 1, 2, 3, 4, 5], g)
    dgq, dgk1, dgv1, dgf1, dgb1, dgr, dsf, dsb, gs["gla_nwt"] = rw_vjp(
        nm("gla_o"), fn_gla_o, ops["gla_o"](), [rows(dy, 256, 1)], list(range(9)), g)
    duf, def_, dub, deb = gla_states_bwd(a["ef"], a["eb"], a["sf"], a["sb"], dsf, dsb)
    dgk2, dgv2, dgf, dgb = rw_vjp(nm("gla_kv"), fn_gla_kv, ops["gla_kv"](),
                                  [chunks(duf, cpt), chunks(def_, cpt), chunks(dub, cpt), chunks(deb, cpt)], [0, 1, 2, 3], g,
                                  adds={2: rows(dgf1), 3: rows(dgb1)})
    dpg, gs["wg"], gs["bg"] = rw_vjp(nm("gates"), fn_gates, ops["gates"], [rows(dgf), rows(dgb)], [0, 1, 2], g)
    delta, = rw(nm("attn_delta"), fn_delta, [rows(dy, 512, 1), rows(a["o"])], [rowout(t, 512, F32)], g)
    dq, dkcat, dv = flash_bwd(a["q"], a["kcat"], a["kvu"], dy, a["lse"], delta)
    dkk, dqu, dkr = rw_vjp(nm("mla_post"), fn_mla_post, ops["mla_post"](), [rows(dkcat), rows(dq)], [0, 1, 2], g,
                           gdt=[BF16, BF16, F32])
    dkvu = jnp.concatenate([dkk, dv.astype(BF16)], axis=1)
    gw["wukv"] = mm_tn(nm("kv_up_w"), a["ckvn"], dkvu)
    gw["wuq"] = mm_tn(nm("q_up_w"), a["dqn"], dqu)
    dckvn = mm(nm("kv_up_x"), dkvu, bw["wukv_t"], F32)
    ddqn = mm(nm("q_up_x"), dqu, bw["wuq_t"], F32)
    dckv, ddq, gs["kvw"], gs["qw"] = rw_vjp(nm("mla_pre"), fn_mla_pre, ops["mla_pre"], [rows(dckvn), rows(ddqn)],
                                            [0, 1, 2, 3], g)
    dp, = rw(nm("assemble"), fn_assemble,
             [rows(x_) for x_ in (dgv1, dgv2, dckv, dsu, dsv, dgr, ddq, dgk1, dgk2, dpg, dkr, dgq)],
             [rowout(t, P_COLS, BF16)], g)
    gw["win"] = mm_tn(nm("in_w"), a["h"], dp)
    dh = mm(nm("in_x"), dp, bw["win_t"], F32)
    dx, dm_c, gs["n1w"] = rw_vjp(nm("norm1"), fn_norm1, [rows(a["x"]), const(modl), const(sv["n1w"])], [rows(dh)],
                                 [0, 1, 2], g, adds={0: rows(dxa)})
    big = dict(w_in=_in_from_padded(gw["win"]), w_out=gw["wout"], mla_w_uq=_uq_from_padded(gw["wuq"]),
               mla_w_ukv=_ukv_from_padded(gw["wukv"]), w_ff1=gw["w1"], w_ff2=gw["w2"])
    return dx, dm_a + dm_b + dm_c, big, _small_grads(gs)


SMALL_NAMES = ("norm1_w", "sgu_norm_w", "sgu_norm_b", "sgu_w", "sgu_b", "gla_wg_fwd", "gla_bg_fwd", "gla_wg_bwd",
               "gla_bg_bwd", "gla_norm_w", "mla_q_norm_w", "mla_kv_norm_w", "norm2_w")
BIG_NAMES = ("w_in", "w_out", "mla_w_uq", "mla_w_ukv", "w_ff1", "w_ff2")


def local_step(x, ctx, target, mods, big, small, final_norm_w):
    n = x.shape[0]
    xt = jnp.concatenate([ctx, x], axis=0)
    tabs = _rope_tables(n)
    depth = len(mods)
    bws = [_big_views(*[big[l][k] for k in BIG_NAMES]) for l in range(depth)]
    svs = [_small_views(small[l]) for l in range(depth)]
    acts = []
    for l in range(depth):
        xt, a = layer_fwd(l, xt, mods[l], bws[l], svs[l], tabs)
        acts.append(a)
    loss, dxl, dfnw = loss_head(xt, target, final_norm_w[None])
    dxt = jnp.concatenate([jnp.zeros((CTX, D), F32), dxl], axis=0)
    dmods, gbig, gsmall = [None] * depth, [None] * depth, [None] * depth
    for l in reversed(range(depth)):
        dxt, dmods[l], gbig[l], gsmall[l] = layer_bwd(l, dxt, acts[l], mods[l], bws[l], svs[l])
    return loss, dxt[CTX:], dmods, gbig, gsmall, dfnw


def _group(group):
    x, y, c = lax.axis_index("x"), lax.axis_index("y"), lax.axis_index("c")
    if group == "sib":
        return 2, c, [((x, y, 1 - c), 1 - c)]
    if group == "chip":
        flips = [(1, 0), (0, 1), (1, 1)]
        return 4, 2 * x + y, [((x ^ fx, y ^ fy, c), 2 * (x ^ fx) + (y ^ fy)) for fx, fy in flips]
    flips = [(fx, fy, fc) for fx in (0, 1) for fy in (0, 1) for fc in (0, 1)][1:]
    return 8, 4 * x + 2 * y + c, [((x ^ fx, y ^ fy, c ^ fc), 4 * (x ^ fx) + 2 * (y ^ fy) + (c ^ fc)) for fx, fy, fc in flips]


def _group_size(group):
    return {"sib": 2, "chip": 4, "all": 8}[group]


def xchg(name, entries):
    n_in = sum(len(arrs) for _, _, arrs in entries)
    n_remote = sum(_group_size(g) - 1 if k != "swap" else 1 for k, g, _ in entries)
    n_local = sum(1 for k, _, _ in entries if k != "swap")
    out_shape = []
    for kind, group, arrs in entries:
        a = arrs[0]
        if kind == "gather":
            out_shape.append(jax.ShapeDtypeStruct((_group_size(group),) + a.shape, a.dtype))
        else:
            out_shape.append(jax.ShapeDtypeStruct(a.shape, a.dtype))

    def body(*refs):
        in_refs, out_refs = refs[:n_in], refs[n_in:n_in + len(entries)]
        send_sems, recv_sems, local_sems = refs[n_in + len(entries):]
        c = lax.axis_index("c")
        pos, k, kl = 0, 0, 0
        waits = []
        for (kind, group, arrs), out in zip(entries, out_refs):
            srcs = in_refs[pos:pos + len(arrs)]
            pos += len(arrs)
            _, mine, peers = _group(group)
            if kind == "swap":
                (dev, _), = peers
                for core, src in ((0, srcs[1]), (1, srcs[0])):
                    @pl.when(c == core)
                    def _(src=src, k=k):
                        pltpu.make_async_remote_copy(src_ref=src, dst_ref=out, send_sem=send_sems.at[k],
                                                     recv_sem=recv_sems.at[k], device_id=dev, device_id_type=MESH).start()
                waits.append(pltpu.make_async_remote_copy(src_ref=srcs[0], dst_ref=out, send_sem=send_sems.at[k],
                                                          recv_sem=recv_sems.at[k], device_id=dev, device_id_type=MESH))
                k += 1
                continue
            src = srcs[0]
            own = pltpu.make_async_copy(src if kind == "gather" else src.at[mine], out.at[mine], local_sems.at[kl])
            own.start()
            kl += 1
            for dev, slot in peers:
                piece = src if kind == "gather" else src.at[slot]
                pltpu.make_async_remote_copy(src_ref=piece, dst_ref=out.at[mine], send_sem=send_sems.at[k],
                                             recv_sem=recv_sems.at[k], device_id=dev, device_id_type=MESH).start()
                waits.append(pltpu.make_async_remote_copy(src_ref=piece, dst_ref=out.at[slot], send_sem=send_sems.at[k],
                                                          recv_sem=recv_sems.at[k], device_id=dev, device_id_type=MESH))
                k += 1
            waits.append(own)
        for w in waits:
            w.wait()

    any_spec = pl.BlockSpec(memory_space=pl.ANY)
    return pl.pallas_call(
        body, name=name,
        in_specs=[any_spec] * n_in, out_specs=[any_spec] * len(entries), out_shape=out_shape,
        scratch_shapes=[pltpu.SemaphoreType.DMA((n_remote,)), pltpu.SemaphoreType.DMA((n_remote,)),
                        pltpu.SemaphoreType.DMA((max(n_local, 1),))],
    )(*[a for _, _, arrs in entries for a in arrs])


def _block_rows(r, c, budget=131072):
    tr = 8
    while tr * 2 * c <= budget and r % (tr * 2) == 0:
        tr *= 2
    return tr if r % tr == 0 else r


def tree_sum(name, parts):
    g, r, c = parts.shape
    tr = _block_rows(r, c)

    def body(p_ref, o_ref):
        p = [p_ref[i] for i in range(g)]
        while len(p) > 1:
            p = [p[i] + p[i + 1] for i in range(0, len(p), 2)]
        o_ref[...] = p[0]

    return pl.pallas_call(
        body, name=name, grid=(r // tr,),
        in_specs=[pl.BlockSpec((g, tr, c), lambda i: (0, i, 0))], out_specs=pl.BlockSpec((tr, c), lambda i: (i, 0)),
        out_shape=jax.ShapeDtypeStruct((r, c), parts.dtype), compiler_params=_params(("arbitrary",)),
    )(parts)


def pair_sum(name, g0, g1, recv, core):
    r, c = recv.shape
    tr = _block_rows(r, c)

    def body(a_ref, b_ref, r_ref, k_ref, o_ref):
        o_ref[...] = jnp.where(k_ref[...] > 0.5, b_ref[...], a_ref[...]) + r_ref[...]

    blk = pl.BlockSpec((tr, c), lambda i: (i, 0))
    return pl.pallas_call(
        body, name=name, grid=(r // tr,), in_specs=[blk, blk, blk, pl.BlockSpec((1, 1), lambda i: (0, 0))],
        out_specs=blk, out_shape=jax.ShapeDtypeStruct((r, c), F32), compiler_params=_params(("arbitrary",)),
    )(g0, g1, recv, core)


def adamw(name, w, g, m, v):
    r, c = w.shape
    tr = _block_rows(r, c)

    def body(w_ref, g_ref, m_ref, v_ref, d_ref, nm_ref, nv_ref):
        gg = g_ref[...]
        nm = ADAM_B1 * m_ref[...] + (1.0 - ADAM_B1) * gg
        nv = ADAM_B2 * v_ref[...] + (1.0 - ADAM_B2) * jnp.square(gg)
        m_hat = nm / (1.0 - ADAM_B1 ** ADAM_STEP)
        v_hat = nv / (1.0 - ADAM_B2 ** ADAM_STEP)
        d_ref[...] = -ADAM_LR * (m_hat / (jnp.sqrt(v_hat) + ADAM_EPS) + ADAM_WD * w_ref[...])
        nm_ref[...] = nm
        nv_ref[...] = nv

    blk = pl.BlockSpec((tr, c), lambda i: (i, 0))
    return pl.pallas_call(
        body, name=name, grid=(r // tr,), in_specs=[blk] * 4, out_specs=[blk] * 3,
        out_shape=[jax.ShapeDtypeStruct((r, c), F32)] * 3, compiler_params=_params(("arbitrary",)),
    )(w, g, m, v)


W_MOD_COLS = 6 * D // 4
MOD_TN = 512


def mod_project(c16, w_mod, b_loc):
    def body(c_ref, w_ref, b_ref, o_ref):
        cv = c_ref[...]
        s = (cv * _sigmoid(cv)).astype(BF16)
        o_ref[0] = _dot(s, w_ref[0].astype(BF16)) + b_ref[0]

    return pl.pallas_call(
        body, name="mod_project", grid=(2, W_MOD_COLS // MOD_TN),
        in_specs=[pl.BlockSpec((16, D), lambda l, j: (0, 0)), pl.BlockSpec((1, D, MOD_TN), lambda l, j: (l, 0, j)),
                  pl.BlockSpec((1, 1, MOD_TN), lambda l, j: (l, 0, j))],
        out_specs=pl.BlockSpec((1, 16, MOD_TN), lambda l, j: (l, 0, j)),
        out_shape=jax.ShapeDtypeStruct((2, 16, W_MOD_COLS), F32), compiler_params=_params(("arbitrary", "arbitrary")),
    )(c16, w_mod, b_loc)


def mod_weight_grad(c16, dm16):
    def body(c_ref, d_ref, o_ref):
        cv = c_ref[...]
        o_ref[0] = _dot(cv * _sigmoid(cv), d_ref[0], ((0,), (0,)), precision=HI)

    return pl.pallas_call(
        body, name="mod_weight_grad", grid=(2, W_MOD_COLS // MOD_TN),
        in_specs=[pl.BlockSpec((16, D), lambda l, j: (0, 0)), pl.BlockSpec((1, 16, MOD_TN), lambda l, j: (l, 0, j))],
        out_specs=pl.BlockSpec((1, D, MOD_TN), lambda l, j: (l, 0, j)),
        out_shape=jax.ShapeDtypeStruct((2, D, W_MOD_COLS), F32), compiler_params=_params(("arbitrary", "arbitrary")),
    )(c16, dm16)


def cctx_partial(dmc, w_mod):
    def body(d_ref, w_ref, o_ref):
        @pl.when(pl.program_id(0) == 0)
        def _():
            o_ref[...] = jnp.zeros_like(o_ref)
        o_ref[...] += _dot(d_ref[0], w_ref[0], ((1,), (1,)), precision=HI)

    return pl.pallas_call(
        body, name="cctx_partial", grid=(2,),
        in_specs=[pl.BlockSpec((1, 8, W_MOD_COLS), lambda l: (l, 0, 0)), pl.BlockSpec((1, D, W_MOD_COLS), lambda l: (l, 0, 0))],
        out_specs=pl.BlockSpec((8, D), lambda l: (0, 0)),
        out_shape=jax.ShapeDtypeStruct((8, D), F32), compiler_params=_params(("arbitrary",)),
    )(dmc, w_mod)


def cctx_grad(parts, c_ctx8):
    def body(p_ref, c_ref, o_ref):
        ds = (p_ref[0] + p_ref[1]) + (p_ref[2] + p_ref[3])
        _, vf = jax.vjp(lambda z: z * _sigmoid(z), c_ref[...])
        o_ref[...] = vf(ds)[0]

    return pl.pallas_call(
        body, name="cctx_grad", out_shape=jax.ShapeDtypeStruct((8, D), F32),
    )(parts, c_ctx8)


ARG_NAMES = ("x", "c", "ctx", "c_ctx", "w_mod", "b_mod", "norm1_w", "w_in", "w_out", "sgu_norm_w", "sgu_norm_b", "sgu_w",
             "sgu_b", "gla_wg_fwd", "gla_bg_fwd", "gla_wg_bwd", "gla_bg_bwd", "gla_norm_w", "mla_q_norm_w", "mla_w_uq",
             "mla_kv_norm_w", "mla_w_ukv", "norm2_w", "w_ff1", "w_ff2", "final_norm_w")
WEIGHT_NAMES = ARG_NAMES[3:]
PACKED = ("c_ctx", "b_mod") + SMALL_NAMES + ("final_norm_w",)
ROW_SHARDED = ("w_out", "w_ff2")
PACK_ROWS = 256


def _pack(vectors):
    flat = jnp.concatenate([v.reshape(-1) for v in vectors])
    n = flat.shape[0]
    total = -(-n // (PACK_ROWS * LANES)) * PACK_ROWS * LANES
    return jnp.pad(flat, (0, total - n)).reshape(-1, LANES)


def _unpack(buf, shapes):
    flat, out, pos = buf.reshape(-1), [], 0
    for shp in shapes:
        n = int(np.prod(shp))
        out.append(flat[pos:pos + n].reshape(shp))
        pos += n
    return out


def _full_weight(name, gathered, l):
    g = gathered[:, l]
    if name in ROW_SHARDED:
        return g.reshape(-1, g.shape[-1])
    return g.transpose(1, 0, 2).reshape(g.shape[1], -1)


def _chip_chunks(name, a):
    if name in ROW_SHARDED:
        return a.reshape(4, a.shape[0] // 4, a.shape[1])
    return a.reshape(a.shape[0], 4, a.shape[1] // 4).transpose(1, 0, 2)


def kernel(x, c, ctx, c_ctx, w_mod, b_mod, norm1_w, w_in, w_out, sgu_norm_w, sgu_norm_b, sgu_w, sgu_b, gla_wg_fwd, gla_bg_fwd, gla_wg_bwd, gla_bg_bwd, gla_norm_w, mla_q_norm_w, mla_w_uq, mla_kv_norm_w, mla_w_ukv, norm2_w, w_ff1, w_ff2, final_norm_w, loss_target, m_c_ctx, m_w_mod, m_b_mod, m_norm1_w, m_w_in, m_w_out, m_sgu_norm_w, m_sgu_norm_b, m_sgu_w, m_sgu_b, m_gla_wg_fwd, m_gla_bg_fwd, m_gla_wg_bwd, m_gla_bg_bwd, m_gla_norm_w, m_mla_q_norm_w, m_mla_w_uq, m_mla_kv_norm_w, m_mla_w_ukv, m_norm2_w, m_w_ff1, m_w_ff2, m_final_norm_w, v_c_ctx, v_w_mod, v_b_mod, v_norm1_w, v_w_in, v_w_out, v_sgu_norm_w, v_sgu_norm_b, v_sgu_w, v_sgu_b, v_gla_wg_fwd, v_gla_bg_fwd, v_gla_wg_bwd, v_gla_bg_bwd, v_gla_norm_w, v_mla_q_norm_w, v_mla_w_uq, v_mla_kv_norm_w, v_mla_w_ukv, v_norm2_w, v_w_ff1, v_w_ff2, v_final_norm_w):
    args = (x, c, ctx, c_ctx, w_mod, b_mod, norm1_w, w_in, w_out, sgu_norm_w, sgu_norm_b, sgu_w, sgu_b, gla_wg_fwd, gla_bg_fwd, gla_wg_bwd, gla_bg_bwd, gla_norm_w, mla_q_norm_w, mla_w_uq, mla_kv_norm_w, mla_w_ukv, norm2_w, w_ff1, w_ff2, final_norm_w)
    w = dict(zip(ARG_NAMES, args))
    moms = (m_c_ctx, m_w_mod, m_b_mod, m_norm1_w, m_w_in, m_w_out, m_sgu_norm_w, m_sgu_norm_b, m_sgu_w, m_sgu_b, m_gla_wg_fwd, m_gla_bg_fwd, m_gla_wg_bwd, m_gla_bg_bwd, m_gla_norm_w, m_mla_q_norm_w, m_mla_w_uq, m_mla_kv_norm_w, m_mla_w_ukv, m_norm2_w, m_w_ff1, m_w_ff2, m_final_norm_w)
    vars_ = (v_c_ctx, v_w_mod, v_b_mod, v_norm1_w, v_w_in, v_w_out, v_sgu_norm_w, v_sgu_norm_b, v_sgu_w, v_sgu_b, v_gla_wg_fwd, v_gla_bg_fwd, v_gla_wg_bwd, v_gla_bg_bwd, v_gla_norm_w, v_mla_q_norm_w, v_mla_w_uq, v_mla_kv_norm_w, v_mla_w_ukv, v_norm2_w, v_w_ff1, v_w_ff2, v_final_norm_w)
    m1 = dict(zip(WEIGHT_NAMES, moms))
    m2 = dict(zip(WEIGHT_NAMES, vars_))
    xi, yi, ci = lax.axis_index("x"), lax.axis_index("y"), lax.axis_index("c")
    chip, dev = 2 * xi + yi, 4 * xi + 2 * yi + ci
    depth = w_mod.shape[0]

    got = xchg("gather_inputs", [("gather", "all", [c])] + [("gather", "chip", [w[k].astype(BF16)]) for k in BIG_NAMES])
    c_all, shards = got[0], dict(zip(BIG_NAMES, got[1:]))
    c16 = jnp.concatenate([c_all.reshape(8, D), c_ctx[None], jnp.zeros((7, D), F32)], axis=0)
    b_loc = lax.dynamic_slice_in_dim(b_mod, chip * W_MOD_COLS, W_MOD_COLS, axis=1)[:, None, :]
    mod_part = mod_project(c16, w_mod, b_loc)
    mod_all, = xchg("gather_mod", [("gather", "chip", [mod_part])])
    mod_full = mod_all.transpose(1, 2, 0, 3).reshape(depth, 16, 6 * D)
    mods = [jnp.stack([mod_full[l, 8], lax.dynamic_index_in_dim(mod_full[l], dev, 0, keepdims=False)])[:, None, :]
            for l in range(depth)]

    big = [{k: _full_weight(k, shards[k], l) for k in BIG_NAMES} for l in range(depth)]
    small = [{k: w[k][l] for k in SMALL_NAMES} for l in range(depth)]
    loss, grad_x, dmods, gbig, gsmall, dfnw = local_step(x[0], ctx[0], loss_target[0], mods, big, small, final_norm_w)
    loss = lax.psum(loss[0, 0], ("x", "y", "c"))

    dm_lat = jnp.stack([dmods[l][1, 0] for l in range(depth)])
    dm_ctx = jnp.stack([dmods[l][0, 0] for l in range(depth)])
    small_pack = _pack([dm_lat, dm_ctx] + [jnp.stack([gsmall[l][k] for l in range(depth)]) for k in SMALL_NAMES] + [dfnw])
    got = xchg("exchange_grads", [("gather", "all", [small_pack])] + [("swap", "sib", [gbig[0][k], gbig[1][k]]) for k in BIG_NAMES])
    small_all, from_sib = got[0], dict(zip(BIG_NAMES, got[1:]))
    small_sum = tree_sum("small_grad_sum", small_all)
    core = ci.astype(F32).reshape(1, 1)
    mine = {k: pair_sum(f"pair_sum_{k}", gbig[0][k], gbig[1][k], from_sib[k], core) for k in BIG_NAMES}

    n_dm = depth * 6 * D
    dm_rows = n_dm // LANES
    dm_lat_all = small_all[:, :dm_rows].reshape(8, depth, 6 * D)
    dm_ctx_sum = small_sum[dm_rows:2 * dm_rows].reshape(depth, 6 * D)
    take = lambda a: lax.dynamic_slice_in_dim(a, chip * W_MOD_COLS, W_MOD_COLS, axis=-1)
    dmc_loc = take(dm_ctx_sum)
    cc_part = cctx_partial(jnp.pad(dmc_loc[:, None, :], ((0, 0), (0, 7), (0, 0))), w_mod)
    got = xchg("scatter_grads", [("gather", "chip", [cc_part])] + [("scatter", "chip", [_chip_chunks(k, mine[k])]) for k in BIG_NAMES])
    cc_parts, chunks_in = got[0], dict(zip(BIG_NAMES, got[1:]))
    reduced = {k: tree_sum(f"chip_sum_{k}", chunks_in[k]) for k in BIG_NAMES}
    g_c_ctx = cctx_grad(cc_parts, jnp.broadcast_to(c_ctx[None], (8, D)))[0]

    got = xchg("share_layers", [("gather", "sib", [reduced[k]]) for k in BIG_NAMES])
    grads = dict(zip(BIG_NAMES, got))

    dm16 = jnp.concatenate([take(dm_lat_all).transpose(1, 0, 2), dmc_loc[:, None, :], jnp.zeros((depth, 7, W_MOD_COLS), F32)], axis=1)
    grads["w_mod"] = mod_weight_grad(c16, dm16)
    flat_sum = small_sum.reshape(-1)
    g_b_mod = (flat_sum[:n_dm] + flat_sum[n_dm:2 * n_dm]).reshape(depth, 6 * D)
    rest_shapes = [w[k].shape for k in PACKED[2:]]
    n_rest = sum(int(np.prod(s)) for s in rest_shapes)
    for k, g in zip(PACKED, [g_c_ctx, g_b_mod] + _unpack(flat_sum[2 * n_dm:2 * n_dm + n_rest], rest_shapes)):
        grads[k] = g

    delta, new_m, new_v = {}, {}, {}
    for k in BIG_NAMES + ("w_mod",):
        view = lambda a: a.reshape(-1, a.shape[-1])
        d_, m_, v_ = adamw(f"adamw_{k}", view(w[k]), view(grads[k]), view(m1[k]), view(m2[k]))
        delta[k], new_m[k], new_v[k] = d_.reshape(w[k].shape), m_.reshape(w[k].shape), v_.reshape(w[k].shape)
    shapes = [w[k].shape for k in PACKED]
    d_, m_, v_ = adamw("adamw_small", _pack([w[k] for k in PACKED]), _pack([grads[k] for k in PACKED]),
                       _pack([m1[k] for k in PACKED]), _pack([m2[k] for k in PACKED]))
    for k, dk, mk, vk in zip(PACKED, _unpack(d_, shapes), _unpack(m_, shapes), _unpack(v_, shapes)):
        delta[k], new_m[k], new_v[k] = dk, mk, vk
    return (loss, grad_x[None], *[grads[k] for k in WEIGHT_NAMES], *[delta[k] for k in WEIGHT_NAMES],
            *[new_m[k] for k in WEIGHT_NAMES], *[new_v[k] for k in WEIGHT_NAMES])
```

```python
import functools
import math

import numpy as np
import jax
import jax.numpy as jnp
from jax import lax
from jax.experimental import pallas as pl
from jax.experimental.pallas import tpu as pltpu

F32 = jnp.float32
BF16 = jnp.bfloat16
HI = lax.Precision.HIGHEST
EPS = 1e-6
VMEM_LIMIT_BYTES = 56 * 1024 * 1024
LANES = 128

D = 1024
D_FF = 4096
CTX = 256
GRID_W = 64
SGU_CHUNK = 128
GLA_CHUNK = 64
GLA_TAU = 16.0
GLA_DK = 32
MLA_SCALE = (128 + 64) ** -0.5
ROPE_BASE = 10000.0
TM = 256
NCTXB = CTX // TM
P_GV, P_CKV, P_SU, P_SV, P_GR, P_DQ, P_GK, P_GATE, P_KR, P_GQ = 0, 256, 512, 768, 1024, 1280, 1536, 1664, 1792, 1920
P_COLS = 2048
IN_GROUPS = ((0, 128, P_GK), (128, 256, P_GV), (384, 32, P_GATE), (416, 256, P_CKV), (672, 64, P_KR),
             (736, 256, P_SU), (992, 256, P_SV), (1248, 128, P_GQ), (1376, 256, P_GR), (1632, 256, P_DQ))
ADAM_LR, ADAM_B1, ADAM_B2, ADAM_EPS, ADAM_WD, ADAM_STEP = 0.001, 0.9, 0.999, 1e-08, 0.01, 10
MESH = pl.DeviceIdType.MESH


def _params(sem):
    return pltpu.CompilerParams(dimension_semantics=sem, vmem_limit_bytes=VMEM_LIMIT_BYTES)


def _pick(n, cands):
    for c in cands:
        if n % c == 0:
            return c
    return n


class Op:
    def __init__(self, arr, blk, idx, gshape, gidx, acc):
        self.arr, self.blk, self.idx, self.gshape, self.gidx, self.acc = arr, blk, idx, gshape, gidx, acc

    def spec(self):
        return pl.BlockSpec(self.blk, self.idx)


def rows(arr, width=None, cb=0, off=0, tm=TM):
    w = arr.shape[1] if width is None else width
    n = arr.shape[0] - off * tm
    return Op(arr, (tm, w), lambda i: (i + off, cb), (n, w), lambda i: (i, 0), False)


def chunks(arr, per_tile):
    z = (0,) * (arr.ndim - 1)
    return Op(arr, (per_tile,) + arr.shape[1:], lambda i: (i,) + z, arr.shape, lambda i: (i,) + z, False)


def const(arr):
    z = (0,) * arr.ndim
    return Op(arr, arr.shape, lambda i: z, arr.shape, lambda i: z, True)


def rw(name, fn, ins, outs, grid):
    nin = len(ins)

    def body(*refs):
        vals = [r[...] for r in refs[:nin]]
        res = fn(pl.program_id(0), *vals)
        for o, r in zip(refs[nin:], res):
            o[...] = r.astype(o.dtype)

    return pl.pallas_call(
        body, name=name, grid=(grid,),
        in_specs=[o.spec() for o in ins],
        out_specs=[pl.BlockSpec(b, ix) for (_, _, b, ix) in outs],
        out_shape=[jax.ShapeDtypeStruct(s, d) for (s, d, _, _) in outs],
        compiler_params=_params(("arbitrary",)),
    )(*[o.arr for o in ins])


def rowout(n, w, dtype, tm=TM):
    return ((n, w), dtype, (tm, w), lambda i: (i, 0))


def chunkout(shape, dtype, per_tile):
    z = (0,) * (len(shape) - 1)
    return (shape, dtype, (per_tile,) + tuple(shape[1:]), lambda i: (i,) + z)


def rw_vjp(name, fn, ins, cots, wrt, grid, gdt=None, adds=None):
    nin = len(ins)
    cot_ops = [c for c in cots if c is not None]
    add_items = sorted((adds or {}).items())
    gdt = gdt or [F32] * len(wrt)
    ncot, nadd = len(cot_ops), len(add_items)

    def body(*refs):
        i = pl.program_id(0)
        vals = [r[...] for r in refs[:nin]]
        cvals = [r[...] for r in refs[nin:nin + ncot]]
        avals = [r[...] for r in refs[nin + ncot:nin + ncot + nadd]]
        grefs = refs[nin + ncot + nadd:]

        def f(*d):
            a = list(vals)
            for k, dv in zip(wrt, d):
                a[k] = dv
            return tuple(fn(i, *a))

        outs, vf = jax.vjp(f, *[vals[k] for k in wrt])
        it = iter(cvals)
        ct = tuple(jnp.zeros_like(o) if c is None else next(it).astype(o.dtype) for c, o in zip(cots, outs))
        gs = list(vf(ct))
        for (pos, _), av in zip(add_items, avals):
            gs[pos] = gs[pos].astype(F32) + av.astype(F32)
        for pos, (k, g, gref) in enumerate(zip(wrt, gs, grefs)):
            if ins[k].acc:
                @pl.when(i == 0)
                def _():
                    gref[...] = jnp.zeros_like(gref)
                gref[...] += g.astype(gref.dtype)
            else:
                gref[...] = g.astype(gref.dtype)

    all_in = list(ins) + cot_ops + [op for _, op in add_items]
    return pl.pallas_call(
        body, name=name, grid=(grid,),
        in_specs=[o.spec() for o in all_in],
        out_specs=[pl.BlockSpec(ins[k].blk, ins[k].gidx) for k in wrt],
        out_shape=[jax.ShapeDtypeStruct(ins[k].gshape, dt) for k, dt in zip(wrt, gdt)],
        compiler_params=_params(("arbitrary",)),
    )(*[o.arr for o in all_in])


MM_VMEM_BUDGET = 40 * 1024 * 1024
MM_COLS = 1024


def _square_bf16(a):
    a = a.astype(F32)
    return (a * a).astype(BF16)


def mm(name, a, b, out_dtype, pre=None, post=None, extras=()):
    m, k = a.shape
    _, n = b.shape
    nc = min(n, MM_COLS)
    row_bytes = k * a.dtype.itemsize + n * jnp.dtype(out_dtype).itemsize + sum(n * e.dtype.itemsize for e in extras)
    tm = next(t for t in (768, 512, 384, 256, 128, 64)
              if m % t == 0 and 2 * t * row_bytes + 2 * k * n * b.dtype.itemsize + t * nc * 4 <= MM_VMEM_BUDGET)

    def body(a_ref, b_ref, *rest):
        o_ref = rest[-1]
        av = a_ref[...]
        if pre is not None:
            av = pre(av)
        for j in range(n // nc):
            cs = slice(j * nc, (j + 1) * nc)
            acc = lax.dot_general(av, b_ref[:, cs], (((1,), (0,)), ((), ())), preferred_element_type=F32)
            if post is not None:
                acc = post(acc, *[e[:, cs] for e in rest[:-1]])
            o_ref[:, cs] = acc.astype(o_ref.dtype)

    row = lambda w: pl.BlockSpec((tm, w), lambda i: (i, 0))
    return pl.pallas_call(
        body, name=name, grid=(m // tm,),
        in_specs=[row(k), pl.BlockSpec((k, n), lambda i: (0, 0))] + [row(n) for _ in extras],
        out_specs=row(n),
        out_shape=jax.ShapeDtypeStruct((m, n), out_dtype),
        compiler_params=_params(("arbitrary",)),
    )(a, b, *extras)


def mm_tn(name, a, b, pre=None):
    m, ka = a.shape
    _, nb = b.shape
    tm = _pick(m, (768, 512, 256))
    ta = _pick(ka, (2048, 1024, 512, 256, 128))
    tb = _pick(nb, tuple(t for t in (4096, 2048, 1024, 512, 256, 128) if ta * t * 4 <= 8 * 1024 * 1024))

    def body(a_ref, b_ref, o_ref):
        @pl.when(pl.program_id(2) == 0)
        def _():
            o_ref[...] = jnp.zeros_like(o_ref)
        av = a_ref[...] if pre is None else pre(a_ref[...])
        o_ref[...] += lax.dot_general(av, b_ref[...], (((0,), (0,)), ((), ())), preferred_element_type=F32)

    return pl.pallas_call(
        body, name=name, grid=(ka // ta, nb // tb, m // tm),
        in_specs=[pl.BlockSpec((tm, ta), lambda i, j, k: (k, i)), pl.BlockSpec((tm, tb), lambda i, j, k: (k, j))],
        out_specs=pl.BlockSpec((ta, tb), lambda i, j, k: (i, j)),
        out_shape=jax.ShapeDtypeStruct((ka, nb), F32),
        compiler_params=_params(("arbitrary", "arbitrary", "arbitrary")),
    )(a, b)


def _rms(x, w):
    return x * lax.rsqrt(jnp.mean(x * x, axis=-1, keepdims=True) + EPS) * w


def _mod_of(blk, m):
    return jnp.where(blk < NCTXB, m[0], m[1])


def _gelu(x):
    return x * (0.5 * (1.0 + jnp.tanh(math.sqrt(2.0 / math.pi) * (x + 0.044715 * (x * x * x)))))


def _sigmoid(x):
    return 1.0 / (1.0 + jnp.exp(-x))


def _log_sigmoid(z):
    return jnp.minimum(z, 0.0) - jnp.log(1.0 + jnp.exp(-jnp.abs(z)))


def _dot(a, b, dims=((1,), (0,)), precision=None):
    return lax.dot_general(a, b, (dims, ((), ())), precision=precision, preferred_element_type=F32)


def _lane_group_mask(width, group, h):
    lane = lax.broadcasted_iota(jnp.int32, (1, width), 1)
    return (lane >= h * group) & (lane < (h + 1) * group)


def fn_norm1(blk, x, m, nw):
    mv = _mod_of(blk, m)
    return ((_rms(x, nw) * (1.0 + mv[:, D:2 * D]) + mv[:, 0:D]),)


def fn_res_norm2(blk, x, yo, m, nw):
    mv = _mod_of(blk, m)
    x1 = x + mv[:, 2 * D:3 * D] * yo
    return x1, _rms(x1, nw) * (1.0 + mv[:, 4 * D:5 * D]) + mv[:, 3 * D:4 * D]


def fn_res2(blk, x1, f, m):
    mv = _mod_of(blk, m)
    return (x1 + mv[:, 5 * D:6 * D] * f,)


def fn_sgu(blk, su, sv, nw, nb, ws, bm):
    u = _gelu(su)
    g = _gelu(sv)
    mu = jnp.mean(g, axis=-1, keepdims=True)
    var = jnp.mean(jnp.square(g - mu), axis=-1, keepdims=True)
    v = (g - mu) * lax.rsqrt(var + EPS) * nw + nb
    out = []
    for c in range(su.shape[0] // SGU_CHUNK):
        vc = v[c * SGU_CHUNK:(c + 1) * SGU_CHUNK]
        s = bm
        for h in range(4):
            vh = jnp.where(_lane_group_mask(256, 64, h), vc, 0.0)
            s = s + _dot(ws[h].astype(BF16), vh.astype(BF16))
        out.append(u[c * SGU_CHUNK:(c + 1) * SGU_CHUNK] * s)
    return (jnp.concatenate(out, axis=0),)


def fn_gates(blk, pg, wg, bg):
    z = _dot(pg.astype(BF16), wg.astype(BF16)) + bg
    g = _log_sigmoid(z) * (1.0 / GLA_TAU)
    return g[:, :128], g[:, 128:]


def _gla_chunk_terms(g, rev):
    r = lax.broadcasted_iota(jnp.int32, (GLA_CHUNK, GLA_CHUNK), 0)
    c = lax.broadcasted_iota(jnp.int32, (GLA_CHUNK, GLA_CHUNK), 1)
    tri = jnp.where((c >= r) if rev else (c <= r), 1.0, 0.0).astype(F32)
    b = _dot(tri, g, precision=HI)
    return b, jnp.sum(g, axis=0, keepdims=True)


def _bd_mask():
    r = lax.broadcasted_iota(jnp.int32, (128, 256), 0)
    c = lax.broadcasted_iota(jnp.int32, (128, 256), 1)
    return (r // GLA_DK) == (c // 64)


def _gla_kv_chunk(k, v, g, rev):
    b, tot = _gla_chunk_terms(g, rev)
    kd = k * jnp.exp(tot - b)
    u = jnp.where(_bd_mask(), _dot(kd.astype(BF16), v.astype(BF16), ((0,), (0,))), 0.0)
    return u, jnp.exp(tot)


def _gla_o_chunk(q, k, v, g, s, rev):
    b, _ = _gla_chunk_terms(g, rev)
    qe = q * jnp.exp(b) * (GLA_DK ** -0.5)
    ke = k * jnp.exp(-b)
    o = _dot(qe.astype(BF16), jnp.where(_bd_mask(), s, 0.0).astype(BF16))
    qs = jnp.concatenate([jnp.where(_lane_group_mask(128, GLA_DK, h), qe, 0.0) for h in range(4)], axis=0)
    a = _dot(qs.astype(BF16), ke.astype(BF16), ((1,), (1,)))
    i = lax.broadcasted_iota(jnp.int32, a.shape, 0) % GLA_CHUNK
    j = lax.broadcasted_iota(jnp.int32, a.shape, 1)
    a = jnp.where((j >= i) if rev else (j <= i), a, 0.0)
    av = _dot(a.astype(BF16), v.astype(BF16))
    for h in range(4):
        o = o + jnp.where(_lane_group_mask(256, 64, h), av[GLA_CHUNK * h:GLA_CHUNK * (h + 1)], 0.0)
    return o


def fn_gla_kv(blk, k, v, gf, gb):
    uf, ef, ub, eb = [], [], [], []
    for c in range(k.shape[0] // GLA_CHUNK):
        sl = slice(c * GLA_CHUNK, (c + 1) * GLA_CHUNK)
        u, e = _gla_kv_chunk(k[sl], v[sl], gf[sl], False)
        uf.append(u[None]); ef.append(e[None])
        u, e = _gla_kv_chunk(k[sl], v[sl], gb[sl], True)
        ub.append(u[None]); eb.append(e[None])
    cat = lambda t: jnp.concatenate(t, axis=0)
    return cat(uf), cat(ef), cat(ub), cat(eb)


def fn_gla_o(blk, q, k, v, gf, gb, gr, sf, sb, nwt):
    out = []
    for c in range(q.shape[0] // GLA_CHUNK):
        sl = slice(c * GLA_CHUNK, (c + 1) * GLA_CHUNK)
        out.append(_gla_o_chunk(q[sl], k[sl], v[sl], gf[sl], sf[c], False)
                   + _gla_o_chunk(q[sl], k[sl], v[sl], gb[sl], sb[c], True))
    o = jnp.concatenate(out, axis=0)
    r = lax.broadcasted_iota(jnp.int32, (256, 256), 0)
    c = lax.broadcasted_iota(jnp.int32, (256, 256), 1)
    head_mean = jnp.where((r // 64) == (c // 64), 1.0 / 64.0, 0.0).astype(F32)
    ms = _dot(o * o, head_mean, precision=HI)
    on = o * lax.rsqrt(ms + EPS) * nwt
    return (on * (gr * _sigmoid(gr)),)


def _rope_partner(x):
    lane = lax.broadcasted_iota(jnp.int32, x.shape, 1)
    return jnp.where((lane // 16) % 2 == 0, pltpu.roll(x, LANES - 16, 1), pltpu.roll(x, 16, 1))


@jax.custom_vjp
def _rope(x, cs, sn):
    return x * cs + _rope_partner(x) * sn


def _rope_fwd(x, cs, sn):
    return _rope(x, cs, sn), (cs, sn)


def _rope_bwd(res, dy):
    cs, sn = res
    return dy * cs + _rope_partner(dy * sn), jnp.zeros_like(cs), jnp.zeros_like(sn)


_rope.defvjp(_rope_fwd, _rope_bwd)


def fn_mla_pre(blk, ckv, dq, kvw, qw):
    return _rms(ckv, kvw), _rms(dq, qw)


def fn_mla_post(blk, kk, qu, kr, cs, sn):
    kro = _rope(kr, cs, sn)
    kcat, q = [], []
    for h in range(4):
        kcat += [kk[:, 128 * h:128 * (h + 1)].astype(F32), kro]
        q += [qu[:, 256 * h:256 * h + 128], _rope(qu[:, 256 * h + 128:256 * (h + 1)], cs, sn)]
    return jnp.concatenate(kcat, axis=1), jnp.concatenate(q, axis=1)


def fn_delta(blk, do, o):
    out = []
    for h in range(4):
        s = jnp.sum(do[:, 128 * h:128 * (h + 1)] * o[:, 128 * h:128 * (h + 1)], axis=-1, keepdims=True)
        out.append(jnp.broadcast_to(s, (do.shape[0], 128)))
    return (jnp.concatenate(out, axis=1),)


SCORE_SCALE = MLA_SCALE * math.log2(math.e)
ATTN_ROWS = 256
NEG = -1e30


def _scores(q, k, k0, masked):
    s = _dot(q, k, ((1,), (1,))) * SCORE_SCALE
    if masked:
        col = k0 + lax.broadcasted_iota(jnp.int32, s.shape, 1)
        s = jnp.where(col >= CTX, NEG, s)
    return s


def flash_fwd(q, kcat, kvu):
    t = q.shape[0]
    tq = _pick(t, (768, 512, 256))
    tk = _pick(t, (2816, 1536, 768, 512, 256))
    nsub = tq // ATTN_ROWS

    def body(q_ref, k_ref, v_ref, o_ref, lse_ref, m_sc, l_sc, acc_sc):
        qi, ki = pl.program_id(1), pl.program_id(2)

        @pl.when(ki == 0)
        def _():
            m_sc[...] = jnp.full_like(m_sc, NEG)
            l_sc[...] = jnp.zeros_like(l_sc)
            acc_sc[...] = jnp.zeros_like(acc_sc)

        def step(first_tile):
            k, v = k_ref[...], v_ref[...]
            for r in range(nsub):
                rs = pl.ds(r * ATTN_ROWS, ATTN_ROWS)
                s = _scores(q_ref[rs, :], k, ki * tk, first_tile and r == 0)
                m_old = m_sc[rs, :]
                m_new = jnp.maximum(m_old, jnp.max(s, axis=-1, keepdims=True))
                alpha = jnp.exp2(m_old - m_new)
                p = jnp.exp2(s - m_new)
                l_sc[rs, :] = alpha * l_sc[rs, :] + jnp.sum(p, axis=-1, keepdims=True)
                acc_sc[rs, :] = alpha * acc_sc[rs, :] + _dot(p.astype(BF16), v)
                m_sc[rs, :] = m_new

        pl.when(qi == 0)(lambda: step(True))
        pl.when(qi != 0)(lambda: step(False))

        @pl.when(ki == pl.num_programs(2) - 1)
        def _():
            o_ref[...] = acc_sc[...] / l_sc[...]
            lse_ref[...] = jnp.broadcast_to(m_sc[...] + jnp.log2(l_sc[...]), lse_ref.shape)

    return pl.pallas_call(
        body, name="mla_flash_fwd", grid=(4, t // tq, t // tk),
        in_specs=[pl.BlockSpec((tq, 256), lambda h, i, j: (i, h)), pl.BlockSpec((tk, 256), lambda h, i, j: (j, h)),
                  pl.BlockSpec((tk, 128), lambda h, i, j: (j, 4 + h))],
        out_specs=[pl.BlockSpec((tq, 128), lambda h, i, j: (i, h)), pl.BlockSpec((tq, 128), lambda h, i, j: (i, h))],
        out_shape=[jax.ShapeDtypeStruct((t, 512), F32), jax.ShapeDtypeStruct((t, 512), F32)],
        scratch_shapes=[pltpu.VMEM((tq, 1), F32), pltpu.VMEM((tq, 1), F32), pltpu.VMEM((tq, 128), F32)],
        compiler_params=_params(("arbitrary", "arbitrary", "arbitrary")),
    )(q, kcat, kvu)


def flash_bwd(q, kcat, kvu, dy, lse, delta):
    t = q.shape[0]
    tq = tk = _pick(t, (768, 512, 256))
    nsub = tq // ATTN_ROWS

    def body(q_ref, k_ref, v_ref, do_ref, lse_ref, dl_ref, dq_ref, dk_ref, dv_ref):
        kj, qi = pl.program_id(1), pl.program_id(2)

        @pl.when(qi == 0)
        def _():
            dk_ref[...] = jnp.zeros_like(dk_ref)
            dv_ref[...] = jnp.zeros_like(dv_ref)

        def step(first_tile):
            k, v = k_ref[...], v_ref[...]
            for r in range(nsub):
                rs = pl.ds(r * ATTN_ROWS, ATTN_ROWS)
                qr, do = q_ref[rs, :], do_ref[rs, :].astype(BF16)
                s = _scores(qr, k, kj * tk, first_tile and r == 0)
                p = jnp.exp2(s - lse_ref[rs, 0:1])
                dp = _dot(do, v, ((1,), (1,)))
                ds = (p * (dp - dl_ref[rs, 0:1]) * MLA_SCALE).astype(BF16)
                dv_ref[...] += _dot(p.astype(BF16), do, ((0,), (0,)))
                dk_ref[...] += _dot(ds, qr, ((0,), (0,)))
                rows_ = pl.ds(pl.multiple_of(qi * tq + r * ATTN_ROWS, ATTN_ROWS), ATTN_ROWS)
                dqr = _dot(ds, k)

                @pl.when(kj == 0)
                def _():
                    dq_ref[rows_, :] = dqr

                @pl.when(kj != 0)
                def _():
                    dq_ref[rows_, :] += dqr

        pl.when(qi == 0)(lambda: step(True))
        pl.when(qi != 0)(lambda: step(False))

    return pl.pallas_call(
        body, name="mla_flash_bwd", grid=(4, t // tk, t // tq),
        in_specs=[pl.BlockSpec((tq, 256), lambda h, j, i: (i, h)), pl.BlockSpec((tk, 256), lambda h, j, i: (j, h)),
                  pl.BlockSpec((tk, 128), lambda h, j, i: (j, 4 + h)), pl.BlockSpec((tq, 128), lambda h, j, i: (i, 4 + h)),
                  pl.BlockSpec((tq, 128), lambda h, j, i: (i, h)), pl.BlockSpec((tq, 128), lambda h, j, i: (i, h))],
        out_specs=[pl.BlockSpec((t, 256), lambda h, j, i: (0, h)), pl.BlockSpec((tk, 256), lambda h, j, i: (j, h)),
                   pl.BlockSpec((tk, 128), lambda h, j, i: (j, h))],
        out_shape=[jax.ShapeDtypeStruct((t, 1024), F32), jax.ShapeDtypeStruct((t, 1024), F32),
                   jax.ShapeDtypeStruct((t, 512), F32)],
        compiler_params=_params(("arbitrary", "arbitrary", "arbitrary")),
    )(q, kcat, kvu, dy, lse, delta)


SCAN_BLOCK = CTX // GLA_CHUNK


def _scan_block(t, nb, rev):
    if not rev:
        return t
    return jnp.where(t < 1, 0, nb - t)


def _scan_order(rev):
    return tuple(reversed(range(SCAN_BLOCK))) if rev else tuple(range(SCAN_BLOCK))


def _diag(e):
    r = lax.broadcasted_iota(jnp.int32, (128, 128), 0)
    c = lax.broadcasted_iota(jnp.int32, (128, 128), 1)
    return jnp.where(r == c, jnp.broadcast_to(e, (128, 128)), 0.0)


def gla_states(uf, ef, ub, eb):
    nb = uf.shape[0] // SCAN_BLOCK

    def body(uf_ref, ef_ref, ub_ref, eb_ref, sf_ref, sb_ref, sf_sc, sb_sc):
        @pl.when(pl.program_id(0) == 0)
        def _():
            sf_sc[...] = jnp.zeros_like(sf_sc)
            sb_sc[...] = jnp.zeros_like(sb_sc)

        for u_ref, e_ref, s_ref, sc, rev in ((uf_ref, ef_ref, sf_ref, sf_sc, False), (ub_ref, eb_ref, sb_ref, sb_sc, True)):
            s = sc[...]
            for c in _scan_order(rev):
                s_ref[c] = s
                s = _dot(_diag(e_ref[c]), s, precision=HI) + u_ref[c]
            sc[...] = s

    big = lambda rev: pl.BlockSpec((SCAN_BLOCK, 128, 256), lambda t: (_scan_block(t, nb, rev), 0, 0))
    small = lambda rev: pl.BlockSpec((SCAN_BLOCK, 1, 128), lambda t: (_scan_block(t, nb, rev), 0, 0))
    return pl.pallas_call(
        body, name="gla_states", grid=(nb,),
        in_specs=[big(False), small(False), big(True), small(True)],
        out_specs=[big(False), big(True)],
        out_shape=[jax.ShapeDtypeStruct(uf.shape, F32)] * 2,
        scratch_shapes=[pltpu.VMEM((128, 256), F32)] * 2,
        compiler_params=_params(("arbitrary",)),
    )(uf, ef, ub, eb)


def gla_states_bwd(ef, eb, sf, sb, dsf, dsb):
    nb = ef.shape[0] // SCAN_BLOCK

    def body(ef_ref, eb_ref, sf_ref, sb_ref, dsf_ref, dsb_ref, duf_ref, def_ref, dub_ref, deb_ref, gf_sc, gb_sc):
        @pl.when(pl.program_id(0) == 0)
        def _():
            gf_sc[...] = jnp.zeros_like(gf_sc)
            gb_sc[...] = jnp.zeros_like(gb_sc)

        r = lax.broadcasted_iota(jnp.int32, (128, 128), 0)
        c = lax.broadcasted_iota(jnp.int32, (128, 128), 1)
        for e_ref, s_ref, ds_ref, du_ref, de_ref, g_sc, rev in ((ef_ref, sf_ref, dsf_ref, duf_ref, def_ref, gf_sc, False),
                                                                 (eb_ref, sb_ref, dsb_ref, dub_ref, deb_ref, gb_sc, True)):
            g = g_sc[...]
            for k in reversed(_scan_order(rev)):
                du_ref[k] = g
                m = _dot(s_ref[k], g, ((1,), (1,)), precision=HI)
                de_ref[k] = jnp.sum(jnp.where(r == c, m, 0.0), axis=0, keepdims=True)
                g = _dot(_diag(e_ref[k]), g, precision=HI) + ds_ref[k]
            g_sc[...] = g

    big = lambda rev: pl.BlockSpec((SCAN_BLOCK, 128, 256), lambda t: (_scan_block(nb - 1 - t, nb, rev), 0, 0))
    small = lambda rev: pl.BlockSpec((SCAN_BLOCK, 1, 128), lambda t: (_scan_block(nb - 1 - t, nb, rev), 0, 0))
    return pl.pallas_call(
        body, name="gla_states_bwd", grid=(nb,),
        in_specs=[small(False), small(True), big(False), big(True), big(False), big(True)],
        out_specs=[big(False), small(False), big(True), small(True)],
        out_shape=[jax.ShapeDtypeStruct(sf.shape, F32), jax.ShapeDtypeStruct(ef.shape, F32)] * 2,
        scratch_shapes=[pltpu.VMEM((128, 256), F32)] * 2,
        compiler_params=_params(("arbitrary",)),
    )(ef, eb, sf, sb, dsf, dsb)


def loss_head(xt, target, fnw):
    n = target.shape[0]

    def f(x, t, w):
        y = _rms(x, w)
        return 0.5 * jnp.sum(jnp.square(y - t)) * (1.0 / D)

    def body(x_ref, t_ref, w_ref, loss_ref, dx_ref, dw_ref):
        @pl.when(pl.program_id(0) == 0)
        def _():
            loss_ref[...] = jnp.zeros_like(loss_ref)
            dw_ref[...] = jnp.zeros_like(dw_ref)

        val, (dx, dw) = jax.value_and_grad(f, argnums=(0, 2))(x_ref[...], t_ref[...], w_ref[...])
        loss_ref[...] += jnp.broadcast_to(val, loss_ref.shape)
        dx_ref[...] = dx
        dw_ref[...] += dw

    return pl.pallas_call(
        body, name="loss_head", grid=(n // TM,),
        in_specs=[pl.BlockSpec((TM, D), lambda i: (i + NCTXB, 0)), pl.BlockSpec((TM, D), lambda i: (i, 0)),
                  pl.BlockSpec((1, D), lambda i: (0, 0))],
        out_specs=[pl.BlockSpec((1, 128), lambda i: (0, 0)), pl.BlockSpec((TM, D), lambda i: (i, 0)),
                   pl.BlockSpec((1, D), lambda i: (0, 0))],
        out_shape=[jax.ShapeDtypeStruct((1, 128), F32), jax.ShapeDtypeStruct((n, D), F32), jax.ShapeDtypeStruct((1, D), F32)],
        compiler_params=_params(("arbitrary",)),
    )(xt, target, fnw)


def _in_to_padded(w):
    out, pos = [], 0
    for src, wd, dst in sorted(IN_GROUPS, key=lambda g: g[2]):
        if dst > pos:
            out.append(jnp.zeros((w.shape[0], dst - pos), w.dtype))
        out.append(w[:, src:src + wd])
        pos = dst + wd
    if pos < P_COLS:
        out.append(jnp.zeros((w.shape[0], P_COLS - pos), w.dtype))
    return jnp.concatenate(out, axis=1)


def _in_from_padded(g):
    return jnp.concatenate([g[:, dst:dst + wd] for _, wd, dst in IN_GROUPS], axis=1)


def _uq_to_padded(w):
    return jnp.pad(w.reshape(256, 4, 192), ((0, 0), (0, 0), (0, 64))).reshape(256, 1024)


def _uq_from_padded(g):
    return g.reshape(256, 4, 256)[:, :, :192].reshape(256, 768)


def _ukv_to_padded(w):
    return w.reshape(256, 4, 2, 128).transpose(0, 2, 1, 3).reshape(256, 1024)


def _ukv_from_padded(g):
    return g.reshape(256, 2, 4, 128).transpose(0, 2, 1, 3).reshape(256, 1024)


def _rope_tables(n):
    pos = jnp.arange(n)
    freq = ROPE_BASE ** (-jnp.arange(16, dtype=F32) * 2.0 / 32.0)
    ar = (pos // GRID_W).astype(F32)[:, None] * freq[None, :]
    ac = (pos % GRID_W).astype(F32)[:, None] * freq[None, :]
    z = jnp.zeros((n, 64), F32)
    cs = jnp.concatenate([jnp.cos(ar), jnp.cos(ar), jnp.cos(ac), jnp.cos(ac), z], axis=1)
    sn = jnp.concatenate([-jnp.sin(ar), jnp.sin(ar), -jnp.sin(ac), jnp.sin(ac), z], axis=1)
    cs_c = jnp.concatenate([jnp.ones((CTX, 64), F32), jnp.zeros((CTX, 64), F32)], axis=1)
    return jnp.concatenate([cs_c, cs], axis=0), jnp.concatenate([jnp.zeros((CTX, 128), F32), sn], axis=0)


def _small_views(sp):
    wg = jnp.concatenate([jnp.pad(sp["gla_wg_fwd"], ((0, 112), (0, 0))), jnp.pad(sp["gla_wg_bwd"], ((16, 96), (0, 0)))], axis=1)
    return dict(
        n1w=sp["norm1_w"][None], n2w=sp["norm2_w"][None],
        sgu_nw=sp["sgu_norm_w"][None], sgu_nb=sp["sgu_norm_b"][None], sgu_w=sp["sgu_w"],
        sgu_bm=jnp.repeat(sp["sgu_b"].T, 64, axis=1),
        wg=wg, bg=jnp.concatenate([sp["gla_bg_fwd"], sp["gla_bg_bwd"]])[None],
        gla_nwt=jnp.tile(sp["gla_norm_w"], 4)[None],
        kvw=sp["mla_kv_norm_w"][None], qw=sp["mla_q_norm_w"][None])


def _small_grads(g):
    return dict(
        norm1_w=g["n1w"][0], norm2_w=g["n2w"][0],
        sgu_norm_w=g["sgu_nw"][0], sgu_norm_b=g["sgu_nb"][0], sgu_w=g["sgu_w"],
        sgu_b=g["sgu_bm"].reshape(128, 4, 64).sum(-1).T,
        gla_wg_fwd=g["wg"][0:16, 0:128], gla_wg_bwd=g["wg"][16:32, 128:256],
        gla_bg_fwd=g["bg"][0, 0:128], gla_bg_bwd=g["bg"][0, 128:256],
        gla_norm_w=g["gla_nwt"].reshape(4, 64).sum(0),
        mla_kv_norm_w=g["kvw"][0], mla_q_norm_w=g["qw"][0])


def _big_views(w_in, w_out, w_uq, w_ukv, w_ff1, w_ff2):
    win, wuq, wukv = _in_to_padded(w_in), _uq_to_padded(w_uq), _ukv_to_padded(w_ukv)
    return dict(win=win, win_t=win.T, wuq=wuq, wuq_t=wuq.T, wukv=wukv, wukv_t=wukv.T,
                wout=w_out, wout_t=w_out.T, w1=w_ff1, w1_t=w_ff1.T, w2=w_ff2, w2_t=w_ff2.T)


def _layer_ops(p, sv, a):
    pc = lambda off, w: rows(p, w, off // w)
    return dict(
        sgu=[pc(P_SU, 256), pc(P_SV, 256), const(sv["sgu_nw"]), const(sv["sgu_nb"]), const(sv["sgu_w"]), const(sv["sgu_bm"])],
        gates=[pc(P_GATE, 128), const(sv["wg"]), const(sv["bg"])],
        mla_pre=[pc(P_CKV, 256), pc(P_DQ, 256), const(sv["kvw"]), const(sv["qw"])],
        gla_kv=lambda: [pc(P_GK, 128), pc(P_GV, 256), rows(a["gf"]), rows(a["gb"])],
        gla_o=lambda: [pc(P_GQ, 128), pc(P_GK, 128), pc(P_GV, 256), rows(a["gf"]), rows(a["gb"]), pc(P_GR, 256),
                       chunks(a["sf"], TM // GLA_CHUNK), chunks(a["sb"], TM // GLA_CHUNK), const(sv["gla_nwt"])],
        mla_post=lambda: [rows(a["kvu"], 512, 0), rows(a["qu"]), pc(P_KR, 128), rows(a["cs"]), rows(a["sn"])])


def layer_fwd(l, xt, modl, bw, sv, tabs):
    t = xt.shape[0]
    g, nc, cpt = t // TM, t // GLA_CHUNK, TM // GLA_CHUNK
    nm = lambda s: f"l{l}_{s}"
    a = dict(x=xt, cs=tabs[0], sn=tabs[1])
    a["h"], = rw(nm("norm1"), fn_norm1, [rows(xt), const(modl), const(sv["n1w"])], [rowout(t, D, BF16)], g)
    p = a["p"] = mm(nm("in_proj"), a["h"], bw["win"], F32)
    ops = _layer_ops(p, sv, a)
    y_sgu, = rw(nm("sgu"), fn_sgu, ops["sgu"], [rowout(t, 256, BF16)], g)
    a["gf"], a["gb"] = rw(nm("gates"), fn_gates, ops["gates"], [rowout(t, 128, F32)] * 2, g)
    a["uf"], a["ef"], a["ub"], a["eb"] = rw(nm("gla_kv"), fn_gla_kv, ops["gla_kv"](),
                                           [chunkout((nc, 128, 256), F32, cpt), chunkout((nc, 1, 128), F32, cpt)] * 2, g)
    a["sf"], a["sb"] = gla_states(a["uf"], a["ef"], a["ub"], a["eb"])
    y_gla, = rw(nm("gla_o"), fn_gla_o, ops["gla_o"](), [rowout(t, 256, BF16)], g)
    a["ckvn"], a["dqn"] = rw(nm("mla_pre"), fn_mla_pre, ops["mla_pre"], [rowout(t, 256, BF16)] * 2, g)
    a["kvu"] = mm(nm("kv_up"), a["ckvn"], bw["wukv"], BF16)
    a["qu"] = mm(nm("q_up"), a["dqn"], bw["wuq"], F32)
    a["kcat"], a["q"] = rw(nm("mla_post"), fn_mla_post, ops["mla_post"](), [rowout(t, 1024, BF16)] * 2, g)
    a["o"], a["lse"] = flash_fwd(a["q"], a["kcat"], a["kvu"])
    a["y"] = jnp.concatenate([y_sgu, y_gla, a["o"].astype(BF16)], axis=1)
    a["yo"] = mm(nm("out_proj"), a["y"], bw["wout"], F32)
    a["x1"], a["h2"] = rw(nm("res_norm2"), fn_res_norm2, [rows(xt), rows(a["yo"]), const(modl), const(sv["n2w"])],
                          [rowout(t, D, F32), rowout(t, D, BF16)], g)
    a["act"] = mm(nm("ff1"), a["h2"], bw["w1"], BF16, post=lambda acc: jnp.maximum(acc, 0.0))
    a["f"] = mm(nm("ff2"), a["act"], bw["w2"], F32, pre=_square_bf16)
    x2, = rw(nm("res2"), fn_res2, [rows(a["x1"]), rows(a["f"]), const(modl)], [rowout(t, D, F32)], g)
    return x2, a


def fn_assemble(blk, gv1, gv2, ckv, su, sv_, gr, dq, gk1, gk2, pg, kr, gq):
    return (jnp.concatenate([gv1 + gv2, ckv, su, sv_, gr, dq, gk1 + gk2, pg, kr, gq], axis=1),)


def layer_bwd(l, dx2, a, modl, bw, sv):
    t = dx2.shape[0]
    g, nc, cpt = t // TM, t // GLA_CHUNK, TM // GLA_CHUNK
    nm = lambda s: f"l{l}_{s}_bwd"
    p = a["p"]
    ops = _layer_ops(p, sv, a)
    gw, gs = {}, {}
    dx1a, df, dm_a = rw_vjp(nm("res2"), fn_res2, [rows(a["x1"]), rows(a["f"]), const(modl)], [rows(dx2)], [0, 1, 2], g,
                            gdt=[F32, BF16, F32])
    gw["w2"] = mm_tn(nm("ff2_w"), a["act"], df, pre=_square_bf16)
    du = mm(nm("ff2_x"), df, bw["w2_t"], BF16, post=lambda acc, act: acc * (2.0 * act.astype(F32)), extras=(a["act"],))
    gw["w1"] = mm_tn(nm("ff1_w"), a["h2"], du)
    dh2 = mm(nm("ff1_x"), du, bw["w1_t"], F32)
    dxa, dyo, dm_b, gs["n2w"] = rw_vjp(nm("res_norm2"), fn_res_norm2,
                                       [rows(a["x"]), rows(a["yo"]), const(modl), const(sv["n2w"])],
                                       [rows(dx1a), rows(dh2)], [0, 1, 2, 3], g, gdt=[F32, BF16, F32, F32])
    gw["wout"] = mm_tn(nm("out_w"), a["y"], dyo)
    dy = mm(nm("out_x"), dyo, bw["wout_t"], F32)
    dsu, dsv, gs["sgu_nw"], gs["sgu_nb"], gs["sgu_w"], gs["sgu_bm"] = rw_vjp(
        nm("sgu"), fn_sgu, ops["sgu"], [rows(dy, 256, 0)], [0, 1, 2, 3, 4, 5], g)
    dgq, dgk1, dgv1, dgf1, dgb1, dgr, dsf, dsb, gs["gla_nwt"] = rw_vjp(
        nm("gla_o"), fn_gla_o, ops["gla_o"](), [rows(dy, 256, 1)], list(range(9)), g)
    duf, def_, dub, deb = gla_states_bwd(a["ef"], a["eb"], a["sf"], a["sb"], dsf, dsb)
    dgk2, dgv2, dgf, dgb = rw_vjp(nm("gla_kv"), fn_gla_kv, ops["gla_kv"](),
                                  [chunks(duf, cpt), chunks(def_, cpt), chunks(dub, cpt), chunks(deb, cpt)], [0, 1, 2, 3], g,
                                  adds={2: rows(dgf1), 3: rows(dgb1)})
    dpg, gs["wg"], gs["bg"] = rw_vjp(nm("gates"), fn_gates, ops["gates"], [rows(dgf), rows(dgb)], [0, 1, 2], g)
    delta, = rw(nm("attn_delta"), fn_delta, [rows(dy, 512, 1), rows(a["o"])], [rowout(t, 512, F32)], g)
    dq, dkcat, dv = flash_bwd(a["q"], a["kcat"], a["kvu"], dy, a["lse"], delta)
    dkk, dqu, dkr = rw_vjp(nm("mla_post"), fn_mla_post, ops["mla_post"](), [rows(dkcat), rows(dq)], [0, 1, 2], g,
                           gdt=[BF16, BF16, F32])
    dkvu = jnp.concatenate([dkk, dv.astype(BF16)], axis=1)
    gw["wukv"] = mm_tn(nm("kv_up_w"), a["ckvn"], dkvu)
    gw["wuq"] = mm_tn(nm("q_up_w"), a["dqn"], dqu)
    dckvn = mm(nm("kv_up_x"), dkvu, bw["wukv_t"], F32)
    ddqn = mm(nm("q_up_x"), dqu, bw["wuq_t"], F32)
    dckv, ddq, gs["kvw"], gs["qw"] = rw_vjp(nm("mla_pre"), fn_mla_pre, ops["mla_pre"], [rows(dckvn), rows(ddqn)],
                                            [0, 1, 2, 3], g)
    dp, = rw(nm("assemble"), fn_assemble,
             [rows(x_) for x_ in (dgv1, dgv2, dckv, dsu, dsv, dgr, ddq, dgk1, dgk2, dpg, dkr, dgq)],
             [rowout(t, P_COLS, BF16)], g)
    gw["win"] = mm_tn(nm("in_w"), a["h"], dp)
    dh = mm(nm("in_x"), dp, bw["win_t"], F32)
    dx, dm_c, gs["n1w"] = rw_vjp(nm("norm1"), fn_norm1, [rows(a["x"]), const(modl), const(sv["n1w"])], [rows(dh)],
                                 [0, 1, 2], g, adds={0: rows(dxa)})
    big = dict(w_in=_in_from_padded(gw["win"]), w_out=gw["wout"], mla_w_uq=_uq_from_padded(gw["wuq"]),
               mla_w_ukv=_ukv_from_padded(gw["wukv"]), w_ff1=gw["w1"], w_ff2=gw["w2"])
    return dx, dm_a + dm_b + dm_c, big, _small_grads(gs)


SMALL_NAMES = ("norm1_w", "sgu_norm_w", "sgu_norm_b", "sgu_w", "sgu_b", "gla_wg_fwd", "gla_bg_fwd", "gla_wg_bwd",
               "gla_bg_bwd", "gla_norm_w", "mla_q_norm_w", "mla_kv_norm_w", "norm2_w")
BIG_NAMES = ("w_in", "w_out", "mla_w_uq", "mla_w_ukv", "w_ff1", "w_ff2")


def local_step(x, ctx, target, mods, big, small, final_norm_w):
    n = x.shape[0]
    xt = jnp.concatenate([ctx, x], axis=0)
    tabs = _rope_tables(n)
    depth = len(mods)
    bws = [_big_views(*[big[l][k] for k in BIG_NAMES]) for l in range(depth)]
    svs = [_small_views(small[l]) for l in range(depth)]
    acts = []
    for l in range(depth):
        xt, a = layer_fwd(l, xt, mods[l], bws[l], svs[l], tabs)
        acts.append(a)
    loss, dxl, dfnw = loss_head(xt, target, final_norm_w[None])
    dxt = jnp.concatenate([jnp.zeros((CTX, D), F32), dxl], axis=0)
    dmods, gbig, gsmall = [None] * depth, [None] * depth, [None] * depth
    for l in reversed(range(depth)):
        dxt, dmods[l], gbig[l], gsmall[l] = layer_bwd(l, dxt, acts[l], mods[l], bws[l], svs[l])
    return loss, dxt[CTX:], dmods, gbig, gsmall, dfnw


def _group(group):
    x, y, c = lax.axis_index("x"), lax.axis_index("y"), lax.axis_index("c")
    if group == "sib":
        return 2, c, [((x, y, 1 - c), 1 - c)]
    if group == "chip":
        flips = [(1, 0), (0, 1), (1, 1)]
        return 4, 2 * x + y, [((x ^ fx, y ^ fy, c), 2 * (x ^ fx) + (y ^ fy)) for fx, fy in flips]
    flips = [(fx, fy, fc) for fx in (0, 1) for fy in (0, 1) for fc in (0, 1)][1:]
    return 8, 4 * x + 2 * y + c, [((x ^ fx, y ^ fy, c ^ fc), 4 * (x ^ fx) + 2 * (y ^ fy) + (c ^ fc)) for fx, fy, fc in flips]


def _group_size(group):
    return {"sib": 2, "chip": 4, "all": 8}[group]


REMOTE_COPIES = {"gather": None, "scatter": None, "swap": 1, "gather2": 6}


def xchg(name, entries):
    n_in = sum(len(arrs) for _, _, arrs in entries)
    n_remote = sum(REMOTE_COPIES[k] or _group_size(g) - 1 for k, g, _ in entries)
    n_local = sum(1 for k, _, _ in entries if k != "swap")
    out_shape = []
    for kind, group, arrs in entries:
        a = arrs[0]
        if kind in ("gather", "gather2"):
            out_shape.append(jax.ShapeDtypeStruct((_group_size(group),) + a.shape, a.dtype))
        else:
            out_shape.append(jax.ShapeDtypeStruct(a.shape, a.dtype))

    def body(*refs):
        in_refs, out_refs = refs[:n_in], refs[n_in:n_in + len(entries)]
        send_sems, recv_sems, local_sems = refs[n_in + len(entries):]
        x, y, c = lax.axis_index("x"), lax.axis_index("y"), lax.axis_index("c")

        def remote(src, dst, k, dev):
            return pltpu.make_async_remote_copy(src_ref=src, dst_ref=dst, send_sem=send_sems.at[k], recv_sem=recv_sems.at[k],
                                                device_id=dev, device_id_type=MESH)

        pos, k, kl = 0, 0, 0
        forwards, finals = [], []
        for (kind, group, arrs), out in zip(entries, out_refs):
            srcs = in_refs[pos:pos + len(arrs)]
            pos += len(arrs)
            _, mine, peers = _group(group)
            if kind == "swap":
                (dev, _), = peers
                for core, src in ((0, srcs[1]), (1, srcs[0])):
                    @pl.when(c == core)
                    def _(src=src, k=k, dev=dev, out=out):
                        remote(src, out, k, dev).start()
                finals.append(remote(srcs[0], out, k, dev).wait)
                k += 1
                continue
            src = srcs[0]
            own = pltpu.make_async_copy(src if kind != "scatter" else src.at[mine], out.at[mine], local_sems.at[kl])
            own.start()
            finals.append(own.wait)
            kl += 1
            if kind == "gather2":
                sibling = (x, y, 1 - c)
                for f, (dev, slot) in enumerate(peers):
                    remote(src.at[c], out.at[mine, c], k + f, dev).start()
                    arrival = remote(src.at[c], out.at[slot, c], k + f, dev)

                    def forward(arrival=arrival, slot=slot, kf=k + 3 + f, out=out):
                        arrival.wait_recv()
                        remote(out.at[slot, c], out.at[slot, c], kf, sibling).start()

                    forwards.append(forward)
                    finals.append(arrival.wait_send)
                    finals.append(remote(out.at[slot, c], out.at[slot, 1 - c], k + 3 + f, sibling).wait)
                k += 6
                continue
            for dev, slot in peers:
                piece = src if kind == "gather" else src.at[slot]
                remote(piece, out.at[mine], k, dev).start()
                finals.append(remote(piece, out.at[slot], k, dev).wait)
                k += 1
        for run in forwards + finals:
            run()

    any_spec = pl.BlockSpec(memory_space=pl.ANY)
    return pl.pallas_call(
        body, name=name,
        in_specs=[any_spec] * n_in, out_specs=[any_spec] * len(entries), out_shape=out_shape,
        scratch_shapes=[pltpu.SemaphoreType.DMA((n_remote,)), pltpu.SemaphoreType.DMA((n_remote,)),
                        pltpu.SemaphoreType.DMA((max(n_local, 1),))],
    )(*[a for _, _, arrs in entries for a in arrs])


def _block_rows(r, c, budget=131072):
    tr = 8
    while tr * 2 * c <= budget and r % (tr * 2) == 0:
        tr *= 2
    return tr if r % tr == 0 else r


def tree_sum(name, parts):
    g, r, c = parts.shape
    tr = _block_rows(r, c)

    def body(p_ref, o_ref):
        p = [p_ref[i].astype(F32) for i in range(g)]
        while len(p) > 1:
            p = [p[i] + p[i + 1] for i in range(0, len(p), 2)]
        o_ref[...] = p[0]

    return pl.pallas_call(
        body, name=name, grid=(r // tr,),
        in_specs=[pl.BlockSpec((g, tr, c), lambda i: (0, i, 0))], out_specs=pl.BlockSpec((tr, c), lambda i: (i, 0)),
        out_shape=jax.ShapeDtypeStruct((r, c), F32), compiler_params=_params(("arbitrary",)),
    )(parts)


def pair_sum(name, g0, g1, recv, core):
    r, c = recv.shape
    tr = _block_rows(r, c)

    def body(a_ref, b_ref, r_ref, k_ref, o_ref):
        o_ref[...] = jnp.where(k_ref[...] > 0.5, b_ref[...], a_ref[...]) + r_ref[...]

    blk = pl.BlockSpec((tr, c), lambda i: (i, 0))
    return pl.pallas_call(
        body, name=name, grid=(r // tr,), in_specs=[blk, blk, blk, pl.BlockSpec((1, 1), lambda i: (0, 0))],
        out_specs=blk, out_shape=jax.ShapeDtypeStruct((r, c), F32), compiler_params=_params(("arbitrary",)),
    )(g0, g1, recv, core)


def adamw(name, w, g, m, v):
    r, c = w.shape
    tr = _block_rows(r, c)

    def body(w_ref, g_ref, m_ref, v_ref, d_ref, nm_ref, nv_ref):
        gg = g_ref[...]
        nm = ADAM_B1 * m_ref[...] + (1.0 - ADAM_B1) * gg
        nv = ADAM_B2 * v_ref[...] + (1.0 - ADAM_B2) * jnp.square(gg)
        m_hat = nm / (1.0 - ADAM_B1 ** ADAM_STEP)
        v_hat = nv / (1.0 - ADAM_B2 ** ADAM_STEP)
        d_ref[...] = -ADAM_LR * (m_hat / (jnp.sqrt(v_hat) + ADAM_EPS) + ADAM_WD * w_ref[...])
        nm_ref[...] = nm
        nv_ref[...] = nv

    blk = pl.BlockSpec((tr, c), lambda i: (i, 0))
    return pl.pallas_call(
        body, name=name, grid=(r // tr,), in_specs=[blk] * 4, out_specs=[blk] * 3,
        out_shape=[jax.ShapeDtypeStruct((r, c), F32)] * 3, compiler_params=_params(("arbitrary",)),
    )(w, g, m, v)


W_MOD_COLS = 6 * D // 4
MOD_TN = 512


def mod_project(c16, w_mod, b_loc):
    def body(c_ref, w_ref, b_ref, o_ref):
        cv = c_ref[...]
        s = (cv * _sigmoid(cv)).astype(BF16)
        o_ref[0] = _dot(s, w_ref[0].astype(BF16)) + b_ref[0]

    return pl.pallas_call(
        body, name="mod_project", grid=(2, W_MOD_COLS // MOD_TN),
        in_specs=[pl.BlockSpec((16, D), lambda l, j: (0, 0)), pl.BlockSpec((1, D, MOD_TN), lambda l, j: (l, 0, j)),
                  pl.BlockSpec((1, 1, MOD_TN), lambda l, j: (l, 0, j))],
        out_specs=pl.BlockSpec((1, 16, MOD_TN), lambda l, j: (l, 0, j)),
        out_shape=jax.ShapeDtypeStruct((2, 16, W_MOD_COLS), F32), compiler_params=_params(("arbitrary", "arbitrary")),
    )(c16, w_mod, b_loc)


def mod_weight_grad(c16, dm16):
    def body(c_ref, d_ref, o_ref):
        cv = c_ref[...]
        o_ref[0] = _dot(cv * _sigmoid(cv), d_ref[0], ((0,), (0,)), precision=HI)

    return pl.pallas_call(
        body, name="mod_weight_grad", grid=(2, W_MOD_COLS // MOD_TN),
        in_specs=[pl.BlockSpec((16, D), lambda l, j: (0, 0)), pl.BlockSpec((1, 16, MOD_TN), lambda l, j: (l, 0, j))],
        out_specs=pl.BlockSpec((1, D, MOD_TN), lambda l, j: (l, 0, j)),
        out_shape=jax.ShapeDtypeStruct((2, D, W_MOD_COLS), F32), compiler_params=_params(("arbitrary", "arbitrary")),
    )(c16, dm16)


def cctx_partial(dmc, w_mod):
    def body(d_ref, w_ref, o_ref):
        @pl.when(pl.program_id(0) == 0)
        def _():
            o_ref[...] = jnp.zeros_like(o_ref)
        o_ref[...] += _dot(d_ref[0], w_ref[0], ((1,), (1,)), precision=HI)

    return pl.pallas_call(
        body, name="cctx_partial", grid=(2,),
        in_specs=[pl.BlockSpec((1, 8, W_MOD_COLS), lambda l: (l, 0, 0)), pl.BlockSpec((1, D, W_MOD_COLS), lambda l: (l, 0, 0))],
        out_specs=pl.BlockSpec((8, D), lambda l: (0, 0)),
        out_shape=jax.ShapeDtypeStruct((8, D), F32), compiler_params=_params(("arbitrary",)),
    )(dmc, w_mod)


def cctx_grad(parts, c_ctx8):
    def body(p_ref, c_ref, o_ref):
        ds = (p_ref[0] + p_ref[1]) + (p_ref[2] + p_ref[3])
        _, vf = jax.vjp(lambda z: z * _sigmoid(z), c_ref[...])
        o_ref[...] = vf(ds)[0]

    return pl.pallas_call(
        body, name="cctx_grad", out_shape=jax.ShapeDtypeStruct((8, D), F32),
    )(parts, c_ctx8)


ARG_NAMES = ("x", "c", "ctx", "c_ctx", "w_mod", "b_mod", "norm1_w", "w_in", "w_out", "sgu_norm_w", "sgu_norm_b", "sgu_w",
             "sgu_b", "gla_wg_fwd", "gla_bg_fwd", "gla_wg_bwd", "gla_bg_bwd", "gla_norm_w", "mla_q_norm_w", "mla_w_uq",
             "mla_kv_norm_w", "mla_w_ukv", "norm2_w", "w_ff1", "w_ff2", "final_norm_w")
WEIGHT_NAMES = ARG_NAMES[3:]
PACKED = ("c_ctx", "b_mod") + SMALL_NAMES + ("final_norm_w",)
ROW_SHARDED = ("w_out", "w_ff2")
PACK_ROWS = 256


def _pack(vectors):
    flat = jnp.concatenate([v.reshape(-1) for v in vectors])
    n = flat.shape[0]
    total = -(-n // (PACK_ROWS * LANES)) * PACK_ROWS * LANES
    return jnp.pad(flat, (0, total - n)).reshape(-1, LANES)


def _unpack(buf, shapes):
    flat, out, pos = buf.reshape(-1), [], 0
    for shp in shapes:
        n = int(np.prod(shp))
        out.append(flat[pos:pos + n].reshape(shp))
        pos += n
    return out


def _full_weight(name, gathered, l):
    g = gathered[:, l]
    if name in ROW_SHARDED:
        return g.reshape(-1, g.shape[-1])
    return g.transpose(1, 0, 2).reshape(g.shape[1], -1)


def _chip_chunks(name, a):
    if name in ROW_SHARDED:
        return a.reshape(4, a.shape[0] // 4, a.shape[1])
    return a.reshape(a.shape[0], 4, a.shape[1] // 4).transpose(1, 0, 2)


def kernel(x, c, ctx, c_ctx, w_mod, b_mod, norm1_w, w_in, w_out, sgu_norm_w, sgu_norm_b, sgu_w, sgu_b, gla_wg_fwd, gla_bg_fwd, gla_wg_bwd, gla_bg_bwd, gla_norm_w, mla_q_norm_w, mla_w_uq, mla_kv_norm_w, mla_w_ukv, norm2_w, w_ff1, w_ff2, final_norm_w, loss_target, m_c_ctx, m_w_mod, m_b_mod, m_norm1_w, m_w_in, m_w_out, m_sgu_norm_w, m_sgu_norm_b, m_sgu_w, m_sgu_b, m_gla_wg_fwd, m_gla_bg_fwd, m_gla_wg_bwd, m_gla_bg_bwd, m_gla_norm_w, m_mla_q_norm_w, m_mla_w_uq, m_mla_kv_norm_w, m_mla_w_ukv, m_norm2_w, m_w_ff1, m_w_ff2, m_final_norm_w, v_c_ctx, v_w_mod, v_b_mod, v_norm1_w, v_w_in, v_w_out, v_sgu_norm_w, v_sgu_norm_b, v_sgu_w, v_sgu_b, v_gla_wg_fwd, v_gla_bg_fwd, v_gla_wg_bwd, v_gla_bg_bwd, v_gla_norm_w, v_mla_q_norm_w, v_mla_w_uq, v_mla_kv_norm_w, v_mla_w_ukv, v_norm2_w, v_w_ff1, v_w_ff2, v_final_norm_w):
    args = (x, c, ctx, c_ctx, w_mod, b_mod, norm1_w, w_in, w_out, sgu_norm_w, sgu_norm_b, sgu_w, sgu_b, gla_wg_fwd, gla_bg_fwd, gla_wg_bwd, gla_bg_bwd, gla_norm_w, mla_q_norm_w, mla_w_uq, mla_kv_norm_w, mla_w_ukv, norm2_w, w_ff1, w_ff2, final_norm_w)
    w = dict(zip(ARG_NAMES, args))
    moms = (m_c_ctx, m_w_mod, m_b_mod, m_norm1_w, m_w_in, m_w_out, m_sgu_norm_w, m_sgu_norm_b, m_sgu_w, m_sgu_b, m_gla_wg_fwd, m_gla_bg_fwd, m_gla_wg_bwd, m_gla_bg_bwd, m_gla_norm_w, m_mla_q_norm_w, m_mla_w_uq, m_mla_kv_norm_w, m_mla_w_ukv, m_norm2_w, m_w_ff1, m_w_ff2, m_final_norm_w)
    vars_ = (v_c_ctx, v_w_mod, v_b_mod, v_norm1_w, v_w_in, v_w_out, v_sgu_norm_w, v_sgu_norm_b, v_sgu_w, v_sgu_b, v_gla_wg_fwd, v_gla_bg_fwd, v_gla_wg_bwd, v_gla_bg_bwd, v_gla_norm_w, v_mla_q_norm_w, v_mla_w_uq, v_mla_kv_norm_w, v_mla_w_ukv, v_norm2_w, v_w_ff1, v_w_ff2, v_final_norm_w)
    m1 = dict(zip(WEIGHT_NAMES, moms))
    m2 = dict(zip(WEIGHT_NAMES, vars_))
    xi, yi, ci = lax.axis_index("x"), lax.axis_index("y"), lax.axis_index("c")
    chip, dev = 2 * xi + yi, 4 * xi + 2 * yi + ci
    depth = w_mod.shape[0]

    got = xchg("gather_inputs", [("gather", "all", [c])] + [("gather2", "chip", [w[k].astype(BF16)]) for k in BIG_NAMES])
    c_all, shards = got[0], dict(zip(BIG_NAMES, got[1:]))
    c16 = jnp.concatenate([c_all.reshape(8, D), c_ctx[None], jnp.zeros((7, D), F32)], axis=0)
    b_loc = lax.dynamic_slice_in_dim(b_mod, chip * W_MOD_COLS, W_MOD_COLS, axis=1)[:, None, :]
    mod_part = mod_project(c16, w_mod, b_loc)
    mod_all, = xchg("gather_mod", [("gather", "chip", [mod_part])])
    mod_full = mod_all.transpose(1, 2, 0, 3).reshape(depth, 16, 6 * D)
    mods = [jnp.stack([mod_full[l, 8], lax.dynamic_index_in_dim(mod_full[l], dev, 0, keepdims=False)])[:, None, :]
            for l in range(depth)]

    big = [{k: _full_weight(k, shards[k], l) for k in BIG_NAMES} for l in range(depth)]
    small = [{k: w[k][l] for k in SMALL_NAMES} for l in range(depth)]
    loss, grad_x, dmods, gbig, gsmall, dfnw = local_step(x[0], ctx[0], loss_target[0], mods, big, small, final_norm_w)
    loss = lax.psum(loss[0, 0], ("x", "y", "c"))

    dm_lat = jnp.stack([dmods[l][1, 0] for l in range(depth)])
    dm_ctx = jnp.stack([dmods[l][0, 0] for l in range(depth)])
    small_pack = _pack([dm_lat, dm_ctx] + [jnp.stack([gsmall[l][k] for l in range(depth)]) for k in SMALL_NAMES] + [dfnw])
    got = xchg("exchange_grads", [("gather", "all", [small_pack])] + [("swap", "sib", [gbig[0][k], gbig[1][k]]) for k in BIG_NAMES])
    small_all, from_sib = got[0], dict(zip(BIG_NAMES, got[1:]))
    small_sum = tree_sum("small_grad_sum", small_all)
    core = ci.astype(F32).reshape(1, 1)
    mine = {k: pair_sum(f"pair_sum_{k}", gbig[0][k], gbig[1][k], from_sib[k], core) for k in BIG_NAMES}

    n_dm = depth * 6 * D
    dm_rows = n_dm // LANES
    dm_lat_all = small_all[:, :dm_rows].reshape(8, depth, 6 * D)
    dm_ctx_sum = small_sum[dm_rows:2 * dm_rows].reshape(depth, 6 * D)
    take = lambda a: lax.dynamic_slice_in_dim(a, chip * W_MOD_COLS, W_MOD_COLS, axis=-1)
    dmc_loc = take(dm_ctx_sum)
    cc_part = cctx_partial(jnp.pad(dmc_loc[:, None, :], ((0, 0), (0, 7), (0, 0))), w_mod)
    got = xchg("scatter_grads", [("gather", "chip", [cc_part])]
               + [("scatter", "chip", [_chip_chunks(k, mine[k]).astype(BF16)]) for k in BIG_NAMES])
    cc_parts, chunks_in = got[0], dict(zip(BIG_NAMES, got[1:]))
    reduced = {k: tree_sum(f"chip_sum_{k}", chunks_in[k]) for k in BIG_NAMES}
    g_c_ctx = cctx_grad(cc_parts, jnp.broadcast_to(c_ctx[None], (8, D)))[0]

    got = xchg("share_layers", [("swap", "sib", [reduced[k], reduced[k]]) for k in BIG_NAMES])
    grads = {k: jnp.where(ci == 0, jnp.stack([reduced[k], r]), jnp.stack([r, reduced[k]])) for k, r in zip(BIG_NAMES, got)}

    dm16 = jnp.concatenate([take(dm_lat_all).transpose(1, 0, 2), dmc_loc[:, None, :], jnp.zeros((depth, 7, W_MOD_COLS), F32)], axis=1)
    grads["w_mod"] = mod_weight_grad(c16, dm16)
    flat_sum = small_sum.reshape(-1)
    g_b_mod = (flat_sum[:n_dm] + flat_sum[n_dm:2 * n_dm]).reshape(depth, 6 * D)
    rest_shapes = [w[k].shape for k in PACKED[2:]]
    n_rest = sum(int(np.prod(s)) for s in rest_shapes)
    for k, g in zip(PACKED, [g_c_ctx, g_b_mod] + _unpack(flat_sum[2 * n_dm:2 * n_dm + n_rest], rest_shapes)):
        grads[k] = g

    delta, new_m, new_v = {}, {}, {}
    for k in BIG_NAMES + ("w_mod",):
        view = lambda a: a.reshape(-1, a.shape[-1])
        d_, m_, v_ = adamw(f"adamw_{k}", view(w[k]), view(grads[k]), view(m1[k]), view(m2[k]))
        delta[k], new_m[k], new_v[k] = d_.reshape(w[k].shape), m_.reshape(w[k].shape), v_.reshape(w[k].shape)
    shapes = [w[k].shape for k in PACKED]
    d_, m_, v_ = adamw("adamw_small", _pack([w[k] for k in PACKED]), _pack([grads[k] for k in PACKED]),
                       _pack([m1[k] for k in PACKED]), _pack([m2[k] for k in PACKED]))
    for k, dk, mk, vk in zip(PACKED, _unpack(d_, shapes), _unpack(m_, shapes), _unpack(v_, shapes)):
        delta[k], new_m[k], new_v[k] = dk, mk, vk
    return (loss, grad_x[None], *[grads[k] for k in WEIGHT_NAMES], *[delta[k] for k in WEIGHT_NAMES],
            *[new_m[k] for k in WEIGHT_NAMES], *[new_v[k] for k in WEIGHT_NAMES])
```

```python
import functools
import math

import numpy as np
import jax
import jax.numpy as jnp
from jax import lax
from jax.experimental import pallas as pl
from jax.experimental.pallas import tpu as pltpu

F32 = jnp.float32
BF16 = jnp.bfloat16
HI = lax.Precision.HIGHEST
EPS = 1e-6
VMEM_LIMIT_BYTES = 56 * 1024 * 1024
LANES = 128

D = 1024
D_FF = 4096
CTX = 256
GRID_W = 64
SGU_CHUNK = 128
GLA_CHUNK = 64
GLA_TAU = 16.0
GLA_DK = 32
MLA_SCALE = (128 + 64) ** -0.5
SCORE_SCALE = MLA_SCALE * math.log2(math.e)
LN2 = math.log(2.0)
ROPE_BASE = 10000.0
TM = 256
NCTXB = CTX // TM
P_GV, P_CKV, P_SU, P_SV, P_GR, P_DQ, P_GK, P_GATE, P_KR, P_GQ = 0, 256, 512, 768, 1024, 1280, 1536, 1664, 1792, 1920
P_COLS = 2048
IN_GROUPS = ((0, 128, P_GK), (128, 256, P_GV), (384, 32, P_GATE), (416, 256, P_CKV), (672, 64, P_KR),
             (736, 256, P_SU), (992, 256, P_SV), (1248, 128, P_GQ), (1376, 256, P_GR), (1632, 256, P_DQ))
ADAM_LR, ADAM_B1, ADAM_B2, ADAM_EPS, ADAM_WD, ADAM_STEP = 0.001, 0.9, 0.999, 1e-08, 0.01, 10
MESH = pl.DeviceIdType.MESH


def _params(sem):
    return pltpu.CompilerParams(dimension_semantics=sem, vmem_limit_bytes=VMEM_LIMIT_BYTES)


def _pick(n, cands):
    for c in cands:
        if n % c == 0:
            return c
    return n


class Op:
    def __init__(self, arr, blk, idx, gshape, gidx, acc):
        self.arr, self.blk, self.idx, self.gshape, self.gidx, self.acc = arr, blk, idx, gshape, gidx, acc

    def spec(self):
        return pl.BlockSpec(self.blk, self.idx)


def rows(arr, width=None, cb=0, off=0, tm=TM):
    w = arr.shape[1] if width is None else width
    n = arr.shape[0] - off * tm
    return Op(arr, (tm, w), lambda i: (i + off, cb), (n, w), lambda i: (i, 0), False)


def chunks(arr, per_tile):
    z = (0,) * (arr.ndim - 1)
    return Op(arr, (per_tile,) + arr.shape[1:], lambda i: (i,) + z, arr.shape, lambda i: (i,) + z, False)


def const(arr):
    z = (0,) * arr.ndim
    return Op(arr, arr.shape, lambda i: z, arr.shape, lambda i: z, True)


def rw(name, fn, ins, outs, grid):
    nin = len(ins)

    def body(*refs):
        vals = [r[...] for r in refs[:nin]]
        res = fn(pl.program_id(0), *vals)
        for o, r in zip(refs[nin:], res):
            o[...] = r.astype(o.dtype)

    return pl.pallas_call(
        body, name=name, grid=(grid,),
        in_specs=[o.spec() for o in ins],
        out_specs=[pl.BlockSpec(b, ix) for (_, _, b, ix) in outs],
        out_shape=[jax.ShapeDtypeStruct(s, d) for (s, d, _, _) in outs],
        compiler_params=_params(("arbitrary",)),
    )(*[o.arr for o in ins])


def rowout(n, w, dtype, tm=TM):
    return ((n, w), dtype, (tm, w), lambda i: (i, 0))


def chunkout(shape, dtype, per_tile):
    z = (0,) * (len(shape) - 1)
    return (shape, dtype, (per_tile,) + tuple(shape[1:]), lambda i: (i,) + z)


def rw_vjp(name, fn, ins, cots, wrt, grid, gdt=None, adds=None):
    nin = len(ins)
    cot_ops = [c for c in cots if c is not None]
    add_items = sorted((adds or {}).items())
    gdt = gdt or [F32] * len(wrt)
    ncot, nadd = len(cot_ops), len(add_items)

    def body(*refs):
        i = pl.program_id(0)
        vals = [r[...] for r in refs[:nin]]
        cvals = [r[...] for r in refs[nin:nin + ncot]]
        avals = [r[...] for r in refs[nin + ncot:nin + ncot + nadd]]
        grefs = refs[nin + ncot + nadd:]

        def f(*d):
            a = list(vals)
            for k, dv in zip(wrt, d):
                a[k] = dv
            return tuple(fn(i, *a))

        outs, vf = jax.vjp(f, *[vals[k] for k in wrt])
        it = iter(cvals)
        ct = tuple(jnp.zeros_like(o) if c is None else next(it).astype(o.dtype) for c, o in zip(cots, outs))
        gs = list(vf(ct))
        for (pos, _), av in zip(add_items, avals):
            gs[pos] = gs[pos].astype(F32) + av.astype(F32)
        for pos, (k, g, gref) in enumerate(zip(wrt, gs, grefs)):
            if ins[k].acc:
                @pl.when(i == 0)
                def _():
                    gref[...] = jnp.zeros_like(gref)
                gref[...] += g.astype(gref.dtype)
            else:
                gref[...] = g.astype(gref.dtype)

    all_in = list(ins) + cot_ops + [op for _, op in add_items]
    return pl.pallas_call(
        body, name=name, grid=(grid,),
        in_specs=[o.spec() for o in all_in],
        out_specs=[pl.BlockSpec(ins[k].blk, ins[k].gidx) for k in wrt],
        out_shape=[jax.ShapeDtypeStruct(ins[k].gshape, dt) for k, dt in zip(wrt, gdt)],
        compiler_params=_params(("arbitrary",)),
    )(*[o.arr for o in all_in])


MM_VMEM_BUDGET = 40 * 1024 * 1024
MM_COLS = 1024


def _square_bf16(a):
    a = a.astype(F32)
    return (a * a).astype(BF16)


def mm(name, a, b, out_dtype, pre=None, post=None, extras=()):
    m, k = a.shape
    _, n = b.shape
    nc = min(n, MM_COLS)
    row_bytes = k * a.dtype.itemsize + n * jnp.dtype(out_dtype).itemsize + sum(n * e.dtype.itemsize for e in extras)
    tm = next(t for t in (768, 512, 384, 256, 128, 64)
              if m % t == 0 and 2 * t * row_bytes + 2 * k * n * b.dtype.itemsize + t * nc * 4 <= MM_VMEM_BUDGET)

    def body(a_ref, b_ref, *rest):
        o_ref = rest[-1]
        av = a_ref[...]
        if pre is not None:
            av = pre(av)
        for j in range(n // nc):
            cs = slice(j * nc, (j + 1) * nc)
            acc = lax.dot_general(av, b_ref[:, cs], (((1,), (0,)), ((), ())), preferred_element_type=F32)
            if post is not None:
                acc = post(acc, *[e[:, cs] for e in rest[:-1]])
            o_ref[:, cs] = acc.astype(o_ref.dtype)

    row = lambda w: pl.BlockSpec((tm, w), lambda i: (i, 0))
    return pl.pallas_call(
        body, name=name, grid=(m // tm,),
        in_specs=[row(k), pl.BlockSpec((k, n), lambda i: (0, 0))] + [row(n) for _ in extras],
        out_specs=row(n),
        out_shape=jax.ShapeDtypeStruct((m, n), out_dtype),
        compiler_params=_params(("arbitrary",)),
    )(a, b, *extras)


def mm_tn(name, a, b, pre=None):
    m, ka = a.shape
    _, nb = b.shape
    tm = _pick(m, (768, 512, 256))
    ta = _pick(ka, (2048, 1024, 512, 256, 128))
    tb = _pick(nb, tuple(t for t in (4096, 2048, 1024, 512, 256, 128) if ta * t * 4 <= 8 * 1024 * 1024))

    def body(a_ref, b_ref, o_ref):
        @pl.when(pl.program_id(2) == 0)
        def _():
            o_ref[...] = jnp.zeros_like(o_ref)
        av = a_ref[...] if pre is None else pre(a_ref[...])
        o_ref[...] += lax.dot_general(av, b_ref[...], (((0,), (0,)), ((), ())), preferred_element_type=F32)

    return pl.pallas_call(
        body, name=name, grid=(ka // ta, nb // tb, m // tm),
        in_specs=[pl.BlockSpec((tm, ta), lambda i, j, k: (k, i)), pl.BlockSpec((tm, tb), lambda i, j, k: (k, j))],
        out_specs=pl.BlockSpec((ta, tb), lambda i, j, k: (i, j)),
        out_shape=jax.ShapeDtypeStruct((ka, nb), F32),
        compiler_params=_params(("arbitrary", "arbitrary", "arbitrary")),
    )(a, b)


def _rms(x, w):
    return x * lax.rsqrt(jnp.mean(x * x, axis=-1, keepdims=True) + EPS) * w


def _mod_of(blk, m):
    return jnp.where(blk < NCTXB, m[0], m[1])


def _gelu(x):
    return x * (0.5 * (1.0 + jnp.tanh(math.sqrt(2.0 / math.pi) * (x + 0.044715 * (x * x * x)))))


def _sigmoid(x):
    return 1.0 / (1.0 + jnp.exp(-x))


def _log_sigmoid(z):
    return jnp.minimum(z, 0.0) - jnp.log(1.0 + jnp.exp(-jnp.abs(z)))


def _dot(a, b, dims=((1,), (0,)), precision=None):
    return lax.dot_general(a, b, (dims, ((), ())), precision=precision, preferred_element_type=F32)


def _lane_group_mask(width, group, h):
    lane = lax.broadcasted_iota(jnp.int32, (1, width), 1)
    return (lane >= h * group) & (lane < (h + 1) * group)


def fn_norm1(blk, x, m, nw):
    mv = _mod_of(blk, m)
    return ((_rms(x, nw) * (1.0 + mv[:, D:2 * D]) + mv[:, 0:D]),)


def fn_res_norm2(blk, x, yo, m, nw):
    mv = _mod_of(blk, m)
    x1 = x + mv[:, 2 * D:3 * D] * yo
    return x1, _rms(x1, nw) * (1.0 + mv[:, 4 * D:5 * D]) + mv[:, 3 * D:4 * D]


def fn_res2(blk, x1, f, m):
    mv = _mod_of(blk, m)
    return (x1 + mv[:, 5 * D:6 * D] * f,)


def fn_sgu(blk, su, sv, nw, nb, ws, bm):
    u = _gelu(su)
    g = _gelu(sv)
    mu = jnp.mean(g, axis=-1, keepdims=True)
    var = jnp.mean(jnp.square(g - mu), axis=-1, keepdims=True)
    v = (g - mu) * lax.rsqrt(var + EPS) * nw + nb
    out = []
    for c in range(su.shape[0] // SGU_CHUNK):
        vc = v[c * SGU_CHUNK:(c + 1) * SGU_CHUNK]
        s = bm
        for h in range(4):
            vh = jnp.where(_lane_group_mask(256, 64, h), vc, 0.0)
            s = s + _dot(ws[h].astype(BF16), vh.astype(BF16))
        out.append(u[c * SGU_CHUNK:(c + 1) * SGU_CHUNK] * s)
    return (jnp.concatenate(out, axis=0),)


def fn_gates(blk, pg, wg, bg):
    z = _dot(pg.astype(BF16), wg.astype(BF16)) + bg
    g = _log_sigmoid(z) * (1.0 / GLA_TAU)
    return g[:, :128], g[:, 128:]


def _gla_chunk_terms(g, rev):
    r = lax.broadcasted_iota(jnp.int32, (GLA_CHUNK, GLA_CHUNK), 0)
    c = lax.broadcasted_iota(jnp.int32, (GLA_CHUNK, GLA_CHUNK), 1)
    tri = jnp.where((c >= r) if rev else (c <= r), 1.0, 0.0).astype(F32)
    b = _dot(tri, g, precision=HI)
    return b, jnp.sum(g, axis=0, keepdims=True)


def _bd_mask():
    r = lax.broadcasted_iota(jnp.int32, (128, 256), 0)
    c = lax.broadcasted_iota(jnp.int32, (128, 256), 1)
    return (r // GLA_DK) == (c // 64)


def _gla_kv_chunk(k, v, g, rev):
    b, tot = _gla_chunk_terms(g, rev)
    kd = k * jnp.exp(tot - b)
    u = jnp.where(_bd_mask(), _dot(kd.astype(BF16), v.astype(BF16), ((0,), (0,))), 0.0)
    r = lax.broadcasted_iota(jnp.int32, (128, 128), 0)
    c = lax.broadcasted_iota(jnp.int32, (128, 128), 1)
    col = jnp.sum(jnp.where(r == c, jnp.broadcast_to(jnp.exp(tot), (128, 128)), 0.0), axis=1, keepdims=True)
    return u, jnp.broadcast_to(col, (128, 128))


def _gla_o_chunk(q, k, v, g, s, rev):
    b, _ = _gla_chunk_terms(g, rev)
    qe = q * jnp.exp(b) * (GLA_DK ** -0.5)
    ke = k * jnp.exp(-b)
    o = _dot(qe.astype(BF16), jnp.where(_bd_mask(), s, 0.0).astype(BF16))
    qs = jnp.concatenate([jnp.where(_lane_group_mask(128, GLA_DK, h), qe, 0.0) for h in range(4)], axis=0)
    a = _dot(qs.astype(BF16), ke.astype(BF16), ((1,), (1,)))
    i = lax.broadcasted_iota(jnp.int32, a.shape, 0) % GLA_CHUNK
    j = lax.broadcasted_iota(jnp.int32, a.shape, 1)
    a = jnp.where((j >= i) if rev else (j <= i), a, 0.0)
    av = _dot(a.astype(BF16), v.astype(BF16))
    for h in range(4):
        o = o + jnp.where(_lane_group_mask(256, 64, h), av[GLA_CHUNK * h:GLA_CHUNK * (h + 1)], 0.0)
    return o


def fn_gla_kv(blk, k, v, gf, gb):
    uf, ef, ub, eb = [], [], [], []
    for c in range(k.shape[0] // GLA_CHUNK):
        sl = slice(c * GLA_CHUNK, (c + 1) * GLA_CHUNK)
        u, e = _gla_kv_chunk(k[sl], v[sl], gf[sl], False)
        uf.append(u[None]); ef.append(e[None])
        u, e = _gla_kv_chunk(k[sl], v[sl], gb[sl], True)
        ub.append(u[None]); eb.append(e[None])
    cat = lambda t: jnp.concatenate(t, axis=0)
    return cat(uf), cat(ef), cat(ub), cat(eb)


def fn_gla_o(blk, q, k, v, gf, gb, gr, sf, sb, nwt):
    out = []
    for c in range(q.shape[0] // GLA_CHUNK):
        sl = slice(c * GLA_CHUNK, (c + 1) * GLA_CHUNK)
        out.append(_gla_o_chunk(q[sl], k[sl], v[sl], gf[sl], sf[c], False)
                   + _gla_o_chunk(q[sl], k[sl], v[sl], gb[sl], sb[c], True))
    o = jnp.concatenate(out, axis=0)
    r = lax.broadcasted_iota(jnp.int32, (256, 256), 0)
    c = lax.broadcasted_iota(jnp.int32, (256, 256), 1)
    head_mean = jnp.where((r // 64) == (c // 64), 1.0 / 64.0, 0.0).astype(F32)
    ms = _dot(o * o, head_mean, precision=HI)
    on = o * lax.rsqrt(ms + EPS) * nwt
    return (on * (gr * _sigmoid(gr)),)


def _rope_partner(x):
    lane = lax.broadcasted_iota(jnp.int32, x.shape, 1)
    return jnp.where((lane // 16) % 2 == 0, pltpu.roll(x, LANES - 16, 1), pltpu.roll(x, 16, 1))


@jax.custom_vjp
def _rope(x, cs, sn):
    return x * cs + _rope_partner(x) * sn


def _rope_fwd(x, cs, sn):
    return _rope(x, cs, sn), (cs, sn)


def _rope_bwd(res, dy):
    cs, sn = res
    return dy * cs + _rope_partner(dy * sn), jnp.zeros_like(cs), jnp.zeros_like(sn)


_rope.defvjp(_rope_fwd, _rope_bwd)


def fn_mla_pre(blk, ckv, dq, kvw, qw):
    return _rms(ckv, kvw), _rms(dq, qw)


def fn_mla_post(blk, kk, qu, kr, cs, sn):
    kro = _rope(kr, cs, sn)
    kcat, q = [], []
    for h in range(4):
        kcat += [kk[:, 128 * h:128 * (h + 1)].astype(F32), kro]
        q += [qu[:, 256 * h:256 * h + 128], _rope(qu[:, 256 * h + 128:256 * (h + 1)], cs, sn)]
    return jnp.concatenate(kcat, axis=1), jnp.concatenate(q, axis=1) * SCORE_SCALE


ATTN_ROWS = 256
NEG = -1e30


def _scores(q, k, k0, masked):
    s = _dot(q, k, ((1,), (1,)))
    if masked:
        col = k0 + lax.broadcasted_iota(jnp.int32, s.shape, 1)
        s = jnp.where(col >= CTX, NEG, s)
    return s


def flash_fwd(q, kcat, kvu):
    t = q.shape[0]
    tq = _pick(t, (768, 512, 256))
    tk = _pick(t, (2816, 1536, 768, 512, 256))
    nsub = tq // ATTN_ROWS

    def body(q_ref, k_ref, v_ref, o_ref, lse_ref, m_sc, l_sc, acc_sc):
        qi, ki = pl.program_id(1), pl.program_id(2)

        @pl.when(ki == 0)
        def _():
            m_sc[...] = jnp.full_like(m_sc, NEG)
            l_sc[...] = jnp.zeros_like(l_sc)
            acc_sc[...] = jnp.zeros_like(acc_sc)

        def step(first_tile):
            k, v = k_ref[...], v_ref[...]
            for r in range(nsub):
                rs = pl.ds(r * ATTN_ROWS, ATTN_ROWS)
                s = _scores(q_ref[rs, :], k, ki * tk, first_tile and r == 0)
                m_old = m_sc[rs, :]
                m_new = jnp.maximum(m_old, jnp.max(s, axis=-1, keepdims=True))
                alpha = jnp.exp2(m_old - m_new)
                p = jnp.exp2(s - m_new)
                l_sc[rs, :] = alpha * l_sc[rs, :] + jnp.sum(p, axis=-1, keepdims=True)
                acc_sc[rs, :] = alpha * acc_sc[rs, :] + _dot(p.astype(BF16), v)
                m_sc[rs, :] = m_new

        pl.when(qi == 0)(lambda: step(True))
        pl.when(qi != 0)(lambda: step(False))

        @pl.when(ki == pl.num_programs(2) - 1)
        def _():
            o_ref[...] = acc_sc[...] / l_sc[...]
            lse_ref[...] = jnp.broadcast_to(m_sc[...] + jnp.log2(l_sc[...]), lse_ref.shape)

    return pl.pallas_call(
        body, name="mla_flash_fwd", grid=(4, t // tq, t // tk),
        in_specs=[pl.BlockSpec((tq, 256), lambda h, i, j: (i, h)), pl.BlockSpec((tk, 256), lambda h, i, j: (j, h)),
                  pl.BlockSpec((tk, 128), lambda h, i, j: (j, 4 + h))],
        out_specs=[pl.BlockSpec((tq, 128), lambda h, i, j: (i, h)), pl.BlockSpec((tq, 128), lambda h, i, j: (i, h))],
        out_shape=[jax.ShapeDtypeStruct((t, 512), F32), jax.ShapeDtypeStruct((t, 512), F32)],
        scratch_shapes=[pltpu.VMEM((tq, 1), F32), pltpu.VMEM((tq, 1), F32), pltpu.VMEM((tq, 128), F32)],
        compiler_params=_params(("arbitrary", "arbitrary", "arbitrary")),
    )(q, kcat, kvu)


def fn_attn_stats(blk, do, o, lse):
    out = []
    for h in range(4):
        hs = slice(128 * h, 128 * (h + 1))
        d = jnp.sum(do[:, hs] * o[:, hs], axis=-1, keepdims=True)
        out.append(lse[:, hs].T[0:8])
        out.append(jnp.broadcast_to(d, (do.shape[0], 128)).T[0:8])
    return (jnp.concatenate(out, axis=0)[None],)


def flash_bwd(q, kcat, kvu, dy, stats):
    t = q.shape[0]
    tq = _pick(t, (2816, 768, 512, 256))
    tk = _pick(t, (768, 512, 256))
    nst = tq // TM

    def body(q_ref, k_ref, v_ref, do_ref, st_ref, dq_ref, dk_ref, dv_ref):
        kj, qi = pl.program_id(1), pl.program_id(2)

        @pl.when(qi == 0)
        def _():
            dk_ref[...] = jnp.zeros_like(dk_ref)
            dv_ref[...] = jnp.zeros_like(dv_ref)

        q_, k, v, do = q_ref[...], k_ref[...], v_ref[...], do_ref[...].astype(BF16)
        lse_row = jnp.concatenate([st_ref[u, 0:1, :] for u in range(nst)], axis=1)
        delta_row = jnp.concatenate([st_ref[u, 8:9, :] for u in range(nst)], axis=1)
        s = _dot(k, q_, ((1,), (1,)))
        key = kj * tk + lax.broadcasted_iota(jnp.int32, s.shape, 0)
        qry = qi * tq + lax.broadcasted_iota(jnp.int32, s.shape, 1)
        s = jnp.where((qry < CTX) & (key >= CTX), NEG, s)
        p = jnp.exp2(s - lse_row)
        dp = _dot(v, do, ((1,), (1,)))
        ds = (p * (dp - delta_row)).astype(BF16)
        dv_ref[...] += _dot(p.astype(BF16), do)
        dk_ref[...] += LN2 * _dot(ds, q_)
        dq_new = LN2 * _dot(ds, k, ((0,), (0,)))
        rows_ = pl.ds(pl.multiple_of(qi * tq, TM), tq)

        @pl.when(kj == 0)
        def _():
            dq_ref[rows_, :] = dq_new

        @pl.when(kj != 0)
        def _():
            dq_ref[rows_, :] += dq_new

    return pl.pallas_call(
        body, name="mla_flash_bwd", grid=(4, t // tk, t // tq),
        in_specs=[pl.BlockSpec((tq, 256), lambda h, j, i: (i, h)), pl.BlockSpec((tk, 256), lambda h, j, i: (j, h)),
                  pl.BlockSpec((tk, 128), lambda h, j, i: (j, 4 + h)), pl.BlockSpec((tq, 128), lambda h, j, i: (i, 4 + h)),
                  pl.BlockSpec((nst, 16, 256), lambda h, j, i: (i, h, 0))],
        out_specs=[pl.BlockSpec((t, 256), lambda h, j, i: (0, h)), pl.BlockSpec((tk, 256), lambda h, j, i: (j, h)),
                   pl.BlockSpec((tk, 128), lambda h, j, i: (j, h))],
        out_shape=[jax.ShapeDtypeStruct((t, 1024), F32), jax.ShapeDtypeStruct((t, 1024), F32),
                   jax.ShapeDtypeStruct((t, 512), F32)],
        compiler_params=_params(("arbitrary", "arbitrary", "arbitrary")),
    )(q, kcat, kvu, dy, stats)


SCAN_BLOCK = CTX // GLA_CHUNK


def _scan_block(t, nb, rev):
    if not rev:
        return t
    return jnp.where(t < 1, 0, nb - t)


def _scan_order(rev):
    return tuple(reversed(range(SCAN_BLOCK))) if rev else tuple(range(SCAN_BLOCK))


def _both_halves(e):
    return jnp.concatenate([e, e], axis=1)


def gla_states(uf, ef, ub, eb):
    nb = uf.shape[0] // SCAN_BLOCK

    def body(uf_ref, ef_ref, ub_ref, eb_ref, sf_ref, sb_ref, sf_sc, sb_sc):
        @pl.when(pl.program_id(0) == 0)
        def _():
            sf_sc[...] = jnp.zeros_like(sf_sc)
            sb_sc[...] = jnp.zeros_like(sb_sc)

        for u_ref, e_ref, s_ref, sc, rev in ((uf_ref, ef_ref, sf_ref, sf_sc, False), (ub_ref, eb_ref, sb_ref, sb_sc, True)):
            s = sc[...]
            for c in _scan_order(rev):
                s_ref[c] = s
                s = _both_halves(e_ref[c]) * s + u_ref[c]
            sc[...] = s

    big = lambda rev: pl.BlockSpec((SCAN_BLOCK, 128, 256), lambda t: (_scan_block(t, nb, rev), 0, 0))
    small = lambda rev: pl.BlockSpec((SCAN_BLOCK, 128, 128), lambda t: (_scan_block(t, nb, rev), 0, 0))
    return pl.pallas_call(
        body, name="gla_states", grid=(nb,),
        in_specs=[big(False), small(False), big(True), small(True)],
        out_specs=[big(False), big(True)],
        out_shape=[jax.ShapeDtypeStruct(uf.shape, F32)] * 2,
        scratch_shapes=[pltpu.VMEM((128, 256), F32)] * 2,
        compiler_params=_params(("arbitrary",)),
    )(uf, ef, ub, eb)


def gla_states_bwd(ef, eb, sf, sb, dsf, dsb):
    nb = ef.shape[0] // SCAN_BLOCK

    def body(ef_ref, eb_ref, sf_ref, sb_ref, dsf_ref, dsb_ref, duf_ref, def_ref, dub_ref, deb_ref, gf_sc, gb_sc):
        @pl.when(pl.program_id(0) == 0)
        def _():
            gf_sc[...] = jnp.zeros_like(gf_sc)
            gb_sc[...] = jnp.zeros_like(gb_sc)

        for e_ref, s_ref, ds_ref, du_ref, de_ref, g_sc, rev in ((ef_ref, sf_ref, dsf_ref, duf_ref, def_ref, gf_sc, False),
                                                                 (eb_ref, sb_ref, dsb_ref, dub_ref, deb_ref, gb_sc, True)):
            g = g_sc[...]
            for k in reversed(_scan_order(rev)):
                du_ref[k] = g
                gs = g * s_ref[k]
                de_ref[k] = gs[:, :128] + gs[:, 128:]
                g = _both_halves(e_ref[k]) * g + ds_ref[k]
            g_sc[...] = g

    big = lambda rev: pl.BlockSpec((SCAN_BLOCK, 128, 256), lambda t: (_scan_block(nb - 1 - t, nb, rev), 0, 0))
    small = lambda rev: pl.BlockSpec((SCAN_BLOCK, 128, 128), lambda t: (_scan_block(nb - 1 - t, nb, rev), 0, 0))
    return pl.pallas_call(
        body, name="gla_states_bwd", grid=(nb,),
        in_specs=[small(False), small(True), big(False), big(True), big(False), big(True)],
        out_specs=[big(False), small(False), big(True), small(True)],
        out_shape=[jax.ShapeDtypeStruct(sf.shape, F32), jax.ShapeDtypeStruct(ef.shape, F32)] * 2,
        scratch_shapes=[pltpu.VMEM((128, 256), F32)] * 2,
        compiler_params=_params(("arbitrary",)),
    )(ef, eb, sf, sb, dsf, dsb)


def loss_head(xt, target, fnw):
    n = target.shape[0]

    def f(x, t, w):
        y = _rms(x, w)
        return 0.5 * jnp.sum(jnp.square(y - t)) * (1.0 / D)

    def body(x_ref, t_ref, w_ref, loss_ref, dx_ref, dw_ref):
        @pl.when(pl.program_id(0) == 0)
        def _():
            loss_ref[...] = jnp.zeros_like(loss_ref)
            dw_ref[...] = jnp.zeros_like(dw_ref)

        val, (dx, dw) = jax.value_and_grad(f, argnums=(0, 2))(x_ref[...], t_ref[...], w_ref[...])
        loss_ref[...] += jnp.broadcast_to(val, loss_ref.shape)
        dx_ref[...] = dx
        dw_ref[...] += dw

    return pl.pallas_call(
        body, name="loss_head", grid=(n // TM,),
        in_specs=[pl.BlockSpec((TM, D), lambda i: (i + NCTXB, 0)), pl.BlockSpec((TM, D), lambda i: (i, 0)),
                  pl.BlockSpec((1, D), lambda i: (0, 0))],
        out_specs=[pl.BlockSpec((1, 128), lambda i: (0, 0)), pl.BlockSpec((TM, D), lambda i: (i, 0)),
                   pl.BlockSpec((1, D), lambda i: (0, 0))],
        out_shape=[jax.ShapeDtypeStruct((1, 128), F32), jax.ShapeDtypeStruct((n, D), F32), jax.ShapeDtypeStruct((1, D), F32)],
        compiler_params=_params(("arbitrary",)),
    )(xt, target, fnw)


def _in_to_padded(w):
    out, pos = [], 0
    for src, wd, dst in sorted(IN_GROUPS, key=lambda g: g[2]):
        if dst > pos:
            out.append(jnp.zeros((w.shape[0], dst - pos), w.dtype))
        out.append(w[:, src:src + wd])
        pos = dst + wd
    if pos < P_COLS:
        out.append(jnp.zeros((w.shape[0], P_COLS - pos), w.dtype))
    return jnp.concatenate(out, axis=1)


def _in_from_padded(g):
    return jnp.concatenate([g[:, dst:dst + wd] for _, wd, dst in IN_GROUPS], axis=1)


def _uq_to_padded(w):
    return jnp.pad(w.reshape(256, 4, 192), ((0, 0), (0, 0), (0, 64))).reshape(256, 1024)


def _uq_from_padded(g):
    return g.reshape(256, 4, 256)[:, :, :192].reshape(256, 768)


def _ukv_to_padded(w):
    return w.reshape(256, 4, 2, 128).transpose(0, 2, 1, 3).reshape(256, 1024)


def _ukv_from_padded(g):
    return g.reshape(256, 2, 4, 128).transpose(0, 2, 1, 3).reshape(256, 1024)


def _rope_tables(n):
    pos = jnp.arange(n)
    freq = ROPE_BASE ** (-jnp.arange(16, dtype=F32) * 2.0 / 32.0)
    ar = (pos // GRID_W).astype(F32)[:, None] * freq[None, :]
    ac = (pos % GRID_W).astype(F32)[:, None] * freq[None, :]
    z = jnp.zeros((n, 64), F32)
    cs = jnp.concatenate([jnp.cos(ar), jnp.cos(ar), jnp.cos(ac), jnp.cos(ac), z], axis=1)
    sn = jnp.concatenate([-jnp.sin(ar), jnp.sin(ar), -jnp.sin(ac), jnp.sin(ac), z], axis=1)
    cs_c = jnp.concatenate([jnp.ones((CTX, 64), F32), jnp.zeros((CTX, 64), F32)], axis=1)
    return jnp.concatenate([cs_c, cs], axis=0), jnp.concatenate([jnp.zeros((CTX, 128), F32), sn], axis=0)


def _small_views(sp):
    wg = jnp.concatenate([jnp.pad(sp["gla_wg_fwd"], ((0, 112), (0, 0))), jnp.pad(sp["gla_wg_bwd"], ((16, 96), (0, 0)))], axis=1)
    return dict(
        n1w=sp["norm1_w"][None], n2w=sp["norm2_w"][None],
        sgu_nw=sp["sgu_norm_w"][None], sgu_nb=sp["sgu_norm_b"][None], sgu_w=sp["sgu_w"],
        sgu_bm=jnp.repeat(sp["sgu_b"].T, 64, axis=1),
        wg=wg, bg=jnp.concatenate([sp["gla_bg_fwd"], sp["gla_bg_bwd"]])[None],
        gla_nwt=jnp.tile(sp["gla_norm_w"], 4)[None],
        kvw=sp["mla_kv_norm_w"][None], qw=sp["mla_q_norm_w"][None])


def _small_grads(g):
    return dict(
        norm1_w=g["n1w"][0], norm2_w=g["n2w"][0],
        sgu_norm_w=g["sgu_nw"][0], sgu_norm_b=g["sgu_nb"][0], sgu_w=g["sgu_w"],
        sgu_b=g["sgu_bm"].reshape(128, 4, 64).sum(-1).T,
        gla_wg_fwd=g["wg"][0:16, 0:128], gla_wg_bwd=g["wg"][16:32, 128:256],
        gla_bg_fwd=g["bg"][0, 0:128], gla_bg_bwd=g["bg"][0, 128:256],
        gla_norm_w=g["gla_nwt"].reshape(4, 64).sum(0),
        mla_kv_norm_w=g["kvw"][0], mla_q_norm_w=g["qw"][0])


def _big_views(w_in, w_out, w_uq, w_ukv, w_ff1, w_ff2):
    win, wuq, wukv = _in_to_padded(w_in), _uq_to_padded(w_uq), _ukv_to_padded(w_ukv)
    return dict(win=win, win_t=win.T, wuq=wuq, wuq_t=wuq.T, wukv=wukv, wukv_t=wukv.T,
                wout=w_out, wout_t=w_out.T, w1=w_ff1, w1_t=w_ff1.T, w2=w_ff2, w2_t=w_ff2.T)


def _layer_ops(p, sv, a):
    pc = lambda off, w: rows(p, w, off // w)
    return dict(
        sgu=[pc(P_SU, 256), pc(P_SV, 256), const(sv["sgu_nw"]), const(sv["sgu_nb"]), const(sv["sgu_w"]), const(sv["sgu_bm"])],
        gates=[pc(P_GATE, 128), const(sv["wg"]), const(sv["bg"])],
        mla_pre=[pc(P_CKV, 256), pc(P_DQ, 256), const(sv["kvw"]), const(sv["qw"])],
        gla_kv=lambda: [pc(P_GK, 128), pc(P_GV, 256), rows(a["gf"]), rows(a["gb"])],
        gla_o=lambda: [pc(P_GQ, 128), pc(P_GK, 128), pc(P_GV, 256), rows(a["gf"]), rows(a["gb"]), pc(P_GR, 256),
                       chunks(a["sf"], TM // GLA_CHUNK), chunks(a["sb"], TM // GLA_CHUNK), const(sv["gla_nwt"])],
        mla_post=lambda: [rows(a["kvu"], 512, 0), rows(a["qu"]), pc(P_KR, 128), rows(a["cs"]), rows(a["sn"])])


def layer_fwd(l, xt, modl, bw, sv, tabs):
    t = xt.shape[0]
    g, nc, cpt = t // TM, t // GLA_CHUNK, TM // GLA_CHUNK
    nm = lambda s: f"l{l}_{s}"
    a = dict(x=xt, cs=tabs[0], sn=tabs[1])
    a["h"], = rw(nm("norm1"), fn_norm1, [rows(xt), const(modl), const(sv["n1w"])], [rowout(t, D, BF16)], g)
    p = a["p"] = mm(nm("in_proj"), a["h"], bw["win"], F32)
    ops = _layer_ops(p, sv, a)
    y_sgu, = rw(nm("sgu"), fn_sgu, ops["sgu"], [rowout(t, 256, BF16)], g)
    a["gf"], a["gb"] = rw(nm("gates"), fn_gates, ops["gates"], [rowout(t, 128, F32)] * 2, g)
    a["uf"], a["ef"], a["ub"], a["eb"] = rw(nm("gla_kv"), fn_gla_kv, ops["gla_kv"](),
                                           [chunkout((nc, 128, 256), F32, cpt), chunkout((nc, 128, 128), F32, cpt)] * 2, g)
    a["sf"], a["sb"] = gla_states(a["uf"], a["ef"], a["ub"], a["eb"])
    y_gla, = rw(nm("gla_o"), fn_gla_o, ops["gla_o"](), [rowout(t, 256, BF16)], g)
    a["ckvn"], a["dqn"] = rw(nm("mla_pre"), fn_mla_pre, ops["mla_pre"], [rowout(t, 256, BF16)] * 2, g)
    a["kvu"] = mm(nm("kv_up"), a["ckvn"], bw["wukv"], BF16)
    a["qu"] = mm(nm("q_up"), a["dqn"], bw["wuq"], F32)
    a["kcat"], a["q"] = rw(nm("mla_post"), fn_mla_post, ops["mla_post"](), [rowout(t, 1024, BF16)] * 2, g)
    a["o"], a["lse"] = flash_fwd(a["q"], a["kcat"], a["kvu"])
    a["y"] = jnp.concatenate([y_sgu, y_gla, a["o"].astype(BF16)], axis=1)
    a["yo"] = mm(nm("out_proj"), a["y"], bw["wout"], F32)
    a["x1"], a["h2"] = rw(nm("res_norm2"), fn_res_norm2, [rows(xt), rows(a["yo"]), const(modl), const(sv["n2w"])],
                          [rowout(t, D, F32), rowout(t, D, BF16)], g)
    a["act"] = mm(nm("ff1"), a["h2"], bw["w1"], BF16, post=lambda acc: jnp.maximum(acc, 0.0))
    a["f"] = mm(nm("ff2"), a["act"], bw["w2"], F32, pre=_square_bf16)
    x2, = rw(nm("res2"), fn_res2, [rows(a["x1"]), rows(a["f"]), const(modl)], [rowout(t, D, F32)], g)
    return x2, a


def fn_assemble(blk, gv1, gv2, ckv, su, sv_, gr, dq, gk1, gk2, pg, kr, gq):
    return (jnp.concatenate([gv1 + gv2, ckv, su, sv_, gr, dq, gk1 + gk2, pg, kr, gq], axis=1),)


def layer_bwd(l, dx2, a, modl, bw, sv):
    t = dx2.shape[0]
    g, nc, cpt = t // TM, t // GLA_CHUNK, TM // GLA_CHUNK
    nm = lambda s: f"l{l}_{s}_bwd"
    p = a["p"]
    ops = _layer_ops(p, sv, a)
    gw, gs = {}, {}
    dx1a, df, dm_a = rw_vjp(nm("res2"), fn_res2, [rows(a["x1"]), rows(a["f"]), const(modl)], [rows(dx2)], [0, 1, 2], g,
                            gdt=[F32, BF16, F32])
    gw["w2"] = mm_tn(nm("ff2_w"), a["act"], df, pre=_square_bf16)
    du = mm(nm("ff2_x"), df, bw["w2_t"], BF16, post=lambda acc, act: acc * (2.0 * act.astype(F32)), extras=(a["act"],))
    gw["w1"] = mm_tn(nm("ff1_w"), a["h2"], du)
    dh2 = mm(nm("ff1_x"), du, bw["w1_t"], F32)
    dxa, dyo, dm_b, gs["n2w"] = rw_vjp(nm("res_norm2"), fn_res_norm2,
                                       [rows(a["x"]), rows(a["yo"]), const(modl), const(sv["n2w"])],
                                       [rows(dx1a), rows(dh2)], [0, 1, 2, 3], g, gdt=[F32, BF16, F32, F32])
    gw["wout"] = mm_tn(nm("out_w"), a["y"], dyo)
    dy = mm(nm("out_x"), dyo, bw["wout_t"], F32)
    dsu, dsv, gs["sgu_nw"], gs["sgu_nb"], gs["sgu_w"], gs["sgu_bm"] = rw_vjp(
        nm("sgu"), fn_sgu, ops["sgu"], [rows(dy, 256, 0)], [0, 1, 2, 3, 4, 5], g)
    dgq, dgk1, dgv1, dgf1, dgb1, dgr, dsf, dsb, gs["gla_nwt"] = rw_vjp(
        nm("gla_o"), fn_gla_o, ops["gla_o"](), [rows(dy, 256, 1)], list(range(9)), g)
    duf, def_, dub, deb = gla_states_bwd(a["ef"], a["eb"], a["sf"], a["sb"], dsf, dsb)
    dgk2, dgv2, dgf, dgb = rw_vjp(nm("gla_kv"), fn_gla_kv, ops["gla_kv"](),
                                  [chunks(duf, cpt), chunks(def_, cpt), chunks(dub, cpt), chunks(deb, cpt)], [0, 1, 2, 3], g,
                                  adds={2: rows(dgf1), 3: rows(dgb1)})
    dpg, gs["wg"], gs["bg"] = rw_vjp(nm("gates"), fn_gates, ops["gates"], [rows(dgf), rows(dgb)], [0, 1, 2], g)
    stats, = rw(nm("attn_stats"), fn_attn_stats, [rows(dy, 512, 1), rows(a["o"]), rows(a["lse"])],
                [chunkout((g, 64, TM), F32, 1)], g)
    dq, dkcat, dv = flash_bwd(a["q"], a["kcat"], a["kvu"], dy, stats)
    dkk, dqu, dkr = rw_vjp(nm("mla_post"), fn_mla_post, ops["mla_post"](), [rows(dkcat), rows(dq)], [0, 1, 2], g,
                           gdt=[BF16, BF16, F32])
    dkvu = jnp.concatenate([dkk, dv.astype(BF16)], axis=1)
    gw["wukv"] = mm_tn(nm("kv_up_w"), a["ckvn"], dkvu)
    gw["wuq"] = mm_tn(nm("q_up_w"), a["dqn"], dqu)
    dckvn = mm(nm("kv_up_x"), dkvu, bw["wukv_t"], F32)
    ddqn = mm(nm("q_up_x"), dqu, bw["wuq_t"], F32)
    dckv, ddq, gs["kvw"], gs["qw"] = rw_vjp(nm("mla_pre"), fn_mla_pre, ops["mla_pre"], [rows(dckvn), rows(ddqn)],
                                            [0, 1, 2, 3], g)
    dp, = rw(nm("assemble"), fn_assemble,
             [rows(x_) for x_ in (dgv1, dgv2, dckv, dsu, dsv, dgr, ddq, dgk1, dgk2, dpg, dkr, dgq)],
             [rowout(t, P_COLS, BF16)], g)
    gw["win"] = mm_tn(nm("in_w"), a["h"], dp)
    dh = mm(nm("in_x"), dp, bw["win_t"], F32)
    dx, dm_c, gs["n1w"] = rw_vjp(nm("norm1"), fn_norm1, [rows(a["x"]), const(modl), const(sv["n1w"])], [rows(dh)],
                                 [0, 1, 2], g, adds={0: rows(dxa)})
    big = dict(w_in=_in_from_padded(gw["win"]), w_out=gw["wout"], mla_w_uq=_uq_from_padded(gw["wuq"]),
               mla_w_ukv=_ukv_from_padded(gw["wukv"]), w_ff1=gw["w1"], w_ff2=gw["w2"])
    return dx, dm_a + dm_b + dm_c, big, _small_grads(gs)


SMALL_NAMES = ("norm1_w", "sgu_norm_w", "sgu_norm_b", "sgu_w", "sgu_b", "gla_wg_fwd", "gla_bg_fwd", "gla_wg_bwd",
               "gla_bg_bwd", "gla_norm_w", "mla_q_norm_w", "mla_kv_norm_w", "norm2_w")
BIG_NAMES = ("w_in", "w_out", "mla_w_uq", "mla_w_ukv", "w_ff1", "w_ff2")


def local_step(x, ctx, target, mods, big, small, final_norm_w):
    n = x.shape[0]
    xt = jnp.concatenate([ctx, x], axis=0)
    tabs = _rope_tables(n)
    depth = len(mods)
    bws = [_big_views(*[big[l][k] for k in BIG_NAMES]) for l in range(depth)]
    svs = [_small_views(small[l]) for l in range(depth)]
    acts = []
    for l in range(depth):
        xt, a = layer_fwd(l, xt, mods[l], bws[l], svs[l], tabs)
        acts.append(a)
    loss, dxl, dfnw = loss_head(xt, target, final_norm_w[None])
    dxt = jnp.concatenate([jnp.zeros((CTX, D), F32), dxl], axis=0)
    dmods, gbig, gsmall = [None] * depth, [None] * depth, [None] * depth
    for l in reversed(range(depth)):
        dxt, dmods[l], gbig[l], gsmall[l] = layer_bwd(l, dxt, acts[l], mods[l], bws[l], svs[l])
    return loss, dxt[CTX:], dmods, gbig, gsmall, dfnw


def _group(group):
    x, y, c = lax.axis_index("x"), lax.axis_index("y"), lax.axis_index("c")
    if group == "sib":
        return 2, c, [((x, y, 1 - c), 1 - c)]
    if group == "chip":
        flips = [(1, 0), (0, 1), (1, 1)]
        return 4, 2 * x + y, [((x ^ fx, y ^ fy, c), 2 * (x ^ fx) + (y ^ fy)) for fx, fy in flips]
    flips = [(fx, fy, fc) for fx in (0, 1) for fy in (0, 1) for fc in (0, 1)][1:]
    return 8, 4 * x + 2 * y + c, [((x ^ fx, y ^ fy, c ^ fc), 4 * (x ^ fx) + 2 * (y ^ fy) + (c ^ fc)) for fx, fy, fc in flips]


def _group_size(group):
    return {"sib": 2, "chip": 4, "all": 8}[group]


REMOTE_COPIES = {"gather": None, "scatter": None, "swap": 1, "gather2": 6}


def xchg(name, entries):
    n_in = sum(len(arrs) for _, _, arrs in entries)
    n_remote = sum(REMOTE_COPIES[k] or _group_size(g) - 1 for k, g, _ in entries)
    n_local = sum(1 for k, _, _ in entries if k != "swap")
    out_shape = []
    for kind, group, arrs in entries:
        a = arrs[0]
        if kind in ("gather", "gather2"):
            out_shape.append(jax.ShapeDtypeStruct((_group_size(group),) + a.shape, a.dtype))
        else:
            out_shape.append(jax.ShapeDtypeStruct(a.shape, a.dtype))

    def body(*refs):
        in_refs, out_refs = refs[:n_in], refs[n_in:n_in + len(entries)]
        send_sems, recv_sems, local_sems = refs[n_in + len(entries):]
        x, y, c = lax.axis_index("x"), lax.axis_index("y"), lax.axis_index("c")

        def remote(src, dst, k, dev):
            return pltpu.make_async_remote_copy(src_ref=src, dst_ref=dst, send_sem=send_sems.at[k], recv_sem=recv_sems.at[k],
                                                device_id=dev, device_id_type=MESH)

        pos, k, kl = 0, 0, 0
        forwards, finals = [], []
        for (kind, group, arrs), out in zip(entries, out_refs):
            srcs = in_refs[pos:pos + len(arrs)]
            pos += len(arrs)
            _, mine, peers = _group(group)
            if kind == "swap":
                (dev, _), = peers
                for core, src in ((0, srcs[1]), (1, srcs[0])):
                    @pl.when(c == core)
                    def _(src=src, k=k, dev=dev, out=out):
                        remote(src, out, k, dev).start()
                finals.append(remote(srcs[0], out, k, dev).wait)
                k += 1
                continue
            src = srcs[0]
            own = pltpu.make_async_copy(src if kind != "scatter" else src.at[mine], out.at[mine], local_sems.at[kl])
            own.start()
            finals.append(own.wait)
            kl += 1
            if kind == "gather2":
                sibling = (x, y, 1 - c)
                for f, (dev, slot) in enumerate(peers):
                    remote(src.at[c], out.at[mine, c], k + f, dev).start()
                    arrival = remote(src.at[c], out.at[slot, c], k + f, dev)

                    def forward(arrival=arrival, slot=slot, kf=k + 3 + f, out=out):
                        arrival.wait_recv()
                        remote(out.at[slot, c], out.at[slot, c], kf, sibling).start()

                    forwards.append(forward)
                    finals.append(arrival.wait_send)
                    finals.append(remote(out.at[slot, c], out.at[slot, 1 - c], k + 3 + f, sibling).wait)
                k += 6
                continue
            for dev, slot in peers:
                piece = src if kind == "gather" else src.at[slot]
                remote(piece, out.at[mine], k, dev).start()
                finals.append(remote(piece, out.at[slot], k, dev).wait)
                k += 1
        for run in forwards + finals:
            run()

    any_spec = pl.BlockSpec(memory_space=pl.ANY)
    return pl.pallas_call(
        body, name=name,
        in_specs=[any_spec] * n_in, out_specs=[any_spec] * len(entries), out_shape=out_shape,
        scratch_shapes=[pltpu.SemaphoreType.DMA((n_remote,)), pltpu.SemaphoreType.DMA((n_remote,)),
                        pltpu.SemaphoreType.DMA((max(n_local, 1),))],
    )(*[a for _, _, arrs in entries for a in arrs])


def _block_rows(r, c, budget=131072):
    tr = 8
    while tr * 2 * c <= budget and r % (tr * 2) == 0:
        tr *= 2
    return tr if r % tr == 0 else r


def tree_sum(name, parts):
    g, r, c = parts.shape
    tr = _block_rows(r, c)

    def body(p_ref, o_ref):
        p = [p_ref[i].astype(F32) for i in range(g)]
        while len(p) > 1:
            p = [p[i] + p[i + 1] for i in range(0, len(p), 2)]
        o_ref[...] = p[0]

    return pl.pallas_call(
        body, name=name, grid=(r // tr,),
        in_specs=[pl.BlockSpec((g, tr, c), lambda i: (0, i, 0))], out_specs=pl.BlockSpec((tr, c), lambda i: (i, 0)),
        out_shape=jax.ShapeDtypeStruct((r, c), F32), compiler_params=_params(("arbitrary",)),
    )(parts)


def pair_sum(name, g0, g1, recv, core):
    r, c = recv.shape
    tr = _block_rows(r, c)

    def body(a_ref, b_ref, r_ref, k_ref, o_ref):
        o_ref[...] = jnp.where(k_ref[...] > 0.5, b_ref[...], a_ref[...]) + r_ref[...]

    blk = pl.BlockSpec((tr, c), lambda i: (i, 0))
    return pl.pallas_call(
        body, name=name, grid=(r // tr,), in_specs=[blk, blk, blk, pl.BlockSpec((1, 1), lambda i: (0, 0))],
        out_specs=blk, out_shape=jax.ShapeDtypeStruct((r, c), F32), compiler_params=_params(("arbitrary",)),
    )(g0, g1, recv, core)


def adamw(name, w, g, m, v):
    r, c = w.shape
    tr = _block_rows(r, c)

    def body(w_ref, g_ref, m_ref, v_ref, d_ref, nm_ref, nv_ref):
        gg = g_ref[...]
        nm = ADAM_B1 * m_ref[...] + (1.0 - ADAM_B1) * gg
        nv = ADAM_B2 * v_ref[...] + (1.0 - ADAM_B2) * jnp.square(gg)
        m_hat = nm / (1.0 - ADAM_B1 ** ADAM_STEP)
        v_hat = nv / (1.0 - ADAM_B2 ** ADAM_STEP)
        d_ref[...] = -ADAM_LR * (m_hat / (jnp.sqrt(v_hat) + ADAM_EPS) + ADAM_WD * w_ref[...])
        nm_ref[...] = nm
        nv_ref[...] = nv

    blk = pl.BlockSpec((tr, c), lambda i: (i, 0))
    return pl.pallas_call(
        body, name=name, grid=(r // tr,), in_specs=[blk] * 4, out_specs=[blk] * 3,
        out_shape=[jax.ShapeDtypeStruct((r, c), F32)] * 3, compiler_params=_params(("arbitrary",)),
    )(w, g, m, v)


W_MOD_COLS = 6 * D // 4
MOD_TN = 512


def mod_project(c16, w_mod, b_loc):
    def body(c_ref, w_ref, b_ref, o_ref):
        cv = c_ref[...]
        s = (cv * _sigmoid(cv)).astype(BF16)
        o_ref[0] = _dot(s, w_ref[0].astype(BF16)) + b_ref[0]

    return pl.pallas_call(
        body, name="mod_project", grid=(2, W_MOD_COLS // MOD_TN),
        in_specs=[pl.BlockSpec((16, D), lambda l, j: (0, 0)), pl.BlockSpec((1, D, MOD_TN), lambda l, j: (l, 0, j)),
                  pl.BlockSpec((1, 1, MOD_TN), lambda l, j: (l, 0, j))],
        out_specs=pl.BlockSpec((1, 16, MOD_TN), lambda l, j: (l, 0, j)),
        out_shape=jax.ShapeDtypeStruct((2, 16, W_MOD_COLS), F32), compiler_params=_params(("arbitrary", "arbitrary")),
    )(c16, w_mod, b_loc)


def mod_weight_grad(c16, dm16):
    def body(c_ref, d_ref, o_ref):
        cv = c_ref[...]
        o_ref[0] = _dot(cv * _sigmoid(cv), d_ref[0], ((0,), (0,)), precision=HI)

    return pl.pallas_call(
        body, name="mod_weight_grad", grid=(2, W_MOD_COLS // MOD_TN),
        in_specs=[pl.BlockSpec((16, D), lambda l, j: (0, 0)), pl.BlockSpec((1, 16, MOD_TN), lambda l, j: (l, 0, j))],
        out_specs=pl.BlockSpec((1, D, MOD_TN), lambda l, j: (l, 0, j)),
        out_shape=jax.ShapeDtypeStruct((2, D, W_MOD_COLS), F32), compiler_params=_params(("arbitrary", "arbitrary")),
    )(c16, dm16)


def cctx_partial(dmc, w_mod):
    def body(d_ref, w_ref, o_ref):
        @pl.when(pl.program_id(0) == 0)
        def _():
            o_ref[...] = jnp.zeros_like(o_ref)
        o_ref[...] += _dot(d_ref[0], w_ref[0], ((1,), (1,)), precision=HI)

    return pl.pallas_call(
        body, name="cctx_partial", grid=(2,),
        in_specs=[pl.BlockSpec((1, 8, W_MOD_COLS), lambda l: (l, 0, 0)), pl.BlockSpec((1, D, W_MOD_COLS), lambda l: (l, 0, 0))],
        out_specs=pl.BlockSpec((8, D), lambda l: (0, 0)),
        out_shape=jax.ShapeDtypeStruct((8, D), F32), compiler_params=_params(("arbitrary",)),
    )(dmc, w_mod)


def cctx_grad(parts, c_ctx8):
    def body(p_ref, c_ref, o_ref):
        ds = (p_ref[0] + p_ref[1]) + (p_ref[2] + p_ref[3])
        _, vf = jax.vjp(lambda z: z * _sigmoid(z), c_ref[...])
        o_ref[...] = vf(ds)[0]

    return pl.pallas_call(
        body, name="cctx_grad", out_shape=jax.ShapeDtypeStruct((8, D), F32),
    )(parts, c_ctx8)


ARG_NAMES = ("x", "c", "ctx", "c_ctx", "w_mod", "b_mod", "norm1_w", "w_in", "w_out", "sgu_norm_w", "sgu_norm_b", "sgu_w",
             "sgu_b", "gla_wg_fwd", "gla_bg_fwd", "gla_wg_bwd", "gla_bg_bwd", "gla_norm_w", "mla_q_norm_w", "mla_w_uq",
             "mla_kv_norm_w", "mla_w_ukv", "norm2_w", "w_ff1", "w_ff2", "final_norm_w")
WEIGHT_NAMES = ARG_NAMES[3:]
PACKED = ("c_ctx", "b_mod") + SMALL_NAMES + ("final_norm_w",)
ROW_SHARDED = ("w_out", "w_ff2")
PACK_ROWS = 256


def _pack(vectors):
    flat = jnp.concatenate([v.reshape(-1) for v in vectors])
    n = flat.shape[0]
    total = -(-n // (PACK_ROWS * LANES)) * PACK_ROWS * LANES
    return jnp.pad(flat, (0, total - n)).reshape(-1, LANES)


def _unpack(buf, shapes):
    flat, out, pos = buf.reshape(-1), [], 0
    for shp in shapes:
        n = int(np.prod(shp))
        out.append(flat[pos:pos + n].reshape(shp))
        pos += n
    return out


def _full_weight(name, gathered, l):
    g = gathered[:, l]
    if name in ROW_SHARDED:
        return g.reshape(-1, g.shape[-1])
    return g.transpose(1, 0, 2).reshape(g.shape[1], -1)


def _chip_chunks(name, a):
    if name in ROW_SHARDED:
        return a.reshape(4, a.shape[0] // 4, a.shape[1])
    return a.reshape(a.shape[0], 4, a.shape[1] // 4).transpose(1, 0, 2)


def kernel(x, c, ctx, c_ctx, w_mod, b_mod, norm1_w, w_in, w_out, sgu_norm_w, sgu_norm_b, sgu_w, sgu_b, gla_wg_fwd, gla_bg_fwd, gla_wg_bwd, gla_bg_bwd, gla_norm_w, mla_q_norm_w, mla_w_uq, mla_kv_norm_w, mla_w_ukv, norm2_w, w_ff1, w_ff2, final_norm_w, loss_target, m_c_ctx, m_w_mod, m_b_mod, m_norm1_w, m_w_in, m_w_out, m_sgu_norm_w, m_sgu_norm_b, m_sgu_w, m_sgu_b, m_gla_wg_fwd, m_gla_bg_fwd, m_gla_wg_bwd, m_gla_bg_bwd, m_gla_norm_w, m_mla_q_norm_w, m_mla_w_uq, m_mla_kv_norm_w, m_mla_w_ukv, m_norm2_w, m_w_ff1, m_w_ff2, m_final_norm_w, v_c_ctx, v_w_mod, v_b_mod, v_norm1_w, v_w_in, v_w_out, v_sgu_norm_w, v_sgu_norm_b, v_sgu_w, v_sgu_b, v_gla_wg_fwd, v_gla_bg_fwd, v_gla_wg_bwd, v_gla_bg_bwd, v_gla_norm_w, v_mla_q_norm_w, v_mla_w_uq, v_mla_kv_norm_w, v_mla_w_ukv, v_norm2_w, v_w_ff1, v_w_ff2, v_final_norm_w):
    args = (x, c, ctx, c_ctx, w_mod, b_mod, norm1_w, w_in, w_out, sgu_norm_w, sgu_norm_b, sgu_w, sgu_b, gla_wg_fwd, gla_bg_fwd, gla_wg_bwd, gla_bg_bwd, gla_norm_w, mla_q_norm_w, mla_w_uq, mla_kv_norm_w, mla_w_ukv, norm2_w, w_ff1, w_ff2, final_norm_w)
    w = dict(zip(ARG_NAMES, args))
    moms = (m_c_ctx, m_w_mod, m_b_mod, m_norm1_w, m_w_in, m_w_out, m_sgu_norm_w, m_sgu_norm_b, m_sgu_w, m_sgu_b, m_gla_wg_fwd, m_gla_bg_fwd, m_gla_wg_bwd, m_gla_bg_bwd, m_gla_norm_w, m_mla_q_norm_w, m_mla_w_uq, m_mla_kv_norm_w, m_mla_w_ukv, m_norm2_w, m_w_ff1, m_w_ff2, m_final_norm_w)
    vars_ = (v_c_ctx, v_w_mod, v_b_mod, v_norm1_w, v_w_in, v_w_out, v_sgu_norm_w, v_sgu_norm_b, v_sgu_w, v_sgu_b, v_gla_wg_fwd, v_gla_bg_fwd, v_gla_wg_bwd, v_gla_bg_bwd, v_gla_norm_w, v_mla_q_norm_w, v_mla_w_uq, v_mla_kv_norm_w, v_mla_w_ukv, v_norm2_w, v_w_ff1, v_w_ff2, v_final_norm_w)
    m1 = dict(zip(WEIGHT_NAMES, moms))
    m2 = dict(zip(WEIGHT_NAMES, vars_))
    xi, yi, ci = lax.axis_index("x"), lax.axis_index("y"), lax.axis_index("c")
    chip, dev = 2 * xi + yi, 4 * xi + 2 * yi + ci
    depth = w_mod.shape[0]

    got = xchg("gather_inputs", [("gather", "all", [c])] + [("gather2", "chip", [w[k].astype(BF16)]) for k in BIG_NAMES])
    c_all, shards = got[0], dict(zip(BIG_NAMES, got[1:]))
    c16 = jnp.concatenate([c_all.reshape(8, D), c_ctx[None], jnp.zeros((7, D), F32)], axis=0)
    b_loc = lax.dynamic_slice_in_dim(b_mod, chip * W_MOD_COLS, W_MOD_COLS, axis=1)[:, None, :]
    mod_part = mod_project(c16, w_mod, b_loc)
    mod_all, = xchg("gather_mod", [("gather", "chip", [mod_part])])
    mod_full = mod_all.transpose(1, 2, 0, 3).reshape(depth, 16, 6 * D)
    mods = [jnp.stack([mod_full[l, 8], lax.dynamic_index_in_dim(mod_full[l], dev, 0, keepdims=False)])[:, None, :]
            for l in range(depth)]

    big = [{k: _full_weight(k, shards[k], l) for k in BIG_NAMES} for l in range(depth)]
    small = [{k: w[k][l] for k in SMALL_NAMES} for l in range(depth)]
    loss, grad_x, dmods, gbig, gsmall, dfnw = local_step(x[0], ctx[0], loss_target[0], mods, big, small, final_norm_w)
    loss = lax.psum(loss[0, 0], ("x", "y", "c"))

    dm_lat = jnp.stack([dmods[l][1, 0] for l in range(depth)])
    dm_ctx = jnp.stack([dmods[l][0, 0] for l in range(depth)])
    small_pack = _pack([dm_lat, dm_ctx] + [jnp.stack([gsmall[l][k] for l in range(depth)]) for k in SMALL_NAMES] + [dfnw])
    got = xchg("exchange_grads", [("gather", "all", [small_pack])] + [("swap", "sib", [gbig[0][k], gbig[1][k]]) for k in BIG_NAMES])
    small_all, from_sib = got[0], dict(zip(BIG_NAMES, got[1:]))
    small_sum = tree_sum("small_grad_sum", small_all)
    core = ci.astype(F32).reshape(1, 1)
    mine = {k: pair_sum(f"pair_sum_{k}", gbig[0][k], gbig[1][k], from_sib[k], core) for k in BIG_NAMES}

    n_dm = depth * 6 * D
    dm_rows = n_dm // LANES
    dm_lat_all = small_all[:, :dm_rows].reshape(8, depth, 6 * D)
    dm_ctx_sum = small_sum[dm_rows:2 * dm_rows].reshape(depth, 6 * D)
    take = lambda a: lax.dynamic_slice_in_dim(a, chip * W_MOD_COLS, W_MOD_COLS, axis=-1)
    dmc_loc = take(dm_ctx_sum)
    cc_part = cctx_partial(jnp.pad(dmc_loc[:, None, :], ((0, 0), (0, 7), (0, 0))), w_mod)
    got = xchg("scatter_grads", [("gather", "chip", [cc_part])]
               + [("scatter", "chip", [_chip_chunks(k, mine[k]).astype(BF16)]) for k in BIG_NAMES])
    cc_parts, chunks_in = got[0], dict(zip(BIG_NAMES, got[1:]))
    reduced = {k: tree_sum(f"chip_sum_{k}", chunks_in[k]) for k in BIG_NAMES}
    g_c_ctx = cctx_grad(cc_parts, jnp.broadcast_to(c_ctx[None], (8, D)))[0]

    got = xchg("share_layers", [("swap", "sib", [reduced[k], reduced[k]]) for k in BIG_NAMES])
    grads = {k: jnp.where(ci == 0, jnp.stack([reduced[k], r]), jnp.stack([r, reduced[k]])) for k, r in zip(BIG_NAMES, got)}

    dm16 = jnp.concatenate([take(dm_lat_all).transpose(1, 0, 2), dmc_loc[:, None, :], jnp.zeros((depth, 7, W_MOD_COLS), F32)], axis=1)
    grads["w_mod"] = mod_weight_grad(c16, dm16)
    flat_sum = small_sum.reshape(-1)
    g_b_mod = (flat_sum[:n_dm] + flat_sum[n_dm:2 * n_dm]).reshape(depth, 6 * D)
    rest_shapes = [w[k].shape for k in PACKED[2:]]
    n_rest = sum(int(np.prod(s)) for s in rest_shapes)
    for k, g in zip(PACKED, [g_c_ctx, g_b_mod] + _unpack(flat_sum[2 * n_dm:2 * n_dm + n_rest], rest_shapes)):
        grads[k] = g

    delta, new_m, new_v = {}, {}, {}
    for k in BIG_NAMES + ("w_mod",):
        view = lambda a: a.reshape(-1, a.shape[-1])
        d_, m_, v_ = adamw(f"adamw_{k}", view(w[k]), view(grads[k]), view(m1[k]), view(m2[k]))
        delta[k], new_m[k], new_v[k] = d_.reshape(w[k].shape), m_.reshape(w[k].shape), v_.reshape(w[k].shape)
    shapes = [w[k].shape for k in PACKED]
    d_, m_, v_ = adamw("adamw_small", _pack([w[k] for k in PACKED]), _pack([grads[k] for k in PACKED]),
                       _pack([m1[k] for k in PACKED]), _pack([m2[k] for k in PACKED]))
    for k, dk, mk, vk in zip(PACKED, _unpack(d_, shapes), _unpack(m_, shapes), _unpack(v_, shapes)):
        delta[k], new_m[k], new_v[k] = dk, mk, vk
    return (loss, grad_x[None], *[grads[k] for k in WEIGHT_NAMES], *[delta[k] for k in WEIGHT_NAMES],
            *[new_m[k] for k in WEIGHT_NAMES], *[new_v[k] for k in WEIGHT_NAMES])
```

```python
import functools
import math

import numpy as np
import jax
import jax.numpy as jnp
from jax import lax
from jax.experimental import pallas as pl
from jax.experimental.pallas import tpu as pltpu

F32 = jnp.float32
BF16 = jnp.bfloat16
HI = lax.Precision.HIGHEST
EPS = 1e-6
VMEM_LIMIT_BYTES = 56 * 1024 * 1024
LANES = 128

D = 1024
D_FF = 4096
CTX = 256
GRID_W = 64
SGU_CHUNK = 128
GLA_CHUNK = 64
GLA_TAU = 16.0
GLA_DK = 32
MLA_SCALE = (128 + 64) ** -0.5
SCORE_SCALE = MLA_SCALE * math.log2(math.e)
LN2 = math.log(2.0)
ROPE_BASE = 10000.0
TM = 256
NCTXB = CTX // TM
P_GV, P_CKV, P_SU, P_SV, P_GR, P_DQ, P_GK, P_GATE, P_KR, P_GQ = 0, 256, 512, 768, 1024, 1280, 1536, 1664, 1792, 1920
P_COLS = 2048
IN_GROUPS = ((0, 128, P_GK), (128, 256, P_GV), (384, 32, P_GATE), (416, 256, P_CKV), (672, 64, P_KR),
             (736, 256, P_SU), (992, 256, P_SV), (1248, 128, P_GQ), (1376, 256, P_GR), (1632, 256, P_DQ))
ADAM_LR, ADAM_B1, ADAM_B2, ADAM_EPS, ADAM_WD, ADAM_STEP = 0.001, 0.9, 0.999, 1e-08, 0.01, 10
MESH = pl.DeviceIdType.MESH


def _params(sem):
    return pltpu.CompilerParams(dimension_semantics=sem, vmem_limit_bytes=VMEM_LIMIT_BYTES)


def _pick(n, cands):
    for c in cands:
        if n % c == 0:
            return c
    return n


class Op:
    def __init__(self, arr, blk, idx, gshape, gidx, acc):
        self.arr, self.blk, self.idx, self.gshape, self.gidx, self.acc = arr, blk, idx, gshape, gidx, acc

    def spec(self):
        return pl.BlockSpec(self.blk, self.idx)


def rows(arr, width=None, cb=0, off=0, tm=TM):
    w = arr.shape[1] if width is None else width
    n = arr.shape[0] - off * tm
    return Op(arr, (tm, w), lambda i: (i + off, cb), (n, w), lambda i: (i, 0), False)


def chunks(arr, per_tile):
    z = (0,) * (arr.ndim - 1)
    return Op(arr, (per_tile,) + arr.shape[1:], lambda i: (i,) + z, arr.shape, lambda i: (i,) + z, False)


def const(arr):
    z = (0,) * arr.ndim
    return Op(arr, arr.shape, lambda i: z, arr.shape, lambda i: z, True)


def rw(name, fn, ins, outs, grid):
    nin = len(ins)

    def body(*refs):
        vals = [r[...] for r in refs[:nin]]
        res = fn(pl.program_id(0), *vals)
        for o, r in zip(refs[nin:], res):
            o[...] = r.astype(o.dtype)

    return pl.pallas_call(
        body, name=name, grid=(grid,),
        in_specs=[o.spec() for o in ins],
        out_specs=[pl.BlockSpec(b, ix) for (_, _, b, ix) in outs],
        out_shape=[jax.ShapeDtypeStruct(s, d) for (s, d, _, _) in outs],
        compiler_params=_params(("arbitrary",)),
    )(*[o.arr for o in ins])


def rowout(n, w, dtype, tm=TM):
    return ((n, w), dtype, (tm, w), lambda i: (i, 0))


def chunkout(shape, dtype, per_tile):
    z = (0,) * (len(shape) - 1)
    return (shape, dtype, (per_tile,) + tuple(shape[1:]), lambda i: (i,) + z)


def rw_vjp(name, fn, ins, cots, wrt, grid, gdt=None, adds=None):
    nin = len(ins)
    cot_ops = [c for c in cots if c is not None]
    add_items = sorted((adds or {}).items())
    gdt = gdt or [F32] * len(wrt)
    ncot, nadd = len(cot_ops), len(add_items)

    def body(*refs):
        i = pl.program_id(0)
        vals = [r[...] for r in refs[:nin]]
        cvals = [r[...] for r in refs[nin:nin + ncot]]
        avals = [r[...] for r in refs[nin + ncot:nin + ncot + nadd]]
        grefs = refs[nin + ncot + nadd:]

        def f(*d):
            a = list(vals)
            for k, dv in zip(wrt, d):
                a[k] = dv
            return tuple(fn(i, *a))

        outs, vf = jax.vjp(f, *[vals[k] for k in wrt])
        it = iter(cvals)
        ct = tuple(jnp.zeros_like(o) if c is None else next(it).astype(o.dtype) for c, o in zip(cots, outs))
        gs = list(vf(ct))
        for (pos, _), av in zip(add_items, avals):
            gs[pos] = gs[pos].astype(F32) + av.astype(F32)
        for pos, (k, g, gref) in enumerate(zip(wrt, gs, grefs)):
            if ins[k].acc:
                @pl.when(i == 0)
                def _():
                    gref[...] = jnp.zeros_like(gref)
                gref[...] += g.astype(gref.dtype)
            else:
                gref[...] = g.astype(gref.dtype)

    all_in = list(ins) + cot_ops + [op for _, op in add_items]
    return pl.pallas_call(
        body, name=name, grid=(grid,),
        in_specs=[o.spec() for o in all_in],
        out_specs=[pl.BlockSpec(ins[k].blk, ins[k].gidx) for k in wrt],
        out_shape=[jax.ShapeDtypeStruct(ins[k].gshape, dt) for k, dt in zip(wrt, gdt)],
        compiler_params=_params(("arbitrary",)),
    )(*[o.arr for o in all_in])


MM_VMEM_BUDGET = 40 * 1024 * 1024
MM_COLS = 1024


def _square_bf16(a):
    a = a.astype(F32)
    return (a * a).astype(BF16)


def mm(name, a, b, out_dtype, pre=None, post=None, extras=(), bt=False):
    m, k = a.shape
    n = b.shape[0] if bt else b.shape[1]
    nc = min(n, MM_COLS)
    row_bytes = k * a.dtype.itemsize + n * jnp.dtype(out_dtype).itemsize + sum(n * e.dtype.itemsize for e in extras)
    tm = next(t for t in (768, 512, 384, 256, 128, 64)
              if m % t == 0 and 2 * t * row_bytes + 2 * k * n * b.dtype.itemsize + t * nc * 4 <= MM_VMEM_BUDGET)

    def body(a_ref, b_ref, *rest):
        o_ref = rest[-1]
        av = a_ref[...]
        if pre is not None:
            av = pre(av)
        for j in range(n // nc):
            cs = slice(j * nc, (j + 1) * nc)
            if bt:
                acc = lax.dot_general(av, b_ref[cs, :], (((1,), (1,)), ((), ())), preferred_element_type=F32)
            else:
                acc = lax.dot_general(av, b_ref[:, cs], (((1,), (0,)), ((), ())), preferred_element_type=F32)
            if post is not None:
                acc = post(acc, *[e[:, cs] for e in rest[:-1]])
            o_ref[:, cs] = acc.astype(o_ref.dtype)

    row = lambda w: pl.BlockSpec((tm, w), lambda i: (i, 0))
    return pl.pallas_call(
        body, name=name, grid=(m // tm,),
        in_specs=[row(k), pl.BlockSpec(b.shape, lambda i: (0, 0))] + [row(n) for _ in extras],
        out_specs=row(n),
        out_shape=jax.ShapeDtypeStruct((m, n), out_dtype),
        compiler_params=_params(("arbitrary",)),
    )(a, b, *extras)


def mm_tn(name, a, b, pre=None):
    m, ka = a.shape
    _, nb = b.shape
    tm = _pick(m, (768, 512, 256))
    ta = _pick(ka, (2048, 1024, 512, 256, 128))
    tb = _pick(nb, tuple(t for t in (4096, 2048, 1024, 512, 256, 128) if ta * t * 4 <= 8 * 1024 * 1024))

    def body(a_ref, b_ref, o_ref):
        @pl.when(pl.program_id(2) == 0)
        def _():
            o_ref[...] = jnp.zeros_like(o_ref)
        av = a_ref[...] if pre is None else pre(a_ref[...])
        o_ref[...] += lax.dot_general(av, b_ref[...], (((0,), (0,)), ((), ())), preferred_element_type=F32)

    return pl.pallas_call(
        body, name=name, grid=(ka // ta, nb // tb, m // tm),
        in_specs=[pl.BlockSpec((tm, ta), lambda i, j, k: (k, i)), pl.BlockSpec((tm, tb), lambda i, j, k: (k, j))],
        out_specs=pl.BlockSpec((ta, tb), lambda i, j, k: (i, j)),
        out_shape=jax.ShapeDtypeStruct((ka, nb), F32),
        compiler_params=_params(("arbitrary", "arbitrary", "arbitrary")),
    )(a, b)


def _rms(x, w):
    return x * lax.rsqrt(jnp.mean(x * x, axis=-1, keepdims=True) + EPS) * w


def _mod_of(blk, m):
    return jnp.where(blk < NCTXB, m[0], m[1])


def _gelu(x):
    return x * (0.5 * (1.0 + jnp.tanh(math.sqrt(2.0 / math.pi) * (x + 0.044715 * (x * x * x)))))


def _sigmoid(x):
    return 1.0 / (1.0 + jnp.exp(-x))


def _log_sigmoid(z):
    return jnp.minimum(z, 0.0) - jnp.log(1.0 + jnp.exp(-jnp.abs(z)))


def _dot(a, b, dims=((1,), (0,)), precision=None):
    return lax.dot_general(a, b, (dims, ((), ())), precision=precision, preferred_element_type=F32)


def _lane_group_mask(width, group, h):
    lane = lax.broadcasted_iota(jnp.int32, (1, width), 1)
    return (lane >= h * group) & (lane < (h + 1) * group)


def fn_norm1(blk, x, m, nw):
    mv = _mod_of(blk, m)
    return ((_rms(x, nw) * (1.0 + mv[:, D:2 * D]) + mv[:, 0:D]),)


def fn_res_norm2(blk, x, yo, m, nw):
    mv = _mod_of(blk, m)
    x1 = x + mv[:, 2 * D:3 * D] * yo
    return x1, _rms(x1, nw) * (1.0 + mv[:, 4 * D:5 * D]) + mv[:, 3 * D:4 * D]


def fn_res2(blk, x1, f, m):
    mv = _mod_of(blk, m)
    return (x1 + mv[:, 5 * D:6 * D] * f,)


def fn_sgu(blk, su, sv, nw, nb, ws, bm):
    u = _gelu(su)
    g = _gelu(sv)
    mu = jnp.mean(g, axis=-1, keepdims=True)
    var = jnp.mean(jnp.square(g - mu), axis=-1, keepdims=True)
    v = (g - mu) * lax.rsqrt(var + EPS) * nw + nb
    out = []
    for c in range(su.shape[0] // SGU_CHUNK):
        vc = v[c * SGU_CHUNK:(c + 1) * SGU_CHUNK]
        s = bm
        for h in range(4):
            vh = jnp.where(_lane_group_mask(256, 64, h), vc, 0.0)
            s = s + _dot(ws[h].astype(BF16), vh.astype(BF16))
        out.append(u[c * SGU_CHUNK:(c + 1) * SGU_CHUNK] * s)
    return (jnp.concatenate(out, axis=0),)


def fn_gates(blk, pg, wg, bg):
    z = _dot(pg.astype(BF16), wg.astype(BF16)) + bg
    g = _log_sigmoid(z) * (1.0 / GLA_TAU)
    return g[:, :128], g[:, 128:]


def _gla_chunk_terms(g, rev):
    r = lax.broadcasted_iota(jnp.int32, (GLA_CHUNK, GLA_CHUNK), 0)
    c = lax.broadcasted_iota(jnp.int32, (GLA_CHUNK, GLA_CHUNK), 1)
    tri = jnp.where((c >= r) if rev else (c <= r), 1.0, 0.0).astype(F32)
    b = _dot(tri, g, precision=HI)
    return b, jnp.sum(g, axis=0, keepdims=True)


def _bd_mask():
    r = lax.broadcasted_iota(jnp.int32, (128, 256), 0)
    c = lax.broadcasted_iota(jnp.int32, (128, 256), 1)
    return (r // GLA_DK) == (c // 64)


def _gla_kv_chunk(k, v, g, rev):
    b, tot = _gla_chunk_terms(g, rev)
    kd = k * jnp.exp(tot - b)
    u = jnp.where(_bd_mask(), _dot(kd.astype(BF16), v.astype(BF16), ((0,), (0,))), 0.0)
    r = lax.broadcasted_iota(jnp.int32, (128, 128), 0)
    c = lax.broadcasted_iota(jnp.int32, (128, 128), 1)
    col = jnp.sum(jnp.where(r == c, jnp.broadcast_to(jnp.exp(tot), (128, 128)), 0.0), axis=1, keepdims=True)
    return u, jnp.broadcast_to(col, (128, 128))


def _gla_o_chunk(q, k, v, g, s, rev):
    b, _ = _gla_chunk_terms(g, rev)
    qe = q * jnp.exp(b) * (GLA_DK ** -0.5)
    ke = k * jnp.exp(-b)
    o = _dot(qe.astype(BF16), jnp.where(_bd_mask(), s, 0.0).astype(BF16))
    qs = jnp.concatenate([jnp.where(_lane_group_mask(128, GLA_DK, h), qe, 0.0) for h in range(4)], axis=0)
    a = _dot(qs.astype(BF16), ke.astype(BF16), ((1,), (1,)))
    i = lax.broadcasted_iota(jnp.int32, a.shape, 0) % GLA_CHUNK
    j = lax.broadcasted_iota(jnp.int32, a.shape, 1)
    a = jnp.where((j >= i) if rev else (j <= i), a, 0.0)
    av = _dot(a.astype(BF16), v.astype(BF16))
    for h in range(4):
        o = o + jnp.where(_lane_group_mask(256, 64, h), av[GLA_CHUNK * h:GLA_CHUNK * (h + 1)], 0.0)
    return o


def fn_gla_kv(blk, k, v, gf, gb):
    uf, ef, ub, eb = [], [], [], []
    for c in range(k.shape[0] // GLA_CHUNK):
        sl = slice(c * GLA_CHUNK, (c + 1) * GLA_CHUNK)
        u, e = _gla_kv_chunk(k[sl], v[sl], gf[sl], False)
        uf.append(u[None]); ef.append(e[None])
        u, e = _gla_kv_chunk(k[sl], v[sl], gb[sl], True)
        ub.append(u[None]); eb.append(e[None])
    cat = lambda t: jnp.concatenate(t, axis=0)
    return cat(uf), cat(ef), cat(ub), cat(eb)


def fn_gla_o(blk, q, k, v, gf, gb, gr, sf, sb, nwt):
    out = []
    for c in range(q.shape[0] // GLA_CHUNK):
        sl = slice(c * GLA_CHUNK, (c + 1) * GLA_CHUNK)
        out.append(_gla_o_chunk(q[sl], k[sl], v[sl], gf[sl], sf[c], False)
                   + _gla_o_chunk(q[sl], k[sl], v[sl], gb[sl], sb[c], True))
    o = jnp.concatenate(out, axis=0)
    r = lax.broadcasted_iota(jnp.int32, (256, 256), 0)
    c = lax.broadcasted_iota(jnp.int32, (256, 256), 1)
    head_mean = jnp.where((r // 64) == (c // 64), 1.0 / 64.0, 0.0).astype(F32)
    ms = _dot(o * o, head_mean, precision=HI)
    on = o * lax.rsqrt(ms + EPS) * nwt
    return (on * (gr * _sigmoid(gr)),)


def _rope_partner(x):
    lane = lax.broadcasted_iota(jnp.int32, x.shape, 1)
    return jnp.where((lane // 16) % 2 == 0, pltpu.roll(x, LANES - 16, 1), pltpu.roll(x, 16, 1))


@jax.custom_vjp
def _rope(x, cs, sn):
    return x * cs + _rope_partner(x) * sn


def _rope_fwd(x, cs, sn):
    return _rope(x, cs, sn), (cs, sn)


def _rope_bwd(res, dy):
    cs, sn = res
    return dy * cs + _rope_partner(dy * sn), jnp.zeros_like(cs), jnp.zeros_like(sn)


_rope.defvjp(_rope_fwd, _rope_bwd)


def fn_mla_pre(blk, ckv, dq, kvw, qw):
    return _rms(ckv, kvw), _rms(dq, qw)


def fn_mla_post(blk, kk, qu, kr, cs, sn):
    kro = _rope(kr, cs, sn)
    kcat, q = [], []
    for h in range(4):
        kcat += [kk[:, 128 * h:128 * (h + 1)].astype(F32), kro]
        q += [qu[:, 256 * h:256 * h + 128], _rope(qu[:, 256 * h + 128:256 * (h + 1)], cs, sn)]
    return jnp.concatenate(kcat, axis=1), jnp.concatenate(q, axis=1) * SCORE_SCALE


ATTN_ROWS = 256
NEG = -1e30


def _scores(q, k, k0, masked):
    s = _dot(q, k, ((1,), (1,)))
    if masked:
        col = k0 + lax.broadcasted_iota(jnp.int32, s.shape, 1)
        s = jnp.where(col >= CTX, NEG, s)
    return s


def flash_fwd(q, kcat, kvu):
    t = q.shape[0]
    tq = _pick(t, (768, 512, 256))
    tk = _pick(t, (2816, 1536, 768, 512, 256))
    nsub = tq // ATTN_ROWS

    def body(q_ref, k_ref, v_ref, o_ref, lse_ref, m_sc, l_sc, acc_sc):
        qi, ki = pl.program_id(1), pl.program_id(2)

        @pl.when(ki == 0)
        def _():
            m_sc[...] = jnp.full_like(m_sc, NEG)
            l_sc[...] = jnp.zeros_like(l_sc)
            acc_sc[...] = jnp.zeros_like(acc_sc)

        def step(first_tile):
            k, v = k_ref[...], v_ref[...]
            for r in range(nsub):
                rs = pl.ds(r * ATTN_ROWS, ATTN_ROWS)
                s = _scores(q_ref[rs, :], k, ki * tk, first_tile and r == 0)
                m_old = m_sc[rs, :]
                m_new = jnp.maximum(m_old, jnp.max(s, axis=-1, keepdims=True))
                alpha = jnp.exp2(m_old - m_new)
                p = jnp.exp2(s - m_new)
                l_sc[rs, :] = alpha * l_sc[rs, :] + jnp.sum(p, axis=-1, keepdims=True)
                acc_sc[rs, :] = alpha * acc_sc[rs, :] + _dot(p.astype(BF16), v)
                m_sc[rs, :] = m_new

        pl.when(qi == 0)(lambda: step(True))
        pl.when(qi != 0)(lambda: step(False))

        @pl.when(ki == pl.num_programs(2) - 1)
        def _():
            o_ref[...] = acc_sc[...] / l_sc[...]
            lse_ref[...] = jnp.broadcast_to(m_sc[...] + jnp.log2(l_sc[...]), lse_ref.shape)

    return pl.pallas_call(
        body, name="mla_flash_fwd", grid=(4, t // tq, t // tk),
        in_specs=[pl.BlockSpec((tq, 256), lambda h, i, j: (i, h)), pl.BlockSpec((tk, 256), lambda h, i, j: (j, h)),
                  pl.BlockSpec((tk, 128), lambda h, i, j: (j, 4 + h))],
        out_specs=[pl.BlockSpec((tq, 128), lambda h, i, j: (i, h)), pl.BlockSpec((tq, 128), lambda h, i, j: (i, h))],
        out_shape=[jax.ShapeDtypeStruct((t, 512), F32), jax.ShapeDtypeStruct((t, 512), F32)],
        scratch_shapes=[pltpu.VMEM((tq, 1), F32), pltpu.VMEM((tq, 1), F32), pltpu.VMEM((tq, 128), F32)],
        compiler_params=_params(("arbitrary", "arbitrary", "arbitrary")),
    )(q, kcat, kvu)


def fn_attn_stats(blk, do, o, lse):
    out = []
    for h in range(4):
        hs = slice(128 * h, 128 * (h + 1))
        d = jnp.sum(do[:, hs] * o[:, hs], axis=-1, keepdims=True)
        out.append(lse[:, hs].T[0:8])
        out.append(jnp.broadcast_to(d, (do.shape[0], 128)).T[0:8])
    return (jnp.concatenate(out, axis=0)[None],)


def flash_bwd(q, kcat, kvu, dy, stats):
    t = q.shape[0]
    tq = _pick(t, (2816, 768, 512, 256))
    tk = _pick(t, (768, 512, 256))
    nst = tq // TM

    def body(q_ref, k_ref, v_ref, do_ref, st_ref, dq_ref, dk_ref, dv_ref):
        kj, qi = pl.program_id(1), pl.program_id(2)

        @pl.when(qi == 0)
        def _():
            dk_ref[...] = jnp.zeros_like(dk_ref)
            dv_ref[...] = jnp.zeros_like(dv_ref)

        q_, k, v, do = q_ref[...], k_ref[...], v_ref[...], do_ref[...].astype(BF16)
        lse_row = jnp.concatenate([st_ref[u, 0:1, :] for u in range(nst)], axis=1)
        delta_row = jnp.concatenate([st_ref[u, 8:9, :] for u in range(nst)], axis=1)
        s = _dot(k, q_, ((1,), (1,)))
        key = kj * tk + lax.broadcasted_iota(jnp.int32, s.shape, 0)
        qry = qi * tq + lax.broadcasted_iota(jnp.int32, s.shape, 1)
        s = jnp.where((qry < CTX) & (key >= CTX), NEG, s)
        p = jnp.exp2(s - lse_row)
        dp = _dot(v, do, ((1,), (1,)))
        ds = (p * (dp - delta_row)).astype(BF16)
        dv_ref[...] += _dot(p.astype(BF16), do)
        dk_ref[...] += LN2 * _dot(ds, q_)
        dq_new = LN2 * _dot(ds, k, ((0,), (0,)))
        rows_ = pl.ds(pl.multiple_of(qi * tq, TM), tq)

        @pl.when(kj == 0)
        def _():
            dq_ref[rows_, :] = dq_new

        @pl.when(kj != 0)
        def _():
            dq_ref[rows_, :] += dq_new

    return pl.pallas_call(
        body, name="mla_flash_bwd", grid=(4, t // tk, t // tq),
        in_specs=[pl.BlockSpec((tq, 256), lambda h, j, i: (i, h)), pl.BlockSpec((tk, 256), lambda h, j, i: (j, h)),
                  pl.BlockSpec((tk, 128), lambda h, j, i: (j, 4 + h)), pl.BlockSpec((tq, 128), lambda h, j, i: (i, 4 + h)),
                  pl.BlockSpec((nst, 16, 256), lambda h, j, i: (i, h, 0))],
        out_specs=[pl.BlockSpec((t, 256), lambda h, j, i: (0, h)), pl.BlockSpec((tk, 256), lambda h, j, i: (j, h)),
                   pl.BlockSpec((tk, 128), lambda h, j, i: (j, h))],
        out_shape=[jax.ShapeDtypeStruct((t, 1024), F32), jax.ShapeDtypeStruct((t, 1024), F32),
                   jax.ShapeDtypeStruct((t, 512), F32)],
        compiler_params=_params(("arbitrary", "arbitrary", "arbitrary")),
    )(q, kcat, kvu, dy, stats)


SCAN_BLOCK = CTX // GLA_CHUNK


def _scan_block(t, nb, rev):
    if not rev:
        return t
    return jnp.where(t < 1, 0, nb - t)


def _scan_order(rev):
    return tuple(reversed(range(SCAN_BLOCK))) if rev else tuple(range(SCAN_BLOCK))


def _both_halves(e):
    return jnp.concatenate([e, e], axis=1)


def gla_states(uf, ef, ub, eb):
    nb = uf.shape[0] // SCAN_BLOCK

    def body(uf_ref, ef_ref, ub_ref, eb_ref, sf_ref, sb_ref, sf_sc, sb_sc):
        @pl.when(pl.program_id(0) == 0)
        def _():
            sf_sc[...] = jnp.zeros_like(sf_sc)
            sb_sc[...] = jnp.zeros_like(sb_sc)

        for u_ref, e_ref, s_ref, sc, rev in ((uf_ref, ef_ref, sf_ref, sf_sc, False), (ub_ref, eb_ref, sb_ref, sb_sc, True)):
            s = sc[...]
            for c in _scan_order(rev):
                s_ref[c] = s
                s = _both_halves(e_ref[c]) * s + u_ref[c]
            sc[...] = s

    big = lambda rev: pl.BlockSpec((SCAN_BLOCK, 128, 256), lambda t: (_scan_block(t, nb, rev), 0, 0))
    small = lambda rev: pl.BlockSpec((SCAN_BLOCK, 128, 128), lambda t: (_scan_block(t, nb, rev), 0, 0))
    return pl.pallas_call(
        body, name="gla_states", grid=(nb,),
        in_specs=[big(False), small(False), big(True), small(True)],
        out_specs=[big(False), big(True)],
        out_shape=[jax.ShapeDtypeStruct(uf.shape, F32)] * 2,
        scratch_shapes=[pltpu.VMEM((128, 256), F32)] * 2,
        compiler_params=_params(("arbitrary",)),
    )(uf, ef, ub, eb)


def gla_states_bwd(ef, eb, sf, sb, dsf, dsb):
    nb = ef.shape[0] // SCAN_BLOCK

    def body(ef_ref, eb_ref, sf_ref, sb_ref, dsf_ref, dsb_ref, duf_ref, def_ref, dub_ref, deb_ref, gf_sc, gb_sc):
        @pl.when(pl.program_id(0) == 0)
        def _():
            gf_sc[...] = jnp.zeros_like(gf_sc)
            gb_sc[...] = jnp.zeros_like(gb_sc)

        for e_ref, s_ref, ds_ref, du_ref, de_ref, g_sc, rev in ((ef_ref, sf_ref, dsf_ref, duf_ref, def_ref, gf_sc, False),
                                                                 (eb_ref, sb_ref, dsb_ref, dub_ref, deb_ref, gb_sc, True)):
            g = g_sc[...]
            for k in reversed(_scan_order(rev)):
                du_ref[k] = g
                gs = g * s_ref[k]
                de_ref[k] = gs[:, :128] + gs[:, 128:]
                g = _both_halves(e_ref[k]) * g + ds_ref[k]
            g_sc[...] = g

    big = lambda rev: pl.BlockSpec((SCAN_BLOCK, 128, 256), lambda t: (_scan_block(nb - 1 - t, nb, rev), 0, 0))
    small = lambda rev: pl.BlockSpec((SCAN_BLOCK, 128, 128), lambda t: (_scan_block(nb - 1 - t, nb, rev), 0, 0))
    return pl.pallas_call(
        body, name="gla_states_bwd", grid=(nb,),
        in_specs=[small(False), small(True), big(False), big(True), big(False), big(True)],
        out_specs=[big(False), small(False), big(True), small(True)],
        out_shape=[jax.ShapeDtypeStruct(sf.shape, F32), jax.ShapeDtypeStruct(ef.shape, F32)] * 2,
        scratch_shapes=[pltpu.VMEM((128, 256), F32)] * 2,
        compiler_params=_params(("arbitrary",)),
    )(ef, eb, sf, sb, dsf, dsb)


def loss_head(xt, target, fnw):
    t = xt.shape[0]

    def f(x, tg, w):
        y = _rms(x, w)
        return 0.5 * jnp.sum(jnp.square(y - tg)) * (1.0 / D)

    def body(x_ref, t_ref, w_ref, loss_ref, dx_ref, dw_ref):
        i = pl.program_id(0)

        @pl.when(i == 0)
        def _():
            loss_ref[...] = jnp.zeros_like(loss_ref)
            dw_ref[...] = jnp.zeros_like(dw_ref)

        @pl.when(i < NCTXB)
        def _():
            dx_ref[...] = jnp.zeros_like(dx_ref)

        @pl.when(i >= NCTXB)
        def _():
            val, (dx, dw) = jax.value_and_grad(f, argnums=(0, 2))(x_ref[...], t_ref[...], w_ref[...])
            loss_ref[...] += jnp.broadcast_to(val, loss_ref.shape)
            dx_ref[...] = dx
            dw_ref[...] += dw

    return pl.pallas_call(
        body, name="loss_head", grid=(t // TM,),
        in_specs=[pl.BlockSpec((TM, D), lambda i: (i, 0)), pl.BlockSpec((TM, D), lambda i: (jnp.maximum(i - NCTXB, 0), 0)),
                  pl.BlockSpec((1, D), lambda i: (0, 0))],
        out_specs=[pl.BlockSpec((1, 128), lambda i: (0, 0)), pl.BlockSpec((TM, D), lambda i: (i, 0)),
                   pl.BlockSpec((1, D), lambda i: (0, 0))],
        out_shape=[jax.ShapeDtypeStruct((1, 128), F32), jax.ShapeDtypeStruct((t, D), F32), jax.ShapeDtypeStruct((1, D), F32)],
        compiler_params=_params(("arbitrary",)),
    )(xt, target, fnw)


def _in_to_padded(w):
    out, pos = [], 0
    for src, wd, dst in sorted(IN_GROUPS, key=lambda g: g[2]):
        if dst > pos:
            out.append(jnp.zeros((w.shape[0], dst - pos), w.dtype))
        out.append(w[:, src:src + wd])
        pos = dst + wd
    if pos < P_COLS:
        out.append(jnp.zeros((w.shape[0], P_COLS - pos), w.dtype))
    return jnp.concatenate(out, axis=1)


def _in_from_padded(g):
    return jnp.concatenate([g[:, dst:dst + wd] for _, wd, dst in IN_GROUPS], axis=1)


def _uq_to_padded(w):
    return jnp.pad(w.reshape(256, 4, 192), ((0, 0), (0, 0), (0, 64))).reshape(256, 1024)


def _uq_from_padded(g):
    return g.reshape(256, 4, 256)[:, :, :192].reshape(256, 768)


def _ukv_to_padded(w):
    return w.reshape(256, 4, 2, 128).transpose(0, 2, 1, 3).reshape(256, 1024)


def _ukv_from_padded(g):
    return g.reshape(256, 2, 4, 128).transpose(0, 2, 1, 3).reshape(256, 1024)


def _rope_tables(n):
    freq = ROPE_BASE ** (-jnp.arange(16, dtype=F32) * 2.0 / 32.0)
    grid_h = n // GRID_W
    ar = jnp.repeat(jnp.arange(grid_h, dtype=F32)[:, None] * freq[None, :], GRID_W, axis=0)
    ac = jnp.tile(jnp.arange(GRID_W, dtype=F32)[:, None] * freq[None, :], (grid_h, 1))
    z = jnp.zeros((n, 64), F32)
    cs = jnp.concatenate([jnp.cos(ar), jnp.cos(ar), jnp.cos(ac), jnp.cos(ac), z], axis=1)
    sn = jnp.concatenate([-jnp.sin(ar), jnp.sin(ar), -jnp.sin(ac), jnp.sin(ac), z], axis=1)
    cs_c = jnp.concatenate([jnp.ones((CTX, 64), F32), jnp.zeros((CTX, 64), F32)], axis=1)
    return jnp.concatenate([cs_c, cs], axis=0), jnp.concatenate([jnp.zeros((CTX, 128), F32), sn], axis=0)


def _small_views(sp):
    wg = jnp.concatenate([jnp.pad(sp["gla_wg_fwd"], ((0, 112), (0, 0))), jnp.pad(sp["gla_wg_bwd"], ((16, 96), (0, 0)))], axis=1)
    return dict(
        n1w=sp["norm1_w"][None], n2w=sp["norm2_w"][None],
        sgu_nw=sp["sgu_norm_w"][None], sgu_nb=sp["sgu_norm_b"][None], sgu_w=sp["sgu_w"],
        sgu_bm=jnp.repeat(sp["sgu_b"].T, 64, axis=1),
        wg=wg, bg=jnp.concatenate([sp["gla_bg_fwd"], sp["gla_bg_bwd"]])[None],
        gla_nwt=jnp.tile(sp["gla_norm_w"], 4)[None],
        kvw=sp["mla_kv_norm_w"][None], qw=sp["mla_q_norm_w"][None])


def _small_grads(g):
    return dict(
        norm1_w=g["n1w"][0], norm2_w=g["n2w"][0],
        sgu_norm_w=g["sgu_nw"][0], sgu_norm_b=g["sgu_nb"][0], sgu_w=g["sgu_w"],
        sgu_b=g["sgu_bm"].reshape(128, 4, 64).sum(-1).T,
        gla_wg_fwd=g["wg"][0:16, 0:128], gla_wg_bwd=g["wg"][16:32, 128:256],
        gla_bg_fwd=g["bg"][0, 0:128], gla_bg_bwd=g["bg"][0, 128:256],
        gla_norm_w=g["gla_nwt"].reshape(4, 64).sum(0),
        mla_kv_norm_w=g["kvw"][0], mla_q_norm_w=g["qw"][0])


def _big_views(w_in, w_out, w_uq, w_ukv, w_ff1, w_ff2):
    win, wuq, wukv = _in_to_padded(w_in), _uq_to_padded(w_uq), _ukv_to_padded(w_ukv)
    return dict(win=win, wuq=wuq, wukv=wukv, wout=w_out, w1=w_ff1, w2=w_ff2)


def _gla_tile(t):
    return _pick(t, (768, 512, 256))


def _layer_ops(p, sv, a):
    pc = lambda off, w, tm=TM: rows(p, w, off // w, tm=tm)
    gt = _gla_tile(p.shape[0])
    gr = lambda arr: rows(arr, tm=gt)
    return dict(
        sgu=[pc(P_SU, 256), pc(P_SV, 256), const(sv["sgu_nw"]), const(sv["sgu_nb"]), const(sv["sgu_w"]), const(sv["sgu_bm"])],
        gates=[pc(P_GATE, 128), const(sv["wg"]), const(sv["bg"])],
        mla_pre=[pc(P_CKV, 256), pc(P_DQ, 256), const(sv["kvw"]), const(sv["qw"])],
        gla_kv=lambda: [pc(P_GK, 128, gt), pc(P_GV, 256, gt), gr(a["gf"]), gr(a["gb"])],
        gla_o=lambda: [pc(P_GQ, 128, gt), pc(P_GK, 128, gt), pc(P_GV, 256, gt), gr(a["gf"]), gr(a["gb"]), pc(P_GR, 256, gt),
                       chunks(a["sf"], gt // GLA_CHUNK), chunks(a["sb"], gt // GLA_CHUNK), const(sv["gla_nwt"])],
        mla_post=lambda: [rows(a["kvu"], 512, 0), rows(a["qu"]), pc(P_KR, 128), rows(a["cs"]), rows(a["sn"])])


def layer_fwd(l, xt, modl, bw, sv, tabs):
    t = xt.shape[0]
    g, nc, gt = t // TM, t // GLA_CHUNK, _gla_tile(t)
    gg, cpt = t // gt, gt // GLA_CHUNK
    nm = lambda s: f"l{l}_{s}"
    a = dict(x=xt, cs=tabs[0], sn=tabs[1])
    a["h"], = rw(nm("norm1"), fn_norm1, [rows(xt), const(modl), const(sv["n1w"])], [rowout(t, D, BF16)], g)
    p = a["p"] = mm(nm("in_proj"), a["h"], bw["win"], F32)
    ops = _layer_ops(p, sv, a)
    y_sgu, = rw(nm("sgu"), fn_sgu, ops["sgu"], [rowout(t, 256, BF16)], g)
    a["gf"], a["gb"] = rw(nm("gates"), fn_gates, ops["gates"], [rowout(t, 128, F32)] * 2, g)
    a["uf"], a["ef"], a["ub"], a["eb"] = rw(nm("gla_kv"), fn_gla_kv, ops["gla_kv"](),
                                           [chunkout((nc, 128, 256), F32, cpt), chunkout((nc, 128, 128), F32, cpt)] * 2, gg)
    a["sf"], a["sb"] = gla_states(a["uf"], a["ef"], a["ub"], a["eb"])
    y_gla, = rw(nm("gla_o"), fn_gla_o, ops["gla_o"](), [rowout(t, 256, BF16, tm=gt)], gg)
    a["ckvn"], a["dqn"] = rw(nm("mla_pre"), fn_mla_pre, ops["mla_pre"], [rowout(t, 256, BF16)] * 2, g)
    a["kvu"] = mm(nm("kv_up"), a["ckvn"], bw["wukv"], BF16)
    a["qu"] = mm(nm("q_up"), a["dqn"], bw["wuq"], F32)
    a["kcat"], a["q"] = rw(nm("mla_post"), fn_mla_post, ops["mla_post"](), [rowout(t, 1024, BF16)] * 2, g)
    a["o"], a["lse"] = flash_fwd(a["q"], a["kcat"], a["kvu"])
    a["y"] = jnp.concatenate([y_sgu, y_gla, a["o"].astype(BF16)], axis=1)
    a["yo"] = mm(nm("out_proj"), a["y"], bw["wout"], F32)
    a["x1"], a["h2"] = rw(nm("res_norm2"), fn_res_norm2, [rows(xt), rows(a["yo"]), const(modl), const(sv["n2w"])],
                          [rowout(t, D, F32), rowout(t, D, BF16)], g)
    a["act"] = mm(nm("ff1"), a["h2"], bw["w1"], BF16, post=lambda acc: jnp.maximum(acc, 0.0))
    a["f"] = mm(nm("ff2"), a["act"], bw["w2"], F32, pre=_square_bf16)
    x2, = rw(nm("res2"), fn_res2, [rows(a["x1"]), rows(a["f"]), const(modl)], [rowout(t, D, F32)], g)
    return x2, a


def fn_assemble(blk, gv1, gv2, ckv, su, sv_, gr, dq, gk1, gk2, pg, kr, gq):
    return (jnp.concatenate([gv1 + gv2, ckv, su, sv_, gr, dq, gk1 + gk2, pg, kr, gq], axis=1),)


def layer_bwd(l, dx2, a, modl, bw, sv):
    t = dx2.shape[0]
    g, gt = t // TM, _gla_tile(t)
    gg, cpt = t // gt, gt // GLA_CHUNK
    nm = lambda s: f"l{l}_{s}_bwd"
    p = a["p"]
    ops = _layer_ops(p, sv, a)
    gw, gs = {}, {}
    df, dm_a = rw_vjp(nm("res2"), fn_res2, [rows(a["x1"]), rows(a["f"]), const(modl)], [rows(dx2)], [1, 2], g,
                      gdt=[BF16, F32])
    gw["w2"] = mm_tn(nm("ff2_w"), a["act"], df, pre=_square_bf16)
    du = mm(nm("ff2_x"), df, bw["w2"], BF16, post=lambda acc, act: acc * (2.0 * act.astype(F32)), extras=(a["act"],), bt=True)
    gw["w1"] = mm_tn(nm("ff1_w"), a["h2"], du)
    dh2 = mm(nm("ff1_x"), du, bw["w1"], F32, bt=True)
    dxa, dyo, dm_b, gs["n2w"] = rw_vjp(nm("res_norm2"), fn_res_norm2,
                                       [rows(a["x"]), rows(a["yo"]), const(modl), const(sv["n2w"])],
                                       [rows(dx2), rows(dh2)], [0, 1, 2, 3], g, gdt=[F32, BF16, F32, F32])
    gw["wout"] = mm_tn(nm("out_w"), a["y"], dyo)
    dy = mm(nm("out_x"), dyo, bw["wout"], F32, bt=True)
    dsu, dsv, gs["sgu_nw"], gs["sgu_nb"], gs["sgu_w"], gs["sgu_bm"] = rw_vjp(
        nm("sgu"), fn_sgu, ops["sgu"], [rows(dy, 256, 0)], [0, 1, 2, 3, 4, 5], g)
    dgq, dgk1, dgv1, dgf1, dgb1, dgr, dsf, dsb, gs["gla_nwt"] = rw_vjp(
        nm("gla_o"), fn_gla_o, ops["gla_o"](), [rows(dy, 256, 1, tm=gt)], list(range(9)), gg)
    duf, def_, dub, deb = gla_states_bwd(a["ef"], a["eb"], a["sf"], a["sb"], dsf, dsb)
    dgk2, dgv2, dgf, dgb = rw_vjp(nm("gla_kv"), fn_gla_kv, ops["gla_kv"](),
                                  [chunks(duf, cpt), chunks(def_, cpt), chunks(dub, cpt), chunks(deb, cpt)], [0, 1, 2, 3], gg,
                                  adds={2: rows(dgf1, tm=gt), 3: rows(dgb1, tm=gt)})
    dpg, gs["wg"], gs["bg"] = rw_vjp(nm("gates"), fn_gates, ops["gates"], [rows(dgf), rows(dgb)], [0, 1, 2], g)
    stats, = rw(nm("attn_stats"), fn_attn_stats, [rows(dy, 512, 1), rows(a["o"]), rows(a["lse"])],
                [chunkout((g, 64, TM), F32, 1)], g)
    dq, dkcat, dv = flash_bwd(a["q"], a["kcat"], a["kvu"], dy, stats)
    dkk, dqu, dkr = rw_vjp(nm("mla_post"), fn_mla_post, ops["mla_post"](), [rows(dkcat), rows(dq)], [0, 1, 2], g,
                           gdt=[BF16, BF16, F32])
    dkvu = jnp.concatenate([dkk, dv.astype(BF16)], axis=1)
    gw["wukv"] = mm_tn(nm("kv_up_w"), a["ckvn"], dkvu)
    gw["wuq"] = mm_tn(nm("q_up_w"), a["dqn"], dqu)
    dckvn = mm(nm("kv_up_x"), dkvu, bw["wukv"], F32, bt=True)
    ddqn = mm(nm("q_up_x"), dqu, bw["wuq"], F32, bt=True)
    dckv, ddq, gs["kvw"], gs["qw"] = rw_vjp(nm("mla_pre"), fn_mla_pre, ops["mla_pre"], [rows(dckvn), rows(ddqn)],
                                            [0, 1, 2, 3], g)
    dp, = rw(nm("assemble"), fn_assemble,
             [rows(x_) for x_ in (dgv1, dgv2, dckv, dsu, dsv, dgr, ddq, dgk1, dgk2, dpg, dkr, dgq)],
             [rowout(t, P_COLS, BF16)], g)
    gw["win"] = mm_tn(nm("in_w"), a["h"], dp)
    dh = mm(nm("in_x"), dp, bw["win"], F32, bt=True)
    dx, dm_c, gs["n1w"] = rw_vjp(nm("norm1"), fn_norm1, [rows(a["x"]), const(modl), const(sv["n1w"])], [rows(dh)],
                                 [0, 1, 2], g, adds={0: rows(dxa)})
    big = dict(w_in=_in_from_padded(gw["win"]), w_out=gw["wout"], mla_w_uq=_uq_from_padded(gw["wuq"]),
               mla_w_ukv=_ukv_from_padded(gw["wukv"]), w_ff1=gw["w1"], w_ff2=gw["w2"])
    return dx, dm_a + dm_b + dm_c, big, _small_grads(gs)


SMALL_NAMES = ("norm1_w", "sgu_norm_w", "sgu_norm_b", "sgu_w", "sgu_b", "gla_wg_fwd", "gla_bg_fwd", "gla_wg_bwd",
               "gla_bg_bwd", "gla_norm_w", "mla_q_norm_w", "mla_kv_norm_w", "norm2_w")
BIG_NAMES = ("w_in", "w_out", "mla_w_uq", "mla_w_ukv", "w_ff1", "w_ff2")


def local_step(x, ctx, target, mods, big, small, final_norm_w):
    n = x.shape[0]
    xt = jnp.concatenate([ctx, x], axis=0)
    tabs = _rope_tables(n)
    depth = len(mods)
    bws = [_big_views(*[big[l][k] for k in BIG_NAMES]) for l in range(depth)]
    svs = [_small_views(small[l]) for l in range(depth)]
    acts = []
    for l in range(depth):
        xt, a = layer_fwd(l, xt, mods[l], bws[l], svs[l], tabs)
        acts.append(a)
    loss, dxt, dfnw = loss_head(xt, target, final_norm_w[None])
    dmods, gbig, gsmall = [None] * depth, [None] * depth, [None] * depth
    for l in reversed(range(depth)):
        dxt, dmods[l], gbig[l], gsmall[l] = layer_bwd(l, dxt, acts[l], mods[l], bws[l], svs[l])
    return loss, dxt[CTX:], dmods, gbig, gsmall, dfnw


def _group(group):
    x, y, c = lax.axis_index("x"), lax.axis_index("y"), lax.axis_index("c")
    if group == "sib":
        return 2, c, [((x, y, 1 - c), 1 - c)]
    if group == "chip":
        flips = [(1, 0), (0, 1), (1, 1)]
        return 4, 2 * x + y, [((x ^ fx, y ^ fy, c), 2 * (x ^ fx) + (y ^ fy)) for fx, fy in flips]
    flips = [(fx, fy, fc) for fx in (0, 1) for fy in (0, 1) for fc in (0, 1)][1:]
    return 8, 4 * x + 2 * y + c, [((x ^ fx, y ^ fy, c ^ fc), 4 * (x ^ fx) + 2 * (y ^ fy) + (c ^ fc)) for fx, fy, fc in flips]


def _group_size(group):
    return {"sib": 2, "chip": 4, "all": 8}[group]


REMOTE_COPIES = {"gather": None, "scatter": None, "swap": 1, "gather2": 6}


def xchg(name, entries):
    n_in = sum(len(arrs) for _, _, arrs in entries)
    n_remote = sum(REMOTE_COPIES[k] or _group_size(g) - 1 for k, g, _ in entries)
    n_local = sum(1 for k, _, _ in entries if k != "swap")
    out_shape = []
    for kind, group, arrs in entries:
        a = arrs[0]
        if kind in ("gather", "gather2"):
            out_shape.append(jax.ShapeDtypeStruct((_group_size(group),) + a.shape, a.dtype))
        else:
            out_shape.append(jax.ShapeDtypeStruct(a.shape, a.dtype))

    def body(*refs):
        in_refs, out_refs = refs[:n_in], refs[n_in:n_in + len(entries)]
        send_sems, recv_sems, local_sems = refs[n_in + len(entries):]
        x, y, c = lax.axis_index("x"), lax.axis_index("y"), lax.axis_index("c")

        def remote(src, dst, k, dev):
            return pltpu.make_async_remote_copy(src_ref=src, dst_ref=dst, send_sem=send_sems.at[k], recv_sem=recv_sems.at[k],
                                                device_id=dev, device_id_type=MESH)

        pos, k, kl = 0, 0, 0
        forwards, finals = [], []
        for (kind, group, arrs), out in zip(entries, out_refs):
            srcs = in_refs[pos:pos + len(arrs)]
            pos += len(arrs)
            _, mine, peers = _group(group)
            if kind == "swap":
                (dev, _), = peers
                for core, src in ((0, srcs[1]), (1, srcs[0])):
                    @pl.when(c == core)
                    def _(src=src, k=k, dev=dev, out=out):
                        remote(src, out, k, dev).start()
                finals.append(remote(srcs[0], out, k, dev).wait)
                k += 1
                continue
            src = srcs[0]
            own = pltpu.make_async_copy(src if kind != "scatter" else src.at[mine], out.at[mine], local_sems.at[kl])
            own.start()
            finals.append(own.wait)
            kl += 1
            if kind == "gather2":
                sibling = (x, y, 1 - c)
                for f, (dev, slot) in enumerate(peers):
                    remote(src.at[c], out.at[mine, c], k + f, dev).start()
                    arrival = remote(src.at[c], out.at[slot, c], k + f, dev)

                    def forward(arrival=arrival, slot=slot, kf=k + 3 + f, out=out):
                        arrival.wait_recv()
                        remote(out.at[slot, c], out.at[slot, c], kf, sibling).start()

                    forwards.append(forward)
                    finals.append(arrival.wait_send)
                    finals.append(remote(out.at[slot, c], out.at[slot, 1 - c], k + 3 + f, sibling).wait)
                k += 6
                continue
            for dev, slot in peers:
                piece = src if kind == "gather" else src.at[slot]
                remote(piece, out.at[mine], k, dev).start()
                finals.append(remote(piece, out.at[slot], k, dev).wait)
                k += 1
        for run in forwards + finals:
            run()

    any_spec = pl.BlockSpec(memory_space=pl.ANY)
    return pl.pallas_call(
        body, name=name,
        in_specs=[any_spec] * n_in, out_specs=[any_spec] * len(entries), out_shape=out_shape,
        scratch_shapes=[pltpu.SemaphoreType.DMA((n_remote,)), pltpu.SemaphoreType.DMA((n_remote,)),
                        pltpu.SemaphoreType.DMA((max(n_local, 1),))],
    )(*[a for _, _, arrs in entries for a in arrs])


def _block_rows(r, c, budget=131072):
    tr = 8
    while tr * 2 * c <= budget and r % (tr * 2) == 0:
        tr *= 2
    return tr if r % tr == 0 else r


def tree_sum(name, parts):
    g, r, c = parts.shape
    tr = _block_rows(r, c)

    def body(p_ref, o_ref):
        p = [p_ref[i].astype(F32) for i in range(g)]
        while len(p) > 1:
            p = [p[i] + p[i + 1] for i in range(0, len(p), 2)]
        o_ref[...] = p[0]

    return pl.pallas_call(
        body, name=name, grid=(r // tr,),
        in_specs=[pl.BlockSpec((g, tr, c), lambda i: (0, i, 0))], out_specs=pl.BlockSpec((tr, c), lambda i: (i, 0)),
        out_shape=jax.ShapeDtypeStruct((r, c), F32), compiler_params=_params(("arbitrary",)),
    )(parts)


def pair_sum(name, g0, g1, recv, core):
    r, c = recv.shape
    tr = _block_rows(r, c)

    def body(a_ref, b_ref, r_ref, k_ref, o_ref):
        o_ref[...] = jnp.where(k_ref[...] > 0.5, b_ref[...], a_ref[...]) + r_ref[...]

    blk = pl.BlockSpec((tr, c), lambda i: (i, 0))
    return pl.pallas_call(
        body, name=name, grid=(r // tr,), in_specs=[blk, blk, blk, pl.BlockSpec((1, 1), lambda i: (0, 0))],
        out_specs=blk, out_shape=jax.ShapeDtypeStruct((r, c), F32), compiler_params=_params(("arbitrary",)),
    )(g0, g1, recv, core)


def adamw(name, w, g, m, v):
    r, c = w.shape
    tr = _block_rows(r, c)

    def body(w_ref, g_ref, m_ref, v_ref, d_ref, nm_ref, nv_ref):
        gg = g_ref[...]
        nm = ADAM_B1 * m_ref[...] + (1.0 - ADAM_B1) * gg
        nv = ADAM_B2 * v_ref[...] + (1.0 - ADAM_B2) * jnp.square(gg)
        m_hat = nm / (1.0 - ADAM_B1 ** ADAM_STEP)
        v_hat = nv / (1.0 - ADAM_B2 ** ADAM_STEP)
        d_ref[...] = -ADAM_LR * (m_hat / (jnp.sqrt(v_hat) + ADAM_EPS) + ADAM_WD * w_ref[...])
        nm_ref[...] = nm
        nv_ref[...] = nv

    blk = pl.BlockSpec((tr, c), lambda i: (i, 0))
    return pl.pallas_call(
        body, name=name, grid=(r // tr,), in_specs=[blk] * 4, out_specs=[blk] * 3,
        out_shape=[jax.ShapeDtypeStruct((r, c), F32)] * 3, compiler_params=_params(("arbitrary",)),
    )(w, g, m, v)


W_MOD_COLS = 6 * D // 4
MOD_TN = 512


def mod_project(c16, w_mod, b_loc):
    def body(c_ref, w_ref, b_ref, o_ref):
        cv = c_ref[...]
        s = (cv * _sigmoid(cv)).astype(BF16)
        o_ref[0] = _dot(s, w_ref[0].astype(BF16)) + b_ref[0]

    return pl.pallas_call(
        body, name="mod_project", grid=(2, W_MOD_COLS // MOD_TN),
        in_specs=[pl.BlockSpec((16, D), lambda l, j: (0, 0)), pl.BlockSpec((1, D, MOD_TN), lambda l, j: (l, 0, j)),
                  pl.BlockSpec((1, 1, MOD_TN), lambda l, j: (l, 0, j))],
        out_specs=pl.BlockSpec((1, 16, MOD_TN), lambda l, j: (l, 0, j)),
        out_shape=jax.ShapeDtypeStruct((2, 16, W_MOD_COLS), F32), compiler_params=_params(("arbitrary", "arbitrary")),
    )(c16, w_mod, b_loc)


def mod_weight_grad(c16, dm16):
    def body(c_ref, d_ref, o_ref):
        cv = c_ref[...]
        o_ref[0] = _dot(cv * _sigmoid(cv), d_ref[0], ((0,), (0,)), precision=HI)

    return pl.pallas_call(
        body, name="mod_weight_grad", grid=(2, W_MOD_COLS // MOD_TN),
        in_specs=[pl.BlockSpec((16, D), lambda l, j: (0, 0)), pl.BlockSpec((1, 16, MOD_TN), lambda l, j: (l, 0, j))],
        out_specs=pl.BlockSpec((1, D, MOD_TN), lambda l, j: (l, 0, j)),
        out_shape=jax.ShapeDtypeStruct((2, D, W_MOD_COLS), F32), compiler_params=_params(("arbitrary", "arbitrary")),
    )(c16, dm16)


def cctx_partial(dmc, w_mod):
    def body(d_ref, w_ref, o_ref):
        @pl.when(pl.program_id(0) == 0)
        def _():
            o_ref[...] = jnp.zeros_like(o_ref)
        o_ref[...] += _dot(d_ref[0], w_ref[0], ((1,), (1,)), precision=HI)

    return pl.pallas_call(
        body, name="cctx_partial", grid=(2,),
        in_specs=[pl.BlockSpec((1, 8, W_MOD_COLS), lambda l: (l, 0, 0)), pl.BlockSpec((1, D, W_MOD_COLS), lambda l: (l, 0, 0))],
        out_specs=pl.BlockSpec((8, D), lambda l: (0, 0)),
        out_shape=jax.ShapeDtypeStruct((8, D), F32), compiler_params=_params(("arbitrary",)),
    )(dmc, w_mod)


def cctx_grad(parts, c_ctx8):
    def body(p_ref, c_ref, o_ref):
        ds = (p_ref[0] + p_ref[1]) + (p_ref[2] + p_ref[3])
        _, vf = jax.vjp(lambda z: z * _sigmoid(z), c_ref[...])
        o_ref[...] = vf(ds)[0]

    return pl.pallas_call(
        body, name="cctx_grad", out_shape=jax.ShapeDtypeStruct((8, D), F32),
    )(parts, c_ctx8)


ARG_NAMES = ("x", "c", "ctx", "c_ctx", "w_mod", "b_mod", "norm1_w", "w_in", "w_out", "sgu_norm_w", "sgu_norm_b", "sgu_w",
             "sgu_b", "gla_wg_fwd", "gla_bg_fwd", "gla_wg_bwd", "gla_bg_bwd", "gla_norm_w", "mla_q_norm_w", "mla_w_uq",
             "mla_kv_norm_w", "mla_w_ukv", "norm2_w", "w_ff1", "w_ff2", "final_norm_w")
WEIGHT_NAMES = ARG_NAMES[3:]
PACKED = ("c_ctx", "b_mod") + SMALL_NAMES + ("final_norm_w",)
ROW_SHARDED = ("w_out", "w_ff2")
PACK_ROWS = 256


def _pack(vectors):
    flat = jnp.concatenate([v.reshape(-1) for v in vectors])
    n = flat.shape[0]
    total = -(-n // (PACK_ROWS * LANES)) * PACK_ROWS * LANES
    return jnp.pad(flat, (0, total - n)).reshape(-1, LANES)


def _unpack(buf, shapes):
    flat, out, pos = buf.reshape(-1), [], 0
    for shp in shapes:
        n = int(np.prod(shp))
        out.append(flat[pos:pos + n].reshape(shp))
        pos += n
    return out


def _full_weight(name, gathered, l):
    g = gathered[:, l]
    if name in ROW_SHARDED:
        return g.reshape(-1, g.shape[-1])
    return g.transpose(1, 0, 2).reshape(g.shape[1], -1)


def _chip_chunks(name, a):
    if name in ROW_SHARDED:
        return a.reshape(4, a.shape[0] // 4, a.shape[1])
    return a.reshape(a.shape[0], 4, a.shape[1] // 4).transpose(1, 0, 2)


def kernel(x, c, ctx, c_ctx, w_mod, b_mod, norm1_w, w_in, w_out, sgu_norm_w, sgu_norm_b, sgu_w, sgu_b, gla_wg_fwd, gla_bg_fwd, gla_wg_bwd, gla_bg_bwd, gla_norm_w, mla_q_norm_w, mla_w_uq, mla_kv_norm_w, mla_w_ukv, norm2_w, w_ff1, w_ff2, final_norm_w, loss_target, m_c_ctx, m_w_mod, m_b_mod, m_norm1_w, m_w_in, m_w_out, m_sgu_norm_w, m_sgu_norm_b, m_sgu_w, m_sgu_b, m_gla_wg_fwd, m_gla_bg_fwd, m_gla_wg_bwd, m_gla_bg_bwd, m_gla_norm_w, m_mla_q_norm_w, m_mla_w_uq, m_mla_kv_norm_w, m_mla_w_ukv, m_norm2_w, m_w_ff1, m_w_ff2, m_final_norm_w, v_c_ctx, v_w_mod, v_b_mod, v_norm1_w, v_w_in, v_w_out, v_sgu_norm_w, v_sgu_norm_b, v_sgu_w, v_sgu_b, v_gla_wg_fwd, v_gla_bg_fwd, v_gla_wg_bwd, v_gla_bg_bwd, v_gla_norm_w, v_mla_q_norm_w, v_mla_w_uq, v_mla_kv_norm_w, v_mla_w_ukv, v_norm2_w, v_w_ff1, v_w_ff2, v_final_norm_w):
    args = (x, c, ctx, c_ctx, w_mod, b_mod, norm1_w, w_in, w_out, sgu_norm_w, sgu_norm_b, sgu_w, sgu_b, gla_wg_fwd, gla_bg_fwd, gla_wg_bwd, gla_bg_bwd, gla_norm_w, mla_q_norm_w, mla_w_uq, mla_kv_norm_w, mla_w_ukv, norm2_w, w_ff1, w_ff2, final_norm_w)
    w = dict(zip(ARG_NAMES, args))
    moms = (m_c_ctx, m_w_mod, m_b_mod, m_norm1_w, m_w_in, m_w_out, m_sgu_norm_w, m_sgu_norm_b, m_sgu_w, m_sgu_b, m_gla_wg_fwd, m_gla_bg_fwd, m_gla_wg_bwd, m_gla_bg_bwd, m_gla_norm_w, m_mla_q_norm_w, m_mla_w_uq, m_mla_kv_norm_w, m_mla_w_ukv, m_norm2_w, m_w_ff1, m_w_ff2, m_final_norm_w)
    vars_ = (v_c_ctx, v_w_mod, v_b_mod, v_norm1_w, v_w_in, v_w_out, v_sgu_norm_w, v_sgu_norm_b, v_sgu_w, v_sgu_b, v_gla_wg_fwd, v_gla_bg_fwd, v_gla_wg_bwd, v_gla_bg_bwd, v_gla_norm_w, v_mla_q_norm_w, v_mla_w_uq, v_mla_kv_norm_w, v_mla_w_ukv, v_norm2_w, v_w_ff1, v_w_ff2, v_final_norm_w)
    m1 = dict(zip(WEIGHT_NAMES, moms))
    m2 = dict(zip(WEIGHT_NAMES, vars_))
    xi, yi, ci = lax.axis_index("x"), lax.axis_index("y"), lax.axis_index("c")
    chip, dev = 2 * xi + yi, 4 * xi + 2 * yi + ci
    depth = w_mod.shape[0]

    got = xchg("gather_inputs", [("gather", "all", [c])] + [("gather2", "chip", [w[k].astype(BF16)]) for k in BIG_NAMES])
    c_all, shards = got[0], dict(zip(BIG_NAMES, got[1:]))
    c16 = jnp.concatenate([c_all.reshape(8, D), c_ctx[None], jnp.zeros((7, D), F32)], axis=0)
    b_loc = lax.dynamic_slice_in_dim(b_mod, chip * W_MOD_COLS, W_MOD_COLS, axis=1)[:, None, :]
    mod_part = mod_project(c16, w_mod, b_loc)
    mod_all, = xchg("gather_mod", [("gather", "chip", [mod_part])])
    mod_full = mod_all.transpose(1, 2, 0, 3).reshape(depth, 16, 6 * D)
    mods = [jnp.stack([mod_full[l, 8], lax.dynamic_index_in_dim(mod_full[l], dev, 0, keepdims=False)])[:, None, :]
            for l in range(depth)]

    big = [{k: _full_weight(k, shards[k], l) for k in BIG_NAMES} for l in range(depth)]
    small = [{k: w[k][l] for k in SMALL_NAMES} for l in range(depth)]
    loss, grad_x, dmods, gbig, gsmall, dfnw = local_step(x[0], ctx[0], loss_target[0], mods, big, small, final_norm_w)
    loss = lax.psum(loss[0, 0], ("x", "y", "c"))

    dm_lat = jnp.stack([dmods[l][1, 0] for l in range(depth)])
    dm_ctx = jnp.stack([dmods[l][0, 0] for l in range(depth)])
    small_pack = _pack([dm_lat, dm_ctx] + [jnp.stack([gsmall[l][k] for l in range(depth)]) for k in SMALL_NAMES] + [dfnw])
    got = xchg("exchange_grads", [("gather", "all", [small_pack])] + [("swap", "sib", [gbig[0][k], gbig[1][k]]) for k in BIG_NAMES])
    small_all, from_sib = got[0], dict(zip(BIG_NAMES, got[1:]))
    small_sum = tree_sum("small_grad_sum", small_all)
    core = ci.astype(F32).reshape(1, 1)
    mine = {k: pair_sum(f"pair_sum_{k}", gbig[0][k], gbig[1][k], from_sib[k], core) for k in BIG_NAMES}

    n_dm = depth * 6 * D
    dm_rows = n_dm // LANES
    dm_lat_all = small_all[:, :dm_rows].reshape(8, depth, 6 * D)
    dm_ctx_sum = small_sum[dm_rows:2 * dm_rows].reshape(depth, 6 * D)
    take = lambda a: lax.dynamic_slice_in_dim(a, chip * W_MOD_COLS, W_MOD_COLS, axis=-1)
    dmc_loc = take(dm_ctx_sum)
    cc_part = cctx_partial(jnp.pad(dmc_loc[:, None, :], ((0, 0), (0, 7), (0, 0))), w_mod)
    got = xchg("scatter_grads", [("gather", "chip", [cc_part])]
               + [("scatter", "chip", [_chip_chunks(k, mine[k]).astype(BF16)]) for k in BIG_NAMES])
    cc_parts, chunks_in = got[0], dict(zip(BIG_NAMES, got[1:]))
    reduced = {k: tree_sum(f"chip_sum_{k}", chunks_in[k]) for k in BIG_NAMES}
    g_c_ctx = cctx_grad(cc_parts, jnp.broadcast_to(c_ctx[None], (8, D)))[0]

    got = xchg("share_layers", [("swap", "sib", [reduced[k], reduced[k]]) for k in BIG_NAMES])
    grads = {k: jnp.where(ci == 0, jnp.stack([reduced[k], r]), jnp.stack([r, reduced[k]])) for k, r in zip(BIG_NAMES, got)}

    dm16 = jnp.concatenate([take(dm_lat_all).transpose(1, 0, 2), dmc_loc[:, None, :], jnp.zeros((depth, 7, W_MOD_COLS), F32)], axis=1)
    grads["w_mod"] = mod_weight_grad(c16, dm16)
    flat_sum = small_sum.reshape(-1)
    g_b_mod = (flat_sum[:n_dm] + flat_sum[n_dm:2 * n_dm]).reshape(depth, 6 * D)
    rest_shapes = [w[k].shape for k in PACKED[2:]]
    n_rest = sum(int(np.prod(s)) for s in rest_shapes)
    for k, g in zip(PACKED, [g_c_ctx, g_b_mod] + _unpack(flat_sum[2 * n_dm:2 * n_dm + n_rest], rest_shapes)):
        grads[k] = g

    delta, new_m, new_v = {}, {}, {}
    for k in BIG_NAMES + ("w_mod",):
        view = lambda a: a.reshape(-1, a.shape[-1])
        d_, m_, v_ = adamw(f"adamw_{k}", view(w[k]), view(grads[k]), view(m1[k]), view(m2[k]))
        delta[k], new_m[k], new_v[k] = d_.reshape(w[k].shape), m_.reshape(w[k].shape), v_.reshape(w[k].shape)
    shapes = [w[k].shape for k in PACKED]
    d_, m_, v_ = adamw("adamw_small", _pack([w[k] for k in PACKED]), _pack([grads[k] for k in PACKED]),
                       _pack([m1[k] for k in PACKED]), _pack([m2[k] for k in PACKED]))
    for k, dk, mk, vk in zip(PACKED, _unpack(d_, shapes), _unpack(m_, shapes), _unpack(v_, shapes)):
        delta[k], new_m[k], new_v[k] = dk, mk, vk
    return (loss, grad_x[None], *[grads[k] for k in WEIGHT_NAMES], *[delta[k] for k in WEIGHT_NAMES],
            *[new_m[k] for k in WEIGHT_NAMES], *[new_v[k] for k in WEIGHT_NAMES])
```

```python
import functools
import math

import numpy as np
import jax
import jax.numpy as jnp
from jax import lax
from jax.experimental import pallas as pl
from jax.experimental.pallas import tpu as pltpu

F32 = jnp.float32
BF16 = jnp.bfloat16
HI = lax.Precision.HIGHEST
EPS = 1e-6
VMEM_LIMIT_BYTES = 56 * 1024 * 1024
LANES = 128

D = 1024
D_FF = 4096
CTX = 256
GRID_W = 64
SGU_CHUNK = 128
GLA_CHUNK = 64
GLA_TAU = 16.0
GLA_DK = 32
MLA_SCALE = (128 + 64) ** -0.5
SCORE_SCALE = MLA_SCALE * math.log2(math.e)
LN2 = math.log(2.0)
ROPE_BASE = 10000.0
TM = 256
NCTXB = CTX // TM
P_GV, P_CKV, P_SU, P_SV, P_GR, P_DQ, P_GK, P_GATE, P_KR, P_GQ = 0, 256, 512, 768, 1024, 1280, 1536, 1664, 1792, 1920
P_COLS = 2048
IN_GROUPS = ((0, 128, P_GK), (128, 256, P_GV), (384, 32, P_GATE), (416, 256, P_CKV), (672, 64, P_KR),
             (736, 256, P_SU), (992, 256, P_SV), (1248, 128, P_GQ), (1376, 256, P_GR), (1632, 256, P_DQ))
ADAM_LR, ADAM_B1, ADAM_B2, ADAM_EPS, ADAM_WD, ADAM_STEP = 0.001, 0.9, 0.999, 1e-08, 0.01, 10
MESH = pl.DeviceIdType.MESH


def _params(sem):
    return pltpu.CompilerParams(dimension_semantics=sem, vmem_limit_bytes=VMEM_LIMIT_BYTES)


def _pick(n, cands):
    for c in cands:
        if n % c == 0:
            return c
    return n


class Op:
    def __init__(self, arr, blk, idx, gshape, gidx, acc):
        self.arr, self.blk, self.idx, self.gshape, self.gidx, self.acc = arr, blk, idx, gshape, gidx, acc

    def spec(self):
        return pl.BlockSpec(self.blk, self.idx)


def rows(arr, width=None, cb=0, off=0, tm=TM):
    w = arr.shape[1] if width is None else width
    n = arr.shape[0] - off * tm
    return Op(arr, (tm, w), lambda i: (i + off, cb), (n, w), lambda i: (i, 0), False)


def chunks(arr, per_tile):
    z = (0,) * (arr.ndim - 1)
    return Op(arr, (per_tile,) + arr.shape[1:], lambda i: (i,) + z, arr.shape, lambda i: (i,) + z, False)


def const(arr):
    z = (0,) * arr.ndim
    return Op(arr, arr.shape, lambda i: z, arr.shape, lambda i: z, True)


def rw(name, fn, ins, outs, grid):
    nin = len(ins)

    def body(*refs):
        vals = [r[...] for r in refs[:nin]]
        res = fn(pl.program_id(0), *vals)
        for o, r in zip(refs[nin:], res):
            o[...] = r.astype(o.dtype)

    return pl.pallas_call(
        body, name=name, grid=(grid,),
        in_specs=[o.spec() for o in ins],
        out_specs=[pl.BlockSpec(b, ix) for (_, _, b, ix) in outs],
        out_shape=[jax.ShapeDtypeStruct(s, d) for (s, d, _, _) in outs],
        compiler_params=_params(("arbitrary",)),
    )(*[o.arr for o in ins])


def rowout(n, w, dtype, tm=TM):
    return ((n, w), dtype, (tm, w), lambda i: (i, 0))


def chunkout(shape, dtype, per_tile):
    z = (0,) * (len(shape) - 1)
    return (shape, dtype, (per_tile,) + tuple(shape[1:]), lambda i: (i,) + z)


def rw_vjp(name, fn, ins, cots, wrt, grid, gdt=None, adds=None):
    nin = len(ins)
    cot_ops = [c for c in cots if c is not None]
    add_items = sorted((adds or {}).items())
    gdt = gdt or [F32] * len(wrt)
    ncot, nadd = len(cot_ops), len(add_items)

    def body(*refs):
        i = pl.program_id(0)
        vals = [r[...] for r in refs[:nin]]
        cvals = [r[...] for r in refs[nin:nin + ncot]]
        avals = [r[...] for r in refs[nin + ncot:nin + ncot + nadd]]
        grefs = refs[nin + ncot + nadd:]

        def f(*d):
            a = list(vals)
            for k, dv in zip(wrt, d):
                a[k] = dv
            return tuple(fn(i, *a))

        outs, vf = jax.vjp(f, *[vals[k] for k in wrt])
        it = iter(cvals)
        ct = tuple(jnp.zeros_like(o) if c is None else next(it).astype(o.dtype) for c, o in zip(cots, outs))
        gs = list(vf(ct))
        for (pos, _), av in zip(add_items, avals):
            gs[pos] = gs[pos].astype(F32) + av.astype(F32)
        for pos, (k, g, gref) in enumerate(zip(wrt, gs, grefs)):
            if ins[k].acc:
                @pl.when(i == 0)
                def _():
                    gref[...] = jnp.zeros_like(gref)
                gref[...] += g.astype(gref.dtype)
            else:
                gref[...] = g.astype(gref.dtype)

    all_in = list(ins) + cot_ops + [op for _, op in add_items]
    return pl.pallas_call(
        body, name=name, grid=(grid,),
        in_specs=[o.spec() for o in all_in],
        out_specs=[pl.BlockSpec(ins[k].blk, ins[k].gidx) for k in wrt],
        out_shape=[jax.ShapeDtypeStruct(ins[k].gshape, dt) for k, dt in zip(wrt, gdt)],
        compiler_params=_params(("arbitrary",)),
    )(*[o.arr for o in all_in])


MM_VMEM_BUDGET = 40 * 1024 * 1024
MM_COLS = 1024


def _square_bf16(a):
    a = a.astype(F32)
    return (a * a).astype(BF16)


def mm(name, a, b, out_dtype, pre=None, post=None, extras=(), bt=False):
    m, k = a.shape
    n = b.shape[0] if bt else b.shape[1]
    nc = min(n, MM_COLS)
    row_bytes = k * a.dtype.itemsize + n * jnp.dtype(out_dtype).itemsize + sum(n * e.dtype.itemsize for e in extras)
    tm = next(t for t in (768, 512, 384, 256, 128, 64)
              if m % t == 0 and 2 * t * row_bytes + 2 * k * n * b.dtype.itemsize + t * nc * 4 <= MM_VMEM_BUDGET)

    def body(a_ref, b_ref, *rest):
        o_ref = rest[-1]
        av = a_ref[...]
        if pre is not None:
            av = pre(av)
        for j in range(n // nc):
            cs = slice(j * nc, (j + 1) * nc)
            if bt:
                acc = lax.dot_general(av, b_ref[cs, :], (((1,), (1,)), ((), ())), preferred_element_type=F32)
            else:
                acc = lax.dot_general(av, b_ref[:, cs], (((1,), (0,)), ((), ())), preferred_element_type=F32)
            if post is not None:
                acc = post(acc, *[e[:, cs] for e in rest[:-1]])
            o_ref[:, cs] = acc.astype(o_ref.dtype)

    row = lambda w: pl.BlockSpec((tm, w), lambda i: (i, 0))
    return pl.pallas_call(
        body, name=name, grid=(m // tm,),
        in_specs=[row(k), pl.BlockSpec(b.shape, lambda i: (0, 0))] + [row(n) for _ in extras],
        out_specs=row(n),
        out_shape=jax.ShapeDtypeStruct((m, n), out_dtype),
        compiler_params=_params(("arbitrary",)),
    )(a, b, *extras)


def mm_tn(name, a, b, pre=None):
    m, ka = a.shape
    _, nb = b.shape
    tm = _pick(m, (768, 512, 256))
    ta = _pick(ka, (2048, 1024, 512, 256, 128))
    tb = _pick(nb, tuple(t for t in (4096, 2048, 1024, 512, 256, 128) if ta * t * 4 <= 8 * 1024 * 1024))

    def body(a_ref, b_ref, o_ref):
        @pl.when(pl.program_id(2) == 0)
        def _():
            o_ref[...] = jnp.zeros_like(o_ref)
        av = a_ref[...] if pre is None else pre(a_ref[...])
        o_ref[...] += lax.dot_general(av, b_ref[...], (((0,), (0,)), ((), ())), preferred_element_type=F32)

    return pl.pallas_call(
        body, name=name, grid=(ka // ta, nb // tb, m // tm),
        in_specs=[pl.BlockSpec((tm, ta), lambda i, j, k: (k, i)), pl.BlockSpec((tm, tb), lambda i, j, k: (k, j))],
        out_specs=pl.BlockSpec((ta, tb), lambda i, j, k: (i, j)),
        out_shape=jax.ShapeDtypeStruct((ka, nb), F32),
        compiler_params=_params(("arbitrary", "arbitrary", "arbitrary")),
    )(a, b)


def _rms(x, w):
    return x * lax.rsqrt(jnp.mean(x * x, axis=-1, keepdims=True) + EPS) * w


def _mod_of(blk, m):
    return jnp.where(blk < NCTXB, m[0], m[1])


def _gelu(x):
    return x * (0.5 * (1.0 + jnp.tanh(math.sqrt(2.0 / math.pi) * (x + 0.044715 * (x * x * x)))))


def _sigmoid(x):
    return 1.0 / (1.0 + jnp.exp(-x))


def _log_sigmoid(z):
    return jnp.minimum(z, 0.0) - jnp.log(1.0 + jnp.exp(-jnp.abs(z)))


def _dot(a, b, dims=((1,), (0,)), precision=None):
    return lax.dot_general(a, b, (dims, ((), ())), precision=precision, preferred_element_type=F32)


def _lane_group_mask(width, group, h):
    lane = lax.broadcasted_iota(jnp.int32, (1, width), 1)
    return (lane >= h * group) & (lane < (h + 1) * group)


def fn_norm1(blk, x, m, nw):
    mv = _mod_of(blk, m)
    return ((_rms(x, nw) * (1.0 + mv[:, D:2 * D]) + mv[:, 0:D]),)


def fn_res_norm2(blk, x, yo, m, nw):
    mv = _mod_of(blk, m)
    x1 = x + mv[:, 2 * D:3 * D] * yo
    return x1, _rms(x1, nw) * (1.0 + mv[:, 4 * D:5 * D]) + mv[:, 3 * D:4 * D]


def fn_res2(blk, x1, f, m):
    mv = _mod_of(blk, m)
    return (x1 + mv[:, 5 * D:6 * D] * f,)


def fn_sgu(blk, su, sv, nw, nb, ws, bm):
    u = _gelu(su)
    g = _gelu(sv)
    mu = jnp.mean(g, axis=-1, keepdims=True)
    var = jnp.mean(jnp.square(g - mu), axis=-1, keepdims=True)
    v = (g - mu) * lax.rsqrt(var + EPS) * nw + nb
    out = []
    for c in range(su.shape[0] // SGU_CHUNK):
        vc = v[c * SGU_CHUNK:(c + 1) * SGU_CHUNK]
        s = bm
        for h in range(4):
            vh = jnp.where(_lane_group_mask(256, 64, h), vc, 0.0)
            s = s + _dot(ws[h].astype(BF16), vh.astype(BF16))
        out.append(u[c * SGU_CHUNK:(c + 1) * SGU_CHUNK] * s)
    return (jnp.concatenate(out, axis=0),)


def fn_gates(blk, pg, wg, bg):
    z = _dot(pg.astype(BF16), wg.astype(BF16)) + bg
    g = _log_sigmoid(z) * (1.0 / GLA_TAU)
    return g[:, :128], g[:, 128:]


def _scan_rows(x, rev):
    n = x.shape[0]
    row = lax.broadcasted_iota(jnp.int32, x.shape, 0)
    d = 1
    while d < n:
        if rev:
            x = x + jnp.where(row < n - d, pltpu.roll(x, n - d, 0), 0.0)
        else:
            x = x + jnp.where(row >= d, pltpu.roll(x, d, 0), 0.0)
        d *= 2
    return x


@functools.partial(jax.custom_vjp, nondiff_argnums=(1,))
def _cumsum_rows(x, rev):
    return _scan_rows(x, rev)


def _cumsum_rows_fwd(x, rev):
    return _scan_rows(x, rev), None


def _cumsum_rows_bwd(rev, _, dy):
    return (_scan_rows(dy, not rev),)


_cumsum_rows.defvjp(_cumsum_rows_fwd, _cumsum_rows_bwd)


def _gla_chunk_terms(g, rev):
    return _cumsum_rows(g, rev), jnp.sum(g, axis=0, keepdims=True)


def _bd_mask():
    r = lax.broadcasted_iota(jnp.int32, (128, 256), 0)
    c = lax.broadcasted_iota(jnp.int32, (128, 256), 1)
    return (r // GLA_DK) == (c // 64)


def _gla_kv_chunk(k, v, g, rev):
    b, tot = _gla_chunk_terms(g, rev)
    kd = k * jnp.exp(tot - b)
    u = jnp.where(_bd_mask(), _dot(kd.astype(BF16), v.astype(BF16), ((0,), (0,))), 0.0)
    r = lax.broadcasted_iota(jnp.int32, (128, 128), 0)
    c = lax.broadcasted_iota(jnp.int32, (128, 128), 1)
    col = jnp.sum(jnp.where(r == c, jnp.broadcast_to(jnp.exp(tot), (128, 128)), 0.0), axis=1, keepdims=True)
    return u, jnp.broadcast_to(col, (128, 128))


def _gla_o_chunk(q, k, v, g, s, rev):
    b, _ = _gla_chunk_terms(g, rev)
    qe = q * jnp.exp(b) * (GLA_DK ** -0.5)
    ke = k * jnp.exp(-b)
    o = _dot(qe.astype(BF16), jnp.where(_bd_mask(), s, 0.0).astype(BF16))
    qs = jnp.concatenate([jnp.where(_lane_group_mask(128, GLA_DK, h), qe, 0.0) for h in range(4)], axis=0)
    a = _dot(qs.astype(BF16), ke.astype(BF16), ((1,), (1,)))
    i = lax.broadcasted_iota(jnp.int32, a.shape, 0) % GLA_CHUNK
    j = lax.broadcasted_iota(jnp.int32, a.shape, 1)
    a = jnp.where((j >= i) if rev else (j <= i), a, 0.0)
    av = _dot(a.astype(BF16), v.astype(BF16))
    for h in range(4):
        o = o + jnp.where(_lane_group_mask(256, 64, h), av[GLA_CHUNK * h:GLA_CHUNK * (h + 1)], 0.0)
    return o


def fn_gla_kv(blk, k, v, gf, gb):
    uf, ef, ub, eb = [], [], [], []
    for c in range(k.shape[0] // GLA_CHUNK):
        sl = slice(c * GLA_CHUNK, (c + 1) * GLA_CHUNK)
        u, e = _gla_kv_chunk(k[sl], v[sl], gf[sl], False)
        uf.append(u[None]); ef.append(e[None])
        u, e = _gla_kv_chunk(k[sl], v[sl], gb[sl], True)
        ub.append(u[None]); eb.append(e[None])
    cat = lambda t: jnp.concatenate(t, axis=0)
    return cat(uf), cat(ef), cat(ub), cat(eb)


def fn_gla_o(blk, q, k, v, gf, gb, gr, sf, sb, nwt):
    out = []
    for c in range(q.shape[0] // GLA_CHUNK):
        sl = slice(c * GLA_CHUNK, (c + 1) * GLA_CHUNK)
        out.append(_gla_o_chunk(q[sl], k[sl], v[sl], gf[sl], sf[c], False)
                   + _gla_o_chunk(q[sl], k[sl], v[sl], gb[sl], sb[c], True))
    o = jnp.concatenate(out, axis=0)
    r = lax.broadcasted_iota(jnp.int32, (256, 256), 0)
    c = lax.broadcasted_iota(jnp.int32, (256, 256), 1)
    head_mean = jnp.where((r // 64) == (c // 64), 1.0 / 64.0, 0.0).astype(F32)
    ms = _dot(o * o, head_mean, precision=HI)
    on = o * lax.rsqrt(ms + EPS) * nwt
    return (on * (gr * _sigmoid(gr)),)


def _rope_partner(x):
    lane = lax.broadcasted_iota(jnp.int32, x.shape, 1)
    return jnp.where((lane // 16) % 2 == 0, pltpu.roll(x, LANES - 16, 1), pltpu.roll(x, 16, 1))


@jax.custom_vjp
def _rope(x, cs, sn):
    return x * cs + _rope_partner(x) * sn


def _rope_fwd(x, cs, sn):
    return _rope(x, cs, sn), (cs, sn)


def _rope_bwd(res, dy):
    cs, sn = res
    return dy * cs + _rope_partner(dy * sn), jnp.zeros_like(cs), jnp.zeros_like(sn)


_rope.defvjp(_rope_fwd, _rope_bwd)


def fn_mla_pre(blk, ckv, dq, kvw, qw):
    return _rms(ckv, kvw), _rms(dq, qw)


def fn_mla_post(blk, kk, qu, kr, cs, sn):
    kro = _rope(kr, cs, sn)
    kcat, q = [], []
    for h in range(4):
        kcat += [kk[:, 128 * h:128 * (h + 1)].astype(F32), kro]
        q += [qu[:, 256 * h:256 * h + 128], _rope(qu[:, 256 * h + 128:256 * (h + 1)], cs, sn)]
    return jnp.concatenate(kcat, axis=1), jnp.concatenate(q, axis=1) * SCORE_SCALE


ATTN_ROWS = 256
NEG = -1e30


def _scores(q, k, k0, context_queries):
    s = _dot(q, k, ((1,), (1,)))
    if context_queries is not None:
        col = k0 + lax.broadcasted_iota(jnp.int32, s.shape, 1)
        s = jnp.where(context_queries & (col >= CTX), NEG, s)
    return s


def flash_fwd(q, kcat, kvu):
    t = q.shape[0]
    tq = _pick(t, (768, 512, 256))
    tk = _pick(t, (2816, 1536, 768, 512, 256))
    nsub = tq // ATTN_ROWS

    def body(q_ref, k_ref, v_ref, o_ref, lse_ref, m_sc, l_sc, acc_sc):
        qi, ki = pl.program_id(1), pl.program_id(2)

        @pl.when(ki == 0)
        def _():
            m_sc[...] = jnp.full_like(m_sc, NEG)
            l_sc[...] = jnp.zeros_like(l_sc)
            acc_sc[...] = jnp.zeros_like(acc_sc)

        k, v = k_ref[...], v_ref[...]
        chains = [pl.ds(r * ATTN_ROWS, ATTN_ROWS) for r in range(nsub)]
        scores = [_scores(q_ref[rs, :], k, ki * tk, (qi == 0) if r == 0 else None) for r, rs in enumerate(chains)]
        probs = []
        for rs, s in zip(chains, scores):
            m_old = m_sc[rs, :]
            m_new = jnp.maximum(m_old, jnp.max(s, axis=-1, keepdims=True))
            alpha = jnp.exp2(m_old - m_new)
            p = jnp.exp2(s - m_new)
            l_sc[rs, :] = alpha * l_sc[rs, :] + jnp.sum(p, axis=-1, keepdims=True)
            m_sc[rs, :] = m_new
            probs.append((alpha, p.astype(BF16)))
        for rs, (alpha, p) in zip(chains, probs):
            acc_sc[rs, :] = alpha * acc_sc[rs, :] + _dot(p, v)

        @pl.when(ki == pl.num_programs(2) - 1)
        def _():
            o_ref[...] = acc_sc[...] / l_sc[...]
            lse_ref[...] = jnp.broadcast_to(m_sc[...] + jnp.log2(l_sc[...]), lse_ref.shape)

    return pl.pallas_call(
        body, name="mla_flash_fwd", grid=(4, t // tq, t // tk),
        in_specs=[pl.BlockSpec((tq, 256), lambda h, i, j: (i, h)), pl.BlockSpec((tk, 256), lambda h, i, j: (j, h)),
                  pl.BlockSpec((tk, 128), lambda h, i, j: (j, 4 + h))],
        out_specs=[pl.BlockSpec((tq, 128), lambda h, i, j: (i, h)), pl.BlockSpec((tq, 128), lambda h, i, j: (i, h))],
        out_shape=[jax.ShapeDtypeStruct((t, 512), F32), jax.ShapeDtypeStruct((t, 512), F32)],
        scratch_shapes=[pltpu.VMEM((tq, 1), F32), pltpu.VMEM((tq, 1), F32), pltpu.VMEM((tq, 128), F32)],
        compiler_params=_params(("arbitrary", "arbitrary", "arbitrary")),
    )(q, kcat, kvu)


def fn_attn_stats(blk, do, o, lse):
    out = []
    for h in range(4):
        hs = slice(128 * h, 128 * (h + 1))
        d = jnp.sum(do[:, hs] * o[:, hs], axis=-1, keepdims=True)
        out.append(lse[:, hs].T[0:8])
        out.append(jnp.broadcast_to(d, (do.shape[0], 128)).T[0:8])
    return (jnp.concatenate(out, axis=0)[None],)


def flash_bwd(q, kcat, kvu, dy, stats):
    t = q.shape[0]
    tq = _pick(t, (2816, 768, 512, 256))
    tk = _pick(t, (768, 512, 256))
    nst = tq // TM

    def body(q_ref, k_ref, v_ref, do_ref, st_ref, dq_ref, dk_ref, dv_ref):
        kj, qi = pl.program_id(1), pl.program_id(2)

        @pl.when(qi == 0)
        def _():
            dk_ref[...] = jnp.zeros_like(dk_ref)
            dv_ref[...] = jnp.zeros_like(dv_ref)

        q_, k, v, do = q_ref[...], k_ref[...], v_ref[...], do_ref[...].astype(BF16)
        lse_row = jnp.concatenate([st_ref[u, 0:1, :] for u in range(nst)], axis=1)
        delta_row = jnp.concatenate([st_ref[u, 8:9, :] for u in range(nst)], axis=1)
        s = _dot(k, q_, ((1,), (1,)))
        key = kj * tk + lax.broadcasted_iota(jnp.int32, s.shape, 0)
        qry = qi * tq + lax.broadcasted_iota(jnp.int32, s.shape, 1)
        s = jnp.where((qry < CTX) & (key >= CTX), NEG, s)
        p = jnp.exp2(s - lse_row)
        dp = _dot(v, do, ((1,), (1,)))
        ds = (p * (dp - delta_row)).astype(BF16)
        dv_ref[...] += _dot(p.astype(BF16), do)
        dk_ref[...] += LN2 * _dot(ds, q_)
        dq_new = LN2 * _dot(ds, k, ((0,), (0,)))
        rows_ = pl.ds(pl.multiple_of(qi * tq, TM), tq)

        @pl.when(kj == 0)
        def _():
            dq_ref[rows_, :] = dq_new

        @pl.when(kj != 0)
        def _():
            dq_ref[rows_, :] += dq_new

    return pl.pallas_call(
        body, name="mla_flash_bwd", grid=(4, t // tk, t // tq),
        in_specs=[pl.BlockSpec((tq, 256), lambda h, j, i: (i, h)), pl.BlockSpec((tk, 256), lambda h, j, i: (j, h)),
                  pl.BlockSpec((tk, 128), lambda h, j, i: (j, 4 + h)), pl.BlockSpec((tq, 128), lambda h, j, i: (i, 4 + h)),
                  pl.BlockSpec((nst, 16, 256), lambda h, j, i: (i, h, 0))],
        out_specs=[pl.BlockSpec((t, 256), lambda h, j, i: (0, h)), pl.BlockSpec((tk, 256), lambda h, j, i: (j, h)),
                   pl.BlockSpec((tk, 128), lambda h, j, i: (j, h))],
        out_shape=[jax.ShapeDtypeStruct((t, 1024), F32), jax.ShapeDtypeStruct((t, 1024), F32),
                   jax.ShapeDtypeStruct((t, 512), F32)],
        compiler_params=_params(("arbitrary", "arbitrary", "arbitrary")),
    )(q, kcat, kvu, dy, stats)


SCAN_BLOCK = CTX // GLA_CHUNK


def _scan_block(t, nb, rev):
    if not rev:
        return t
    return jnp.where(t < 1, 0, nb - t)


def _scan_order(rev):
    return tuple(reversed(range(SCAN_BLOCK))) if rev else tuple(range(SCAN_BLOCK))


def _both_halves(e):
    return jnp.concatenate([e, e], axis=1)


def gla_states(uf, ef, ub, eb):
    nb = uf.shape[0] // SCAN_BLOCK

    def body(uf_ref, ef_ref, ub_ref, eb_ref, sf_ref, sb_ref, sf_sc, sb_sc):
        @pl.when(pl.program_id(0) == 0)
        def _():
            sf_sc[...] = jnp.zeros_like(sf_sc)
            sb_sc[...] = jnp.zeros_like(sb_sc)

        for u_ref, e_ref, s_ref, sc, rev in ((uf_ref, ef_ref, sf_ref, sf_sc, False), (ub_ref, eb_ref, sb_ref, sb_sc, True)):
            s = sc[...]
            for c in _scan_order(rev):
                s_ref[c] = s
                s = _both_halves(e_ref[c]) * s + u_ref[c]
            sc[...] = s

    big = lambda rev: pl.BlockSpec((SCAN_BLOCK, 128, 256), lambda t: (_scan_block(t, nb, rev), 0, 0))
    small = lambda rev: pl.BlockSpec((SCAN_BLOCK, 128, 128), lambda t: (_scan_block(t, nb, rev), 0, 0))
    return pl.pallas_call(
        body, name="gla_states", grid=(nb,),
        in_specs=[big(False), small(False), big(True), small(True)],
        out_specs=[big(False), big(True)],
        out_shape=[jax.ShapeDtypeStruct(uf.shape, F32)] * 2,
        scratch_shapes=[pltpu.VMEM((128, 256), F32)] * 2,
        compiler_params=_params(("arbitrary",)),
    )(uf, ef, ub, eb)


def gla_states_bwd(ef, eb, sf, sb, dsf, dsb):
    nb = ef.shape[0] // SCAN_BLOCK

    def body(ef_ref, eb_ref, sf_ref, sb_ref, dsf_ref, dsb_ref, duf_ref, def_ref, dub_ref, deb_ref, gf_sc, gb_sc):
        @pl.when(pl.program_id(0) == 0)
        def _():
            gf_sc[...] = jnp.zeros_like(gf_sc)
            gb_sc[...] = jnp.zeros_like(gb_sc)

        for e_ref, s_ref, ds_ref, du_ref, de_ref, g_sc, rev in ((ef_ref, sf_ref, dsf_ref, duf_ref, def_ref, gf_sc, False),
                                                                 (eb_ref, sb_ref, dsb_ref, dub_ref, deb_ref, gb_sc, True)):
            g = g_sc[...]
            for k in reversed(_scan_order(rev)):
                du_ref[k] = g
                gs = g * s_ref[k]
                de_ref[k] = gs[:, :128] + gs[:, 128:]
                g = _both_halves(e_ref[k]) * g + ds_ref[k]
            g_sc[...] = g

    big = lambda rev: pl.BlockSpec((SCAN_BLOCK, 128, 256), lambda t: (_scan_block(nb - 1 - t, nb, rev), 0, 0))
    small = lambda rev: pl.BlockSpec((SCAN_BLOCK, 128, 128), lambda t: (_scan_block(nb - 1 - t, nb, rev), 0, 0))
    return pl.pallas_call(
        body, name="gla_states_bwd", grid=(nb,),
        in_specs=[small(False), small(True), big(False), big(True), big(False), big(True)],
        out_specs=[big(False), small(False), big(True), small(True)],
        out_shape=[jax.ShapeDtypeStruct(sf.shape, F32), jax.ShapeDtypeStruct(ef.shape, F32)] * 2,
        scratch_shapes=[pltpu.VMEM((128, 256), F32)] * 2,
        compiler_params=_params(("arbitrary",)),
    )(ef, eb, sf, sb, dsf, dsb)


def loss_head(xt, target, fnw):
    t = xt.shape[0]

    def f(x, tg, w):
        y = _rms(x, w)
        return 0.5 * jnp.sum(jnp.square(y - tg)) * (1.0 / D)

    def body(x_ref, t_ref, w_ref, loss_ref, dx_ref, dw_ref):
        i = pl.program_id(0)

        @pl.when(i == 0)
        def _():
            loss_ref[...] = jnp.zeros_like(loss_ref)
            dw_ref[...] = jnp.zeros_like(dw_ref)

        @pl.when(i < NCTXB)
        def _():
            dx_ref[...] = jnp.zeros_like(dx_ref)

        @pl.when(i >= NCTXB)
        def _():
            val, (dx, dw) = jax.value_and_grad(f, argnums=(0, 2))(x_ref[...], t_ref[...], w_ref[...])
            loss_ref[...] += jnp.broadcast_to(val, loss_ref.shape)
            dx_ref[...] = dx
            dw_ref[...] += dw

    return pl.pallas_call(
        body, name="loss_head", grid=(t // TM,),
        in_specs=[pl.BlockSpec((TM, D), lambda i: (i, 0)), pl.BlockSpec((TM, D), lambda i: (jnp.maximum(i - NCTXB, 0), 0)),
                  pl.BlockSpec((1, D), lambda i: (0, 0))],
        out_specs=[pl.BlockSpec((1, 128), lambda i: (0, 0)), pl.BlockSpec((TM, D), lambda i: (i, 0)),
                   pl.BlockSpec((1, D), lambda i: (0, 0))],
        out_shape=[jax.ShapeDtypeStruct((1, 128), F32), jax.ShapeDtypeStruct((t, D), F32), jax.ShapeDtypeStruct((1, D), F32)],
        compiler_params=_params(("arbitrary",)),
    )(xt, target, fnw)


def _in_to_padded(w):
    out, pos = [], 0
    for src, wd, dst in sorted(IN_GROUPS, key=lambda g: g[2]):
        if dst > pos:
            out.append(jnp.zeros((w.shape[0], dst - pos), w.dtype))
        out.append(w[:, src:src + wd])
        pos = dst + wd
    if pos < P_COLS:
        out.append(jnp.zeros((w.shape[0], P_COLS - pos), w.dtype))
    return jnp.concatenate(out, axis=1)


def _in_from_padded(g):
    return jnp.concatenate([g[:, dst:dst + wd] for _, wd, dst in IN_GROUPS], axis=1)


def _uq_to_padded(w):
    return jnp.pad(w.reshape(256, 4, 192), ((0, 0), (0, 0), (0, 64))).reshape(256, 1024)


def _uq_from_padded(g):
    return g.reshape(256, 4, 256)[:, :, :192].reshape(256, 768)


def _ukv_to_padded(w):
    return w.reshape(256, 4, 2, 128).transpose(0, 2, 1, 3).reshape(256, 1024)


def _ukv_from_padded(g):
    return g.reshape(256, 2, 4, 128).transpose(0, 2, 1, 3).reshape(256, 1024)


def _rope_tables(n):
    freq = ROPE_BASE ** (-jnp.arange(16, dtype=F32) * 2.0 / 32.0)
    grid_h = n // GRID_W
    ar = jnp.repeat(jnp.arange(grid_h, dtype=F32)[:, None] * freq[None, :], GRID_W, axis=0)
    ac = jnp.tile(jnp.arange(GRID_W, dtype=F32)[:, None] * freq[None, :], (grid_h, 1))
    z = jnp.zeros((n, 64), F32)
    cs = jnp.concatenate([jnp.cos(ar), jnp.cos(ar), jnp.cos(ac), jnp.cos(ac), z], axis=1)
    sn = jnp.concatenate([-jnp.sin(ar), jnp.sin(ar), -jnp.sin(ac), jnp.sin(ac), z], axis=1)
    cs_c = jnp.concatenate([jnp.ones((CTX, 64), F32), jnp.zeros((CTX, 64), F32)], axis=1)
    return jnp.concatenate([cs_c, cs], axis=0), jnp.concatenate([jnp.zeros((CTX, 128), F32), sn], axis=0)


def _small_views(sp):
    wg = jnp.concatenate([jnp.pad(sp["gla_wg_fwd"], ((0, 112), (0, 0))), jnp.pad(sp["gla_wg_bwd"], ((16, 96), (0, 0)))], axis=1)
    return dict(
        n1w=sp["norm1_w"][None], n2w=sp["norm2_w"][None],
        sgu_nw=sp["sgu_norm_w"][None], sgu_nb=sp["sgu_norm_b"][None], sgu_w=sp["sgu_w"],
        sgu_bm=jnp.repeat(sp["sgu_b"].T, 64, axis=1),
        wg=wg, bg=jnp.concatenate([sp["gla_bg_fwd"], sp["gla_bg_bwd"]])[None],
        gla_nwt=jnp.tile(sp["gla_norm_w"], 4)[None],
        kvw=sp["mla_kv_norm_w"][None], qw=sp["mla_q_norm_w"][None])


def _small_grads(g):
    return dict(
        norm1_w=g["n1w"][0], norm2_w=g["n2w"][0],
        sgu_norm_w=g["sgu_nw"][0], sgu_norm_b=g["sgu_nb"][0], sgu_w=g["sgu_w"],
        sgu_b=g["sgu_bm"].reshape(128, 4, 64).sum(-1).T,
        gla_wg_fwd=g["wg"][0:16, 0:128], gla_wg_bwd=g["wg"][16:32, 128:256],
        gla_bg_fwd=g["bg"][0, 0:128], gla_bg_bwd=g["bg"][0, 128:256],
        gla_norm_w=g["gla_nwt"].reshape(4, 64).sum(0),
        mla_kv_norm_w=g["kvw"][0], mla_q_norm_w=g["qw"][0])


def _big_views(w_in, w_out, w_uq, w_ukv, w_ff1, w_ff2):
    win, wuq, wukv = _in_to_padded(w_in), _uq_to_padded(w_uq), _ukv_to_padded(w_ukv)
    return dict(win=win, wuq=wuq, wukv=wukv, wout=w_out, w1=w_ff1, w2=w_ff2)


def _gla_tile(t):
    return _pick(t, (768, 512, 256))


def _layer_ops(p, sv, a):
    pc = lambda off, w, tm=TM: rows(p, w, off // w, tm=tm)
    gt = _gla_tile(p.shape[0])
    gr = lambda arr: rows(arr, tm=gt)
    return dict(
        sgu=[pc(P_SU, 256), pc(P_SV, 256), const(sv["sgu_nw"]), const(sv["sgu_nb"]), const(sv["sgu_w"]), const(sv["sgu_bm"])],
        gates=[pc(P_GATE, 128), const(sv["wg"]), const(sv["bg"])],
        mla_pre=[pc(P_CKV, 256), pc(P_DQ, 256), const(sv["kvw"]), const(sv["qw"])],
        gla_kv=lambda: [pc(P_GK, 128, gt), pc(P_GV, 256, gt), gr(a["gf"]), gr(a["gb"])],
        gla_o=lambda: [pc(P_GQ, 128, gt), pc(P_GK, 128, gt), pc(P_GV, 256, gt), gr(a["gf"]), gr(a["gb"]), pc(P_GR, 256, gt),
                       chunks(a["sf"], gt // GLA_CHUNK), chunks(a["sb"], gt // GLA_CHUNK), const(sv["gla_nwt"])],
        mla_post=lambda: [rows(a["kvu"], 512, 0), rows(a["qu"]), pc(P_KR, 128), rows(a["cs"]), rows(a["sn"])])


def layer_fwd(l, xt, modl, bw, sv, tabs):
    t = xt.shape[0]
    g, nc, gt = t // TM, t // GLA_CHUNK, _gla_tile(t)
    gg, cpt = t // gt, gt // GLA_CHUNK
    nm = lambda s: f"l{l}_{s}"
    a = dict(x=xt, cs=tabs[0], sn=tabs[1])
    a["h"], = rw(nm("norm1"), fn_norm1, [rows(xt), const(modl), const(sv["n1w"])], [rowout(t, D, BF16)], g)
    p = a["p"] = mm(nm("in_proj"), a["h"], bw["win"], F32)
    ops = _layer_ops(p, sv, a)
    y_sgu, = rw(nm("sgu"), fn_sgu, ops["sgu"], [rowout(t, 256, BF16)], g)
    a["gf"], a["gb"] = rw(nm("gates"), fn_gates, ops["gates"], [rowout(t, 128, F32)] * 2, g)
    a["uf"], a["ef"], a["ub"], a["eb"] = rw(nm("gla_kv"), fn_gla_kv, ops["gla_kv"](),
                                           [chunkout((nc, 128, 256), F32, cpt), chunkout((nc, 128, 128), F32, cpt)] * 2, gg)
    a["sf"], a["sb"] = gla_states(a["uf"], a["ef"], a["ub"], a["eb"])
    y_gla, = rw(nm("gla_o"), fn_gla_o, ops["gla_o"](), [rowout(t, 256, BF16, tm=gt)], gg)
    a["ckvn"], a["dqn"] = rw(nm("mla_pre"), fn_mla_pre, ops["mla_pre"], [rowout(t, 256, BF16)] * 2, g)
    a["kvu"] = mm(nm("kv_up"), a["ckvn"], bw["wukv"], BF16)
    a["qu"] = mm(nm("q_up"), a["dqn"], bw["wuq"], F32)
    a["kcat"], a["q"] = rw(nm("mla_post"), fn_mla_post, ops["mla_post"](), [rowout(t, 1024, BF16)] * 2, g)
    a["o"], a["lse"] = flash_fwd(a["q"], a["kcat"], a["kvu"])
    a["y"] = jnp.concatenate([y_sgu, y_gla, a["o"].astype(BF16)], axis=1)
    a["yo"] = mm(nm("out_proj"), a["y"], bw["wout"], F32)
    a["x1"], a["h2"] = rw(nm("res_norm2"), fn_res_norm2, [rows(xt), rows(a["yo"]), const(modl), const(sv["n2w"])],
                          [rowout(t, D, F32), rowout(t, D, BF16)], g)
    a["act"] = mm(nm("ff1"), a["h2"], bw["w1"], BF16, post=lambda acc: jnp.maximum(acc, 0.0))
    a["f"] = mm(nm("ff2"), a["act"], bw["w2"], F32, pre=_square_bf16)
    x2, = rw(nm("res2"), fn_res2, [rows(a["x1"]), rows(a["f"]), const(modl)], [rowout(t, D, F32)], g)
    return x2, a


def fn_assemble(blk, gv1, gv2, ckv, su, sv_, gr, dq, gk1, gk2, pg, kr, gq):
    return (jnp.concatenate([gv1 + gv2, ckv, su, sv_, gr, dq, gk1 + gk2, pg, kr, gq], axis=1),)


def layer_bwd(l, dx2, a, modl, bw, sv):
    t = dx2.shape[0]
    g, gt = t // TM, _gla_tile(t)
    gg, cpt = t // gt, gt // GLA_CHUNK
    nm = lambda s: f"l{l}_{s}_bwd"
    p = a["p"]
    ops = _layer_ops(p, sv, a)
    gw, gs = {}, {}
    df, dm_a = rw_vjp(nm("res2"), fn_res2, [rows(a["x1"]), rows(a["f"]), const(modl)], [rows(dx2)], [1, 2], g,
                      gdt=[BF16, F32])
    gw["w2"] = mm_tn(nm("ff2_w"), a["act"], df, pre=_square_bf16)
    du = mm(nm("ff2_x"), df, bw["w2"], BF16, post=lambda acc, act: acc * (2.0 * act.astype(F32)), extras=(a["act"],), bt=True)
    gw["w1"] = mm_tn(nm("ff1_w"), a["h2"], du)
    dh2 = mm(nm("ff1_x"), du, bw["w1"], F32, bt=True)
    dxa, dyo, dm_b, gs["n2w"] = rw_vjp(nm("res_norm2"), fn_res_norm2,
                                       [rows(a["x"]), rows(a["yo"]), const(modl), const(sv["n2w"])],
                                       [rows(dx2), rows(dh2)], [0, 1, 2, 3], g, gdt=[F32, BF16, F32, F32])
    gw["wout"] = mm_tn(nm("out_w"), a["y"], dyo)
    dy = mm(nm("out_x"), dyo, bw["wout"], F32, bt=True)
    dsu, dsv, gs["sgu_nw"], gs["sgu_nb"], gs["sgu_w"], gs["sgu_bm"] = rw_vjp(
        nm("sgu"), fn_sgu, ops["sgu"], [rows(dy, 256, 0)], [0, 1, 2, 3, 4, 5], g)
    dgq, dgk1, dgv1, dgf1, dgb1, dgr, dsf, dsb, gs["gla_nwt"] = rw_vjp(
        nm("gla_o"), fn_gla_o, ops["gla_o"](), [rows(dy, 256, 1, tm=gt)], list(range(9)), gg)
    duf, def_, dub, deb = gla_states_bwd(a["ef"], a["eb"], a["sf"], a["sb"], dsf, dsb)
    dgk2, dgv2, dgf, dgb = rw_vjp(nm("gla_kv"), fn_gla_kv, ops["gla_kv"](),
                                  [chunks(duf, cpt), chunks(def_, cpt), chunks(dub, cpt), chunks(deb, cpt)], [0, 1, 2, 3], gg,
                                  adds={2: rows(dgf1, tm=gt), 3: rows(dgb1, tm=gt)})
    dpg, gs["wg"], gs["bg"] = rw_vjp(nm("gates"), fn_gates, ops["gates"], [rows(dgf), rows(dgb)], [0, 1, 2], g)
    stats, = rw(nm("attn_stats"), fn_attn_stats, [rows(dy, 512, 1), rows(a["o"]), rows(a["lse"])],
                [chunkout((g, 64, TM), F32, 1)], g)
    dq, dkcat, dv = flash_bwd(a["q"], a["kcat"], a["kvu"], dy, stats)
    dkk, dqu, dkr = rw_vjp(nm("mla_post"), fn_mla_post, ops["mla_post"](), [rows(dkcat), rows(dq)], [0, 1, 2], g,
                           gdt=[BF16, BF16, F32])
    dkvu = jnp.concatenate([dkk, dv.astype(BF16)], axis=1)
    gw["wukv"] = mm_tn(nm("kv_up_w"), a["ckvn"], dkvu)
    gw["wuq"] = mm_tn(nm("q_up_w"), a["dqn"], dqu)
    dckvn = mm(nm("kv_up_x"), dkvu, bw["wukv"], F32, bt=True)
    ddqn = mm(nm("q_up_x"), dqu, bw["wuq"], F32, bt=True)
    dckv, ddq, gs["kvw"], gs["qw"] = rw_vjp(nm("mla_pre"), fn_mla_pre, ops["mla_pre"], [rows(dckvn), rows(ddqn)],
                                            [0, 1, 2, 3], g)
    dp, = rw(nm("assemble"), fn_assemble,
             [rows(x_) for x_ in (dgv1, dgv2, dckv, dsu, dsv, dgr, ddq, dgk1, dgk2, dpg, dkr, dgq)],
             [rowout(t, P_COLS, BF16)], g)
    gw["win"] = mm_tn(nm("in_w"), a["h"], dp)
    dh = mm(nm("in_x"), dp, bw["win"], F32, bt=True)
    dx, dm_c, gs["n1w"] = rw_vjp(nm("norm1"), fn_norm1, [rows(a["x"]), const(modl), const(sv["n1w"])], [rows(dh)],
                                 [0, 1, 2], g, adds={0: rows(dxa)})
    big = dict(w_in=_in_from_padded(gw["win"]), w_out=gw["wout"], mla_w_uq=_uq_from_padded(gw["wuq"]),
               mla_w_ukv=_ukv_from_padded(gw["wukv"]), w_ff1=gw["w1"], w_ff2=gw["w2"])
    return dx, dm_a + dm_b + dm_c, big, _small_grads(gs)


SMALL_NAMES = ("norm1_w", "sgu_norm_w", "sgu_norm_b", "sgu_w", "sgu_b", "gla_wg_fwd", "gla_bg_fwd", "gla_wg_bwd",
               "gla_bg_bwd", "gla_norm_w", "mla_q_norm_w", "mla_kv_norm_w", "norm2_w")
BIG_NAMES = ("w_in", "w_out", "mla_w_uq", "mla_w_ukv", "w_ff1", "w_ff2")


def local_step(x, ctx, target, mods, big, small, final_norm_w):
    n = x.shape[0]
    xt = jnp.concatenate([ctx, x], axis=0)
    tabs = _rope_tables(n)
    depth = len(mods)
    bws = [_big_views(*[big[l][k] for k in BIG_NAMES]) for l in range(depth)]
    svs = [_small_views(small[l]) for l in range(depth)]
    acts = []
    for l in range(depth):
        xt, a = layer_fwd(l, xt, mods[l], bws[l], svs[l], tabs)
        acts.append(a)
    loss, dxt, dfnw = loss_head(xt, target, final_norm_w[None])
    dmods, gbig, gsmall = [None] * depth, [None] * depth, [None] * depth
    for l in reversed(range(depth)):
        dxt, dmods[l], gbig[l], gsmall[l] = layer_bwd(l, dxt, acts[l], mods[l], bws[l], svs[l])
    return loss, dxt[CTX:], dmods, gbig, gsmall, dfnw


def _group(group):
    x, y, c = lax.axis_index("x"), lax.axis_index("y"), lax.axis_index("c")
    if group == "sib":
        return 2, c, [((x, y, 1 - c), 1 - c)]
    if group == "chip":
        flips = [(1, 0), (0, 1), (1, 1)]
        return 4, 2 * x + y, [((x ^ fx, y ^ fy, c), 2 * (x ^ fx) + (y ^ fy)) for fx, fy in flips]
    flips = [(fx, fy, fc) for fx in (0, 1) for fy in (0, 1) for fc in (0, 1)][1:]
    return 8, 4 * x + 2 * y + c, [((x ^ fx, y ^ fy, c ^ fc), 4 * (x ^ fx) + 2 * (y ^ fy) + (c ^ fc)) for fx, fy, fc in flips]


def _group_size(group):
    return {"sib": 2, "chip": 4, "all": 8}[group]


REMOTE_COPIES = {"gather": None, "scatter": None, "swap": 1, "gather2": 6}


def xchg(name, entries):
    n_in = sum(len(arrs) for _, _, arrs in entries)
    n_remote = sum(REMOTE_COPIES[k] or _group_size(g) - 1 for k, g, _ in entries)
    n_local = sum(1 for k, _, _ in entries if k != "swap")
    out_shape = []
    for kind, group, arrs in entries:
        a = arrs[0]
        if kind in ("gather", "gather2"):
            out_shape.append(jax.ShapeDtypeStruct((_group_size(group),) + a.shape, a.dtype))
        else:
            out_shape.append(jax.ShapeDtypeStruct(a.shape, a.dtype))

    def body(*refs):
        in_refs, out_refs = refs[:n_in], refs[n_in:n_in + len(entries)]
        send_sems, recv_sems, local_sems = refs[n_in + len(entries):]
        x, y, c = lax.axis_index("x"), lax.axis_index("y"), lax.axis_index("c")

        def remote(src, dst, k, dev):
            return pltpu.make_async_remote_copy(src_ref=src, dst_ref=dst, send_sem=send_sems.at[k], recv_sem=recv_sems.at[k],
                                                device_id=dev, device_id_type=MESH)

        pos, k, kl = 0, 0, 0
        forwards, finals = [], []
        for (kind, group, arrs), out in zip(entries, out_refs):
            srcs = in_refs[pos:pos + len(arrs)]
            pos += len(arrs)
            _, mine, peers = _group(group)
            if kind == "swap":
                (dev, _), = peers
                for core, src in ((0, srcs[1]), (1, srcs[0])):
                    @pl.when(c == core)
                    def _(src=src, k=k, dev=dev, out=out):
                        remote(src, out, k, dev).start()
                finals.append(remote(srcs[0], out, k, dev).wait)
                k += 1
                continue
            src = srcs[0]
            own = pltpu.make_async_copy(src if kind != "scatter" else src.at[mine], out.at[mine], local_sems.at[kl])
            own.start()
            finals.append(own.wait)
            kl += 1
            if kind == "gather2":
                sibling = (x, y, 1 - c)
                for f, (dev, slot) in enumerate(peers):
                    remote(src.at[c], out.at[mine, c], k + f, dev).start()
                    arrival = remote(src.at[c], out.at[slot, c], k + f, dev)

                    def forward(arrival=arrival, slot=slot, kf=k + 3 + f, out=out):
                        arrival.wait_recv()
                        remote(out.at[slot, c], out.at[slot, c], kf, sibling).start()

                    forwards.append(forward)
                    finals.append(arrival.wait_send)
                    finals.append(remote(out.at[slot, c], out.at[slot, 1 - c], k + 3 + f, sibling).wait)
                k += 6
                continue
            for dev, slot in peers:
                piece = src if kind == "gather" else src.at[slot]
                remote(piece, out.at[mine], k, dev).start()
                finals.append(remote(piece, out.at[slot], k, dev).wait)
                k += 1
        for run in forwards + finals:
            run()

    any_spec = pl.BlockSpec(memory_space=pl.ANY)
    return pl.pallas_call(
        body, name=name,
        in_specs=[any_spec] * n_in, out_specs=[any_spec] * len(entries), out_shape=out_shape,
        scratch_shapes=[pltpu.SemaphoreType.DMA((n_remote,)), pltpu.SemaphoreType.DMA((n_remote,)),
                        pltpu.SemaphoreType.DMA((max(n_local, 1),))],
    )(*[a for _, _, arrs in entries for a in arrs])


def _block_rows(r, c, budget=131072):
    tr = 8
    while tr * 2 * c <= budget and r % (tr * 2) == 0:
        tr *= 2
    return tr if r % tr == 0 else r


def tree_sum(name, parts):
    g, r, c = parts.shape
    tr = _block_rows(r, c)

    def body(p_ref, o_ref):
        p = [p_ref[i].astype(F32) for i in range(g)]
        while len(p) > 1:
            p = [p[i] + p[i + 1] for i in range(0, len(p), 2)]
        o_ref[...] = p[0]

    return pl.pallas_call(
        body, name=name, grid=(r // tr,),
        in_specs=[pl.BlockSpec((g, tr, c), lambda i: (0, i, 0))], out_specs=pl.BlockSpec((tr, c), lambda i: (i, 0)),
        out_shape=jax.ShapeDtypeStruct((r, c), F32), compiler_params=_params(("arbitrary",)),
    )(parts)


def pair_sum(name, g0, g1, recv, core):
    r, c = recv.shape
    tr = _block_rows(r, c)

    def body(a_ref, b_ref, r_ref, k_ref, o_ref):
        o_ref[...] = jnp.where(k_ref[...] > 0.5, b_ref[...], a_ref[...]) + r_ref[...]

    blk = pl.BlockSpec((tr, c), lambda i: (i, 0))
    return pl.pallas_call(
        body, name=name, grid=(r // tr,), in_specs=[blk, blk, blk, pl.BlockSpec((1, 1), lambda i: (0, 0))],
        out_specs=blk, out_shape=jax.ShapeDtypeStruct((r, c), F32), compiler_params=_params(("arbitrary",)),
    )(g0, g1, recv, core)


def adamw(name, w, g, m, v):
    r, c = w.shape
    tr = _block_rows(r, c)

    def body(w_ref, g_ref, m_ref, v_ref, d_ref, nm_ref, nv_ref):
        gg = g_ref[...]
        nm = ADAM_B1 * m_ref[...] + (1.0 - ADAM_B1) * gg
        nv = ADAM_B2 * v_ref[...] + (1.0 - ADAM_B2) * jnp.square(gg)
        m_hat = nm / (1.0 - ADAM_B1 ** ADAM_STEP)
        v_hat = nv / (1.0 - ADAM_B2 ** ADAM_STEP)
        d_ref[...] = -ADAM_LR * (m_hat / (jnp.sqrt(v_hat) + ADAM_EPS) + ADAM_WD * w_ref[...])
        nm_ref[...] = nm
        nv_ref[...] = nv

    blk = pl.BlockSpec((tr, c), lambda i: (i, 0))
    return pl.pallas_call(
        body, name=name, grid=(r // tr,), in_specs=[blk] * 4, out_specs=[blk] * 3,
        out_shape=[jax.ShapeDtypeStruct((r, c), F32)] * 3, compiler_params=_params(("arbitrary",)),
    )(w, g, m, v)


W_MOD_COLS = 6 * D // 4
MOD_TN = 512


def mod_project(c16, w_mod, b_loc):
    def body(c_ref, w_ref, b_ref, o_ref):
        cv = c_ref[...]
        s = (cv * _sigmoid(cv)).astype(BF16)
        o_ref[0] = _dot(s, w_ref[0].astype(BF16)) + b_ref[0]

    return pl.pallas_call(
        body, name="mod_project", grid=(2, W_MOD_COLS // MOD_TN),
        in_specs=[pl.BlockSpec((16, D), lambda l, j: (0, 0)), pl.BlockSpec((1, D, MOD_TN), lambda l, j: (l, 0, j)),
                  pl.BlockSpec((1, 1, MOD_TN), lambda l, j: (l, 0, j))],
        out_specs=pl.BlockSpec((1, 16, MOD_TN), lambda l, j: (l, 0, j)),
        out_shape=jax.ShapeDtypeStruct((2, 16, W_MOD_COLS), F32), compiler_params=_params(("arbitrary", "arbitrary")),
    )(c16, w_mod, b_loc)


def mod_weight_grad(c16, dm16):
    def body(c_ref, d_ref, o_ref):
        cv = c_ref[...]
        o_ref[0] = _dot(cv * _sigmoid(cv), d_ref[0], ((0,), (0,)), precision=HI)

    return pl.pallas_call(
        body, name="mod_weight_grad", grid=(2, W_MOD_COLS // MOD_TN),
        in_specs=[pl.BlockSpec((16, D), lambda l, j: (0, 0)), pl.BlockSpec((1, 16, MOD_TN), lambda l, j: (l, 0, j))],
        out_specs=pl.BlockSpec((1, D, MOD_TN), lambda l, j: (l, 0, j)),
        out_shape=jax.ShapeDtypeStruct((2, D, W_MOD_COLS), F32), compiler_params=_params(("arbitrary", "arbitrary")),
    )(c16, dm16)


def cctx_partial(dmc, w_mod):
    def body(d_ref, w_ref, o_ref):
        @pl.when(pl.program_id(0) == 0)
        def _():
            o_ref[...] = jnp.zeros_like(o_ref)
        o_ref[...] += _dot(d_ref[0], w_ref[0], ((1,), (1,)), precision=HI)

    return pl.pallas_call(
        body, name="cctx_partial", grid=(2,),
        in_specs=[pl.BlockSpec((1, 8, W_MOD_COLS), lambda l: (l, 0, 0)), pl.BlockSpec((1, D, W_MOD_COLS), lambda l: (l, 0, 0))],
        out_specs=pl.BlockSpec((8, D), lambda l: (0, 0)),
        out_shape=jax.ShapeDtypeStruct((8, D), F32), compiler_params=_params(("arbitrary",)),
    )(dmc, w_mod)


def cctx_grad(parts, c_ctx8):
    def body(p_ref, c_ref, o_ref):
        ds = (p_ref[0] + p_ref[1]) + (p_ref[2] + p_ref[3])
        _, vf = jax.vjp(lambda z: z * _sigmoid(z), c_ref[...])
        o_ref[...] = vf(ds)[0]

    return pl.pallas_call(
        body, name="cctx_grad", out_shape=jax.ShapeDtypeStruct((8, D), F32),
    )(parts, c_ctx8)


ARG_NAMES = ("x", "c", "ctx", "c_ctx", "w_mod", "b_mod", "norm1_w", "w_in", "w_out", "sgu_norm_w", "sgu_norm_b", "sgu_w",
             "sgu_b", "gla_wg_fwd", "gla_bg_fwd", "gla_wg_bwd", "gla_bg_bwd", "gla_norm_w", "mla_q_norm_w", "mla_w_uq",
             "mla_kv_norm_w", "mla_w_ukv", "norm2_w", "w_ff1", "w_ff2", "final_norm_w")
WEIGHT_NAMES = ARG_NAMES[3:]
PACKED = ("c_ctx", "b_mod") + SMALL_NAMES + ("final_norm_w",)
ROW_SHARDED = ("w_out", "w_ff2")
PACK_ROWS = 256


def _pack(vectors):
    flat = jnp.concatenate([v.reshape(-1) for v in vectors])
    n = flat.shape[0]
    total = -(-n // (PACK_ROWS * LANES)) * PACK_ROWS * LANES
    return jnp.pad(flat, (0, total - n)).reshape(-1, LANES)


def _unpack(buf, shapes):
    flat, out, pos = buf.reshape(-1), [], 0
    for shp in shapes:
        n = int(np.prod(shp))
        out.append(flat[pos:pos + n].reshape(shp))
        pos += n
    return out


def _full_weight(name, gathered, l):
    g = gathered[:, l]
    if name in ROW_SHARDED:
        return g.reshape(-1, g.shape[-1])
    return g.transpose(1, 0, 2).reshape(g.shape[1], -1)


def _chip_chunks(name, a):
    if name in ROW_SHARDED:
        return a.reshape(4, a.shape[0] // 4, a.shape[1])
    return a.reshape(a.shape[0], 4, a.shape[1] // 4).transpose(1, 0, 2)


def kernel(x, c, ctx, c_ctx, w_mod, b_mod, norm1_w, w_in, w_out, sgu_norm_w, sgu_norm_b, sgu_w, sgu_b, gla_wg_fwd, gla_bg_fwd, gla_wg_bwd, gla_bg_bwd, gla_norm_w, mla_q_norm_w, mla_w_uq, mla_kv_norm_w, mla_w_ukv, norm2_w, w_ff1, w_ff2, final_norm_w, loss_target, m_c_ctx, m_w_mod, m_b_mod, m_norm1_w, m_w_in, m_w_out, m_sgu_norm_w, m_sgu_norm_b, m_sgu_w, m_sgu_b, m_gla_wg_fwd, m_gla_bg_fwd, m_gla_wg_bwd, m_gla_bg_bwd, m_gla_norm_w, m_mla_q_norm_w, m_mla_w_uq, m_mla_kv_norm_w, m_mla_w_ukv, m_norm2_w, m_w_ff1, m_w_ff2, m_final_norm_w, v_c_ctx, v_w_mod, v_b_mod, v_norm1_w, v_w_in, v_w_out, v_sgu_norm_w, v_sgu_norm_b, v_sgu_w, v_sgu_b, v_gla_wg_fwd, v_gla_bg_fwd, v_gla_wg_bwd, v_gla_bg_bwd, v_gla_norm_w, v_mla_q_norm_w, v_mla_w_uq, v_mla_kv_norm_w, v_mla_w_ukv, v_norm2_w, v_w_ff1, v_w_ff2, v_final_norm_w):
    args = (x, c, ctx, c_ctx, w_mod, b_mod, norm1_w, w_in, w_out, sgu_norm_w, sgu_norm_b, sgu_w, sgu_b, gla_wg_fwd, gla_bg_fwd, gla_wg_bwd, gla_bg_bwd, gla_norm_w, mla_q_norm_w, mla_w_uq, mla_kv_norm_w, mla_w_ukv, norm2_w, w_ff1, w_ff2, final_norm_w)
    w = dict(zip(ARG_NAMES, args))
    moms = (m_c_ctx, m_w_mod, m_b_mod, m_norm1_w, m_w_in, m_w_out, m_sgu_norm_w, m_sgu_norm_b, m_sgu_w, m_sgu_b, m_gla_wg_fwd, m_gla_bg_fwd, m_gla_wg_bwd, m_gla_bg_bwd, m_gla_norm_w, m_mla_q_norm_w, m_mla_w_uq, m_mla_kv_norm_w, m_mla_w_ukv, m_norm2_w, m_w_ff1, m_w_ff2, m_final_norm_w)
    vars_ = (v_c_ctx, v_w_mod, v_b_mod, v_norm1_w, v_w_in, v_w_out, v_sgu_norm_w, v_sgu_norm_b, v_sgu_w, v_sgu_b, v_gla_wg_fwd, v_gla_bg_fwd, v_gla_wg_bwd, v_gla_bg_bwd, v_gla_norm_w, v_mla_q_norm_w, v_mla_w_uq, v_mla_kv_norm_w, v_mla_w_ukv, v_norm2_w, v_w_ff1, v_w_ff2, v_final_norm_w)
    m1 = dict(zip(WEIGHT_NAMES, moms))
    m2 = dict(zip(WEIGHT_NAMES, vars_))
    xi, yi, ci = lax.axis_index("x"), lax.axis_index("y"), lax.axis_index("c")
    chip, dev = 2 * xi + yi, 4 * xi + 2 * yi + ci
    depth = w_mod.shape[0]

    got = xchg("gather_inputs", [("gather", "all", [c])] + [("gather2", "chip", [w[k].astype(BF16)]) for k in BIG_NAMES])
    c_all, shards = got[0], dict(zip(BIG_NAMES, got[1:]))
    c16 = jnp.concatenate([c_all.reshape(8, D), c_ctx[None], jnp.zeros((7, D), F32)], axis=0)
    b_loc = lax.dynamic_slice_in_dim(b_mod, chip * W_MOD_COLS, W_MOD_COLS, axis=1)[:, None, :]
    mod_part = mod_project(c16, w_mod, b_loc)
    mod_all, = xchg("gather_mod", [("gather", "chip", [mod_part])])
    mod_full = mod_all.transpose(1, 2, 0, 3).reshape(depth, 16, 6 * D)
    mods = [jnp.stack([mod_full[l, 8], lax.dynamic_index_in_dim(mod_full[l], dev, 0, keepdims=False)])[:, None, :]
            for l in range(depth)]

    big = [{k: _full_weight(k, shards[k], l) for k in BIG_NAMES} for l in range(depth)]
    small = [{k: w[k][l] for k in SMALL_NAMES} for l in range(depth)]
    loss, grad_x, dmods, gbig, gsmall, dfnw = local_step(x[0], ctx[0], loss_target[0], mods, big, small, final_norm_w)
    loss = lax.psum(loss[0, 0], ("x", "y", "c"))

    dm_lat = jnp.stack([dmods[l][1, 0] for l in range(depth)])
    dm_ctx = jnp.stack([dmods[l][0, 0] for l in range(depth)])
    small_pack = _pack([dm_lat, dm_ctx] + [jnp.stack([gsmall[l][k] for l in range(depth)]) for k in SMALL_NAMES] + [dfnw])
    got = xchg("exchange_grads", [("gather", "all", [small_pack])] + [("swap", "sib", [gbig[0][k], gbig[1][k]]) for k in BIG_NAMES])
    small_all, from_sib = got[0], dict(zip(BIG_NAMES, got[1:]))
    small_sum = tree_sum("small_grad_sum", small_all)
    core = ci.astype(F32).reshape(1, 1)
    mine = {k: pair_sum(f"pair_sum_{k}", gbig[0][k], gbig[1][k], from_sib[k], core) for k in BIG_NAMES}

    n_dm = depth * 6 * D
    dm_rows = n_dm // LANES
    dm_lat_all = small_all[:, :dm_rows].reshape(8, depth, 6 * D)
    dm_ctx_sum = small_sum[dm_rows:2 * dm_rows].reshape(depth, 6 * D)
    take = lambda a: lax.dynamic_slice_in_dim(a, chip * W_MOD_COLS, W_MOD_COLS, axis=-1)
    dmc_loc = take(dm_ctx_sum)
    cc_part = cctx_partial(jnp.pad(dmc_loc[:, None, :], ((0, 0), (0, 7), (0, 0))), w_mod)
    got = xchg("scatter_grads", [("gather", "chip", [cc_part])]
               + [("scatter", "chip", [_chip_chunks(k, mine[k]).astype(BF16)]) for k in BIG_NAMES])
    cc_parts, chunks_in = got[0], dict(zip(BIG_NAMES, got[1:]))
    reduced = {k: tree_sum(f"chip_sum_{k}", chunks_in[k]) for k in BIG_NAMES}
    g_c_ctx = cctx_grad(cc_parts, jnp.broadcast_to(c_ctx[None], (8, D)))[0]

    got = xchg("share_layers", [("swap", "sib", [reduced[k], reduced[k]]) for k in BIG_NAMES])
    grads = {k: jnp.where(ci == 0, jnp.stack([reduced[k], r]), jnp.stack([r, reduced[k]])) for k, r in zip(BIG_NAMES, got)}

    dm16 = jnp.concatenate([take(dm_lat_all).transpose(1, 0, 2), dmc_loc[:, None, :], jnp.zeros((depth, 7, W_MOD_COLS), F32)], axis=1)
    grads["w_mod"] = mod_weight_grad(c16, dm16)
    flat_sum = small_sum.reshape(-1)
    g_b_mod = (flat_sum[:n_dm] + flat_sum[n_dm:2 * n_dm]).reshape(depth, 6 * D)
    rest_shapes = [w[k].shape for k in PACKED[2:]]
    n_rest = sum(int(np.prod(s)) for s in rest_shapes)
    for k, g in zip(PACKED, [g_c_ctx, g_b_mod] + _unpack(flat_sum[2 * n_dm:2 * n_dm + n_rest], rest_shapes)):
        grads[k] = g

    delta, new_m, new_v = {}, {}, {}
    for k in BIG_NAMES + ("w_mod",):
        view = lambda a: a.reshape(-1, a.shape[-1])
        d_, m_, v_ = adamw(f"adamw_{k}", view(w[k]), view(grads[k]), view(m1[k]), view(m2[k]))
        delta[k], new_m[k], new_v[k] = d_.reshape(w[k].shape), m_.reshape(w[k].shape), v_.reshape(w[k].shape)
    shapes = [w[k].shape for k in PACKED]
    d_, m_, v_ = adamw("adamw_small", _pack([w[k] for k in PACKED]), _pack([grads[k] for k in PACKED]),
                       _pack([m1[k] for k in PACKED]), _pack([m2[k] for k in PACKED]))
    for k, dk, mk, vk in zip(PACKED, _unpack(d_, shapes), _unpack(m_, shapes), _unpack(v_, shapes)):
        delta[k], new_m[k], new_v[k] = dk, mk, vk
    return (loss, grad_x[None], *[grads[k] for k in WEIGHT_NAMES], *[delta[k] for k in WEIGHT_NAMES],
            *[new_m[k] for k in WEIGHT_NAMES], *[new_v[k] for k in WEIGHT_NAMES])
```

```python
import functools
import math

import numpy as np
import jax
import jax.numpy as jnp
from jax import lax
from jax.experimental import pallas as pl
from jax.experimental.pallas import tpu as pltpu

F32 = jnp.float32
BF16 = jnp.bfloat16
HI = lax.Precision.HIGHEST
EPS = 1e-6
VMEM_LIMIT_BYTES = 56 * 1024 * 1024
LANES = 128

D = 1024
D_FF = 4096
CTX = 256
GRID_W = 64
SGU_CHUNK = 128
GLA_CHUNK = 64
GLA_TAU = 16.0
GLA_DK = 32
MLA_SCALE = (128 + 64) ** -0.5
SCORE_SCALE = MLA_SCALE * math.log2(math.e)
LN2 = math.log(2.0)
ROPE_BASE = 10000.0
TM = 256
NCTXB = CTX // TM
P_GV, P_CKV, P_SU, P_SV, P_GR, P_DQ, P_GK, P_GATE, P_KR, P_GQ = 0, 256, 512, 768, 1024, 1280, 1536, 1664, 1792, 1920
P_COLS = 2048
IN_GROUPS = ((0, 128, P_GK), (128, 256, P_GV), (384, 32, P_GATE), (416, 256, P_CKV), (672, 64, P_KR),
             (736, 256, P_SU), (992, 256, P_SV), (1248, 128, P_GQ), (1376, 256, P_GR), (1632, 256, P_DQ))
ADAM_LR, ADAM_B1, ADAM_B2, ADAM_EPS, ADAM_WD, ADAM_STEP = 0.001, 0.9, 0.999, 1e-08, 0.01, 10
MESH = pl.DeviceIdType.MESH


def _params(sem):
    return pltpu.CompilerParams(dimension_semantics=sem, vmem_limit_bytes=VMEM_LIMIT_BYTES)


def _pick(n, cands):
    for c in cands:
        if n % c == 0:
            return c
    return n


class Op:
    def __init__(self, arr, blk, idx, gshape, gidx, acc):
        self.arr, self.blk, self.idx, self.gshape, self.gidx, self.acc = arr, blk, idx, gshape, gidx, acc

    def spec(self):
        return pl.BlockSpec(self.blk, self.idx)


def rows(arr, width=None, cb=0, off=0, tm=TM):
    w = arr.shape[1] if width is None else width
    n = arr.shape[0] - off * tm
    return Op(arr, (tm, w), lambda i: (i + off, cb), (n, w), lambda i: (i, 0), False)


def chunks(arr, per_tile):
    z = (0,) * (arr.ndim - 1)
    return Op(arr, (per_tile,) + arr.shape[1:], lambda i: (i,) + z, arr.shape, lambda i: (i,) + z, False)


def const(arr):
    z = (0,) * arr.ndim
    return Op(arr, arr.shape, lambda i: z, arr.shape, lambda i: z, True)


def rw(name, fn, ins, outs, grid):
    nin = len(ins)

    def body(*refs):
        vals = [r[...] for r in refs[:nin]]
        res = fn(pl.program_id(0), *vals)
        for o, r in zip(refs[nin:], res):
            o[...] = r.astype(o.dtype)

    return pl.pallas_call(
        body, name=name, grid=(grid,),
        in_specs=[o.spec() for o in ins],
        out_specs=[pl.BlockSpec(b, ix) for (_, _, b, ix) in outs],
        out_shape=[jax.ShapeDtypeStruct(s, d) for (s, d, _, _) in outs],
        compiler_params=_params(("arbitrary",)),
    )(*[o.arr for o in ins])


def rowout(n, w, dtype, tm=TM):
    return ((n, w), dtype, (tm, w), lambda i: (i, 0))


def chunkout(shape, dtype, per_tile):
    z = (0,) * (len(shape) - 1)
    return (shape, dtype, (per_tile,) + tuple(shape[1:]), lambda i: (i,) + z)


def rw_vjp(name, fn, ins, cots, wrt, grid, gdt=None, adds=None):
    nin = len(ins)
    cot_ops = [c for c in cots if c is not None]
    add_items = sorted((adds or {}).items())
    gdt = gdt or [F32] * len(wrt)
    ncot, nadd = len(cot_ops), len(add_items)

    def body(*refs):
        i = pl.program_id(0)
        vals = [r[...] for r in refs[:nin]]
        cvals = [r[...] for r in refs[nin:nin + ncot]]
        avals = [r[...] for r in refs[nin + ncot:nin + ncot + nadd]]
        grefs = refs[nin + ncot + nadd:]

        def f(*d):
            a = list(vals)
            for k, dv in zip(wrt, d):
                a[k] = dv
            return tuple(fn(i, *a))

        outs, vf = jax.vjp(f, *[vals[k] for k in wrt])
        it = iter(cvals)
        ct = tuple(jnp.zeros_like(o) if c is None else next(it).astype(o.dtype) for c, o in zip(cots, outs))
        gs = list(vf(ct))
        for (pos, _), av in zip(add_items, avals):
            gs[pos] = gs[pos].astype(F32) + av.astype(F32)
        for pos, (k, g, gref) in enumerate(zip(wrt, gs, grefs)):
            if ins[k].acc:
                @pl.when(i == 0)
                def _():
                    gref[...] = jnp.zeros_like(gref)
                gref[...] += g.astype(gref.dtype)
            else:
                gref[...] = g.astype(gref.dtype)

    all_in = list(ins) + cot_ops + [op for _, op in add_items]
    return pl.pallas_call(
        body, name=name, grid=(grid,),
        in_specs=[o.spec() for o in all_in],
        out_specs=[pl.BlockSpec(ins[k].blk, ins[k].gidx) for k in wrt],
        out_shape=[jax.ShapeDtypeStruct(ins[k].gshape, dt) for k, dt in zip(wrt, gdt)],
        compiler_params=_params(("arbitrary",)),
    )(*[o.arr for o in all_in])


MM_VMEM_BUDGET = 40 * 1024 * 1024
MM_COLS = 1024


def _square_bf16(a):
    a = a.astype(F32)
    return (a * a).astype(BF16)


def mm(name, a, b, out_dtype, pre=None, post=None, extras=(), bt=False):
    m, k = a.shape
    n = b.shape[0] if bt else b.shape[1]
    nc = min(n, MM_COLS)
    row_bytes = k * a.dtype.itemsize + n * jnp.dtype(out_dtype).itemsize + sum(n * e.dtype.itemsize for e in extras)
    tm = next(t for t in (768, 512, 384, 256, 128, 64)
              if m % t == 0 and 2 * t * row_bytes + 2 * k * n * b.dtype.itemsize + t * nc * 4 <= MM_VMEM_BUDGET)

    def body(a_ref, b_ref, *rest):
        o_ref = rest[-1]
        av = a_ref[...]
        if pre is not None:
            av = pre(av)
        for j in range(n // nc):
            cs = slice(j * nc, (j + 1) * nc)
            if bt:
                acc = lax.dot_general(av, b_ref[cs, :], (((1,), (1,)), ((), ())), preferred_element_type=F32)
            else:
                acc = lax.dot_general(av, b_ref[:, cs], (((1,), (0,)), ((), ())), preferred_element_type=F32)
            if post is not None:
                acc = post(acc, *[e[:, cs] for e in rest[:-1]])
            o_ref[:, cs] = acc.astype(o_ref.dtype)

    row = lambda w: pl.BlockSpec((tm, w), lambda i: (i, 0))
    return pl.pallas_call(
        body, name=name, grid=(m // tm,),
        in_specs=[row(k), pl.BlockSpec(b.shape, lambda i: (0, 0))] + [row(n) for _ in extras],
        out_specs=row(n),
        out_shape=jax.ShapeDtypeStruct((m, n), out_dtype),
        compiler_params=_params(("arbitrary",)),
    )(a, b, *extras)


def mm_tn(name, a, b, pre=None):
    m, ka = a.shape
    _, nb = b.shape
    tm = _pick(m, (768, 512, 256))
    ta = _pick(ka, (2048, 1024, 512, 256, 128))
    tb = _pick(nb, tuple(t for t in (4096, 2048, 1024, 512, 256, 128) if ta * t * 4 <= 8 * 1024 * 1024))

    def body(a_ref, b_ref, o_ref):
        @pl.when(pl.program_id(2) == 0)
        def _():
            o_ref[...] = jnp.zeros_like(o_ref)
        av = a_ref[...] if pre is None else pre(a_ref[...])
        o_ref[...] += lax.dot_general(av, b_ref[...], (((0,), (0,)), ((), ())), preferred_element_type=F32)

    return pl.pallas_call(
        body, name=name, grid=(ka // ta, nb // tb, m // tm),
        in_specs=[pl.BlockSpec((tm, ta), lambda i, j, k: (k, i)), pl.BlockSpec((tm, tb), lambda i, j, k: (k, j))],
        out_specs=pl.BlockSpec((ta, tb), lambda i, j, k: (i, j)),
        out_shape=jax.ShapeDtypeStruct((ka, nb), F32),
        compiler_params=_params(("arbitrary", "arbitrary", "arbitrary")),
    )(a, b)


def _rms(x, w):
    return x * lax.rsqrt(jnp.mean(x * x, axis=-1, keepdims=True) + EPS) * w


def _mod_of(blk, m):
    return jnp.where(blk < NCTXB, m[0], m[1])


def _gelu(x):
    return x * (0.5 * (1.0 + jnp.tanh(math.sqrt(2.0 / math.pi) * (x + 0.044715 * (x * x * x)))))


def _sigmoid(x):
    return 1.0 / (1.0 + jnp.exp(-x))


def _log_sigmoid(z):
    return jnp.minimum(z, 0.0) - jnp.log(1.0 + jnp.exp(-jnp.abs(z)))


def _dot(a, b, dims=((1,), (0,)), precision=None):
    return lax.dot_general(a, b, (dims, ((), ())), precision=precision, preferred_element_type=F32)


def _lane_group_mask(width, group, h):
    lane = lax.broadcasted_iota(jnp.int32, (1, width), 1)
    return (lane >= h * group) & (lane < (h + 1) * group)


def fn_norm1(blk, x, m, nw):
    mv = _mod_of(blk, m)
    return ((_rms(x, nw) * (1.0 + mv[:, D:2 * D]) + mv[:, 0:D]),)


def fn_res_norm2(blk, x, yo, m, nw):
    mv = _mod_of(blk, m)
    x1 = x + mv[:, 2 * D:3 * D] * yo
    return x1, _rms(x1, nw) * (1.0 + mv[:, 4 * D:5 * D]) + mv[:, 3 * D:4 * D]


def fn_res2(blk, x1, f, m):
    mv = _mod_of(blk, m)
    return (x1 + mv[:, 5 * D:6 * D] * f,)


def fn_sgu(blk, su, sv, nw, nb, ws, bm):
    u = _gelu(su)
    g = _gelu(sv)
    mu = jnp.mean(g, axis=-1, keepdims=True)
    var = jnp.mean(jnp.square(g - mu), axis=-1, keepdims=True)
    v = (g - mu) * lax.rsqrt(var + EPS) * nw + nb
    out = []
    for c in range(su.shape[0] // SGU_CHUNK):
        vc = v[c * SGU_CHUNK:(c + 1) * SGU_CHUNK]
        s = bm
        for h in range(4):
            vh = jnp.where(_lane_group_mask(256, 64, h), vc, 0.0)
            s = s + _dot(ws[h].astype(BF16), vh.astype(BF16))
        out.append(u[c * SGU_CHUNK:(c + 1) * SGU_CHUNK] * s)
    return (jnp.concatenate(out, axis=0),)


def fn_gates(blk, pg, wg, bg):
    z = _dot(pg.astype(BF16), wg.astype(BF16)) + bg
    g = _log_sigmoid(z) * (1.0 / GLA_TAU)
    return g[:, :128], g[:, 128:]


def _scan_rows(x, rev):
    n = x.shape[0]
    row = lax.broadcasted_iota(jnp.int32, x.shape, 0)
    d = 1
    while d < n:
        if rev:
            x = x + jnp.where(row < n - d, pltpu.roll(x, n - d, 0), 0.0)
        else:
            x = x + jnp.where(row >= d, pltpu.roll(x, d, 0), 0.0)
        d *= 2
    return x


@functools.partial(jax.custom_vjp, nondiff_argnums=(1,))
def _cumsum_rows(x, rev):
    return _scan_rows(x, rev)


def _cumsum_rows_fwd(x, rev):
    return _scan_rows(x, rev), None


def _cumsum_rows_bwd(rev, _, dy):
    return (_scan_rows(dy, not rev),)


_cumsum_rows.defvjp(_cumsum_rows_fwd, _cumsum_rows_bwd)


def _gla_chunk_terms(g, rev):
    return _cumsum_rows(g, rev), jnp.sum(g, axis=0, keepdims=True)


def _bd_mask():
    r = lax.broadcasted_iota(jnp.int32, (128, 256), 0)
    c = lax.broadcasted_iota(jnp.int32, (128, 256), 1)
    return (r // GLA_DK) == (c // 64)


def _gla_kv_chunk(k, v, g, rev):
    b, tot = _gla_chunk_terms(g, rev)
    kd = k * jnp.exp(tot - b)
    u = jnp.where(_bd_mask(), _dot(kd.astype(BF16), v.astype(BF16), ((0,), (0,))), 0.0)
    r = lax.broadcasted_iota(jnp.int32, (128, 128), 0)
    c = lax.broadcasted_iota(jnp.int32, (128, 128), 1)
    col = jnp.sum(jnp.where(r == c, jnp.broadcast_to(jnp.exp(tot), (128, 128)), 0.0), axis=1, keepdims=True)
    return u, jnp.broadcast_to(col, (128, 128))


def _gla_o_chunk(q, k, v, g, s, rev):
    b, _ = _gla_chunk_terms(g, rev)
    qe = q * jnp.exp(b) * (GLA_DK ** -0.5)
    ke = k * jnp.exp(-b)
    o = _dot(qe.astype(BF16), jnp.where(_bd_mask(), s, 0.0).astype(BF16))
    qs = jnp.concatenate([jnp.where(_lane_group_mask(128, GLA_DK, h), qe, 0.0) for h in range(4)], axis=0)
    a = _dot(qs.astype(BF16), ke.astype(BF16), ((1,), (1,)))
    i = lax.broadcasted_iota(jnp.int32, a.shape, 0) % GLA_CHUNK
    j = lax.broadcasted_iota(jnp.int32, a.shape, 1)
    a = jnp.where((j >= i) if rev else (j <= i), a, 0.0)
    av = _dot(a.astype(BF16), v.astype(BF16))
    for h in range(4):
        o = o + jnp.where(_lane_group_mask(256, 64, h), av[GLA_CHUNK * h:GLA_CHUNK * (h + 1)], 0.0)
    return o


def fn_gla_kv(blk, k, v, gf, gb):
    uf, ef, ub, eb = [], [], [], []
    for c in range(k.shape[0] // GLA_CHUNK):
        sl = slice(c * GLA_CHUNK, (c + 1) * GLA_CHUNK)
        u, e = _gla_kv_chunk(k[sl], v[sl], gf[sl], False)
        uf.append(u[None]); ef.append(e[None])
        u, e = _gla_kv_chunk(k[sl], v[sl], gb[sl], True)
        ub.append(u[None]); eb.append(e[None])
    cat = lambda t: jnp.concatenate(t, axis=0)
    return cat(uf), cat(ef), cat(ub), cat(eb)


def fn_gla_o(blk, q, k, v, gf, gb, gr, sf, sb, nwt):
    out = []
    for c in range(q.shape[0] // GLA_CHUNK):
        sl = slice(c * GLA_CHUNK, (c + 1) * GLA_CHUNK)
        out.append(_gla_o_chunk(q[sl], k[sl], v[sl], gf[sl], sf[c], False)
                   + _gla_o_chunk(q[sl], k[sl], v[sl], gb[sl], sb[c], True))
    o = jnp.concatenate(out, axis=0)
    r = lax.broadcasted_iota(jnp.int32, (256, 256), 0)
    c = lax.broadcasted_iota(jnp.int32, (256, 256), 1)
    head_mean = jnp.where((r // 64) == (c // 64), 1.0 / 64.0, 0.0).astype(F32)
    ms = _dot(o * o, head_mean, precision=HI)
    on = o * lax.rsqrt(ms + EPS) * nwt
    return (on * (gr * _sigmoid(gr)),)


def _rope_partner(x):
    lane = lax.broadcasted_iota(jnp.int32, x.shape, 1)
    return jnp.where((lane // 16) % 2 == 0, pltpu.roll(x, LANES - 16, 1), pltpu.roll(x, 16, 1))


@jax.custom_vjp
def _rope(x, cs, sn):
    return x * cs + _rope_partner(x) * sn


def _rope_fwd(x, cs, sn):
    return _rope(x, cs, sn), (cs, sn)


def _rope_bwd(res, dy):
    cs, sn = res
    return dy * cs + _rope_partner(dy * sn), jnp.zeros_like(cs), jnp.zeros_like(sn)


_rope.defvjp(_rope_fwd, _rope_bwd)


def fn_mla_pre(blk, ckv, dq, kvw, qw):
    return _rms(ckv, kvw), _rms(dq, qw)


def fn_mla_post(blk, kk, qu, kr, cs, sn):
    kro = _rope(kr, cs, sn)
    kcat, q = [], []
    for h in range(4):
        kcat += [kk[:, 128 * h:128 * (h + 1)].astype(F32), kro]
        q += [qu[:, 256 * h:256 * h + 128], _rope(qu[:, 256 * h + 128:256 * (h + 1)], cs, sn)]
    return jnp.concatenate(kcat, axis=1), jnp.concatenate(q, axis=1) * SCORE_SCALE


ATTN_ROWS = 256
NEG = -1e30


def _scores(q, k, k0, context_queries):
    s = _dot(q, k, ((1,), (1,)))
    if context_queries is not None:
        col = k0 + lax.broadcasted_iota(jnp.int32, s.shape, 1)
        s = jnp.where(context_queries & (col >= CTX), NEG, s)
    return s


def flash_fwd(q, kcat, kvu, side=()):
    t = q.shape[0]
    tq = _pick(t, (768, 512, 256))
    tk = _pick(t, (2816, 1536, 768, 512, 256))
    nsub = tq // ATTN_ROWS
    n_side_in, _, _, side_shapes = _exchange_shapes(side)
    n_side = len(side)

    def body(q_ref, k_ref, v_ref, *rest):
        side_in, rest = rest[:n_side_in], rest[n_side_in:]
        o_ref, lse_ref = rest[:2]
        side_out, (m_sc, l_sc, acc_sc), side_sems = rest[2:2 + n_side], rest[2 + n_side:5 + n_side], rest[5 + n_side:]
        h, qi, ki = pl.program_id(0), pl.program_id(1), pl.program_id(2)
        if side:
            starts, forwards, finals = _exchange_phases(side, side_in, side_out, *side_sems)
            at_tile0 = (qi == 0) & (ki == 0)
            last = (h == pl.num_programs(0) - 1) & (qi == pl.num_programs(1) - 1) & (ki == pl.num_programs(2) - 1)
            for when, phase in (((h == 0) & at_tile0, starts), ((h == 2) & at_tile0, forwards)):
                @pl.when(when)
                def _(phase=phase):
                    for run in phase:
                        run()

        @pl.when(ki == 0)
        def _():
            m_sc[...] = jnp.full_like(m_sc, NEG)
            l_sc[...] = jnp.zeros_like(l_sc)
            acc_sc[...] = jnp.zeros_like(acc_sc)

        k, v = k_ref[...], v_ref[...]
        chains = [pl.ds(r * ATTN_ROWS, ATTN_ROWS) for r in range(nsub)]
        scores = [_scores(q_ref[rs, :], k, ki * tk, (qi == 0) if r == 0 else None) for r, rs in enumerate(chains)]
        probs = []
        for rs, s in zip(chains, scores):
            m_old = m_sc[rs, :]
            m_new = jnp.maximum(m_old, jnp.max(s, axis=-1, keepdims=True))
            alpha = jnp.exp2(m_old - m_new)
            p = jnp.exp2(s - m_new)
            l_sc[rs, :] = alpha * l_sc[rs, :] + jnp.sum(p, axis=-1, keepdims=True)
            m_sc[rs, :] = m_new
            probs.append((alpha, p.astype(BF16)))
        for rs, (alpha, p) in zip(chains, probs):
            acc_sc[rs, :] = alpha * acc_sc[rs, :] + _dot(p, v)

        @pl.when(ki == pl.num_programs(2) - 1)
        def _():
            o_ref[...] = acc_sc[...] / l_sc[...]
            lse_ref[...] = jnp.broadcast_to(m_sc[...] + jnp.log2(l_sc[...]), lse_ref.shape)

        if side:
            @pl.when(last)
            def _():
                for run in finals:
                    run()

    any_spec = pl.BlockSpec(memory_space=pl.ANY)
    res = pl.pallas_call(
        body, name="mla_flash_fwd", grid=(4, t // tq, t // tk),
        in_specs=[pl.BlockSpec((tq, 256), lambda h, i, j: (i, h)), pl.BlockSpec((tk, 256), lambda h, i, j: (j, h)),
                  pl.BlockSpec((tk, 128), lambda h, i, j: (j, 4 + h))] + [any_spec] * n_side_in,
        out_specs=[pl.BlockSpec((tq, 128), lambda h, i, j: (i, h)), pl.BlockSpec((tq, 128), lambda h, i, j: (i, h))]
        + [any_spec] * n_side,
        out_shape=[jax.ShapeDtypeStruct((t, 512), F32), jax.ShapeDtypeStruct((t, 512), F32)] + side_shapes,
        scratch_shapes=[pltpu.VMEM((tq, 1), F32), pltpu.VMEM((tq, 1), F32), pltpu.VMEM((tq, 128), F32)]
        + (_exchange_sems(side) if side else []),
        compiler_params=_params(("arbitrary", "arbitrary", "arbitrary")),
    )(q, kcat, kvu, *[a for _, _, arrs in side for a in arrs])
    return res[0], res[1], list(res[2:])


def fn_attn_stats(blk, do, o, lse):
    out = []
    for h in range(4):
        hs = slice(128 * h, 128 * (h + 1))
        d = jnp.sum(do[:, hs] * o[:, hs], axis=-1, keepdims=True)
        out.append(lse[:, hs].T[0:8])
        out.append(jnp.broadcast_to(d, (do.shape[0], 128)).T[0:8])
    return (jnp.concatenate(out, axis=0)[None],)


def flash_bwd(q, kcat, kvu, dy, stats):
    t = q.shape[0]
    tq = _pick(t, (2816, 768, 512, 256))
    tk = _pick(t, (768, 512, 256))
    nst = tq // TM

    def body(q_ref, k_ref, v_ref, do_ref, st_ref, dq_ref, dk_ref, dv_ref):
        kj, qi = pl.program_id(1), pl.program_id(2)

        @pl.when(qi == 0)
        def _():
            dk_ref[...] = jnp.zeros_like(dk_ref)
            dv_ref[...] = jnp.zeros_like(dv_ref)

        q_, k, v, do = q_ref[...], k_ref[...], v_ref[...], do_ref[...].astype(BF16)
        lse_row = jnp.concatenate([st_ref[u, 0:1, :] for u in range(nst)], axis=1)
        delta_row = jnp.concatenate([st_ref[u, 8:9, :] for u in range(nst)], axis=1)
        s = _dot(k, q_, ((1,), (1,)))
        key = kj * tk + lax.broadcasted_iota(jnp.int32, s.shape, 0)
        qry = qi * tq + lax.broadcasted_iota(jnp.int32, s.shape, 1)
        s = jnp.where((qry < CTX) & (key >= CTX), NEG, s)
        p = jnp.exp2(s - lse_row)
        dp = _dot(v, do, ((1,), (1,)))
        ds = (p * (dp - delta_row)).astype(BF16)
        dv_ref[...] += _dot(p.astype(BF16), do)
        dk_ref[...] += LN2 * _dot(ds, q_)
        dq_new = LN2 * _dot(ds, k, ((0,), (0,)))
        rows_ = pl.ds(pl.multiple_of(qi * tq, TM), tq)

        @pl.when(kj == 0)
        def _():
            dq_ref[rows_, :] = dq_new

        @pl.when(kj != 0)
        def _():
            dq_ref[rows_, :] += dq_new

    return pl.pallas_call(
        body, name="mla_flash_bwd", grid=(4, t // tk, t // tq),
        in_specs=[pl.BlockSpec((tq, 256), lambda h, j, i: (i, h)), pl.BlockSpec((tk, 256), lambda h, j, i: (j, h)),
                  pl.BlockSpec((tk, 128), lambda h, j, i: (j, 4 + h)), pl.BlockSpec((tq, 128), lambda h, j, i: (i, 4 + h)),
                  pl.BlockSpec((nst, 16, 256), lambda h, j, i: (i, h, 0))],
        out_specs=[pl.BlockSpec((t, 256), lambda h, j, i: (0, h)), pl.BlockSpec((tk, 256), lambda h, j, i: (j, h)),
                   pl.BlockSpec((tk, 128), lambda h, j, i: (j, h))],
        out_shape=[jax.ShapeDtypeStruct((t, 1024), F32), jax.ShapeDtypeStruct((t, 1024), F32),
                   jax.ShapeDtypeStruct((t, 512), F32)],
        compiler_params=_params(("arbitrary", "arbitrary", "arbitrary")),
    )(q, kcat, kvu, dy, stats)


SCAN_BLOCK = CTX // GLA_CHUNK


def _scan_block(t, nb, rev):
    if not rev:
        return t
    return jnp.where(t < 1, 0, nb - t)


def _scan_order(rev):
    return tuple(reversed(range(SCAN_BLOCK))) if rev else tuple(range(SCAN_BLOCK))


def _both_halves(e):
    return jnp.concatenate([e, e], axis=1)


def gla_states(uf, ef, ub, eb):
    nb = uf.shape[0] // SCAN_BLOCK

    def body(uf_ref, ef_ref, ub_ref, eb_ref, sf_ref, sb_ref, sf_sc, sb_sc):
        @pl.when(pl.program_id(0) == 0)
        def _():
            sf_sc[...] = jnp.zeros_like(sf_sc)
            sb_sc[...] = jnp.zeros_like(sb_sc)

        for u_ref, e_ref, s_ref, sc, rev in ((uf_ref, ef_ref, sf_ref, sf_sc, False), (ub_ref, eb_ref, sb_ref, sb_sc, True)):
            s = sc[...]
            for c in _scan_order(rev):
                s_ref[c] = s
                s = _both_halves(e_ref[c]) * s + u_ref[c]
            sc[...] = s

    big = lambda rev: pl.BlockSpec((SCAN_BLOCK, 128, 256), lambda t: (_scan_block(t, nb, rev), 0, 0))
    small = lambda rev: pl.BlockSpec((SCAN_BLOCK, 128, 128), lambda t: (_scan_block(t, nb, rev), 0, 0))
    return pl.pallas_call(
        body, name="gla_states", grid=(nb,),
        in_specs=[big(False), small(False), big(True), small(True)],
        out_specs=[big(False), big(True)],
        out_shape=[jax.ShapeDtypeStruct(uf.shape, F32)] * 2,
        scratch_shapes=[pltpu.VMEM((128, 256), F32)] * 2,
        compiler_params=_params(("arbitrary",)),
    )(uf, ef, ub, eb)


def gla_states_bwd(ef, eb, sf, sb, dsf, dsb):
    nb = ef.shape[0] // SCAN_BLOCK

    def body(ef_ref, eb_ref, sf_ref, sb_ref, dsf_ref, dsb_ref, duf_ref, def_ref, dub_ref, deb_ref, gf_sc, gb_sc):
        @pl.when(pl.program_id(0) == 0)
        def _():
            gf_sc[...] = jnp.zeros_like(gf_sc)
            gb_sc[...] = jnp.zeros_like(gb_sc)

        for e_ref, s_ref, ds_ref, du_ref, de_ref, g_sc, rev in ((ef_ref, sf_ref, dsf_ref, duf_ref, def_ref, gf_sc, False),
                                                                 (eb_ref, sb_ref, dsb_ref, dub_ref, deb_ref, gb_sc, True)):
            g = g_sc[...]
            for k in reversed(_scan_order(rev)):
                du_ref[k] = g
                gs = g * s_ref[k]
                de_ref[k] = gs[:, :128] + gs[:, 128:]
                g = _both_halves(e_ref[k]) * g + ds_ref[k]
            g_sc[...] = g

    big = lambda rev: pl.BlockSpec((SCAN_BLOCK, 128, 256), lambda t: (_scan_block(nb - 1 - t, nb, rev), 0, 0))
    small = lambda rev: pl.BlockSpec((SCAN_BLOCK, 128, 128), lambda t: (_scan_block(nb - 1 - t, nb, rev), 0, 0))
    return pl.pallas_call(
        body, name="gla_states_bwd", grid=(nb,),
        in_specs=[small(False), small(True), big(False), big(True), big(False), big(True)],
        out_specs=[big(False), small(False), big(True), small(True)],
        out_shape=[jax.ShapeDtypeStruct(sf.shape, F32), jax.ShapeDtypeStruct(ef.shape, F32)] * 2,
        scratch_shapes=[pltpu.VMEM((128, 256), F32)] * 2,
        compiler_params=_params(("arbitrary",)),
    )(ef, eb, sf, sb, dsf, dsb)


def loss_head(xt, target, fnw):
    t = xt.shape[0]

    def f(x, tg, w):
        y = _rms(x, w)
        return 0.5 * jnp.sum(jnp.square(y - tg)) * (1.0 / D)

    def body(x_ref, t_ref, w_ref, loss_ref, dx_ref, dw_ref):
        i = pl.program_id(0)

        @pl.when(i == 0)
        def _():
            loss_ref[...] = jnp.zeros_like(loss_ref)
            dw_ref[...] = jnp.zeros_like(dw_ref)

        @pl.when(i < NCTXB)
        def _():
            dx_ref[...] = jnp.zeros_like(dx_ref)

        @pl.when(i >= NCTXB)
        def _():
            val, (dx, dw) = jax.value_and_grad(f, argnums=(0, 2))(x_ref[...], t_ref[...], w_ref[...])
            loss_ref[...] += jnp.broadcast_to(val, loss_ref.shape)
            dx_ref[...] = dx
            dw_ref[...] += dw

    return pl.pallas_call(
        body, name="loss_head", grid=(t // TM,),
        in_specs=[pl.BlockSpec((TM, D), lambda i: (i, 0)), pl.BlockSpec((TM, D), lambda i: (jnp.maximum(i - NCTXB, 0), 0)),
                  pl.BlockSpec((1, D), lambda i: (0, 0))],
        out_specs=[pl.BlockSpec((1, 128), lambda i: (0, 0)), pl.BlockSpec((TM, D), lambda i: (i, 0)),
                   pl.BlockSpec((1, D), lambda i: (0, 0))],
        out_shape=[jax.ShapeDtypeStruct((1, 128), F32), jax.ShapeDtypeStruct((t, D), F32), jax.ShapeDtypeStruct((1, D), F32)],
        compiler_params=_params(("arbitrary",)),
    )(xt, target, fnw)


def _in_to_padded(w):
    out, pos = [], 0
    for src, wd, dst in sorted(IN_GROUPS, key=lambda g: g[2]):
        if dst > pos:
            out.append(jnp.zeros((w.shape[0], dst - pos), w.dtype))
        out.append(w[:, src:src + wd])
        pos = dst + wd
    if pos < P_COLS:
        out.append(jnp.zeros((w.shape[0], P_COLS - pos), w.dtype))
    return jnp.concatenate(out, axis=1)


def _in_from_padded(g):
    return jnp.concatenate([g[:, dst:dst + wd] for _, wd, dst in IN_GROUPS], axis=1)


def _uq_to_padded(w):
    return jnp.pad(w.reshape(256, 4, 192), ((0, 0), (0, 0), (0, 64))).reshape(256, 1024)


def _uq_from_padded(g):
    return g.reshape(256, 4, 256)[:, :, :192].reshape(256, 768)


def _ukv_to_padded(w):
    return w.reshape(256, 4, 2, 128).transpose(0, 2, 1, 3).reshape(256, 1024)


def _ukv_from_padded(g):
    return g.reshape(256, 2, 4, 128).transpose(0, 2, 1, 3).reshape(256, 1024)


def _rope_tables(n):
    freq = ROPE_BASE ** (-jnp.arange(16, dtype=F32) * 2.0 / 32.0)
    grid_h = n // GRID_W
    ar = jnp.repeat(jnp.arange(grid_h, dtype=F32)[:, None] * freq[None, :], GRID_W, axis=0)
    ac = jnp.tile(jnp.arange(GRID_W, dtype=F32)[:, None] * freq[None, :], (grid_h, 1))
    z = jnp.zeros((n, 64), F32)
    cs = jnp.concatenate([jnp.cos(ar), jnp.cos(ar), jnp.cos(ac), jnp.cos(ac), z], axis=1)
    sn = jnp.concatenate([-jnp.sin(ar), jnp.sin(ar), -jnp.sin(ac), jnp.sin(ac), z], axis=1)
    cs_c = jnp.concatenate([jnp.ones((CTX, 64), F32), jnp.zeros((CTX, 64), F32)], axis=1)
    return jnp.concatenate([cs_c, cs], axis=0), jnp.concatenate([jnp.zeros((CTX, 128), F32), sn], axis=0)


def _small_views(sp):
    wg = jnp.concatenate([jnp.pad(sp["gla_wg_fwd"], ((0, 112), (0, 0))), jnp.pad(sp["gla_wg_bwd"], ((16, 96), (0, 0)))], axis=1)
    return dict(
        n1w=sp["norm1_w"][None], n2w=sp["norm2_w"][None],
        sgu_nw=sp["sgu_norm_w"][None], sgu_nb=sp["sgu_norm_b"][None], sgu_w=sp["sgu_w"],
        sgu_bm=jnp.repeat(sp["sgu_b"].T, 64, axis=1),
        wg=wg, bg=jnp.concatenate([sp["gla_bg_fwd"], sp["gla_bg_bwd"]])[None],
        gla_nwt=jnp.tile(sp["gla_norm_w"], 4)[None],
        kvw=sp["mla_kv_norm_w"][None], qw=sp["mla_q_norm_w"][None])


def _small_grads(g):
    return dict(
        norm1_w=g["n1w"][0], norm2_w=g["n2w"][0],
        sgu_norm_w=g["sgu_nw"][0], sgu_norm_b=g["sgu_nb"][0], sgu_w=g["sgu_w"],
        sgu_b=g["sgu_bm"].reshape(128, 4, 64).sum(-1).T,
        gla_wg_fwd=g["wg"][0:16, 0:128], gla_wg_bwd=g["wg"][16:32, 128:256],
        gla_bg_fwd=g["bg"][0, 0:128], gla_bg_bwd=g["bg"][0, 128:256],
        gla_norm_w=g["gla_nwt"].reshape(4, 64).sum(0),
        mla_kv_norm_w=g["kvw"][0], mla_q_norm_w=g["qw"][0])


def _big_views(w_in, w_out, w_uq, w_ukv, w_ff1, w_ff2):
    win, wuq, wukv = _in_to_padded(w_in), _uq_to_padded(w_uq), _ukv_to_padded(w_ukv)
    return dict(win=win, wuq=wuq, wukv=wukv, wout=w_out, w1=w_ff1, w2=w_ff2)


def _gla_tile(t):
    return _pick(t, (768, 512, 256))


def _layer_ops(p, sv, a):
    pc = lambda off, w, tm=TM: rows(p, w, off // w, tm=tm)
    gt = _gla_tile(p.shape[0])
    gr = lambda arr: rows(arr, tm=gt)
    return dict(
        sgu=[pc(P_SU, 256), pc(P_SV, 256), const(sv["sgu_nw"]), const(sv["sgu_nb"]), const(sv["sgu_w"]), const(sv["sgu_bm"])],
        gates=[pc(P_GATE, 128), const(sv["wg"]), const(sv["bg"])],
        mla_pre=[pc(P_CKV, 256), pc(P_DQ, 256), const(sv["kvw"]), const(sv["qw"])],
        gla_kv=lambda: [pc(P_GK, 128, gt), pc(P_GV, 256, gt), gr(a["gf"]), gr(a["gb"])],
        gla_o=lambda: [pc(P_GQ, 128, gt), pc(P_GK, 128, gt), pc(P_GV, 256, gt), gr(a["gf"]), gr(a["gb"]), pc(P_GR, 256, gt),
                       chunks(a["sf"], gt // GLA_CHUNK), chunks(a["sb"], gt // GLA_CHUNK), const(sv["gla_nwt"])],
        mla_post=lambda: [rows(a["kvu"], 512, 0), rows(a["qu"]), pc(P_KR, 128), rows(a["cs"]), rows(a["sn"])])


def layer_fwd(l, xt, modl, bw, sv, tabs, side=()):
    t = xt.shape[0]
    g, nc, gt = t // TM, t // GLA_CHUNK, _gla_tile(t)
    gg, cpt = t // gt, gt // GLA_CHUNK
    nm = lambda s: f"l{l}_{s}"
    a = dict(x=xt, cs=tabs[0], sn=tabs[1])
    a["h"], = rw(nm("norm1"), fn_norm1, [rows(xt), const(modl), const(sv["n1w"])], [rowout(t, D, BF16)], g)
    p = a["p"] = mm(nm("in_proj"), a["h"], bw["win"], F32)
    ops = _layer_ops(p, sv, a)
    y_sgu, = rw(nm("sgu"), fn_sgu, ops["sgu"], [rowout(t, 256, BF16)], g)
    a["gf"], a["gb"] = rw(nm("gates"), fn_gates, ops["gates"], [rowout(t, 128, F32)] * 2, g)
    a["uf"], a["ef"], a["ub"], a["eb"] = rw(nm("gla_kv"), fn_gla_kv, ops["gla_kv"](),
                                           [chunkout((nc, 128, 256), F32, cpt), chunkout((nc, 128, 128), F32, cpt)] * 2, gg)
    a["sf"], a["sb"] = gla_states(a["uf"], a["ef"], a["ub"], a["eb"])
    y_gla, = rw(nm("gla_o"), fn_gla_o, ops["gla_o"](), [rowout(t, 256, BF16, tm=gt)], gg)
    a["ckvn"], a["dqn"] = rw(nm("mla_pre"), fn_mla_pre, ops["mla_pre"], [rowout(t, 256, BF16)] * 2, g)
    a["kvu"] = mm(nm("kv_up"), a["ckvn"], bw["wukv"], BF16)
    a["qu"] = mm(nm("q_up"), a["dqn"], bw["wuq"], F32)
    a["kcat"], a["q"] = rw(nm("mla_post"), fn_mla_post, ops["mla_post"](), [rowout(t, 1024, BF16)] * 2, g)
    a["o"], a["lse"], side_out = flash_fwd(a["q"], a["kcat"], a["kvu"], side)
    a["y"] = jnp.concatenate([y_sgu, y_gla, a["o"].astype(BF16)], axis=1)
    a["yo"] = mm(nm("out_proj"), a["y"], bw["wout"], F32)
    a["x1"], a["h2"] = rw(nm("res_norm2"), fn_res_norm2, [rows(xt), rows(a["yo"]), const(modl), const(sv["n2w"])],
                          [rowout(t, D, F32), rowout(t, D, BF16)], g)
    a["act"] = mm(nm("ff1"), a["h2"], bw["w1"], BF16, post=lambda acc: jnp.maximum(acc, 0.0))
    a["f"] = mm(nm("ff2"), a["act"], bw["w2"], F32, pre=_square_bf16)
    x2, = rw(nm("res2"), fn_res2, [rows(a["x1"]), rows(a["f"]), const(modl)], [rowout(t, D, F32)], g)
    return x2, a, side_out


def fn_assemble(blk, gv1, gv2, ckv, su, sv_, gr, dq, gk1, gk2, pg, kr, gq):
    return (jnp.concatenate([gv1 + gv2, ckv, su, sv_, gr, dq, gk1 + gk2, pg, kr, gq], axis=1),)


def layer_bwd(l, dx2, a, modl, bw, sv):
    t = dx2.shape[0]
    g, gt = t // TM, _gla_tile(t)
    gg, cpt = t // gt, gt // GLA_CHUNK
    nm = lambda s: f"l{l}_{s}_bwd"
    p = a["p"]
    ops = _layer_ops(p, sv, a)
    gw, gs = {}, {}
    df, dm_a = rw_vjp(nm("res2"), fn_res2, [rows(a["x1"]), rows(a["f"]), const(modl)], [rows(dx2)], [1, 2], g,
                      gdt=[BF16, F32])
    gw["w2"] = mm_tn(nm("ff2_w"), a["act"], df, pre=_square_bf16)
    du = mm(nm("ff2_x"), df, bw["w2"], BF16, post=lambda acc, act: acc * (2.0 * act.astype(F32)), extras=(a["act"],), bt=True)
    gw["w1"] = mm_tn(nm("ff1_w"), a["h2"], du)
    dh2 = mm(nm("ff1_x"), du, bw["w1"], F32, bt=True)
    dxa, dyo, dm_b, gs["n2w"] = rw_vjp(nm("res_norm2"), fn_res_norm2,
                                       [rows(a["x"]), rows(a["yo"]), const(modl), const(sv["n2w"])],
                                       [rows(dx2), rows(dh2)], [0, 1, 2, 3], g, gdt=[F32, BF16, F32, F32])
    gw["wout"] = mm_tn(nm("out_w"), a["y"], dyo)
    dy = mm(nm("out_x"), dyo, bw["wout"], F32, bt=True)
    dsu, dsv, gs["sgu_nw"], gs["sgu_nb"], gs["sgu_w"], gs["sgu_bm"] = rw_vjp(
        nm("sgu"), fn_sgu, ops["sgu"], [rows(dy, 256, 0)], [0, 1, 2, 3, 4, 5], g)
    dgq, dgk1, dgv1, dgf1, dgb1, dgr, dsf, dsb, gs["gla_nwt"] = rw_vjp(
        nm("gla_o"), fn_gla_o, ops["gla_o"](), [rows(dy, 256, 1, tm=gt)], list(range(9)), gg)
    duf, def_, dub, deb = gla_states_bwd(a["ef"], a["eb"], a["sf"], a["sb"], dsf, dsb)
    dgk2, dgv2, dgf, dgb = rw_vjp(nm("gla_kv"), fn_gla_kv, ops["gla_kv"](),
                                  [chunks(duf, cpt), chunks(def_, cpt), chunks(dub, cpt), chunks(deb, cpt)], [0, 1, 2, 3], gg,
                                  adds={2: rows(dgf1, tm=gt), 3: rows(dgb1, tm=gt)})
    dpg, gs["wg"], gs["bg"] = rw_vjp(nm("gates"), fn_gates, ops["gates"], [rows(dgf), rows(dgb)], [0, 1, 2], g)
    stats, = rw(nm("attn_stats"), fn_attn_stats, [rows(dy, 512, 1), rows(a["o"]), rows(a["lse"])],
                [chunkout((g, 64, TM), F32, 1)], g)
    dq, dkcat, dv = flash_bwd(a["q"], a["kcat"], a["kvu"], dy, stats)
    dkk, dqu, dkr = rw_vjp(nm("mla_post"), fn_mla_post, ops["mla_post"](), [rows(dkcat), rows(dq)], [0, 1, 2], g,
                           gdt=[BF16, BF16, F32])
    dkvu = jnp.concatenate([dkk, dv.astype(BF16)], axis=1)
    gw["wukv"] = mm_tn(nm("kv_up_w"), a["ckvn"], dkvu)
    gw["wuq"] = mm_tn(nm("q_up_w"), a["dqn"], dqu)
    dckvn = mm(nm("kv_up_x"), dkvu, bw["wukv"], F32, bt=True)
    ddqn = mm(nm("q_up_x"), dqu, bw["wuq"], F32, bt=True)
    dckv, ddq, gs["kvw"], gs["qw"] = rw_vjp(nm("mla_pre"), fn_mla_pre, ops["mla_pre"], [rows(dckvn), rows(ddqn)],
                                            [0, 1, 2, 3], g)
    dp, = rw(nm("assemble"), fn_assemble,
             [rows(x_) for x_ in (dgv1, dgv2, dckv, dsu, dsv, dgr, ddq, dgk1, dgk2, dpg, dkr, dgq)],
             [rowout(t, P_COLS, BF16)], g)
    gw["win"] = mm_tn(nm("in_w"), a["h"], dp)
    dh = mm(nm("in_x"), dp, bw["win"], F32, bt=True)
    dx, dm_c, gs["n1w"] = rw_vjp(nm("norm1"), fn_norm1, [rows(a["x"]), const(modl), const(sv["n1w"])], [rows(dh)],
                                 [0, 1, 2], g, adds={0: rows(dxa)})
    big = dict(w_in=_in_from_padded(gw["win"]), w_out=gw["wout"], mla_w_uq=_uq_from_padded(gw["wuq"]),
               mla_w_ukv=_ukv_from_padded(gw["wukv"]), w_ff1=gw["w1"], w_ff2=gw["w2"])
    return dx, dm_a + dm_b + dm_c, big, _small_grads(gs)


SMALL_NAMES = ("norm1_w", "sgu_norm_w", "sgu_norm_b", "sgu_w", "sgu_b", "gla_wg_fwd", "gla_bg_fwd", "gla_wg_bwd",
               "gla_bg_bwd", "gla_norm_w", "mla_q_norm_w", "mla_kv_norm_w", "norm2_w")
BIG_NAMES = ("w_in", "w_out", "mla_w_uq", "mla_w_ukv", "w_ff1", "w_ff2")


def local_step(x, ctx, target, mods, big, small, final_norm_w, side=(), later_big=None):
    n = x.shape[0]
    xt = jnp.concatenate([ctx, x], axis=0)
    tabs = _rope_tables(n)
    depth = len(mods)
    big = list(big)
    svs = [_small_views(small[l]) for l in range(depth)]
    acts, bws = [], []
    for l in range(depth):
        bws.append(_big_views(*[big[l][k] for k in BIG_NAMES]))
        xt, a, side_out = layer_fwd(l, xt, mods[l], bws[l], svs[l], tabs, side if l == 0 else ())
        if l == 0 and later_big is not None:
            big += later_big(side_out)
        acts.append(a)
    loss, dxt, dfnw = loss_head(xt, target, final_norm_w[None])
    dmods, gbig, gsmall = [None] * depth, [None] * depth, [None] * depth
    for l in reversed(range(depth)):
        dxt, dmods[l], gbig[l], gsmall[l] = layer_bwd(l, dxt, acts[l], mods[l], bws[l], svs[l])
    return loss, dxt[CTX:], dmods, gbig, gsmall, dfnw


def _group(group):
    x, y, c = lax.axis_index("x"), lax.axis_index("y"), lax.axis_index("c")
    if group == "sib":
        return 2, c, [((x, y, 1 - c), 1 - c)]
    if group == "chip":
        flips = [(1, 0), (0, 1), (1, 1)]
        return 4, 2 * x + y, [((x ^ fx, y ^ fy, c), 2 * (x ^ fx) + (y ^ fy)) for fx, fy in flips]
    flips = [(fx, fy, fc) for fx in (0, 1) for fy in (0, 1) for fc in (0, 1)][1:]
    return 8, 4 * x + 2 * y + c, [((x ^ fx, y ^ fy, c ^ fc), 4 * (x ^ fx) + 2 * (y ^ fy) + (c ^ fc)) for fx, fy, fc in flips]


def _group_size(group):
    return {"sib": 2, "chip": 4, "all": 8}[group]


REMOTE_COPIES = {"gather": None, "scatter": None, "swap": 1, "gather2": 6}


def _exchange_shapes(entries):
    n_in = sum(len(arrs) for _, _, arrs in entries)
    n_remote = sum(REMOTE_COPIES[k] or _group_size(g) - 1 for k, g, _ in entries)
    n_local = sum(1 for k, _, _ in entries if k != "swap")
    out_shape = []
    for kind, group, arrs in entries:
        a = arrs[0]
        if kind in ("gather", "gather2"):
            out_shape.append(jax.ShapeDtypeStruct((_group_size(group),) + a.shape, a.dtype))
        else:
            out_shape.append(jax.ShapeDtypeStruct(a.shape, a.dtype))
    return n_in, n_remote, n_local, out_shape


def _exchange_sems(entries):
    _, n_remote, n_local, _ = _exchange_shapes(entries)
    return [pltpu.SemaphoreType.DMA((n_remote,)), pltpu.SemaphoreType.DMA((n_remote,)), pltpu.SemaphoreType.DMA((max(n_local, 1),))]


def _exchange_phases(entries, in_refs, out_refs, send_sems, recv_sems, local_sems):
    x, y, c = lax.axis_index("x"), lax.axis_index("y"), lax.axis_index("c")

    def remote(src, dst, k, dev):
        return pltpu.make_async_remote_copy(src_ref=src, dst_ref=dst, send_sem=send_sems.at[k], recv_sem=recv_sems.at[k],
                                            device_id=dev, device_id_type=MESH)

    pos, k, kl = 0, 0, 0
    starts, forwards, finals = [], [], []
    for (kind, group, arrs), out in zip(entries, out_refs):
        srcs = in_refs[pos:pos + len(arrs)]
        pos += len(arrs)
        _, mine, peers = _group(group)
        if kind == "swap":
            (dev, _), = peers

            def start_swap(srcs=srcs, k=k, dev=dev, out=out):
                for core, src in ((0, srcs[1]), (1, srcs[0])):
                    @pl.when(c == core)
                    def _(src=src):
                        remote(src, out, k, dev).start()

            starts.append(start_swap)
            finals.append(remote(srcs[0], out, k, dev).wait)
            k += 1
            continue
        src = srcs[0]
        own = pltpu.make_async_copy(src if kind != "scatter" else src.at[mine], out.at[mine], local_sems.at[kl])
        starts.append(own.start)
        finals.append(own.wait)
        kl += 1
        if kind == "gather2":
            sibling = (x, y, 1 - c)
            for f, (dev, slot) in enumerate(peers):
                starts.append(remote(src.at[c], out.at[mine, c], k + f, dev).start)
                arrival = remote(src.at[c], out.at[slot, c], k + f, dev)

                def forward(arrival=arrival, slot=slot, kf=k + 3 + f, out=out):
                    arrival.wait_recv()
                    remote(out.at[slot, c], out.at[slot, c], kf, sibling).start()

                forwards.append(forward)
                finals.append(arrival.wait_send)
                finals.append(remote(out.at[slot, c], out.at[slot, 1 - c], k + 3 + f, sibling).wait)
            k += 6
            continue
        for dev, slot in peers:
            piece = src if kind == "gather" else src.at[slot]
            starts.append(remote(piece, out.at[mine], k, dev).start)
            finals.append(remote(piece, out.at[slot], k, dev).wait)
            k += 1
    return starts, forwards, finals


def xchg(name, entries):
    n_in, _, _, out_shape = _exchange_shapes(entries)

    def body(*refs):
        in_refs, out_refs = refs[:n_in], refs[n_in:n_in + len(entries)]
        for phase in _exchange_phases(entries, in_refs, out_refs, *refs[n_in + len(entries):]):
            for run in phase:
                run()

    any_spec = pl.BlockSpec(memory_space=pl.ANY)
    return pl.pallas_call(
        body, name=name,
        in_specs=[any_spec] * n_in, out_specs=[any_spec] * len(entries), out_shape=out_shape,
        scratch_shapes=_exchange_sems(entries),
    )(*[a for _, _, arrs in entries for a in arrs])


def _block_rows(r, c, budget=131072):
    tr = 8
    while tr * 2 * c <= budget and r % (tr * 2) == 0:
        tr *= 2
    return tr if r % tr == 0 else r


def tree_sum(name, parts):
    g, r, c = parts.shape
    tr = _block_rows(r, c)

    def body(p_ref, o_ref):
        p = [p_ref[i].astype(F32) for i in range(g)]
        while len(p) > 1:
            p = [p[i] + p[i + 1] for i in range(0, len(p), 2)]
        o_ref[...] = p[0]

    return pl.pallas_call(
        body, name=name, grid=(r // tr,),
        in_specs=[pl.BlockSpec((g, tr, c), lambda i: (0, i, 0))], out_specs=pl.BlockSpec((tr, c), lambda i: (i, 0)),
        out_shape=jax.ShapeDtypeStruct((r, c), F32), compiler_params=_params(("arbitrary",)),
    )(parts)


def pair_sum(name, g0, g1, recv, core):
    r, c = recv.shape
    tr = _block_rows(r, c)

    def body(a_ref, b_ref, r_ref, k_ref, o_ref):
        o_ref[...] = jnp.where(k_ref[...] > 0.5, b_ref[...], a_ref[...]) + r_ref[...]

    blk = pl.BlockSpec((tr, c), lambda i: (i, 0))
    return pl.pallas_call(
        body, name=name, grid=(r // tr,), in_specs=[blk, blk, blk, pl.BlockSpec((1, 1), lambda i: (0, 0))],
        out_specs=blk, out_shape=jax.ShapeDtypeStruct((r, c), F32), compiler_params=_params(("arbitrary",)),
    )(g0, g1, recv, core)


def adamw(name, w, g, m, v):
    r, c = w.shape
    tr = _block_rows(r, c)

    def body(w_ref, g_ref, m_ref, v_ref, d_ref, nm_ref, nv_ref):
        gg = g_ref[...]
        nm = ADAM_B1 * m_ref[...] + (1.0 - ADAM_B1) * gg
        nv = ADAM_B2 * v_ref[...] + (1.0 - ADAM_B2) * jnp.square(gg)
        m_hat = nm / (1.0 - ADAM_B1 ** ADAM_STEP)
        v_hat = nv / (1.0 - ADAM_B2 ** ADAM_STEP)
        d_ref[...] = -ADAM_LR * (m_hat / (jnp.sqrt(v_hat) + ADAM_EPS) + ADAM_WD * w_ref[...])
        nm_ref[...] = nm
        nv_ref[...] = nv

    blk = pl.BlockSpec((tr, c), lambda i: (i, 0))
    return pl.pallas_call(
        body, name=name, grid=(r // tr,), in_specs=[blk] * 4, out_specs=[blk] * 3,
        out_shape=[jax.ShapeDtypeStruct((r, c), F32)] * 3, compiler_params=_params(("arbitrary",)),
    )(w, g, m, v)


W_MOD_COLS = 6 * D // 4
MOD_TN = 512


def mod_project(c16, w_mod, b_loc):
    def body(c_ref, w_ref, b_ref, o_ref):
        cv = c_ref[...]
        s = (cv * _sigmoid(cv)).astype(BF16)
        o_ref[0] = _dot(s, w_ref[0].astype(BF16)) + b_ref[0]

    return pl.pallas_call(
        body, name="mod_project", grid=(2, W_MOD_COLS // MOD_TN),
        in_specs=[pl.BlockSpec((16, D), lambda l, j: (0, 0)), pl.BlockSpec((1, D, MOD_TN), lambda l, j: (l, 0, j)),
                  pl.BlockSpec((1, 1, MOD_TN), lambda l, j: (l, 0, j))],
        out_specs=pl.BlockSpec((1, 16, MOD_TN), lambda l, j: (l, 0, j)),
        out_shape=jax.ShapeDtypeStruct((2, 16, W_MOD_COLS), F32), compiler_params=_params(("arbitrary", "arbitrary")),
    )(c16, w_mod, b_loc)


def mod_weight_grad(c16, dm16):
    def body(c_ref, d_ref, o_ref):
        cv = c_ref[...]
        o_ref[0] = _dot(cv * _sigmoid(cv), d_ref[0], ((0,), (0,)), precision=HI)

    return pl.pallas_call(
        body, name="mod_weight_grad", grid=(2, W_MOD_COLS // MOD_TN),
        in_specs=[pl.BlockSpec((16, D), lambda l, j: (0, 0)), pl.BlockSpec((1, 16, MOD_TN), lambda l, j: (l, 0, j))],
        out_specs=pl.BlockSpec((1, D, MOD_TN), lambda l, j: (l, 0, j)),
        out_shape=jax.ShapeDtypeStruct((2, D, W_MOD_COLS), F32), compiler_params=_params(("arbitrary", "arbitrary")),
    )(c16, dm16)


def cctx_partial(dmc, w_mod):
    def body(d_ref, w_ref, o_ref):
        @pl.when(pl.program_id(0) == 0)
        def _():
            o_ref[...] = jnp.zeros_like(o_ref)
        o_ref[...] += _dot(d_ref[0], w_ref[0], ((1,), (1,)), precision=HI)

    return pl.pallas_call(
        body, name="cctx_partial", grid=(2,),
        in_specs=[pl.BlockSpec((1, 8, W_MOD_COLS), lambda l: (l, 0, 0)), pl.BlockSpec((1, D, W_MOD_COLS), lambda l: (l, 0, 0))],
        out_specs=pl.BlockSpec((8, D), lambda l: (0, 0)),
        out_shape=jax.ShapeDtypeStruct((8, D), F32), compiler_params=_params(("arbitrary",)),
    )(dmc, w_mod)


def cctx_grad(parts, c_ctx8):
    def body(p_ref, c_ref, o_ref):
        ds = (p_ref[0] + p_ref[1]) + (p_ref[2] + p_ref[3])
        _, vf = jax.vjp(lambda z: z * _sigmoid(z), c_ref[...])
        o_ref[...] = vf(ds)[0]

    return pl.pallas_call(
        body, name="cctx_grad", out_shape=jax.ShapeDtypeStruct((8, D), F32),
    )(parts, c_ctx8)


ARG_NAMES = ("x", "c", "ctx", "c_ctx", "w_mod", "b_mod", "norm1_w", "w_in", "w_out", "sgu_norm_w", "sgu_norm_b", "sgu_w",
             "sgu_b", "gla_wg_fwd", "gla_bg_fwd", "gla_wg_bwd", "gla_bg_bwd", "gla_norm_w", "mla_q_norm_w", "mla_w_uq",
             "mla_kv_norm_w", "mla_w_ukv", "norm2_w", "w_ff1", "w_ff2", "final_norm_w")
WEIGHT_NAMES = ARG_NAMES[3:]
PACKED = ("c_ctx", "b_mod") + SMALL_NAMES + ("final_norm_w",)
ROW_SHARDED = ("w_out", "w_ff2")
PACK_ROWS = 256


def _pack(vectors):
    flat = jnp.concatenate([v.reshape(-1) for v in vectors])
    n = flat.shape[0]
    total = -(-n // (PACK_ROWS * LANES)) * PACK_ROWS * LANES
    return jnp.pad(flat, (0, total - n)).reshape(-1, LANES)


def _unpack(buf, shapes):
    flat, out, pos = buf.reshape(-1), [], 0
    for shp in shapes:
        n = int(np.prod(shp))
        out.append(flat[pos:pos + n].reshape(shp))
        pos += n
    return out


def _full_weight(name, g):
    if name in ROW_SHARDED:
        return g.reshape(-1, g.shape[-1])
    return g.transpose(1, 0, 2).reshape(g.shape[1], -1)


def _chip_chunks(name, a):
    if name in ROW_SHARDED:
        return a.reshape(4, a.shape[0] // 4, a.shape[1])
    return a.reshape(a.shape[0], 4, a.shape[1] // 4).transpose(1, 0, 2)


def kernel(x, c, ctx, c_ctx, w_mod, b_mod, norm1_w, w_in, w_out, sgu_norm_w, sgu_norm_b, sgu_w, sgu_b, gla_wg_fwd, gla_bg_fwd, gla_wg_bwd, gla_bg_bwd, gla_norm_w, mla_q_norm_w, mla_w_uq, mla_kv_norm_w, mla_w_ukv, norm2_w, w_ff1, w_ff2, final_norm_w, loss_target, m_c_ctx, m_w_mod, m_b_mod, m_norm1_w, m_w_in, m_w_out, m_sgu_norm_w, m_sgu_norm_b, m_sgu_w, m_sgu_b, m_gla_wg_fwd, m_gla_bg_fwd, m_gla_wg_bwd, m_gla_bg_bwd, m_gla_norm_w, m_mla_q_norm_w, m_mla_w_uq, m_mla_kv_norm_w, m_mla_w_ukv, m_norm2_w, m_w_ff1, m_w_ff2, m_final_norm_w, v_c_ctx, v_w_mod, v_b_mod, v_norm1_w, v_w_in, v_w_out, v_sgu_norm_w, v_sgu_norm_b, v_sgu_w, v_sgu_b, v_gla_wg_fwd, v_gla_bg_fwd, v_gla_wg_bwd, v_gla_bg_bwd, v_gla_norm_w, v_mla_q_norm_w, v_mla_w_uq, v_mla_kv_norm_w, v_mla_w_ukv, v_norm2_w, v_w_ff1, v_w_ff2, v_final_norm_w):
    args = (x, c, ctx, c_ctx, w_mod, b_mod, norm1_w, w_in, w_out, sgu_norm_w, sgu_norm_b, sgu_w, sgu_b, gla_wg_fwd, gla_bg_fwd, gla_wg_bwd, gla_bg_bwd, gla_norm_w, mla_q_norm_w, mla_w_uq, mla_kv_norm_w, mla_w_ukv, norm2_w, w_ff1, w_ff2, final_norm_w)
    w = dict(zip(ARG_NAMES, args))
    moms = (m_c_ctx, m_w_mod, m_b_mod, m_norm1_w, m_w_in, m_w_out, m_sgu_norm_w, m_sgu_norm_b, m_sgu_w, m_sgu_b, m_gla_wg_fwd, m_gla_bg_fwd, m_gla_wg_bwd, m_gla_bg_bwd, m_gla_norm_w, m_mla_q_norm_w, m_mla_w_uq, m_mla_kv_norm_w, m_mla_w_ukv, m_norm2_w, m_w_ff1, m_w_ff2, m_final_norm_w)
    vars_ = (v_c_ctx, v_w_mod, v_b_mod, v_norm1_w, v_w_in, v_w_out, v_sgu_norm_w, v_sgu_norm_b, v_sgu_w, v_sgu_b, v_gla_wg_fwd, v_gla_bg_fwd, v_gla_wg_bwd, v_gla_bg_bwd, v_gla_norm_w, v_mla_q_norm_w, v_mla_w_uq, v_mla_kv_norm_w, v_mla_w_ukv, v_norm2_w, v_w_ff1, v_w_ff2, v_final_norm_w)
    m1 = dict(zip(WEIGHT_NAMES, moms))
    m2 = dict(zip(WEIGHT_NAMES, vars_))
    xi, yi, ci = lax.axis_index("x"), lax.axis_index("y"), lax.axis_index("c")
    chip, dev = 2 * xi + yi, 4 * xi + 2 * yi + ci
    depth = w_mod.shape[0]

    def shard_halves(l):
        return [("gather2", "chip", [w[k][l].astype(BF16).reshape(2, w[k].shape[1] // 2, w[k].shape[2])]) for k in BIG_NAMES]

    def full_weights(gathered):
        return {k: _full_weight(k, g.reshape(4, *w[k].shape[1:])) for k, g in zip(BIG_NAMES, gathered)}

    got = xchg("gather_inputs", [("gather", "all", [c])] + shard_halves(0))
    c_all = got[0]
    c16 = jnp.concatenate([c_all.reshape(8, D), c_ctx[None], jnp.zeros((7, D), F32)], axis=0)
    b_loc = lax.dynamic_slice_in_dim(b_mod, chip * W_MOD_COLS, W_MOD_COLS, axis=1)[:, None, :]
    mod_part = mod_project(c16, w_mod, b_loc)
    mod_all, = xchg("gather_mod", [("gather", "chip", [mod_part])])
    mod_full = mod_all.transpose(1, 2, 0, 3).reshape(depth, 16, 6 * D)
    mods = [jnp.stack([mod_full[l, 8], lax.dynamic_index_in_dim(mod_full[l], dev, 0, keepdims=False)])[:, None, :]
            for l in range(depth)]

    small = [{k: w[k][l] for k in SMALL_NAMES} for l in range(depth)]
    n_big = len(BIG_NAMES)
    later = [e for l in range(1, depth) for e in shard_halves(l)]
    loss, grad_x, dmods, gbig, gsmall, dfnw = local_step(
        x[0], ctx[0], loss_target[0], mods, [full_weights(got[1:])], small, final_norm_w, side=later,
        later_big=lambda res: [full_weights(res[i * n_big:(i + 1) * n_big]) for i in range(depth - 1)])
    loss = lax.psum(loss[0, 0], ("x", "y", "c"))

    dm_lat = jnp.stack([dmods[l][1, 0] for l in range(depth)])
    dm_ctx = jnp.stack([dmods[l][0, 0] for l in range(depth)])
    small_pack = _pack([dm_lat, dm_ctx] + [jnp.stack([gsmall[l][k] for l in range(depth)]) for k in SMALL_NAMES] + [dfnw])
    got = xchg("exchange_grads", [("gather", "all", [small_pack])] + [("swap", "sib", [gbig[0][k], gbig[1][k]]) for k in BIG_NAMES])
    small_all, from_sib = got[0], dict(zip(BIG_NAMES, got[1:]))
    small_sum = tree_sum("small_grad_sum", small_all)
    core = ci.astype(F32).reshape(1, 1)
    mine = {k: pair_sum(f"pair_sum_{k}", gbig[0][k], gbig[1][k], from_sib[k], core) for k in BIG_NAMES}

    n_dm = depth * 6 * D
    dm_rows = n_dm // LANES
    dm_lat_all = small_all[:, :dm_rows].reshape(8, depth, 6 * D)
    dm_ctx_sum = small_sum[dm_rows:2 * dm_rows].reshape(depth, 6 * D)
    take = lambda a: lax.dynamic_slice_in_dim(a, chip * W_MOD_COLS, W_MOD_COLS, axis=-1)
    dmc_loc = take(dm_ctx_sum)
    cc_part = cctx_partial(jnp.pad(dmc_loc[:, None, :], ((0, 0), (0, 7), (0, 0))), w_mod)
    got = xchg("scatter_grads", [("gather", "chip", [cc_part])]
               + [("scatter", "chip", [_chip_chunks(k, mine[k]).astype(BF16)]) for k in BIG_NAMES])
    cc_parts, chunks_in = got[0], dict(zip(BIG_NAMES, got[1:]))
    reduced = {k: tree_sum(f"chip_sum_{k}", chunks_in[k]) for k in BIG_NAMES}
    g_c_ctx = cctx_grad(cc_parts, jnp.broadcast_to(c_ctx[None], (8, D)))[0]

    got = xchg("share_layers", [("swap", "sib", [reduced[k], reduced[k]]) for k in BIG_NAMES])
    grads = {k: jnp.where(ci == 0, jnp.stack([reduced[k], r]), jnp.stack([r, reduced[k]])) for k, r in zip(BIG_NAMES, got)}

    dm16 = jnp.concatenate([take(dm_lat_all).transpose(1, 0, 2), dmc_loc[:, None, :], jnp.zeros((depth, 7, W_MOD_COLS), F32)], axis=1)
    grads["w_mod"] = mod_weight_grad(c16, dm16)
    flat_sum = small_sum.reshape(-1)
    g_b_mod = (flat_sum[:n_dm] + flat_sum[n_dm:2 * n_dm]).reshape(depth, 6 * D)
    rest_shapes = [w[k].shape for k in PACKED[2:]]
    n_rest = sum(int(np.prod(s)) for s in rest_shapes)
    for k, g in zip(PACKED, [g_c_ctx, g_b_mod] + _unpack(flat_sum[2 * n_dm:2 * n_dm + n_rest], rest_shapes)):
        grads[k] = g

    delta, new_m, new_v = {}, {}, {}
    for k in BIG_NAMES + ("w_mod",):
        view = lambda a: a.reshape(-1, a.shape[-1])
        d_, m_, v_ = adamw(f"adamw_{k}", view(w[k]), view(grads[k]), view(m1[k]), view(m2[k]))
        delta[k], new_m[k], new_v[k] = d_.reshape(w[k].shape), m_.reshape(w[k].shape), v_.reshape(w[k].shape)
    shapes = [w[k].shape for k in PACKED]
    d_, m_, v_ = adamw("adamw_small", _pack([w[k] for k in PACKED]), _pack([grads[k] for k in PACKED]),
                       _pack([m1[k] for k in PACKED]), _pack([m2[k] for k in PACKED]))
    for k, dk, mk, vk in zip(PACKED, _unpack(d_, shapes), _unpack(m_, shapes), _unpack(v_, shapes)):
        delta[k], new_m[k], new_v[k] = dk, mk, vk
    return (loss, grad_x[None], *[grads[k] for k in WEIGHT_NAMES], *[delta[k] for k in WEIGHT_NAMES],
            *[new_m[k] for k in WEIGHT_NAMES], *[new_v[k] for k in WEIGHT_NAMES])
```

```python
import functools
import math

import numpy as np
import jax
import jax.numpy as jnp
from jax import lax
from jax.experimental import pallas as pl
from jax.experimental.pallas import tpu as pltpu

F32 = jnp.float32
BF16 = jnp.bfloat16
HI = lax.Precision.HIGHEST
EPS = 1e-6
VMEM_LIMIT_BYTES = 56 * 1024 * 1024
LANES = 128

D = 1024
D_FF = 4096
CTX = 256
GRID_W = 64
SGU_CHUNK = 128
GLA_CHUNK = 64
GLA_TAU = 16.0
GLA_DK = 32
MLA_SCALE = (128 + 64) ** -0.5
SCORE_SCALE = MLA_SCALE * math.log2(math.e)
LN2 = math.log(2.0)
ROPE_BASE = 10000.0
TM = 256
NCTXB = CTX // TM
P_GV, P_CKV, P_SU, P_SV, P_GR, P_DQ, P_GK, P_GATE, P_KR, P_GQ = 0, 256, 512, 768, 1024, 1280, 1536, 1664, 1792, 1920
P_COLS = 2048
IN_GROUPS = ((0, 128, P_GK), (128, 256, P_GV), (384, 32, P_GATE), (416, 256, P_CKV), (672, 64, P_KR),
             (736, 256, P_SU), (992, 256, P_SV), (1248, 128, P_GQ), (1376, 256, P_GR), (1632, 256, P_DQ))
ADAM_LR, ADAM_B1, ADAM_B2, ADAM_EPS, ADAM_WD, ADAM_STEP = 0.001, 0.9, 0.999, 1e-08, 0.01, 10
MESH = pl.DeviceIdType.MESH


def _params(sem):
    return pltpu.CompilerParams(dimension_semantics=sem, vmem_limit_bytes=VMEM_LIMIT_BYTES)


def _pick(n, cands):
    for c in cands:
        if n % c == 0:
            return c
    return n


class Op:
    def __init__(self, arr, blk, idx, gshape, gidx, acc):
        self.arr, self.blk, self.idx, self.gshape, self.gidx, self.acc = arr, blk, idx, gshape, gidx, acc

    def spec(self):
        return pl.BlockSpec(self.blk, self.idx)


def rows(arr, width=None, cb=0, off=0, tm=TM):
    w = arr.shape[1] if width is None else width
    n = arr.shape[0] - off * tm
    return Op(arr, (tm, w), lambda i: (i + off, cb), (n, w), lambda i: (i, 0), False)


def chunks(arr, per_tile):
    z = (0,) * (arr.ndim - 1)
    return Op(arr, (per_tile,) + arr.shape[1:], lambda i: (i,) + z, arr.shape, lambda i: (i,) + z, False)


def const(arr):
    z = (0,) * arr.ndim
    return Op(arr, arr.shape, lambda i: z, arr.shape, lambda i: z, True)


def rw(name, fn, ins, outs, grid):
    nin = len(ins)

    def body(*refs):
        vals = [r[...] for r in refs[:nin]]
        res = fn(pl.program_id(0), *vals)
        for o, r in zip(refs[nin:], res):
            o[...] = r.astype(o.dtype)

    return pl.pallas_call(
        body, name=name, grid=(grid,),
        in_specs=[o.spec() for o in ins],
        out_specs=[pl.BlockSpec(b, ix) for (_, _, b, ix) in outs],
        out_shape=[jax.ShapeDtypeStruct(s, d) for (s, d, _, _) in outs],
        compiler_params=_params(("arbitrary",)),
    )(*[o.arr for o in ins])


def rowout(n, w, dtype, tm=TM):
    return ((n, w), dtype, (tm, w), lambda i: (i, 0))


def chunkout(shape, dtype, per_tile):
    z = (0,) * (len(shape) - 1)
    return (shape, dtype, (per_tile,) + tuple(shape[1:]), lambda i: (i,) + z)


def rw_vjp(name, fn, ins, cots, wrt, grid, gdt=None, adds=None):
    nin = len(ins)
    cot_ops = [c for c in cots if c is not None]
    add_items = sorted((adds or {}).items())
    gdt = gdt or [F32] * len(wrt)
    ncot, nadd = len(cot_ops), len(add_items)

    def body(*refs):
        i = pl.program_id(0)
        vals = [r[...] for r in refs[:nin]]
        cvals = [r[...] for r in refs[nin:nin + ncot]]
        avals = [r[...] for r in refs[nin + ncot:nin + ncot + nadd]]
        grefs = refs[nin + ncot + nadd:]

        def f(*d):
            a = list(vals)
            for k, dv in zip(wrt, d):
                a[k] = dv
            return tuple(fn(i, *a))

        outs, vf = jax.vjp(f, *[vals[k] for k in wrt])
        it = iter(cvals)
        ct = tuple(jnp.zeros_like(o) if c is None else next(it).astype(o.dtype) for c, o in zip(cots, outs))
        gs = list(vf(ct))
        for (pos, _), av in zip(add_items, avals):
            gs[pos] = gs[pos].astype(F32) + av.astype(F32)
        for pos, (k, g, gref) in enumerate(zip(wrt, gs, grefs)):
            if ins[k].acc:
                @pl.when(i == 0)
                def _():
                    gref[...] = jnp.zeros_like(gref)
                gref[...] += g.astype(gref.dtype)
            else:
                gref[...] = g.astype(gref.dtype)

    all_in = list(ins) + cot_ops + [op for _, op in add_items]
    return pl.pallas_call(
        body, name=name, grid=(grid,),
        in_specs=[o.spec() for o in all_in],
        out_specs=[pl.BlockSpec(ins[k].blk, ins[k].gidx) for k in wrt],
        out_shape=[jax.ShapeDtypeStruct(ins[k].gshape, dt) for k, dt in zip(wrt, gdt)],
        compiler_params=_params(("arbitrary",)),
    )(*[o.arr for o in all_in])


MM_VMEM_BUDGET = 40 * 1024 * 1024
MM_COLS = 1024


def _square_bf16(a):
    a = a.astype(F32)
    return (a * a).astype(BF16)


def mm(name, a, b, out_dtype, pre=None, post=None, extras=(), bt=False):
    m, k = a.shape
    n = b.shape[0] if bt else b.shape[1]
    nc = min(n, MM_COLS)
    row_bytes = k * a.dtype.itemsize + n * jnp.dtype(out_dtype).itemsize + sum(n * e.dtype.itemsize for e in extras)
    tm = next(t for t in (768, 512, 384, 256, 128, 64)
              if m % t == 0 and 2 * t * row_bytes + 2 * k * n * b.dtype.itemsize + t * nc * 4 <= MM_VMEM_BUDGET)

    def body(a_ref, b_ref, *rest):
        o_ref = rest[-1]
        av = a_ref[...]
        if pre is not None:
            av = pre(av)
        for j in range(n // nc):
            cs = slice(j * nc, (j + 1) * nc)
            if bt:
                acc = lax.dot_general(av, b_ref[cs, :], (((1,), (1,)), ((), ())), preferred_element_type=F32)
            else:
                acc = lax.dot_general(av, b_ref[:, cs], (((1,), (0,)), ((), ())), preferred_element_type=F32)
            if post is not None:
                acc = post(acc, *[e[:, cs] for e in rest[:-1]])
            o_ref[:, cs] = acc.astype(o_ref.dtype)

    row = lambda w: pl.BlockSpec((tm, w), lambda i: (i, 0))
    return pl.pallas_call(
        body, name=name, grid=(m // tm,),
        in_specs=[row(k), pl.BlockSpec(b.shape, lambda i: (0, 0))] + [row(n) for _ in extras],
        out_specs=row(n),
        out_shape=jax.ShapeDtypeStruct((m, n), out_dtype),
        compiler_params=_params(("arbitrary",)),
    )(a, b, *extras)


def mm_tn(name, a, b, pre=None, side=None):
    m, ka = a.shape
    _, nb = b.shape
    tm = _pick(m, (768, 512, 256))
    ta = _pick(ka, (2048, 1024, 512, 256, 128))
    tb = _pick(nb, tuple(t for t in (4096, 2048, 1024, 512, 256, 128) if ta * t * 4 <= 8 * 1024 * 1024))
    sd = side or Side(())

    def body(a_ref, b_ref, *rest):
        o_ref = rest[sd.n_in]
        finish = sd.start(rest[:sd.n_in], rest[sd.n_in + 1:sd.n_in + 1 + sd.n_out], rest[sd.n_in + 1 + sd.n_out:])

        @pl.when(pl.program_id(2) == 0)
        def _():
            o_ref[...] = jnp.zeros_like(o_ref)
        av = a_ref[...] if pre is None else pre(a_ref[...])
        o_ref[...] += lax.dot_general(av, b_ref[...], (((0,), (0,)), ((), ())), preferred_element_type=F32)
        finish()

    res = pl.pallas_call(
        body, name=name, grid=(ka // ta, nb // tb, m // tm),
        in_specs=[pl.BlockSpec((tm, ta), lambda i, j, k: (k, i)), pl.BlockSpec((tm, tb), lambda i, j, k: (k, j))] + sd.in_specs,
        out_specs=[pl.BlockSpec((ta, tb), lambda i, j, k: (i, j))] + sd.out_specs,
        out_shape=[jax.ShapeDtypeStruct((ka, nb), F32)] + sd.shapes,
        scratch_shapes=sd.sems,
        compiler_params=_params(("arbitrary", "arbitrary", "arbitrary")),
    )(a, b, *sd.arrays)
    return res[0] if side is None else (res[0], list(res[1:]))


def _rms(x, w):
    return x * lax.rsqrt(jnp.mean(x * x, axis=-1, keepdims=True) + EPS) * w


def _mod_of(blk, m):
    return jnp.where(blk < NCTXB, m[0], m[1])


def _gelu(x):
    return x * (0.5 * (1.0 + jnp.tanh(math.sqrt(2.0 / math.pi) * (x + 0.044715 * (x * x * x)))))


def _sigmoid(x):
    return 1.0 / (1.0 + jnp.exp(-x))


def _log_sigmoid(z):
    return jnp.minimum(z, 0.0) - jnp.log(1.0 + jnp.exp(-jnp.abs(z)))


def _dot(a, b, dims=((1,), (0,)), precision=None):
    return lax.dot_general(a, b, (dims, ((), ())), precision=precision, preferred_element_type=F32)


def _lane_group_mask(width, group, h):
    lane = lax.broadcasted_iota(jnp.int32, (1, width), 1)
    return (lane >= h * group) & (lane < (h + 1) * group)


def fn_norm1(blk, x, m, nw):
    mv = _mod_of(blk, m)
    return ((_rms(x, nw) * (1.0 + mv[:, D:2 * D]) + mv[:, 0:D]),)


def fn_res_norm2(blk, x, yo, m, nw):
    mv = _mod_of(blk, m)
    x1 = x + mv[:, 2 * D:3 * D] * yo
    return x1, _rms(x1, nw) * (1.0 + mv[:, 4 * D:5 * D]) + mv[:, 3 * D:4 * D]


def fn_res2(blk, x1, f, m):
    mv = _mod_of(blk, m)
    return (x1 + mv[:, 5 * D:6 * D] * f,)


def fn_sgu(blk, su, sv, nw, nb, ws, bm):
    u = _gelu(su)
    g = _gelu(sv)
    mu = jnp.mean(g, axis=-1, keepdims=True)
    var = jnp.mean(jnp.square(g - mu), axis=-1, keepdims=True)
    v = (g - mu) * lax.rsqrt(var + EPS) * nw + nb
    out = []
    for c in range(su.shape[0] // SGU_CHUNK):
        vc = v[c * SGU_CHUNK:(c + 1) * SGU_CHUNK]
        s = bm
        for h in range(4):
            vh = jnp.where(_lane_group_mask(256, 64, h), vc, 0.0)
            s = s + _dot(ws[h].astype(BF16), vh.astype(BF16))
        out.append(u[c * SGU_CHUNK:(c + 1) * SGU_CHUNK] * s)
    return (jnp.concatenate(out, axis=0),)


def fn_gates(blk, pg, wg, bg):
    z = _dot(pg.astype(BF16), wg.astype(BF16)) + bg
    g = _log_sigmoid(z) * (1.0 / GLA_TAU)
    return g[:, :128], g[:, 128:]


def _scan_rows(x, rev):
    n = x.shape[0]
    row = lax.broadcasted_iota(jnp.int32, x.shape, 0)
    d = 1
    while d < n:
        if rev:
            x = x + jnp.where(row < n - d, pltpu.roll(x, n - d, 0), 0.0)
        else:
            x = x + jnp.where(row >= d, pltpu.roll(x, d, 0), 0.0)
        d *= 2
    return x


@functools.partial(jax.custom_vjp, nondiff_argnums=(1,))
def _cumsum_rows(x, rev):
    return _scan_rows(x, rev)


def _cumsum_rows_fwd(x, rev):
    return _scan_rows(x, rev), None


def _cumsum_rows_bwd(rev, _, dy):
    return (_scan_rows(dy, not rev),)


_cumsum_rows.defvjp(_cumsum_rows_fwd, _cumsum_rows_bwd)


def _gla_chunk_terms(g, rev):
    return _cumsum_rows(g, rev), jnp.sum(g, axis=0, keepdims=True)


def _bd_mask():
    r = lax.broadcasted_iota(jnp.int32, (128, 256), 0)
    c = lax.broadcasted_iota(jnp.int32, (128, 256), 1)
    return (r // GLA_DK) == (c // 64)


def _gla_kv_chunk(k, v, g, rev):
    b, tot = _gla_chunk_terms(g, rev)
    kd = k * jnp.exp(tot - b)
    u = jnp.where(_bd_mask(), _dot(kd.astype(BF16), v.astype(BF16), ((0,), (0,))), 0.0)
    r = lax.broadcasted_iota(jnp.int32, (128, 128), 0)
    c = lax.broadcasted_iota(jnp.int32, (128, 128), 1)
    col = jnp.sum(jnp.where(r == c, jnp.broadcast_to(jnp.exp(tot), (128, 128)), 0.0), axis=1, keepdims=True)
    return u, jnp.broadcast_to(col, (128, 128))


def _gla_o_chunk(q, k, v, g, s, rev):
    b, _ = _gla_chunk_terms(g, rev)
    qe = q * jnp.exp(b) * (GLA_DK ** -0.5)
    ke = k * jnp.exp(-b)
    o = _dot(qe.astype(BF16), jnp.where(_bd_mask(), s, 0.0).astype(BF16))
    qs = jnp.concatenate([jnp.where(_lane_group_mask(128, GLA_DK, h), qe, 0.0) for h in range(4)], axis=0)
    a = _dot(qs.astype(BF16), ke.astype(BF16), ((1,), (1,)))
    i = lax.broadcasted_iota(jnp.int32, a.shape, 0) % GLA_CHUNK
    j = lax.broadcasted_iota(jnp.int32, a.shape, 1)
    a = jnp.where((j >= i) if rev else (j <= i), a, 0.0)
    av = _dot(a.astype(BF16), v.astype(BF16))
    for h in range(4):
        o = o + jnp.where(_lane_group_mask(256, 64, h), av[GLA_CHUNK * h:GLA_CHUNK * (h + 1)], 0.0)
    return o


def fn_gla_kv(blk, k, v, gf, gb):
    uf, ef, ub, eb = [], [], [], []
    for c in range(k.shape[0] // GLA_CHUNK):
        sl = slice(c * GLA_CHUNK, (c + 1) * GLA_CHUNK)
        u, e = _gla_kv_chunk(k[sl], v[sl], gf[sl], False)
        uf.append(u[None]); ef.append(e[None])
        u, e = _gla_kv_chunk(k[sl], v[sl], gb[sl], True)
        ub.append(u[None]); eb.append(e[None])
    cat = lambda t: jnp.concatenate(t, axis=0)
    return cat(uf), cat(ef), cat(ub), cat(eb)


def fn_gla_o(blk, q, k, v, gf, gb, gr, sf, sb, nwt):
    out = []
    for c in range(q.shape[0] // GLA_CHUNK):
        sl = slice(c * GLA_CHUNK, (c + 1) * GLA_CHUNK)
        out.append(_gla_o_chunk(q[sl], k[sl], v[sl], gf[sl], sf[c], False)
                   + _gla_o_chunk(q[sl], k[sl], v[sl], gb[sl], sb[c], True))
    o = jnp.concatenate(out, axis=0)
    r = lax.broadcasted_iota(jnp.int32, (256, 256), 0)
    c = lax.broadcasted_iota(jnp.int32, (256, 256), 1)
    head_mean = jnp.where((r // 64) == (c // 64), 1.0 / 64.0, 0.0).astype(F32)
    ms = _dot(o * o, head_mean, precision=HI)
    on = o * lax.rsqrt(ms + EPS) * nwt
    return (on * (gr * _sigmoid(gr)),)


def _rope_partner(x):
    lane = lax.broadcasted_iota(jnp.int32, x.shape, 1)
    return jnp.where((lane // 16) % 2 == 0, pltpu.roll(x, LANES - 16, 1), pltpu.roll(x, 16, 1))


@jax.custom_vjp
def _rope(x, cs, sn):
    return x * cs + _rope_partner(x) * sn


def _rope_fwd(x, cs, sn):
    return _rope(x, cs, sn), (cs, sn)


def _rope_bwd(res, dy):
    cs, sn = res
    return dy * cs + _rope_partner(dy * sn), jnp.zeros_like(cs), jnp.zeros_like(sn)


_rope.defvjp(_rope_fwd, _rope_bwd)


def fn_mla_pre(blk, ckv, dq, kvw, qw):
    return _rms(ckv, kvw), _rms(dq, qw)


def fn_mla_post(blk, kk, qu, kr, cs, sn):
    kro = _rope(kr, cs, sn)
    kcat, q = [], []
    for h in range(4):
        kcat += [kk[:, 128 * h:128 * (h + 1)].astype(F32), kro]
        q += [qu[:, 256 * h:256 * h + 128], _rope(qu[:, 256 * h + 128:256 * (h + 1)], cs, sn)]
    return jnp.concatenate(kcat, axis=1), jnp.concatenate(q, axis=1) * SCORE_SCALE


ATTN_ROWS = 256
NEG = -1e30


def _scores(q, k, k0, context_queries):
    s = _dot(q, k, ((1,), (1,)))
    if context_queries is not None:
        col = k0 + lax.broadcasted_iota(jnp.int32, s.shape, 1)
        s = jnp.where(context_queries & (col >= CTX), NEG, s)
    return s


def flash_fwd(q, kcat, kvu, side=()):
    t = q.shape[0]
    tq = _pick(t, (768, 512, 256))
    tk = _pick(t, (2816, 1536, 768, 512, 256))
    nsub = tq // ATTN_ROWS
    n_side_in, _, _, side_shapes = _exchange_shapes(side)
    n_side = len(side)

    def body(q_ref, k_ref, v_ref, *rest):
        side_in, rest = rest[:n_side_in], rest[n_side_in:]
        o_ref, lse_ref = rest[:2]
        side_out, (m_sc, l_sc, acc_sc), side_sems = rest[2:2 + n_side], rest[2 + n_side:5 + n_side], rest[5 + n_side:]
        h, qi, ki = pl.program_id(0), pl.program_id(1), pl.program_id(2)
        if side:
            starts, forwards, finals = _exchange_phases(side, side_in, side_out, *side_sems)
            at_tile0 = (qi == 0) & (ki == 0)
            last = (h == pl.num_programs(0) - 1) & (qi == pl.num_programs(1) - 1) & (ki == pl.num_programs(2) - 1)
            for when, phase in (((h == 0) & at_tile0, starts), ((h == 2) & at_tile0, forwards)):
                @pl.when(when)
                def _(phase=phase):
                    for run in phase:
                        run()

        @pl.when(ki == 0)
        def _():
            m_sc[...] = jnp.full_like(m_sc, NEG)
            l_sc[...] = jnp.zeros_like(l_sc)
            acc_sc[...] = jnp.zeros_like(acc_sc)

        k, v = k_ref[...], v_ref[...]
        chains = [pl.ds(r * ATTN_ROWS, ATTN_ROWS) for r in range(nsub)]
        scores = [_scores(q_ref[rs, :], k, ki * tk, (qi == 0) if r == 0 else None) for r, rs in enumerate(chains)]
        probs = []
        for rs, s in zip(chains, scores):
            m_old = m_sc[rs, :]
            m_new = jnp.maximum(m_old, jnp.max(s, axis=-1, keepdims=True))
            alpha = jnp.exp2(m_old - m_new)
            p = jnp.exp2(s - m_new)
            l_sc[rs, :] = alpha * l_sc[rs, :] + jnp.sum(p, axis=-1, keepdims=True)
            m_sc[rs, :] = m_new
            probs.append((alpha, p.astype(BF16)))
        for rs, (alpha, p) in zip(chains, probs):
            acc_sc[rs, :] = alpha * acc_sc[rs, :] + _dot(p, v)

        @pl.when(ki == pl.num_programs(2) - 1)
        def _():
            o_ref[...] = acc_sc[...] / l_sc[...]
            lse_ref[...] = jnp.broadcast_to(m_sc[...] + jnp.log2(l_sc[...]), lse_ref.shape)

        if side:
            @pl.when(last)
            def _():
                for run in finals:
                    run()

    any_spec = pl.BlockSpec(memory_space=pl.ANY)
    res = pl.pallas_call(
        body, name="mla_flash_fwd", grid=(4, t // tq, t // tk),
        in_specs=[pl.BlockSpec((tq, 256), lambda h, i, j: (i, h)), pl.BlockSpec((tk, 256), lambda h, i, j: (j, h)),
                  pl.BlockSpec((tk, 128), lambda h, i, j: (j, 4 + h))] + [any_spec] * n_side_in,
        out_specs=[pl.BlockSpec((tq, 128), lambda h, i, j: (i, h)), pl.BlockSpec((tq, 128), lambda h, i, j: (i, h))]
        + [any_spec] * n_side,
        out_shape=[jax.ShapeDtypeStruct((t, 512), F32), jax.ShapeDtypeStruct((t, 512), F32)] + side_shapes,
        scratch_shapes=[pltpu.VMEM((tq, 1), F32), pltpu.VMEM((tq, 1), F32), pltpu.VMEM((tq, 128), F32)]
        + (_exchange_sems(side) if side else []),
        compiler_params=_params(("arbitrary", "arbitrary", "arbitrary")),
    )(q, kcat, kvu, *[a for _, _, arrs in side for a in arrs])
    return res[0], res[1], list(res[2:])


def fn_attn_stats(blk, do, o, lse):
    out = []
    for h in range(4):
        hs = slice(128 * h, 128 * (h + 1))
        d = jnp.sum(do[:, hs] * o[:, hs], axis=-1, keepdims=True)
        out.append(lse[:, hs].T[0:8])
        out.append(jnp.broadcast_to(d, (do.shape[0], 128)).T[0:8])
    return (jnp.concatenate(out, axis=0)[None],)


def flash_bwd(q, kcat, kvu, dy, stats, side=None):
    t = q.shape[0]
    tq = _pick(t, (2816, 768, 512, 256))
    tk = _pick(t, (768, 512, 256))
    nst = tq // TM
    sd = side or Side(())

    def body(q_ref, k_ref, v_ref, do_ref, st_ref, *rest):
        dq_ref, dk_ref, dv_ref = rest[sd.n_in:sd.n_in + 3]
        finish = sd.start(rest[:sd.n_in], rest[sd.n_in + 3:sd.n_in + 3 + sd.n_out], rest[sd.n_in + 3 + sd.n_out:])
        kj, qi = pl.program_id(1), pl.program_id(2)

        @pl.when(qi == 0)
        def _():
            dk_ref[...] = jnp.zeros_like(dk_ref)
            dv_ref[...] = jnp.zeros_like(dv_ref)

        q_, k, v, do = q_ref[...], k_ref[...], v_ref[...], do_ref[...].astype(BF16)
        lse_row = jnp.concatenate([st_ref[u, 0:1, :] for u in range(nst)], axis=1)
        delta_row = jnp.concatenate([st_ref[u, 8:9, :] for u in range(nst)], axis=1)
        s = _dot(k, q_, ((1,), (1,)))
        key = kj * tk + lax.broadcasted_iota(jnp.int32, s.shape, 0)
        qry = qi * tq + lax.broadcasted_iota(jnp.int32, s.shape, 1)
        s = jnp.where((qry < CTX) & (key >= CTX), NEG, s)
        p = jnp.exp2(s - lse_row)
        dp = _dot(v, do, ((1,), (1,)))
        ds = (p * (dp - delta_row)).astype(BF16)
        dv_ref[...] += _dot(p.astype(BF16), do)
        dk_ref[...] += LN2 * _dot(ds, q_)
        dq_new = LN2 * _dot(ds, k, ((0,), (0,)))
        rows_ = pl.ds(pl.multiple_of(qi * tq, TM), tq)

        @pl.when(kj == 0)
        def _():
            dq_ref[rows_, :] = dq_new

        @pl.when(kj != 0)
        def _():
            dq_ref[rows_, :] += dq_new

        finish()

    res = pl.pallas_call(
        body, name="mla_flash_bwd", grid=(4, t // tk, t // tq),
        in_specs=[pl.BlockSpec((tq, 256), lambda h, j, i: (i, h)), pl.BlockSpec((tk, 256), lambda h, j, i: (j, h)),
                  pl.BlockSpec((tk, 128), lambda h, j, i: (j, 4 + h)), pl.BlockSpec((tq, 128), lambda h, j, i: (i, 4 + h)),
                  pl.BlockSpec((nst, 16, 256), lambda h, j, i: (i, h, 0))] + sd.in_specs,
        out_specs=[pl.BlockSpec((t, 256), lambda h, j, i: (0, h)), pl.BlockSpec((tk, 256), lambda h, j, i: (j, h)),
                   pl.BlockSpec((tk, 128), lambda h, j, i: (j, h))] + sd.out_specs,
        out_shape=[jax.ShapeDtypeStruct((t, 1024), F32), jax.ShapeDtypeStruct((t, 1024), F32),
                   jax.ShapeDtypeStruct((t, 512), F32)] + sd.shapes,
        scratch_shapes=sd.sems,
        compiler_params=_params(("arbitrary", "arbitrary", "arbitrary")),
    )(q, kcat, kvu, dy, stats, *sd.arrays)
    return res[0], res[1], res[2], list(res[3:])


SCAN_BLOCK = CTX // GLA_CHUNK


def _scan_block(t, nb, rev):
    if not rev:
        return t
    return jnp.where(t < 1, 0, nb - t)


def _scan_order(rev):
    return tuple(reversed(range(SCAN_BLOCK))) if rev else tuple(range(SCAN_BLOCK))


def _both_halves(e):
    return jnp.concatenate([e, e], axis=1)


def gla_states(uf, ef, ub, eb):
    nb = uf.shape[0] // SCAN_BLOCK

    def body(uf_ref, ef_ref, ub_ref, eb_ref, sf_ref, sb_ref, sf_sc, sb_sc):
        @pl.when(pl.program_id(0) == 0)
        def _():
            sf_sc[...] = jnp.zeros_like(sf_sc)
            sb_sc[...] = jnp.zeros_like(sb_sc)

        for u_ref, e_ref, s_ref, sc, rev in ((uf_ref, ef_ref, sf_ref, sf_sc, False), (ub_ref, eb_ref, sb_ref, sb_sc, True)):
            s = sc[...]
            for c in _scan_order(rev):
                s_ref[c] = s
                s = _both_halves(e_ref[c]) * s + u_ref[c]
            sc[...] = s

    big = lambda rev: pl.BlockSpec((SCAN_BLOCK, 128, 256), lambda t: (_scan_block(t, nb, rev), 0, 0))
    small = lambda rev: pl.BlockSpec((SCAN_BLOCK, 128, 128), lambda t: (_scan_block(t, nb, rev), 0, 0))
    return pl.pallas_call(
        body, name="gla_states", grid=(nb,),
        in_specs=[big(False), small(False), big(True), small(True)],
        out_specs=[big(False), big(True)],
        out_shape=[jax.ShapeDtypeStruct(uf.shape, F32)] * 2,
        scratch_shapes=[pltpu.VMEM((128, 256), F32)] * 2,
        compiler_params=_params(("arbitrary",)),
    )(uf, ef, ub, eb)


def gla_states_bwd(ef, eb, sf, sb, dsf, dsb):
    nb = ef.shape[0] // SCAN_BLOCK

    def body(ef_ref, eb_ref, sf_ref, sb_ref, dsf_ref, dsb_ref, duf_ref, def_ref, dub_ref, deb_ref, gf_sc, gb_sc):
        @pl.when(pl.program_id(0) == 0)
        def _():
            gf_sc[...] = jnp.zeros_like(gf_sc)
            gb_sc[...] = jnp.zeros_like(gb_sc)

        for e_ref, s_ref, ds_ref, du_ref, de_ref, g_sc, rev in ((ef_ref, sf_ref, dsf_ref, duf_ref, def_ref, gf_sc, False),
                                                                 (eb_ref, sb_ref, dsb_ref, dub_ref, deb_ref, gb_sc, True)):
            g = g_sc[...]
            for k in reversed(_scan_order(rev)):
                du_ref[k] = g
                gs = g * s_ref[k]
                de_ref[k] = gs[:, :128] + gs[:, 128:]
                g = _both_halves(e_ref[k]) * g + ds_ref[k]
            g_sc[...] = g

    big = lambda rev: pl.BlockSpec((SCAN_BLOCK, 128, 256), lambda t: (_scan_block(nb - 1 - t, nb, rev), 0, 0))
    small = lambda rev: pl.BlockSpec((SCAN_BLOCK, 128, 128), lambda t: (_scan_block(nb - 1 - t, nb, rev), 0, 0))
    return pl.pallas_call(
        body, name="gla_states_bwd", grid=(nb,),
        in_specs=[small(False), small(True), big(False), big(True), big(False), big(True)],
        out_specs=[big(False), small(False), big(True), small(True)],
        out_shape=[jax.ShapeDtypeStruct(sf.shape, F32), jax.ShapeDtypeStruct(ef.shape, F32)] * 2,
        scratch_shapes=[pltpu.VMEM((128, 256), F32)] * 2,
        compiler_params=_params(("arbitrary",)),
    )(ef, eb, sf, sb, dsf, dsb)


def loss_head(xt, target, fnw):
    t = xt.shape[0]

    def f(x, tg, w):
        y = _rms(x, w)
        return 0.5 * jnp.sum(jnp.square(y - tg)) * (1.0 / D)

    def body(x_ref, t_ref, w_ref, loss_ref, dx_ref, dw_ref):
        i = pl.program_id(0)

        @pl.when(i == 0)
        def _():
            loss_ref[...] = jnp.zeros_like(loss_ref)
            dw_ref[...] = jnp.zeros_like(dw_ref)

        @pl.when(i < NCTXB)
        def _():
            dx_ref[...] = jnp.zeros_like(dx_ref)

        @pl.when(i >= NCTXB)
        def _():
            val, (dx, dw) = jax.value_and_grad(f, argnums=(0, 2))(x_ref[...], t_ref[...], w_ref[...])
            loss_ref[...] += jnp.broadcast_to(val, loss_ref.shape)
            dx_ref[...] = dx
            dw_ref[...] += dw

    return pl.pallas_call(
        body, name="loss_head", grid=(t // TM,),
        in_specs=[pl.BlockSpec((TM, D), lambda i: (i, 0)), pl.BlockSpec((TM, D), lambda i: (jnp.maximum(i - NCTXB, 0), 0)),
                  pl.BlockSpec((1, D), lambda i: (0, 0))],
        out_specs=[pl.BlockSpec((1, 128), lambda i: (0, 0)), pl.BlockSpec((TM, D), lambda i: (i, 0)),
                   pl.BlockSpec((1, D), lambda i: (0, 0))],
        out_shape=[jax.ShapeDtypeStruct((1, 128), F32), jax.ShapeDtypeStruct((t, D), F32), jax.ShapeDtypeStruct((1, D), F32)],
        compiler_params=_params(("arbitrary",)),
    )(xt, target, fnw)


def _in_to_padded(w):
    out, pos = [], 0
    for src, wd, dst in sorted(IN_GROUPS, key=lambda g: g[2]):
        if dst > pos:
            out.append(jnp.zeros((w.shape[0], dst - pos), w.dtype))
        out.append(w[:, src:src + wd])
        pos = dst + wd
    if pos < P_COLS:
        out.append(jnp.zeros((w.shape[0], P_COLS - pos), w.dtype))
    return jnp.concatenate(out, axis=1)


def _in_from_padded(g):
    return jnp.concatenate([g[:, dst:dst + wd] for _, wd, dst in IN_GROUPS], axis=1)


def _uq_to_padded(w):
    return jnp.pad(w.reshape(256, 4, 192), ((0, 0), (0, 0), (0, 64))).reshape(256, 1024)


def _uq_from_padded(g):
    return g.reshape(256, 4, 256)[:, :, :192].reshape(256, 768)


def _ukv_to_padded(w):
    return w.reshape(256, 4, 2, 128).transpose(0, 2, 1, 3).reshape(256, 1024)


def _ukv_from_padded(g):
    return g.reshape(256, 2, 4, 128).transpose(0, 2, 1, 3).reshape(256, 1024)


def _rope_tables(n):
    freq = ROPE_BASE ** (-jnp.arange(16, dtype=F32) * 2.0 / 32.0)
    grid_h = n // GRID_W
    ar = jnp.repeat(jnp.arange(grid_h, dtype=F32)[:, None] * freq[None, :], GRID_W, axis=0)
    ac = jnp.tile(jnp.arange(GRID_W, dtype=F32)[:, None] * freq[None, :], (grid_h, 1))
    z = jnp.zeros((n, 64), F32)
    cs = jnp.concatenate([jnp.cos(ar), jnp.cos(ar), jnp.cos(ac), jnp.cos(ac), z], axis=1)
    sn = jnp.concatenate([-jnp.sin(ar), jnp.sin(ar), -jnp.sin(ac), jnp.sin(ac), z], axis=1)
    cs_c = jnp.concatenate([jnp.ones((CTX, 64), F32), jnp.zeros((CTX, 64), F32)], axis=1)
    return jnp.concatenate([cs_c, cs], axis=0), jnp.concatenate([jnp.zeros((CTX, 128), F32), sn], axis=0)


def _small_views(sp):
    wg = jnp.concatenate([jnp.pad(sp["gla_wg_fwd"], ((0, 112), (0, 0))), jnp.pad(sp["gla_wg_bwd"], ((16, 96), (0, 0)))], axis=1)
    return dict(
        n1w=sp["norm1_w"][None], n2w=sp["norm2_w"][None],
        sgu_nw=sp["sgu_norm_w"][None], sgu_nb=sp["sgu_norm_b"][None], sgu_w=sp["sgu_w"],
        sgu_bm=jnp.repeat(sp["sgu_b"].T, 64, axis=1),
        wg=wg, bg=jnp.concatenate([sp["gla_bg_fwd"], sp["gla_bg_bwd"]])[None],
        gla_nwt=jnp.tile(sp["gla_norm_w"], 4)[None],
        kvw=sp["mla_kv_norm_w"][None], qw=sp["mla_q_norm_w"][None])


def _small_grads(g):
    return dict(
        norm1_w=g["n1w"][0], norm2_w=g["n2w"][0],
        sgu_norm_w=g["sgu_nw"][0], sgu_norm_b=g["sgu_nb"][0], sgu_w=g["sgu_w"],
        sgu_b=g["sgu_bm"].reshape(128, 4, 64).sum(-1).T,
        gla_wg_fwd=g["wg"][0:16, 0:128], gla_wg_bwd=g["wg"][16:32, 128:256],
        gla_bg_fwd=g["bg"][0, 0:128], gla_bg_bwd=g["bg"][0, 128:256],
        gla_norm_w=g["gla_nwt"].reshape(4, 64).sum(0),
        mla_kv_norm_w=g["kvw"][0], mla_q_norm_w=g["qw"][0])


def _big_views(w_in, w_out, w_uq, w_ukv, w_ff1, w_ff2):
    win, wuq, wukv = _in_to_padded(w_in), _uq_to_padded(w_uq), _ukv_to_padded(w_ukv)
    return dict(win=win, wuq=wuq, wukv=wukv, wout=w_out, w1=w_ff1, w2=w_ff2)


def _gla_tile(t):
    return _pick(t, (768, 512, 256))


def _layer_ops(p, sv, a):
    pc = lambda off, w, tm=TM: rows(p, w, off // w, tm=tm)
    gt = _gla_tile(p.shape[0])
    gr = lambda arr: rows(arr, tm=gt)
    return dict(
        sgu=[pc(P_SU, 256), pc(P_SV, 256), const(sv["sgu_nw"]), const(sv["sgu_nb"]), const(sv["sgu_w"]), const(sv["sgu_bm"])],
        gates=[pc(P_GATE, 128), const(sv["wg"]), const(sv["bg"])],
        mla_pre=[pc(P_CKV, 256), pc(P_DQ, 256), const(sv["kvw"]), const(sv["qw"])],
        gla_kv=lambda: [pc(P_GK, 128, gt), pc(P_GV, 256, gt), gr(a["gf"]), gr(a["gb"])],
        gla_o=lambda: [pc(P_GQ, 128, gt), pc(P_GK, 128, gt), pc(P_GV, 256, gt), gr(a["gf"]), gr(a["gb"]), pc(P_GR, 256, gt),
                       chunks(a["sf"], gt // GLA_CHUNK), chunks(a["sb"], gt // GLA_CHUNK), const(sv["gla_nwt"])],
        mla_post=lambda: [rows(a["kvu"], 512, 0), rows(a["qu"]), pc(P_KR, 128), rows(a["cs"]), rows(a["sn"])])


def layer_fwd(l, xt, modl, bw, sv, tabs, side=()):
    t = xt.shape[0]
    g, nc, gt = t // TM, t // GLA_CHUNK, _gla_tile(t)
    gg, cpt = t // gt, gt // GLA_CHUNK
    nm = lambda s: f"l{l}_{s}"
    a = dict(x=xt, cs=tabs[0], sn=tabs[1])
    a["h"], = rw(nm("norm1"), fn_norm1, [rows(xt), const(modl), const(sv["n1w"])], [rowout(t, D, BF16)], g)
    p = a["p"] = mm(nm("in_proj"), a["h"], bw["win"], F32)
    ops = _layer_ops(p, sv, a)
    y_sgu, = rw(nm("sgu"), fn_sgu, ops["sgu"], [rowout(t, 256, BF16)], g)
    a["gf"], a["gb"] = rw(nm("gates"), fn_gates, ops["gates"], [rowout(t, 128, F32)] * 2, g)
    a["uf"], a["ef"], a["ub"], a["eb"] = rw(nm("gla_kv"), fn_gla_kv, ops["gla_kv"](),
                                           [chunkout((nc, 128, 256), F32, cpt), chunkout((nc, 128, 128), F32, cpt)] * 2, gg)
    a["sf"], a["sb"] = gla_states(a["uf"], a["ef"], a["ub"], a["eb"])
    y_gla, = rw(nm("gla_o"), fn_gla_o, ops["gla_o"](), [rowout(t, 256, BF16, tm=gt)], gg)
    a["ckvn"], a["dqn"] = rw(nm("mla_pre"), fn_mla_pre, ops["mla_pre"], [rowout(t, 256, BF16)] * 2, g)
    a["kvu"] = mm(nm("kv_up"), a["ckvn"], bw["wukv"], BF16)
    a["qu"] = mm(nm("q_up"), a["dqn"], bw["wuq"], F32)
    a["kcat"], a["q"] = rw(nm("mla_post"), fn_mla_post, ops["mla_post"](), [rowout(t, 1024, BF16)] * 2, g)
    a["o"], a["lse"], side_out = flash_fwd(a["q"], a["kcat"], a["kvu"], side)
    a["y"] = jnp.concatenate([y_sgu, y_gla, a["o"].astype(BF16)], axis=1)
    a["yo"] = mm(nm("out_proj"), a["y"], bw["wout"], F32)
    a["x1"], a["h2"] = rw(nm("res_norm2"), fn_res_norm2, [rows(xt), rows(a["yo"]), const(modl), const(sv["n2w"])],
                          [rowout(t, D, F32), rowout(t, D, BF16)], g)
    a["act"] = mm(nm("ff1"), a["h2"], bw["w1"], BF16, post=lambda acc: jnp.maximum(acc, 0.0))
    a["f"] = mm(nm("ff2"), a["act"], bw["w2"], F32, pre=_square_bf16)
    x2, = rw(nm("res2"), fn_res2, [rows(a["x1"]), rows(a["f"]), const(modl)], [rowout(t, D, F32)], g)
    return x2, a, side_out


def fn_assemble(blk, gv1, gv2, ckv, su, sv_, gr, dq, gk1, gk2, pg, kr, gq):
    return (jnp.concatenate([gv1 + gv2, ckv, su, sv_, gr, dq, gk1 + gk2, pg, kr, gq], axis=1),)


def layer_bwd(l, dx2, a, modl, bw, sv, side_a=None, make_side_b=None):
    t = dx2.shape[0]
    g, gt = t // TM, _gla_tile(t)
    gg, cpt = t // gt, gt // GLA_CHUNK
    nm = lambda s: f"l{l}_{s}_bwd"
    p = a["p"]
    ops = _layer_ops(p, sv, a)
    gw, gs = {}, {}
    df, dm_a = rw_vjp(nm("res2"), fn_res2, [rows(a["x1"]), rows(a["f"]), const(modl)], [rows(dx2)], [1, 2], g,
                      gdt=[BF16, F32])
    side_b = None
    if side_a:
        gw["w2"], got_a = mm_tn(nm("ff2_w"), a["act"], df, pre=_square_bf16, side=Side(side_a))
        side_b = Side(make_side_b(got_a))
    else:
        gw["w2"] = mm_tn(nm("ff2_w"), a["act"], df, pre=_square_bf16)
    du = mm(nm("ff2_x"), df, bw["w2"], BF16, post=lambda acc, act: acc * (2.0 * act.astype(F32)), extras=(a["act"],), bt=True)
    gw["w1"] = mm_tn(nm("ff1_w"), a["h2"], du)
    dh2 = mm(nm("ff1_x"), du, bw["w1"], F32, bt=True)
    dxa, dyo, dm_b, gs["n2w"] = rw_vjp(nm("res_norm2"), fn_res_norm2,
                                       [rows(a["x"]), rows(a["yo"]), const(modl), const(sv["n2w"])],
                                       [rows(dx2), rows(dh2)], [0, 1, 2, 3], g, gdt=[F32, BF16, F32, F32])
    gw["wout"] = mm_tn(nm("out_w"), a["y"], dyo)
    dy = mm(nm("out_x"), dyo, bw["wout"], F32, bt=True)
    dsu, dsv, gs["sgu_nw"], gs["sgu_nb"], gs["sgu_w"], gs["sgu_bm"] = rw_vjp(
        nm("sgu"), fn_sgu, ops["sgu"], [rows(dy, 256, 0)], [0, 1, 2, 3, 4, 5], g)
    dgq, dgk1, dgv1, dgf1, dgb1, dgr, dsf, dsb, gs["gla_nwt"] = rw_vjp(
        nm("gla_o"), fn_gla_o, ops["gla_o"](), [rows(dy, 256, 1, tm=gt)], list(range(9)), gg)
    duf, def_, dub, deb = gla_states_bwd(a["ef"], a["eb"], a["sf"], a["sb"], dsf, dsb)
    dgk2, dgv2, dgf, dgb = rw_vjp(nm("gla_kv"), fn_gla_kv, ops["gla_kv"](),
                                  [chunks(duf, cpt), chunks(def_, cpt), chunks(dub, cpt), chunks(deb, cpt)], [0, 1, 2, 3], gg,
                                  adds={2: rows(dgf1, tm=gt), 3: rows(dgb1, tm=gt)})
    dpg, gs["wg"], gs["bg"] = rw_vjp(nm("gates"), fn_gates, ops["gates"], [rows(dgf), rows(dgb)], [0, 1, 2], g)
    stats, = rw(nm("attn_stats"), fn_attn_stats, [rows(dy, 512, 1), rows(a["o"]), rows(a["lse"])],
                [chunkout((g, 64, TM), F32, 1)], g)
    dq, dkcat, dv, got_b = flash_bwd(a["q"], a["kcat"], a["kvu"], dy, stats, side=side_b)
    dkk, dqu, dkr = rw_vjp(nm("mla_post"), fn_mla_post, ops["mla_post"](), [rows(dkcat), rows(dq)], [0, 1, 2], g,
                           gdt=[BF16, BF16, F32])
    dkvu = jnp.concatenate([dkk, dv.astype(BF16)], axis=1)
    gw["wukv"] = mm_tn(nm("kv_up_w"), a["ckvn"], dkvu)
    gw["wuq"] = mm_tn(nm("q_up_w"), a["dqn"], dqu)
    dckvn = mm(nm("kv_up_x"), dkvu, bw["wukv"], F32, bt=True)
    ddqn = mm(nm("q_up_x"), dqu, bw["wuq"], F32, bt=True)
    dckv, ddq, gs["kvw"], gs["qw"] = rw_vjp(nm("mla_pre"), fn_mla_pre, ops["mla_pre"], [rows(dckvn), rows(ddqn)],
                                            [0, 1, 2, 3], g)
    dp, = rw(nm("assemble"), fn_assemble,
             [rows(x_) for x_ in (dgv1, dgv2, dckv, dsu, dsv, dgr, ddq, dgk1, dgk2, dpg, dkr, dgq)],
             [rowout(t, P_COLS, BF16)], g)
    gw["win"] = mm_tn(nm("in_w"), a["h"], dp)
    dh = mm(nm("in_x"), dp, bw["win"], F32, bt=True)
    dx, dm_c, gs["n1w"] = rw_vjp(nm("norm1"), fn_norm1, [rows(a["x"]), const(modl), const(sv["n1w"])], [rows(dh)],
                                 [0, 1, 2], g, adds={0: rows(dxa)})
    big = dict(w_in=_in_from_padded(gw["win"]), w_out=gw["wout"], mla_w_uq=_uq_from_padded(gw["wuq"]),
               mla_w_ukv=_ukv_from_padded(gw["wukv"]), w_ff1=gw["w1"], w_ff2=gw["w2"])
    return dx, dm_a + dm_b + dm_c, big, _small_grads(gs), got_b


SMALL_NAMES = ("norm1_w", "sgu_norm_w", "sgu_norm_b", "sgu_w", "sgu_b", "gla_wg_fwd", "gla_bg_fwd", "gla_wg_bwd",
               "gla_bg_bwd", "gla_norm_w", "mla_q_norm_w", "mla_kv_norm_w", "norm2_w")
BIG_NAMES = ("w_in", "w_out", "mla_w_uq", "mla_w_ukv", "w_ff1", "w_ff2")


def local_step(x, ctx, target, mods, big, small, final_norm_w, side=(), later_big=None, grad_side=None):
    n = x.shape[0]
    xt = jnp.concatenate([ctx, x], axis=0)
    tabs = _rope_tables(n)
    depth = len(mods)
    big = list(big)
    svs = [_small_views(small[l]) for l in range(depth)]
    acts, bws = [], []
    for l in range(depth):
        bws.append(_big_views(*[big[l][k] for k in BIG_NAMES]))
        xt, a, side_out = layer_fwd(l, xt, mods[l], bws[l], svs[l], tabs, side if l == 0 else ())
        if l == 0 and later_big is not None:
            big += later_big(side_out)
        acts.append(a)
    loss, dxt, dfnw = loss_head(xt, target, final_norm_w[None])
    dmods, gbig, gsmall = [None] * depth, [None] * depth, [None] * depth
    got = []
    for l in reversed(range(depth)):
        hooks = grad_side(gbig[1:]) if (l == 0 and grad_side is not None and depth > 1) else (None, None)
        dxt, dmods[l], gbig[l], gsmall[l], got = layer_bwd(l, dxt, acts[l], mods[l], bws[l], svs[l], *hooks)
    return loss, dxt[CTX:], dmods, gbig, gsmall, dfnw, got


def _group(group):
    x, y, c = lax.axis_index("x"), lax.axis_index("y"), lax.axis_index("c")
    if group == "sib":
        return 2, c, [((x, y, 1 - c), 1 - c)]
    if group == "chip":
        flips = [(1, 0), (0, 1), (1, 1)]
        return 4, 2 * x + y, [((x ^ fx, y ^ fy, c), 2 * (x ^ fx) + (y ^ fy)) for fx, fy in flips]
    flips = [(fx, fy, fc) for fx in (0, 1) for fy in (0, 1) for fc in (0, 1)][1:]
    return 8, 4 * x + 2 * y + c, [((x ^ fx, y ^ fy, c ^ fc), 4 * (x ^ fx) + 2 * (y ^ fy) + (c ^ fc)) for fx, fy, fc in flips]


def _group_size(group):
    return {"sib": 2, "chip": 4, "all": 8}[group]


REMOTE_COPIES = {"gather": None, "scatter": None, "swap": 1, "gather2": 6}


def _exchange_shapes(entries):
    n_in = sum(len(arrs) for _, _, arrs in entries)
    n_remote = sum(REMOTE_COPIES[k] or _group_size(g) - 1 for k, g, _ in entries)
    n_local = sum(1 for k, _, _ in entries if k != "swap")
    out_shape = []
    for kind, group, arrs in entries:
        a = arrs[0]
        if kind in ("gather", "gather2"):
            out_shape.append(jax.ShapeDtypeStruct((_group_size(group),) + a.shape, a.dtype))
        elif kind == "swap" and len(arrs) == 1:
            out_shape.append(jax.ShapeDtypeStruct(a.shape[1:], a.dtype))
        else:
            out_shape.append(jax.ShapeDtypeStruct(a.shape, a.dtype))
    return n_in, n_remote, n_local, out_shape


def _exchange_sems(entries):
    _, n_remote, n_local, _ = _exchange_shapes(entries)
    return [pltpu.SemaphoreType.DMA((n_remote,)), pltpu.SemaphoreType.DMA((n_remote,)), pltpu.SemaphoreType.DMA((max(n_local, 1),))]


def _exchange_phases(entries, in_refs, out_refs, send_sems, recv_sems, local_sems):
    x, y, c = lax.axis_index("x"), lax.axis_index("y"), lax.axis_index("c")

    def remote(src, dst, k, dev):
        return pltpu.make_async_remote_copy(src_ref=src, dst_ref=dst, send_sem=send_sems.at[k], recv_sem=recv_sems.at[k],
                                            device_id=dev, device_id_type=MESH)

    pos, k, kl = 0, 0, 0
    starts, forwards, finals = [], [], []
    for (kind, group, arrs), out in zip(entries, out_refs):
        srcs = in_refs[pos:pos + len(arrs)]
        pos += len(arrs)
        _, mine, peers = _group(group)
        if kind == "swap":
            (dev, _), = peers
            if len(srcs) == 1:
                starts.append(remote(srcs[0].at[1 - c], out, k, dev).start)
                finals.append(remote(srcs[0].at[0], out, k, dev).wait)
            else:
                def start_swap(srcs=srcs, k=k, dev=dev, out=out):
                    for core, src in ((0, srcs[1]), (1, srcs[0])):
                        @pl.when(c == core)
                        def _(src=src):
                            remote(src, out, k, dev).start()

                starts.append(start_swap)
                finals.append(remote(srcs[0], out, k, dev).wait)
            k += 1
            continue
        src = srcs[0]
        own = pltpu.make_async_copy(src if kind != "scatter" else src.at[mine], out.at[mine], local_sems.at[kl])
        starts.append(own.start)
        finals.append(own.wait)
        kl += 1
        if kind == "gather2":
            sibling = (x, y, 1 - c)
            for f, (dev, slot) in enumerate(peers):
                starts.append(remote(src.at[c], out.at[mine, c], k + f, dev).start)
                arrival = remote(src.at[c], out.at[slot, c], k + f, dev)

                def forward(arrival=arrival, slot=slot, kf=k + 3 + f, out=out):
                    arrival.wait_recv()
                    remote(out.at[slot, c], out.at[slot, c], kf, sibling).start()

                forwards.append(forward)
                finals.append(arrival.wait_send)
                finals.append(remote(out.at[slot, c], out.at[slot, 1 - c], k + 3 + f, sibling).wait)
            k += 6
            continue
        for dev, slot in peers:
            piece = src if kind == "gather" else src.at[slot]
            starts.append(remote(piece, out.at[mine], k, dev).start)
            finals.append(remote(piece, out.at[slot], k, dev).wait)
            k += 1
    return starts, forwards, finals


class Side:
    def __init__(self, entries):
        self.entries = tuple(entries)
        self.n_in, _, _, self.shapes = _exchange_shapes(self.entries)
        self.n_out = len(self.entries)
        self.arrays = [a for _, _, arrs in self.entries for a in arrs]
        any_spec = pl.BlockSpec(memory_space=pl.ANY)
        self.in_specs, self.out_specs = [any_spec] * self.n_in, [any_spec] * self.n_out
        self.sems = _exchange_sems(self.entries) if self.entries else []

    def start(self, in_refs, out_refs, sem_refs):
        if not self.entries:
            return lambda: None
        ids = [pl.program_id(d) for d in range(3)]
        first = (ids[0] == 0) & (ids[1] == 0) & (ids[2] == 0)
        last = ((ids[0] == pl.num_programs(0) - 1) & (ids[1] == pl.num_programs(1) - 1) & (ids[2] == pl.num_programs(2) - 1))
        starts, forwards, finals = _exchange_phases(self.entries, in_refs, out_refs, *sem_refs)
        assert not forwards

        @pl.when(first)
        def _():
            for run in starts:
                run()

        def finish():
            @pl.when(last)
            def _():
                for run in finals:
                    run()

        return finish


def xchg(name, entries):
    n_in, _, _, out_shape = _exchange_shapes(entries)

    def body(*refs):
        in_refs, out_refs = refs[:n_in], refs[n_in:n_in + len(entries)]
        for phase in _exchange_phases(entries, in_refs, out_refs, *refs[n_in + len(entries):]):
            for run in phase:
                run()

    any_spec = pl.BlockSpec(memory_space=pl.ANY)
    return pl.pallas_call(
        body, name=name,
        in_specs=[any_spec] * n_in, out_specs=[any_spec] * len(entries), out_shape=out_shape,
        scratch_shapes=_exchange_sems(entries),
    )(*[a for _, _, arrs in entries for a in arrs])


def _block_rows(r, c, budget=131072):
    tr = 8
    while tr * 2 * c <= budget and r % (tr * 2) == 0:
        tr *= 2
    return tr if r % tr == 0 else r


def tree_sum(name, parts):
    g, r, c = parts.shape
    tr = _block_rows(r, c)

    def body(p_ref, o_ref):
        p = [p_ref[i].astype(F32) for i in range(g)]
        while len(p) > 1:
            p = [p[i] + p[i + 1] for i in range(0, len(p), 2)]
        o_ref[...] = p[0]

    return pl.pallas_call(
        body, name=name, grid=(r // tr,),
        in_specs=[pl.BlockSpec((g, tr, c), lambda i: (0, i, 0))], out_specs=pl.BlockSpec((tr, c), lambda i: (i, 0)),
        out_shape=jax.ShapeDtypeStruct((r, c), F32), compiler_params=_params(("arbitrary",)),
    )(parts)


def pair_sum(name, halves, recv, core):
    r, c = recv.shape
    tr = _block_rows(r, c)

    def body(h_ref, r_ref, k_ref, o_ref):
        o_ref[...] = (jnp.where(k_ref[...] > 0.5, h_ref[1], h_ref[0]) + r_ref[...]).astype(o_ref.dtype)

    blk = pl.BlockSpec((tr, c), lambda i: (i, 0))
    return pl.pallas_call(
        body, name=name, grid=(r // tr,),
        in_specs=[pl.BlockSpec((2, tr, c), lambda i: (0, i, 0)), blk, pl.BlockSpec((1, 1), lambda i: (0, 0))],
        out_specs=blk, out_shape=jax.ShapeDtypeStruct((r, c), BF16), compiler_params=_params(("arbitrary",)),
    )(halves, recv, core)


def adamw(name, w, g, m, v):
    r, c = w.shape
    tr = _block_rows(r, c)

    def body(w_ref, g_ref, m_ref, v_ref, d_ref, nm_ref, nv_ref):
        gg = g_ref[...]
        nm = ADAM_B1 * m_ref[...] + (1.0 - ADAM_B1) * gg
        nv = ADAM_B2 * v_ref[...] + (1.0 - ADAM_B2) * jnp.square(gg)
        m_hat = nm / (1.0 - ADAM_B1 ** ADAM_STEP)
        v_hat = nv / (1.0 - ADAM_B2 ** ADAM_STEP)
        d_ref[...] = -ADAM_LR * (m_hat / (jnp.sqrt(v_hat) + ADAM_EPS) + ADAM_WD * w_ref[...])
        nm_ref[...] = nm
        nv_ref[...] = nv

    blk = pl.BlockSpec((tr, c), lambda i: (i, 0))
    return pl.pallas_call(
        body, name=name, grid=(r // tr,), in_specs=[blk] * 4, out_specs=[blk] * 3,
        out_shape=[jax.ShapeDtypeStruct((r, c), F32)] * 3, compiler_params=_params(("arbitrary",)),
    )(w, g, m, v)


W_MOD_COLS = 6 * D // 4
MOD_TN = 512


def mod_project(c16, w_mod, b_loc):
    def body(c_ref, w_ref, b_ref, o_ref):
        cv = c_ref[...]
        s = (cv * _sigmoid(cv)).astype(BF16)
        o_ref[0] = _dot(s, w_ref[0].astype(BF16)) + b_ref[0]

    return pl.pallas_call(
        body, name="mod_project", grid=(2, W_MOD_COLS // MOD_TN),
        in_specs=[pl.BlockSpec((16, D), lambda l, j: (0, 0)), pl.BlockSpec((1, D, MOD_TN), lambda l, j: (l, 0, j)),
                  pl.BlockSpec((1, 1, MOD_TN), lambda l, j: (l, 0, j))],
        out_specs=pl.BlockSpec((1, 16, MOD_TN), lambda l, j: (l, 0, j)),
        out_shape=jax.ShapeDtypeStruct((2, 16, W_MOD_COLS), F32), compiler_params=_params(("arbitrary", "arbitrary")),
    )(c16, w_mod, b_loc)


def mod_weight_grad(c16, dm16):
    def body(c_ref, d_ref, o_ref):
        cv = c_ref[...]
        o_ref[0] = _dot(cv * _sigmoid(cv), d_ref[0], ((0,), (0,)), precision=HI)

    return pl.pallas_call(
        body, name="mod_weight_grad", grid=(2, W_MOD_COLS // MOD_TN),
        in_specs=[pl.BlockSpec((16, D), lambda l, j: (0, 0)), pl.BlockSpec((1, 16, MOD_TN), lambda l, j: (l, 0, j))],
        out_specs=pl.BlockSpec((1, D, MOD_TN), lambda l, j: (l, 0, j)),
        out_shape=jax.ShapeDtypeStruct((2, D, W_MOD_COLS), F32), compiler_params=_params(("arbitrary", "arbitrary")),
    )(c16, dm16)


def cctx_partial(dmc, w_mod):
    def body(d_ref, w_ref, o_ref):
        @pl.when(pl.program_id(0) == 0)
        def _():
            o_ref[...] = jnp.zeros_like(o_ref)
        o_ref[...] += _dot(d_ref[0], w_ref[0], ((1,), (1,)), precision=HI)

    return pl.pallas_call(
        body, name="cctx_partial", grid=(2,),
        in_specs=[pl.BlockSpec((1, 8, W_MOD_COLS), lambda l: (l, 0, 0)), pl.BlockSpec((1, D, W_MOD_COLS), lambda l: (l, 0, 0))],
        out_specs=pl.BlockSpec((8, D), lambda l: (0, 0)),
        out_shape=jax.ShapeDtypeStruct((8, D), F32), compiler_params=_params(("arbitrary",)),
    )(dmc, w_mod)


def cctx_grad(parts, c_ctx8):
    def body(p_ref, c_ref, o_ref):
        ds = (p_ref[0] + p_ref[1]) + (p_ref[2] + p_ref[3])
        _, vf = jax.vjp(lambda z: z * _sigmoid(z), c_ref[...])
        o_ref[...] = vf(ds)[0]

    return pl.pallas_call(
        body, name="cctx_grad", out_shape=jax.ShapeDtypeStruct((8, D), F32),
    )(parts, c_ctx8)


ARG_NAMES = ("x", "c", "ctx", "c_ctx", "w_mod", "b_mod", "norm1_w", "w_in", "w_out", "sgu_norm_w", "sgu_norm_b", "sgu_w",
             "sgu_b", "gla_wg_fwd", "gla_bg_fwd", "gla_wg_bwd", "gla_bg_bwd", "gla_norm_w", "mla_q_norm_w", "mla_w_uq",
             "mla_kv_norm_w", "mla_w_ukv", "norm2_w", "w_ff1", "w_ff2", "final_norm_w")
WEIGHT_NAMES = ARG_NAMES[3:]
PACKED = ("c_ctx", "b_mod") + SMALL_NAMES + ("final_norm_w",)
ROW_SHARDED = ("w_out", "w_ff2")
PACK_ROWS = 256


def _pack(vectors):
    flat = jnp.concatenate([v.reshape(-1) for v in vectors])
    n = flat.shape[0]
    total = -(-n // (PACK_ROWS * LANES)) * PACK_ROWS * LANES
    return jnp.pad(flat, (0, total - n)).reshape(-1, LANES)


def _unpack(buf, shapes):
    flat, out, pos = buf.reshape(-1), [], 0
    for shp in shapes:
        n = int(np.prod(shp))
        out.append(flat[pos:pos + n].reshape(shp))
        pos += n
    return out


def _full_weight(name, g):
    if name in ROW_SHARDED:
        return g.reshape(-1, g.shape[-1])
    return g.transpose(1, 0, 2).reshape(g.shape[1], -1)


def _chip_chunks(name, a):
    if name in ROW_SHARDED:
        return a.reshape(4, a.shape[0] // 4, a.shape[1])
    return a.reshape(a.shape[0], 4, a.shape[1] // 4).transpose(1, 0, 2)


def kernel(x, c, ctx, c_ctx, w_mod, b_mod, norm1_w, w_in, w_out, sgu_norm_w, sgu_norm_b, sgu_w, sgu_b, gla_wg_fwd, gla_bg_fwd, gla_wg_bwd, gla_bg_bwd, gla_norm_w, mla_q_norm_w, mla_w_uq, mla_kv_norm_w, mla_w_ukv, norm2_w, w_ff1, w_ff2, final_norm_w, loss_target, m_c_ctx, m_w_mod, m_b_mod, m_norm1_w, m_w_in, m_w_out, m_sgu_norm_w, m_sgu_norm_b, m_sgu_w, m_sgu_b, m_gla_wg_fwd, m_gla_bg_fwd, m_gla_wg_bwd, m_gla_bg_bwd, m_gla_norm_w, m_mla_q_norm_w, m_mla_w_uq, m_mla_kv_norm_w, m_mla_w_ukv, m_norm2_w, m_w_ff1, m_w_ff2, m_final_norm_w, v_c_ctx, v_w_mod, v_b_mod, v_norm1_w, v_w_in, v_w_out, v_sgu_norm_w, v_sgu_norm_b, v_sgu_w, v_sgu_b, v_gla_wg_fwd, v_gla_bg_fwd, v_gla_wg_bwd, v_gla_bg_bwd, v_gla_norm_w, v_mla_q_norm_w, v_mla_w_uq, v_mla_kv_norm_w, v_mla_w_ukv, v_norm2_w, v_w_ff1, v_w_ff2, v_final_norm_w):
    args = (x, c, ctx, c_ctx, w_mod, b_mod, norm1_w, w_in, w_out, sgu_norm_w, sgu_norm_b, sgu_w, sgu_b, gla_wg_fwd, gla_bg_fwd, gla_wg_bwd, gla_bg_bwd, gla_norm_w, mla_q_norm_w, mla_w_uq, mla_kv_norm_w, mla_w_ukv, norm2_w, w_ff1, w_ff2, final_norm_w)
    w = dict(zip(ARG_NAMES, args))
    moms = (m_c_ctx, m_w_mod, m_b_mod, m_norm1_w, m_w_in, m_w_out, m_sgu_norm_w, m_sgu_norm_b, m_sgu_w, m_sgu_b, m_gla_wg_fwd, m_gla_bg_fwd, m_gla_wg_bwd, m_gla_bg_bwd, m_gla_norm_w, m_mla_q_norm_w, m_mla_w_uq, m_mla_kv_norm_w, m_mla_w_ukv, m_norm2_w, m_w_ff1, m_w_ff2, m_final_norm_w)
    vars_ = (v_c_ctx, v_w_mod, v_b_mod, v_norm1_w, v_w_in, v_w_out, v_sgu_norm_w, v_sgu_norm_b, v_sgu_w, v_sgu_b, v_gla_wg_fwd, v_gla_bg_fwd, v_gla_wg_bwd, v_gla_bg_bwd, v_gla_norm_w, v_mla_q_norm_w, v_mla_w_uq, v_mla_kv_norm_w, v_mla_w_ukv, v_norm2_w, v_w_ff1, v_w_ff2, v_final_norm_w)
    m1 = dict(zip(WEIGHT_NAMES, moms))
    m2 = dict(zip(WEIGHT_NAMES, vars_))
    xi, yi, ci = lax.axis_index("x"), lax.axis_index("y"), lax.axis_index("c")
    chip, dev = 2 * xi + yi, 4 * xi + 2 * yi + ci
    depth = w_mod.shape[0]

    def shard_halves(l):
        return [("gather2", "chip", [w[k][l].astype(BF16).reshape(2, w[k].shape[1] // 2, w[k].shape[2])]) for k in BIG_NAMES]

    def full_weights(gathered):
        return {k: _full_weight(k, g.reshape(4, *w[k].shape[1:])) for k, g in zip(BIG_NAMES, gathered)}

    got = xchg("gather_inputs", [("gather", "all", [c])] + shard_halves(0))
    c_all = got[0]
    c16 = jnp.concatenate([c_all.reshape(8, D), c_ctx[None], jnp.zeros((7, D), F32)], axis=0)
    b_loc = lax.dynamic_slice_in_dim(b_mod, chip * W_MOD_COLS, W_MOD_COLS, axis=1)[:, None, :]
    mod_part = mod_project(c16, w_mod, b_loc)
    mod_all, = xchg("gather_mod", [("gather", "chip", [mod_part])])
    mod_full = mod_all.transpose(1, 2, 0, 3).reshape(depth, 16, 6 * D)
    mods = [jnp.stack([mod_full[l, 8], lax.dynamic_index_in_dim(mod_full[l], dev, 0, keepdims=False)])[:, None, :]
            for l in range(depth)]

    small = [{k: w[k][l] for k in SMALL_NAMES} for l in range(depth)]
    n_big = len(BIG_NAMES)
    later = [e for l in range(1, depth) for e in shard_halves(l)]
    core = ci.astype(F32).reshape(1, 1)

    def half_major(k, g):
        ch = _chip_chunks(k, g)
        return ch.reshape(4, 2, ch.shape[1] // 2, ch.shape[2]).transpose(1, 0, 2, 3)

    def swap_entries(gb):
        hm = [half_major(k, gb[k]) for k in BIG_NAMES]
        return hm, [("swap", "sib", [h]) for h in hm]

    def scatter_entries(tag, hm, recv):
        out = []
        for k, h, r in zip(BIG_NAMES, hm, recv):
            s2 = pair_sum(f"pair_sum_{tag}_{k}", h.reshape(2, -1, h.shape[-1]), r.reshape(-1, r.shape[-1]), core)
            out.append(("scatter", "chip", [s2.reshape(r.shape)]))
        return out

    def grad_side(gb_later):
        hms, entries = [], []
        for gb in gb_later:
            hm, e = swap_entries(gb)
            hms.append(hm)
            entries += e

        def make_scatter(recv):
            return [e for i, hm in enumerate(hms) for e in scatter_entries(f"l{i + 1}", hm, recv[i * n_big:(i + 1) * n_big])]

        return entries, make_scatter

    loss, grad_x, dmods, gbig, gsmall, dfnw, later_chunks = local_step(
        x[0], ctx[0], loss_target[0], mods, [full_weights(got[1:])], small, final_norm_w, side=later,
        later_big=lambda res: [full_weights(res[i * n_big:(i + 1) * n_big]) for i in range(depth - 1)], grad_side=grad_side)
    loss = lax.psum(loss[0, 0], ("x", "y", "c"))

    dm_lat = jnp.stack([dmods[l][1, 0] for l in range(depth)])
    dm_ctx = jnp.stack([dmods[l][0, 0] for l in range(depth)])
    small_pack = _pack([dm_lat, dm_ctx] + [jnp.stack([gsmall[l][k] for l in range(depth)]) for k in SMALL_NAMES] + [dfnw])
    hm0, swap0 = swap_entries(gbig[0])
    got = xchg("exchange_grads", [("gather", "all", [small_pack])] + swap0)
    small_all, recv0 = got[0], got[1:]
    small_sum = tree_sum("small_grad_sum", small_all)

    n_dm = depth * 6 * D
    dm_rows = n_dm // LANES
    dm_lat_all = small_all[:, :dm_rows].reshape(8, depth, 6 * D)
    dm_ctx_sum = small_sum[dm_rows:2 * dm_rows].reshape(depth, 6 * D)
    take = lambda a: lax.dynamic_slice_in_dim(a, chip * W_MOD_COLS, W_MOD_COLS, axis=-1)
    dmc_loc = take(dm_ctx_sum)
    cc_part = cctx_partial(jnp.pad(dmc_loc[:, None, :], ((0, 0), (0, 7), (0, 0))), w_mod)
    got = xchg("scatter_grads", [("gather", "chip", [cc_part])] + scatter_entries("l0", hm0, recv0))
    cc_parts = got[0]
    pieces = list(got[1:]) + list(later_chunks)
    reduced = [tree_sum(f"chip_sum_l{i // n_big}_{BIG_NAMES[i % n_big]}", p) for i, p in enumerate(pieces)]
    g_c_ctx = cctx_grad(cc_parts, jnp.broadcast_to(c_ctx[None], (8, D)))[0]

    others = xchg("share_halves", [("swap", "sib", [r, r]) for r in reduced])
    shard = [jnp.where(ci == 0, jnp.concatenate([r, o], axis=0), jnp.concatenate([o, r], axis=0)) for r, o in zip(reduced, others)]
    grads = {k: jnp.stack([shard[l * n_big + j] for l in range(depth)]) for j, k in enumerate(BIG_NAMES)}

    dm16 = jnp.concatenate([take(dm_lat_all).transpose(1, 0, 2), dmc_loc[:, None, :], jnp.zeros((depth, 7, W_MOD_COLS), F32)], axis=1)
    grads["w_mod"] = mod_weight_grad(c16, dm16)
    flat_sum = small_sum.reshape(-1)
    g_b_mod = (flat_sum[:n_dm] + flat_sum[n_dm:2 * n_dm]).reshape(depth, 6 * D)
    rest_shapes = [w[k].shape for k in PACKED[2:]]
    n_rest = sum(int(np.prod(s)) for s in rest_shapes)
    for k, g in zip(PACKED, [g_c_ctx, g_b_mod] + _unpack(flat_sum[2 * n_dm:2 * n_dm + n_rest], rest_shapes)):
        grads[k] = g

    delta, new_m, new_v = {}, {}, {}
    for k in BIG_NAMES + ("w_mod",):
        view = lambda a: a.reshape(-1, a.shape[-1])
        d_, m_, v_ = adamw(f"adamw_{k}", view(w[k]), view(grads[k]), view(m1[k]), view(m2[k]))
        delta[k], new_m[k], new_v[k] = d_.reshape(w[k].shape), m_.reshape(w[k].shape), v_.reshape(w[k].shape)
    shapes = [w[k].shape for k in PACKED]
    d_, m_, v_ = adamw("adamw_small", _pack([w[k] for k in PACKED]), _pack([grads[k] for k in PACKED]),
                       _pack([m1[k] for k in PACKED]), _pack([m2[k] for k in PACKED]))
    for k, dk, mk, vk in zip(PACKED, _unpack(d_, shapes), _unpack(m_, shapes), _unpack(v_, shapes)):
        delta[k], new_m[k], new_v[k] = dk, mk, vk
    return (loss, grad_x[None], *[grads[k] for k in WEIGHT_NAMES], *[delta[k] for k in WEIGHT_NAMES],
            *[new_m[k] for k in WEIGHT_NAMES], *[new_v[k] for k in WEIGHT_NAMES])
```

```python
import functools
import math

import numpy as np
import jax
import jax.numpy as jnp
from jax import lax
from jax.experimental import pallas as pl
from jax.experimental.pallas import tpu as pltpu

F32 = jnp.float32
BF16 = jnp.bfloat16
HI = lax.Precision.HIGHEST
EPS = 1e-6
VMEM_LIMIT_BYTES = 56 * 1024 * 1024
LANES = 128

D = 1024
D_FF = 4096
CTX = 256
GRID_W = 64
SGU_CHUNK = 128
GLA_CHUNK = 64
GLA_TAU = 16.0
GLA_DK = 32
MLA_SCALE = (128 + 64) ** -0.5
SCORE_SCALE = MLA_SCALE * math.log2(math.e)
LN2 = math.log(2.0)
ROPE_BASE = 10000.0
TM = 256
NCTXB = CTX // TM
P_GV, P_CKV, P_SU, P_SV, P_GR, P_DQ, P_GK, P_GATE, P_KR, P_GQ = 0, 256, 512, 768, 1024, 1280, 1536, 1664, 1792, 1920
P_COLS = 2048
IN_GROUPS = ((0, 128, P_GK), (128, 256, P_GV), (384, 32, P_GATE), (416, 256, P_CKV), (672, 64, P_KR),
             (736, 256, P_SU), (992, 256, P_SV), (1248, 128, P_GQ), (1376, 256, P_GR), (1632, 256, P_DQ))
ADAM_LR, ADAM_B1, ADAM_B2, ADAM_EPS, ADAM_WD, ADAM_STEP = 0.001, 0.9, 0.999, 1e-08, 0.01, 10
MESH = pl.DeviceIdType.MESH


def _params(sem):
    return pltpu.CompilerParams(dimension_semantics=sem, vmem_limit_bytes=VMEM_LIMIT_BYTES)


def _pick(n, cands):
    for c in cands:
        if n % c == 0:
            return c
    return n


class Op:
    def __init__(self, arr, blk, idx, gshape, gidx, acc):
        self.arr, self.blk, self.idx, self.gshape, self.gidx, self.acc = arr, blk, idx, gshape, gidx, acc

    def spec(self):
        return pl.BlockSpec(self.blk, self.idx)


def rows(arr, width=None, cb=0, off=0, tm=TM):
    w = arr.shape[1] if width is None else width
    n = arr.shape[0] - off * tm
    return Op(arr, (tm, w), lambda i: (i + off, cb), (n, w), lambda i: (i, 0), False)


def chunks(arr, per_tile):
    z = (0,) * (arr.ndim - 1)
    return Op(arr, (per_tile,) + arr.shape[1:], lambda i: (i,) + z, arr.shape, lambda i: (i,) + z, False)


def const(arr):
    z = (0,) * arr.ndim
    return Op(arr, arr.shape, lambda i: z, arr.shape, lambda i: z, True)


def rw(name, fn, ins, outs, grid):
    nin = len(ins)

    def body(*refs):
        vals = [r[...] for r in refs[:nin]]
        res = fn(pl.program_id(0), *vals)
        for o, r in zip(refs[nin:], res):
            o[...] = r.astype(o.dtype)

    return pl.pallas_call(
        body, name=name, grid=(grid,),
        in_specs=[o.spec() for o in ins],
        out_specs=[pl.BlockSpec(b, ix) for (_, _, b, ix) in outs],
        out_shape=[jax.ShapeDtypeStruct(s, d) for (s, d, _, _) in outs],
        compiler_params=_params(("arbitrary",)),
    )(*[o.arr for o in ins])


def rowout(n, w, dtype, tm=TM):
    return ((n, w), dtype, (tm, w), lambda i: (i, 0))


def chunkout(shape, dtype, per_tile):
    z = (0,) * (len(shape) - 1)
    return (shape, dtype, (per_tile,) + tuple(shape[1:]), lambda i: (i,) + z)


def rw_vjp(name, fn, ins, cots, wrt, grid, gdt=None, adds=None):
    nin = len(ins)
    cot_ops = [c for c in cots if c is not None]
    add_items = sorted((adds or {}).items())
    gdt = gdt or [F32] * len(wrt)
    ncot, nadd = len(cot_ops), len(add_items)

    def body(*refs):
        i = pl.program_id(0)
        vals = [r[...] for r in refs[:nin]]
        cvals = [r[...] for r in refs[nin:nin + ncot]]
        avals = [r[...] for r in refs[nin + ncot:nin + ncot + nadd]]
        grefs = refs[nin + ncot + nadd:]

        def f(*d):
            a = list(vals)
            for k, dv in zip(wrt, d):
                a[k] = dv
            return tuple(fn(i, *a))

        outs, vf = jax.vjp(f, *[vals[k] for k in wrt])
        it = iter(cvals)
        ct = tuple(jnp.zeros_like(o) if c is None else next(it).astype(o.dtype) for c, o in zip(cots, outs))
        gs = list(vf(ct))
        for (pos, _), av in zip(add_items, avals):
            gs[pos] = gs[pos].astype(F32) + av.astype(F32)
        for pos, (k, g, gref) in enumerate(zip(wrt, gs, grefs)):
            if ins[k].acc:
                @pl.when(i == 0)
                def _():
                    gref[...] = jnp.zeros_like(gref)
                gref[...] += g.astype(gref.dtype)
            else:
                gref[...] = g.astype(gref.dtype)

    all_in = list(ins) + cot_ops + [op for _, op in add_items]
    return pl.pallas_call(
        body, name=name, grid=(grid,),
        in_specs=[o.spec() for o in all_in],
        out_specs=[pl.BlockSpec(ins[k].blk, ins[k].gidx) for k in wrt],
        out_shape=[jax.ShapeDtypeStruct(ins[k].gshape, dt) for k, dt in zip(wrt, gdt)],
        compiler_params=_params(("arbitrary",)),
    )(*[o.arr for o in all_in])


MM_VMEM_BUDGET = 40 * 1024 * 1024
MM_COLS = 1024


def _square_bf16(a):
    a = a.astype(F32)
    return (a * a).astype(BF16)


def mm(name, a, b, out_dtype, pre=None, post=None, extras=(), bt=False):
    m, k = a.shape
    n = b.shape[0] if bt else b.shape[1]
    nc = min(n, MM_COLS)
    row_bytes = k * a.dtype.itemsize + n * jnp.dtype(out_dtype).itemsize + sum(n * e.dtype.itemsize for e in extras)
    tm = next(t for t in (768, 512, 384, 256, 128, 64)
              if m % t == 0 and 2 * t * row_bytes + 2 * k * n * b.dtype.itemsize + t * nc * 4 <= MM_VMEM_BUDGET)

    def body(a_ref, b_ref, *rest):
        o_ref = rest[-1]
        av = a_ref[...]
        if pre is not None:
            av = pre(av)
        for j in range(n // nc):
            cs = slice(j * nc, (j + 1) * nc)
            if bt:
                acc = lax.dot_general(av, b_ref[cs, :], (((1,), (1,)), ((), ())), preferred_element_type=F32)
            else:
                acc = lax.dot_general(av, b_ref[:, cs], (((1,), (0,)), ((), ())), preferred_element_type=F32)
            if post is not None:
                acc = post(acc, *[e[:, cs] for e in rest[:-1]])
            o_ref[:, cs] = acc.astype(o_ref.dtype)

    row = lambda w: pl.BlockSpec((tm, w), lambda i: (i, 0))
    return pl.pallas_call(
        body, name=name, grid=(m // tm,),
        in_specs=[row(k), pl.BlockSpec(b.shape, lambda i: (0, 0))] + [row(n) for _ in extras],
        out_specs=row(n),
        out_shape=jax.ShapeDtypeStruct((m, n), out_dtype),
        compiler_params=_params(("arbitrary",)),
    )(a, b, *extras)


def mm_tn(name, a, b, pre=None, side=None):
    m, ka = a.shape
    _, nb = b.shape
    tm = _pick(m, (768, 512, 256))
    ta = _pick(ka, (2048, 1024, 512, 256, 128))
    tb = _pick(nb, tuple(t for t in (4096, 2048, 1024, 512, 256, 128) if ta * t * 4 <= 8 * 1024 * 1024))
    sd = side or Side(())

    def body(a_ref, b_ref, *rest):
        o_ref = rest[sd.n_in]
        finish = sd.start(rest[:sd.n_in], rest[sd.n_in + 1:sd.n_in + 1 + sd.n_out], rest[sd.n_in + 1 + sd.n_out:])

        @pl.when(pl.program_id(2) == 0)
        def _():
            o_ref[...] = jnp.zeros_like(o_ref)
        av = a_ref[...] if pre is None else pre(a_ref[...])
        o_ref[...] += lax.dot_general(av, b_ref[...], (((0,), (0,)), ((), ())), preferred_element_type=F32)
        finish()

    res = pl.pallas_call(
        body, name=name, grid=(ka // ta, nb // tb, m // tm),
        in_specs=[pl.BlockSpec((tm, ta), lambda i, j, k: (k, i)), pl.BlockSpec((tm, tb), lambda i, j, k: (k, j))] + sd.in_specs,
        out_specs=[pl.BlockSpec((ta, tb), lambda i, j, k: (i, j))] + sd.out_specs,
        out_shape=[jax.ShapeDtypeStruct((ka, nb), F32)] + sd.shapes,
        scratch_shapes=sd.sems,
        compiler_params=_params(("arbitrary", "arbitrary", "arbitrary")),
    )(a, b, *sd.arrays)
    return res[0] if side is None else (res[0], list(res[1:]))


def _rms(x, w):
    return x * lax.rsqrt(jnp.mean(x * x, axis=-1, keepdims=True) + EPS) * w


def _mod_of(blk, m):
    return jnp.where(blk < NCTXB, m[0], m[1])


def _gelu(x):
    return x * (0.5 * (1.0 + jnp.tanh(math.sqrt(2.0 / math.pi) * (x + 0.044715 * (x * x * x)))))


def _sigmoid(x):
    return 1.0 / (1.0 + jnp.exp(-x))


def _log_sigmoid(z):
    return jnp.minimum(z, 0.0) - jnp.log(1.0 + jnp.exp(-jnp.abs(z)))


def _dot(a, b, dims=((1,), (0,)), precision=None):
    return lax.dot_general(a, b, (dims, ((), ())), precision=precision, preferred_element_type=F32)


def _lane_group_mask(width, group, h):
    lane = lax.broadcasted_iota(jnp.int32, (1, width), 1)
    return (lane >= h * group) & (lane < (h + 1) * group)


def fn_norm1(blk, x, m, nw):
    mv = _mod_of(blk, m)
    return ((_rms(x, nw) * (1.0 + mv[:, D:2 * D]) + mv[:, 0:D]),)


def fn_res_norm2(blk, x, yo, m, nw):
    mv = _mod_of(blk, m)
    x1 = x + mv[:, 2 * D:3 * D] * yo
    return x1, _rms(x1, nw) * (1.0 + mv[:, 4 * D:5 * D]) + mv[:, 3 * D:4 * D]


def fn_res2(blk, x1, f, m):
    mv = _mod_of(blk, m)
    return (x1 + mv[:, 5 * D:6 * D] * f,)


def fn_sgu(blk, su, sv, nw, nb, ws, bm):
    u = _gelu(su)
    g = _gelu(sv)
    mu = jnp.mean(g, axis=-1, keepdims=True)
    var = jnp.mean(jnp.square(g - mu), axis=-1, keepdims=True)
    v = (g - mu) * lax.rsqrt(var + EPS) * nw + nb
    out = []
    for c in range(su.shape[0] // SGU_CHUNK):
        vc = v[c * SGU_CHUNK:(c + 1) * SGU_CHUNK]
        s = bm
        for h in range(4):
            vh = jnp.where(_lane_group_mask(256, 64, h), vc, 0.0)
            s = s + _dot(ws[h].astype(BF16), vh.astype(BF16))
        out.append(u[c * SGU_CHUNK:(c + 1) * SGU_CHUNK] * s)
    return (jnp.concatenate(out, axis=0),)


def fn_gates(blk, pg, wg, bg):
    z = _dot(pg.astype(BF16), wg.astype(BF16)) + bg
    g = _log_sigmoid(z) * (1.0 / GLA_TAU)
    return g[:, :128], g[:, 128:]


def _scan_rows(x, rev):
    n = x.shape[0]
    row = lax.broadcasted_iota(jnp.int32, x.shape, 0)
    d = 1
    while d < n:
        if rev:
            x = x + jnp.where(row < n - d, pltpu.roll(x, n - d, 0), 0.0)
        else:
            x = x + jnp.where(row >= d, pltpu.roll(x, d, 0), 0.0)
        d *= 2
    return x


@functools.partial(jax.custom_vjp, nondiff_argnums=(1,))
def _cumsum_rows(x, rev):
    return _scan_rows(x, rev)


def _cumsum_rows_fwd(x, rev):
    return _scan_rows(x, rev), None


def _cumsum_rows_bwd(rev, _, dy):
    return (_scan_rows(dy, not rev),)


_cumsum_rows.defvjp(_cumsum_rows_fwd, _cumsum_rows_bwd)


def _gla_chunk_terms(g, rev):
    return _cumsum_rows(g, rev), jnp.sum(g, axis=0, keepdims=True)


def _bd_mask():
    r = lax.broadcasted_iota(jnp.int32, (128, 256), 0)
    c = lax.broadcasted_iota(jnp.int32, (128, 256), 1)
    return (r // GLA_DK) == (c // 64)


def _gla_kv_chunk(k, v, g, rev):
    b, tot = _gla_chunk_terms(g, rev)
    kd = k * jnp.exp(tot - b)
    u = jnp.where(_bd_mask(), _dot(kd.astype(BF16), v.astype(BF16), ((0,), (0,))), 0.0)
    r = lax.broadcasted_iota(jnp.int32, (128, 128), 0)
    c = lax.broadcasted_iota(jnp.int32, (128, 128), 1)
    col = jnp.sum(jnp.where(r == c, jnp.broadcast_to(jnp.exp(tot), (128, 128)), 0.0), axis=1, keepdims=True)
    return u, jnp.broadcast_to(col, (128, 128))


def _gla_o_chunk(q, k, v, g, s, rev):
    b, _ = _gla_chunk_terms(g, rev)
    qe = q * jnp.exp(b) * (GLA_DK ** -0.5)
    ke = k * jnp.exp(-b)
    o = _dot(qe.astype(BF16), jnp.where(_bd_mask(), s, 0.0).astype(BF16))
    qs = jnp.concatenate([jnp.where(_lane_group_mask(128, GLA_DK, h), qe, 0.0) for h in range(4)], axis=0)
    a = _dot(qs.astype(BF16), ke.astype(BF16), ((1,), (1,)))
    i = lax.broadcasted_iota(jnp.int32, a.shape, 0) % GLA_CHUNK
    j = lax.broadcasted_iota(jnp.int32, a.shape, 1)
    a = jnp.where((j >= i) if rev else (j <= i), a, 0.0)
    av = _dot(a.astype(BF16), v.astype(BF16))
    for h in range(4):
        o = o + jnp.where(_lane_group_mask(256, 64, h), av[GLA_CHUNK * h:GLA_CHUNK * (h + 1)], 0.0)
    return o


def fn_gla_kv(blk, k, v, gf, gb):
    uf, ef, ub, eb = [], [], [], []
    for c in range(k.shape[0] // GLA_CHUNK):
        sl = slice(c * GLA_CHUNK, (c + 1) * GLA_CHUNK)
        u, e = _gla_kv_chunk(k[sl], v[sl], gf[sl], False)
        uf.append(u[None]); ef.append(e[None])
        u, e = _gla_kv_chunk(k[sl], v[sl], gb[sl], True)
        ub.append(u[None]); eb.append(e[None])
    cat = lambda t: jnp.concatenate(t, axis=0)
    return cat(uf), cat(ef), cat(ub), cat(eb)


def fn_gla_o(blk, q, k, v, gf, gb, gr, sf, sb, nwt):
    out = []
    for c in range(q.shape[0] // GLA_CHUNK):
        sl = slice(c * GLA_CHUNK, (c + 1) * GLA_CHUNK)
        out.append(_gla_o_chunk(q[sl], k[sl], v[sl], gf[sl], sf[c], False)
                   + _gla_o_chunk(q[sl], k[sl], v[sl], gb[sl], sb[c], True))
    o = jnp.concatenate(out, axis=0)
    r = lax.broadcasted_iota(jnp.int32, (256, 256), 0)
    c = lax.broadcasted_iota(jnp.int32, (256, 256), 1)
    head_mean = jnp.where((r // 64) == (c // 64), 1.0 / 64.0, 0.0).astype(F32)
    ms = _dot(o * o, head_mean, precision=HI)
    on = o * lax.rsqrt(ms + EPS) * nwt
    return (on * (gr * _sigmoid(gr)),)


def _rope_partner(x):
    lane = lax.broadcasted_iota(jnp.int32, x.shape, 1)
    return jnp.where((lane // 16) % 2 == 0, pltpu.roll(x, LANES - 16, 1), pltpu.roll(x, 16, 1))


@jax.custom_vjp
def _rope(x, cs, sn):
    return x * cs + _rope_partner(x) * sn


def _rope_fwd(x, cs, sn):
    return _rope(x, cs, sn), (cs, sn)


def _rope_bwd(res, dy):
    cs, sn = res
    return dy * cs + _rope_partner(dy * sn), jnp.zeros_like(cs), jnp.zeros_like(sn)


_rope.defvjp(_rope_fwd, _rope_bwd)


def fn_mla_pre(blk, ckv, dq, kvw, qw):
    return _rms(ckv, kvw), _rms(dq, qw)


def fn_mla_post(blk, kk, qu, kr, cs, sn):
    kro = _rope(kr, cs, sn)
    kcat, q = [], []
    for h in range(4):
        kcat += [kk[:, 128 * h:128 * (h + 1)].astype(F32), kro]
        q += [qu[:, 256 * h:256 * h + 128], _rope(qu[:, 256 * h + 128:256 * (h + 1)], cs, sn)]
    return jnp.concatenate(kcat, axis=1), jnp.concatenate(q, axis=1) * SCORE_SCALE


ATTN_ROWS = 256
NEG = -1e30


def _scores(q, k, k0, context_queries):
    s = _dot(q, k, ((1,), (1,)))
    if context_queries is not None:
        col = k0 + lax.broadcasted_iota(jnp.int32, s.shape, 1)
        s = jnp.where(context_queries & (col >= CTX), NEG, s)
    return s


def flash_fwd(q, kcat, kvu, side=()):
    t = q.shape[0]
    tq = _pick(t, (768, 512, 256))
    tk = _pick(t, (2816, 1536, 768, 512, 256))
    nsub = tq // ATTN_ROWS
    n_side_in, _, _, side_shapes = _exchange_shapes(side)
    n_side = len(side)

    def body(q_ref, k_ref, v_ref, *rest):
        side_in, rest = rest[:n_side_in], rest[n_side_in:]
        o_ref, lse_ref = rest[:2]
        side_out, (m_sc, l_sc, acc_sc), side_sems = rest[2:2 + n_side], rest[2 + n_side:5 + n_side], rest[5 + n_side:]
        h, qi, ki = pl.program_id(0), pl.program_id(1), pl.program_id(2)
        if side:
            starts, forwards, finals = _exchange_phases(side, side_in, side_out, *side_sems)
            at_tile0 = (qi == 0) & (ki == 0)
            last = (h == pl.num_programs(0) - 1) & (qi == pl.num_programs(1) - 1) & (ki == pl.num_programs(2) - 1)
            for when, phase in (((h == 0) & at_tile0, starts), ((h == 2) & at_tile0, forwards)):
                @pl.when(when)
                def _(phase=phase):
                    for run in phase:
                        run()

        @pl.when(ki == 0)
        def _():
            m_sc[...] = jnp.full_like(m_sc, NEG)
            l_sc[...] = jnp.zeros_like(l_sc)
            acc_sc[...] = jnp.zeros_like(acc_sc)

        k, v = k_ref[...], v_ref[...]
        chains = [pl.ds(r * ATTN_ROWS, ATTN_ROWS) for r in range(nsub)]
        scores = [_scores(q_ref[rs, :], k, ki * tk, (qi == 0) if r == 0 else None) for r, rs in enumerate(chains)]
        probs = []
        for rs, s in zip(chains, scores):
            m_old = m_sc[rs, :]
            m_new = jnp.maximum(m_old, jnp.max(s, axis=-1, keepdims=True))
            alpha = jnp.exp2(m_old - m_new)
            p = jnp.exp2(s - m_new)
            l_sc[rs, :] = alpha * l_sc[rs, :] + jnp.sum(p, axis=-1, keepdims=True)
            m_sc[rs, :] = m_new
            probs.append((alpha, p.astype(BF16)))
        for rs, (alpha, p) in zip(chains, probs):
            acc_sc[rs, :] = alpha * acc_sc[rs, :] + _dot(p, v)

        @pl.when(ki == pl.num_programs(2) - 1)
        def _():
            o_ref[...] = acc_sc[...] / l_sc[...]
            lse_ref[...] = jnp.broadcast_to(m_sc[...] + jnp.log2(l_sc[...]), lse_ref.shape)

        if side:
            @pl.when(last)
            def _():
                for run in finals:
                    run()

    any_spec = pl.BlockSpec(memory_space=pl.ANY)
    res = pl.pallas_call(
        body, name="mla_flash_fwd", grid=(4, t // tq, t // tk),
        in_specs=[pl.BlockSpec((tq, 256), lambda h, i, j: (i, h)), pl.BlockSpec((tk, 256), lambda h, i, j: (j, h)),
                  pl.BlockSpec((tk, 128), lambda h, i, j: (j, 4 + h))] + [any_spec] * n_side_in,
        out_specs=[pl.BlockSpec((tq, 128), lambda h, i, j: (i, h)), pl.BlockSpec((tq, 128), lambda h, i, j: (i, h))]
        + [any_spec] * n_side,
        out_shape=[jax.ShapeDtypeStruct((t, 512), F32), jax.ShapeDtypeStruct((t, 512), F32)] + side_shapes,
        scratch_shapes=[pltpu.VMEM((tq, 1), F32), pltpu.VMEM((tq, 1), F32), pltpu.VMEM((tq, 128), F32)]
        + (_exchange_sems(side) if side else []),
        compiler_params=_params(("arbitrary", "arbitrary", "arbitrary")),
    )(q, kcat, kvu, *[a for _, _, arrs in side for a in arrs])
    return res[0], res[1], list(res[2:])


def fn_attn_stats(blk, do, o, lse):
    out = []
    for h in range(4):
        hs = slice(128 * h, 128 * (h + 1))
        d = jnp.sum(do[:, hs] * o[:, hs], axis=-1, keepdims=True)
        out.append(lse[:, hs].T[0:8])
        out.append(jnp.broadcast_to(d, (do.shape[0], 128)).T[0:8])
    return (jnp.concatenate(out, axis=0)[None],)


def flash_bwd(q, kcat, kvu, dy, stats, side=None):
    t = q.shape[0]
    tq = _pick(t, (2816, 768, 512, 256))
    tk = _pick(t, (768, 512, 256))
    nst = tq // TM
    sd = side or Side(())

    def body(q_ref, k_ref, v_ref, do_ref, st_ref, *rest):
        dq_ref, dk_ref, dv_ref = rest[sd.n_in:sd.n_in + 3]
        finish = sd.start(rest[:sd.n_in], rest[sd.n_in + 3:sd.n_in + 3 + sd.n_out], rest[sd.n_in + 3 + sd.n_out:])
        kj, qi = pl.program_id(1), pl.program_id(2)

        @pl.when(qi == 0)
        def _():
            dk_ref[...] = jnp.zeros_like(dk_ref)
            dv_ref[...] = jnp.zeros_like(dv_ref)

        q_, k, v, do = q_ref[...], k_ref[...], v_ref[...], do_ref[...].astype(BF16)
        lse_row = jnp.concatenate([st_ref[u, 0:1, :] for u in range(nst)], axis=1)
        delta_row = jnp.concatenate([st_ref[u, 8:9, :] for u in range(nst)], axis=1)
        s = _dot(k, q_, ((1,), (1,)))
        key = kj * tk + lax.broadcasted_iota(jnp.int32, s.shape, 0)
        qry = qi * tq + lax.broadcasted_iota(jnp.int32, s.shape, 1)
        s = jnp.where((qry < CTX) & (key >= CTX), NEG, s)
        p = jnp.exp2(s - lse_row)
        dp = _dot(v, do, ((1,), (1,)))
        ds = (p * (dp - delta_row)).astype(BF16)
        dv_ref[...] += _dot(p.astype(BF16), do)
        dk_ref[...] += LN2 * _dot(ds, q_)
        dq_new = LN2 * _dot(ds, k, ((0,), (0,)))
        rows_ = pl.ds(pl.multiple_of(qi * tq, TM), tq)

        @pl.when(kj == 0)
        def _():
            dq_ref[rows_, :] = dq_new

        @pl.when(kj != 0)
        def _():
            dq_ref[rows_, :] += dq_new

        finish()

    res = pl.pallas_call(
        body, name="mla_flash_bwd", grid=(4, t // tk, t // tq),
        in_specs=[pl.BlockSpec((tq, 256), lambda h, j, i: (i, h)), pl.BlockSpec((tk, 256), lambda h, j, i: (j, h)),
                  pl.BlockSpec((tk, 128), lambda h, j, i: (j, 4 + h)), pl.BlockSpec((tq, 128), lambda h, j, i: (i, 4 + h)),
                  pl.BlockSpec((nst, 16, 256), lambda h, j, i: (i, h, 0))] + sd.in_specs,
        out_specs=[pl.BlockSpec((t, 256), lambda h, j, i: (0, h)), pl.BlockSpec((tk, 256), lambda h, j, i: (j, h)),
                   pl.BlockSpec((tk, 128), lambda h, j, i: (j, h))] + sd.out_specs,
        out_shape=[jax.ShapeDtypeStruct((t, 1024), F32), jax.ShapeDtypeStruct((t, 1024), F32),
                   jax.ShapeDtypeStruct((t, 512), F32)] + sd.shapes,
        scratch_shapes=sd.sems,
        compiler_params=_params(("arbitrary", "arbitrary", "arbitrary")),
    )(q, kcat, kvu, dy, stats, *sd.arrays)
    return res[0], res[1], res[2], list(res[3:])


SCAN_BLOCK = CTX // GLA_CHUNK


def _scan_block(t, nb, rev):
    if not rev:
        return t
    return jnp.where(t < 1, 0, nb - t)


def _scan_order(rev):
    return tuple(reversed(range(SCAN_BLOCK))) if rev else tuple(range(SCAN_BLOCK))


def _both_halves(e):
    return jnp.concatenate([e, e], axis=1)


def gla_states(uf, ef, ub, eb):
    nb = uf.shape[0] // SCAN_BLOCK

    def body(uf_ref, ef_ref, ub_ref, eb_ref, sf_ref, sb_ref, sf_sc, sb_sc):
        @pl.when(pl.program_id(0) == 0)
        def _():
            sf_sc[...] = jnp.zeros_like(sf_sc)
            sb_sc[...] = jnp.zeros_like(sb_sc)

        for u_ref, e_ref, s_ref, sc, rev in ((uf_ref, ef_ref, sf_ref, sf_sc, False), (ub_ref, eb_ref, sb_ref, sb_sc, True)):
            s = sc[...]
            for c in _scan_order(rev):
                s_ref[c] = s
                s = _both_halves(e_ref[c]) * s + u_ref[c]
            sc[...] = s

    big = lambda rev: pl.BlockSpec((SCAN_BLOCK, 128, 256), lambda t: (_scan_block(t, nb, rev), 0, 0))
    small = lambda rev: pl.BlockSpec((SCAN_BLOCK, 128, 128), lambda t: (_scan_block(t, nb, rev), 0, 0))
    return pl.pallas_call(
        body, name="gla_states", grid=(nb,),
        in_specs=[big(False), small(False), big(True), small(True)],
        out_specs=[big(False), big(True)],
        out_shape=[jax.ShapeDtypeStruct(uf.shape, F32)] * 2,
        scratch_shapes=[pltpu.VMEM((128, 256), F32)] * 2,
        compiler_params=_params(("arbitrary",)),
    )(uf, ef, ub, eb)


def gla_states_bwd(ef, eb, sf, sb, dsf, dsb):
    nb = ef.shape[0] // SCAN_BLOCK

    def body(ef_ref, eb_ref, sf_ref, sb_ref, dsf_ref, dsb_ref, duf_ref, def_ref, dub_ref, deb_ref, gf_sc, gb_sc):
        @pl.when(pl.program_id(0) == 0)
        def _():
            gf_sc[...] = jnp.zeros_like(gf_sc)
            gb_sc[...] = jnp.zeros_like(gb_sc)

        for e_ref, s_ref, ds_ref, du_ref, de_ref, g_sc, rev in ((ef_ref, sf_ref, dsf_ref, duf_ref, def_ref, gf_sc, False),
                                                                 (eb_ref, sb_ref, dsb_ref, dub_ref, deb_ref, gb_sc, True)):
            g = g_sc[...]
            for k in reversed(_scan_order(rev)):
                du_ref[k] = g
                gs = g * s_ref[k]
                de_ref[k] = gs[:, :128] + gs[:, 128:]
                g = _both_halves(e_ref[k]) * g + ds_ref[k]
            g_sc[...] = g

    big = lambda rev: pl.BlockSpec((SCAN_BLOCK, 128, 256), lambda t: (_scan_block(nb - 1 - t, nb, rev), 0, 0))
    small = lambda rev: pl.BlockSpec((SCAN_BLOCK, 128, 128), lambda t: (_scan_block(nb - 1 - t, nb, rev), 0, 0))
    return pl.pallas_call(
        body, name="gla_states_bwd", grid=(nb,),
        in_specs=[small(False), small(True), big(False), big(True), big(False), big(True)],
        out_specs=[big(False), small(False), big(True), small(True)],
        out_shape=[jax.ShapeDtypeStruct(sf.shape, F32), jax.ShapeDtypeStruct(ef.shape, F32)] * 2,
        scratch_shapes=[pltpu.VMEM((128, 256), F32)] * 2,
        compiler_params=_params(("arbitrary",)),
    )(ef, eb, sf, sb, dsf, dsb)


def loss_head(xt, target, fnw):
    t = xt.shape[0]

    def f(x, tg, w):
        y = _rms(x, w)
        return 0.5 * jnp.sum(jnp.square(y - tg)) * (1.0 / D)

    def body(x_ref, t_ref, w_ref, loss_ref, dx_ref, dw_ref):
        i = pl.program_id(0)

        @pl.when(i == 0)
        def _():
            loss_ref[...] = jnp.zeros_like(loss_ref)
            dw_ref[...] = jnp.zeros_like(dw_ref)

        @pl.when(i < NCTXB)
        def _():
            dx_ref[...] = jnp.zeros_like(dx_ref)

        @pl.when(i >= NCTXB)
        def _():
            val, (dx, dw) = jax.value_and_grad(f, argnums=(0, 2))(x_ref[...], t_ref[...], w_ref[...])
            loss_ref[...] += jnp.broadcast_to(val, loss_ref.shape)
            dx_ref[...] = dx
            dw_ref[...] += dw

    return pl.pallas_call(
        body, name="loss_head", grid=(t // TM,),
        in_specs=[pl.BlockSpec((TM, D), lambda i: (i, 0)), pl.BlockSpec((TM, D), lambda i: (jnp.maximum(i - NCTXB, 0), 0)),
                  pl.BlockSpec((1, D), lambda i: (0, 0))],
        out_specs=[pl.BlockSpec((1, 128), lambda i: (0, 0)), pl.BlockSpec((TM, D), lambda i: (i, 0)),
                   pl.BlockSpec((1, D), lambda i: (0, 0))],
        out_shape=[jax.ShapeDtypeStruct((1, 128), F32), jax.ShapeDtypeStruct((t, D), F32), jax.ShapeDtypeStruct((1, D), F32)],
        compiler_params=_params(("arbitrary",)),
    )(xt, target, fnw)


def _in_to_padded(w):
    out, pos = [], 0
    for src, wd, dst in sorted(IN_GROUPS, key=lambda g: g[2]):
        if dst > pos:
            out.append(jnp.zeros((w.shape[0], dst - pos), w.dtype))
        out.append(w[:, src:src + wd])
        pos = dst + wd
    if pos < P_COLS:
        out.append(jnp.zeros((w.shape[0], P_COLS - pos), w.dtype))
    return jnp.concatenate(out, axis=1)


def _in_from_padded(g):
    return jnp.concatenate([g[:, dst:dst + wd] for _, wd, dst in IN_GROUPS], axis=1)


def _uq_to_padded(w):
    return jnp.pad(w.reshape(256, 4, 192), ((0, 0), (0, 0), (0, 64))).reshape(256, 1024)


def _uq_from_padded(g):
    return g.reshape(256, 4, 256)[:, :, :192].reshape(256, 768)


def _ukv_to_padded(w):
    return w.reshape(256, 4, 2, 128).transpose(0, 2, 1, 3).reshape(256, 1024)


def _ukv_from_padded(g):
    return g.reshape(256, 2, 4, 128).transpose(0, 2, 1, 3).reshape(256, 1024)


def _rope_tables(n):
    freq = ROPE_BASE ** (-jnp.arange(16, dtype=F32) * 2.0 / 32.0)
    grid_h = n // GRID_W
    ar = jnp.repeat(jnp.arange(grid_h, dtype=F32)[:, None] * freq[None, :], GRID_W, axis=0)
    ac = jnp.tile(jnp.arange(GRID_W, dtype=F32)[:, None] * freq[None, :], (grid_h, 1))
    z = jnp.zeros((n, 64), F32)
    cs = jnp.concatenate([jnp.cos(ar), jnp.cos(ar), jnp.cos(ac), jnp.cos(ac), z], axis=1)
    sn = jnp.concatenate([-jnp.sin(ar), jnp.sin(ar), -jnp.sin(ac), jnp.sin(ac), z], axis=1)
    cs_c = jnp.concatenate([jnp.ones((CTX, 64), F32), jnp.zeros((CTX, 64), F32)], axis=1)
    return jnp.concatenate([cs_c, cs], axis=0), jnp.concatenate([jnp.zeros((CTX, 128), F32), sn], axis=0)


def _small_views(sp):
    wg = jnp.concatenate([jnp.pad(sp["gla_wg_fwd"], ((0, 112), (0, 0))), jnp.pad(sp["gla_wg_bwd"], ((16, 96), (0, 0)))], axis=1)
    return dict(
        n1w=sp["norm1_w"][None], n2w=sp["norm2_w"][None],
        sgu_nw=sp["sgu_norm_w"][None], sgu_nb=sp["sgu_norm_b"][None], sgu_w=sp["sgu_w"],
        sgu_bm=jnp.repeat(sp["sgu_b"].T, 64, axis=1),
        wg=wg, bg=jnp.concatenate([sp["gla_bg_fwd"], sp["gla_bg_bwd"]])[None],
        gla_nwt=jnp.tile(sp["gla_norm_w"], 4)[None],
        kvw=sp["mla_kv_norm_w"][None], qw=sp["mla_q_norm_w"][None])


def _small_grads(g):
    return dict(
        norm1_w=g["n1w"][0], norm2_w=g["n2w"][0],
        sgu_norm_w=g["sgu_nw"][0], sgu_norm_b=g["sgu_nb"][0], sgu_w=g["sgu_w"],
        sgu_b=g["sgu_bm"].reshape(128, 4, 64).sum(-1).T,
        gla_wg_fwd=g["wg"][0:16, 0:128], gla_wg_bwd=g["wg"][16:32, 128:256],
        gla_bg_fwd=g["bg"][0, 0:128], gla_bg_bwd=g["bg"][0, 128:256],
        gla_norm_w=g["gla_nwt"].reshape(4, 64).sum(0),
        mla_kv_norm_w=g["kvw"][0], mla_q_norm_w=g["qw"][0])


def _big_views(full):
    views = {}
    if "w_in" in full:
        views.update(win=_in_to_padded(full["w_in"]), wuq=_uq_to_padded(full["mla_w_uq"]),
                     wukv=_ukv_to_padded(full["mla_w_ukv"]), wout=full["w_out"])
    if "w_ff1" in full:
        views.update(w1=full["w_ff1"], w2=full["w_ff2"])
    return views


def _gla_tile(t):
    return _pick(t, (768, 512, 256))


def _layer_ops(p, sv, a):
    pc = lambda off, w, tm=TM: rows(p, w, off // w, tm=tm)
    gt = _gla_tile(p.shape[0])
    gr = lambda arr: rows(arr, tm=gt)
    return dict(
        sgu=[pc(P_SU, 256), pc(P_SV, 256), const(sv["sgu_nw"]), const(sv["sgu_nb"]), const(sv["sgu_w"]), const(sv["sgu_bm"])],
        gates=[pc(P_GATE, 128), const(sv["wg"]), const(sv["bg"])],
        mla_pre=[pc(P_CKV, 256), pc(P_DQ, 256), const(sv["kvw"]), const(sv["qw"])],
        gla_kv=lambda: [pc(P_GK, 128, gt), pc(P_GV, 256, gt), gr(a["gf"]), gr(a["gb"])],
        gla_o=lambda: [pc(P_GQ, 128, gt), pc(P_GK, 128, gt), pc(P_GV, 256, gt), gr(a["gf"]), gr(a["gb"]), pc(P_GR, 256, gt),
                       chunks(a["sf"], gt // GLA_CHUNK), chunks(a["sb"], gt // GLA_CHUNK), const(sv["gla_nwt"])],
        mla_post=lambda: [rows(a["kvu"], 512, 0), rows(a["qu"]), pc(P_KR, 128), rows(a["cs"]), rows(a["sn"])])


def layer_fwd(l, xt, modl, bw, sv, tabs, side=(), late=None):
    t = xt.shape[0]
    g, nc, gt = t // TM, t // GLA_CHUNK, _gla_tile(t)
    gg, cpt = t // gt, gt // GLA_CHUNK
    nm = lambda s: f"l{l}_{s}"
    a = dict(x=xt, cs=tabs[0], sn=tabs[1])
    a["h"], = rw(nm("norm1"), fn_norm1, [rows(xt), const(modl), const(sv["n1w"])], [rowout(t, D, BF16)], g)
    p = a["p"] = mm(nm("in_proj"), a["h"], bw["win"], F32)
    ops = _layer_ops(p, sv, a)
    y_sgu, = rw(nm("sgu"), fn_sgu, ops["sgu"], [rowout(t, 256, BF16)], g)
    a["gf"], a["gb"] = rw(nm("gates"), fn_gates, ops["gates"], [rowout(t, 128, F32)] * 2, g)
    a["uf"], a["ef"], a["ub"], a["eb"] = rw(nm("gla_kv"), fn_gla_kv, ops["gla_kv"](),
                                           [chunkout((nc, 128, 256), F32, cpt), chunkout((nc, 128, 128), F32, cpt)] * 2, gg)
    a["sf"], a["sb"] = gla_states(a["uf"], a["ef"], a["ub"], a["eb"])
    y_gla, = rw(nm("gla_o"), fn_gla_o, ops["gla_o"](), [rowout(t, 256, BF16, tm=gt)], gg)
    a["ckvn"], a["dqn"] = rw(nm("mla_pre"), fn_mla_pre, ops["mla_pre"], [rowout(t, 256, BF16)] * 2, g)
    a["kvu"] = mm(nm("kv_up"), a["ckvn"], bw["wukv"], BF16)
    a["qu"] = mm(nm("q_up"), a["dqn"], bw["wuq"], F32)
    a["kcat"], a["q"] = rw(nm("mla_post"), fn_mla_post, ops["mla_post"](), [rowout(t, 1024, BF16)] * 2, g)
    a["o"], a["lse"], side_out = flash_fwd(a["q"], a["kcat"], a["kvu"], side)
    if late is not None:
        late(side_out)
    a["y"] = jnp.concatenate([y_sgu, y_gla, a["o"].astype(BF16)], axis=1)
    a["yo"] = mm(nm("out_proj"), a["y"], bw["wout"], F32)
    a["x1"], a["h2"] = rw(nm("res_norm2"), fn_res_norm2, [rows(xt), rows(a["yo"]), const(modl), const(sv["n2w"])],
                          [rowout(t, D, F32), rowout(t, D, BF16)], g)
    a["act"] = mm(nm("ff1"), a["h2"], bw["w1"], BF16, post=lambda acc: jnp.maximum(acc, 0.0))
    a["f"] = mm(nm("ff2"), a["act"], bw["w2"], F32, pre=_square_bf16)
    x2, = rw(nm("res2"), fn_res2, [rows(a["x1"]), rows(a["f"]), const(modl)], [rowout(t, D, F32)], g)
    return x2, a


def fn_assemble(blk, gv1, gv2, ckv, su, sv_, gr, dq, gk1, gk2, pg, kr, gq):
    return (jnp.concatenate([gv1 + gv2, ckv, su, sv_, gr, dq, gk1 + gk2, pg, kr, gq], axis=1),)


def layer_bwd(l, dx2, a, modl, bw, sv, side_a=None, make_side_b=None, ff_side=None):
    t = dx2.shape[0]
    g, gt = t // TM, _gla_tile(t)
    gg, cpt = t // gt, gt // GLA_CHUNK
    nm = lambda s: f"l{l}_{s}_bwd"
    p = a["p"]
    ops = _layer_ops(p, sv, a)
    gw, gs = {}, {}
    df, dm_a = rw_vjp(nm("res2"), fn_res2, [rows(a["x1"]), rows(a["f"]), const(modl)], [rows(dx2)], [1, 2], g,
                      gdt=[BF16, F32])
    during_attention = []
    if side_a:
        gw["w2"], got_a = mm_tn(nm("ff2_w"), a["act"], df, pre=_square_bf16, side=Side(side_a))
        during_attention += make_side_b(got_a)
    else:
        gw["w2"] = mm_tn(nm("ff2_w"), a["act"], df, pre=_square_bf16)
    du = mm(nm("ff2_x"), df, bw["w2"], BF16, post=lambda acc, act: acc * (2.0 * act.astype(F32)), extras=(a["act"],), bt=True)
    gw["w1"] = mm_tn(nm("ff1_w"), a["h2"], du)
    dh2 = mm(nm("ff1_x"), du, bw["w1"], F32, bt=True)
    dxa, dyo, dm_b, gs["n2w"] = rw_vjp(nm("res_norm2"), fn_res_norm2,
                                       [rows(a["x"]), rows(a["yo"]), const(modl), const(sv["n2w"])],
                                       [rows(dx2), rows(dh2)], [0, 1, 2, 3], g, gdt=[F32, BF16, F32, F32])
    if ff_side is not None:
        ff_entries, make_ff_next = ff_side(dict(w_ff1=gw["w1"], w_ff2=gw["w2"]))
        gw["wout"], got_ff = mm_tn(nm("out_w"), a["y"], dyo, side=Side(ff_entries))
        during_attention += make_ff_next(got_ff)
    else:
        gw["wout"] = mm_tn(nm("out_w"), a["y"], dyo)
    dy = mm(nm("out_x"), dyo, bw["wout"], F32, bt=True)
    dsu, dsv, gs["sgu_nw"], gs["sgu_nb"], gs["sgu_w"], gs["sgu_bm"] = rw_vjp(
        nm("sgu"), fn_sgu, ops["sgu"], [rows(dy, 256, 0)], [0, 1, 2, 3, 4, 5], g)
    dgq, dgk1, dgv1, dgf1, dgb1, dgr, dsf, dsb, gs["gla_nwt"] = rw_vjp(
        nm("gla_o"), fn_gla_o, ops["gla_o"](), [rows(dy, 256, 1, tm=gt)], list(range(9)), gg)
    duf, def_, dub, deb = gla_states_bwd(a["ef"], a["eb"], a["sf"], a["sb"], dsf, dsb)
    dgk2, dgv2, dgf, dgb = rw_vjp(nm("gla_kv"), fn_gla_kv, ops["gla_kv"](),
                                  [chunks(duf, cpt), chunks(def_, cpt), chunks(dub, cpt), chunks(deb, cpt)], [0, 1, 2, 3], gg,
                                  adds={2: rows(dgf1, tm=gt), 3: rows(dgb1, tm=gt)})
    dpg, gs["wg"], gs["bg"] = rw_vjp(nm("gates"), fn_gates, ops["gates"], [rows(dgf), rows(dgb)], [0, 1, 2], g)
    stats, = rw(nm("attn_stats"), fn_attn_stats, [rows(dy, 512, 1), rows(a["o"]), rows(a["lse"])],
                [chunkout((g, 64, TM), F32, 1)], g)
    dq, dkcat, dv, got_b = flash_bwd(a["q"], a["kcat"], a["kvu"], dy, stats,
                                     side=Side(during_attention) if during_attention else None)
    dkk, dqu, dkr = rw_vjp(nm("mla_post"), fn_mla_post, ops["mla_post"](), [rows(dkcat), rows(dq)], [0, 1, 2], g,
                           gdt=[BF16, BF16, F32])
    dkvu = jnp.concatenate([dkk, dv.astype(BF16)], axis=1)
    gw["wukv"] = mm_tn(nm("kv_up_w"), a["ckvn"], dkvu)
    gw["wuq"] = mm_tn(nm("q_up_w"), a["dqn"], dqu)
    dckvn = mm(nm("kv_up_x"), dkvu, bw["wukv"], F32, bt=True)
    ddqn = mm(nm("q_up_x"), dqu, bw["wuq"], F32, bt=True)
    dckv, ddq, gs["kvw"], gs["qw"] = rw_vjp(nm("mla_pre"), fn_mla_pre, ops["mla_pre"], [rows(dckvn), rows(ddqn)],
                                            [0, 1, 2, 3], g)
    dp, = rw(nm("assemble"), fn_assemble,
             [rows(x_) for x_ in (dgv1, dgv2, dckv, dsu, dsv, dgr, ddq, dgk1, dgk2, dpg, dkr, dgq)],
             [rowout(t, P_COLS, BF16)], g)
    gw["win"] = mm_tn(nm("in_w"), a["h"], dp)
    dh = mm(nm("in_x"), dp, bw["win"], F32, bt=True)
    dx, dm_c, gs["n1w"] = rw_vjp(nm("norm1"), fn_norm1, [rows(a["x"]), const(modl), const(sv["n1w"])], [rows(dh)],
                                 [0, 1, 2], g, adds={0: rows(dxa)})
    big = dict(w_in=_in_from_padded(gw["win"]), w_out=gw["wout"], mla_w_uq=_uq_from_padded(gw["wuq"]),
               mla_w_ukv=_ukv_from_padded(gw["wukv"]), w_ff1=gw["w1"], w_ff2=gw["w2"])
    return dx, dm_a + dm_b + dm_c, big, _small_grads(gs), got_b


SMALL_NAMES = ("norm1_w", "sgu_norm_w", "sgu_norm_b", "sgu_w", "sgu_b", "gla_wg_fwd", "gla_bg_fwd", "gla_wg_bwd",
               "gla_bg_bwd", "gla_norm_w", "mla_q_norm_w", "mla_kv_norm_w", "norm2_w")
BIG_NAMES = ("w_in", "w_out", "mla_w_uq", "mla_w_ukv", "w_ff1", "w_ff2")
ATTN_WEIGHTS, FF_WEIGHTS = BIG_NAMES[:4], BIG_NAMES[4:]


def local_step(x, ctx, target, mods, big, small, final_norm_w, side=(), on_side=None, grad_side=None, ff_side=None):
    n = x.shape[0]
    xt = jnp.concatenate([ctx, x], axis=0)
    tabs = _rope_tables(n)
    depth = len(mods)
    big = list(big)
    svs = [_small_views(small[l]) for l in range(depth)]
    acts, bws = [], []
    for l in range(depth):
        bws.append(_big_views(big[l]))
        late = None
        if l == 0 and on_side is not None:
            def late(results):
                rest0, later = on_side(results)
                bws[0].update(_big_views(rest0))
                big.extend(later)
        xt, a = layer_fwd(l, xt, mods[l], bws[l], svs[l], tabs, side if l == 0 else (), late)
        acts.append(a)
    loss, dxt, dfnw = loss_head(xt, target, final_norm_w[None])
    dmods, gbig, gsmall = [None] * depth, [None] * depth, [None] * depth
    got = []
    for l in reversed(range(depth)):
        hooks = (None, None, None)
        if l == 0 and grad_side is not None and depth > 1:
            hooks = (*grad_side(gbig[1:]), ff_side)
        dxt, dmods[l], gbig[l], gsmall[l], got = layer_bwd(l, dxt, acts[l], mods[l], bws[l], svs[l], *hooks)
    return loss, dxt[CTX:], dmods, gbig, gsmall, dfnw, got


def _group(group):
    x, y, c = lax.axis_index("x"), lax.axis_index("y"), lax.axis_index("c")
    if group == "sib":
        return 2, c, [((x, y, 1 - c), 1 - c)]
    if group == "chip":
        flips = [(1, 0), (0, 1), (1, 1)]
        return 4, 2 * x + y, [((x ^ fx, y ^ fy, c), 2 * (x ^ fx) + (y ^ fy)) for fx, fy in flips]
    flips = [(fx, fy, fc) for fx in (0, 1) for fy in (0, 1) for fc in (0, 1)][1:]
    return 8, 4 * x + 2 * y + c, [((x ^ fx, y ^ fy, c ^ fc), 4 * (x ^ fx) + 2 * (y ^ fy) + (c ^ fc)) for fx, fy, fc in flips]


def _group_size(group):
    return {"sib": 2, "chip": 4, "all": 8}[group]


REMOTE_COPIES = {"gather": None, "scatter": None, "swap": 1, "gather2": 6}


def _exchange_shapes(entries):
    n_in = sum(len(arrs) for _, _, arrs in entries)
    n_remote = sum(REMOTE_COPIES[k] or _group_size(g) - 1 for k, g, _ in entries)
    n_local = sum(1 for k, _, _ in entries if k != "swap")
    out_shape = []
    for kind, group, arrs in entries:
        a = arrs[0]
        if kind in ("gather", "gather2"):
            out_shape.append(jax.ShapeDtypeStruct((_group_size(group),) + a.shape, a.dtype))
        elif kind == "swap" and len(arrs) == 1:
            out_shape.append(jax.ShapeDtypeStruct(a.shape[1:], a.dtype))
        else:
            out_shape.append(jax.ShapeDtypeStruct(a.shape, a.dtype))
    return n_in, n_remote, n_local, out_shape


def _exchange_sems(entries):
    _, n_remote, n_local, _ = _exchange_shapes(entries)
    return [pltpu.SemaphoreType.DMA((n_remote,)), pltpu.SemaphoreType.DMA((n_remote,)), pltpu.SemaphoreType.DMA((max(n_local, 1),))]


def _exchange_phases(entries, in_refs, out_refs, send_sems, recv_sems, local_sems):
    x, y, c = lax.axis_index("x"), lax.axis_index("y"), lax.axis_index("c")

    def remote(src, dst, k, dev):
        return pltpu.make_async_remote_copy(src_ref=src, dst_ref=dst, send_sem=send_sems.at[k], recv_sem=recv_sems.at[k],
                                            device_id=dev, device_id_type=MESH)

    pos, k, kl = 0, 0, 0
    starts, forwards, finals = [], [], []
    for (kind, group, arrs), out in zip(entries, out_refs):
        srcs = in_refs[pos:pos + len(arrs)]
        pos += len(arrs)
        _, mine, peers = _group(group)
        if kind == "swap":
            (dev, _), = peers
            if len(srcs) == 1:
                starts.append(remote(srcs[0].at[1 - c], out, k, dev).start)
                finals.append(remote(srcs[0].at[0], out, k, dev).wait)
            else:
                def start_swap(srcs=srcs, k=k, dev=dev, out=out):
                    for core, src in ((0, srcs[1]), (1, srcs[0])):
                        @pl.when(c == core)
                        def _(src=src):
                            remote(src, out, k, dev).start()

                starts.append(start_swap)
                finals.append(remote(srcs[0], out, k, dev).wait)
            k += 1
            continue
        src = srcs[0]
        own = pltpu.make_async_copy(src if kind != "scatter" else src.at[mine], out.at[mine], local_sems.at[kl])
        starts.append(own.start)
        finals.append(own.wait)
        kl += 1
        if kind == "gather2":
            sibling = (x, y, 1 - c)
            for f, (dev, slot) in enumerate(peers):
                starts.append(remote(src.at[c], out.at[mine, c], k + f, dev).start)
                arrival = remote(src.at[c], out.at[slot, c], k + f, dev)

                def forward(arrival=arrival, slot=slot, kf=k + 3 + f, out=out):
                    arrival.wait_recv()
                    remote(out.at[slot, c], out.at[slot, c], kf, sibling).start()

                forwards.append(forward)
                finals.append(arrival.wait_send)
                finals.append(remote(out.at[slot, c], out.at[slot, 1 - c], k + 3 + f, sibling).wait)
            k += 6
            continue
        for dev, slot in peers:
            piece = src if kind == "gather" else src.at[slot]
            starts.append(remote(piece, out.at[mine], k, dev).start)
            finals.append(remote(piece, out.at[slot], k, dev).wait)
            k += 1
    return starts, forwards, finals


class Side:
    def __init__(self, entries):
        self.entries = tuple(entries)
        self.n_in, _, _, self.shapes = _exchange_shapes(self.entries)
        self.n_out = len(self.entries)
        self.arrays = [a for _, _, arrs in self.entries for a in arrs]
        any_spec = pl.BlockSpec(memory_space=pl.ANY)
        self.in_specs, self.out_specs = [any_spec] * self.n_in, [any_spec] * self.n_out
        self.sems = _exchange_sems(self.entries) if self.entries else []

    def start(self, in_refs, out_refs, sem_refs):
        if not self.entries:
            return lambda: None
        ids = [pl.program_id(d) for d in range(3)]
        first = (ids[0] == 0) & (ids[1] == 0) & (ids[2] == 0)
        last = ((ids[0] == pl.num_programs(0) - 1) & (ids[1] == pl.num_programs(1) - 1) & (ids[2] == pl.num_programs(2) - 1))
        starts, forwards, finals = _exchange_phases(self.entries, in_refs, out_refs, *sem_refs)
        assert not forwards

        @pl.when(first)
        def _():
            for run in starts:
                run()

        def finish():
            @pl.when(last)
            def _():
                for run in finals:
                    run()

        return finish


def xchg(name, entries):
    n_in, _, _, out_shape = _exchange_shapes(entries)

    def body(*refs):
        in_refs, out_refs = refs[:n_in], refs[n_in:n_in + len(entries)]
        for phase in _exchange_phases(entries, in_refs, out_refs, *refs[n_in + len(entries):]):
            for run in phase:
                run()

    any_spec = pl.BlockSpec(memory_space=pl.ANY)
    return pl.pallas_call(
        body, name=name,
        in_specs=[any_spec] * n_in, out_specs=[any_spec] * len(entries), out_shape=out_shape,
        scratch_shapes=_exchange_sems(entries),
    )(*[a for _, _, arrs in entries for a in arrs])


def _block_rows(r, c, budget=131072):
    tr = 8
    while tr * 2 * c <= budget and r % (tr * 2) == 0:
        tr *= 2
    return tr if r % tr == 0 else r


def tree_sum(name, parts):
    g, r, c = parts.shape
    tr = _block_rows(r, c)

    def body(p_ref, o_ref):
        p = [p_ref[i].astype(F32) for i in range(g)]
        while len(p) > 1:
            p = [p[i] + p[i + 1] for i in range(0, len(p), 2)]
        o_ref[...] = p[0]

    return pl.pallas_call(
        body, name=name, grid=(r // tr,),
        in_specs=[pl.BlockSpec((g, tr, c), lambda i: (0, i, 0))], out_specs=pl.BlockSpec((tr, c), lambda i: (i, 0)),
        out_shape=jax.ShapeDtypeStruct((r, c), F32), compiler_params=_params(("arbitrary",)),
    )(parts)


def pair_sum(name, halves, recv, core):
    r, c = recv.shape
    tr = _block_rows(r, c)

    def body(h_ref, r_ref, k_ref, o_ref):
        o_ref[...] = (jnp.where(k_ref[...] > 0.5, h_ref[1], h_ref[0]) + r_ref[...]).astype(o_ref.dtype)

    blk = pl.BlockSpec((tr, c), lambda i: (i, 0))
    return pl.pallas_call(
        body, name=name, grid=(r // tr,),
        in_specs=[pl.BlockSpec((2, tr, c), lambda i: (0, i, 0)), blk, pl.BlockSpec((1, 1), lambda i: (0, 0))],
        out_specs=blk, out_shape=jax.ShapeDtypeStruct((r, c), BF16), compiler_params=_params(("arbitrary",)),
    )(halves, recv, core)


def adamw(name, w, g, m, v):
    r, c = w.shape
    tr = _block_rows(r, c)

    def body(w_ref, g_ref, m_ref, v_ref, d_ref, nm_ref, nv_ref):
        gg = g_ref[...]
        nm = ADAM_B1 * m_ref[...] + (1.0 - ADAM_B1) * gg
        nv = ADAM_B2 * v_ref[...] + (1.0 - ADAM_B2) * jnp.square(gg)
        m_hat = nm / (1.0 - ADAM_B1 ** ADAM_STEP)
        v_hat = nv / (1.0 - ADAM_B2 ** ADAM_STEP)
        d_ref[...] = -ADAM_LR * (m_hat / (jnp.sqrt(v_hat) + ADAM_EPS) + ADAM_WD * w_ref[...])
        nm_ref[...] = nm
        nv_ref[...] = nv

    blk = pl.BlockSpec((tr, c), lambda i: (i, 0))
    return pl.pallas_call(
        body, name=name, grid=(r // tr,), in_specs=[blk] * 4, out_specs=[blk] * 3,
        out_shape=[jax.ShapeDtypeStruct((r, c), F32)] * 3, compiler_params=_params(("arbitrary",)),
    )(w, g, m, v)


W_MOD_COLS = 6 * D // 4
MOD_TN = 512


def mod_project(c16, w_mod, b_loc):
    def body(c_ref, w_ref, b_ref, o_ref):
        cv = c_ref[...]
        s = (cv * _sigmoid(cv)).astype(BF16)
        o_ref[0] = _dot(s, w_ref[0].astype(BF16)) + b_ref[0]

    return pl.pallas_call(
        body, name="mod_project", grid=(2, W_MOD_COLS // MOD_TN),
        in_specs=[pl.BlockSpec((16, D), lambda l, j: (0, 0)), pl.BlockSpec((1, D, MOD_TN), lambda l, j: (l, 0, j)),
                  pl.BlockSpec((1, 1, MOD_TN), lambda l, j: (l, 0, j))],
        out_specs=pl.BlockSpec((1, 16, MOD_TN), lambda l, j: (l, 0, j)),
        out_shape=jax.ShapeDtypeStruct((2, 16, W_MOD_COLS), F32), compiler_params=_params(("arbitrary", "arbitrary")),
    )(c16, w_mod, b_loc)


def mod_weight_grad(c16, dm16):
    def body(c_ref, d_ref, o_ref):
        cv = c_ref[...]
        o_ref[0] = _dot(cv * _sigmoid(cv), d_ref[0], ((0,), (0,)), precision=HI)

    return pl.pallas_call(
        body, name="mod_weight_grad", grid=(2, W_MOD_COLS // MOD_TN),
        in_specs=[pl.BlockSpec((16, D), lambda l, j: (0, 0)), pl.BlockSpec((1, 16, MOD_TN), lambda l, j: (l, 0, j))],
        out_specs=pl.BlockSpec((1, D, MOD_TN), lambda l, j: (l, 0, j)),
        out_shape=jax.ShapeDtypeStruct((2, D, W_MOD_COLS), F32), compiler_params=_params(("arbitrary", "arbitrary")),
    )(c16, dm16)


def cctx_partial(dmc, w_mod):
    def body(d_ref, w_ref, o_ref):
        @pl.when(pl.program_id(0) == 0)
        def _():
            o_ref[...] = jnp.zeros_like(o_ref)
        o_ref[...] += _dot(d_ref[0], w_ref[0], ((1,), (1,)), precision=HI)

    return pl.pallas_call(
        body, name="cctx_partial", grid=(2,),
        in_specs=[pl.BlockSpec((1, 8, W_MOD_COLS), lambda l: (l, 0, 0)), pl.BlockSpec((1, D, W_MOD_COLS), lambda l: (l, 0, 0))],
        out_specs=pl.BlockSpec((8, D), lambda l: (0, 0)),
        out_shape=jax.ShapeDtypeStruct((8, D), F32), compiler_params=_params(("arbitrary",)),
    )(dmc, w_mod)


def cctx_grad(parts, c_ctx8):
    def body(p_ref, c_ref, o_ref):
        ds = (p_ref[0] + p_ref[1]) + (p_ref[2] + p_ref[3])
        _, vf = jax.vjp(lambda z: z * _sigmoid(z), c_ref[...])
        o_ref[...] = vf(ds)[0]

    return pl.pallas_call(
        body, name="cctx_grad", out_shape=jax.ShapeDtypeStruct((8, D), F32),
    )(parts, c_ctx8)


ARG_NAMES = ("x", "c", "ctx", "c_ctx", "w_mod", "b_mod", "norm1_w", "w_in", "w_out", "sgu_norm_w", "sgu_norm_b", "sgu_w",
             "sgu_b", "gla_wg_fwd", "gla_bg_fwd", "gla_wg_bwd", "gla_bg_bwd", "gla_norm_w", "mla_q_norm_w", "mla_w_uq",
             "mla_kv_norm_w", "mla_w_ukv", "norm2_w", "w_ff1", "w_ff2", "final_norm_w")
WEIGHT_NAMES = ARG_NAMES[3:]
PACKED = ("c_ctx", "b_mod") + SMALL_NAMES + ("final_norm_w",)
ROW_SHARDED = ("w_out", "w_ff2")
PACK_ROWS = 256


def _pack(vectors):
    flat = jnp.concatenate([v.reshape(-1) for v in vectors])
    n = flat.shape[0]
    total = -(-n // (PACK_ROWS * LANES)) * PACK_ROWS * LANES
    return jnp.pad(flat, (0, total - n)).reshape(-1, LANES)


def _unpack(buf, shapes):
    flat, out, pos = buf.reshape(-1), [], 0
    for shp in shapes:
        n = int(np.prod(shp))
        out.append(flat[pos:pos + n].reshape(shp))
        pos += n
    return out


def _full_weight(name, g):
    if name in ROW_SHARDED:
        return g.reshape(-1, g.shape[-1])
    return g.transpose(1, 0, 2).reshape(g.shape[1], -1)


def _chip_chunks(name, a):
    if name in ROW_SHARDED:
        return a.reshape(4, a.shape[0] // 4, a.shape[1])
    return a.reshape(a.shape[0], 4, a.shape[1] // 4).transpose(1, 0, 2)


def kernel(x, c, ctx, c_ctx, w_mod, b_mod, norm1_w, w_in, w_out, sgu_norm_w, sgu_norm_b, sgu_w, sgu_b, gla_wg_fwd, gla_bg_fwd, gla_wg_bwd, gla_bg_bwd, gla_norm_w, mla_q_norm_w, mla_w_uq, mla_kv_norm_w, mla_w_ukv, norm2_w, w_ff1, w_ff2, final_norm_w, loss_target, m_c_ctx, m_w_mod, m_b_mod, m_norm1_w, m_w_in, m_w_out, m_sgu_norm_w, m_sgu_norm_b, m_sgu_w, m_sgu_b, m_gla_wg_fwd, m_gla_bg_fwd, m_gla_wg_bwd, m_gla_bg_bwd, m_gla_norm_w, m_mla_q_norm_w, m_mla_w_uq, m_mla_kv_norm_w, m_mla_w_ukv, m_norm2_w, m_w_ff1, m_w_ff2, m_final_norm_w, v_c_ctx, v_w_mod, v_b_mod, v_norm1_w, v_w_in, v_w_out, v_sgu_norm_w, v_sgu_norm_b, v_sgu_w, v_sgu_b, v_gla_wg_fwd, v_gla_bg_fwd, v_gla_wg_bwd, v_gla_bg_bwd, v_gla_norm_w, v_mla_q_norm_w, v_mla_w_uq, v_mla_kv_norm_w, v_mla_w_ukv, v_norm2_w, v_w_ff1, v_w_ff2, v_final_norm_w):
    args = (x, c, ctx, c_ctx, w_mod, b_mod, norm1_w, w_in, w_out, sgu_norm_w, sgu_norm_b, sgu_w, sgu_b, gla_wg_fwd, gla_bg_fwd, gla_wg_bwd, gla_bg_bwd, gla_norm_w, mla_q_norm_w, mla_w_uq, mla_kv_norm_w, mla_w_ukv, norm2_w, w_ff1, w_ff2, final_norm_w)
    w = dict(zip(ARG_NAMES, args))
    moms = (m_c_ctx, m_w_mod, m_b_mod, m_norm1_w, m_w_in, m_w_out, m_sgu_norm_w, m_sgu_norm_b, m_sgu_w, m_sgu_b, m_gla_wg_fwd, m_gla_bg_fwd, m_gla_wg_bwd, m_gla_bg_bwd, m_gla_norm_w, m_mla_q_norm_w, m_mla_w_uq, m_mla_kv_norm_w, m_mla_w_ukv, m_norm2_w, m_w_ff1, m_w_ff2, m_final_norm_w)
    vars_ = (v_c_ctx, v_w_mod, v_b_mod, v_norm1_w, v_w_in, v_w_out, v_sgu_norm_w, v_sgu_norm_b, v_sgu_w, v_sgu_b, v_gla_wg_fwd, v_gla_bg_fwd, v_gla_wg_bwd, v_gla_bg_bwd, v_gla_norm_w, v_mla_q_norm_w, v_mla_w_uq, v_mla_kv_norm_w, v_mla_w_ukv, v_norm2_w, v_w_ff1, v_w_ff2, v_final_norm_w)
    m1 = dict(zip(WEIGHT_NAMES, moms))
    m2 = dict(zip(WEIGHT_NAMES, vars_))
    xi, yi, ci = lax.axis_index("x"), lax.axis_index("y"), lax.axis_index("c")
    chip, dev = 2 * xi + yi, 4 * xi + 2 * yi + ci
    depth = w_mod.shape[0]

    def shard_halves(l, names):
        return [("gather2", "chip", [w[k][l].astype(BF16).reshape(2, w[k].shape[1] // 2, w[k].shape[2])]) for k in names]

    def full_weights(names, gathered):
        return {k: _full_weight(k, g.reshape(4, *w[k].shape[1:])) for k, g in zip(names, gathered)}

    got = xchg("gather_inputs", [("gather", "all", [c])] + shard_halves(0, ATTN_WEIGHTS))
    c_all = got[0]
    c16 = jnp.concatenate([c_all.reshape(8, D), c_ctx[None], jnp.zeros((7, D), F32)], axis=0)
    b_loc = lax.dynamic_slice_in_dim(b_mod, chip * W_MOD_COLS, W_MOD_COLS, axis=1)[:, None, :]
    mod_part = mod_project(c16, w_mod, b_loc)
    mod_all, = xchg("gather_mod", [("gather", "chip", [mod_part])])
    mod_full = mod_all.transpose(1, 2, 0, 3).reshape(depth, 16, 6 * D)
    mods = [jnp.stack([mod_full[l, 8], lax.dynamic_index_in_dim(mod_full[l], dev, 0, keepdims=False)])[:, None, :]
            for l in range(depth)]

    small = [{k: w[k][l] for k in SMALL_NAMES} for l in range(depth)]
    n_big = len(BIG_NAMES)
    n_ff = len(FF_WEIGHTS)
    later = shard_halves(0, FF_WEIGHTS) + [e for l in range(1, depth) for e in shard_halves(l, BIG_NAMES)]
    core = ci.astype(F32).reshape(1, 1)

    def on_side(res):
        return (full_weights(FF_WEIGHTS, res[:n_ff]),
                [full_weights(BIG_NAMES, res[n_ff + i * n_big:n_ff + (i + 1) * n_big]) for i in range(depth - 1)])

    def half_major(k, g):
        ch = _chip_chunks(k, g)
        return ch.reshape(4, 2, ch.shape[1] // 2, ch.shape[2]).transpose(1, 0, 2, 3)

    def swap_entries(gb, names):
        hm = [half_major(k, gb[k]) for k in names]
        return hm, [("swap", "sib", [h]) for h in hm]

    def scatter_entries(tag, names, hm, recv):
        out = []
        for k, h, r in zip(names, hm, recv):
            s2 = pair_sum(f"pair_sum_{tag}_{k}", h.reshape(2, -1, h.shape[-1]), r.reshape(-1, r.shape[-1]), core)
            out.append(("scatter", "chip", [s2.reshape(r.shape)]))
        return out

    def grad_side(gb_later):
        hms, entries = [], []
        for gb in gb_later:
            hm, e = swap_entries(gb, BIG_NAMES)
            hms.append(hm)
            entries += e

        def make_scatter(recv):
            return [e for i, hm in enumerate(hms)
                    for e in scatter_entries(f"l{i + 1}", BIG_NAMES, hm, recv[i * n_big:(i + 1) * n_big])]

        return entries, make_scatter

    def ff_side(g_ff):
        hm, entries = swap_entries(g_ff, FF_WEIGHTS)
        return entries, lambda recv: scatter_entries("l0", FF_WEIGHTS, hm, recv)

    loss, grad_x, dmods, gbig, gsmall, dfnw, early_pieces = local_step(
        x[0], ctx[0], loss_target[0], mods, [full_weights(ATTN_WEIGHTS, got[1:])], small, final_norm_w, side=later,
        on_side=on_side, grad_side=grad_side, ff_side=ff_side)
    loss = lax.psum(loss[0, 0], ("x", "y", "c"))

    dm_lat = jnp.stack([dmods[l][1, 0] for l in range(depth)])
    dm_ctx = jnp.stack([dmods[l][0, 0] for l in range(depth)])
    small_pack = _pack([dm_lat, dm_ctx] + [jnp.stack([gsmall[l][k] for l in range(depth)]) for k in SMALL_NAMES] + [dfnw])
    hm0, swap0 = swap_entries(gbig[0], ATTN_WEIGHTS)
    got = xchg("exchange_grads", [("gather", "all", [small_pack])] + swap0)
    small_all, recv0 = got[0], got[1:]
    small_sum = tree_sum("small_grad_sum", small_all)

    n_dm = depth * 6 * D
    dm_rows = n_dm // LANES
    dm_lat_all = small_all[:, :dm_rows].reshape(8, depth, 6 * D)
    dm_ctx_sum = small_sum[dm_rows:2 * dm_rows].reshape(depth, 6 * D)
    take = lambda a: lax.dynamic_slice_in_dim(a, chip * W_MOD_COLS, W_MOD_COLS, axis=-1)
    dmc_loc = take(dm_ctx_sum)
    cc_part = cctx_partial(jnp.pad(dmc_loc[:, None, :], ((0, 0), (0, 7), (0, 0))), w_mod)
    got = xchg("scatter_grads", [("gather", "chip", [cc_part])] + scatter_entries("l0", ATTN_WEIGHTS, hm0, recv0))
    cc_parts = got[0]
    keys = ([(0, k) for k in ATTN_WEIGHTS] + [(l, k) for l in range(1, depth) for k in BIG_NAMES] + [(0, k) for k in FF_WEIGHTS])
    pieces = dict(zip(keys, list(got[1:]) + list(early_pieces)))
    keys = [(l, k) for l in range(depth) for k in BIG_NAMES]
    reduced = {lk: tree_sum(f"chip_sum_l{lk[0]}_{lk[1]}", pieces[lk]) for lk in keys}
    g_c_ctx = cctx_grad(cc_parts, jnp.broadcast_to(c_ctx[None], (8, D)))[0]

    others = dict(zip(keys, xchg("share_halves", [("swap", "sib", [reduced[lk], reduced[lk]]) for lk in keys])))
    shard = {lk: jnp.where(ci == 0, jnp.concatenate([reduced[lk], others[lk]], axis=0),
                           jnp.concatenate([others[lk], reduced[lk]], axis=0)) for lk in keys}
    grads = {k: jnp.stack([shard[(l, k)] for l in range(depth)]) for k in BIG_NAMES}

    dm16 = jnp.concatenate([take(dm_lat_all).transpose(1, 0, 2), dmc_loc[:, None, :], jnp.zeros((depth, 7, W_MOD_COLS), F32)], axis=1)
    grads["w_mod"] = mod_weight_grad(c16, dm16)
    flat_sum = small_sum.reshape(-1)
    g_b_mod = (flat_sum[:n_dm] + flat_sum[n_dm:2 * n_dm]).reshape(depth, 6 * D)
    rest_shapes = [w[k].shape for k in PACKED[2:]]
    n_rest = sum(int(np.prod(s)) for s in rest_shapes)
    for k, g in zip(PACKED, [g_c_ctx, g_b_mod] + _unpack(flat_sum[2 * n_dm:2 * n_dm + n_rest], rest_shapes)):
        grads[k] = g

    delta, new_m, new_v = {}, {}, {}
    for k in BIG_NAMES + ("w_mod",):
        view = lambda a: a.reshape(-1, a.shape[-1])
        d_, m_, v_ = adamw(f"adamw_{k}", view(w[k]), view(grads[k]), view(m1[k]), view(m2[k]))
        delta[k], new_m[k], new_v[k] = d_.reshape(w[k].shape), m_.reshape(w[k].shape), v_.reshape(w[k].shape)
    shapes = [w[k].shape for k in PACKED]
    d_, m_, v_ = adamw("adamw_small", _pack([w[k] for k in PACKED]), _pack([grads[k] for k in PACKED]),
                       _pack([m1[k] for k in PACKED]), _pack([m2[k] for k in PACKED]))
    for k, dk, mk, vk in zip(PACKED, _unpack(d_, shapes), _unpack(m_, shapes), _unpack(v_, shapes)):
        delta[k], new_m[k], new_v[k] = dk, mk, vk
    return (loss, grad_x[None], *[grads[k] for k in WEIGHT_NAMES], *[delta[k] for k in WEIGHT_NAMES],
            *[new_m[k] for k in WEIGHT_NAMES], *[new_v[k] for k in WEIGHT_NAMES])
```

```python
import functools
import math

import numpy as np
import jax
import jax.numpy as jnp
from jax import lax
from jax.experimental import pallas as pl
from jax.experimental.pallas import tpu as pltpu

F32 = jnp.float32
BF16 = jnp.bfloat16
HI = lax.Precision.HIGHEST
EPS = 1e-6
VMEM_LIMIT_BYTES = 56 * 1024 * 1024
LANES = 128

D = 1024
D_FF = 4096
CTX = 256
GRID_W = 64
SGU_CHUNK = 128
GLA_CHUNK = 64
GLA_TAU = 16.0
GLA_DK = 32
MLA_SCALE = (128 + 64) ** -0.5
SCORE_SCALE = MLA_SCALE * math.log2(math.e)
LN2 = math.log(2.0)
ROPE_BASE = 10000.0
TM = 256
NCTXB = CTX // TM
P_GV, P_CKV, P_SU, P_SV, P_GR, P_DQ, P_GK, P_GATE, P_KR, P_GQ = 0, 256, 512, 768, 1024, 1280, 1536, 1664, 1792, 1920
P_COLS = 2048
IN_GROUPS = ((0, 128, P_GK), (128, 256, P_GV), (384, 32, P_GATE), (416, 256, P_CKV), (672, 64, P_KR),
             (736, 256, P_SU), (992, 256, P_SV), (1248, 128, P_GQ), (1376, 256, P_GR), (1632, 256, P_DQ))
ADAM_LR, ADAM_B1, ADAM_B2, ADAM_EPS, ADAM_WD, ADAM_STEP = 0.001, 0.9, 0.999, 1e-08, 0.01, 10
MESH = pl.DeviceIdType.MESH


def _params(sem):
    return pltpu.CompilerParams(dimension_semantics=sem, vmem_limit_bytes=VMEM_LIMIT_BYTES)


def _pick(n, cands):
    for c in cands:
        if n % c == 0:
            return c
    return n


class Op:
    def __init__(self, arr, blk, idx, gshape, gidx, acc):
        self.arr, self.blk, self.idx, self.gshape, self.gidx, self.acc = arr, blk, idx, gshape, gidx, acc

    def spec(self):
        return pl.BlockSpec(self.blk, self.idx)


def rows(arr, width=None, cb=0, off=0, tm=TM):
    w = arr.shape[1] if width is None else width
    n = arr.shape[0] - off * tm
    return Op(arr, (tm, w), lambda i: (i + off, cb), (n, w), lambda i: (i, 0), False)


def chunks(arr, per_tile):
    z = (0,) * (arr.ndim - 1)
    return Op(arr, (per_tile,) + arr.shape[1:], lambda i: (i,) + z, arr.shape, lambda i: (i,) + z, False)


def const(arr):
    z = (0,) * arr.ndim
    return Op(arr, arr.shape, lambda i: z, arr.shape, lambda i: z, True)


def rw(name, fn, ins, outs, grid):
    nin = len(ins)

    def body(*refs):
        vals = [r[...] for r in refs[:nin]]
        res = fn(pl.program_id(0), *vals)
        for o, r in zip(refs[nin:], res):
            o[...] = r.astype(o.dtype)

    return pl.pallas_call(
        body, name=name, grid=(grid,),
        in_specs=[o.spec() for o in ins],
        out_specs=[pl.BlockSpec(b, ix) for (_, _, b, ix) in outs],
        out_shape=[jax.ShapeDtypeStruct(s, d) for (s, d, _, _) in outs],
        compiler_params=_params(("arbitrary",)),
    )(*[o.arr for o in ins])


def rowout(n, w, dtype, tm=TM):
    return ((n, w), dtype, (tm, w), lambda i: (i, 0))


def chunkout(shape, dtype, per_tile):
    z = (0,) * (len(shape) - 1)
    return (shape, dtype, (per_tile,) + tuple(shape[1:]), lambda i: (i,) + z)


def rw_vjp(name, fn, ins, cots, wrt, grid, gdt=None, adds=None):
    nin = len(ins)
    cot_ops = [c for c in cots if c is not None]
    add_items = sorted((adds or {}).items())
    gdt = gdt or [F32] * len(wrt)
    ncot, nadd = len(cot_ops), len(add_items)

    def body(*refs):
        i = pl.program_id(0)
        vals = [r[...] for r in refs[:nin]]
        cvals = [r[...] for r in refs[nin:nin + ncot]]
        avals = [r[...] for r in refs[nin + ncot:nin + ncot + nadd]]
        grefs = refs[nin + ncot + nadd:]

        def f(*d):
            a = list(vals)
            for k, dv in zip(wrt, d):
                a[k] = dv
            return tuple(fn(i, *a))

        outs, vf = jax.vjp(f, *[vals[k] for k in wrt])
        it = iter(cvals)
        ct = tuple(jnp.zeros_like(o) if c is None else next(it).astype(o.dtype) for c, o in zip(cots, outs))
        gs = list(vf(ct))
        for (pos, _), av in zip(add_items, avals):
            gs[pos] = gs[pos].astype(F32) + av.astype(F32)
        for pos, (k, g, gref) in enumerate(zip(wrt, gs, grefs)):
            if ins[k].acc:
                @pl.when(i == 0)
                def _():
                    gref[...] = jnp.zeros_like(gref)
                gref[...] += g.astype(gref.dtype)
            else:
                gref[...] = g.astype(gref.dtype)

    all_in = list(ins) + cot_ops + [op for _, op in add_items]
    return pl.pallas_call(
        body, name=name, grid=(grid,),
        in_specs=[o.spec() for o in all_in],
        out_specs=[pl.BlockSpec(ins[k].blk, ins[k].gidx) for k in wrt],
        out_shape=[jax.ShapeDtypeStruct(ins[k].gshape, dt) for k, dt in zip(wrt, gdt)],
        compiler_params=_params(("arbitrary",)),
    )(*[o.arr for o in all_in])


MM_VMEM_BUDGET = 40 * 1024 * 1024
MM_COLS = 1024


def _square_bf16(a):
    a = a.astype(F32)
    return (a * a).astype(BF16)


def mm(name, a, b, out_dtype, pre=None, post=None, extras=(), bt=False):
    m, k = a.shape
    n = b.shape[0] if bt else b.shape[1]
    nc = min(n, MM_COLS)
    row_bytes = k * a.dtype.itemsize + n * jnp.dtype(out_dtype).itemsize + sum(n * e.dtype.itemsize for e in extras)
    tm = next(t for t in (768, 512, 384, 256, 128, 64)
              if m % t == 0 and 2 * t * row_bytes + 2 * k * n * b.dtype.itemsize + t * nc * 4 <= MM_VMEM_BUDGET)

    def body(a_ref, b_ref, *rest):
        o_ref = rest[-1]
        av = a_ref[...]
        if pre is not None:
            av = pre(av)
        for j in range(n // nc):
            cs = slice(j * nc, (j + 1) * nc)
            if bt:
                acc = lax.dot_general(av, b_ref[cs, :], (((1,), (1,)), ((), ())), preferred_element_type=F32)
            else:
                acc = lax.dot_general(av, b_ref[:, cs], (((1,), (0,)), ((), ())), preferred_element_type=F32)
            if post is not None:
                acc = post(acc, *[e[:, cs] for e in rest[:-1]])
            o_ref[:, cs] = acc.astype(o_ref.dtype)

    row = lambda w: pl.BlockSpec((tm, w), lambda i: (i, 0))
    return pl.pallas_call(
        body, name=name, grid=(m // tm,),
        in_specs=[row(k), pl.BlockSpec(b.shape, lambda i: (0, 0))] + [row(n) for _ in extras],
        out_specs=row(n),
        out_shape=jax.ShapeDtypeStruct((m, n), out_dtype),
        compiler_params=_params(("arbitrary",)),
    )(a, b, *extras)


def mm_tn(name, a, b, pre=None, side=None):
    m, ka = a.shape
    _, nb = b.shape
    tm = _pick(m, (768, 512, 256))
    ta = _pick(ka, (2048, 1024, 512, 256, 128))
    tb = _pick(nb, tuple(t for t in (4096, 2048, 1024, 512, 256, 128) if ta * t * 4 <= 8 * 1024 * 1024))
    sd = side or Side(())

    def body(a_ref, b_ref, *rest):
        o_ref = rest[sd.n_in]
        finish = sd.start(rest[:sd.n_in], rest[sd.n_in + 1:sd.n_in + 1 + sd.n_out], rest[sd.n_in + 1 + sd.n_out:])

        @pl.when(pl.program_id(2) == 0)
        def _():
            o_ref[...] = jnp.zeros_like(o_ref)
        av = a_ref[...] if pre is None else pre(a_ref[...])
        o_ref[...] += lax.dot_general(av, b_ref[...], (((0,), (0,)), ((), ())), preferred_element_type=F32)
        finish()

    res = pl.pallas_call(
        body, name=name, grid=(ka // ta, nb // tb, m // tm),
        in_specs=[pl.BlockSpec((tm, ta), lambda i, j, k: (k, i)), pl.BlockSpec((tm, tb), lambda i, j, k: (k, j))] + sd.in_specs,
        out_specs=[pl.BlockSpec((ta, tb), lambda i, j, k: (i, j))] + sd.out_specs,
        out_shape=[jax.ShapeDtypeStruct((ka, nb), F32)] + sd.shapes,
        scratch_shapes=sd.sems,
        compiler_params=_params(("arbitrary", "arbitrary", "arbitrary")),
    )(a, b, *sd.arrays)
    return res[0] if side is None else (res[0], list(res[1:]))


def _rms(x, w):
    return x * lax.rsqrt(jnp.mean(x * x, axis=-1, keepdims=True) + EPS) * w


def _mod_of(blk, m):
    return jnp.where(blk < NCTXB, m[0], m[1])


def _gelu(x):
    return x * (0.5 * (1.0 + jnp.tanh(math.sqrt(2.0 / math.pi) * (x + 0.044715 * (x * x * x)))))


def _sigmoid(x):
    return 1.0 / (1.0 + jnp.exp(-x))


def _log_sigmoid(z):
    return jnp.minimum(z, 0.0) - jnp.log(1.0 + jnp.exp(-jnp.abs(z)))


def _dot(a, b, dims=((1,), (0,)), precision=None):
    return lax.dot_general(a, b, (dims, ((), ())), precision=precision, preferred_element_type=F32)


def _lane_group_mask(width, group, h):
    lane = lax.broadcasted_iota(jnp.int32, (1, width), 1)
    return (lane >= h * group) & (lane < (h + 1) * group)


def fn_norm1(blk, x, m, nw):
    mv = _mod_of(blk, m)
    return ((_rms(x, nw) * (1.0 + mv[:, D:2 * D]) + mv[:, 0:D]),)


def fn_res_norm2(blk, x, yo, m, nw):
    mv = _mod_of(blk, m)
    x1 = x + mv[:, 2 * D:3 * D] * yo
    return x1, _rms(x1, nw) * (1.0 + mv[:, 4 * D:5 * D]) + mv[:, 3 * D:4 * D]


def fn_res2(blk, x1, f, m):
    mv = _mod_of(blk, m)
    return (x1 + mv[:, 5 * D:6 * D] * f,)


def fn_sgu(blk, su, sv, nw, nb, ws, bm):
    u = _gelu(su)
    g = _gelu(sv)
    mu = jnp.mean(g, axis=-1, keepdims=True)
    var = jnp.mean(jnp.square(g - mu), axis=-1, keepdims=True)
    v = (g - mu) * lax.rsqrt(var + EPS) * nw + nb
    out = []
    for c in range(su.shape[0] // SGU_CHUNK):
        vc = v[c * SGU_CHUNK:(c + 1) * SGU_CHUNK]
        s = bm
        for h in range(4):
            vh = jnp.where(_lane_group_mask(256, 64, h), vc, 0.0)
            s = s + _dot(ws[h].astype(BF16), vh.astype(BF16))
        out.append(u[c * SGU_CHUNK:(c + 1) * SGU_CHUNK] * s)
    return (jnp.concatenate(out, axis=0),)


def fn_gates(blk, pg, wg, bg):
    z = _dot(pg.astype(BF16), wg.astype(BF16)) + bg
    g = _log_sigmoid(z) * (1.0 / GLA_TAU)
    return g[:, :128], g[:, 128:]


def _scan_rows(x, rev):
    n = x.shape[0]
    row = lax.broadcasted_iota(jnp.int32, x.shape, 0)
    d = 1
    while d < n:
        if rev:
            x = x + jnp.where(row < n - d, pltpu.roll(x, n - d, 0), 0.0)
        else:
            x = x + jnp.where(row >= d, pltpu.roll(x, d, 0), 0.0)
        d *= 2
    return x


@functools.partial(jax.custom_vjp, nondiff_argnums=(1,))
def _cumsum_rows(x, rev):
    return _scan_rows(x, rev)


def _cumsum_rows_fwd(x, rev):
    return _scan_rows(x, rev), None


def _cumsum_rows_bwd(rev, _, dy):
    return (_scan_rows(dy, not rev),)


_cumsum_rows.defvjp(_cumsum_rows_fwd, _cumsum_rows_bwd)


def _gla_chunk_terms(g, rev):
    return _cumsum_rows(g, rev), jnp.sum(g, axis=0, keepdims=True)


def _bd_mask():
    r = lax.broadcasted_iota(jnp.int32, (128, 256), 0)
    c = lax.broadcasted_iota(jnp.int32, (128, 256), 1)
    return (r // GLA_DK) == (c // 64)


def _gla_kv_chunk(k, v, g, rev):
    b, tot = _gla_chunk_terms(g, rev)
    kd = k * jnp.exp(tot - b)
    u = jnp.where(_bd_mask(), _dot(kd.astype(BF16), v.astype(BF16), ((0,), (0,))), 0.0)
    r = lax.broadcasted_iota(jnp.int32, (128, 128), 0)
    c = lax.broadcasted_iota(jnp.int32, (128, 128), 1)
    col = jnp.sum(jnp.where(r == c, jnp.broadcast_to(jnp.exp(tot), (128, 128)), 0.0), axis=1, keepdims=True)
    return u, jnp.broadcast_to(col, (128, 128))


def _gla_o_chunk(q, k, v, g, s, rev):
    b, _ = _gla_chunk_terms(g, rev)
    qe = q * jnp.exp(b) * (GLA_DK ** -0.5)
    ke = k * jnp.exp(-b)
    o = _dot(qe.astype(BF16), jnp.where(_bd_mask(), s, 0.0).astype(BF16))
    qs = jnp.concatenate([jnp.where(_lane_group_mask(128, GLA_DK, h), qe, 0.0) for h in range(4)], axis=0)
    a = _dot(qs.astype(BF16), ke.astype(BF16), ((1,), (1,)))
    i = lax.broadcasted_iota(jnp.int32, a.shape, 0) % GLA_CHUNK
    j = lax.broadcasted_iota(jnp.int32, a.shape, 1)
    a = jnp.where((j >= i) if rev else (j <= i), a, 0.0)
    av = _dot(a.astype(BF16), v.astype(BF16))
    for h in range(4):
        o = o + jnp.where(_lane_group_mask(256, 64, h), av[GLA_CHUNK * h:GLA_CHUNK * (h + 1)], 0.0)
    return o


def fn_gla_kv(blk, k, v, gf, gb):
    uf, ef, ub, eb = [], [], [], []
    for c in range(k.shape[0] // GLA_CHUNK):
        sl = slice(c * GLA_CHUNK, (c + 1) * GLA_CHUNK)
        u, e = _gla_kv_chunk(k[sl], v[sl], gf[sl], False)
        uf.append(u[None]); ef.append(e[None])
        u, e = _gla_kv_chunk(k[sl], v[sl], gb[sl], True)
        ub.append(u[None]); eb.append(e[None])
    cat = lambda t: jnp.concatenate(t, axis=0)
    return cat(uf), cat(ef), cat(ub), cat(eb)


def fn_gla_o(blk, q, k, v, gf, gb, gr, sf, sb, nwt):
    out = []
    for c in range(q.shape[0] // GLA_CHUNK):
        sl = slice(c * GLA_CHUNK, (c + 1) * GLA_CHUNK)
        out.append(_gla_o_chunk(q[sl], k[sl], v[sl], gf[sl], sf[c], False)
                   + _gla_o_chunk(q[sl], k[sl], v[sl], gb[sl], sb[c], True))
    o = jnp.concatenate(out, axis=0)
    r = lax.broadcasted_iota(jnp.int32, (256, 256), 0)
    c = lax.broadcasted_iota(jnp.int32, (256, 256), 1)
    head_mean = jnp.where((r // 64) == (c // 64), 1.0 / 64.0, 0.0).astype(F32)
    ms = _dot(o * o, head_mean, precision=HI)
    on = o * lax.rsqrt(ms + EPS) * nwt
    return (on * (gr * _sigmoid(gr)),)


def _rope_partner(x):
    lane = lax.broadcasted_iota(jnp.int32, x.shape, 1)
    return jnp.where((lane // 16) % 2 == 0, pltpu.roll(x, LANES - 16, 1), pltpu.roll(x, 16, 1))


@jax.custom_vjp
def _rope(x, cs, sn):
    return x * cs + _rope_partner(x) * sn


def _rope_fwd(x, cs, sn):
    return _rope(x, cs, sn), (cs, sn)


def _rope_bwd(res, dy):
    cs, sn = res
    return dy * cs + _rope_partner(dy * sn), jnp.zeros_like(cs), jnp.zeros_like(sn)


_rope.defvjp(_rope_fwd, _rope_bwd)


def fn_mla_pre(blk, ckv, dq, kvw, qw):
    return _rms(ckv, kvw), _rms(dq, qw)


def fn_mla_post(blk, kk, qu, kr, cs, sn):
    kro = _rope(kr, cs, sn)
    kcat, q = [], []
    for h in range(4):
        kcat += [kk[:, 128 * h:128 * (h + 1)].astype(F32), kro]
        q += [qu[:, 256 * h:256 * h + 128], _rope(qu[:, 256 * h + 128:256 * (h + 1)], cs, sn)]
    return jnp.concatenate(kcat, axis=1), jnp.concatenate(q, axis=1) * SCORE_SCALE


ATTN_ROWS = 256
NEG = -1e30


def _scores(q, k, k0, context_queries):
    s = _dot(q, k, ((1,), (1,)))
    if context_queries is not None:
        col = k0 + lax.broadcasted_iota(jnp.int32, s.shape, 1)
        s = jnp.where(context_queries & (col >= CTX), NEG, s)
    return s


def flash_fwd(q, kcat, kvu, side=()):
    t = q.shape[0]
    tq = _pick(t, (768, 512, 256))
    tk = _pick(t, (2816, 1536, 768, 512, 256))
    nsub = tq // ATTN_ROWS
    n_side_in, _, _, side_shapes = _exchange_shapes(side)
    n_side = len(side)

    def body(q_ref, k_ref, v_ref, *rest):
        side_in, rest = rest[:n_side_in], rest[n_side_in:]
        o_ref, lse_ref = rest[:2]
        side_out, (m_sc, l_sc, acc_sc), side_sems = rest[2:2 + n_side], rest[2 + n_side:5 + n_side], rest[5 + n_side:]
        h, qi, ki = pl.program_id(0), pl.program_id(1), pl.program_id(2)
        if side:
            starts, forwards, finals = _exchange_phases(side, side_in, side_out, *side_sems)
            at_tile0 = (qi == 0) & (ki == 0)
            last = (h == pl.num_programs(0) - 1) & (qi == pl.num_programs(1) - 1) & (ki == pl.num_programs(2) - 1)
            for when, phase in (((h == 0) & at_tile0, starts), ((h == 2) & at_tile0, forwards)):
                @pl.when(when)
                def _(phase=phase):
                    for run in phase:
                        run()

        @pl.when(ki == 0)
        def _():
            m_sc[...] = jnp.full_like(m_sc, NEG)
            l_sc[...] = jnp.zeros_like(l_sc)
            acc_sc[...] = jnp.zeros_like(acc_sc)

        k, v = k_ref[...], v_ref[...]
        chains = [pl.ds(r * ATTN_ROWS, ATTN_ROWS) for r in range(nsub)]
        scores = [_scores(q_ref[rs, :], k, ki * tk, (qi == 0) if r == 0 else None) for r, rs in enumerate(chains)]
        probs = []
        for rs, s in zip(chains, scores):
            m_old = m_sc[rs, :]
            m_new = jnp.maximum(m_old, jnp.max(s, axis=-1, keepdims=True))
            alpha = jnp.exp2(m_old - m_new)
            p = jnp.exp2(s - m_new)
            l_sc[rs, :] = alpha * l_sc[rs, :] + jnp.sum(p, axis=-1, keepdims=True)
            m_sc[rs, :] = m_new
            probs.append((alpha, p.astype(BF16)))
        for rs, (alpha, p) in zip(chains, probs):
            acc_sc[rs, :] = alpha * acc_sc[rs, :] + _dot(p, v)

        @pl.when(ki == pl.num_programs(2) - 1)
        def _():
            o_ref[...] = acc_sc[...] / l_sc[...]
            lse_ref[...] = jnp.broadcast_to(m_sc[...] + jnp.log2(l_sc[...]), lse_ref.shape)

        if side:
            @pl.when(last)
            def _():
                for run in finals:
                    run()

    any_spec = pl.BlockSpec(memory_space=pl.ANY)
    res = pl.pallas_call(
        body, name="mla_flash_fwd", grid=(4, t // tq, t // tk),
        in_specs=[pl.BlockSpec((tq, 256), lambda h, i, j: (i, h)), pl.BlockSpec((tk, 256), lambda h, i, j: (j, h)),
                  pl.BlockSpec((tk, 128), lambda h, i, j: (j, 4 + h))] + [any_spec] * n_side_in,
        out_specs=[pl.BlockSpec((tq, 128), lambda h, i, j: (i, h)), pl.BlockSpec((tq, 128), lambda h, i, j: (i, h))]
        + [any_spec] * n_side,
        out_shape=[jax.ShapeDtypeStruct((t, 512), F32), jax.ShapeDtypeStruct((t, 512), F32)] + side_shapes,
        scratch_shapes=[pltpu.VMEM((tq, 1), F32), pltpu.VMEM((tq, 1), F32), pltpu.VMEM((tq, 128), F32)]
        + (_exchange_sems(side) if side else []),
        compiler_params=_params(("arbitrary", "arbitrary", "arbitrary")),
    )(q, kcat, kvu, *[a for _, _, arrs in side for a in arrs])
    return res[0], res[1], list(res[2:])


def fn_attn_stats(blk, do, o, lse):
    out = []
    for h in range(4):
        hs = slice(128 * h, 128 * (h + 1))
        d = jnp.sum(do[:, hs] * o[:, hs], axis=-1, keepdims=True)
        out.append(lse[:, hs].T[0:8])
        out.append(jnp.broadcast_to(d, (do.shape[0], 128)).T[0:8])
    return (jnp.concatenate(out, axis=0)[None],)


def flash_bwd(q, kcat, kvu, dy, stats, side=None):
    t = q.shape[0]
    tq = _pick(t, (2816, 768, 512, 256))
    tk = _pick(t, (768, 512, 256))
    nst = tq // TM
    sd = side or Side(())

    def body(q_ref, k_ref, v_ref, do_ref, st_ref, *rest):
        dq_ref, dk_ref, dv_ref = rest[sd.n_in:sd.n_in + 3]
        finish = sd.start(rest[:sd.n_in], rest[sd.n_in + 3:sd.n_in + 3 + sd.n_out], rest[sd.n_in + 3 + sd.n_out:])
        kj, qi = pl.program_id(1), pl.program_id(2)

        @pl.when(qi == 0)
        def _():
            dk_ref[...] = jnp.zeros_like(dk_ref)
            dv_ref[...] = jnp.zeros_like(dv_ref)

        def step(has_context_queries):
            q_, k, v, do = q_ref[...], k_ref[...], v_ref[...], do_ref[...].astype(BF16)
            lse_row = jnp.concatenate([st_ref[u, 0:1, :] for u in range(nst)], axis=1)
            delta_row = jnp.concatenate([st_ref[u, 8:9, :] for u in range(nst)], axis=1)
            s = _dot(k, q_, ((1,), (1,)))
            if has_context_queries:
                key = kj * tk + lax.broadcasted_iota(jnp.int32, s.shape, 0)
                qry = lax.broadcasted_iota(jnp.int32, s.shape, 1)
                s = jnp.where((qry < CTX) & (key >= CTX), NEG, s)
            p = jnp.exp2(s - lse_row)
            dp = _dot(v, do, ((1,), (1,)))
            ds = (p * (dp - delta_row)).astype(BF16)
            dv_ref[...] += _dot(p.astype(BF16), do)
            dk_ref[...] += LN2 * _dot(ds, q_)
            dq_new = LN2 * _dot(ds, k, ((0,), (0,)))
            rows_ = pl.ds(pl.multiple_of(qi * tq, TM), tq)

            @pl.when(kj == 0)
            def _():
                dq_ref[rows_, :] = dq_new

            @pl.when(kj != 0)
            def _():
                dq_ref[rows_, :] += dq_new

        pl.when(qi == 0)(lambda: step(True))
        pl.when(qi != 0)(lambda: step(False))
        finish()

    res = pl.pallas_call(
        body, name="mla_flash_bwd", grid=(4, t // tk, t // tq),
        in_specs=[pl.BlockSpec((tq, 256), lambda h, j, i: (i, h)), pl.BlockSpec((tk, 256), lambda h, j, i: (j, h)),
                  pl.BlockSpec((tk, 128), lambda h, j, i: (j, 4 + h)), pl.BlockSpec((tq, 128), lambda h, j, i: (i, 4 + h)),
                  pl.BlockSpec((nst, 16, 256), lambda h, j, i: (i, h, 0))] + sd.in_specs,
        out_specs=[pl.BlockSpec((t, 256), lambda h, j, i: (0, h)), pl.BlockSpec((tk, 256), lambda h, j, i: (j, h)),
                   pl.BlockSpec((tk, 128), lambda h, j, i: (j, h))] + sd.out_specs,
        out_shape=[jax.ShapeDtypeStruct((t, 1024), F32), jax.ShapeDtypeStruct((t, 1024), F32),
                   jax.ShapeDtypeStruct((t, 512), F32)] + sd.shapes,
        scratch_shapes=sd.sems,
        compiler_params=_params(("arbitrary", "arbitrary", "arbitrary")),
    )(q, kcat, kvu, dy, stats, *sd.arrays)
    return res[0], res[1], res[2], list(res[3:])


SCAN_BLOCK = CTX // GLA_CHUNK


def _scan_block(t, nb, rev):
    if not rev:
        return t
    return jnp.where(t < 1, 0, nb - t)


def _scan_order(rev):
    return tuple(reversed(range(SCAN_BLOCK))) if rev else tuple(range(SCAN_BLOCK))


def _both_halves(e):
    return jnp.concatenate([e, e], axis=1)


def gla_states(uf, ef, ub, eb):
    nb = uf.shape[0] // SCAN_BLOCK

    def body(uf_ref, ef_ref, ub_ref, eb_ref, sf_ref, sb_ref, sf_sc, sb_sc):
        @pl.when(pl.program_id(0) == 0)
        def _():
            sf_sc[...] = jnp.zeros_like(sf_sc)
            sb_sc[...] = jnp.zeros_like(sb_sc)

        for u_ref, e_ref, s_ref, sc, rev in ((uf_ref, ef_ref, sf_ref, sf_sc, False), (ub_ref, eb_ref, sb_ref, sb_sc, True)):
            s = sc[...]
            for c in _scan_order(rev):
                s_ref[c] = s
                s = _both_halves(e_ref[c]) * s + u_ref[c]
            sc[...] = s

    big = lambda rev: pl.BlockSpec((SCAN_BLOCK, 128, 256), lambda t: (_scan_block(t, nb, rev), 0, 0))
    small = lambda rev: pl.BlockSpec((SCAN_BLOCK, 128, 128), lambda t: (_scan_block(t, nb, rev), 0, 0))
    return pl.pallas_call(
        body, name="gla_states", grid=(nb,),
        in_specs=[big(False), small(False), big(True), small(True)],
        out_specs=[big(False), big(True)],
        out_shape=[jax.ShapeDtypeStruct(uf.shape, F32)] * 2,
        scratch_shapes=[pltpu.VMEM((128, 256), F32)] * 2,
        compiler_params=_params(("arbitrary",)),
    )(uf, ef, ub, eb)


def gla_states_bwd(ef, eb, sf, sb, dsf, dsb):
    nb = ef.shape[0] // SCAN_BLOCK

    def body(ef_ref, eb_ref, sf_ref, sb_ref, dsf_ref, dsb_ref, duf_ref, def_ref, dub_ref, deb_ref, gf_sc, gb_sc):
        @pl.when(pl.program_id(0) == 0)
        def _():
            gf_sc[...] = jnp.zeros_like(gf_sc)
            gb_sc[...] = jnp.zeros_like(gb_sc)

        for e_ref, s_ref, ds_ref, du_ref, de_ref, g_sc, rev in ((ef_ref, sf_ref, dsf_ref, duf_ref, def_ref, gf_sc, False),
                                                                 (eb_ref, sb_ref, dsb_ref, dub_ref, deb_ref, gb_sc, True)):
            g = g_sc[...]
            for k in reversed(_scan_order(rev)):
                du_ref[k] = g
                gs = g * s_ref[k]
                de_ref[k] = gs[:, :128] + gs[:, 128:]
                g = _both_halves(e_ref[k]) * g + ds_ref[k]
            g_sc[...] = g

    big = lambda rev: pl.BlockSpec((SCAN_BLOCK, 128, 256), lambda t: (_scan_block(nb - 1 - t, nb, rev), 0, 0))
    small = lambda rev: pl.BlockSpec((SCAN_BLOCK, 128, 128), lambda t: (_scan_block(nb - 1 - t, nb, rev), 0, 0))
    return pl.pallas_call(
        body, name="gla_states_bwd", grid=(nb,),
        in_specs=[small(False), small(True), big(False), big(True), big(False), big(True)],
        out_specs=[big(False), small(False), big(True), small(True)],
        out_shape=[jax.ShapeDtypeStruct(sf.shape, F32), jax.ShapeDtypeStruct(ef.shape, F32)] * 2,
        scratch_shapes=[pltpu.VMEM((128, 256), F32)] * 2,
        compiler_params=_params(("arbitrary",)),
    )(ef, eb, sf, sb, dsf, dsb)


def loss_head(xt, target, fnw):
    t = xt.shape[0]

    def f(x, tg, w):
        y = _rms(x, w)
        return 0.5 * jnp.sum(jnp.square(y - tg)) * (1.0 / D)

    def body(x_ref, t_ref, w_ref, loss_ref, dx_ref, dw_ref):
        i = pl.program_id(0)

        @pl.when(i == 0)
        def _():
            loss_ref[...] = jnp.zeros_like(loss_ref)
            dw_ref[...] = jnp.zeros_like(dw_ref)

        @pl.when(i < NCTXB)
        def _():
            dx_ref[...] = jnp.zeros_like(dx_ref)

        @pl.when(i >= NCTXB)
        def _():
            val, (dx, dw) = jax.value_and_grad(f, argnums=(0, 2))(x_ref[...], t_ref[...], w_ref[...])
            loss_ref[...] += jnp.broadcast_to(val, loss_ref.shape)
            dx_ref[...] = dx
            dw_ref[...] += dw

    return pl.pallas_call(
        body, name="loss_head", grid=(t // TM,),
        in_specs=[pl.BlockSpec((TM, D), lambda i: (i, 0)), pl.BlockSpec((TM, D), lambda i: (jnp.maximum(i - NCTXB, 0), 0)),
                  pl.BlockSpec((1, D), lambda i: (0, 0))],
        out_specs=[pl.BlockSpec((1, 128), lambda i: (0, 0)), pl.BlockSpec((TM, D), lambda i: (i, 0)),
                   pl.BlockSpec((1, D), lambda i: (0, 0))],
        out_shape=[jax.ShapeDtypeStruct((1, 128), F32), jax.ShapeDtypeStruct((t, D), F32), jax.ShapeDtypeStruct((1, D), F32)],
        compiler_params=_params(("arbitrary",)),
    )(xt, target, fnw)


def _in_to_padded(w):
    out, pos = [], 0
    for src, wd, dst in sorted(IN_GROUPS, key=lambda g: g[2]):
        if dst > pos:
            out.append(jnp.zeros((w.shape[0], dst - pos), w.dtype))
        out.append(w[:, src:src + wd])
        pos = dst + wd
    if pos < P_COLS:
        out.append(jnp.zeros((w.shape[0], P_COLS - pos), w.dtype))
    return jnp.concatenate(out, axis=1)


def _in_from_padded(g):
    return jnp.concatenate([g[:, dst:dst + wd] for _, wd, dst in IN_GROUPS], axis=1)


def _uq_to_padded(w):
    return jnp.pad(w.reshape(256, 4, 192), ((0, 0), (0, 0), (0, 64))).reshape(256, 1024)


def _uq_from_padded(g):
    return g.reshape(256, 4, 256)[:, :, :192].reshape(256, 768)


def _ukv_to_padded(w):
    return w.reshape(256, 4, 2, 128).transpose(0, 2, 1, 3).reshape(256, 1024)


def _ukv_from_padded(g):
    return g.reshape(256, 2, 4, 128).transpose(0, 2, 1, 3).reshape(256, 1024)


def _rope_tables(n):
    freq = ROPE_BASE ** (-jnp.arange(16, dtype=F32) * 2.0 / 32.0)
    grid_h = n // GRID_W
    ar = jnp.repeat(jnp.arange(grid_h, dtype=F32)[:, None] * freq[None, :], GRID_W, axis=0)
    ac = jnp.tile(jnp.arange(GRID_W, dtype=F32)[:, None] * freq[None, :], (grid_h, 1))
    z = jnp.zeros((n, 64), F32)
    cs = jnp.concatenate([jnp.cos(ar), jnp.cos(ar), jnp.cos(ac), jnp.cos(ac), z], axis=1)
    sn = jnp.concatenate([-jnp.sin(ar), jnp.sin(ar), -jnp.sin(ac), jnp.sin(ac), z], axis=1)
    cs_c = jnp.concatenate([jnp.ones((CTX, 64), F32), jnp.zeros((CTX, 64), F32)], axis=1)
    return jnp.concatenate([cs_c, cs], axis=0), jnp.concatenate([jnp.zeros((CTX, 128), F32), sn], axis=0)


def _small_views(sp):
    wg = jnp.concatenate([jnp.pad(sp["gla_wg_fwd"], ((0, 112), (0, 0))), jnp.pad(sp["gla_wg_bwd"], ((16, 96), (0, 0)))], axis=1)
    return dict(
        n1w=sp["norm1_w"][None], n2w=sp["norm2_w"][None],
        sgu_nw=sp["sgu_norm_w"][None], sgu_nb=sp["sgu_norm_b"][None], sgu_w=sp["sgu_w"],
        sgu_bm=jnp.repeat(sp["sgu_b"].T, 64, axis=1),
        wg=wg, bg=jnp.concatenate([sp["gla_bg_fwd"], sp["gla_bg_bwd"]])[None],
        gla_nwt=jnp.tile(sp["gla_norm_w"], 4)[None],
        kvw=sp["mla_kv_norm_w"][None], qw=sp["mla_q_norm_w"][None])


def _small_grads(g):
    return dict(
        norm1_w=g["n1w"][0], norm2_w=g["n2w"][0],
        sgu_norm_w=g["sgu_nw"][0], sgu_norm_b=g["sgu_nb"][0], sgu_w=g["sgu_w"],
        sgu_b=g["sgu_bm"].reshape(128, 4, 64).sum(-1).T,
        gla_wg_fwd=g["wg"][0:16, 0:128], gla_wg_bwd=g["wg"][16:32, 128:256],
        gla_bg_fwd=g["bg"][0, 0:128], gla_bg_bwd=g["bg"][0, 128:256],
        gla_norm_w=g["gla_nwt"].reshape(4, 64).sum(0),
        mla_kv_norm_w=g["kvw"][0], mla_q_norm_w=g["qw"][0])


def _big_views(full):
    views = {}
    if "w_in" in full:
        views.update(win=_in_to_padded(full["w_in"]), wuq=_uq_to_padded(full["mla_w_uq"]),
                     wukv=_ukv_to_padded(full["mla_w_ukv"]), wout=full["w_out"])
    if "w_ff1" in full:
        views.update(w1=full["w_ff1"], w2=full["w_ff2"])
    return views


def _gla_tile(t):
    return _pick(t, (768, 512, 256))


def _layer_ops(p, sv, a):
    pc = lambda off, w, tm=TM: rows(p, w, off // w, tm=tm)
    gt = _gla_tile(p.shape[0])
    gr = lambda arr: rows(arr, tm=gt)
    return dict(
        sgu=[pc(P_SU, 256), pc(P_SV, 256), const(sv["sgu_nw"]), const(sv["sgu_nb"]), const(sv["sgu_w"]), const(sv["sgu_bm"])],
        gates=[pc(P_GATE, 128), const(sv["wg"]), const(sv["bg"])],
        mla_pre=[pc(P_CKV, 256), pc(P_DQ, 256), const(sv["kvw"]), const(sv["qw"])],
        gla_kv=lambda: [pc(P_GK, 128, gt), pc(P_GV, 256, gt), gr(a["gf"]), gr(a["gb"])],
        gla_o=lambda: [pc(P_GQ, 128, gt), pc(P_GK, 128, gt), pc(P_GV, 256, gt), gr(a["gf"]), gr(a["gb"]), pc(P_GR, 256, gt),
                       chunks(a["sf"], gt // GLA_CHUNK), chunks(a["sb"], gt // GLA_CHUNK), const(sv["gla_nwt"])],
        mla_post=lambda: [rows(a["kvu"], 512, 0), rows(a["qu"]), pc(P_KR, 128), rows(a["cs"]), rows(a["sn"])])


def layer_fwd(l, xt, modl, bw, sv, tabs, side=(), late=None):
    t = xt.shape[0]
    g, nc, gt = t // TM, t // GLA_CHUNK, _gla_tile(t)
    gg, cpt = t // gt, gt // GLA_CHUNK
    nm = lambda s: f"l{l}_{s}"
    a = dict(x=xt, cs=tabs[0], sn=tabs[1])
    a["h"], = rw(nm("norm1"), fn_norm1, [rows(xt), const(modl), const(sv["n1w"])], [rowout(t, D, BF16)], g)
    p = a["p"] = mm(nm("in_proj"), a["h"], bw["win"], F32)
    ops = _layer_ops(p, sv, a)
    y_sgu, = rw(nm("sgu"), fn_sgu, ops["sgu"], [rowout(t, 256, BF16)], g)
    a["gf"], a["gb"] = rw(nm("gates"), fn_gates, ops["gates"], [rowout(t, 128, F32)] * 2, g)
    a["uf"], a["ef"], a["ub"], a["eb"] = rw(nm("gla_kv"), fn_gla_kv, ops["gla_kv"](),
                                           [chunkout((nc, 128, 256), F32, cpt), chunkout((nc, 128, 128), F32, cpt)] * 2, gg)
    a["sf"], a["sb"] = gla_states(a["uf"], a["ef"], a["ub"], a["eb"])
    y_gla, = rw(nm("gla_o"), fn_gla_o, ops["gla_o"](), [rowout(t, 256, BF16, tm=gt)], gg)
    a["ckvn"], a["dqn"] = rw(nm("mla_pre"), fn_mla_pre, ops["mla_pre"], [rowout(t, 256, BF16)] * 2, g)
    a["kvu"] = mm(nm("kv_up"), a["ckvn"], bw["wukv"], BF16)
    a["qu"] = mm(nm("q_up"), a["dqn"], bw["wuq"], F32)
    a["kcat"], a["q"] = rw(nm("mla_post"), fn_mla_post, ops["mla_post"](), [rowout(t, 1024, BF16)] * 2, g)
    a["o"], a["lse"], side_out = flash_fwd(a["q"], a["kcat"], a["kvu"], side)
    if late is not None:
        late(side_out)
    a["y"] = jnp.concatenate([y_sgu, y_gla, a["o"].astype(BF16)], axis=1)
    a["yo"] = mm(nm("out_proj"), a["y"], bw["wout"], F32)
    a["x1"], a["h2"] = rw(nm("res_norm2"), fn_res_norm2, [rows(xt), rows(a["yo"]), const(modl), const(sv["n2w"])],
                          [rowout(t, D, F32), rowout(t, D, BF16)], g)
    a["act"] = mm(nm("ff1"), a["h2"], bw["w1"], BF16, post=lambda acc: jnp.maximum(acc, 0.0))
    a["f"] = mm(nm("ff2"), a["act"], bw["w2"], F32, pre=_square_bf16)
    x2, = rw(nm("res2"), fn_res2, [rows(a["x1"]), rows(a["f"]), const(modl)], [rowout(t, D, F32)], g)
    return x2, a


def fn_assemble(blk, gv1, gv2, ckv, su, sv_, gr, dq, gk1, gk2, pg, kr, gq):
    return (jnp.concatenate([gv1 + gv2, ckv, su, sv_, gr, dq, gk1 + gk2, pg, kr, gq], axis=1),)


def layer_bwd(l, dx2, a, modl, bw, sv, side_a=None, make_side_b=None, ff_side=None):
    t = dx2.shape[0]
    g, gt = t // TM, _gla_tile(t)
    gg, cpt = t // gt, gt // GLA_CHUNK
    nm = lambda s: f"l{l}_{s}_bwd"
    p = a["p"]
    ops = _layer_ops(p, sv, a)
    gw, gs = {}, {}
    df, dm_a = rw_vjp(nm("res2"), fn_res2, [rows(a["x1"]), rows(a["f"]), const(modl)], [rows(dx2)], [1, 2], g,
                      gdt=[BF16, F32])
    during_attention = []
    if side_a:
        gw["w2"], got_a = mm_tn(nm("ff2_w"), a["act"], df, pre=_square_bf16, side=Side(side_a))
        during_attention += make_side_b(got_a)
    else:
        gw["w2"] = mm_tn(nm("ff2_w"), a["act"], df, pre=_square_bf16)
    du = mm(nm("ff2_x"), df, bw["w2"], BF16, post=lambda acc, act: acc * (2.0 * act.astype(F32)), extras=(a["act"],), bt=True)
    gw["w1"] = mm_tn(nm("ff1_w"), a["h2"], du)
    dh2 = mm(nm("ff1_x"), du, bw["w1"], F32, bt=True)
    dxa, dyo, dm_b, gs["n2w"] = rw_vjp(nm("res_norm2"), fn_res_norm2,
                                       [rows(a["x"]), rows(a["yo"]), const(modl), const(sv["n2w"])],
                                       [rows(dx2), rows(dh2)], [0, 1, 2, 3], g, gdt=[F32, BF16, F32, F32])
    if ff_side is not None:
        ff_entries, make_ff_next = ff_side(dict(w_ff1=gw["w1"], w_ff2=gw["w2"]))
        gw["wout"], got_ff = mm_tn(nm("out_w"), a["y"], dyo, side=Side(ff_entries))
        during_attention += make_ff_next(got_ff)
    else:
        gw["wout"] = mm_tn(nm("out_w"), a["y"], dyo)
    dy = mm(nm("out_x"), dyo, bw["wout"], F32, bt=True)
    dsu, dsv, gs["sgu_nw"], gs["sgu_nb"], gs["sgu_w"], gs["sgu_bm"] = rw_vjp(
        nm("sgu"), fn_sgu, ops["sgu"], [rows(dy, 256, 0)], [0, 1, 2, 3, 4, 5], g)
    dgq, dgk1, dgv1, dgf1, dgb1, dgr, dsf, dsb, gs["gla_nwt"] = rw_vjp(
        nm("gla_o"), fn_gla_o, ops["gla_o"](), [rows(dy, 256, 1, tm=gt)], list(range(9)), gg)
    duf, def_, dub, deb = gla_states_bwd(a["ef"], a["eb"], a["sf"], a["sb"], dsf, dsb)
    dgk2, dgv2, dgf, dgb = rw_vjp(nm("gla_kv"), fn_gla_kv, ops["gla_kv"](),
                                  [chunks(duf, cpt), chunks(def_, cpt), chunks(dub, cpt), chunks(deb, cpt)], [0, 1, 2, 3], gg,
                                  adds={2: rows(dgf1, tm=gt), 3: rows(dgb1, tm=gt)})
    dpg, gs["wg"], gs["bg"] = rw_vjp(nm("gates"), fn_gates, ops["gates"], [rows(dgf), rows(dgb)], [0, 1, 2], g)
    stats, = rw(nm("attn_stats"), fn_attn_stats, [rows(dy, 512, 1), rows(a["o"]), rows(a["lse"])],
                [chunkout((g, 64, TM), F32, 1)], g)
    dq, dkcat, dv, got_b = flash_bwd(a["q"], a["kcat"], a["kvu"], dy, stats,
                                     side=Side(during_attention) if during_attention else None)
    dkk, dqu, dkr = rw_vjp(nm("mla_post"), fn_mla_post, ops["mla_post"](), [rows(dkcat), rows(dq)], [0, 1, 2], g,
                           gdt=[BF16, BF16, F32])
    dkvu = jnp.concatenate([dkk, dv.astype(BF16)], axis=1)
    gw["wukv"] = mm_tn(nm("kv_up_w"), a["ckvn"], dkvu)
    gw["wuq"] = mm_tn(nm("q_up_w"), a["dqn"], dqu)
    dckvn = mm(nm("kv_up_x"), dkvu, bw["wukv"], F32, bt=True)
    ddqn = mm(nm("q_up_x"), dqu, bw["wuq"], F32, bt=True)
    dckv, ddq, gs["kvw"], gs["qw"] = rw_vjp(nm("mla_pre"), fn_mla_pre, ops["mla_pre"], [rows(dckvn), rows(ddqn)],
                                            [0, 1, 2, 3], g)
    dp, = rw(nm("assemble"), fn_assemble,
             [rows(x_) for x_ in (dgv1, dgv2, dckv, dsu, dsv, dgr, ddq, dgk1, dgk2, dpg, dkr, dgq)],
             [rowout(t, P_COLS, BF16)], g)
    gw["win"] = mm_tn(nm("in_w"), a["h"], dp)
    dh = mm(nm("in_x"), dp, bw["win"], F32, bt=True)
    dx, dm_c, gs["n1w"] = rw_vjp(nm("norm1"), fn_norm1, [rows(a["x"]), const(modl), const(sv["n1w"])], [rows(dh)],
                                 [0, 1, 2], g, adds={0: rows(dxa)})
    big = dict(w_in=_in_from_padded(gw["win"]), w_out=gw["wout"], mla_w_uq=_uq_from_padded(gw["wuq"]),
               mla_w_ukv=_ukv_from_padded(gw["wukv"]), w_ff1=gw["w1"], w_ff2=gw["w2"])
    return dx, dm_a + dm_b + dm_c, big, _small_grads(gs), got_b


SMALL_NAMES = ("norm1_w", "sgu_norm_w", "sgu_norm_b", "sgu_w", "sgu_b", "gla_wg_fwd", "gla_bg_fwd", "gla_wg_bwd",
               "gla_bg_bwd", "gla_norm_w", "mla_q_norm_w", "mla_kv_norm_w", "norm2_w")
BIG_NAMES = ("w_in", "w_out", "mla_w_uq", "mla_w_ukv", "w_ff1", "w_ff2")
ATTN_WEIGHTS, FF_WEIGHTS = BIG_NAMES[:4], BIG_NAMES[4:]


def local_step(x, ctx, target, mods, big, small, final_norm_w, side=(), on_side=None, grad_side=None, ff_side=None):
    n = x.shape[0]
    xt = jnp.concatenate([ctx, x], axis=0)
    tabs = _rope_tables(n)
    depth = len(mods)
    big = list(big)
    svs = [_small_views(small[l]) for l in range(depth)]
    acts, bws = [], []
    for l in range(depth):
        bws.append(_big_views(big[l]))
        late = None
        if l == 0 and on_side is not None:
            def late(results):
                rest0, later = on_side(results)
                bws[0].update(_big_views(rest0))
                big.extend(later)
        xt, a = layer_fwd(l, xt, mods[l], bws[l], svs[l], tabs, side if l == 0 else (), late)
        acts.append(a)
    loss, dxt, dfnw = loss_head(xt, target, final_norm_w[None])
    dmods, gbig, gsmall = [None] * depth, [None] * depth, [None] * depth
    got = []
    for l in reversed(range(depth)):
        hooks = (None, None, None)
        if l == 0 and grad_side is not None and depth > 1:
            hooks = (*grad_side(gbig[1:]), ff_side)
        dxt, dmods[l], gbig[l], gsmall[l], got = layer_bwd(l, dxt, acts[l], mods[l], bws[l], svs[l], *hooks)
    return loss, dxt[CTX:], dmods, gbig, gsmall, dfnw, got


def _group(group):
    x, y, c = lax.axis_index("x"), lax.axis_index("y"), lax.axis_index("c")
    if group == "sib":
        return 2, c, [((x, y, 1 - c), 1 - c)]
    if group == "chip":
        flips = [(1, 0), (0, 1), (1, 1)]
        return 4, 2 * x + y, [((x ^ fx, y ^ fy, c), 2 * (x ^ fx) + (y ^ fy)) for fx, fy in flips]
    flips = [(fx, fy, fc) for fx in (0, 1) for fy in (0, 1) for fc in (0, 1)][1:]
    return 8, 4 * x + 2 * y + c, [((x ^ fx, y ^ fy, c ^ fc), 4 * (x ^ fx) + 2 * (y ^ fy) + (c ^ fc)) for fx, fy, fc in flips]


def _group_size(group):
    return {"sib": 2, "chip": 4, "all": 8}[group]


REMOTE_COPIES = {"gather": None, "scatter": None, "swap": 1, "gather2": 6}


def _exchange_shapes(entries):
    n_in = sum(len(arrs) for _, _, arrs in entries)
    n_remote = sum(REMOTE_COPIES[k] or _group_size(g) - 1 for k, g, _ in entries)
    n_local = sum(1 for k, _, _ in entries if k != "swap")
    out_shape = []
    for kind, group, arrs in entries:
        a = arrs[0]
        if kind in ("gather", "gather2"):
            out_shape.append(jax.ShapeDtypeStruct((_group_size(group),) + a.shape, a.dtype))
        elif kind == "swap" and len(arrs) == 1:
            out_shape.append(jax.ShapeDtypeStruct(a.shape[1:], a.dtype))
        else:
            out_shape.append(jax.ShapeDtypeStruct(a.shape, a.dtype))
    return n_in, n_remote, n_local, out_shape


def _exchange_sems(entries):
    _, n_remote, n_local, _ = _exchange_shapes(entries)
    return [pltpu.SemaphoreType.DMA((n_remote,)), pltpu.SemaphoreType.DMA((n_remote,)), pltpu.SemaphoreType.DMA((max(n_local, 1),))]


def _exchange_phases(entries, in_refs, out_refs, send_sems, recv_sems, local_sems):
    x, y, c = lax.axis_index("x"), lax.axis_index("y"), lax.axis_index("c")

    def remote(src, dst, k, dev):
        return pltpu.make_async_remote_copy(src_ref=src, dst_ref=dst, send_sem=send_sems.at[k], recv_sem=recv_sems.at[k],
                                            device_id=dev, device_id_type=MESH)

    pos, k, kl = 0, 0, 0
    starts, forwards, finals = [], [], []
    for (kind, group, arrs), out in zip(entries, out_refs):
        srcs = in_refs[pos:pos + len(arrs)]
        pos += len(arrs)
        _, mine, peers = _group(group)
        if kind == "swap":
            (dev, _), = peers
            if len(srcs) == 1:
                starts.append(remote(srcs[0].at[1 - c], out, k, dev).start)
                finals.append(remote(srcs[0].at[0], out, k, dev).wait)
            else:
                def start_swap(srcs=srcs, k=k, dev=dev, out=out):
                    for core, src in ((0, srcs[1]), (1, srcs[0])):
                        @pl.when(c == core)
                        def _(src=src):
                            remote(src, out, k, dev).start()

                starts.append(start_swap)
                finals.append(remote(srcs[0], out, k, dev).wait)
            k += 1
            continue
        src = srcs[0]
        own = pltpu.make_async_copy(src if kind != "scatter" else src.at[mine], out.at[mine], local_sems.at[kl])
        starts.append(own.start)
        finals.append(own.wait)
        kl += 1
        if kind == "gather2":
            sibling = (x, y, 1 - c)
            for f, (dev, slot) in enumerate(peers):
                starts.append(remote(src.at[c], out.at[mine, c], k + f, dev).start)
                arrival = remote(src.at[c], out.at[slot, c], k + f, dev)

                def forward(arrival=arrival, slot=slot, kf=k + 3 + f, out=out):
                    arrival.wait_recv()
                    remote(out.at[slot, c], out.at[slot, c], kf, sibling).start()

                forwards.append(forward)
                finals.append(arrival.wait_send)
                finals.append(remote(out.at[slot, c], out.at[slot, 1 - c], k + 3 + f, sibling).wait)
            k += 6
            continue
        for dev, slot in peers:
            piece = src if kind == "gather" else src.at[slot]
            starts.append(remote(piece, out.at[mine], k, dev).start)
            finals.append(remote(piece, out.at[slot], k, dev).wait)
            k += 1
    return starts, forwards, finals


class Side:
    def __init__(self, entries):
        self.entries = tuple(entries)
        self.n_in, _, _, self.shapes = _exchange_shapes(self.entries)
        self.n_out = len(self.entries)
        self.arrays = [a for _, _, arrs in self.entries for a in arrs]
        any_spec = pl.BlockSpec(memory_space=pl.ANY)
        self.in_specs, self.out_specs = [any_spec] * self.n_in, [any_spec] * self.n_out
        self.sems = _exchange_sems(self.entries) if self.entries else []

    def start(self, in_refs, out_refs, sem_refs):
        if not self.entries:
            return lambda: None
        ids = [pl.program_id(d) for d in range(3)]
        first = (ids[0] == 0) & (ids[1] == 0) & (ids[2] == 0)
        last = ((ids[0] == pl.num_programs(0) - 1) & (ids[1] == pl.num_programs(1) - 1) & (ids[2] == pl.num_programs(2) - 1))
        starts, forwards, finals = _exchange_phases(self.entries, in_refs, out_refs, *sem_refs)
        assert not forwards

        @pl.when(first)
        def _():
            for run in starts:
                run()

        def finish():
            @pl.when(last)
            def _():
                for run in finals:
                    run()

        return finish


def xchg(name, entries):
    n_in, _, _, out_shape = _exchange_shapes(entries)

    def body(*refs):
        in_refs, out_refs = refs[:n_in], refs[n_in:n_in + len(entries)]
        for phase in _exchange_phases(entries, in_refs, out_refs, *refs[n_in + len(entries):]):
            for run in phase:
                run()

    any_spec = pl.BlockSpec(memory_space=pl.ANY)
    return pl.pallas_call(
        body, name=name,
        in_specs=[any_spec] * n_in, out_specs=[any_spec] * len(entries), out_shape=out_shape,
        scratch_shapes=_exchange_sems(entries),
    )(*[a for _, _, arrs in entries for a in arrs])


def _block_rows(r, c, budget=131072):
    tr = 8
    while tr * 2 * c <= budget and r % (tr * 2) == 0:
        tr *= 2
    return tr if r % tr == 0 else r


def tree_sum(name, parts):
    g, r, c = parts.shape
    tr = _block_rows(r, c)

    def body(p_ref, o_ref):
        p = [p_ref[i].astype(F32) for i in range(g)]
        while len(p) > 1:
            p = [p[i] + p[i + 1] for i in range(0, len(p), 2)]
        o_ref[...] = p[0]

    return pl.pallas_call(
        body, name=name, grid=(r // tr,),
        in_specs=[pl.BlockSpec((g, tr, c), lambda i: (0, i, 0))], out_specs=pl.BlockSpec((tr, c), lambda i: (i, 0)),
        out_shape=jax.ShapeDtypeStruct((r, c), F32), compiler_params=_params(("arbitrary",)),
    )(parts)


def pair_sum(name, halves, recv, core):
    r, c = recv.shape
    tr = _block_rows(r, c)

    def body(h_ref, r_ref, k_ref, o_ref):
        o_ref[...] = (jnp.where(k_ref[...] > 0.5, h_ref[1], h_ref[0]) + r_ref[...]).astype(o_ref.dtype)

    blk = pl.BlockSpec((tr, c), lambda i: (i, 0))
    return pl.pallas_call(
        body, name=name, grid=(r // tr,),
        in_specs=[pl.BlockSpec((2, tr, c), lambda i: (0, i, 0)), blk, pl.BlockSpec((1, 1), lambda i: (0, 0))],
        out_specs=blk, out_shape=jax.ShapeDtypeStruct((r, c), BF16), compiler_params=_params(("arbitrary",)),
    )(halves, recv, core)


def adamw(name, w, g, m, v):
    r, c = w.shape
    tr = _block_rows(r, c)

    def body(w_ref, g_ref, m_ref, v_ref, d_ref, nm_ref, nv_ref):
        gg = g_ref[...]
        nm = ADAM_B1 * m_ref[...] + (1.0 - ADAM_B1) * gg
        nv = ADAM_B2 * v_ref[...] + (1.0 - ADAM_B2) * jnp.square(gg)
        m_hat = nm / (1.0 - ADAM_B1 ** ADAM_STEP)
        v_hat = nv / (1.0 - ADAM_B2 ** ADAM_STEP)
        d_ref[...] = -ADAM_LR * (m_hat / (jnp.sqrt(v_hat) + ADAM_EPS) + ADAM_WD * w_ref[...])
        nm_ref[...] = nm
        nv_ref[...] = nv

    blk = pl.BlockSpec((tr, c), lambda i: (i, 0))
    return pl.pallas_call(
        body, name=name, grid=(r // tr,), in_specs=[blk] * 4, out_specs=[blk] * 3,
        out_shape=[jax.ShapeDtypeStruct((r, c), F32)] * 3, compiler_params=_params(("arbitrary",)),
    )(w, g, m, v)


W_MOD_COLS = 6 * D // 4
MOD_TN = 512


def mod_project(c16, w_mod, b_loc):
    def body(c_ref, w_ref, b_ref, o_ref):
        cv = c_ref[...]
        s = (cv * _sigmoid(cv)).astype(BF16)
        o_ref[0] = _dot(s, w_ref[0].astype(BF16)) + b_ref[0]

    return pl.pallas_call(
        body, name="mod_project", grid=(2, W_MOD_COLS // MOD_TN),
        in_specs=[pl.BlockSpec((16, D), lambda l, j: (0, 0)), pl.BlockSpec((1, D, MOD_TN), lambda l, j: (l, 0, j)),
                  pl.BlockSpec((1, 1, MOD_TN), lambda l, j: (l, 0, j))],
        out_specs=pl.BlockSpec((1, 16, MOD_TN), lambda l, j: (l, 0, j)),
        out_shape=jax.ShapeDtypeStruct((2, 16, W_MOD_COLS), F32), compiler_params=_params(("arbitrary", "arbitrary")),
    )(c16, w_mod, b_loc)


def mod_weight_grad(c16, dm16):
    def body(c_ref, d_ref, o_ref):
        cv = c_ref[...]
        o_ref[0] = _dot(cv * _sigmoid(cv), d_ref[0], ((0,), (0,)), precision=HI)

    return pl.pallas_call(
        body, name="mod_weight_grad", grid=(2, W_MOD_COLS // MOD_TN),
        in_specs=[pl.BlockSpec((16, D), lambda l, j: (0, 0)), pl.BlockSpec((1, 16, MOD_TN), lambda l, j: (l, 0, j))],
        out_specs=pl.BlockSpec((1, D, MOD_TN), lambda l, j: (l, 0, j)),
        out_shape=jax.ShapeDtypeStruct((2, D, W_MOD_COLS), F32), compiler_params=_params(("arbitrary", "arbitrary")),
    )(c16, dm16)


def cctx_partial(dmc, w_mod):
    def body(d_ref, w_ref, o_ref):
        @pl.when(pl.program_id(0) == 0)
        def _():
            o_ref[...] = jnp.zeros_like(o_ref)
        o_ref[...] += _dot(d_ref[0], w_ref[0], ((1,), (1,)), precision=HI)

    return pl.pallas_call(
        body, name="cctx_partial", grid=(2,),
        in_specs=[pl.BlockSpec((1, 8, W_MOD_COLS), lambda l: (l, 0, 0)), pl.BlockSpec((1, D, W_MOD_COLS), lambda l: (l, 0, 0))],
        out_specs=pl.BlockSpec((8, D), lambda l: (0, 0)),
        out_shape=jax.ShapeDtypeStruct((8, D), F32), compiler_params=_params(("arbitrary",)),
    )(dmc, w_mod)


def cctx_grad(parts, c_ctx8):
    def body(p_ref, c_ref, o_ref):
        ds = (p_ref[0] + p_ref[1]) + (p_ref[2] + p_ref[3])
        _, vf = jax.vjp(lambda z: z * _sigmoid(z), c_ref[...])
        o_ref[...] = vf(ds)[0]

    return pl.pallas_call(
        body, name="cctx_grad", out_shape=jax.ShapeDtypeStruct((8, D), F32),
    )(parts, c_ctx8)


ARG_NAMES = ("x", "c", "ctx", "c_ctx", "w_mod", "b_mod", "norm1_w", "w_in", "w_out", "sgu_norm_w", "sgu_norm_b", "sgu_w",
             "sgu_b", "gla_wg_fwd", "gla_bg_fwd", "gla_wg_bwd", "gla_bg_bwd", "gla_norm_w", "mla_q_norm_w", "mla_w_uq",
             "mla_kv_norm_w", "mla_w_ukv", "norm2_w", "w_ff1", "w_ff2", "final_norm_w")
WEIGHT_NAMES = ARG_NAMES[3:]
PACKED = ("c_ctx", "b_mod") + SMALL_NAMES + ("final_norm_w",)
ROW_SHARDED = ("w_out", "w_ff2")
PACK_ROWS = 256


def _pack(vectors):
    flat = jnp.concatenate([v.reshape(-1) for v in vectors])
    n = flat.shape[0]
    total = -(-n // (PACK_ROWS * LANES)) * PACK_ROWS * LANES
    return jnp.pad(flat, (0, total - n)).reshape(-1, LANES)


def _unpack(buf, shapes):
    flat, out, pos = buf.reshape(-1), [], 0
    for shp in shapes:
        n = int(np.prod(shp))
        out.append(flat[pos:pos + n].reshape(shp))
        pos += n
    return out


def _full_weight(name, g):
    if name in ROW_SHARDED:
        return g.reshape(-1, g.shape[-1])
    return g.transpose(1, 0, 2).reshape(g.shape[1], -1)


def _chip_chunks(name, a):
    if name in ROW_SHARDED:
        return a.reshape(4, a.shape[0] // 4, a.shape[1])
    return a.reshape(a.shape[0], 4, a.shape[1] // 4).transpose(1, 0, 2)


def kernel(x, c, ctx, c_ctx, w_mod, b_mod, norm1_w, w_in, w_out, sgu_norm_w, sgu_norm_b, sgu_w, sgu_b, gla_wg_fwd, gla_bg_fwd, gla_wg_bwd, gla_bg_bwd, gla_norm_w, mla_q_norm_w, mla_w_uq, mla_kv_norm_w, mla_w_ukv, norm2_w, w_ff1, w_ff2, final_norm_w, loss_target, m_c_ctx, m_w_mod, m_b_mod, m_norm1_w, m_w_in, m_w_out, m_sgu_norm_w, m_sgu_norm_b, m_sgu_w, m_sgu_b, m_gla_wg_fwd, m_gla_bg_fwd, m_gla_wg_bwd, m_gla_bg_bwd, m_gla_norm_w, m_mla_q_norm_w, m_mla_w_uq, m_mla_kv_norm_w, m_mla_w_ukv, m_norm2_w, m_w_ff1, m_w_ff2, m_final_norm_w, v_c_ctx, v_w_mod, v_b_mod, v_norm1_w, v_w_in, v_w_out, v_sgu_norm_w, v_sgu_norm_b, v_sgu_w, v_sgu_b, v_gla_wg_fwd, v_gla_bg_fwd, v_gla_wg_bwd, v_gla_bg_bwd, v_gla_norm_w, v_mla_q_norm_w, v_mla_w_uq, v_mla_kv_norm_w, v_mla_w_ukv, v_norm2_w, v_w_ff1, v_w_ff2, v_final_norm_w):
    args = (x, c, ctx, c_ctx, w_mod, b_mod, norm1_w, w_in, w_out, sgu_norm_w, sgu_norm_b, sgu_w, sgu_b, gla_wg_fwd, gla_bg_fwd, gla_wg_bwd, gla_bg_bwd, gla_norm_w, mla_q_norm_w, mla_w_uq, mla_kv_norm_w, mla_w_ukv, norm2_w, w_ff1, w_ff2, final_norm_w)
    w = dict(zip(ARG_NAMES, args))
    moms = (m_c_ctx, m_w_mod, m_b_mod, m_norm1_w, m_w_in, m_w_out, m_sgu_norm_w, m_sgu_norm_b, m_sgu_w, m_sgu_b, m_gla_wg_fwd, m_gla_bg_fwd, m_gla_wg_bwd, m_gla_bg_bwd, m_gla_norm_w, m_mla_q_norm_w, m_mla_w_uq, m_mla_kv_norm_w, m_mla_w_ukv, m_norm2_w, m_w_ff1, m_w_ff2, m_final_norm_w)
    vars_ = (v_c_ctx, v_w_mod, v_b_mod, v_norm1_w, v_w_in, v_w_out, v_sgu_norm_w, v_sgu_norm_b, v_sgu_w, v_sgu_b, v_gla_wg_fwd, v_gla_bg_fwd, v_gla_wg_bwd, v_gla_bg_bwd, v_gla_norm_w, v_mla_q_norm_w, v_mla_w_uq, v_mla_kv_norm_w, v_mla_w_ukv, v_norm2_w, v_w_ff1, v_w_ff2, v_final_norm_w)
    m1 = dict(zip(WEIGHT_NAMES, moms))
    m2 = dict(zip(WEIGHT_NAMES, vars_))
    xi, yi, ci = lax.axis_index("x"), lax.axis_index("y"), lax.axis_index("c")
    chip, dev = 2 * xi + yi, 4 * xi + 2 * yi + ci
    depth = w_mod.shape[0]

    def shard_halves(l, names):
        return [("gather2", "chip", [w[k][l].astype(BF16).reshape(2, w[k].shape[1] // 2, w[k].shape[2])]) for k in names]

    def full_weights(names, gathered):
        return {k: _full_weight(k, g.reshape(4, *w[k].shape[1:])) for k, g in zip(names, gathered)}

    got = xchg("gather_inputs", [("gather", "all", [c])] + shard_halves(0, ATTN_WEIGHTS))
    c_all = got[0]
    c16 = jnp.concatenate([c_all.reshape(8, D), c_ctx[None], jnp.zeros((7, D), F32)], axis=0)
    b_loc = lax.dynamic_slice_in_dim(b_mod, chip * W_MOD_COLS, W_MOD_COLS, axis=1)[:, None, :]
    mod_part = mod_project(c16, w_mod, b_loc)
    mod_all, = xchg("gather_mod", [("gather", "chip", [mod_part])])
    mod_full = mod_all.transpose(1, 2, 0, 3).reshape(depth, 16, 6 * D)
    mods = [jnp.stack([mod_full[l, 8], lax.dynamic_index_in_dim(mod_full[l], dev, 0, keepdims=False)])[:, None, :]
            for l in range(depth)]

    small = [{k: w[k][l] for k in SMALL_NAMES} for l in range(depth)]
    n_big = len(BIG_NAMES)
    n_ff = len(FF_WEIGHTS)
    later = shard_halves(0, FF_WEIGHTS) + [e for l in range(1, depth) for e in shard_halves(l, BIG_NAMES)]
    core = ci.astype(F32).reshape(1, 1)

    def on_side(res):
        return (full_weights(FF_WEIGHTS, res[:n_ff]),
                [full_weights(BIG_NAMES, res[n_ff + i * n_big:n_ff + (i + 1) * n_big]) for i in range(depth - 1)])

    def half_major(k, g):
        ch = _chip_chunks(k, g)
        return ch.reshape(4, 2, ch.shape[1] // 2, ch.shape[2]).transpose(1, 0, 2, 3)

    def swap_entries(gb, names):
        hm = [half_major(k, gb[k]) for k in names]
        return hm, [("swap", "sib", [h]) for h in hm]

    def scatter_entries(tag, names, hm, recv):
        out = []
        for k, h, r in zip(names, hm, recv):
            s2 = pair_sum(f"pair_sum_{tag}_{k}", h.reshape(2, -1, h.shape[-1]), r.reshape(-1, r.shape[-1]), core)
            out.append(("scatter", "chip", [s2.reshape(r.shape)]))
        return out

    def grad_side(gb_later):
        hms, entries = [], []
        for gb in gb_later:
            hm, e = swap_entries(gb, BIG_NAMES)
            hms.append(hm)
            entries += e

        def make_scatter(recv):
            return [e for i, hm in enumerate(hms)
                    for e in scatter_entries(f"l{i + 1}", BIG_NAMES, hm, recv[i * n_big:(i + 1) * n_big])]

        return entries, make_scatter

    def ff_side(g_ff):
        hm, entries = swap_entries(g_ff, FF_WEIGHTS)
        return entries, lambda recv: scatter_entries("l0", FF_WEIGHTS, hm, recv)

    loss, grad_x, dmods, gbig, gsmall, dfnw, early_pieces = local_step(
        x[0], ctx[0], loss_target[0], mods, [full_weights(ATTN_WEIGHTS, got[1:])], small, final_norm_w, side=later,
        on_side=on_side, grad_side=grad_side, ff_side=ff_side)

    dm_lat = jnp.stack([dmods[l][1, 0] for l in range(depth)])
    dm_ctx = jnp.stack([dmods[l][0, 0] for l in range(depth)])
    small_pack = _pack([dm_lat, dm_ctx] + [jnp.stack([gsmall[l][k] for l in range(depth)]) for k in SMALL_NAMES] + [dfnw, loss])
    hm0, swap0 = swap_entries(gbig[0], ATTN_WEIGHTS)
    got = xchg("exchange_grads", [("gather", "all", [small_pack])] + swap0)
    small_all, recv0 = got[0], got[1:]
    small_sum = tree_sum("small_grad_sum", small_all)

    n_dm = depth * 6 * D
    dm_rows = n_dm // LANES
    dm_lat_all = small_all[:, :dm_rows].reshape(8, depth, 6 * D)
    dm_ctx_sum = small_sum[dm_rows:2 * dm_rows].reshape(depth, 6 * D)
    take = lambda a: lax.dynamic_slice_in_dim(a, chip * W_MOD_COLS, W_MOD_COLS, axis=-1)
    dmc_loc = take(dm_ctx_sum)
    cc_part = cctx_partial(jnp.pad(dmc_loc[:, None, :], ((0, 0), (0, 7), (0, 0))), w_mod)
    got = xchg("scatter_grads", [("gather", "chip", [cc_part])] + scatter_entries("l0", ATTN_WEIGHTS, hm0, recv0))
    cc_parts = got[0]
    keys = ([(0, k) for k in ATTN_WEIGHTS] + [(l, k) for l in range(1, depth) for k in BIG_NAMES] + [(0, k) for k in FF_WEIGHTS])
    pieces = dict(zip(keys, list(got[1:]) + list(early_pieces)))
    keys = [(l, k) for l in range(depth) for k in BIG_NAMES]
    reduced = {lk: tree_sum(f"chip_sum_l{lk[0]}_{lk[1]}", pieces[lk]) for lk in keys}
    g_c_ctx = cctx_grad(cc_parts, jnp.broadcast_to(c_ctx[None], (8, D)))[0]

    others = dict(zip(keys, xchg("share_halves", [("swap", "sib", [reduced[lk], reduced[lk]]) for lk in keys])))
    shard = {lk: jnp.where(ci == 0, jnp.concatenate([reduced[lk], others[lk]], axis=0),
                           jnp.concatenate([others[lk], reduced[lk]], axis=0)) for lk in keys}
    grads = {k: jnp.stack([shard[(l, k)] for l in range(depth)]) for k in BIG_NAMES}

    dm16 = jnp.concatenate([take(dm_lat_all).transpose(1, 0, 2), dmc_loc[:, None, :], jnp.zeros((depth, 7, W_MOD_COLS), F32)], axis=1)
    grads["w_mod"] = mod_weight_grad(c16, dm16)
    flat_sum = small_sum.reshape(-1)
    g_b_mod = (flat_sum[:n_dm] + flat_sum[n_dm:2 * n_dm]).reshape(depth, 6 * D)
    rest_shapes = [w[k].shape for k in PACKED[2:]]
    n_rest = sum(int(np.prod(s)) for s in rest_shapes)
    for k, g in zip(PACKED, [g_c_ctx, g_b_mod] + _unpack(flat_sum[2 * n_dm:2 * n_dm + n_rest], rest_shapes)):
        grads[k] = g
    loss = flat_sum[2 * n_dm + n_rest]

    delta, new_m, new_v = {}, {}, {}
    for k in BIG_NAMES + ("w_mod",):
        view = lambda a: a.reshape(-1, a.shape[-1])
        d_, m_, v_ = adamw(f"adamw_{k}", view(w[k]), view(grads[k]), view(m1[k]), view(m2[k]))
        delta[k], new_m[k], new_v[k] = d_.reshape(w[k].shape), m_.reshape(w[k].shape), v_.reshape(w[k].shape)
    shapes = [w[k].shape for k in PACKED]
    d_, m_, v_ = adamw("adamw_small", _pack([w[k] for k in PACKED]), _pack([grads[k] for k in PACKED]),
                       _pack([m1[k] for k in PACKED]), _pack([m2[k] for k in PACKED]))
    for k, dk, mk, vk in zip(PACKED, _unpack(d_, shapes), _unpack(m_, shapes), _unpack(v_, shapes)):
        delta[k], new_m[k], new_v[k] = dk, mk, vk
    return (loss, grad_x[None], *[grads[k] for k in WEIGHT_NAMES], *[delta[k] for k in WEIGHT_NAMES],
            *[new_m[k] for k in WEIGHT_NAMES], *[new_v[k] for k in WEIGHT_NAMES])
```

```python
import functools
import math

import numpy as np
import jax
import jax.numpy as jnp
from jax import lax
from jax.experimental import pallas as pl
from jax.experimental.pallas import tpu as pltpu

F32 = jnp.float32
BF16 = jnp.bfloat16
HI = lax.Precision.HIGHEST
EPS = 1e-6
VMEM_LIMIT_BYTES = 56 * 1024 * 1024
LANES = 128

D = 1024
D_FF = 4096
CTX = 256
GRID_W = 64
SGU_CHUNK = 128
GLA_CHUNK = 64
GLA_TAU = 16.0
GLA_DK = 32
MLA_SCALE = (128 + 64) ** -0.5
SCORE_SCALE = MLA_SCALE * math.log2(math.e)
LN2 = math.log(2.0)
ROPE_BASE = 10000.0
TM = 256
NCTXB = CTX // TM
P_GV, P_CKV, P_SU, P_SV, P_GR, P_DQ, P_GK, P_GATE, P_KR, P_GQ = 0, 256, 512, 768, 1024, 1280, 1536, 1664, 1792, 1920
P_COLS = 2048
IN_GROUPS = ((0, 128, P_GK), (128, 256, P_GV), (384, 32, P_GATE), (416, 256, P_CKV), (672, 64, P_KR),
             (736, 256, P_SU), (992, 256, P_SV), (1248, 128, P_GQ), (1376, 256, P_GR), (1632, 256, P_DQ))
ADAM_LR, ADAM_B1, ADAM_B2, ADAM_EPS, ADAM_WD, ADAM_STEP = 0.001, 0.9, 0.999, 1e-08, 0.01, 10
MESH = pl.DeviceIdType.MESH


def _params(sem):
    return pltpu.CompilerParams(dimension_semantics=sem, vmem_limit_bytes=VMEM_LIMIT_BYTES)


def _pick(n, cands):
    for c in cands:
        if n % c == 0:
            return c
    return n


class Op:
    def __init__(self, arr, blk, idx, gshape, gidx, acc):
        self.arr, self.blk, self.idx, self.gshape, self.gidx, self.acc = arr, blk, idx, gshape, gidx, acc

    def spec(self):
        return pl.BlockSpec(self.blk, self.idx)


def rows(arr, width=None, cb=0, off=0, tm=TM):
    w = arr.shape[1] if width is None else width
    n = arr.shape[0] - off * tm
    return Op(arr, (tm, w), lambda i: (i + off, cb), (n, w), lambda i: (i, 0), False)


def blank_rows(n, w, dtype, tm=TM):
    return Op(jnp.zeros((tm, w), dtype), (tm, w), lambda i: (0, 0), (n, w), lambda i: (i, 0), False)


def chunks(arr, per_tile):
    z = (0,) * (arr.ndim - 1)
    return Op(arr, (per_tile,) + arr.shape[1:], lambda i: (i,) + z, arr.shape, lambda i: (i,) + z, False)


def const(arr):
    z = (0,) * arr.ndim
    return Op(arr, arr.shape, lambda i: z, arr.shape, lambda i: z, True)


def rw(name, fn, ins, outs, grid):
    nin = len(ins)

    def body(*refs):
        vals = [r[...] for r in refs[:nin]]
        res = fn(pl.program_id(0), *vals)
        for o, r in zip(refs[nin:], res):
            o[...] = r.astype(o.dtype)

    return pl.pallas_call(
        body, name=name, grid=(grid,),
        in_specs=[o.spec() for o in ins],
        out_specs=[pl.BlockSpec(b, ix) for (_, _, b, ix) in outs],
        out_shape=[jax.ShapeDtypeStruct(s, d) for (s, d, _, _) in outs],
        compiler_params=_params(("arbitrary",)),
    )(*[o.arr for o in ins])


def rowout(n, w, dtype, tm=TM):
    return ((n, w), dtype, (tm, w), lambda i: (i, 0))


def chunkout(shape, dtype, per_tile):
    z = (0,) * (len(shape) - 1)
    return (shape, dtype, (per_tile,) + tuple(shape[1:]), lambda i: (i,) + z)


def rw_vjp(name, fn, ins, cots, wrt, grid, gdt=None, adds=None):
    nin = len(ins)
    cot_ops = [c for c in cots if c is not None]
    add_items = sorted((adds or {}).items())
    gdt = gdt or [F32] * len(wrt)
    ncot, nadd = len(cot_ops), len(add_items)

    def body(*refs):
        i = pl.program_id(0)
        vals = [r[...] for r in refs[:nin]]
        cvals = [r[...] for r in refs[nin:nin + ncot]]
        avals = [r[...] for r in refs[nin + ncot:nin + ncot + nadd]]
        grefs = refs[nin + ncot + nadd:]

        def f(*d):
            a = list(vals)
            for k, dv in zip(wrt, d):
                a[k] = dv
            return tuple(fn(i, *a))

        outs, vf = jax.vjp(f, *[vals[k] for k in wrt])
        it = iter(cvals)
        ct = tuple(jnp.zeros_like(o) if c is None else next(it).astype(o.dtype) for c, o in zip(cots, outs))
        gs = list(vf(ct))
        for (pos, _), av in zip(add_items, avals):
            gs[pos] = gs[pos].astype(F32) + av.astype(F32)
        for pos, (k, g, gref) in enumerate(zip(wrt, gs, grefs)):
            if ins[k].acc:
                @pl.when(i == 0)
                def _():
                    gref[...] = jnp.zeros_like(gref)
                gref[...] += g.astype(gref.dtype)
            else:
                gref[...] = g.astype(gref.dtype)

    all_in = list(ins) + cot_ops + [op for _, op in add_items]
    return pl.pallas_call(
        body, name=name, grid=(grid,),
        in_specs=[o.spec() for o in all_in],
        out_specs=[pl.BlockSpec(ins[k].blk, ins[k].gidx) for k in wrt],
        out_shape=[jax.ShapeDtypeStruct(ins[k].gshape, dt) for k, dt in zip(wrt, gdt)],
        compiler_params=_params(("arbitrary",)),
    )(*[o.arr for o in all_in])


MM_VMEM_BUDGET = 40 * 1024 * 1024
MM_COLS = 1024


def _square_bf16(a):
    a = a.astype(F32)
    return (a * a).astype(BF16)


def mm(name, a, b, out_dtype, pre=None, post=None, extras=(), bt=False):
    m, k = a.shape
    n = b.shape[0] if bt else b.shape[1]
    nc = min(n, MM_COLS)
    row_bytes = k * a.dtype.itemsize + n * jnp.dtype(out_dtype).itemsize + sum(n * e.dtype.itemsize for e in extras)
    tm = next(t for t in (768, 512, 384, 256, 128, 64)
              if m % t == 0 and 2 * t * row_bytes + 2 * k * n * b.dtype.itemsize + t * nc * 4 <= MM_VMEM_BUDGET)

    def body(a_ref, b_ref, *rest):
        o_ref = rest[-1]
        av = a_ref[...]
        if pre is not None:
            av = pre(av)
        for j in range(n // nc):
            cs = slice(j * nc, (j + 1) * nc)
            if bt:
                acc = lax.dot_general(av, b_ref[cs, :], (((1,), (1,)), ((), ())), preferred_element_type=F32)
            else:
                acc = lax.dot_general(av, b_ref[:, cs], (((1,), (0,)), ((), ())), preferred_element_type=F32)
            if post is not None:
                acc = post(acc, *[e[:, cs] for e in rest[:-1]])
            o_ref[:, cs] = acc.astype(o_ref.dtype)

    row = lambda w: pl.BlockSpec((tm, w), lambda i: (i, 0))
    return pl.pallas_call(
        body, name=name, grid=(m // tm,),
        in_specs=[row(k), pl.BlockSpec(b.shape, lambda i: (0, 0))] + [row(n) for _ in extras],
        out_specs=row(n),
        out_shape=jax.ShapeDtypeStruct((m, n), out_dtype),
        compiler_params=_params(("arbitrary",)),
    )(a, b, *extras)


def mm_tn(name, a, b, pre=None, side=None):
    m, ka = a.shape
    _, nb = b.shape
    tm = _pick(m, (768, 512, 256))
    ta = _pick(ka, (2048, 1024, 512, 256, 128))
    tb = _pick(nb, tuple(t for t in (4096, 2048, 1024, 512, 256, 128) if ta * t * 4 <= 8 * 1024 * 1024))
    sd = side or Side(())

    def body(a_ref, b_ref, *rest):
        o_ref = rest[sd.n_in]
        finish = sd.start(rest[:sd.n_in], rest[sd.n_in + 1:sd.n_in + 1 + sd.n_out], rest[sd.n_in + 1 + sd.n_out:])

        @pl.when(pl.program_id(2) == 0)
        def _():
            o_ref[...] = jnp.zeros_like(o_ref)
        av = a_ref[...] if pre is None else pre(a_ref[...])
        o_ref[...] += lax.dot_general(av, b_ref[...], (((0,), (0,)), ((), ())), preferred_element_type=F32)
        finish()

    res = pl.pallas_call(
        body, name=name, grid=(ka // ta, nb // tb, m // tm),
        in_specs=[pl.BlockSpec((tm, ta), lambda i, j, k: (k, i)), pl.BlockSpec((tm, tb), lambda i, j, k: (k, j))] + sd.in_specs,
        out_specs=[pl.BlockSpec((ta, tb), lambda i, j, k: (i, j))] + sd.out_specs,
        out_shape=[jax.ShapeDtypeStruct((ka, nb), F32)] + sd.shapes,
        scratch_shapes=sd.sems,
        compiler_params=_params(("arbitrary", "arbitrary", "arbitrary")),
    )(a, b, *sd.arrays)
    return res[0] if side is None else (res[0], list(res[1:]))


def _rms(x, w):
    return x * lax.rsqrt(jnp.mean(x * x, axis=-1, keepdims=True) + EPS) * w


def _mod_of(blk, m):
    return jnp.where(blk < NCTXB, m[0], m[1])


def _gelu(x):
    return x * (0.5 * (1.0 + jnp.tanh(math.sqrt(2.0 / math.pi) * (x + 0.044715 * (x * x * x)))))


def _sigmoid(x):
    return 1.0 / (1.0 + jnp.exp(-x))


def _log_sigmoid(z):
    return jnp.minimum(z, 0.0) - jnp.log(1.0 + jnp.exp(-jnp.abs(z)))


def _dot(a, b, dims=((1,), (0,)), precision=None):
    return lax.dot_general(a, b, (dims, ((), ())), precision=precision, preferred_element_type=F32)


def _lane_group_mask(width, group, h):
    lane = lax.broadcasted_iota(jnp.int32, (1, width), 1)
    return (lane >= h * group) & (lane < (h + 1) * group)


def fn_norm1(blk, x, m, nw):
    mv = _mod_of(blk, m)
    return ((_rms(x, nw) * (1.0 + mv[:, D:2 * D]) + mv[:, 0:D]),)


def fn_res_norm2(blk, x, yo, m, nw):
    mv = _mod_of(blk, m)
    x1 = x + mv[:, 2 * D:3 * D] * yo
    return x1, _rms(x1, nw) * (1.0 + mv[:, 4 * D:5 * D]) + mv[:, 3 * D:4 * D]


def fn_res2(blk, x1, f, m):
    mv = _mod_of(blk, m)
    return (x1 + mv[:, 5 * D:6 * D] * f,)


def fn_sgu(blk, su, sv, nw, nb, ws, bm):
    u = _gelu(su)
    g = _gelu(sv)
    mu = jnp.mean(g, axis=-1, keepdims=True)
    var = jnp.mean(jnp.square(g - mu), axis=-1, keepdims=True)
    v = (g - mu) * lax.rsqrt(var + EPS) * nw + nb
    out = []
    for c in range(su.shape[0] // SGU_CHUNK):
        vc = v[c * SGU_CHUNK:(c + 1) * SGU_CHUNK]
        s = bm
        for h in range(4):
            vh = jnp.where(_lane_group_mask(256, 64, h), vc, 0.0)
            s = s + _dot(ws[h].astype(BF16), vh.astype(BF16))
        out.append(u[c * SGU_CHUNK:(c + 1) * SGU_CHUNK] * s)
    return (jnp.concatenate(out, axis=0),)


def fn_gates(blk, pg, wg, bg):
    z = _dot(pg.astype(BF16), wg.astype(BF16)) + bg
    g = _log_sigmoid(z) * (1.0 / GLA_TAU)
    return g[:, :128], g[:, 128:]


def _scan_rows(x, rev):
    n = x.shape[0]
    row = lax.broadcasted_iota(jnp.int32, x.shape, 0)
    d = 1
    while d < n:
        if rev:
            x = x + jnp.where(row < n - d, pltpu.roll(x, n - d, 0), 0.0)
        else:
            x = x + jnp.where(row >= d, pltpu.roll(x, d, 0), 0.0)
        d *= 2
    return x


@functools.partial(jax.custom_vjp, nondiff_argnums=(1,))
def _cumsum_rows(x, rev):
    return _scan_rows(x, rev)


def _cumsum_rows_fwd(x, rev):
    return _scan_rows(x, rev), None


def _cumsum_rows_bwd(rev, _, dy):
    return (_scan_rows(dy, not rev),)


_cumsum_rows.defvjp(_cumsum_rows_fwd, _cumsum_rows_bwd)


def _gla_chunk_terms(g, rev):
    return _cumsum_rows(g, rev), jnp.sum(g, axis=0, keepdims=True)


def _bd_mask():
    r = lax.broadcasted_iota(jnp.int32, (128, 256), 0)
    c = lax.broadcasted_iota(jnp.int32, (128, 256), 1)
    return (r // GLA_DK) == (c // 64)


def _gla_kv_chunk(k, v, g, rev):
    b, tot = _gla_chunk_terms(g, rev)
    kd = k * jnp.exp(tot - b)
    u = jnp.where(_bd_mask(), _dot(kd.astype(BF16), v.astype(BF16), ((0,), (0,))), 0.0)
    r = lax.broadcasted_iota(jnp.int32, (128, 128), 0)
    c = lax.broadcasted_iota(jnp.int32, (128, 128), 1)
    col = jnp.sum(jnp.where(r == c, jnp.broadcast_to(jnp.exp(tot), (128, 128)), 0.0), axis=1, keepdims=True)
    return u, jnp.broadcast_to(col, (128, 128))


def _gla_o_chunk(q, k, v, g, s, rev):
    b, _ = _gla_chunk_terms(g, rev)
    qe = q * jnp.exp(b) * (GLA_DK ** -0.5)
    ke = k * jnp.exp(-b)
    o = _dot(qe.astype(BF16), jnp.where(_bd_mask(), s, 0.0).astype(BF16))
    qs = jnp.concatenate([jnp.where(_lane_group_mask(128, GLA_DK, h), qe, 0.0) for h in range(4)], axis=0)
    a = _dot(qs.astype(BF16), ke.astype(BF16), ((1,), (1,)))
    i = lax.broadcasted_iota(jnp.int32, a.shape, 0) % GLA_CHUNK
    j = lax.broadcasted_iota(jnp.int32, a.shape, 1)
    a = jnp.where((j >= i) if rev else (j <= i), a, 0.0)
    av = _dot(a.astype(BF16), v.astype(BF16))
    for h in range(4):
        o = o + jnp.where(_lane_group_mask(256, 64, h), av[GLA_CHUNK * h:GLA_CHUNK * (h + 1)], 0.0)
    return o


def fn_gla_kv(blk, k, v, gf, gb):
    uf, ef, ub, eb = [], [], [], []
    for c in range(k.shape[0] // GLA_CHUNK):
        sl = slice(c * GLA_CHUNK, (c + 1) * GLA_CHUNK)
        u, e = _gla_kv_chunk(k[sl], v[sl], gf[sl], False)
        uf.append(u[None]); ef.append(e[None])
        u, e = _gla_kv_chunk(k[sl], v[sl], gb[sl], True)
        ub.append(u[None]); eb.append(e[None])
    cat = lambda t: jnp.concatenate(t, axis=0)
    return cat(uf), cat(ef), cat(ub), cat(eb)


def fn_gla_o(blk, q, k, v, gf, gb, gr, sf, sb, nwt):
    out = []
    for c in range(q.shape[0] // GLA_CHUNK):
        sl = slice(c * GLA_CHUNK, (c + 1) * GLA_CHUNK)
        out.append(_gla_o_chunk(q[sl], k[sl], v[sl], gf[sl], sf[c], False)
                   + _gla_o_chunk(q[sl], k[sl], v[sl], gb[sl], sb[c], True))
    o = jnp.concatenate(out, axis=0)
    r = lax.broadcasted_iota(jnp.int32, (256, 256), 0)
    c = lax.broadcasted_iota(jnp.int32, (256, 256), 1)
    head_mean = jnp.where((r // 64) == (c // 64), 1.0 / 64.0, 0.0).astype(F32)
    ms = _dot(o * o, head_mean, precision=HI)
    on = o * lax.rsqrt(ms + EPS) * nwt
    return (on * (gr * _sigmoid(gr)),)


def _rope_partner(x):
    lane = lax.broadcasted_iota(jnp.int32, x.shape, 1)
    return jnp.where((lane // 16) % 2 == 0, pltpu.roll(x, LANES - 16, 1), pltpu.roll(x, 16, 1))


@jax.custom_vjp
def _rope(x, cs, sn):
    return x * cs + _rope_partner(x) * sn


def _rope_fwd(x, cs, sn):
    return _rope(x, cs, sn), (cs, sn)


def _rope_bwd(res, dy):
    cs, sn = res
    return dy * cs + _rope_partner(dy * sn), jnp.zeros_like(cs), jnp.zeros_like(sn)


_rope.defvjp(_rope_fwd, _rope_bwd)


def fn_mla_pre(blk, ckv, dq, kvw, qw):
    return _rms(ckv, kvw), _rms(dq, qw)


def fn_mla_post(blk, kk, qu, kr, cs, sn):
    kro = _rope(kr, cs, sn)
    kcat, q = [], []
    for h in range(4):
        kcat += [kk[:, 128 * h:128 * (h + 1)].astype(F32), kro]
        q += [qu[:, 256 * h:256 * h + 128], _rope(qu[:, 256 * h + 128:256 * (h + 1)], cs, sn)]
    return jnp.concatenate(kcat, axis=1), jnp.concatenate(q, axis=1) * SCORE_SCALE


ATTN_ROWS = 256
NEG = -1e30


def _scores(q, k, k0, context_queries):
    s = _dot(q, k, ((1,), (1,)))
    if context_queries is not None:
        col = k0 + lax.broadcasted_iota(jnp.int32, s.shape, 1)
        s = jnp.where(context_queries & (col >= CTX), NEG, s)
    return s


def flash_fwd(q, kcat, kvu, side=()):
    t = q.shape[0]
    tq = _pick(t, (768, 512, 256))
    tk = _pick(t, (2816, 1536, 768, 512, 256))
    nsub = tq // ATTN_ROWS
    n_side_in, _, _, side_shapes = _exchange_shapes(side)
    n_side = len(side)

    def body(q_ref, k_ref, v_ref, *rest):
        side_in, rest = rest[:n_side_in], rest[n_side_in:]
        o_ref, lse_ref = rest[:2]
        side_out, (m_sc, l_sc, acc_sc), side_sems = rest[2:2 + n_side], rest[2 + n_side:5 + n_side], rest[5 + n_side:]
        h, qi, ki = pl.program_id(0), pl.program_id(1), pl.program_id(2)
        if side:
            starts, forwards, finals = _exchange_phases(side, side_in, side_out, *side_sems)
            at_tile0 = (qi == 0) & (ki == 0)
            last = (h == pl.num_programs(0) - 1) & (qi == pl.num_programs(1) - 1) & (ki == pl.num_programs(2) - 1)
            for when, phase in (((h == 0) & at_tile0, starts), ((h == 2) & at_tile0, forwards)):
                @pl.when(when)
                def _(phase=phase):
                    for run in phase:
                        run()

        @pl.when(ki == 0)
        def _():
            m_sc[...] = jnp.full_like(m_sc, NEG)
            l_sc[...] = jnp.zeros_like(l_sc)
            acc_sc[...] = jnp.zeros_like(acc_sc)

        k, v = k_ref[...], v_ref[...]
        chains = [pl.ds(r * ATTN_ROWS, ATTN_ROWS) for r in range(nsub)]
        scores = [_scores(q_ref[rs, :], k, ki * tk, (qi == 0) if r == 0 else None) for r, rs in enumerate(chains)]
        probs = []
        for rs, s in zip(chains, scores):
            m_old = m_sc[rs, :]
            m_new = jnp.maximum(m_old, jnp.max(s, axis=-1, keepdims=True))
            alpha = jnp.exp2(m_old - m_new)
            p = jnp.exp2(s - m_new)
            l_sc[rs, :] = alpha * l_sc[rs, :] + jnp.sum(p, axis=-1, keepdims=True)
            m_sc[rs, :] = m_new
            probs.append((alpha, p.astype(BF16)))
        for rs, (alpha, p) in zip(chains, probs):
            acc_sc[rs, :] = alpha * acc_sc[rs, :] + _dot(p, v)

        @pl.when(ki == pl.num_programs(2) - 1)
        def _():
            o_ref[...] = acc_sc[...] / l_sc[...]
            lse_ref[...] = jnp.broadcast_to(m_sc[...] + jnp.log2(l_sc[...]), lse_ref.shape)

        if side:
            @pl.when(last)
            def _():
                for run in finals:
                    run()

    any_spec = pl.BlockSpec(memory_space=pl.ANY)
    res = pl.pallas_call(
        body, name="mla_flash_fwd", grid=(4, t // tq, t // tk),
        in_specs=[pl.BlockSpec((tq, 256), lambda h, i, j: (i, h)), pl.BlockSpec((tk, 256), lambda h, i, j: (j, h)),
                  pl.BlockSpec((tk, 128), lambda h, i, j: (j, 4 + h))] + [any_spec] * n_side_in,
        out_specs=[pl.BlockSpec((tq, 128), lambda h, i, j: (i, h)), pl.BlockSpec((tq, 128), lambda h, i, j: (i, h))]
        + [any_spec] * n_side,
        out_shape=[jax.ShapeDtypeStruct((t, 512), F32), jax.ShapeDtypeStruct((t, 512), F32)] + side_shapes,
        scratch_shapes=[pltpu.VMEM((tq, 1), F32), pltpu.VMEM((tq, 1), F32), pltpu.VMEM((tq, 128), F32)]
        + (_exchange_sems(side) if side else []),
        compiler_params=_params(("arbitrary", "arbitrary", "arbitrary")),
    )(q, kcat, kvu, *[a for _, _, arrs in side for a in arrs])
    return res[0], res[1], list(res[2:])


def fn_attn_stats(blk, do, o, lse):
    blocks = []
    for u in range(do.shape[0] // TM):
        rs = slice(u * TM, (u + 1) * TM)
        out = []
        for h in range(4):
            hs = slice(128 * h, 128 * (h + 1))
            d = jnp.sum(do[rs, hs] * o[rs, hs], axis=-1, keepdims=True)
            out.append(lse[rs, hs].T[0:8])
            out.append(jnp.broadcast_to(d, (TM, 128)).T[0:8])
        blocks.append(jnp.concatenate(out, axis=0)[None])
    return (jnp.concatenate(blocks, axis=0),)


def flash_bwd(q, kcat, kvu, dy, stats, side=None):
    t = q.shape[0]
    tq = _pick(t, (2816, 768, 512, 256))
    tk = _pick(t, (768, 512, 256))
    nst = tq // TM
    sd = side or Side(())

    def body(q_ref, k_ref, v_ref, do_ref, st_ref, *rest):
        dq_ref, dk_ref, dv_ref = rest[sd.n_in:sd.n_in + 3]
        finish = sd.start(rest[:sd.n_in], rest[sd.n_in + 3:sd.n_in + 3 + sd.n_out], rest[sd.n_in + 3 + sd.n_out:])
        kj, qi = pl.program_id(1), pl.program_id(2)

        @pl.when(qi == 0)
        def _():
            dk_ref[...] = jnp.zeros_like(dk_ref)
            dv_ref[...] = jnp.zeros_like(dv_ref)

        def step(has_context_queries):
            q_, k, v, do = q_ref[...], k_ref[...], v_ref[...], do_ref[...].astype(BF16)
            lse_row = jnp.concatenate([st_ref[u, 0:1, :] for u in range(nst)], axis=1)
            delta_row = jnp.concatenate([st_ref[u, 8:9, :] for u in range(nst)], axis=1)
            s = _dot(k, q_, ((1,), (1,)))
            if has_context_queries:
                key = kj * tk + lax.broadcasted_iota(jnp.int32, s.shape, 0)
                qry = lax.broadcasted_iota(jnp.int32, s.shape, 1)
                s = jnp.where((qry < CTX) & (key >= CTX), NEG, s)
            p = jnp.exp2(s - lse_row)
            dp = _dot(v, do, ((1,), (1,)))
            ds = (p * (dp - delta_row)).astype(BF16)
            dv_ref[...] += _dot(p.astype(BF16), do)
            dk_ref[...] += LN2 * _dot(ds, q_)
            dq_new = LN2 * _dot(ds, k, ((0,), (0,)))
            rows_ = pl.ds(pl.multiple_of(qi * tq, TM), tq)

            @pl.when(kj == 0)
            def _():
                dq_ref[rows_, :] = dq_new

            @pl.when(kj != 0)
            def _():
                dq_ref[rows_, :] += dq_new

        pl.when(qi == 0)(lambda: step(True))
        pl.when(qi != 0)(lambda: step(False))
        finish()

    res = pl.pallas_call(
        body, name="mla_flash_bwd", grid=(4, t // tk, t // tq),
        in_specs=[pl.BlockSpec((tq, 256), lambda h, j, i: (i, h)), pl.BlockSpec((tk, 256), lambda h, j, i: (j, h)),
                  pl.BlockSpec((tk, 128), lambda h, j, i: (j, 4 + h)), pl.BlockSpec((tq, 128), lambda h, j, i: (i, 4 + h)),
                  pl.BlockSpec((nst, 16, 256), lambda h, j, i: (i, h, 0))] + sd.in_specs,
        out_specs=[pl.BlockSpec((t, 256), lambda h, j, i: (0, h)), pl.BlockSpec((tk, 256), lambda h, j, i: (j, h)),
                   pl.BlockSpec((tk, 128), lambda h, j, i: (j, h))] + sd.out_specs,
        out_shape=[jax.ShapeDtypeStruct((t, 1024), F32), jax.ShapeDtypeStruct((t, 1024), F32),
                   jax.ShapeDtypeStruct((t, 512), F32)] + sd.shapes,
        scratch_shapes=sd.sems,
        compiler_params=_params(("arbitrary", "arbitrary", "arbitrary")),
    )(q, kcat, kvu, dy, stats, *sd.arrays)
    return res[0], res[1], res[2], list(res[3:])


SCAN_BLOCK = CTX // GLA_CHUNK


def _scan_block(t, nb, rev):
    if not rev:
        return t
    return jnp.where(t < 1, 0, nb - t)


def _scan_order(rev):
    return tuple(reversed(range(SCAN_BLOCK))) if rev else tuple(range(SCAN_BLOCK))


def _both_halves(e):
    return jnp.concatenate([e, e], axis=1)


def gla_states(uf, ef, ub, eb):
    nb = uf.shape[0] // SCAN_BLOCK

    def body(uf_ref, ef_ref, ub_ref, eb_ref, sf_ref, sb_ref, sf_sc, sb_sc):
        @pl.when(pl.program_id(0) == 0)
        def _():
            sf_sc[...] = jnp.zeros_like(sf_sc)
            sb_sc[...] = jnp.zeros_like(sb_sc)

        for u_ref, e_ref, s_ref, sc, rev in ((uf_ref, ef_ref, sf_ref, sf_sc, False), (ub_ref, eb_ref, sb_ref, sb_sc, True)):
            s = sc[...]
            for c in _scan_order(rev):
                s_ref[c] = s
                s = _both_halves(e_ref[c]) * s + u_ref[c]
            sc[...] = s

    big = lambda rev: pl.BlockSpec((SCAN_BLOCK, 128, 256), lambda t: (_scan_block(t, nb, rev), 0, 0))
    small = lambda rev: pl.BlockSpec((SCAN_BLOCK, 128, 128), lambda t: (_scan_block(t, nb, rev), 0, 0))
    return pl.pallas_call(
        body, name="gla_states", grid=(nb,),
        in_specs=[big(False), small(False), big(True), small(True)],
        out_specs=[big(False), big(True)],
        out_shape=[jax.ShapeDtypeStruct(uf.shape, F32)] * 2,
        scratch_shapes=[pltpu.VMEM((128, 256), F32)] * 2,
        compiler_params=_params(("arbitrary",)),
    )(uf, ef, ub, eb)


def gla_states_bwd(ef, eb, sf, sb, dsf, dsb):
    nb = ef.shape[0] // SCAN_BLOCK

    def body(ef_ref, eb_ref, sf_ref, sb_ref, dsf_ref, dsb_ref, duf_ref, def_ref, dub_ref, deb_ref, gf_sc, gb_sc):
        @pl.when(pl.program_id(0) == 0)
        def _():
            gf_sc[...] = jnp.zeros_like(gf_sc)
            gb_sc[...] = jnp.zeros_like(gb_sc)

        for e_ref, s_ref, ds_ref, du_ref, de_ref, g_sc, rev in ((ef_ref, sf_ref, dsf_ref, duf_ref, def_ref, gf_sc, False),
                                                                 (eb_ref, sb_ref, dsb_ref, dub_ref, deb_ref, gb_sc, True)):
            g = g_sc[...]
            for k in reversed(_scan_order(rev)):
                du_ref[k] = g
                gs = g * s_ref[k]
                de_ref[k] = gs[:, :128] + gs[:, 128:]
                g = _both_halves(e_ref[k]) * g + ds_ref[k]
            g_sc[...] = g

    big = lambda rev: pl.BlockSpec((SCAN_BLOCK, 128, 256), lambda t: (_scan_block(nb - 1 - t, nb, rev), 0, 0))
    small = lambda rev: pl.BlockSpec((SCAN_BLOCK, 128, 128), lambda t: (_scan_block(nb - 1 - t, nb, rev), 0, 0))
    return pl.pallas_call(
        body, name="gla_states_bwd", grid=(nb,),
        in_specs=[small(False), small(True), big(False), big(True), big(False), big(True)],
        out_specs=[big(False), small(False), big(True), small(True)],
        out_shape=[jax.ShapeDtypeStruct(sf.shape, F32), jax.ShapeDtypeStruct(ef.shape, F32)] * 2,
        scratch_shapes=[pltpu.VMEM((128, 256), F32)] * 2,
        compiler_params=_params(("arbitrary",)),
    )(ef, eb, sf, sb, dsf, dsb)


def loss_head(xt, target, fnw):
    t = xt.shape[0]

    def f(x, tg, w):
        y = _rms(x, w)
        return 0.5 * jnp.sum(jnp.square(y - tg)) * (1.0 / D)

    def body(x_ref, t_ref, w_ref, loss_ref, dx_ref, dw_ref):
        i = pl.program_id(0)

        @pl.when(i == 0)
        def _():
            loss_ref[...] = jnp.zeros_like(loss_ref)
            dw_ref[...] = jnp.zeros_like(dw_ref)

        @pl.when(i < NCTXB)
        def _():
            dx_ref[...] = jnp.zeros_like(dx_ref)

        @pl.when(i >= NCTXB)
        def _():
            val, (dx, dw) = jax.value_and_grad(f, argnums=(0, 2))(x_ref[...], t_ref[...], w_ref[...])
            loss_ref[...] += jnp.broadcast_to(val, loss_ref.shape)
            dx_ref[...] = dx
            dw_ref[...] += dw

    return pl.pallas_call(
        body, name="loss_head", grid=(t // TM,),
        in_specs=[pl.BlockSpec((TM, D), lambda i: (i, 0)), pl.BlockSpec((TM, D), lambda i: (jnp.maximum(i - NCTXB, 0), 0)),
                  pl.BlockSpec((1, D), lambda i: (0, 0))],
        out_specs=[pl.BlockSpec((1, 128), lambda i: (0, 0)), pl.BlockSpec((TM, D), lambda i: (i, 0)),
                   pl.BlockSpec((1, D), lambda i: (0, 0))],
        out_shape=[jax.ShapeDtypeStruct((1, 128), F32), jax.ShapeDtypeStruct((t, D), F32), jax.ShapeDtypeStruct((1, D), F32)],
        compiler_params=_params(("arbitrary",)),
    )(xt, target, fnw)


def _in_to_padded(w):
    out, pos = [], 0
    for src, wd, dst in sorted(IN_GROUPS, key=lambda g: g[2]):
        if dst > pos:
            out.append(jnp.zeros((w.shape[0], dst - pos), w.dtype))
        out.append(w[:, src:src + wd])
        pos = dst + wd
    if pos < P_COLS:
        out.append(jnp.zeros((w.shape[0], P_COLS - pos), w.dtype))
    return jnp.concatenate(out, axis=1)


def _in_from_padded(g):
    return jnp.concatenate([g[:, dst:dst + wd] for _, wd, dst in IN_GROUPS], axis=1)


def _uq_to_padded(w):
    return jnp.pad(w.reshape(256, 4, 192), ((0, 0), (0, 0), (0, 64))).reshape(256, 1024)


def _uq_from_padded(g):
    return g.reshape(256, 4, 256)[:, :, :192].reshape(256, 768)


def _ukv_to_padded(w):
    return w.reshape(256, 4, 2, 128).transpose(0, 2, 1, 3).reshape(256, 1024)


def _ukv_from_padded(g):
    return g.reshape(256, 2, 4, 128).transpose(0, 2, 1, 3).reshape(256, 1024)


def _rope_tables(n):
    freq = ROPE_BASE ** (-jnp.arange(16, dtype=F32) * 2.0 / 32.0)
    grid_h = n // GRID_W
    ar = jnp.repeat(jnp.arange(grid_h, dtype=F32)[:, None] * freq[None, :], GRID_W, axis=0)
    ac = jnp.tile(jnp.arange(GRID_W, dtype=F32)[:, None] * freq[None, :], (grid_h, 1))
    z = jnp.zeros((n, 64), F32)
    cs = jnp.concatenate([jnp.cos(ar), jnp.cos(ar), jnp.cos(ac), jnp.cos(ac), z], axis=1)
    sn = jnp.concatenate([-jnp.sin(ar), jnp.sin(ar), -jnp.sin(ac), jnp.sin(ac), z], axis=1)
    cs_c = jnp.concatenate([jnp.ones((CTX, 64), F32), jnp.zeros((CTX, 64), F32)], axis=1)
    return jnp.concatenate([cs_c, cs], axis=0), jnp.concatenate([jnp.zeros((CTX, 128), F32), sn], axis=0)


def _small_views(sp):
    wg = jnp.concatenate([jnp.pad(sp["gla_wg_fwd"], ((0, 112), (0, 0))), jnp.pad(sp["gla_wg_bwd"], ((16, 96), (0, 0)))], axis=1)
    return dict(
        n1w=sp["norm1_w"][None], n2w=sp["norm2_w"][None],
        sgu_nw=sp["sgu_norm_w"][None], sgu_nb=sp["sgu_norm_b"][None], sgu_w=sp["sgu_w"],
        sgu_bm=jnp.repeat(sp["sgu_b"].T, 64, axis=1),
        wg=wg, bg=jnp.concatenate([sp["gla_bg_fwd"], sp["gla_bg_bwd"]])[None],
        gla_nwt=jnp.tile(sp["gla_norm_w"], 4)[None],
        kvw=sp["mla_kv_norm_w"][None], qw=sp["mla_q_norm_w"][None])


def _small_grads(g):
    return dict(
        norm1_w=g["n1w"][0], norm2_w=g["n2w"][0],
        sgu_norm_w=g["sgu_nw"][0], sgu_norm_b=g["sgu_nb"][0], sgu_w=g["sgu_w"],
        sgu_b=g["sgu_bm"].reshape(128, 4, 64).sum(-1).T,
        gla_wg_fwd=g["wg"][0:16, 0:128], gla_wg_bwd=g["wg"][16:32, 128:256],
        gla_bg_fwd=g["bg"][0, 0:128], gla_bg_bwd=g["bg"][0, 128:256],
        gla_norm_w=g["gla_nwt"].reshape(4, 64).sum(0),
        mla_kv_norm_w=g["kvw"][0], mla_q_norm_w=g["qw"][0])


def _big_views(full):
    views = {}
    if "w_in" in full:
        views.update(win=_in_to_padded(full["w_in"]), wuq=_uq_to_padded(full["mla_w_uq"]),
                     wukv=_ukv_to_padded(full["mla_w_ukv"]), wout=full["w_out"])
    if "w_ff1" in full:
        views.update(w1=full["w_ff1"], w2=full["w_ff2"])
    return views


def _gla_tile(t):
    return _pick(t, (768, 512, 256))


def _layer_ops(p, sv, a):
    gt = _gla_tile(p.shape[0])
    pc = lambda off, w: rows(p, w, off // w, tm=gt)
    gr = lambda arr, w=None, cb=0: rows(arr, w, cb, tm=gt)
    return dict(
        sgu=[pc(P_SU, 256), pc(P_SV, 256), const(sv["sgu_nw"]), const(sv["sgu_nb"]), const(sv["sgu_w"]), const(sv["sgu_bm"])],
        gates=[pc(P_GATE, 128), const(sv["wg"]), const(sv["bg"])],
        mla_pre=[pc(P_CKV, 256), pc(P_DQ, 256), const(sv["kvw"]), const(sv["qw"])],
        gla_kv=lambda: [pc(P_GK, 128), pc(P_GV, 256), gr(a["gf"]), gr(a["gb"])],
        gla_o=lambda: [pc(P_GQ, 128), pc(P_GK, 128), pc(P_GV, 256), gr(a["gf"]), gr(a["gb"]), pc(P_GR, 256),
                       chunks(a["sf"], gt // GLA_CHUNK), chunks(a["sb"], gt // GLA_CHUNK), const(sv["gla_nwt"])],
        mla_post=lambda: [gr(a["kvu"], 512, 0), gr(a["qu"]), pc(P_KR, 128), gr(a["cs"]), gr(a["sn"])])


def layer_fwd(l, xt, modl, bw, sv, tabs, side=(), late=None):
    t = xt.shape[0]
    g, nc, gt = t // TM, t // GLA_CHUNK, _gla_tile(t)
    gg, cpt = t // gt, gt // GLA_CHUNK
    nm = lambda s: f"l{l}_{s}"
    a = dict(x=xt, cs=tabs[0], sn=tabs[1])
    a["h"], = rw(nm("norm1"), fn_norm1, [rows(xt), const(modl), const(sv["n1w"])], [rowout(t, D, BF16)], g)
    p = a["p"] = mm(nm("in_proj"), a["h"], bw["win"], F32)
    ops = _layer_ops(p, sv, a)
    y_sgu, = rw(nm("sgu"), fn_sgu, ops["sgu"], [rowout(t, 256, BF16, tm=gt)], gg)
    a["gf"], a["gb"] = rw(nm("gates"), fn_gates, ops["gates"], [rowout(t, 128, F32, tm=gt)] * 2, gg)
    a["uf"], a["ef"], a["ub"], a["eb"] = rw(nm("gla_kv"), fn_gla_kv, ops["gla_kv"](),
                                           [chunkout((nc, 128, 256), F32, cpt), chunkout((nc, 128, 128), F32, cpt)] * 2, gg)
    a["sf"], a["sb"] = gla_states(a["uf"], a["ef"], a["ub"], a["eb"])
    y_gla, = rw(nm("gla_o"), fn_gla_o, ops["gla_o"](), [rowout(t, 256, BF16, tm=gt)], gg)
    a["ckvn"], a["dqn"] = rw(nm("mla_pre"), fn_mla_pre, ops["mla_pre"], [rowout(t, 256, BF16, tm=gt)] * 2, gg)
    a["kvu"] = mm(nm("kv_up"), a["ckvn"], bw["wukv"], BF16)
    a["qu"] = mm(nm("q_up"), a["dqn"], bw["wuq"], F32)
    a["kcat"], a["q"] = rw(nm("mla_post"), fn_mla_post, ops["mla_post"](), [rowout(t, 1024, BF16, tm=gt)] * 2, gg)
    a["o"], a["lse"], side_out = flash_fwd(a["q"], a["kcat"], a["kvu"], side)
    if late is not None:
        late(side_out)
    a["y"] = jnp.concatenate([y_sgu, y_gla, a["o"].astype(BF16)], axis=1)
    a["yo"] = mm(nm("out_proj"), a["y"], bw["wout"], F32)
    a["x1"], a["h2"] = rw(nm("res_norm2"), fn_res_norm2, [rows(xt), rows(a["yo"]), const(modl), const(sv["n2w"])],
                          [rowout(t, D, F32), rowout(t, D, BF16)], g)
    a["act"] = mm(nm("ff1"), a["h2"], bw["w1"], BF16, post=lambda acc: jnp.maximum(acc, 0.0))
    a["f"] = mm(nm("ff2"), a["act"], bw["w2"], F32, pre=_square_bf16)
    x2, = rw(nm("res2"), fn_res2, [rows(a["x1"]), rows(a["f"]), const(modl)], [rowout(t, D, F32)], g)
    return x2, a


def fn_assemble(blk, gv1, gv2, ckv, su, sv_, gr, dq, gk1, gk2, pg, kr, gq):
    return (jnp.concatenate([gv1 + gv2, ckv, su, sv_, gr, dq, gk1 + gk2, pg, kr, gq], axis=1),)


def layer_bwd(l, dx2, a, modl, bw, sv, side_a=None, make_side_b=None, ff_side=None):
    t = dx2.shape[0]
    g, gt = t // TM, _gla_tile(t)
    gg, cpt = t // gt, gt // GLA_CHUNK
    nm = lambda s: f"l{l}_{s}_bwd"
    p = a["p"]
    ops = _layer_ops(p, sv, a)
    gw, gs = {}, {}
    df, dm_a = rw_vjp(nm("res2"), fn_res2, [rows(a["x1"]), rows(a["f"]), const(modl)], [rows(dx2)], [1, 2], g,
                      gdt=[BF16, F32])
    during_attention = []
    if side_a:
        gw["w2"], got_a = mm_tn(nm("ff2_w"), a["act"], df, pre=_square_bf16, side=Side(side_a))
        during_attention += make_side_b(got_a)
    else:
        gw["w2"] = mm_tn(nm("ff2_w"), a["act"], df, pre=_square_bf16)
    du = mm(nm("ff2_x"), df, bw["w2"], BF16, post=lambda acc, act: acc * (2.0 * act.astype(F32)), extras=(a["act"],), bt=True)
    gw["w1"] = mm_tn(nm("ff1_w"), a["h2"], du)
    dh2 = mm(nm("ff1_x"), du, bw["w1"], F32, bt=True)
    dxa, dyo, dm_b, gs["n2w"] = rw_vjp(nm("res_norm2"), fn_res_norm2,
                                       [rows(a["x"]), rows(a["yo"]), const(modl), const(sv["n2w"])],
                                       [rows(dx2), rows(dh2)], [0, 1, 2, 3], g, gdt=[F32, BF16, F32, F32])
    if ff_side is not None:
        ff_entries, make_ff_next = ff_side(dict(w_ff1=gw["w1"], w_ff2=gw["w2"]))
        gw["wout"], got_ff = mm_tn(nm("out_w"), a["y"], dyo, side=Side(ff_entries))
        during_attention += make_ff_next(got_ff)
    else:
        gw["wout"] = mm_tn(nm("out_w"), a["y"], dyo)
    dy = mm(nm("out_x"), dyo, bw["wout"], F32, bt=True)
    dsu, dsv, gs["sgu_nw"], gs["sgu_nb"], gs["sgu_w"], gs["sgu_bm"] = rw_vjp(
        nm("sgu"), fn_sgu, ops["sgu"], [rows(dy, 256, 0, tm=gt)], [0, 1, 2, 3, 4, 5], gg)
    dgq, dgk1, dgv1, dgf1, dgb1, dgr, dsf, dsb, gs["gla_nwt"] = rw_vjp(
        nm("gla_o"), fn_gla_o, ops["gla_o"](), [rows(dy, 256, 1, tm=gt)], list(range(9)), gg)
    duf, def_, dub, deb = gla_states_bwd(a["ef"], a["eb"], a["sf"], a["sb"], dsf, dsb)
    dgk2, dgv2, dgf, dgb = rw_vjp(nm("gla_kv"), fn_gla_kv, ops["gla_kv"](),
                                  [chunks(duf, cpt), chunks(def_, cpt), chunks(dub, cpt), chunks(deb, cpt)], [0, 1, 2, 3], gg,
                                  adds={2: rows(dgf1, tm=gt), 3: rows(dgb1, tm=gt)})
    dpg, gs["wg"], gs["bg"] = rw_vjp(nm("gates"), fn_gates, ops["gates"], [rows(dgf, tm=gt), rows(dgb, tm=gt)], [0, 1, 2], gg)
    stats, = rw(nm("attn_stats"), fn_attn_stats, [rows(dy, 512, 1, tm=gt), rows(a["o"], tm=gt), rows(a["lse"], tm=gt)],
                [chunkout((g, 64, TM), F32, gt // TM)], gg)
    dq, dkcat, dv, got_b = flash_bwd(a["q"], a["kcat"], a["kvu"], dy, stats,
                                     side=Side(during_attention) if during_attention else None)
    post_ins = [blank_rows(t, 512, BF16, gt), blank_rows(t, 1024, F32, gt), blank_rows(t, 128, F32, gt)] + ops["mla_post"]()[3:]
    dkk, dqu, dkr = rw_vjp(nm("mla_post"), fn_mla_post, post_ins, [rows(dkcat, tm=gt), rows(dq, tm=gt)], [0, 1, 2],
                           gg, gdt=[BF16, BF16, F32])
    dkvu = jnp.concatenate([dkk, dv.astype(BF16)], axis=1)
    gw["wukv"] = mm_tn(nm("kv_up_w"), a["ckvn"], dkvu)
    gw["wuq"] = mm_tn(nm("q_up_w"), a["dqn"], dqu)
    dckvn = mm(nm("kv_up_x"), dkvu, bw["wukv"], F32, bt=True)
    ddqn = mm(nm("q_up_x"), dqu, bw["wuq"], F32, bt=True)
    dckv, ddq, gs["kvw"], gs["qw"] = rw_vjp(nm("mla_pre"), fn_mla_pre, ops["mla_pre"],
                                            [rows(dckvn, tm=gt), rows(ddqn, tm=gt)], [0, 1, 2, 3], gg)
    dp, = rw(nm("assemble"), fn_assemble,
             [rows(x_, tm=gt) for x_ in (dgv1, dgv2, dckv, dsu, dsv, dgr, ddq, dgk1, dgk2, dpg, dkr, dgq)],
             [rowout(t, P_COLS, BF16, tm=gt)], gg)
    gw["win"] = mm_tn(nm("in_w"), a["h"], dp)
    dh = mm(nm("in_x"), dp, bw["win"], F32, bt=True)
    dx, dm_c, gs["n1w"] = rw_vjp(nm("norm1"), fn_norm1, [rows(a["x"]), const(modl), const(sv["n1w"])], [rows(dh)],
                                 [0, 1, 2], g, adds={0: rows(dxa)})
    big = dict(w_in=_in_from_padded(gw["win"]), w_out=gw["wout"], mla_w_uq=_uq_from_padded(gw["wuq"]),
               mla_w_ukv=_ukv_from_padded(gw["wukv"]), w_ff1=gw["w1"], w_ff2=gw["w2"])
    return dx, dm_a + dm_b + dm_c, big, _small_grads(gs), got_b


SMALL_NAMES = ("norm1_w", "sgu_norm_w", "sgu_norm_b", "sgu_w", "sgu_b", "gla_wg_fwd", "gla_bg_fwd", "gla_wg_bwd",
               "gla_bg_bwd", "gla_norm_w", "mla_q_norm_w", "mla_kv_norm_w", "norm2_w")
BIG_NAMES = ("w_in", "w_out", "mla_w_uq", "mla_w_ukv", "w_ff1", "w_ff2")
ATTN_WEIGHTS, FF_WEIGHTS = BIG_NAMES[:4], BIG_NAMES[4:]


def local_step(x, ctx, target, mods, big, small, final_norm_w, side=(), on_side=None, grad_side=None, ff_side=None):
    n = x.shape[0]
    xt = jnp.concatenate([ctx, x], axis=0)
    tabs = _rope_tables(n)
    depth = len(mods)
    big = list(big)
    svs = [_small_views(small[l]) for l in range(depth)]
    acts, bws = [], []
    for l in range(depth):
        bws.append(_big_views(big[l]))
        late = None
        if l == 0 and on_side is not None:
            def late(results):
                rest0, later = on_side(results)
                bws[0].update(_big_views(rest0))
                big.extend(later)
        xt, a = layer_fwd(l, xt, mods[l], bws[l], svs[l], tabs, side if l == 0 else (), late)
        acts.append(a)
    loss, dxt, dfnw = loss_head(xt, target, final_norm_w[None])
    dmods, gbig, gsmall = [None] * depth, [None] * depth, [None] * depth
    got = []
    for l in reversed(range(depth)):
        hooks = (None, None, None)
        if l == 0 and grad_side is not None and depth > 1:
            hooks = (*grad_side(gbig[1:]), ff_side)
        dxt, dmods[l], gbig[l], gsmall[l], got = layer_bwd(l, dxt, acts[l], mods[l], bws[l], svs[l], *hooks)
    return loss, dxt[CTX:], dmods, gbig, gsmall, dfnw, got


def _group(group):
    x, y, c = lax.axis_index("x"), lax.axis_index("y"), lax.axis_index("c")
    if group == "sib":
        return 2, c, [((x, y, 1 - c), 1 - c)]
    if group == "chip":
        flips = [(1, 0), (0, 1), (1, 1)]
        return 4, 2 * x + y, [((x ^ fx, y ^ fy, c), 2 * (x ^ fx) + (y ^ fy)) for fx, fy in flips]
    flips = [(fx, fy, fc) for fx in (0, 1) for fy in (0, 1) for fc in (0, 1)][1:]
    return 8, 4 * x + 2 * y + c, [((x ^ fx, y ^ fy, c ^ fc), 4 * (x ^ fx) + 2 * (y ^ fy) + (c ^ fc)) for fx, fy, fc in flips]


def _group_size(group):
    return {"sib": 2, "chip": 4, "all": 8}[group]


REMOTE_COPIES = {"gather": None, "scatter": None, "swap": 1, "gather2": 6}


def _exchange_shapes(entries):
    n_in = sum(len(arrs) for _, _, arrs in entries)
    n_remote = sum(REMOTE_COPIES[k] or _group_size(g) - 1 for k, g, _ in entries)
    n_local = sum(1 for k, _, _ in entries if k != "swap")
    out_shape = []
    for kind, group, arrs in entries:
        a = arrs[0]
        if kind in ("gather", "gather2"):
            out_shape.append(jax.ShapeDtypeStruct((_group_size(group),) + a.shape, a.dtype))
        elif kind == "swap" and len(arrs) == 1:
            out_shape.append(jax.ShapeDtypeStruct(a.shape[1:], a.dtype))
        else:
            out_shape.append(jax.ShapeDtypeStruct(a.shape, a.dtype))
    return n_in, n_remote, n_local, out_shape


def _exchange_sems(entries):
    _, n_remote, n_local, _ = _exchange_shapes(entries)
    return [pltpu.SemaphoreType.DMA((n_remote,)), pltpu.SemaphoreType.DMA((n_remote,)), pltpu.SemaphoreType.DMA((max(n_local, 1),))]


def _exchange_phases(entries, in_refs, out_refs, send_sems, recv_sems, local_sems):
    x, y, c = lax.axis_index("x"), lax.axis_index("y"), lax.axis_index("c")

    def remote(src, dst, k, dev):
        return pltpu.make_async_remote_copy(src_ref=src, dst_ref=dst, send_sem=send_sems.at[k], recv_sem=recv_sems.at[k],
                                            device_id=dev, device_id_type=MESH)

    pos, k, kl = 0, 0, 0
    starts, forwards, finals = [], [], []
    for (kind, group, arrs), out in zip(entries, out_refs):
        srcs = in_refs[pos:pos + len(arrs)]
        pos += len(arrs)
        _, mine, peers = _group(group)
        if kind == "swap":
            (dev, _), = peers
            if len(srcs) == 1:
                starts.append(remote(srcs[0].at[1 - c], out, k, dev).start)
                finals.append(remote(srcs[0].at[0], out, k, dev).wait)
            else:
                def start_swap(srcs=srcs, k=k, dev=dev, out=out):
                    for core, src in ((0, srcs[1]), (1, srcs[0])):
                        @pl.when(c == core)
                        def _(src=src):
                            remote(src, out, k, dev).start()

                starts.append(start_swap)
                finals.append(remote(srcs[0], out, k, dev).wait)
            k += 1
            continue
        src = srcs[0]
        own = pltpu.make_async_copy(src if kind != "scatter" else src.at[mine], out.at[mine], local_sems.at[kl])
        starts.append(own.start)
        finals.append(own.wait)
        kl += 1
        if kind == "gather2":
            sibling = (x, y, 1 - c)
            for f, (dev, slot) in enumerate(peers):
                starts.append(remote(src.at[c], out.at[mine, c], k + f, dev).start)
                arrival = remote(src.at[c], out.at[slot, c], k + f, dev)

                def forward(arrival=arrival, slot=slot, kf=k + 3 + f, out=out):
                    arrival.wait_recv()
                    remote(out.at[slot, c], out.at[slot, c], kf, sibling).start()

                forwards.append(forward)
                finals.append(arrival.wait_send)
                finals.append(remote(out.at[slot, c], out.at[slot, 1 - c], k + 3 + f, sibling).wait)
            k += 6
            continue
        for dev, slot in peers:
            piece = src if kind == "gather" else src.at[slot]
            starts.append(remote(piece, out.at[mine], k, dev).start)
            finals.append(remote(piece, out.at[slot], k, dev).wait)
            k += 1
    return starts, forwards, finals


class Side:
    def __init__(self, entries):
        self.entries = tuple(entries)
        self.n_in, _, _, self.shapes = _exchange_shapes(self.entries)
        self.n_out = len(self.entries)
        self.arrays = [a for _, _, arrs in self.entries for a in arrs]
        any_spec = pl.BlockSpec(memory_space=pl.ANY)
        self.in_specs, self.out_specs = [any_spec] * self.n_in, [any_spec] * self.n_out
        self.sems = _exchange_sems(self.entries) if self.entries else []

    def start(self, in_refs, out_refs, sem_refs):
        if not self.entries:
            return lambda: None
        ids = [pl.program_id(d) for d in range(3)]
        first = (ids[0] == 0) & (ids[1] == 0) & (ids[2] == 0)
        last = ((ids[0] == pl.num_programs(0) - 1) & (ids[1] == pl.num_programs(1) - 1) & (ids[2] == pl.num_programs(2) - 1))
        starts, forwards, finals = _exchange_phases(self.entries, in_refs, out_refs, *sem_refs)
        assert not forwards

        @pl.when(first)
        def _():
            for run in starts:
                run()

        def finish():
            @pl.when(last)
            def _():
                for run in finals:
                    run()

        return finish


def xchg(name, entries):
    n_in, _, _, out_shape = _exchange_shapes(entries)

    def body(*refs):
        in_refs, out_refs = refs[:n_in], refs[n_in:n_in + len(entries)]
        for phase in _exchange_phases(entries, in_refs, out_refs, *refs[n_in + len(entries):]):
            for run in phase:
                run()

    any_spec = pl.BlockSpec(memory_space=pl.ANY)
    return pl.pallas_call(
        body, name=name,
        in_specs=[any_spec] * n_in, out_specs=[any_spec] * len(entries), out_shape=out_shape,
        scratch_shapes=_exchange_sems(entries),
    )(*[a for _, _, arrs in entries for a in arrs])


def _block_rows(r, c, budget=131072):
    tr = 8
    while tr * 2 * c <= budget and r % (tr * 2) == 0:
        tr *= 2
    return tr if r % tr == 0 else r


def tree_sum(name, parts):
    g, r, c = parts.shape
    tr = _block_rows(r, c)

    def body(p_ref, o_ref):
        p = [p_ref[i].astype(F32) for i in range(g)]
        while len(p) > 1:
            p = [p[i] + p[i + 1] for i in range(0, len(p), 2)]
        o_ref[...] = p[0]

    return pl.pallas_call(
        body, name=name, grid=(r // tr,),
        in_specs=[pl.BlockSpec((g, tr, c), lambda i: (0, i, 0))], out_specs=pl.BlockSpec((tr, c), lambda i: (i, 0)),
        out_shape=jax.ShapeDtypeStruct((r, c), F32), compiler_params=_params(("arbitrary",)),
    )(parts)


def pair_sum(name, halves, recv, core):
    r, c = recv.shape
    tr = _block_rows(r, c)

    def body(h_ref, r_ref, k_ref, o_ref):
        o_ref[...] = (jnp.where(k_ref[...] > 0.5, h_ref[1], h_ref[0]) + r_ref[...]).astype(o_ref.dtype)

    blk = pl.BlockSpec((tr, c), lambda i: (i, 0))
    return pl.pallas_call(
        body, name=name, grid=(r // tr,),
        in_specs=[pl.BlockSpec((2, tr, c), lambda i: (0, i, 0)), blk, pl.BlockSpec((1, 1), lambda i: (0, 0))],
        out_specs=blk, out_shape=jax.ShapeDtypeStruct((r, c), BF16), compiler_params=_params(("arbitrary",)),
    )(halves, recv, core)


def adamw(name, w, g, m, v):
    r, c = w.shape
    tr = _block_rows(r, c)

    def body(w_ref, g_ref, m_ref, v_ref, d_ref, nm_ref, nv_ref):
        gg = g_ref[...]
        nm = ADAM_B1 * m_ref[...] + (1.0 - ADAM_B1) * gg
        nv = ADAM_B2 * v_ref[...] + (1.0 - ADAM_B2) * jnp.square(gg)
        m_hat = nm / (1.0 - ADAM_B1 ** ADAM_STEP)
        v_hat = nv / (1.0 - ADAM_B2 ** ADAM_STEP)
        d_ref[...] = -ADAM_LR * (m_hat / (jnp.sqrt(v_hat) + ADAM_EPS) + ADAM_WD * w_ref[...])
        nm_ref[...] = nm
        nv_ref[...] = nv

    blk = pl.BlockSpec((tr, c), lambda i: (i, 0))
    return pl.pallas_call(
        body, name=name, grid=(r // tr,), in_specs=[blk] * 4, out_specs=[blk] * 3,
        out_shape=[jax.ShapeDtypeStruct((r, c), F32)] * 3, compiler_params=_params(("arbitrary",)),
    )(w, g, m, v)


W_MOD_COLS = 6 * D // 4
MOD_TN = 512


def mod_project(c16, w_mod, b_loc):
    def body(c_ref, w_ref, b_ref, o_ref):
        cv = c_ref[...]
        s = (cv * _sigmoid(cv)).astype(BF16)
        o_ref[0] = _dot(s, w_ref[0].astype(BF16)) + b_ref[0]

    return pl.pallas_call(
        body, name="mod_project", grid=(2, W_MOD_COLS // MOD_TN),
        in_specs=[pl.BlockSpec((16, D), lambda l, j: (0, 0)), pl.BlockSpec((1, D, MOD_TN), lambda l, j: (l, 0, j)),
                  pl.BlockSpec((1, 1, MOD_TN), lambda l, j: (l, 0, j))],
        out_specs=pl.BlockSpec((1, 16, MOD_TN), lambda l, j: (l, 0, j)),
        out_shape=jax.ShapeDtypeStruct((2, 16, W_MOD_COLS), F32), compiler_params=_params(("arbitrary", "arbitrary")),
    )(c16, w_mod, b_loc)


def mod_weight_grad(c16, dm16):
    def body(c_ref, d_ref, o_ref):
        cv = c_ref[...]
        o_ref[0] = _dot(cv * _sigmoid(cv), d_ref[0], ((0,), (0,)), precision=HI)

    return pl.pallas_call(
        body, name="mod_weight_grad", grid=(2, W_MOD_COLS // MOD_TN),
        in_specs=[pl.BlockSpec((16, D), lambda l, j: (0, 0)), pl.BlockSpec((1, 16, MOD_TN), lambda l, j: (l, 0, j))],
        out_specs=pl.BlockSpec((1, D, MOD_TN), lambda l, j: (l, 0, j)),
        out_shape=jax.ShapeDtypeStruct((2, D, W_MOD_COLS), F32), compiler_params=_params(("arbitrary", "arbitrary")),
    )(c16, dm16)


def cctx_partial(dmc, w_mod):
    def body(d_ref, w_ref, o_ref):
        @pl.when(pl.program_id(0) == 0)
        def _():
            o_ref[...] = jnp.zeros_like(o_ref)
        o_ref[...] += _dot(d_ref[0], w_ref[0], ((1,), (1,)), precision=HI)

    return pl.pallas_call(
        body, name="cctx_partial", grid=(2,),
        in_specs=[pl.BlockSpec((1, 8, W_MOD_COLS), lambda l: (l, 0, 0)), pl.BlockSpec((1, D, W_MOD_COLS), lambda l: (l, 0, 0))],
        out_specs=pl.BlockSpec((8, D), lambda l: (0, 0)),
        out_shape=jax.ShapeDtypeStruct((8, D), F32), compiler_params=_params(("arbitrary",)),
    )(dmc, w_mod)


def cctx_grad(parts, c_ctx8):
    def body(p_ref, c_ref, o_ref):
        ds = (p_ref[0] + p_ref[1]) + (p_ref[2] + p_ref[3])
        _, vf = jax.vjp(lambda z: z * _sigmoid(z), c_ref[...])
        o_ref[...] = vf(ds)[0]

    return pl.pallas_call(
        body, name="cctx_grad", out_shape=jax.ShapeDtypeStruct((8, D), F32),
    )(parts, c_ctx8)


ARG_NAMES = ("x", "c", "ctx", "c_ctx", "w_mod", "b_mod", "norm1_w", "w_in", "w_out", "sgu_norm_w", "sgu_norm_b", "sgu_w",
             "sgu_b", "gla_wg_fwd", "gla_bg_fwd", "gla_wg_bwd", "gla_bg_bwd", "gla_norm_w", "mla_q_norm_w", "mla_w_uq",
             "mla_kv_norm_w", "mla_w_ukv", "norm2_w", "w_ff1", "w_ff2", "final_norm_w")
WEIGHT_NAMES = ARG_NAMES[3:]
PACKED = ("c_ctx", "b_mod") + SMALL_NAMES + ("final_norm_w",)
ROW_SHARDED = ("w_out", "w_ff2")
PACK_ROWS = 256


def _pack(vectors):
    flat = jnp.concatenate([v.reshape(-1) for v in vectors])
    n = flat.shape[0]
    total = -(-n // (PACK_ROWS * LANES)) * PACK_ROWS * LANES
    return jnp.pad(flat, (0, total - n)).reshape(-1, LANES)


def _unpack(buf, shapes):
    flat, out, pos = buf.reshape(-1), [], 0
    for shp in shapes:
        n = int(np.prod(shp))
        out.append(flat[pos:pos + n].reshape(shp))
        pos += n
    return out


def _full_weight(name, g):
    if name in ROW_SHARDED:
        return g.reshape(-1, g.shape[-1])
    return g.transpose(1, 0, 2).reshape(g.shape[1], -1)


def _chip_chunks(name, a):
    if name in ROW_SHARDED:
        return a.reshape(4, a.shape[0] // 4, a.shape[1])
    return a.reshape(a.shape[0], 4, a.shape[1] // 4).transpose(1, 0, 2)


def kernel(x, c, ctx, c_ctx, w_mod, b_mod, norm1_w, w_in, w_out, sgu_norm_w, sgu_norm_b, sgu_w, sgu_b, gla_wg_fwd, gla_bg_fwd, gla_wg_bwd, gla_bg_bwd, gla_norm_w, mla_q_norm_w, mla_w_uq, mla_kv_norm_w, mla_w_ukv, norm2_w, w_ff1, w_ff2, final_norm_w, loss_target, m_c_ctx, m_w_mod, m_b_mod, m_norm1_w, m_w_in, m_w_out, m_sgu_norm_w, m_sgu_norm_b, m_sgu_w, m_sgu_b, m_gla_wg_fwd, m_gla_bg_fwd, m_gla_wg_bwd, m_gla_bg_bwd, m_gla_norm_w, m_mla_q_norm_w, m_mla_w_uq, m_mla_kv_norm_w, m_mla_w_ukv, m_norm2_w, m_w_ff1, m_w_ff2, m_final_norm_w, v_c_ctx, v_w_mod, v_b_mod, v_norm1_w, v_w_in, v_w_out, v_sgu_norm_w, v_sgu_norm_b, v_sgu_w, v_sgu_b, v_gla_wg_fwd, v_gla_bg_fwd, v_gla_wg_bwd, v_gla_bg_bwd, v_gla_norm_w, v_mla_q_norm_w, v_mla_w_uq, v_mla_kv_norm_w, v_mla_w_ukv, v_norm2_w, v_w_ff1, v_w_ff2, v_final_norm_w):
    args = (x, c, ctx, c_ctx, w_mod, b_mod, norm1_w, w_in, w_out, sgu_norm_w, sgu_norm_b, sgu_w, sgu_b, gla_wg_fwd, gla_bg_fwd, gla_wg_bwd, gla_bg_bwd, gla_norm_w, mla_q_norm_w, mla_w_uq, mla_kv_norm_w, mla_w_ukv, norm2_w, w_ff1, w_ff2, final_norm_w)
    w = dict(zip(ARG_NAMES, args))
    moms = (m_c_ctx, m_w_mod, m_b_mod, m_norm1_w, m_w_in, m_w_out, m_sgu_norm_w, m_sgu_norm_b, m_sgu_w, m_sgu_b, m_gla_wg_fwd, m_gla_bg_fwd, m_gla_wg_bwd, m_gla_bg_bwd, m_gla_norm_w, m_mla_q_norm_w, m_mla_w_uq, m_mla_kv_norm_w, m_mla_w_ukv, m_norm2_w, m_w_ff1, m_w_ff2, m_final_norm_w)
    vars_ = (v_c_ctx, v_w_mod, v_b_mod, v_norm1_w, v_w_in, v_w_out, v_sgu_norm_w, v_sgu_norm_b, v_sgu_w, v_sgu_b, v_gla_wg_fwd, v_gla_bg_fwd, v_gla_wg_bwd, v_gla_bg_bwd, v_gla_norm_w, v_mla_q_norm_w, v_mla_w_uq, v_mla_kv_norm_w, v_mla_w_ukv, v_norm2_w, v_w_ff1, v_w_ff2, v_final_norm_w)
    m1 = dict(zip(WEIGHT_NAMES, moms))
    m2 = dict(zip(WEIGHT_NAMES, vars_))
    xi, yi, ci = lax.axis_index("x"), lax.axis_index("y"), lax.axis_index("c")
    chip, dev = 2 * xi + yi, 4 * xi + 2 * yi + ci
    depth = w_mod.shape[0]

    def shard_halves(l, names):
        return [("gather2", "chip", [w[k][l].astype(BF16).reshape(2, w[k].shape[1] // 2, w[k].shape[2])]) for k in names]

    def full_weights(names, gathered):
        return {k: _full_weight(k, g.reshape(4, *w[k].shape[1:])) for k, g in zip(names, gathered)}

    got = xchg("gather_inputs", [("gather", "all", [c])] + shard_halves(0, ATTN_WEIGHTS))
    c_all = got[0]
    c16 = jnp.concatenate([c_all.reshape(8, D), c_ctx[None], jnp.zeros((7, D), F32)], axis=0)
    b_loc = lax.dynamic_slice_in_dim(b_mod, chip * W_MOD_COLS, W_MOD_COLS, axis=1)[:, None, :]
    mod_part = mod_project(c16, w_mod, b_loc)
    mod_all, = xchg("gather_mod", [("gather", "chip", [mod_part])])
    mod_full = mod_all.transpose(1, 2, 0, 3).reshape(depth, 16, 6 * D)
    mods = [jnp.stack([mod_full[l, 8], lax.dynamic_index_in_dim(mod_full[l], dev, 0, keepdims=False)])[:, None, :]
            for l in range(depth)]

    small = [{k: w[k][l] for k in SMALL_NAMES} for l in range(depth)]
    n_big = len(BIG_NAMES)
    n_ff = len(FF_WEIGHTS)
    later = shard_halves(0, FF_WEIGHTS) + [e for l in range(1, depth) for e in shard_halves(l, BIG_NAMES)]
    core = ci.astype(F32).reshape(1, 1)

    def on_side(res):
        return (full_weights(FF_WEIGHTS, res[:n_ff]),
                [full_weights(BIG_NAMES, res[n_ff + i * n_big:n_ff + (i + 1) * n_big]) for i in range(depth - 1)])

    def half_major(k, g):
        ch = _chip_chunks(k, g)
        return ch.reshape(4, 2, ch.shape[1] // 2, ch.shape[2]).transpose(1, 0, 2, 3)

    def swap_entries(gb, names):
        hm = [half_major(k, gb[k]) for k in names]
        return hm, [("swap", "sib", [h]) for h in hm]

    def scatter_entries(tag, names, hm, recv):
        out = []
        for k, h, r in zip(names, hm, recv):
            s2 = pair_sum(f"pair_sum_{tag}_{k}", h.reshape(2, -1, h.shape[-1]), r.reshape(-1, r.shape[-1]), core)
            out.append(("scatter", "chip", [s2.reshape(r.shape)]))
        return out

    def grad_side(gb_later):
        hms, entries = [], []
        for gb in gb_later:
            hm, e = swap_entries(gb, BIG_NAMES)
            hms.append(hm)
            entries += e

        def make_scatter(recv):
            return [e for i, hm in enumerate(hms)
                    for e in scatter_entries(f"l{i + 1}", BIG_NAMES, hm, recv[i * n_big:(i + 1) * n_big])]

        return entries, make_scatter

    def ff_side(g_ff):
        hm, entries = swap_entries(g_ff, FF_WEIGHTS)
        return entries, lambda recv: scatter_entries("l0", FF_WEIGHTS, hm, recv)

    loss, grad_x, dmods, gbig, gsmall, dfnw, early_pieces = local_step(
        x[0], ctx[0], loss_target[0], mods, [full_weights(ATTN_WEIGHTS, got[1:])], small, final_norm_w, side=later,
        on_side=on_side, grad_side=grad_side, ff_side=ff_side)

    dm_lat = jnp.stack([dmods[l][1, 0] for l in range(depth)])
    dm_ctx = jnp.stack([dmods[l][0, 0] for l in range(depth)])
    small_pack = _pack([dm_lat, dm_ctx] + [jnp.stack([gsmall[l][k] for l in range(depth)]) for k in SMALL_NAMES] + [dfnw, loss])
    hm0, swap0 = swap_entries(gbig[0], ATTN_WEIGHTS)
    got = xchg("exchange_grads", [("gather", "all", [small_pack])] + swap0)
    small_all, recv0 = got[0], got[1:]
    small_sum = tree_sum("small_grad_sum", small_all)

    n_dm = depth * 6 * D
    dm_rows = n_dm // LANES
    dm_lat_all = small_all[:, :dm_rows].reshape(8, depth, 6 * D)
    dm_ctx_sum = small_sum[dm_rows:2 * dm_rows].reshape(depth, 6 * D)
    take = lambda a: lax.dynamic_slice_in_dim(a, chip * W_MOD_COLS, W_MOD_COLS, axis=-1)
    dmc_loc = take(dm_ctx_sum)
    cc_part = cctx_partial(jnp.pad(dmc_loc[:, None, :], ((0, 0), (0, 7), (0, 0))), w_mod)
    got = xchg("scatter_grads", [("gather", "chip", [cc_part])] + scatter_entries("l0", ATTN_WEIGHTS, hm0, recv0))
    cc_parts = got[0]
    keys = ([(0, k) for k in ATTN_WEIGHTS] + [(l, k) for l in range(1, depth) for k in BIG_NAMES] + [(0, k) for k in FF_WEIGHTS])
    pieces = dict(zip(keys, list(got[1:]) + list(early_pieces)))
    keys = [(l, k) for l in range(depth) for k in BIG_NAMES]
    reduced = {lk: tree_sum(f"chip_sum_l{lk[0]}_{lk[1]}", pieces[lk]) for lk in keys}
    g_c_ctx = cctx_grad(cc_parts, jnp.broadcast_to(c_ctx[None], (8, D)))[0]

    others = dict(zip(keys, xchg("share_halves", [("swap", "sib", [reduced[lk], reduced[lk]]) for lk in keys])))
    shard = {lk: jnp.where(ci == 0, jnp.concatenate([reduced[lk], others[lk]], axis=0),
                           jnp.concatenate([others[lk], reduced[lk]], axis=0)) for lk in keys}
    grads = {k: jnp.stack([shard[(l, k)] for l in range(depth)]) for k in BIG_NAMES}

    dm16 = jnp.concatenate([take(dm_lat_all).transpose(1, 0, 2), dmc_loc[:, None, :], jnp.zeros((depth, 7, W_MOD_COLS), F32)], axis=1)
    grads["w_mod"] = mod_weight_grad(c16, dm16)
    flat_sum = small_sum.reshape(-1)
    g_b_mod = (flat_sum[:n_dm] + flat_sum[n_dm:2 * n_dm]).reshape(depth, 6 * D)
    rest_shapes = [w[k].shape for k in PACKED[2:]]
    n_rest = sum(int(np.prod(s)) for s in rest_shapes)
    for k, g in zip(PACKED, [g_c_ctx, g_b_mod] + _unpack(flat_sum[2 * n_dm:2 * n_dm + n_rest], rest_shapes)):
        grads[k] = g
    loss = flat_sum[2 * n_dm + n_rest]

    delta, new_m, new_v = {}, {}, {}
    for k in BIG_NAMES + ("w_mod",):
        view = lambda a: a.reshape(-1, a.shape[-1])
        d_, m_, v_ = adamw(f"adamw_{k}", view(w[k]), view(grads[k]), view(m1[k]), view(m2[k]))
        delta[k], new_m[k], new_v[k] = d_.reshape(w[k].shape), m_.reshape(w[k].shape), v_.reshape(w[k].shape)
    shapes = [w[k].shape for k in PACKED]
    d_, m_, v_ = adamw("adamw_small", _pack([w[k] for k in PACKED]), _pack([grads[k] for k in PACKED]),
                       _pack([m1[k] for k in PACKED]), _pack([m2[k] for k in PACKED]))
    for k, dk, mk, vk in zip(PACKED, _unpack(d_, shapes), _unpack(m_, shapes), _unpack(v_, shapes)):
        delta[k], new_m[k], new_v[k] = dk, mk, vk
    return (loss, grad_x[None], *[grads[k] for k in WEIGHT_NAMES], *[delta[k] for k in WEIGHT_NAMES],
            *[new_m[k] for k in WEIGHT_NAMES], *[new_v[k] for k in WEIGHT_NAMES])
```

```python
import functools
import math

import numpy as np
import jax
import jax.numpy as jnp
from jax import lax
from jax.experimental import pallas as pl
from jax.experimental.pallas import tpu as pltpu

F32 = jnp.float32
BF16 = jnp.bfloat16
HI = lax.Precision.HIGHEST
EPS = 1e-6
VMEM_LIMIT_BYTES = 56 * 1024 * 1024
LANES = 128

D = 1024
D_FF = 4096
CTX = 256
GRID_W = 64
SGU_CHUNK = 128
GLA_CHUNK = 64
GLA_TAU = 16.0
GLA_DK = 32
MLA_SCALE = (128 + 64) ** -0.5
SCORE_SCALE = MLA_SCALE * math.log2(math.e)
LN2 = math.log(2.0)
ROPE_BASE = 10000.0
TM = 256
NCTXB = CTX // TM
P_GV, P_CKV, P_SU, P_SV, P_GR, P_DQ, P_GK, P_GATE, P_KR, P_GQ = 0, 256, 512, 768, 1024, 1280, 1536, 1664, 1792, 1920
P_COLS = 2048
IN_GROUPS = ((0, 128, P_GK), (128, 256, P_GV), (384, 32, P_GATE), (416, 256, P_CKV), (672, 64, P_KR),
             (736, 256, P_SU), (992, 256, P_SV), (1248, 128, P_GQ), (1376, 256, P_GR), (1632, 256, P_DQ))
ADAM_LR, ADAM_B1, ADAM_B2, ADAM_EPS, ADAM_WD, ADAM_STEP = 0.001, 0.9, 0.999, 1e-08, 0.01, 10
MESH = pl.DeviceIdType.MESH


def _params(sem):
    return pltpu.CompilerParams(dimension_semantics=sem, vmem_limit_bytes=VMEM_LIMIT_BYTES)


def _pick(n, cands):
    for c in cands:
        if n % c == 0:
            return c
    return n


class Op:
    def __init__(self, arr, blk, idx, gshape, gidx, acc):
        self.arr, self.blk, self.idx, self.gshape, self.gidx, self.acc = arr, blk, idx, gshape, gidx, acc

    def spec(self):
        return pl.BlockSpec(self.blk, self.idx)


def rows(arr, width=None, cb=0, off=0, tm=TM):
    w = arr.shape[1] if width is None else width
    n = arr.shape[0] - off * tm
    return Op(arr, (tm, w), lambda i: (i + off, cb), (n, w), lambda i: (i, 0), False)


def blank_rows(n, w, dtype, tm=TM):
    return Op(jnp.zeros((tm, w), dtype), (tm, w), lambda i: (0, 0), (n, w), lambda i: (i, 0), False)


def chunks(arr, per_tile):
    z = (0,) * (arr.ndim - 1)
    return Op(arr, (per_tile,) + arr.shape[1:], lambda i: (i,) + z, arr.shape, lambda i: (i,) + z, False)


def const(arr):
    z = (0,) * arr.ndim
    return Op(arr, arr.shape, lambda i: z, arr.shape, lambda i: z, True)


def rw(name, fn, ins, outs, grid):
    nin = len(ins)

    def body(*refs):
        vals = [r[...] for r in refs[:nin]]
        res = fn(pl.program_id(0), *vals)
        for o, r in zip(refs[nin:], res):
            o[...] = r.astype(o.dtype)

    return pl.pallas_call(
        body, name=name, grid=(grid,),
        in_specs=[o.spec() for o in ins],
        out_specs=[pl.BlockSpec(b, ix) for (_, _, b, ix) in outs],
        out_shape=[jax.ShapeDtypeStruct(s, d) for (s, d, _, _) in outs],
        compiler_params=_params(("arbitrary",)),
    )(*[o.arr for o in ins])


def rowout(n, w, dtype, tm=TM):
    return ((n, w), dtype, (tm, w), lambda i: (i, 0))


def chunkout(shape, dtype, per_tile):
    z = (0,) * (len(shape) - 1)
    return (shape, dtype, (per_tile,) + tuple(shape[1:]), lambda i: (i,) + z)


def rw_vjp(name, fn, ins, cots, wrt, grid, gdt=None, adds=None):
    nin = len(ins)
    cot_ops = [c for c in cots if c is not None]
    add_items = sorted((adds or {}).items())
    gdt = gdt or [F32] * len(wrt)
    ncot, nadd = len(cot_ops), len(add_items)

    def body(*refs):
        i = pl.program_id(0)
        vals = [r[...] for r in refs[:nin]]
        cvals = [r[...] for r in refs[nin:nin + ncot]]
        avals = [r[...] for r in refs[nin + ncot:nin + ncot + nadd]]
        grefs = refs[nin + ncot + nadd:]

        def f(*d):
            a = list(vals)
            for k, dv in zip(wrt, d):
                a[k] = dv
            return tuple(fn(i, *a))

        outs, vf = jax.vjp(f, *[vals[k] for k in wrt])
        it = iter(cvals)
        ct = tuple(jnp.zeros_like(o) if c is None else next(it).astype(o.dtype) for c, o in zip(cots, outs))
        gs = list(vf(ct))
        for (pos, _), av in zip(add_items, avals):
            gs[pos] = gs[pos].astype(F32) + av.astype(F32)
        for pos, (k, g, gref) in enumerate(zip(wrt, gs, grefs)):
            if ins[k].acc:
                @pl.when(i == 0)
                def _():
                    gref[...] = jnp.zeros_like(gref)
                gref[...] += g.astype(gref.dtype)
            else:
                gref[...] = g.astype(gref.dtype)

    all_in = list(ins) + cot_ops + [op for _, op in add_items]
    return pl.pallas_call(
        body, name=name, grid=(grid,),
        in_specs=[o.spec() for o in all_in],
        out_specs=[pl.BlockSpec(ins[k].blk, ins[k].gidx) for k in wrt],
        out_shape=[jax.ShapeDtypeStruct(ins[k].gshape, dt) for k, dt in zip(wrt, gdt)],
        compiler_params=_params(("arbitrary",)),
    )(*[o.arr for o in all_in])


MM_VMEM_BUDGET = 40 * 1024 * 1024
MM_COLS = 1024


def _square_bf16(a):
    a = a.astype(F32)
    return (a * a).astype(BF16)


def mm(name, a, b, out_dtype, pre=None, post=None, extras=(), bt=False):
    m, k = a.shape
    n = b.shape[0] if bt else b.shape[1]
    nc = min(n, MM_COLS)
    row_bytes = k * a.dtype.itemsize + n * jnp.dtype(out_dtype).itemsize + sum(n * e.dtype.itemsize for e in extras)
    tm = next(t for t in (768, 512, 384, 256, 128, 64)
              if m % t == 0 and 2 * t * row_bytes + 2 * k * n * b.dtype.itemsize + t * nc * 4 <= MM_VMEM_BUDGET)

    def body(a_ref, b_ref, *rest):
        o_ref = rest[-1]
        av = a_ref[...]
        if pre is not None:
            av = pre(av)
        for j in range(n // nc):
            cs = slice(j * nc, (j + 1) * nc)
            if bt:
                acc = lax.dot_general(av, b_ref[cs, :], (((1,), (1,)), ((), ())), preferred_element_type=F32)
            else:
                acc = lax.dot_general(av, b_ref[:, cs], (((1,), (0,)), ((), ())), preferred_element_type=F32)
            if post is not None:
                acc = post(acc, *[e[:, cs] for e in rest[:-1]])
            o_ref[:, cs] = acc.astype(o_ref.dtype)

    row = lambda w: pl.BlockSpec((tm, w), lambda i: (i, 0))
    return pl.pallas_call(
        body, name=name, grid=(m // tm,),
        in_specs=[row(k), pl.BlockSpec(b.shape, lambda i: (0, 0))] + [row(n) for _ in extras],
        out_specs=row(n),
        out_shape=jax.ShapeDtypeStruct((m, n), out_dtype),
        compiler_params=_params(("arbitrary",)),
    )(a, b, *extras)


def mm_tn(name, a, b, pre=None, side=None):
    m, ka = a.shape
    _, nb = b.shape
    tm = _pick(m, (768, 512, 256))
    ta = _pick(ka, (2048, 1024, 512, 256, 128))
    tb = _pick(nb, tuple(t for t in (4096, 2048, 1024, 512, 256, 128) if ta * t * 4 <= 8 * 1024 * 1024))
    sd = side or Side(())

    def body(a_ref, b_ref, *rest):
        o_ref = rest[sd.n_in]
        finish = sd.start(rest[:sd.n_in], rest[sd.n_in + 1:sd.n_in + 1 + sd.n_out], rest[sd.n_in + 1 + sd.n_out:])

        @pl.when(pl.program_id(2) == 0)
        def _():
            o_ref[...] = jnp.zeros_like(o_ref)
        av = a_ref[...] if pre is None else pre(a_ref[...])
        o_ref[...] += lax.dot_general(av, b_ref[...], (((0,), (0,)), ((), ())), preferred_element_type=F32)
        finish()

    res = pl.pallas_call(
        body, name=name, grid=(ka // ta, nb // tb, m // tm),
        in_specs=[pl.BlockSpec((tm, ta), lambda i, j, k: (k, i)), pl.BlockSpec((tm, tb), lambda i, j, k: (k, j))] + sd.in_specs,
        out_specs=[pl.BlockSpec((ta, tb), lambda i, j, k: (i, j))] + sd.out_specs,
        out_shape=[jax.ShapeDtypeStruct((ka, nb), F32)] + sd.shapes,
        scratch_shapes=sd.sems,
        compiler_params=_params(("arbitrary", "arbitrary", "arbitrary")),
    )(a, b, *sd.arrays)
    return res[0] if side is None else (res[0], list(res[1:]))


def _rms(x, w):
    return x * lax.rsqrt(jnp.mean(x * x, axis=-1, keepdims=True) + EPS) * w


def _mod_of(blk, m, n_rows):
    is_ctx = blk * n_rows + lax.broadcasted_iota(jnp.int32, (n_rows, 1), 0) < CTX
    return lambda lo, hi: jnp.where(is_ctx, m[0][:, lo:hi], m[1][:, lo:hi])


def _gelu(x):
    return x * (0.5 * (1.0 + jnp.tanh(math.sqrt(2.0 / math.pi) * (x + 0.044715 * (x * x * x)))))


def _sigmoid(x):
    return 1.0 / (1.0 + jnp.exp(-x))


def _log_sigmoid(z):
    return jnp.minimum(z, 0.0) - jnp.log(1.0 + jnp.exp(-jnp.abs(z)))


def _dot(a, b, dims=((1,), (0,)), precision=None):
    return lax.dot_general(a, b, (dims, ((), ())), precision=precision, preferred_element_type=F32)


def _lane_group_mask(width, group, h):
    lane = lax.broadcasted_iota(jnp.int32, (1, width), 1)
    return (lane >= h * group) & (lane < (h + 1) * group)


def fn_norm1(blk, x, m, nw):
    mv = _mod_of(blk, m, x.shape[0])
    return ((_rms(x, nw) * (1.0 + mv(D, 2 * D)) + mv(0, D)),)


def fn_res_norm2(blk, x, yo, m, nw):
    mv = _mod_of(blk, m, x.shape[0])
    x1 = x + mv(2 * D, 3 * D) * yo
    return x1, _rms(x1, nw) * (1.0 + mv(4 * D, 5 * D)) + mv(3 * D, 4 * D)


def fn_res2(blk, x1, f, m):
    mv = _mod_of(blk, m, x1.shape[0])
    return (x1 + mv(5 * D, 6 * D) * f,)


def fn_sgu(blk, su, sv, nw, nb, ws, bm):
    u = _gelu(su)
    g = _gelu(sv)
    mu = jnp.mean(g, axis=-1, keepdims=True)
    var = jnp.mean(jnp.square(g - mu), axis=-1, keepdims=True)
    v = (g - mu) * lax.rsqrt(var + EPS) * nw + nb
    out = []
    for c in range(su.shape[0] // SGU_CHUNK):
        vc = v[c * SGU_CHUNK:(c + 1) * SGU_CHUNK]
        s = bm
        for h in range(4):
            vh = jnp.where(_lane_group_mask(256, 64, h), vc, 0.0)
            s = s + _dot(ws[h].astype(BF16), vh.astype(BF16))
        out.append(u[c * SGU_CHUNK:(c + 1) * SGU_CHUNK] * s)
    return (jnp.concatenate(out, axis=0),)


def fn_gates(blk, pg, wg, bg):
    z = _dot(pg.astype(BF16), wg.astype(BF16)) + bg
    g = _log_sigmoid(z) * (1.0 / GLA_TAU)
    return g[:, :128], g[:, 128:]


def _scan_rows(x, rev):
    n = x.shape[0]
    row = lax.broadcasted_iota(jnp.int32, x.shape, 0)
    d = 1
    while d < n:
        if rev:
            x = x + jnp.where(row < n - d, pltpu.roll(x, n - d, 0), 0.0)
        else:
            x = x + jnp.where(row >= d, pltpu.roll(x, d, 0), 0.0)
        d *= 2
    return x


@functools.partial(jax.custom_vjp, nondiff_argnums=(1,))
def _cumsum_rows(x, rev):
    return _scan_rows(x, rev)


def _cumsum_rows_fwd(x, rev):
    return _scan_rows(x, rev), None


def _cumsum_rows_bwd(rev, _, dy):
    return (_scan_rows(dy, not rev),)


_cumsum_rows.defvjp(_cumsum_rows_fwd, _cumsum_rows_bwd)


def _gla_chunk_terms(g, rev):
    return _cumsum_rows(g, rev), jnp.sum(g, axis=0, keepdims=True)


def _bd_mask():
    r = lax.broadcasted_iota(jnp.int32, (128, 256), 0)
    c = lax.broadcasted_iota(jnp.int32, (128, 256), 1)
    return (r // GLA_DK) == (c // 64)


def _gla_kv_chunk(k, v, g, rev):
    b, tot = _gla_chunk_terms(g, rev)
    kd = k * jnp.exp(tot - b)
    u = jnp.where(_bd_mask(), _dot(kd.astype(BF16), v.astype(BF16), ((0,), (0,))), 0.0)
    r = lax.broadcasted_iota(jnp.int32, (128, 128), 0)
    c = lax.broadcasted_iota(jnp.int32, (128, 128), 1)
    col = jnp.sum(jnp.where(r == c, jnp.broadcast_to(jnp.exp(tot), (128, 128)), 0.0), axis=1, keepdims=True)
    return u, jnp.broadcast_to(col, (128, 128))


def _gla_o_chunk(q, k, v, g, s, rev):
    b, _ = _gla_chunk_terms(g, rev)
    qe = q * jnp.exp(b) * (GLA_DK ** -0.5)
    ke = k * jnp.exp(-b)
    o = _dot(qe.astype(BF16), jnp.where(_bd_mask(), s, 0.0).astype(BF16))
    qs = jnp.concatenate([jnp.where(_lane_group_mask(128, GLA_DK, h), qe, 0.0) for h in range(4)], axis=0)
    a = _dot(qs.astype(BF16), ke.astype(BF16), ((1,), (1,)))
    i = lax.broadcasted_iota(jnp.int32, a.shape, 0) % GLA_CHUNK
    j = lax.broadcasted_iota(jnp.int32, a.shape, 1)
    a = jnp.where((j >= i) if rev else (j <= i), a, 0.0)
    av = _dot(a.astype(BF16), v.astype(BF16))
    for h in range(4):
        o = o + jnp.where(_lane_group_mask(256, 64, h), av[GLA_CHUNK * h:GLA_CHUNK * (h + 1)], 0.0)
    return o


def fn_gla_kv(blk, k, v, gf, gb):
    uf, ef, ub, eb = [], [], [], []
    for c in range(k.shape[0] // GLA_CHUNK):
        sl = slice(c * GLA_CHUNK, (c + 1) * GLA_CHUNK)
        u, e = _gla_kv_chunk(k[sl], v[sl], gf[sl], False)
        uf.append(u[None]); ef.append(e[None])
        u, e = _gla_kv_chunk(k[sl], v[sl], gb[sl], True)
        ub.append(u[None]); eb.append(e[None])
    cat = lambda t: jnp.concatenate(t, axis=0)
    return cat(uf), cat(ef), cat(ub), cat(eb)


def fn_gla_o(blk, q, k, v, gf, gb, gr, sf, sb, nwt):
    out = []
    for c in range(q.shape[0] // GLA_CHUNK):
        sl = slice(c * GLA_CHUNK, (c + 1) * GLA_CHUNK)
        out.append(_gla_o_chunk(q[sl], k[sl], v[sl], gf[sl], sf[c], False)
                   + _gla_o_chunk(q[sl], k[sl], v[sl], gb[sl], sb[c], True))
    o = jnp.concatenate(out, axis=0)
    r = lax.broadcasted_iota(jnp.int32, (256, 256), 0)
    c = lax.broadcasted_iota(jnp.int32, (256, 256), 1)
    head_mean = jnp.where((r // 64) == (c // 64), 1.0 / 64.0, 0.0).astype(F32)
    ms = _dot(o * o, head_mean, precision=HI)
    on = o * lax.rsqrt(ms + EPS) * nwt
    return (on * (gr * _sigmoid(gr)),)


def _rope_partner(x):
    lane = lax.broadcasted_iota(jnp.int32, x.shape, 1)
    return jnp.where((lane // 16) % 2 == 0, pltpu.roll(x, LANES - 16, 1), pltpu.roll(x, 16, 1))


@jax.custom_vjp
def _rope(x, cs, sn):
    return x * cs + _rope_partner(x) * sn


def _rope_fwd(x, cs, sn):
    return _rope(x, cs, sn), (cs, sn)


def _rope_bwd(res, dy):
    cs, sn = res
    return dy * cs + _rope_partner(dy * sn), jnp.zeros_like(cs), jnp.zeros_like(sn)


_rope.defvjp(_rope_fwd, _rope_bwd)


def fn_mla_pre(blk, ckv, dq, kvw, qw):
    return _rms(ckv, kvw), _rms(dq, qw)


def fn_mla_post(blk, kk, qu, kr, cs, sn):
    kro = _rope(kr, cs, sn)
    kcat, q = [], []
    for h in range(4):
        kcat += [kk[:, 128 * h:128 * (h + 1)].astype(F32), kro]
        q += [qu[:, 256 * h:256 * h + 128], _rope(qu[:, 256 * h + 128:256 * (h + 1)], cs, sn)]
    return jnp.concatenate(kcat, axis=1), jnp.concatenate(q, axis=1) * SCORE_SCALE


ATTN_ROWS = 256
NEG = -1e30


def _scores(q, k, k0, context_queries):
    s = _dot(q, k, ((1,), (1,)))
    if context_queries is not None:
        col = k0 + lax.broadcasted_iota(jnp.int32, s.shape, 1)
        s = jnp.where(context_queries & (col >= CTX), NEG, s)
    return s


def flash_fwd(q, kcat, kvu, side=()):
    t = q.shape[0]
    tq = _pick(t, (768, 512, 256))
    tk = _pick(t, (2816, 1536, 768, 512, 256))
    nsub = tq // ATTN_ROWS
    n_side_in, _, _, side_shapes = _exchange_shapes(side)
    n_side = len(side)

    def body(q_ref, k_ref, v_ref, *rest):
        side_in, rest = rest[:n_side_in], rest[n_side_in:]
        o_ref, lse_ref = rest[:2]
        side_out, (m_sc, l_sc, acc_sc), side_sems = rest[2:2 + n_side], rest[2 + n_side:5 + n_side], rest[5 + n_side:]
        h, qi, ki = pl.program_id(0), pl.program_id(1), pl.program_id(2)
        if side:
            starts, forwards, finals = _exchange_phases(side, side_in, side_out, *side_sems)
            at_tile0 = (qi == 0) & (ki == 0)
            last = (h == pl.num_programs(0) - 1) & (qi == pl.num_programs(1) - 1) & (ki == pl.num_programs(2) - 1)
            for when, phase in (((h == 0) & at_tile0, starts), ((h == 2) & at_tile0, forwards)):
                @pl.when(when)
                def _(phase=phase):
                    for run in phase:
                        run()

        @pl.when(ki == 0)
        def _():
            m_sc[...] = jnp.full_like(m_sc, NEG)
            l_sc[...] = jnp.zeros_like(l_sc)
            acc_sc[...] = jnp.zeros_like(acc_sc)

        k, v = k_ref[...], v_ref[...]
        chains = [pl.ds(r * ATTN_ROWS, ATTN_ROWS) for r in range(nsub)]
        scores = [_scores(q_ref[rs, :], k, ki * tk, (qi == 0) if r == 0 else None) for r, rs in enumerate(chains)]
        probs = []
        for rs, s in zip(chains, scores):
            m_old = m_sc[rs, :]
            m_new = jnp.maximum(m_old, jnp.max(s, axis=-1, keepdims=True))
            alpha = jnp.exp2(m_old - m_new)
            p = jnp.exp2(s - m_new)
            l_sc[rs, :] = alpha * l_sc[rs, :] + jnp.sum(p, axis=-1, keepdims=True)
            m_sc[rs, :] = m_new
            probs.append((alpha, p.astype(BF16)))
        for rs, (alpha, p) in zip(chains, probs):
            acc_sc[rs, :] = alpha * acc_sc[rs, :] + _dot(p, v)

        @pl.when(ki == pl.num_programs(2) - 1)
        def _():
            o_ref[...] = acc_sc[...] / l_sc[...]
            lse_ref[...] = jnp.broadcast_to(m_sc[...] + jnp.log2(l_sc[...]), lse_ref.shape)

        if side:
            @pl.when(last)
            def _():
                for run in finals:
                    run()

    any_spec = pl.BlockSpec(memory_space=pl.ANY)
    res = pl.pallas_call(
        body, name="mla_flash_fwd", grid=(4, t // tq, t // tk),
        in_specs=[pl.BlockSpec((tq, 256), lambda h, i, j: (i, h)), pl.BlockSpec((tk, 256), lambda h, i, j: (j, h)),
                  pl.BlockSpec((tk, 128), lambda h, i, j: (j, 4 + h))] + [any_spec] * n_side_in,
        out_specs=[pl.BlockSpec((tq, 128), lambda h, i, j: (i, h)), pl.BlockSpec((tq, 128), lambda h, i, j: (i, h))]
        + [any_spec] * n_side,
        out_shape=[jax.ShapeDtypeStruct((t, 512), F32), jax.ShapeDtypeStruct((t, 512), F32)] + side_shapes,
        scratch_shapes=[pltpu.VMEM((tq, 1), F32), pltpu.VMEM((tq, 1), F32), pltpu.VMEM((tq, 128), F32)]
        + (_exchange_sems(side) if side else []),
        compiler_params=_params(("arbitrary", "arbitrary", "arbitrary")),
    )(q, kcat, kvu, *[a for _, _, arrs in side for a in arrs])
    return res[0], res[1], list(res[2:])


def fn_attn_stats(blk, do, o, lse):
    blocks = []
    for u in range(do.shape[0] // TM):
        rs = slice(u * TM, (u + 1) * TM)
        out = []
        for h in range(4):
            hs = slice(128 * h, 128 * (h + 1))
            d = jnp.sum(do[rs, hs] * o[rs, hs], axis=-1, keepdims=True)
            out.append(lse[rs, hs].T[0:8])
            out.append(jnp.broadcast_to(d, (TM, 128)).T[0:8])
        blocks.append(jnp.concatenate(out, axis=0)[None])
    return (jnp.concatenate(blocks, axis=0),)


def flash_bwd(q, kcat, kvu, dy, stats, side=None):
    t = q.shape[0]
    tq = _pick(t, (2816, 768, 512, 256))
    tk = _pick(t, (768, 512, 256))
    nst = tq // TM
    sd = side or Side(())

    def body(q_ref, k_ref, v_ref, do_ref, st_ref, *rest):
        dq_ref, dk_ref, dv_ref = rest[sd.n_in:sd.n_in + 3]
        finish = sd.start(rest[:sd.n_in], rest[sd.n_in + 3:sd.n_in + 3 + sd.n_out], rest[sd.n_in + 3 + sd.n_out:])
        kj, qi = pl.program_id(1), pl.program_id(2)

        @pl.when(qi == 0)
        def _():
            dk_ref[...] = jnp.zeros_like(dk_ref)
            dv_ref[...] = jnp.zeros_like(dv_ref)

        def step(has_context_queries):
            q_, k, v, do = q_ref[...], k_ref[...], v_ref[...], do_ref[...].astype(BF16)
            lse_row = jnp.concatenate([st_ref[u, 0:1, :] for u in range(nst)], axis=1)
            delta_row = jnp.concatenate([st_ref[u, 8:9, :] for u in range(nst)], axis=1)
            s = _dot(k, q_, ((1,), (1,)))
            if has_context_queries:
                key = kj * tk + lax.broadcasted_iota(jnp.int32, s.shape, 0)
                qry = lax.broadcasted_iota(jnp.int32, s.shape, 1)
                s = jnp.where((qry < CTX) & (key >= CTX), NEG, s)
            p = jnp.exp2(s - lse_row)
            dp = _dot(v, do, ((1,), (1,)))
            ds = (p * (dp - delta_row)).astype(BF16)
            dv_ref[...] += _dot(p.astype(BF16), do)
            dk_ref[...] += LN2 * _dot(ds, q_)
            dq_new = LN2 * _dot(ds, k, ((0,), (0,)))
            rows_ = pl.ds(pl.multiple_of(qi * tq, TM), tq)

            @pl.when(kj == 0)
            def _():
                dq_ref[rows_, :] = dq_new

            @pl.when(kj != 0)
            def _():
                dq_ref[rows_, :] += dq_new

        pl.when(qi == 0)(lambda: step(True))
        pl.when(qi != 0)(lambda: step(False))
        finish()

    res = pl.pallas_call(
        body, name="mla_flash_bwd", grid=(4, t // tk, t // tq),
        in_specs=[pl.BlockSpec((tq, 256), lambda h, j, i: (i, h)), pl.BlockSpec((tk, 256), lambda h, j, i: (j, h)),
                  pl.BlockSpec((tk, 128), lambda h, j, i: (j, 4 + h)), pl.BlockSpec((tq, 128), lambda h, j, i: (i, 4 + h)),
                  pl.BlockSpec((nst, 16, 256), lambda h, j, i: (i, h, 0))] + sd.in_specs,
        out_specs=[pl.BlockSpec((t, 256), lambda h, j, i: (0, h)), pl.BlockSpec((tk, 256), lambda h, j, i: (j, h)),
                   pl.BlockSpec((tk, 128), lambda h, j, i: (j, h))] + sd.out_specs,
        out_shape=[jax.ShapeDtypeStruct((t, 1024), F32), jax.ShapeDtypeStruct((t, 1024), F32),
                   jax.ShapeDtypeStruct((t, 512), F32)] + sd.shapes,
        scratch_shapes=sd.sems,
        compiler_params=_params(("arbitrary", "arbitrary", "arbitrary")),
    )(q, kcat, kvu, dy, stats, *sd.arrays)
    return res[0], res[1], res[2], list(res[3:])


SCAN_BLOCK = CTX // GLA_CHUNK


def _scan_block(t, nb, rev):
    if not rev:
        return t
    return jnp.where(t < 1, 0, nb - t)


def _scan_order(rev):
    return tuple(reversed(range(SCAN_BLOCK))) if rev else tuple(range(SCAN_BLOCK))


def _both_halves(e):
    return jnp.concatenate([e, e], axis=1)


def gla_states(uf, ef, ub, eb):
    nb = uf.shape[0] // SCAN_BLOCK

    def body(uf_ref, ef_ref, ub_ref, eb_ref, sf_ref, sb_ref, sf_sc, sb_sc):
        @pl.when(pl.program_id(0) == 0)
        def _():
            sf_sc[...] = jnp.zeros_like(sf_sc)
            sb_sc[...] = jnp.zeros_like(sb_sc)

        for u_ref, e_ref, s_ref, sc, rev in ((uf_ref, ef_ref, sf_ref, sf_sc, False), (ub_ref, eb_ref, sb_ref, sb_sc, True)):
            s = sc[...]
            for c in _scan_order(rev):
                s_ref[c] = s
                s = _both_halves(e_ref[c]) * s + u_ref[c]
            sc[...] = s

    big = lambda rev: pl.BlockSpec((SCAN_BLOCK, 128, 256), lambda t: (_scan_block(t, nb, rev), 0, 0))
    small = lambda rev: pl.BlockSpec((SCAN_BLOCK, 128, 128), lambda t: (_scan_block(t, nb, rev), 0, 0))
    return pl.pallas_call(
        body, name="gla_states", grid=(nb,),
        in_specs=[big(False), small(False), big(True), small(True)],
        out_specs=[big(False), big(True)],
        out_shape=[jax.ShapeDtypeStruct(uf.shape, F32)] * 2,
        scratch_shapes=[pltpu.VMEM((128, 256), F32)] * 2,
        compiler_params=_params(("arbitrary",)),
    )(uf, ef, ub, eb)


def gla_states_bwd(ef, eb, sf, sb, dsf, dsb):
    nb = ef.shape[0] // SCAN_BLOCK

    def body(ef_ref, eb_ref, sf_ref, sb_ref, dsf_ref, dsb_ref, duf_ref, def_ref, dub_ref, deb_ref, gf_sc, gb_sc):
        @pl.when(pl.program_id(0) == 0)
        def _():
            gf_sc[...] = jnp.zeros_like(gf_sc)
            gb_sc[...] = jnp.zeros_like(gb_sc)

        for e_ref, s_ref, ds_ref, du_ref, de_ref, g_sc, rev in ((ef_ref, sf_ref, dsf_ref, duf_ref, def_ref, gf_sc, False),
                                                                 (eb_ref, sb_ref, dsb_ref, dub_ref, deb_ref, gb_sc, True)):
            g = g_sc[...]
            for k in reversed(_scan_order(rev)):
                du_ref[k] = g
                gs = g * s_ref[k]
                de_ref[k] = gs[:, :128] + gs[:, 128:]
                g = _both_halves(e_ref[k]) * g + ds_ref[k]
            g_sc[...] = g

    big = lambda rev: pl.BlockSpec((SCAN_BLOCK, 128, 256), lambda t: (_scan_block(nb - 1 - t, nb, rev), 0, 0))
    small = lambda rev: pl.BlockSpec((SCAN_BLOCK, 128, 128), lambda t: (_scan_block(nb - 1 - t, nb, rev), 0, 0))
    return pl.pallas_call(
        body, name="gla_states_bwd", grid=(nb,),
        in_specs=[small(False), small(True), big(False), big(True), big(False), big(True)],
        out_specs=[big(False), small(False), big(True), small(True)],
        out_shape=[jax.ShapeDtypeStruct(sf.shape, F32), jax.ShapeDtypeStruct(ef.shape, F32)] * 2,
        scratch_shapes=[pltpu.VMEM((128, 256), F32)] * 2,
        compiler_params=_params(("arbitrary",)),
    )(ef, eb, sf, sb, dsf, dsb)


def loss_head(xt, target, fnw):
    t = xt.shape[0]

    def f(x, tg, w):
        y = _rms(x, w)
        return 0.5 * jnp.sum(jnp.square(y - tg)) * (1.0 / D)

    def body(x_ref, t_ref, w_ref, loss_ref, dx_ref, dw_ref):
        i = pl.program_id(0)

        @pl.when(i == 0)
        def _():
            loss_ref[...] = jnp.zeros_like(loss_ref)
            dw_ref[...] = jnp.zeros_like(dw_ref)

        @pl.when(i < NCTXB)
        def _():
            dx_ref[...] = jnp.zeros_like(dx_ref)

        @pl.when(i >= NCTXB)
        def _():
            val, (dx, dw) = jax.value_and_grad(f, argnums=(0, 2))(x_ref[...], t_ref[...], w_ref[...])
            loss_ref[...] += jnp.broadcast_to(val, loss_ref.shape)
            dx_ref[...] = dx
            dw_ref[...] += dw

    return pl.pallas_call(
        body, name="loss_head", grid=(t // TM,),
        in_specs=[pl.BlockSpec((TM, D), lambda i: (i, 0)), pl.BlockSpec((TM, D), lambda i: (jnp.maximum(i - NCTXB, 0), 0)),
                  pl.BlockSpec((1, D), lambda i: (0, 0))],
        out_specs=[pl.BlockSpec((1, 128), lambda i: (0, 0)), pl.BlockSpec((TM, D), lambda i: (i, 0)),
                   pl.BlockSpec((1, D), lambda i: (0, 0))],
        out_shape=[jax.ShapeDtypeStruct((1, 128), F32), jax.ShapeDtypeStruct((t, D), F32), jax.ShapeDtypeStruct((1, D), F32)],
        compiler_params=_params(("arbitrary",)),
    )(xt, target, fnw)


def _in_to_padded(w):
    out, pos = [], 0
    for src, wd, dst in sorted(IN_GROUPS, key=lambda g: g[2]):
        if dst > pos:
            out.append(jnp.zeros((w.shape[0], dst - pos), w.dtype))
        out.append(w[:, src:src + wd])
        pos = dst + wd
    if pos < P_COLS:
        out.append(jnp.zeros((w.shape[0], P_COLS - pos), w.dtype))
    return jnp.concatenate(out, axis=1)


def _in_from_padded(g):
    return jnp.concatenate([g[:, dst:dst + wd] for _, wd, dst in IN_GROUPS], axis=1)


def _uq_to_padded(w):
    return jnp.pad(w.reshape(256, 4, 192), ((0, 0), (0, 0), (0, 64))).reshape(256, 1024)


def _uq_from_padded(g):
    return g.reshape(256, 4, 256)[:, :, :192].reshape(256, 768)


def _ukv_to_padded(w):
    return w.reshape(256, 4, 2, 128).transpose(0, 2, 1, 3).reshape(256, 1024)


def _ukv_from_padded(g):
    return g.reshape(256, 2, 4, 128).transpose(0, 2, 1, 3).reshape(256, 1024)


def _rope_tables(n):
    freq = ROPE_BASE ** (-jnp.arange(16, dtype=F32) * 2.0 / 32.0)
    grid_h = n // GRID_W
    ar = jnp.repeat(jnp.arange(grid_h, dtype=F32)[:, None] * freq[None, :], GRID_W, axis=0)
    ac = jnp.tile(jnp.arange(GRID_W, dtype=F32)[:, None] * freq[None, :], (grid_h, 1))
    z = jnp.zeros((n, 64), F32)
    cs = jnp.concatenate([jnp.cos(ar), jnp.cos(ar), jnp.cos(ac), jnp.cos(ac), z], axis=1)
    sn = jnp.concatenate([-jnp.sin(ar), jnp.sin(ar), -jnp.sin(ac), jnp.sin(ac), z], axis=1)
    cs_c = jnp.concatenate([jnp.ones((CTX, 64), F32), jnp.zeros((CTX, 64), F32)], axis=1)
    return jnp.concatenate([cs_c, cs], axis=0), jnp.concatenate([jnp.zeros((CTX, 128), F32), sn], axis=0)


def _small_views(sp):
    wg = jnp.concatenate([jnp.pad(sp["gla_wg_fwd"], ((0, 112), (0, 0))), jnp.pad(sp["gla_wg_bwd"], ((16, 96), (0, 0)))], axis=1)
    return dict(
        n1w=sp["norm1_w"][None], n2w=sp["norm2_w"][None],
        sgu_nw=sp["sgu_norm_w"][None], sgu_nb=sp["sgu_norm_b"][None], sgu_w=sp["sgu_w"],
        sgu_bm=jnp.repeat(sp["sgu_b"].T, 64, axis=1),
        wg=wg, bg=jnp.concatenate([sp["gla_bg_fwd"], sp["gla_bg_bwd"]])[None],
        gla_nwt=jnp.tile(sp["gla_norm_w"], 4)[None],
        kvw=sp["mla_kv_norm_w"][None], qw=sp["mla_q_norm_w"][None])


def _small_grads(g):
    return dict(
        norm1_w=g["n1w"][0], norm2_w=g["n2w"][0],
        sgu_norm_w=g["sgu_nw"][0], sgu_norm_b=g["sgu_nb"][0], sgu_w=g["sgu_w"],
        sgu_b=g["sgu_bm"].reshape(128, 4, 64).sum(-1).T,
        gla_wg_fwd=g["wg"][0:16, 0:128], gla_wg_bwd=g["wg"][16:32, 128:256],
        gla_bg_fwd=g["bg"][0, 0:128], gla_bg_bwd=g["bg"][0, 128:256],
        gla_norm_w=g["gla_nwt"].reshape(4, 64).sum(0),
        mla_kv_norm_w=g["kvw"][0], mla_q_norm_w=g["qw"][0])


def _big_views(full):
    views = {}
    if "w_in" in full:
        views.update(win=_in_to_padded(full["w_in"]), wuq=_uq_to_padded(full["mla_w_uq"]),
                     wukv=_ukv_to_padded(full["mla_w_ukv"]), wout=full["w_out"])
    if "w_ff1" in full:
        views.update(w1=full["w_ff1"], w2=full["w_ff2"])
    return views


def _gla_tile(t):
    return _pick(t, (768, 512, 256))


def _layer_ops(p, sv, a):
    gt = _gla_tile(p.shape[0])
    pc = lambda off, w: rows(p, w, off // w, tm=gt)
    gr = lambda arr, w=None, cb=0: rows(arr, w, cb, tm=gt)
    return dict(
        sgu=[pc(P_SU, 256), pc(P_SV, 256), const(sv["sgu_nw"]), const(sv["sgu_nb"]), const(sv["sgu_w"]), const(sv["sgu_bm"])],
        gates=[pc(P_GATE, 128), const(sv["wg"]), const(sv["bg"])],
        mla_pre=[pc(P_CKV, 256), pc(P_DQ, 256), const(sv["kvw"]), const(sv["qw"])],
        gla_kv=lambda: [pc(P_GK, 128), pc(P_GV, 256), gr(a["gf"]), gr(a["gb"])],
        gla_o=lambda: [pc(P_GQ, 128), pc(P_GK, 128), pc(P_GV, 256), gr(a["gf"]), gr(a["gb"]), pc(P_GR, 256),
                       chunks(a["sf"], gt // GLA_CHUNK), chunks(a["sb"], gt // GLA_CHUNK), const(sv["gla_nwt"])],
        mla_post=lambda: [gr(a["kvu"], 512, 0), gr(a["qu"]), pc(P_KR, 128), gr(a["cs"]), gr(a["sn"])])


def layer_fwd(l, xt, modl, bw, sv, tabs, side=(), late=None):
    t = xt.shape[0]
    g, nc, gt = t // TM, t // GLA_CHUNK, _gla_tile(t)
    gg, cpt = t // gt, gt // GLA_CHUNK
    nm = lambda s: f"l{l}_{s}"
    a = dict(x=xt, cs=tabs[0], sn=tabs[1])
    a["h"], = rw(nm("norm1"), fn_norm1, [rows(xt, tm=gt), const(modl), const(sv["n1w"])], [rowout(t, D, BF16, tm=gt)], gg)
    p = a["p"] = mm(nm("in_proj"), a["h"], bw["win"], F32)
    ops = _layer_ops(p, sv, a)
    y_sgu, = rw(nm("sgu"), fn_sgu, ops["sgu"], [rowout(t, 256, BF16, tm=gt)], gg)
    a["gf"], a["gb"] = rw(nm("gates"), fn_gates, ops["gates"], [rowout(t, 128, F32, tm=gt)] * 2, gg)
    a["uf"], a["ef"], a["ub"], a["eb"] = rw(nm("gla_kv"), fn_gla_kv, ops["gla_kv"](),
                                           [chunkout((nc, 128, 256), F32, cpt), chunkout((nc, 128, 128), F32, cpt)] * 2, gg)
    a["sf"], a["sb"] = gla_states(a["uf"], a["ef"], a["ub"], a["eb"])
    y_gla, = rw(nm("gla_o"), fn_gla_o, ops["gla_o"](), [rowout(t, 256, BF16, tm=gt)], gg)
    a["ckvn"], a["dqn"] = rw(nm("mla_pre"), fn_mla_pre, ops["mla_pre"], [rowout(t, 256, BF16, tm=gt)] * 2, gg)
    a["kvu"] = mm(nm("kv_up"), a["ckvn"], bw["wukv"], BF16)
    a["qu"] = mm(nm("q_up"), a["dqn"], bw["wuq"], F32)
    a["kcat"], a["q"] = rw(nm("mla_post"), fn_mla_post, ops["mla_post"](), [rowout(t, 1024, BF16, tm=gt)] * 2, gg)
    a["o"], a["lse"], side_out = flash_fwd(a["q"], a["kcat"], a["kvu"], side)
    if late is not None:
        late(side_out)
    a["y"] = jnp.concatenate([y_sgu, y_gla, a["o"].astype(BF16)], axis=1)
    a["yo"] = mm(nm("out_proj"), a["y"], bw["wout"], F32)
    a["x1"], a["h2"] = rw(nm("res_norm2"), fn_res_norm2,
                          [rows(xt, tm=gt), rows(a["yo"], tm=gt), const(modl), const(sv["n2w"])],
                          [rowout(t, D, F32, tm=gt), rowout(t, D, BF16, tm=gt)], gg)
    a["act"] = mm(nm("ff1"), a["h2"], bw["w1"], BF16, post=lambda acc: jnp.maximum(acc, 0.0))
    a["f"] = mm(nm("ff2"), a["act"], bw["w2"], F32, pre=_square_bf16)
    x2, = rw(nm("res2"), fn_res2, [rows(a["x1"], tm=gt), rows(a["f"], tm=gt), const(modl)], [rowout(t, D, F32, tm=gt)], gg)
    return x2, a


def fn_assemble(blk, gv1, gv2, ckv, su, sv_, gr, dq, gk1, gk2, pg, kr, gq):
    return (jnp.concatenate([gv1 + gv2, ckv, su, sv_, gr, dq, gk1 + gk2, pg, kr, gq], axis=1),)


def layer_bwd(l, dx2, a, modl, bw, sv, side_a=None, make_side_b=None, ff_side=None):
    t = dx2.shape[0]
    g, gt = t // TM, _gla_tile(t)
    gg, cpt = t // gt, gt // GLA_CHUNK
    ht = _pick(t, (384, 256))
    nm = lambda s: f"l{l}_{s}_bwd"
    p = a["p"]
    ops = _layer_ops(p, sv, a)
    gw, gs = {}, {}
    df, dm_a = rw_vjp(nm("res2"), fn_res2, [rows(a["x1"], tm=gt), rows(a["f"], tm=gt), const(modl)], [rows(dx2, tm=gt)],
                      [1, 2], gg, gdt=[BF16, F32])
    during_attention = []
    if side_a:
        gw["w2"], got_a = mm_tn(nm("ff2_w"), a["act"], df, pre=_square_bf16, side=Side(side_a))
        during_attention += make_side_b(got_a)
    else:
        gw["w2"] = mm_tn(nm("ff2_w"), a["act"], df, pre=_square_bf16)
    du = mm(nm("ff2_x"), df, bw["w2"], BF16, post=lambda acc, act: acc * (2.0 * act.astype(F32)), extras=(a["act"],), bt=True)
    gw["w1"] = mm_tn(nm("ff1_w"), a["h2"], du)
    dh2 = mm(nm("ff1_x"), du, bw["w1"], F32, bt=True)
    dxa, dyo, dm_b, gs["n2w"] = rw_vjp(nm("res_norm2"), fn_res_norm2,
                                       [rows(a["x"], tm=ht), rows(a["yo"], tm=ht), const(modl), const(sv["n2w"])],
                                       [rows(dx2, tm=ht), rows(dh2, tm=ht)], [0, 1, 2, 3], t // ht, gdt=[F32, BF16, F32, F32])
    if ff_side is not None:
        ff_entries, make_ff_next = ff_side(dict(w_ff1=gw["w1"], w_ff2=gw["w2"]))
        gw["wout"], got_ff = mm_tn(nm("out_w"), a["y"], dyo, side=Side(ff_entries))
        during_attention += make_ff_next(got_ff)
    else:
        gw["wout"] = mm_tn(nm("out_w"), a["y"], dyo)
    dy = mm(nm("out_x"), dyo, bw["wout"], F32, bt=True)
    dsu, dsv, gs["sgu_nw"], gs["sgu_nb"], gs["sgu_w"], gs["sgu_bm"] = rw_vjp(
        nm("sgu"), fn_sgu, ops["sgu"], [rows(dy, 256, 0, tm=gt)], [0, 1, 2, 3, 4, 5], gg)
    dgq, dgk1, dgv1, dgf1, dgb1, dgr, dsf, dsb, gs["gla_nwt"] = rw_vjp(
        nm("gla_o"), fn_gla_o, ops["gla_o"](), [rows(dy, 256, 1, tm=gt)], list(range(9)), gg)
    duf, def_, dub, deb = gla_states_bwd(a["ef"], a["eb"], a["sf"], a["sb"], dsf, dsb)
    dgk2, dgv2, dgf, dgb = rw_vjp(nm("gla_kv"), fn_gla_kv, ops["gla_kv"](),
                                  [chunks(duf, cpt), chunks(def_, cpt), chunks(dub, cpt), chunks(deb, cpt)], [0, 1, 2, 3], gg,
                                  adds={2: rows(dgf1, tm=gt), 3: rows(dgb1, tm=gt)})
    dpg, gs["wg"], gs["bg"] = rw_vjp(nm("gates"), fn_gates, ops["gates"], [rows(dgf, tm=gt), rows(dgb, tm=gt)], [0, 1, 2], gg)
    stats, = rw(nm("attn_stats"), fn_attn_stats, [rows(dy, 512, 1, tm=gt), rows(a["o"], tm=gt), rows(a["lse"], tm=gt)],
                [chunkout((g, 64, TM), F32, gt // TM)], gg)
    dq, dkcat, dv, got_b = flash_bwd(a["q"], a["kcat"], a["kvu"], dy, stats,
                                     side=Side(during_attention) if during_attention else None)
    post_ins = [blank_rows(t, 512, BF16, gt), blank_rows(t, 1024, F32, gt), blank_rows(t, 128, F32, gt)] + ops["mla_post"]()[3:]
    dkk, dqu, dkr = rw_vjp(nm("mla_post"), fn_mla_post, post_ins, [rows(dkcat, tm=gt), rows(dq, tm=gt)], [0, 1, 2],
                           gg, gdt=[BF16, BF16, F32])
    dkvu = jnp.concatenate([dkk, dv.astype(BF16)], axis=1)
    gw["wukv"] = mm_tn(nm("kv_up_w"), a["ckvn"], dkvu)
    gw["wuq"] = mm_tn(nm("q_up_w"), a["dqn"], dqu)
    dckvn = mm(nm("kv_up_x"), dkvu, bw["wukv"], F32, bt=True)
    ddqn = mm(nm("q_up_x"), dqu, bw["wuq"], F32, bt=True)
    dckv, ddq, gs["kvw"], gs["qw"] = rw_vjp(nm("mla_pre"), fn_mla_pre, ops["mla_pre"],
                                            [rows(dckvn, tm=gt), rows(ddqn, tm=gt)], [0, 1, 2, 3], gg)
    dp, = rw(nm("assemble"), fn_assemble,
             [rows(x_, tm=gt) for x_ in (dgv1, dgv2, dckv, dsu, dsv, dgr, ddq, dgk1, dgk2, dpg, dkr, dgq)],
             [rowout(t, P_COLS, BF16, tm=gt)], gg)
    gw["win"] = mm_tn(nm("in_w"), a["h"], dp)
    dh = mm(nm("in_x"), dp, bw["win"], F32, bt=True)
    dx, dm_c, gs["n1w"] = rw_vjp(nm("norm1"), fn_norm1, [rows(a["x"], tm=gt), const(modl), const(sv["n1w"])],
                                 [rows(dh, tm=gt)], [0, 1, 2], gg, adds={0: rows(dxa, tm=gt)})
    big = dict(w_in=_in_from_padded(gw["win"]), w_out=gw["wout"], mla_w_uq=_uq_from_padded(gw["wuq"]),
               mla_w_ukv=_ukv_from_padded(gw["wukv"]), w_ff1=gw["w1"], w_ff2=gw["w2"])
    return dx, dm_a + dm_b + dm_c, big, _small_grads(gs), got_b


SMALL_NAMES = ("norm1_w", "sgu_norm_w", "sgu_norm_b", "sgu_w", "sgu_b", "gla_wg_fwd", "gla_bg_fwd", "gla_wg_bwd",
               "gla_bg_bwd", "gla_norm_w", "mla_q_norm_w", "mla_kv_norm_w", "norm2_w")
BIG_NAMES = ("w_in", "w_out", "mla_w_uq", "mla_w_ukv", "w_ff1", "w_ff2")
ATTN_WEIGHTS, FF_WEIGHTS = BIG_NAMES[:4], BIG_NAMES[4:]


def local_step(x, ctx, target, mods, big, small, final_norm_w, side=(), on_side=None, grad_side=None, ff_side=None):
    n = x.shape[0]
    xt = jnp.concatenate([ctx, x], axis=0)
    tabs = _rope_tables(n)
    depth = len(mods)
    big = list(big)
    svs = [_small_views(small[l]) for l in range(depth)]
    acts, bws = [], []
    for l in range(depth):
        bws.append(_big_views(big[l]))
        late = None
        if l == 0 and on_side is not None:
            def late(results):
                rest0, later = on_side(results)
                bws[0].update(_big_views(rest0))
                big.extend(later)
        xt, a = layer_fwd(l, xt, mods[l], bws[l], svs[l], tabs, side if l == 0 else (), late)
        acts.append(a)
    loss, dxt, dfnw = loss_head(xt, target, final_norm_w[None])
    dmods, gbig, gsmall = [None] * depth, [None] * depth, [None] * depth
    got = []
    for l in reversed(range(depth)):
        hooks = (None, None, None)
        if l == 0 and grad_side is not None and depth > 1:
            hooks = (*grad_side(gbig[1:]), ff_side)
        dxt, dmods[l], gbig[l], gsmall[l], got = layer_bwd(l, dxt, acts[l], mods[l], bws[l], svs[l], *hooks)
    return loss, dxt[CTX:], dmods, gbig, gsmall, dfnw, got


def _group(group):
    x, y, c = lax.axis_index("x"), lax.axis_index("y"), lax.axis_index("c")
    if group == "sib":
        return 2, c, [((x, y, 1 - c), 1 - c)]
    if group == "chip":
        flips = [(1, 0), (0, 1), (1, 1)]
        return 4, 2 * x + y, [((x ^ fx, y ^ fy, c), 2 * (x ^ fx) + (y ^ fy)) for fx, fy in flips]
    flips = [(fx, fy, fc) for fx in (0, 1) for fy in (0, 1) for fc in (0, 1)][1:]
    return 8, 4 * x + 2 * y + c, [((x ^ fx, y ^ fy, c ^ fc), 4 * (x ^ fx) + 2 * (y ^ fy) + (c ^ fc)) for fx, fy, fc in flips]


def _group_size(group):
    return {"sib": 2, "chip": 4, "all": 8}[group]


REMOTE_COPIES = {"gather": None, "scatter": None, "swap": 1, "gather2": 6}


def _exchange_shapes(entries):
    n_in = sum(len(arrs) for _, _, arrs in entries)
    n_remote = sum(REMOTE_COPIES[k] or _group_size(g) - 1 for k, g, _ in entries)
    n_local = sum(1 for k, _, _ in entries if k != "swap")
    out_shape = []
    for kind, group, arrs in entries:
        a = arrs[0]
        if kind in ("gather", "gather2"):
            out_shape.append(jax.ShapeDtypeStruct((_group_size(group),) + a.shape, a.dtype))
        elif kind == "swap" and len(arrs) == 1:
            out_shape.append(jax.ShapeDtypeStruct(a.shape[1:], a.dtype))
        else:
            out_shape.append(jax.ShapeDtypeStruct(a.shape, a.dtype))
    return n_in, n_remote, n_local, out_shape


def _exchange_sems(entries):
    _, n_remote, n_local, _ = _exchange_shapes(entries)
    return [pltpu.SemaphoreType.DMA((n_remote,)), pltpu.SemaphoreType.DMA((n_remote,)), pltpu.SemaphoreType.DMA((max(n_local, 1),))]


def _exchange_phases(entries, in_refs, out_refs, send_sems, recv_sems, local_sems):
    x, y, c = lax.axis_index("x"), lax.axis_index("y"), lax.axis_index("c")

    def remote(src, dst, k, dev):
        return pltpu.make_async_remote_copy(src_ref=src, dst_ref=dst, send_sem=send_sems.at[k], recv_sem=recv_sems.at[k],
                                            device_id=dev, device_id_type=MESH)

    pos, k, kl = 0, 0, 0
    starts, forwards, finals = [], [], []
    for (kind, group, arrs), out in zip(entries, out_refs):
        srcs = in_refs[pos:pos + len(arrs)]
        pos += len(arrs)
        _, mine, peers = _group(group)
        if kind == "swap":
            (dev, _), = peers
            if len(srcs) == 1:
                starts.append(remote(srcs[0].at[1 - c], out, k, dev).start)
                finals.append(remote(srcs[0].at[0], out, k, dev).wait)
            else:
                def start_swap(srcs=srcs, k=k, dev=dev, out=out):
                    for core, src in ((0, srcs[1]), (1, srcs[0])):
                        @pl.when(c == core)
                        def _(src=src):
                            remote(src, out, k, dev).start()

                starts.append(start_swap)
                finals.append(remote(srcs[0], out, k, dev).wait)
            k += 1
            continue
        src = srcs[0]
        own = pltpu.make_async_copy(src if kind != "scatter" else src.at[mine], out.at[mine], local_sems.at[kl])
        starts.append(own.start)
        finals.append(own.wait)
        kl += 1
        if kind == "gather2":
            sibling = (x, y, 1 - c)
            for f, (dev, slot) in enumerate(peers):
                starts.append(remote(src.at[c], out.at[mine, c], k + f, dev).start)
                arrival = remote(src.at[c], out.at[slot, c], k + f, dev)

                def forward(arrival=arrival, slot=slot, kf=k + 3 + f, out=out):
                    arrival.wait_recv()
                    remote(out.at[slot, c], out.at[slot, c], kf, sibling).start()

                forwards.append(forward)
                finals.append(arrival.wait_send)
                finals.append(remote(out.at[slot, c], out.at[slot, 1 - c], k + 3 + f, sibling).wait)
            k += 6
            continue
        for dev, slot in peers:
            piece = src if kind == "gather" else src.at[slot]
            starts.append(remote(piece, out.at[mine], k, dev).start)
            finals.append(remote(piece, out.at[slot], k, dev).wait)
            k += 1
    return starts, forwards, finals


class Side:
    def __init__(self, entries):
        self.entries = tuple(entries)
        self.n_in, _, _, self.shapes = _exchange_shapes(self.entries)
        self.n_out = len(self.entries)
        self.arrays = [a for _, _, arrs in self.entries for a in arrs]
        any_spec = pl.BlockSpec(memory_space=pl.ANY)
        self.in_specs, self.out_specs = [any_spec] * self.n_in, [any_spec] * self.n_out
        self.sems = _exchange_sems(self.entries) if self.entries else []

    def start(self, in_refs, out_refs, sem_refs):
        if not self.entries:
            return lambda: None
        ids = [pl.program_id(d) for d in range(3)]
        first = (ids[0] == 0) & (ids[1] == 0) & (ids[2] == 0)
        last = ((ids[0] == pl.num_programs(0) - 1) & (ids[1] == pl.num_programs(1) - 1) & (ids[2] == pl.num_programs(2) - 1))
        starts, forwards, finals = _exchange_phases(self.entries, in_refs, out_refs, *sem_refs)
        assert not forwards

        @pl.when(first)
        def _():
            for run in starts:
                run()

        def finish():
            @pl.when(last)
            def _():
                for run in finals:
                    run()

        return finish


def xchg(name, entries):
    n_in, _, _, out_shape = _exchange_shapes(entries)

    def body(*refs):
        in_refs, out_refs = refs[:n_in], refs[n_in:n_in + len(entries)]
        for phase in _exchange_phases(entries, in_refs, out_refs, *refs[n_in + len(entries):]):
            for run in phase:
                run()

    any_spec = pl.BlockSpec(memory_space=pl.ANY)
    return pl.pallas_call(
        body, name=name,
        in_specs=[any_spec] * n_in, out_specs=[any_spec] * len(entries), out_shape=out_shape,
        scratch_shapes=_exchange_sems(entries),
    )(*[a for _, _, arrs in entries for a in arrs])


def _block_rows(r, c, budget=131072):
    tr = 8
    while tr * 2 * c <= budget and r % (tr * 2) == 0:
        tr *= 2
    return tr if r % tr == 0 else r


def tree_sum(name, parts):
    g, r, c = parts.shape
    tr = _block_rows(r, c)

    def body(p_ref, o_ref):
        p = [p_ref[i].astype(F32) for i in range(g)]
        while len(p) > 1:
            p = [p[i] + p[i + 1] for i in range(0, len(p), 2)]
        o_ref[...] = p[0]

    return pl.pallas_call(
        body, name=name, grid=(r // tr,),
        in_specs=[pl.BlockSpec((g, tr, c), lambda i: (0, i, 0))], out_specs=pl.BlockSpec((tr, c), lambda i: (i, 0)),
        out_shape=jax.ShapeDtypeStruct((r, c), F32), compiler_params=_params(("arbitrary",)),
    )(parts)


def pair_sum(name, halves, recv, core):
    r, c = recv.shape
    tr = _block_rows(r, c)

    def body(h_ref, r_ref, k_ref, o_ref):
        o_ref[...] = (jnp.where(k_ref[...] > 0.5, h_ref[1], h_ref[0]) + r_ref[...]).astype(o_ref.dtype)

    blk = pl.BlockSpec((tr, c), lambda i: (i, 0))
    return pl.pallas_call(
        body, name=name, grid=(r // tr,),
        in_specs=[pl.BlockSpec((2, tr, c), lambda i: (0, i, 0)), blk, pl.BlockSpec((1, 1), lambda i: (0, 0))],
        out_specs=blk, out_shape=jax.ShapeDtypeStruct((r, c), BF16), compiler_params=_params(("arbitrary",)),
    )(halves, recv, core)


def adamw(name, w, g, m, v):
    r, c = w.shape
    tr = _block_rows(r, c)

    def body(w_ref, g_ref, m_ref, v_ref, d_ref, nm_ref, nv_ref):
        gg = g_ref[...]
        nm = ADAM_B1 * m_ref[...] + (1.0 - ADAM_B1) * gg
        nv = ADAM_B2 * v_ref[...] + (1.0 - ADAM_B2) * jnp.square(gg)
        m_hat = nm / (1.0 - ADAM_B1 ** ADAM_STEP)
        v_hat = nv / (1.0 - ADAM_B2 ** ADAM_STEP)
        d_ref[...] = -ADAM_LR * (m_hat / (jnp.sqrt(v_hat) + ADAM_EPS) + ADAM_WD * w_ref[...])
        nm_ref[...] = nm
        nv_ref[...] = nv

    blk = pl.BlockSpec((tr, c), lambda i: (i, 0))
    return pl.pallas_call(
        body, name=name, grid=(r // tr,), in_specs=[blk] * 4, out_specs=[blk] * 3,
        out_shape=[jax.ShapeDtypeStruct((r, c), F32)] * 3, compiler_params=_params(("arbitrary",)),
    )(w, g, m, v)


W_MOD_COLS = 6 * D // 4
MOD_TN = 512


def mod_project(c16, w_mod, b_loc):
    def body(c_ref, w_ref, b_ref, o_ref):
        cv = c_ref[...]
        s = (cv * _sigmoid(cv)).astype(BF16)
        o_ref[0] = _dot(s, w_ref[0].astype(BF16)) + b_ref[0]

    return pl.pallas_call(
        body, name="mod_project", grid=(2, W_MOD_COLS // MOD_TN),
        in_specs=[pl.BlockSpec((16, D), lambda l, j: (0, 0)), pl.BlockSpec((1, D, MOD_TN), lambda l, j: (l, 0, j)),
                  pl.BlockSpec((1, 1, MOD_TN), lambda l, j: (l, 0, j))],
        out_specs=pl.BlockSpec((1, 16, MOD_TN), lambda l, j: (l, 0, j)),
        out_shape=jax.ShapeDtypeStruct((2, 16, W_MOD_COLS), F32), compiler_params=_params(("arbitrary", "arbitrary")),
    )(c16, w_mod, b_loc)


def mod_weight_grad(c16, dm16):
    def body(c_ref, d_ref, o_ref):
        cv = c_ref[...]
        o_ref[0] = _dot(cv * _sigmoid(cv), d_ref[0], ((0,), (0,)), precision=HI)

    return pl.pallas_call(
        body, name="mod_weight_grad", grid=(2, W_MOD_COLS // MOD_TN),
        in_specs=[pl.BlockSpec((16, D), lambda l, j: (0, 0)), pl.BlockSpec((1, 16, MOD_TN), lambda l, j: (l, 0, j))],
        out_specs=pl.BlockSpec((1, D, MOD_TN), lambda l, j: (l, 0, j)),
        out_shape=jax.ShapeDtypeStruct((2, D, W_MOD_COLS), F32), compiler_params=_params(("arbitrary", "arbitrary")),
    )(c16, dm16)


def cctx_partial(dmc, w_mod):
    def body(d_ref, w_ref, o_ref):
        @pl.when(pl.program_id(0) == 0)
        def _():
            o_ref[...] = jnp.zeros_like(o_ref)
        o_ref[...] += _dot(d_ref[0], w_ref[0], ((1,), (1,)), precision=HI)

    return pl.pallas_call(
        body, name="cctx_partial", grid=(2,),
        in_specs=[pl.BlockSpec((1, 8, W_MOD_COLS), lambda l: (l, 0, 0)), pl.BlockSpec((1, D, W_MOD_COLS), lambda l: (l, 0, 0))],
        out_specs=pl.BlockSpec((8, D), lambda l: (0, 0)),
        out_shape=jax.ShapeDtypeStruct((8, D), F32), compiler_params=_params(("arbitrary",)),
    )(dmc, w_mod)


def cctx_grad(parts, c_ctx8):
    def body(p_ref, c_ref, o_ref):
        ds = (p_ref[0] + p_ref[1]) + (p_ref[2] + p_ref[3])
        _, vf = jax.vjp(lambda z: z * _sigmoid(z), c_ref[...])
        o_ref[...] = vf(ds)[0]

    return pl.pallas_call(
        body, name="cctx_grad", out_shape=jax.ShapeDtypeStruct((8, D), F32),
    )(parts, c_ctx8)


ARG_NAMES = ("x", "c", "ctx", "c_ctx", "w_mod", "b_mod", "norm1_w", "w_in", "w_out", "sgu_norm_w", "sgu_norm_b", "sgu_w",
             "sgu_b", "gla_wg_fwd", "gla_bg_fwd", "gla_wg_bwd", "gla_bg_bwd", "gla_norm_w", "mla_q_norm_w", "mla_w_uq",
             "mla_kv_norm_w", "mla_w_ukv", "norm2_w", "w_ff1", "w_ff2", "final_norm_w")
WEIGHT_NAMES = ARG_NAMES[3:]
PACKED = ("c_ctx", "b_mod") + SMALL_NAMES + ("final_norm_w",)
ROW_SHARDED = ("w_out", "w_ff2")
PACK_ROWS = 256


def _pack(vectors):
    flat = jnp.concatenate([v.reshape(-1) for v in vectors])
    n = flat.shape[0]
    total = -(-n // (PACK_ROWS * LANES)) * PACK_ROWS * LANES
    return jnp.pad(flat, (0, total - n)).reshape(-1, LANES)


def _unpack(buf, shapes):
    flat, out, pos = buf.reshape(-1), [], 0
    for shp in shapes:
        n = int(np.prod(shp))
        out.append(flat[pos:pos + n].reshape(shp))
        pos += n
    return out


def _full_weight(name, g):
    if name in ROW_SHARDED:
        return g.reshape(-1, g.shape[-1])
    return g.transpose(1, 0, 2).reshape(g.shape[1], -1)


def _chip_chunks(name, a):
    if name in ROW_SHARDED:
        return a.reshape(4, a.shape[0] // 4, a.shape[1])
    return a.reshape(a.shape[0], 4, a.shape[1] // 4).transpose(1, 0, 2)


def kernel(x, c, ctx, c_ctx, w_mod, b_mod, norm1_w, w_in, w_out, sgu_norm_w, sgu_norm_b, sgu_w, sgu_b, gla_wg_fwd, gla_bg_fwd, gla_wg_bwd, gla_bg_bwd, gla_norm_w, mla_q_norm_w, mla_w_uq, mla_kv_norm_w, mla_w_ukv, norm2_w, w_ff1, w_ff2, final_norm_w, loss_target, m_c_ctx, m_w_mod, m_b_mod, m_norm1_w, m_w_in, m_w_out, m_sgu_norm_w, m_sgu_norm_b, m_sgu_w, m_sgu_b, m_gla_wg_fwd, m_gla_bg_fwd, m_gla_wg_bwd, m_gla_bg_bwd, m_gla_norm_w, m_mla_q_norm_w, m_mla_w_uq, m_mla_kv_norm_w, m_mla_w_ukv, m_norm2_w, m_w_ff1, m_w_ff2, m_final_norm_w, v_c_ctx, v_w_mod, v_b_mod, v_norm1_w, v_w_in, v_w_out, v_sgu_norm_w, v_sgu_norm_b, v_sgu_w, v_sgu_b, v_gla_wg_fwd, v_gla_bg_fwd, v_gla_wg_bwd, v_gla_bg_bwd, v_gla_norm_w, v_mla_q_norm_w, v_mla_w_uq, v_mla_kv_norm_w, v_mla_w_ukv, v_norm2_w, v_w_ff1, v_w_ff2, v_final_norm_w):
    args = (x, c, ctx, c_ctx, w_mod, b_mod, norm1_w, w_in, w_out, sgu_norm_w, sgu_norm_b, sgu_w, sgu_b, gla_wg_fwd, gla_bg_fwd, gla_wg_bwd, gla_bg_bwd, gla_norm_w, mla_q_norm_w, mla_w_uq, mla_kv_norm_w, mla_w_ukv, norm2_w, w_ff1, w_ff2, final_norm_w)
    w = dict(zip(ARG_NAMES, args))
    moms = (m_c_ctx, m_w_mod, m_b_mod, m_norm1_w, m_w_in, m_w_out, m_sgu_norm_w, m_sgu_norm_b, m_sgu_w, m_sgu_b, m_gla_wg_fwd, m_gla_bg_fwd, m_gla_wg_bwd, m_gla_bg_bwd, m_gla_norm_w, m_mla_q_norm_w, m_mla_w_uq, m_mla_kv_norm_w, m_mla_w_ukv, m_norm2_w, m_w_ff1, m_w_ff2, m_final_norm_w)
    vars_ = (v_c_ctx, v_w_mod, v_b_mod, v_norm1_w, v_w_in, v_w_out, v_sgu_norm_w, v_sgu_norm_b, v_sgu_w, v_sgu_b, v_gla_wg_fwd, v_gla_bg_fwd, v_gla_wg_bwd, v_gla_bg_bwd, v_gla_norm_w, v_mla_q_norm_w, v_mla_w_uq, v_mla_kv_norm_w, v_mla_w_ukv, v_norm2_w, v_w_ff1, v_w_ff2, v_final_norm_w)
    m1 = dict(zip(WEIGHT_NAMES, moms))
    m2 = dict(zip(WEIGHT_NAMES, vars_))
    xi, yi, ci = lax.axis_index("x"), lax.axis_index("y"), lax.axis_index("c")
    chip, dev = 2 * xi + yi, 4 * xi + 2 * yi + ci
    depth = w_mod.shape[0]

    def shard_halves(l, names):
        return [("gather2", "chip", [w[k][l].astype(BF16).reshape(2, w[k].shape[1] // 2, w[k].shape[2])]) for k in names]

    def full_weights(names, gathered):
        return {k: _full_weight(k, g.reshape(4, *w[k].shape[1:])) for k, g in zip(names, gathered)}

    got = xchg("gather_inputs", [("gather", "all", [c])] + shard_halves(0, ATTN_WEIGHTS))
    c_all = got[0]
    c16 = jnp.concatenate([c_all.reshape(8, D), c_ctx[None], jnp.zeros((7, D), F32)], axis=0)
    b_loc = lax.dynamic_slice_in_dim(b_mod, chip * W_MOD_COLS, W_MOD_COLS, axis=1)[:, None, :]
    mod_part = mod_project(c16, w_mod, b_loc)
    mod_all, = xchg("gather_mod", [("gather", "chip", [mod_part])])
    mod_full = mod_all.transpose(1, 2, 0, 3).reshape(depth, 16, 6 * D)
    mods = [jnp.stack([mod_full[l, 8], lax.dynamic_index_in_dim(mod_full[l], dev, 0, keepdims=False)])[:, None, :]
            for l in range(depth)]

    small = [{k: w[k][l] for k in SMALL_NAMES} for l in range(depth)]
    n_big = len(BIG_NAMES)
    n_ff = len(FF_WEIGHTS)
    later = shard_halves(0, FF_WEIGHTS) + [e for l in range(1, depth) for e in shard_halves(l, BIG_NAMES)]
    core = ci.astype(F32).reshape(1, 1)

    def on_side(res):
        return (full_weights(FF_WEIGHTS, res[:n_ff]),
                [full_weights(BIG_NAMES, res[n_ff + i * n_big:n_ff + (i + 1) * n_big]) for i in range(depth - 1)])

    def half_major(k, g):
        ch = _chip_chunks(k, g)
        return ch.reshape(4, 2, ch.shape[1] // 2, ch.shape[2]).transpose(1, 0, 2, 3)

    def swap_entries(gb, names):
        hm = [half_major(k, gb[k]) for k in names]
        return hm, [("swap", "sib", [h]) for h in hm]

    def scatter_entries(tag, names, hm, recv):
        out = []
        for k, h, r in zip(names, hm, recv):
            s2 = pair_sum(f"pair_sum_{tag}_{k}", h.reshape(2, -1, h.shape[-1]), r.reshape(-1, r.shape[-1]), core)
            out.append(("scatter", "chip", [s2.reshape(r.shape)]))
        return out

    def grad_side(gb_later):
        hms, entries = [], []
        for gb in gb_later:
            hm, e = swap_entries(gb, BIG_NAMES)
            hms.append(hm)
            entries += e

        def make_scatter(recv):
            return [e for i, hm in enumerate(hms)
                    for e in scatter_entries(f"l{i + 1}", BIG_NAMES, hm, recv[i * n_big:(i + 1) * n_big])]

        return entries, make_scatter

    def ff_side(g_ff):
        hm, entries = swap_entries(g_ff, FF_WEIGHTS)
        return entries, lambda recv: scatter_entries("l0", FF_WEIGHTS, hm, recv)

    loss, grad_x, dmods, gbig, gsmall, dfnw, early_pieces = local_step(
        x[0], ctx[0], loss_target[0], mods, [full_weights(ATTN_WEIGHTS, got[1:])], small, final_norm_w, side=later,
        on_side=on_side, grad_side=grad_side, ff_side=ff_side)

    dm_lat = jnp.stack([dmods[l][1, 0] for l in range(depth)])
    dm_ctx = jnp.stack([dmods[l][0, 0] for l in range(depth)])
    small_pack = _pack([dm_lat, dm_ctx] + [jnp.stack([gsmall[l][k] for l in range(depth)]) for k in SMALL_NAMES] + [dfnw, loss])
    hm0, swap0 = swap_entries(gbig[0], ATTN_WEIGHTS)
    got = xchg("exchange_grads", [("gather", "all", [small_pack])] + swap0)
    small_all, recv0 = got[0], got[1:]
    small_sum = tree_sum("small_grad_sum", small_all)

    n_dm = depth * 6 * D
    dm_rows = n_dm // LANES
    dm_lat_all = small_all[:, :dm_rows].reshape(8, depth, 6 * D)
    dm_ctx_sum = small_sum[dm_rows:2 * dm_rows].reshape(depth, 6 * D)
    take = lambda a: lax.dynamic_slice_in_dim(a, chip * W_MOD_COLS, W_MOD_COLS, axis=-1)
    dmc_loc = take(dm_ctx_sum)
    cc_part = cctx_partial(jnp.pad(dmc_loc[:, None, :], ((0, 0), (0, 7), (0, 0))), w_mod)
    got = xchg("scatter_grads", [("gather", "chip", [cc_part])] + scatter_entries("l0", ATTN_WEIGHTS, hm0, recv0))
    cc_parts = got[0]
    keys = ([(0, k) for k in ATTN_WEIGHTS] + [(l, k) for l in range(1, depth) for k in BIG_NAMES] + [(0, k) for k in FF_WEIGHTS])
    pieces = dict(zip(keys, list(got[1:]) + list(early_pieces)))
    keys = [(l, k) for l in range(depth) for k in BIG_NAMES]
    reduced = {lk: tree_sum(f"chip_sum_l{lk[0]}_{lk[1]}", pieces[lk]) for lk in keys}
    g_c_ctx = cctx_grad(cc_parts, jnp.broadcast_to(c_ctx[None], (8, D)))[0]

    others = dict(zip(keys, xchg("share_halves", [("swap", "sib", [reduced[lk], reduced[lk]]) for lk in keys])))
    shard = {lk: jnp.where(ci == 0, jnp.concatenate([reduced[lk], others[lk]], axis=0),
                           jnp.concatenate([others[lk], reduced[lk]], axis=0)) for lk in keys}
    grads = {k: jnp.stack([shard[(l, k)] for l in range(depth)]) for k in BIG_NAMES}

    dm16 = jnp.concatenate([take(dm_lat_all).transpose(1, 0, 2), dmc_loc[:, None, :], jnp.zeros((depth, 7, W_MOD_COLS), F32)], axis=1)
    grads["w_mod"] = mod_weight_grad(c16, dm16)
    flat_sum = small_sum.reshape(-1)
    g_b_mod = (flat_sum[:n_dm] + flat_sum[n_dm:2 * n_dm]).reshape(depth, 6 * D)
    rest_shapes = [w[k].shape for k in PACKED[2:]]
    n_rest = sum(int(np.prod(s)) for s in rest_shapes)
    for k, g in zip(PACKED, [g_c_ctx, g_b_mod] + _unpack(flat_sum[2 * n_dm:2 * n_dm + n_rest], rest_shapes)):
        grads[k] = g
    loss = flat_sum[2 * n_dm + n_rest]

    delta, new_m, new_v = {}, {}, {}
    for k in BIG_NAMES + ("w_mod",):
        view = lambda a: a.reshape(-1, a.shape[-1])
        d_, m_, v_ = adamw(f"adamw_{k}", view(w[k]), view(grads[k]), view(m1[k]), view(m2[k]))
        delta[k], new_m[k], new_v[k] = d_.reshape(w[k].shape), m_.reshape(w[k].shape), v_.reshape(w[k].shape)
    shapes = [w[k].shape for k in PACKED]
    d_, m_, v_ = adamw("adamw_small", _pack([w[k] for k in PACKED]), _pack([grads[k] for k in PACKED]),
                       _pack([m1[k] for k in PACKED]), _pack([m2[k] for k in PACKED]))
    for k, dk, mk, vk in zip(PACKED, _unpack(d_, shapes), _unpack(m_, shapes), _unpack(v_, shapes)):
        delta[k], new_m[k], new_v[k] = dk, mk, vk
    return (loss, grad_x[None], *[grads[k] for k in WEIGHT_NAMES], *[delta[k] for k in WEIGHT_NAMES],
            *[new_m[k] for k in WEIGHT_NAMES], *[new_v[k] for k in WEIGHT_NAMES])
```

```python
import functools
import math

import numpy as np
import jax
import jax.numpy as jnp
from jax import lax
from jax.experimental import pallas as pl
from jax.experimental.pallas import tpu as pltpu

F32 = jnp.float32
BF16 = jnp.bfloat16
HI = lax.Precision.HIGHEST
EPS = 1e-6
VMEM_LIMIT_BYTES = 56 * 1024 * 1024
LANES = 128

D = 1024
D_FF = 4096
CTX = 256
GRID_W = 64
SGU_CHUNK = 128
GLA_CHUNK = 64
GLA_TAU = 16.0
GLA_DK = 32
MLA_SCALE = (128 + 64) ** -0.5
SCORE_SCALE = MLA_SCALE * math.log2(math.e)
LN2 = math.log(2.0)
ROPE_BASE = 10000.0
TM = 256
NCTXB = CTX // TM
P_GV, P_CKV, P_SU, P_SV, P_GR, P_DQ, P_GK, P_GATE, P_KR, P_GQ = 0, 256, 512, 768, 1024, 1280, 1536, 1664, 1792, 1920
P_COLS = 2048
IN_GROUPS = ((0, 128, P_GK), (128, 256, P_GV), (384, 32, P_GATE), (416, 256, P_CKV), (672, 64, P_KR),
             (736, 256, P_SU), (992, 256, P_SV), (1248, 128, P_GQ), (1376, 256, P_GR), (1632, 256, P_DQ))
ADAM_LR, ADAM_B1, ADAM_B2, ADAM_EPS, ADAM_WD, ADAM_STEP = 0.001, 0.9, 0.999, 1e-08, 0.01, 10
MESH = pl.DeviceIdType.MESH


def _params(sem):
    return pltpu.CompilerParams(dimension_semantics=sem, vmem_limit_bytes=VMEM_LIMIT_BYTES)


def _pick(n, cands):
    for c in cands:
        if n % c == 0:
            return c
    return n


class Op:
    def __init__(self, arr, blk, idx, gshape, gidx, acc):
        self.arr, self.blk, self.idx, self.gshape, self.gidx, self.acc = arr, blk, idx, gshape, gidx, acc

    def spec(self):
        return pl.BlockSpec(self.blk, self.idx)


def rows(arr, width=None, cb=0, off=0, tm=TM):
    w = arr.shape[1] if width is None else width
    n = arr.shape[0] - off * tm
    return Op(arr, (tm, w), lambda i: (i + off, cb), (n, w), lambda i: (i, 0), False)


def blank_rows(n, w, dtype, tm=TM):
    return Op(jnp.zeros((tm, w), dtype), (tm, w), lambda i: (0, 0), (n, w), lambda i: (i, 0), False)


def chunks(arr, per_tile):
    z = (0,) * (arr.ndim - 1)
    return Op(arr, (per_tile,) + arr.shape[1:], lambda i: (i,) + z, arr.shape, lambda i: (i,) + z, False)


def const(arr):
    z = (0,) * arr.ndim
    return Op(arr, arr.shape, lambda i: z, arr.shape, lambda i: z, True)


def rw(name, fn, ins, outs, grid):
    nin = len(ins)

    def body(*refs):
        vals = [r[...] for r in refs[:nin]]
        res = fn(pl.program_id(0), *vals)
        for o, r in zip(refs[nin:], res):
            o[...] = r.astype(o.dtype)

    return pl.pallas_call(
        body, name=name, grid=(grid,),
        in_specs=[o.spec() for o in ins],
        out_specs=[pl.BlockSpec(b, ix) for (_, _, b, ix) in outs],
        out_shape=[jax.ShapeDtypeStruct(s, d) for (s, d, _, _) in outs],
        compiler_params=_params(("arbitrary",)),
    )(*[o.arr for o in ins])


def rowout(n, w, dtype, tm=TM):
    return ((n, w), dtype, (tm, w), lambda i: (i, 0))


def chunkout(shape, dtype, per_tile):
    z = (0,) * (len(shape) - 1)
    return (shape, dtype, (per_tile,) + tuple(shape[1:]), lambda i: (i,) + z)


def rw_vjp(name, fn, ins, cots, wrt, grid, gdt=None, adds=None):
    nin = len(ins)
    cot_ops = [c for c in cots if c is not None]
    add_items = sorted((adds or {}).items())
    gdt = gdt or [F32] * len(wrt)
    ncot, nadd = len(cot_ops), len(add_items)

    def body(*refs):
        i = pl.program_id(0)
        vals = [r[...] for r in refs[:nin]]
        cvals = [r[...] for r in refs[nin:nin + ncot]]
        avals = [r[...] for r in refs[nin + ncot:nin + ncot + nadd]]
        grefs = refs[nin + ncot + nadd:]

        def f(*d):
            a = list(vals)
            for k, dv in zip(wrt, d):
                a[k] = dv
            return tuple(fn(i, *a))

        outs, vf = jax.vjp(f, *[vals[k] for k in wrt])
        it = iter(cvals)
        ct = tuple(jnp.zeros_like(o) if c is None else next(it).astype(o.dtype) for c, o in zip(cots, outs))
        gs = list(vf(ct))
        for (pos, _), av in zip(add_items, avals):
            gs[pos] = gs[pos].astype(F32) + av.astype(F32)
        for pos, (k, g, gref) in enumerate(zip(wrt, gs, grefs)):
            if ins[k].acc:
                @pl.when(i == 0)
                def _():
                    gref[...] = jnp.zeros_like(gref)
                gref[...] += g.astype(gref.dtype)
            else:
                gref[...] = g.astype(gref.dtype)

    all_in = list(ins) + cot_ops + [op for _, op in add_items]
    return pl.pallas_call(
        body, name=name, grid=(grid,),
        in_specs=[o.spec() for o in all_in],
        out_specs=[pl.BlockSpec(ins[k].blk, ins[k].gidx) for k in wrt],
        out_shape=[jax.ShapeDtypeStruct(ins[k].gshape, dt) for k, dt in zip(wrt, gdt)],
        compiler_params=_params(("arbitrary",)),
    )(*[o.arr for o in all_in])


MM_VMEM_BUDGET = 40 * 1024 * 1024
MM_COLS = 1024


def _square_bf16(a):
    a = a.astype(F32)
    return (a * a).astype(BF16)


def mm(name, a, b, out_dtype, pre=None, post=None, extras=(), bt=False):
    m, k = a.shape
    n = b.shape[0] if bt else b.shape[1]
    nc = min(n, MM_COLS)
    row_bytes = k * a.dtype.itemsize + n * jnp.dtype(out_dtype).itemsize + sum(n * e.dtype.itemsize for e in extras)
    tm = next(t for t in (768, 512, 384, 256, 128, 64)
              if m % t == 0 and 2 * t * row_bytes + 2 * k * n * b.dtype.itemsize + t * nc * 4 <= MM_VMEM_BUDGET)

    def body(a_ref, b_ref, *rest):
        o_ref = rest[-1]
        av = a_ref[...]
        if pre is not None:
            av = pre(av)
        for j in range(n // nc):
            cs = slice(j * nc, (j + 1) * nc)
            if bt:
                acc = lax.dot_general(av, b_ref[cs, :], (((1,), (1,)), ((), ())), preferred_element_type=F32)
            else:
                acc = lax.dot_general(av, b_ref[:, cs], (((1,), (0,)), ((), ())), preferred_element_type=F32)
            if post is not None:
                acc = post(acc, *[e[:, cs] for e in rest[:-1]])
            o_ref[:, cs] = acc.astype(o_ref.dtype)

    row = lambda w: pl.BlockSpec((tm, w), lambda i: (i, 0))
    return pl.pallas_call(
        body, name=name, grid=(m // tm,),
        in_specs=[row(k), pl.BlockSpec(b.shape, lambda i: (0, 0))] + [row(n) for _ in extras],
        out_specs=row(n),
        out_shape=jax.ShapeDtypeStruct((m, n), out_dtype),
        compiler_params=_params(("arbitrary",)),
    )(a, b, *extras)


def mm_tn(name, a, b, pre=None, side=None):
    m, ka = a.shape
    _, nb = b.shape
    tm = _pick(m, (768, 512, 256))
    ta = _pick(ka, (2048, 1024, 512, 256, 128))
    tb = _pick(nb, tuple(t for t in (4096, 2048, 1024, 512, 256, 128) if ta * t * 4 <= 8 * 1024 * 1024))
    sd = side or Side(())

    def body(a_ref, b_ref, *rest):
        o_ref = rest[sd.n_in]
        finish = sd.start(rest[:sd.n_in], rest[sd.n_in + 1:sd.n_in + 1 + sd.n_out], rest[sd.n_in + 1 + sd.n_out:])

        @pl.when(pl.program_id(2) == 0)
        def _():
            o_ref[...] = jnp.zeros_like(o_ref)
        av = a_ref[...] if pre is None else pre(a_ref[...])
        o_ref[...] += lax.dot_general(av, b_ref[...], (((0,), (0,)), ((), ())), preferred_element_type=F32)
        finish()

    res = pl.pallas_call(
        body, name=name, grid=(ka // ta, nb // tb, m // tm),
        in_specs=[pl.BlockSpec((tm, ta), lambda i, j, k: (k, i)), pl.BlockSpec((tm, tb), lambda i, j, k: (k, j))] + sd.in_specs,
        out_specs=[pl.BlockSpec((ta, tb), lambda i, j, k: (i, j))] + sd.out_specs,
        out_shape=[jax.ShapeDtypeStruct((ka, nb), F32)] + sd.shapes,
        scratch_shapes=sd.sems,
        compiler_params=_params(("arbitrary", "arbitrary", "arbitrary")),
    )(a, b, *sd.arrays)
    return res[0] if side is None else (res[0], list(res[1:]))


def _rms(x, w):
    return x * lax.rsqrt(jnp.mean(x * x, axis=-1, keepdims=True) + EPS) * w


def _mod_of(blk, m, n_rows):
    if n_rows == CTX:
        mv = jnp.where(blk == 0, m[0], m[1])
        return lambda lo, hi: mv[:, lo:hi]
    is_ctx = blk * n_rows + lax.broadcasted_iota(jnp.int32, (n_rows, 1), 0) < CTX
    return lambda lo, hi: jnp.where(is_ctx, m[0][:, lo:hi], m[1][:, lo:hi])


def _gelu(x):
    return x * (0.5 * (1.0 + jnp.tanh(math.sqrt(2.0 / math.pi) * (x + 0.044715 * (x * x * x)))))


def _sigmoid(x):
    return 1.0 / (1.0 + jnp.exp(-x))


def _log_sigmoid(z):
    return jnp.minimum(z, 0.0) - jnp.log(1.0 + jnp.exp(-jnp.abs(z)))


def _dot(a, b, dims=((1,), (0,)), precision=None):
    return lax.dot_general(a, b, (dims, ((), ())), precision=precision, preferred_element_type=F32)


def _lane_group_mask(width, group, h):
    lane = lax.broadcasted_iota(jnp.int32, (1, width), 1)
    return (lane >= h * group) & (lane < (h + 1) * group)


def fn_norm1(blk, x, m, nw):
    mv = _mod_of(blk, m, x.shape[0])
    return ((_rms(x, nw) * (1.0 + mv(D, 2 * D)) + mv(0, D)),)


def fn_res_norm2(blk, x, yo, m, nw):
    mv = _mod_of(blk, m, x.shape[0])
    x1 = x + mv(2 * D, 3 * D) * yo
    return x1, _rms(x1, nw) * (1.0 + mv(4 * D, 5 * D)) + mv(3 * D, 4 * D)


def fn_res2(blk, x1, f, m):
    mv = _mod_of(blk, m, x1.shape[0])
    return (x1 + mv(5 * D, 6 * D) * f,)


def fn_sgu(blk, su, sv, nw, nb, ws, bm):
    u = _gelu(su)
    g = _gelu(sv)
    mu = jnp.mean(g, axis=-1, keepdims=True)
    var = jnp.mean(jnp.square(g - mu), axis=-1, keepdims=True)
    v = (g - mu) * lax.rsqrt(var + EPS) * nw + nb
    out = []
    for c in range(su.shape[0] // SGU_CHUNK):
        vc = v[c * SGU_CHUNK:(c + 1) * SGU_CHUNK]
        s = bm
        for h in range(4):
            vh = jnp.where(_lane_group_mask(256, 64, h), vc, 0.0)
            s = s + _dot(ws[h].astype(BF16), vh.astype(BF16))
        out.append(u[c * SGU_CHUNK:(c + 1) * SGU_CHUNK] * s)
    return (jnp.concatenate(out, axis=0),)


def fn_gates(blk, pg, wg, bg):
    z = _dot(pg.astype(BF16), wg.astype(BF16)) + bg
    g = _log_sigmoid(z) * (1.0 / GLA_TAU)
    return g[:, :128], g[:, 128:]


def _scan_rows(x, rev):
    n = x.shape[0]
    row = lax.broadcasted_iota(jnp.int32, x.shape, 0)
    d = 1
    while d < n:
        if rev:
            x = x + jnp.where(row < n - d, pltpu.roll(x, n - d, 0), 0.0)
        else:
            x = x + jnp.where(row >= d, pltpu.roll(x, d, 0), 0.0)
        d *= 2
    return x


@functools.partial(jax.custom_vjp, nondiff_argnums=(1,))
def _cumsum_rows(x, rev):
    return _scan_rows(x, rev)


def _cumsum_rows_fwd(x, rev):
    return _scan_rows(x, rev), None


def _cumsum_rows_bwd(rev, _, dy):
    return (_scan_rows(dy, not rev),)


_cumsum_rows.defvjp(_cumsum_rows_fwd, _cumsum_rows_bwd)


def _gla_chunk_terms(g, rev):
    return _cumsum_rows(g, rev), jnp.sum(g, axis=0, keepdims=True)


def _bd_mask():
    r = lax.broadcasted_iota(jnp.int32, (128, 256), 0)
    c = lax.broadcasted_iota(jnp.int32, (128, 256), 1)
    return (r // GLA_DK) == (c // 64)


def _gla_kv_chunk(k, v, g, rev):
    b, tot = _gla_chunk_terms(g, rev)
    kd = k * jnp.exp(tot - b)
    u = jnp.where(_bd_mask(), _dot(kd.astype(BF16), v.astype(BF16), ((0,), (0,))), 0.0)
    r = lax.broadcasted_iota(jnp.int32, (128, 128), 0)
    c = lax.broadcasted_iota(jnp.int32, (128, 128), 1)
    col = jnp.sum(jnp.where(r == c, jnp.broadcast_to(jnp.exp(tot), (128, 128)), 0.0), axis=1, keepdims=True)
    return u, jnp.broadcast_to(col, (128, 128))


def _gla_o_chunk(q, k, v, g, s, rev):
    b, _ = _gla_chunk_terms(g, rev)
    qe = q * jnp.exp(b) * (GLA_DK ** -0.5)
    ke = k * jnp.exp(-b)
    o = _dot(qe.astype(BF16), jnp.where(_bd_mask(), s, 0.0).astype(BF16))
    qs = jnp.concatenate([jnp.where(_lane_group_mask(128, GLA_DK, h), qe, 0.0) for h in range(4)], axis=0)
    a = _dot(qs.astype(BF16), ke.astype(BF16), ((1,), (1,)))
    i = lax.broadcasted_iota(jnp.int32, a.shape, 0) % GLA_CHUNK
    j = lax.broadcasted_iota(jnp.int32, a.shape, 1)
    a = jnp.where((j >= i) if rev else (j <= i), a, 0.0)
    av = _dot(a.astype(BF16), v.astype(BF16))
    for h in range(4):
        o = o + jnp.where(_lane_group_mask(256, 64, h), av[GLA_CHUNK * h:GLA_CHUNK * (h + 1)], 0.0)
    return o


def fn_gla_kv(blk, k, v, gf, gb):
    uf, ef, ub, eb = [], [], [], []
    for c in range(k.shape[0] // GLA_CHUNK):
        sl = slice(c * GLA_CHUNK, (c + 1) * GLA_CHUNK)
        u, e = _gla_kv_chunk(k[sl], v[sl], gf[sl], False)
        uf.append(u[None]); ef.append(e[None])
        u, e = _gla_kv_chunk(k[sl], v[sl], gb[sl], True)
        ub.append(u[None]); eb.append(e[None])
    cat = lambda t: jnp.concatenate(t, axis=0)
    return cat(uf), cat(ef), cat(ub), cat(eb)


def fn_gla_o(blk, q, k, v, gf, gb, gr, sf, sb, nwt):
    out = []
    for c in range(q.shape[0] // GLA_CHUNK):
        sl = slice(c * GLA_CHUNK, (c + 1) * GLA_CHUNK)
        out.append(_gla_o_chunk(q[sl], k[sl], v[sl], gf[sl], sf[c], False)
                   + _gla_o_chunk(q[sl], k[sl], v[sl], gb[sl], sb[c], True))
    o = jnp.concatenate(out, axis=0)
    r = lax.broadcasted_iota(jnp.int32, (256, 256), 0)
    c = lax.broadcasted_iota(jnp.int32, (256, 256), 1)
    head_mean = jnp.where((r // 64) == (c // 64), 1.0 / 64.0, 0.0).astype(F32)
    ms = _dot(o * o, head_mean, precision=HI)
    on = o * lax.rsqrt(ms + EPS) * nwt
    return (on * (gr * _sigmoid(gr)),)


def _rope_partner(x):
    lane = lax.broadcasted_iota(jnp.int32, x.shape, 1)
    return jnp.where((lane // 16) % 2 == 0, pltpu.roll(x, LANES - 16, 1), pltpu.roll(x, 16, 1))


@jax.custom_vjp
def _rope(x, cs, sn):
    return x * cs + _rope_partner(x) * sn


def _rope_fwd(x, cs, sn):
    return _rope(x, cs, sn), (cs, sn)


def _rope_bwd(res, dy):
    cs, sn = res
    return dy * cs + _rope_partner(dy * sn), jnp.zeros_like(cs), jnp.zeros_like(sn)


_rope.defvjp(_rope_fwd, _rope_bwd)


def fn_mla_pre(blk, ckv, dq, kvw, qw):
    return _rms(ckv, kvw), _rms(dq, qw)


def fn_mla_post(blk, kk, qu, kr, cs, sn):
    kro = _rope(kr, cs, sn)
    kcat, q = [], []
    for h in range(4):
        kcat += [kk[:, 128 * h:128 * (h + 1)].astype(F32), kro]
        q += [qu[:, 256 * h:256 * h + 128], _rope(qu[:, 256 * h + 128:256 * (h + 1)], cs, sn)]
    return jnp.concatenate(kcat, axis=1), jnp.concatenate(q, axis=1) * SCORE_SCALE


ATTN_ROWS = 256
NEG = -1e30


def _scores(q, k, k0, context_queries):
    s = _dot(q, k, ((1,), (1,)))
    if context_queries is not None:
        col = k0 + lax.broadcasted_iota(jnp.int32, s.shape, 1)
        s = jnp.where(context_queries & (col >= CTX), NEG, s)
    return s


def flash_fwd(q, kcat, kvu, side=()):
    t = q.shape[0]
    tq = _pick(t, (768, 512, 256))
    tk = _pick(t, (2816, 1536, 768, 512, 256))
    nsub = tq // ATTN_ROWS
    n_side_in, _, _, side_shapes = _exchange_shapes(side)
    n_side = len(side)

    def body(q_ref, k_ref, v_ref, *rest):
        side_in, rest = rest[:n_side_in], rest[n_side_in:]
        o_ref, lse_ref = rest[:2]
        side_out, (m_sc, l_sc, acc_sc), side_sems = rest[2:2 + n_side], rest[2 + n_side:5 + n_side], rest[5 + n_side:]
        h, qi, ki = pl.program_id(0), pl.program_id(1), pl.program_id(2)
        if side:
            starts, forwards, finals = _exchange_phases(side, side_in, side_out, *side_sems)
            at_tile0 = (qi == 0) & (ki == 0)
            last = (h == pl.num_programs(0) - 1) & (qi == pl.num_programs(1) - 1) & (ki == pl.num_programs(2) - 1)
            for when, phase in (((h == 0) & at_tile0, starts), ((h == 2) & at_tile0, forwards)):
                @pl.when(when)
                def _(phase=phase):
                    for run in phase:
                        run()

        @pl.when(ki == 0)
        def _():
            m_sc[...] = jnp.full_like(m_sc, NEG)
            l_sc[...] = jnp.zeros_like(l_sc)
            acc_sc[...] = jnp.zeros_like(acc_sc)

        k, v = k_ref[...], v_ref[...]
        chains = [pl.ds(r * ATTN_ROWS, ATTN_ROWS) for r in range(nsub)]
        scores = [_scores(q_ref[rs, :], k, ki * tk, (qi == 0) if r == 0 else None) for r, rs in enumerate(chains)]
        probs = []
        for rs, s in zip(chains, scores):
            m_old = m_sc[rs, :]
            m_new = jnp.maximum(m_old, jnp.max(s, axis=-1, keepdims=True))
            alpha = jnp.exp2(m_old - m_new)
            p = jnp.exp2(s - m_new)
            l_sc[rs, :] = alpha * l_sc[rs, :] + jnp.sum(p, axis=-1, keepdims=True)
            m_sc[rs, :] = m_new
            probs.append((alpha, p.astype(BF16)))
        for rs, (alpha, p) in zip(chains, probs):
            acc_sc[rs, :] = alpha * acc_sc[rs, :] + _dot(p, v)

        @pl.when(ki == pl.num_programs(2) - 1)
        def _():
            o_ref[...] = acc_sc[...] / l_sc[...]
            lse_ref[...] = jnp.broadcast_to(m_sc[...] + jnp.log2(l_sc[...]), lse_ref.shape)

        if side:
            @pl.when(last)
            def _():
                for run in finals:
                    run()

    any_spec = pl.BlockSpec(memory_space=pl.ANY)
    res = pl.pallas_call(
        body, name="mla_flash_fwd", grid=(4, t // tq, t // tk),
        in_specs=[pl.BlockSpec((tq, 256), lambda h, i, j: (i, h)), pl.BlockSpec((tk, 256), lambda h, i, j: (j, h)),
                  pl.BlockSpec((tk, 128), lambda h, i, j: (j, 4 + h))] + [any_spec] * n_side_in,
        out_specs=[pl.BlockSpec((tq, 128), lambda h, i, j: (i, h)), pl.BlockSpec((tq, 128), lambda h, i, j: (i, h))]
        + [any_spec] * n_side,
        out_shape=[jax.ShapeDtypeStruct((t, 512), F32), jax.ShapeDtypeStruct((t, 512), F32)] + side_shapes,
        scratch_shapes=[pltpu.VMEM((tq, 1), F32), pltpu.VMEM((tq, 1), F32), pltpu.VMEM((tq, 128), F32)]
        + (_exchange_sems(side) if side else []),
        compiler_params=_params(("arbitrary", "arbitrary", "arbitrary")),
    )(q, kcat, kvu, *[a for _, _, arrs in side for a in arrs])
    return res[0], res[1], list(res[2:])


def fn_attn_stats(blk, do, o, lse):
    blocks = []
    for u in range(do.shape[0] // TM):
        rs = slice(u * TM, (u + 1) * TM)
        out = []
        for h in range(4):
            hs = slice(128 * h, 128 * (h + 1))
            d = jnp.sum(do[rs, hs] * o[rs, hs], axis=-1, keepdims=True)
            out.append(lse[rs, hs].T[0:8])
            out.append(jnp.broadcast_to(d, (TM, 128)).T[0:8])
        blocks.append(jnp.concatenate(out, axis=0)[None])
    return (jnp.concatenate(blocks, axis=0),)


def flash_bwd(q, kcat, kvu, dy, stats, side=None):
    t = q.shape[0]
    tq = _pick(t, (2816, 768, 512, 256))
    tk = _pick(t, (768, 512, 256))
    nst = tq // TM
    sd = side or Side(())

    def body(q_ref, k_ref, v_ref, do_ref, st_ref, *rest):
        dq_ref, dk_ref, dv_ref = rest[sd.n_in:sd.n_in + 3]
        finish = sd.start(rest[:sd.n_in], rest[sd.n_in + 3:sd.n_in + 3 + sd.n_out], rest[sd.n_in + 3 + sd.n_out:])
        kj, qi = pl.program_id(1), pl.program_id(2)

        @pl.when(qi == 0)
        def _():
            dk_ref[...] = jnp.zeros_like(dk_ref)
            dv_ref[...] = jnp.zeros_like(dv_ref)

        def step(has_context_queries):
            q_, k, v, do = q_ref[...], k_ref[...], v_ref[...], do_ref[...].astype(BF16)
            lse_row = jnp.concatenate([st_ref[u, 0:1, :] for u in range(nst)], axis=1)
            delta_row = jnp.concatenate([st_ref[u, 8:9, :] for u in range(nst)], axis=1)
            s = _dot(k, q_, ((1,), (1,)))
            if has_context_queries:
                key = kj * tk + lax.broadcasted_iota(jnp.int32, s.shape, 0)
                qry = lax.broadcasted_iota(jnp.int32, s.shape, 1)
                s = jnp.where((qry < CTX) & (key >= CTX), NEG, s)
            p = jnp.exp2(s - lse_row)
            dp = _dot(v, do, ((1,), (1,)))
            ds = (p * (dp - delta_row)).astype(BF16)
            dv_ref[...] += _dot(p.astype(BF16), do)
            dk_ref[...] += LN2 * _dot(ds, q_)
            dq_new = LN2 * _dot(ds, k, ((0,), (0,)))
            rows_ = pl.ds(pl.multiple_of(qi * tq, TM), tq)

            @pl.when(kj == 0)
            def _():
                dq_ref[rows_, :] = dq_new

            @pl.when(kj != 0)
            def _():
                dq_ref[rows_, :] += dq_new

        pl.when(qi == 0)(lambda: step(True))
        pl.when(qi != 0)(lambda: step(False))
        finish()

    res = pl.pallas_call(
        body, name="mla_flash_bwd", grid=(4, t // tk, t // tq),
        in_specs=[pl.BlockSpec((tq, 256), lambda h, j, i: (i, h)), pl.BlockSpec((tk, 256), lambda h, j, i: (j, h)),
                  pl.BlockSpec((tk, 128), lambda h, j, i: (j, 4 + h)), pl.BlockSpec((tq, 128), lambda h, j, i: (i, 4 + h)),
                  pl.BlockSpec((nst, 16, 256), lambda h, j, i: (i, h, 0))] + sd.in_specs,
        out_specs=[pl.BlockSpec((t, 256), lambda h, j, i: (0, h)), pl.BlockSpec((tk, 256), lambda h, j, i: (j, h)),
                   pl.BlockSpec((tk, 128), lambda h, j, i: (j, h))] + sd.out_specs,
        out_shape=[jax.ShapeDtypeStruct((t, 1024), F32), jax.ShapeDtypeStruct((t, 1024), F32),
                   jax.ShapeDtypeStruct((t, 512), F32)] + sd.shapes,
        scratch_shapes=sd.sems,
        compiler_params=_params(("arbitrary", "arbitrary", "arbitrary")),
    )(q, kcat, kvu, dy, stats, *sd.arrays)
    return res[0], res[1], res[2], list(res[3:])


SCAN_BLOCK = CTX // GLA_CHUNK


def _scan_block(t, nb, rev):
    if not rev:
        return t
    return jnp.where(t < 1, 0, nb - t)


def _scan_order(rev):
    return tuple(reversed(range(SCAN_BLOCK))) if rev else tuple(range(SCAN_BLOCK))


def _both_halves(e):
    return jnp.concatenate([e, e], axis=1)


def gla_states(uf, ef, ub, eb):
    nb = uf.shape[0] // SCAN_BLOCK

    def body(uf_ref, ef_ref, ub_ref, eb_ref, sf_ref, sb_ref, sf_sc, sb_sc):
        @pl.when(pl.program_id(0) == 0)
        def _():
            sf_sc[...] = jnp.zeros_like(sf_sc)
            sb_sc[...] = jnp.zeros_like(sb_sc)

        for u_ref, e_ref, s_ref, sc, rev in ((uf_ref, ef_ref, sf_ref, sf_sc, False), (ub_ref, eb_ref, sb_ref, sb_sc, True)):
            s = sc[...]
            for c in _scan_order(rev):
                s_ref[c] = s
                s = _both_halves(e_ref[c]) * s + u_ref[c]
            sc[...] = s

    big = lambda rev: pl.BlockSpec((SCAN_BLOCK, 128, 256), lambda t: (_scan_block(t, nb, rev), 0, 0))
    small = lambda rev: pl.BlockSpec((SCAN_BLOCK, 128, 128), lambda t: (_scan_block(t, nb, rev), 0, 0))
    return pl.pallas_call(
        body, name="gla_states", grid=(nb,),
        in_specs=[big(False), small(False), big(True), small(True)],
        out_specs=[big(False), big(True)],
        out_shape=[jax.ShapeDtypeStruct(uf.shape, F32)] * 2,
        scratch_shapes=[pltpu.VMEM((128, 256), F32)] * 2,
        compiler_params=_params(("arbitrary",)),
    )(uf, ef, ub, eb)


def gla_states_bwd(ef, eb, sf, sb, dsf, dsb):
    nb = ef.shape[0] // SCAN_BLOCK

    def body(ef_ref, eb_ref, sf_ref, sb_ref, dsf_ref, dsb_ref, duf_ref, def_ref, dub_ref, deb_ref, gf_sc, gb_sc):
        @pl.when(pl.program_id(0) == 0)
        def _():
            gf_sc[...] = jnp.zeros_like(gf_sc)
            gb_sc[...] = jnp.zeros_like(gb_sc)

        for e_ref, s_ref, ds_ref, du_ref, de_ref, g_sc, rev in ((ef_ref, sf_ref, dsf_ref, duf_ref, def_ref, gf_sc, False),
                                                                 (eb_ref, sb_ref, dsb_ref, dub_ref, deb_ref, gb_sc, True)):
            g = g_sc[...]
            for k in reversed(_scan_order(rev)):
                du_ref[k] = g
                gs = g * s_ref[k]
                de_ref[k] = gs[:, :128] + gs[:, 128:]
                g = _both_halves(e_ref[k]) * g + ds_ref[k]
            g_sc[...] = g

    big = lambda rev: pl.BlockSpec((SCAN_BLOCK, 128, 256), lambda t: (_scan_block(nb - 1 - t, nb, rev), 0, 0))
    small = lambda rev: pl.BlockSpec((SCAN_BLOCK, 128, 128), lambda t: (_scan_block(nb - 1 - t, nb, rev), 0, 0))
    return pl.pallas_call(
        body, name="gla_states_bwd", grid=(nb,),
        in_specs=[small(False), small(True), big(False), big(True), big(False), big(True)],
        out_specs=[big(False), small(False), big(True), small(True)],
        out_shape=[jax.ShapeDtypeStruct(sf.shape, F32), jax.ShapeDtypeStruct(ef.shape, F32)] * 2,
        scratch_shapes=[pltpu.VMEM((128, 256), F32)] * 2,
        compiler_params=_params(("arbitrary",)),
    )(ef, eb, sf, sb, dsf, dsb)


def loss_head(xt, target, fnw):
    t = xt.shape[0]

    def f(x, tg, w):
        y = _rms(x, w)
        return 0.5 * jnp.sum(jnp.square(y - tg)) * (1.0 / D)

    def body(x_ref, t_ref, w_ref, loss_ref, dx_ref, dw_ref):
        i = pl.program_id(0)

        @pl.when(i == 0)
        def _():
            loss_ref[...] = jnp.zeros_like(loss_ref)
            dw_ref[...] = jnp.zeros_like(dw_ref)

        @pl.when(i < NCTXB)
        def _():
            dx_ref[...] = jnp.zeros_like(dx_ref)

        @pl.when(i >= NCTXB)
        def _():
            val, (dx, dw) = jax.value_and_grad(f, argnums=(0, 2))(x_ref[...], t_ref[...], w_ref[...])
            loss_ref[...] += jnp.broadcast_to(val, loss_ref.shape)
            dx_ref[...] = dx
            dw_ref[...] += dw

    return pl.pallas_call(
        body, name="loss_head", grid=(t // TM,),
        in_specs=[pl.BlockSpec((TM, D), lambda i: (i, 0)), pl.BlockSpec((TM, D), lambda i: (jnp.maximum(i - NCTXB, 0), 0)),
                  pl.BlockSpec((1, D), lambda i: (0, 0))],
        out_specs=[pl.BlockSpec((1, 128), lambda i: (0, 0)), pl.BlockSpec((TM, D), lambda i: (i, 0)),
                   pl.BlockSpec((1, D), lambda i: (0, 0))],
        out_shape=[jax.ShapeDtypeStruct((1, 128), F32), jax.ShapeDtypeStruct((t, D), F32), jax.ShapeDtypeStruct((1, D), F32)],
        compiler_params=_params(("arbitrary",)),
    )(xt, target, fnw)


def _in_to_padded(w):
    out, pos = [], 0
    for src, wd, dst in sorted(IN_GROUPS, key=lambda g: g[2]):
        if dst > pos:
            out.append(jnp.zeros((w.shape[0], dst - pos), w.dtype))
        out.append(w[:, src:src + wd])
        pos = dst + wd
    if pos < P_COLS:
        out.append(jnp.zeros((w.shape[0], P_COLS - pos), w.dtype))
    return jnp.concatenate(out, axis=1)


def _in_from_padded(g):
    return jnp.concatenate([g[:, dst:dst + wd] for _, wd, dst in IN_GROUPS], axis=1)


def _uq_to_padded(w):
    return jnp.pad(w.reshape(256, 4, 192), ((0, 0), (0, 0), (0, 64))).reshape(256, 1024)


def _uq_from_padded(g):
    return g.reshape(256, 4, 256)[:, :, :192].reshape(256, 768)


def _ukv_to_padded(w):
    return w.reshape(256, 4, 2, 128).transpose(0, 2, 1, 3).reshape(256, 1024)


def _ukv_from_padded(g):
    return g.reshape(256, 2, 4, 128).transpose(0, 2, 1, 3).reshape(256, 1024)


def _rope_tables(n):
    freq = ROPE_BASE ** (-jnp.arange(16, dtype=F32) * 2.0 / 32.0)
    grid_h = n // GRID_W
    ar = jnp.repeat(jnp.arange(grid_h, dtype=F32)[:, None] * freq[None, :], GRID_W, axis=0)
    ac = jnp.tile(jnp.arange(GRID_W, dtype=F32)[:, None] * freq[None, :], (grid_h, 1))
    z = jnp.zeros((n, 64), F32)
    cs = jnp.concatenate([jnp.cos(ar), jnp.cos(ar), jnp.cos(ac), jnp.cos(ac), z], axis=1)
    sn = jnp.concatenate([-jnp.sin(ar), jnp.sin(ar), -jnp.sin(ac), jnp.sin(ac), z], axis=1)
    cs_c = jnp.concatenate([jnp.ones((CTX, 64), F32), jnp.zeros((CTX, 64), F32)], axis=1)
    return jnp.concatenate([cs_c, cs], axis=0), jnp.concatenate([jnp.zeros((CTX, 128), F32), sn], axis=0)


def _small_views(sp):
    wg = jnp.concatenate([jnp.pad(sp["gla_wg_fwd"], ((0, 112), (0, 0))), jnp.pad(sp["gla_wg_bwd"], ((16, 96), (0, 0)))], axis=1)
    return dict(
        n1w=sp["norm1_w"][None], n2w=sp["norm2_w"][None],
        sgu_nw=sp["sgu_norm_w"][None], sgu_nb=sp["sgu_norm_b"][None], sgu_w=sp["sgu_w"],
        sgu_bm=jnp.repeat(sp["sgu_b"].T, 64, axis=1),
        wg=wg, bg=jnp.concatenate([sp["gla_bg_fwd"], sp["gla_bg_bwd"]])[None],
        gla_nwt=jnp.tile(sp["gla_norm_w"], 4)[None],
        kvw=sp["mla_kv_norm_w"][None], qw=sp["mla_q_norm_w"][None])


def _small_grads(g):
    return dict(
        norm1_w=g["n1w"][0], norm2_w=g["n2w"][0],
        sgu_norm_w=g["sgu_nw"][0], sgu_norm_b=g["sgu_nb"][0], sgu_w=g["sgu_w"],
        sgu_b=g["sgu_bm"].reshape(128, 4, 64).sum(-1).T,
        gla_wg_fwd=g["wg"][0:16, 0:128], gla_wg_bwd=g["wg"][16:32, 128:256],
        gla_bg_fwd=g["bg"][0, 0:128], gla_bg_bwd=g["bg"][0, 128:256],
        gla_norm_w=g["gla_nwt"].reshape(4, 64).sum(0),
        mla_kv_norm_w=g["kvw"][0], mla_q_norm_w=g["qw"][0])


def _big_views(full):
    views = {}
    if "w_in" in full:
        views.update(win=_in_to_padded(full["w_in"]), wuq=_uq_to_padded(full["mla_w_uq"]),
                     wukv=_ukv_to_padded(full["mla_w_ukv"]), wout=full["w_out"])
    if "w_ff1" in full:
        views.update(w1=full["w_ff1"], w2=full["w_ff2"])
    return views


def _gla_tile(t):
    return _pick(t, (768, 512, 256))


def _layer_ops(p, sv, a):
    gt = _gla_tile(p.shape[0])
    pc = lambda off, w: rows(p, w, off // w, tm=gt)
    gr = lambda arr, w=None, cb=0: rows(arr, w, cb, tm=gt)
    return dict(
        sgu=[pc(P_SU, 256), pc(P_SV, 256), const(sv["sgu_nw"]), const(sv["sgu_nb"]), const(sv["sgu_w"]), const(sv["sgu_bm"])],
        gates=[pc(P_GATE, 128), const(sv["wg"]), const(sv["bg"])],
        mla_pre=[pc(P_CKV, 256), pc(P_DQ, 256), const(sv["kvw"]), const(sv["qw"])],
        gla_kv=lambda: [pc(P_GK, 128), pc(P_GV, 256), gr(a["gf"]), gr(a["gb"])],
        gla_o=lambda: [pc(P_GQ, 128), pc(P_GK, 128), pc(P_GV, 256), gr(a["gf"]), gr(a["gb"]), pc(P_GR, 256),
                       chunks(a["sf"], gt // GLA_CHUNK), chunks(a["sb"], gt // GLA_CHUNK), const(sv["gla_nwt"])],
        mla_post=lambda: [gr(a["kvu"], 512, 0), gr(a["qu"]), pc(P_KR, 128), gr(a["cs"]), gr(a["sn"])])


def layer_fwd(l, xt, modl, bw, sv, tabs, side=(), late=None):
    t = xt.shape[0]
    g, nc, gt = t // TM, t // GLA_CHUNK, _gla_tile(t)
    gg, cpt = t // gt, gt // GLA_CHUNK
    nm = lambda s: f"l{l}_{s}"
    a = dict(x=xt, cs=tabs[0], sn=tabs[1])
    a["h"], = rw(nm("norm1"), fn_norm1, [rows(xt, tm=gt), const(modl), const(sv["n1w"])], [rowout(t, D, BF16, tm=gt)], gg)
    p = a["p"] = mm(nm("in_proj"), a["h"], bw["win"], F32)
    ops = _layer_ops(p, sv, a)
    y_sgu, = rw(nm("sgu"), fn_sgu, ops["sgu"], [rowout(t, 256, BF16, tm=gt)], gg)
    a["gf"], a["gb"] = rw(nm("gates"), fn_gates, ops["gates"], [rowout(t, 128, F32, tm=gt)] * 2, gg)
    a["uf"], a["ef"], a["ub"], a["eb"] = rw(nm("gla_kv"), fn_gla_kv, ops["gla_kv"](),
                                           [chunkout((nc, 128, 256), F32, cpt), chunkout((nc, 128, 128), F32, cpt)] * 2, gg)
    a["sf"], a["sb"] = gla_states(a["uf"], a["ef"], a["ub"], a["eb"])
    y_gla, = rw(nm("gla_o"), fn_gla_o, ops["gla_o"](), [rowout(t, 256, BF16, tm=gt)], gg)
    a["ckvn"], a["dqn"] = rw(nm("mla_pre"), fn_mla_pre, ops["mla_pre"], [rowout(t, 256, BF16, tm=gt)] * 2, gg)
    a["kvu"] = mm(nm("kv_up"), a["ckvn"], bw["wukv"], BF16)
    a["qu"] = mm(nm("q_up"), a["dqn"], bw["wuq"], F32)
    a["kcat"], a["q"] = rw(nm("mla_post"), fn_mla_post, ops["mla_post"](), [rowout(t, 1024, BF16, tm=gt)] * 2, gg)
    a["o"], a["lse"], side_out = flash_fwd(a["q"], a["kcat"], a["kvu"], side)
    if late is not None:
        late(side_out)
    a["y"] = jnp.concatenate([y_sgu, y_gla, a["o"].astype(BF16)], axis=1)
    a["yo"] = mm(nm("out_proj"), a["y"], bw["wout"], F32)
    a["x1"], a["h2"] = rw(nm("res_norm2"), fn_res_norm2,
                          [rows(xt, tm=gt), rows(a["yo"], tm=gt), const(modl), const(sv["n2w"])],
                          [rowout(t, D, F32, tm=gt), rowout(t, D, BF16, tm=gt)], gg)
    a["act"] = mm(nm("ff1"), a["h2"], bw["w1"], BF16, post=lambda acc: jnp.maximum(acc, 0.0))
    a["f"] = mm(nm("ff2"), a["act"], bw["w2"], F32, pre=_square_bf16)
    x2, = rw(nm("res2"), fn_res2, [rows(a["x1"], tm=gt), rows(a["f"], tm=gt), const(modl)], [rowout(t, D, F32, tm=gt)], gg)
    return x2, a


def fn_assemble(blk, gv1, gv2, ckv, su, sv_, gr, dq, gk1, gk2, pg, kr, gq):
    return (jnp.concatenate([gv1 + gv2, ckv, su, sv_, gr, dq, gk1 + gk2, pg, kr, gq], axis=1),)


def layer_bwd(l, dx2, a, modl, bw, sv, side_a=None, make_side_b=None, ff_side=None):
    t = dx2.shape[0]
    g, gt = t // TM, _gla_tile(t)
    gg, cpt = t // gt, gt // GLA_CHUNK
    ht = TM
    nm = lambda s: f"l{l}_{s}_bwd"
    p = a["p"]
    ops = _layer_ops(p, sv, a)
    gw, gs = {}, {}
    df, dm_a = rw_vjp(nm("res2"), fn_res2, [rows(a["x1"], tm=gt), rows(a["f"], tm=gt), const(modl)], [rows(dx2, tm=gt)],
                      [1, 2], gg, gdt=[BF16, F32])
    during_attention = []
    if side_a:
        gw["w2"], got_a = mm_tn(nm("ff2_w"), a["act"], df, pre=_square_bf16, side=Side(side_a))
        during_attention += make_side_b(got_a)
    else:
        gw["w2"] = mm_tn(nm("ff2_w"), a["act"], df, pre=_square_bf16)
    du = mm(nm("ff2_x"), df, bw["w2"], BF16, post=lambda acc, act: acc * (2.0 * act.astype(F32)), extras=(a["act"],), bt=True)
    gw["w1"] = mm_tn(nm("ff1_w"), a["h2"], du)
    dh2 = mm(nm("ff1_x"), du, bw["w1"], F32, bt=True)
    dxa, dyo, dm_b, gs["n2w"] = rw_vjp(nm("res_norm2"), fn_res_norm2,
                                       [rows(a["x"], tm=ht), rows(a["yo"], tm=ht), const(modl), const(sv["n2w"])],
                                       [rows(dx2, tm=ht), rows(dh2, tm=ht)], [0, 1, 2, 3], t // ht, gdt=[F32, BF16, F32, F32])
    if ff_side is not None:
        ff_entries, make_ff_next = ff_side(dict(w_ff1=gw["w1"], w_ff2=gw["w2"]))
        gw["wout"], got_ff = mm_tn(nm("out_w"), a["y"], dyo, side=Side(ff_entries))
        during_attention += make_ff_next(got_ff)
    else:
        gw["wout"] = mm_tn(nm("out_w"), a["y"], dyo)
    dy = mm(nm("out_x"), dyo, bw["wout"], F32, bt=True)
    dsu, dsv, gs["sgu_nw"], gs["sgu_nb"], gs["sgu_w"], gs["sgu_bm"] = rw_vjp(
        nm("sgu"), fn_sgu, ops["sgu"], [rows(dy, 256, 0, tm=gt)], [0, 1, 2, 3, 4, 5], gg)
    dgq, dgk1, dgv1, dgf1, dgb1, dgr, dsf, dsb, gs["gla_nwt"] = rw_vjp(
        nm("gla_o"), fn_gla_o, ops["gla_o"](), [rows(dy, 256, 1, tm=gt)], list(range(9)), gg)
    duf, def_, dub, deb = gla_states_bwd(a["ef"], a["eb"], a["sf"], a["sb"], dsf, dsb)
    dgk2, dgv2, dgf, dgb = rw_vjp(nm("gla_kv"), fn_gla_kv, ops["gla_kv"](),
                                  [chunks(duf, cpt), chunks(def_, cpt), chunks(dub, cpt), chunks(deb, cpt)], [0, 1, 2, 3], gg,
                                  adds={2: rows(dgf1, tm=gt), 3: rows(dgb1, tm=gt)})
    dpg, gs["wg"], gs["bg"] = rw_vjp(nm("gates"), fn_gates, ops["gates"], [rows(dgf, tm=gt), rows(dgb, tm=gt)], [0, 1, 2], gg)
    stats, = rw(nm("attn_stats"), fn_attn_stats, [rows(dy, 512, 1, tm=gt), rows(a["o"], tm=gt), rows(a["lse"], tm=gt)],
                [chunkout((g, 64, TM), F32, gt // TM)], gg)
    dq, dkcat, dv, got_b = flash_bwd(a["q"], a["kcat"], a["kvu"], dy, stats,
                                     side=Side(during_attention) if during_attention else None)
    post_ins = [blank_rows(t, 512, BF16, gt), blank_rows(t, 1024, F32, gt), blank_rows(t, 128, F32, gt)] + ops["mla_post"]()[3:]
    dkk, dqu, dkr = rw_vjp(nm("mla_post"), fn_mla_post, post_ins, [rows(dkcat, tm=gt), rows(dq, tm=gt)], [0, 1, 2],
                           gg, gdt=[BF16, BF16, F32])
    dkvu = jnp.concatenate([dkk, dv.astype(BF16)], axis=1)
    gw["wukv"] = mm_tn(nm("kv_up_w"), a["ckvn"], dkvu)
    gw["wuq"] = mm_tn(nm("q_up_w"), a["dqn"], dqu)
    dckvn = mm(nm("kv_up_x"), dkvu, bw["wukv"], F32, bt=True)
    ddqn = mm(nm("q_up_x"), dqu, bw["wuq"], F32, bt=True)
    dckv, ddq, gs["kvw"], gs["qw"] = rw_vjp(nm("mla_pre"), fn_mla_pre, ops["mla_pre"],
                                            [rows(dckvn, tm=gt), rows(ddqn, tm=gt)], [0, 1, 2, 3], gg)
    dp, = rw(nm("assemble"), fn_assemble,
             [rows(x_, tm=gt) for x_ in (dgv1, dgv2, dckv, dsu, dsv, dgr, ddq, dgk1, dgk2, dpg, dkr, dgq)],
             [rowout(t, P_COLS, BF16, tm=gt)], gg)
    gw["win"] = mm_tn(nm("in_w"), a["h"], dp)
    dh = mm(nm("in_x"), dp, bw["win"], F32, bt=True)
    dx, dm_c, gs["n1w"] = rw_vjp(nm("norm1"), fn_norm1, [rows(a["x"], tm=gt), const(modl), const(sv["n1w"])],
                                 [rows(dh, tm=gt)], [0, 1, 2], gg, adds={0: rows(dxa, tm=gt)})
    big = dict(w_in=_in_from_padded(gw["win"]), w_out=gw["wout"], mla_w_uq=_uq_from_padded(gw["wuq"]),
               mla_w_ukv=_ukv_from_padded(gw["wukv"]), w_ff1=gw["w1"], w_ff2=gw["w2"])
    return dx, dm_a + dm_b + dm_c, big, _small_grads(gs), got_b


SMALL_NAMES = ("norm1_w", "sgu_norm_w", "sgu_norm_b", "sgu_w", "sgu_b", "gla_wg_fwd", "gla_bg_fwd", "gla_wg_bwd",
               "gla_bg_bwd", "gla_norm_w", "mla_q_norm_w", "mla_kv_norm_w", "norm2_w")
BIG_NAMES = ("w_in", "w_out", "mla_w_uq", "mla_w_ukv", "w_ff1", "w_ff2")
ATTN_WEIGHTS, FF_WEIGHTS = BIG_NAMES[:4], BIG_NAMES[4:]


def local_step(x, ctx, target, mods, big, small, final_norm_w, side=(), on_side=None, grad_side=None, ff_side=None):
    n = x.shape[0]
    xt = jnp.concatenate([ctx, x], axis=0)
    tabs = _rope_tables(n)
    depth = len(mods)
    big = list(big)
    svs = [_small_views(small[l]) for l in range(depth)]
    acts, bws = [], []
    for l in range(depth):
        bws.append(_big_views(big[l]))
        late = None
        if l == 0 and on_side is not None:
            def late(results):
                rest0, later = on_side(results)
                bws[0].update(_big_views(rest0))
                big.extend(later)
        xt, a = layer_fwd(l, xt, mods[l], bws[l], svs[l], tabs, side if l == 0 else (), late)
        acts.append(a)
    loss, dxt, dfnw = loss_head(xt, target, final_norm_w[None])
    dmods, gbig, gsmall = [None] * depth, [None] * depth, [None] * depth
    got = []
    for l in reversed(range(depth)):
        hooks = (None, None, None)
        if l == 0 and grad_side is not None and depth > 1:
            hooks = (*grad_side(gbig[1:]), ff_side)
        dxt, dmods[l], gbig[l], gsmall[l], got = layer_bwd(l, dxt, acts[l], mods[l], bws[l], svs[l], *hooks)
    return loss, dxt[CTX:], dmods, gbig, gsmall, dfnw, got


def _group(group):
    x, y, c = lax.axis_index("x"), lax.axis_index("y"), lax.axis_index("c")
    if group == "sib":
        return 2, c, [((x, y, 1 - c), 1 - c)]
    if group == "chip":
        flips = [(1, 0), (0, 1), (1, 1)]
        return 4, 2 * x + y, [((x ^ fx, y ^ fy, c), 2 * (x ^ fx) + (y ^ fy)) for fx, fy in flips]
    flips = [(fx, fy, fc) for fx in (0, 1) for fy in (0, 1) for fc in (0, 1)][1:]
    return 8, 4 * x + 2 * y + c, [((x ^ fx, y ^ fy, c ^ fc), 4 * (x ^ fx) + 2 * (y ^ fy) + (c ^ fc)) for fx, fy, fc in flips]


def _group_size(group):
    return {"sib": 2, "chip": 4, "all": 8}[group]


REMOTE_COPIES = {"gather": None, "scatter": None, "swap": 1, "gather2": 6}


def _exchange_shapes(entries):
    n_in = sum(len(arrs) for _, _, arrs in entries)
    n_remote = sum(REMOTE_COPIES[k] or _group_size(g) - 1 for k, g, _ in entries)
    n_local = sum(1 for k, _, _ in entries if k != "swap")
    out_shape = []
    for kind, group, arrs in entries:
        a = arrs[0]
        if kind in ("gather", "gather2"):
            out_shape.append(jax.ShapeDtypeStruct((_group_size(group),) + a.shape, a.dtype))
        elif kind == "swap" and len(arrs) == 1:
            out_shape.append(jax.ShapeDtypeStruct(a.shape[1:], a.dtype))
        else:
            out_shape.append(jax.ShapeDtypeStruct(a.shape, a.dtype))
    return n_in, n_remote, n_local, out_shape


def _exchange_sems(entries):
    _, n_remote, n_local, _ = _exchange_shapes(entries)
    return [pltpu.SemaphoreType.DMA((n_remote,)), pltpu.SemaphoreType.DMA((n_remote,)), pltpu.SemaphoreType.DMA((max(n_local, 1),))]


def _exchange_phases(entries, in_refs, out_refs, send_sems, recv_sems, local_sems):
    x, y, c = lax.axis_index("x"), lax.axis_index("y"), lax.axis_index("c")

    def remote(src, dst, k, dev):
        return pltpu.make_async_remote_copy(src_ref=src, dst_ref=dst, send_sem=send_sems.at[k], recv_sem=recv_sems.at[k],
                                            device_id=dev, device_id_type=MESH)

    pos, k, kl = 0, 0, 0
    starts, forwards, finals = [], [], []
    for (kind, group, arrs), out in zip(entries, out_refs):
        srcs = in_refs[pos:pos + len(arrs)]
        pos += len(arrs)
        _, mine, peers = _group(group)
        if kind == "swap":
            (dev, _), = peers
            if len(srcs) == 1:
                starts.append(remote(srcs[0].at[1 - c], out, k, dev).start)
                finals.append(remote(srcs[0].at[0], out, k, dev).wait)
            else:
                def start_swap(srcs=srcs, k=k, dev=dev, out=out):
                    for core, src in ((0, srcs[1]), (1, srcs[0])):
                        @pl.when(c == core)
                        def _(src=src):
                            remote(src, out, k, dev).start()

                starts.append(start_swap)
                finals.append(remote(srcs[0], out, k, dev).wait)
            k += 1
            continue
        src = srcs[0]
        own = pltpu.make_async_copy(src if kind != "scatter" else src.at[mine], out.at[mine], local_sems.at[kl])
        starts.append(own.start)
        finals.append(own.wait)
        kl += 1
        if kind == "gather2":
            sibling = (x, y, 1 - c)
            for f, (dev, slot) in enumerate(peers):
                starts.append(remote(src.at[c], out.at[mine, c], k + f, dev).start)
                arrival = remote(src.at[c], out.at[slot, c], k + f, dev)

                def forward(arrival=arrival, slot=slot, kf=k + 3 + f, out=out):
                    arrival.wait_recv()
                    remote(out.at[slot, c], out.at[slot, c], kf, sibling).start()

                forwards.append(forward)
                finals.append(arrival.wait_send)
                finals.append(remote(out.at[slot, c], out.at[slot, 1 - c], k + 3 + f, sibling).wait)
            k += 6
            continue
        for dev, slot in peers:
            piece = src if kind == "gather" else src.at[slot]
            starts.append(remote(piece, out.at[mine], k, dev).start)
            finals.append(remote(piece, out.at[slot], k, dev).wait)
            k += 1
    return starts, forwards, finals


class Side:
    def __init__(self, entries):
        self.entries = tuple(entries)
        self.n_in, _, _, self.shapes = _exchange_shapes(self.entries)
        self.n_out = len(self.entries)
        self.arrays = [a for _, _, arrs in self.entries for a in arrs]
        any_spec = pl.BlockSpec(memory_space=pl.ANY)
        self.in_specs, self.out_specs = [any_spec] * self.n_in, [any_spec] * self.n_out
        self.sems = _exchange_sems(self.entries) if self.entries else []

    def start(self, in_refs, out_refs, sem_refs):
        if not self.entries:
            return lambda: None
        ids = [pl.program_id(d) for d in range(3)]
        first = (ids[0] == 0) & (ids[1] == 0) & (ids[2] == 0)
        last = ((ids[0] == pl.num_programs(0) - 1) & (ids[1] == pl.num_programs(1) - 1) & (ids[2] == pl.num_programs(2) - 1))
        starts, forwards, finals = _exchange_phases(self.entries, in_refs, out_refs, *sem_refs)
        assert not forwards

        @pl.when(first)
        def _():
            for run in starts:
                run()

        def finish():
            @pl.when(last)
            def _():
                for run in finals:
                    run()

        return finish


def xchg(name, entries):
    n_in, _, _, out_shape = _exchange_shapes(entries)

    def body(*refs):
        in_refs, out_refs = refs[:n_in], refs[n_in:n_in + len(entries)]
        for phase in _exchange_phases(entries, in_refs, out_refs, *refs[n_in + len(entries):]):
            for run in phase:
                run()

    any_spec = pl.BlockSpec(memory_space=pl.ANY)
    return pl.pallas_call(
        body, name=name,
        in_specs=[any_spec] * n_in, out_specs=[any_spec] * len(entries), out_shape=out_shape,
        scratch_shapes=_exchange_sems(entries),
    )(*[a for _, _, arrs in entries for a in arrs])


def _block_rows(r, c, budget=131072):
    tr = 8
    while tr * 2 * c <= budget and r % (tr * 2) == 0:
        tr *= 2
    return tr if r % tr == 0 else r


def tree_sum(name, parts):
    g, r, c = parts.shape
    tr = _block_rows(r, c)

    def body(p_ref, o_ref):
        p = [p_ref[i].astype(F32) for i in range(g)]
        while len(p) > 1:
            p = [p[i] + p[i + 1] for i in range(0, len(p), 2)]
        o_ref[...] = p[0]

    return pl.pallas_call(
        body, name=name, grid=(r // tr,),
        in_specs=[pl.BlockSpec((g, tr, c), lambda i: (0, i, 0))], out_specs=pl.BlockSpec((tr, c), lambda i: (i, 0)),
        out_shape=jax.ShapeDtypeStruct((r, c), F32), compiler_params=_params(("arbitrary",)),
    )(parts)


def pair_sum(name, halves, recv, core):
    r, c = recv.shape
    tr = _block_rows(r, c)

    def body(h_ref, r_ref, k_ref, o_ref):
        o_ref[...] = (jnp.where(k_ref[...] > 0.5, h_ref[1], h_ref[0]) + r_ref[...]).astype(o_ref.dtype)

    blk = pl.BlockSpec((tr, c), lambda i: (i, 0))
    return pl.pallas_call(
        body, name=name, grid=(r // tr,),
        in_specs=[pl.BlockSpec((2, tr, c), lambda i: (0, i, 0)), blk, pl.BlockSpec((1, 1), lambda i: (0, 0))],
        out_specs=blk, out_shape=jax.ShapeDtypeStruct((r, c), BF16), compiler_params=_params(("arbitrary",)),
    )(halves, recv, core)


def adamw(name, w, g, m, v):
    r, c = w.shape
    tr = _block_rows(r, c)

    def body(w_ref, g_ref, m_ref, v_ref, d_ref, nm_ref, nv_ref):
        gg = g_ref[...]
        nm = ADAM_B1 * m_ref[...] + (1.0 - ADAM_B1) * gg
        nv = ADAM_B2 * v_ref[...] + (1.0 - ADAM_B2) * jnp.square(gg)
        m_hat = nm / (1.0 - ADAM_B1 ** ADAM_STEP)
        v_hat = nv / (1.0 - ADAM_B2 ** ADAM_STEP)
        d_ref[...] = -ADAM_LR * (m_hat / (jnp.sqrt(v_hat) + ADAM_EPS) + ADAM_WD * w_ref[...])
        nm_ref[...] = nm
        nv_ref[...] = nv

    blk = pl.BlockSpec((tr, c), lambda i: (i, 0))
    return pl.pallas_call(
        body, name=name, grid=(r // tr,), in_specs=[blk] * 4, out_specs=[blk] * 3,
        out_shape=[jax.ShapeDtypeStruct((r, c), F32)] * 3, compiler_params=_params(("arbitrary",)),
    )(w, g, m, v)


W_MOD_COLS = 6 * D // 4
MOD_TN = 512


def mod_project(c16, w_mod, b_loc):
    def body(c_ref, w_ref, b_ref, o_ref):
        cv = c_ref[...]
        s = (cv * _sigmoid(cv)).astype(BF16)
        o_ref[0] = _dot(s, w_ref[0].astype(BF16)) + b_ref[0]

    return pl.pallas_call(
        body, name="mod_project", grid=(2, W_MOD_COLS // MOD_TN),
        in_specs=[pl.BlockSpec((16, D), lambda l, j: (0, 0)), pl.BlockSpec((1, D, MOD_TN), lambda l, j: (l, 0, j)),
                  pl.BlockSpec((1, 1, MOD_TN), lambda l, j: (l, 0, j))],
        out_specs=pl.BlockSpec((1, 16, MOD_TN), lambda l, j: (l, 0, j)),
        out_shape=jax.ShapeDtypeStruct((2, 16, W_MOD_COLS), F32), compiler_params=_params(("arbitrary", "arbitrary")),
    )(c16, w_mod, b_loc)


def mod_weight_grad(c16, dm16):
    def body(c_ref, d_ref, o_ref):
        cv = c_ref[...]
        o_ref[0] = _dot(cv * _sigmoid(cv), d_ref[0], ((0,), (0,)), precision=HI)

    return pl.pallas_call(
        body, name="mod_weight_grad", grid=(2, W_MOD_COLS // MOD_TN),
        in_specs=[pl.BlockSpec((16, D), lambda l, j: (0, 0)), pl.BlockSpec((1, 16, MOD_TN), lambda l, j: (l, 0, j))],
        out_specs=pl.BlockSpec((1, D, MOD_TN), lambda l, j: (l, 0, j)),
        out_shape=jax.ShapeDtypeStruct((2, D, W_MOD_COLS), F32), compiler_params=_params(("arbitrary", "arbitrary")),
    )(c16, dm16)


def cctx_partial(dmc, w_mod):
    def body(d_ref, w_ref, o_ref):
        @pl.when(pl.program_id(0) == 0)
        def _():
            o_ref[...] = jnp.zeros_like(o_ref)
        o_ref[...] += _dot(d_ref[0], w_ref[0], ((1,), (1,)), precision=HI)

    return pl.pallas_call(
        body, name="cctx_partial", grid=(2,),
        in_specs=[pl.BlockSpec((1, 8, W_MOD_COLS), lambda l: (l, 0, 0)), pl.BlockSpec((1, D, W_MOD_COLS), lambda l: (l, 0, 0))],
        out_specs=pl.BlockSpec((8, D), lambda l: (0, 0)),
        out_shape=jax.ShapeDtypeStruct((8, D), F32), compiler_params=_params(("arbitrary",)),
    )(dmc, w_mod)


def cctx_grad(parts, c_ctx8):
    def body(p_ref, c_ref, o_ref):
        ds = (p_ref[0] + p_ref[1]) + (p_ref[2] + p_ref[3])
        _, vf = jax.vjp(lambda z: z * _sigmoid(z), c_ref[...])
        o_ref[...] = vf(ds)[0]

    return pl.pallas_call(
        body, name="cctx_grad", out_shape=jax.ShapeDtypeStruct((8, D), F32),
    )(parts, c_ctx8)


ARG_NAMES = ("x", "c", "ctx", "c_ctx", "w_mod", "b_mod", "norm1_w", "w_in", "w_out", "sgu_norm_w", "sgu_norm_b", "sgu_w",
             "sgu_b", "gla_wg_fwd", "gla_bg_fwd", "gla_wg_bwd", "gla_bg_bwd", "gla_norm_w", "mla_q_norm_w", "mla_w_uq",
             "mla_kv_norm_w", "mla_w_ukv", "norm2_w", "w_ff1", "w_ff2", "final_norm_w")
WEIGHT_NAMES = ARG_NAMES[3:]
PACKED = ("c_ctx", "b_mod") + SMALL_NAMES + ("final_norm_w",)
ROW_SHARDED = ("w_out", "w_ff2")
PACK_ROWS = 256


def _pack(vectors):
    flat = jnp.concatenate([v.reshape(-1) for v in vectors])
    n = flat.shape[0]
    total = -(-n // (PACK_ROWS * LANES)) * PACK_ROWS * LANES
    return jnp.pad(flat, (0, total - n)).reshape(-1, LANES)


def _unpack(buf, shapes):
    flat, out, pos = buf.reshape(-1), [], 0
    for shp in shapes:
        n = int(np.prod(shp))
        out.append(flat[pos:pos + n].reshape(shp))
        pos += n
    return out


def _full_weight(name, g):
    if name in ROW_SHARDED:
        return g.reshape(-1, g.shape[-1])
    return g.transpose(1, 0, 2).reshape(g.shape[1], -1)


def _chip_chunks(name, a):
    if name in ROW_SHARDED:
        return a.reshape(4, a.shape[0] // 4, a.shape[1])
    return a.reshape(a.shape[0], 4, a.shape[1] // 4).transpose(1, 0, 2)


def kernel(x, c, ctx, c_ctx, w_mod, b_mod, norm1_w, w_in, w_out, sgu_norm_w, sgu_norm_b, sgu_w, sgu_b, gla_wg_fwd, gla_bg_fwd, gla_wg_bwd, gla_bg_bwd, gla_norm_w, mla_q_norm_w, mla_w_uq, mla_kv_norm_w, mla_w_ukv, norm2_w, w_ff1, w_ff2, final_norm_w, loss_target, m_c_ctx, m_w_mod, m_b_mod, m_norm1_w, m_w_in, m_w_out, m_sgu_norm_w, m_sgu_norm_b, m_sgu_w, m_sgu_b, m_gla_wg_fwd, m_gla_bg_fwd, m_gla_wg_bwd, m_gla_bg_bwd, m_gla_norm_w, m_mla_q_norm_w, m_mla_w_uq, m_mla_kv_norm_w, m_mla_w_ukv, m_norm2_w, m_w_ff1, m_w_ff2, m_final_norm_w, v_c_ctx, v_w_mod, v_b_mod, v_norm1_w, v_w_in, v_w_out, v_sgu_norm_w, v_sgu_norm_b, v_sgu_w, v_sgu_b, v_gla_wg_fwd, v_gla_bg_fwd, v_gla_wg_bwd, v_gla_bg_bwd, v_gla_norm_w, v_mla_q_norm_w, v_mla_w_uq, v_mla_kv_norm_w, v_mla_w_ukv, v_norm2_w, v_w_ff1, v_w_ff2, v_final_norm_w):
    args = (x, c, ctx, c_ctx, w_mod, b_mod, norm1_w, w_in, w_out, sgu_norm_w, sgu_norm_b, sgu_w, sgu_b, gla_wg_fwd, gla_bg_fwd, gla_wg_bwd, gla_bg_bwd, gla_norm_w, mla_q_norm_w, mla_w_uq, mla_kv_norm_w, mla_w_ukv, norm2_w, w_ff1, w_ff2, final_norm_w)
    w = dict(zip(ARG_NAMES, args))
    moms = (m_c_ctx, m_w_mod, m_b_mod, m_norm1_w, m_w_in, m_w_out, m_sgu_norm_w, m_sgu_norm_b, m_sgu_w, m_sgu_b, m_gla_wg_fwd, m_gla_bg_fwd, m_gla_wg_bwd, m_gla_bg_bwd, m_gla_norm_w, m_mla_q_norm_w, m_mla_w_uq, m_mla_kv_norm_w, m_mla_w_ukv, m_norm2_w, m_w_ff1, m_w_ff2, m_final_norm_w)
    vars_ = (v_c_ctx, v_w_mod, v_b_mod, v_norm1_w, v_w_in, v_w_out, v_sgu_norm_w, v_sgu_norm_b, v_sgu_w, v_sgu_b, v_gla_wg_fwd, v_gla_bg_fwd, v_gla_wg_bwd, v_gla_bg_bwd, v_gla_norm_w, v_mla_q_norm_w, v_mla_w_uq, v_mla_kv_norm_w, v_mla_w_ukv, v_norm2_w, v_w_ff1, v_w_ff2, v_final_norm_w)
    m1 = dict(zip(WEIGHT_NAMES, moms))
    m2 = dict(zip(WEIGHT_NAMES, vars_))
    xi, yi, ci = lax.axis_index("x"), lax.axis_index("y"), lax.axis_index("c")
    chip, dev = 2 * xi + yi, 4 * xi + 2 * yi + ci
    depth = w_mod.shape[0]

    def shard_halves(l, names):
        return [("gather2", "chip", [w[k][l].astype(BF16).reshape(2, w[k].shape[1] // 2, w[k].shape[2])]) for k in names]

    def full_weights(names, gathered):
        return {k: _full_weight(k, g.reshape(4, *w[k].shape[1:])) for k, g in zip(names, gathered)}

    got = xchg("gather_inputs", [("gather", "all", [c])] + shard_halves(0, ATTN_WEIGHTS))
    c_all = got[0]
    c16 = jnp.concatenate([c_all.reshape(8, D), c_ctx[None], jnp.zeros((7, D), F32)], axis=0)
    b_loc = lax.dynamic_slice_in_dim(b_mod, chip * W_MOD_COLS, W_MOD_COLS, axis=1)[:, None, :]
    mod_part = mod_project(c16, w_mod, b_loc)
    mod_all, = xchg("gather_mod", [("gather", "chip", [mod_part])])
    mod_full = mod_all.transpose(1, 2, 0, 3).reshape(depth, 16, 6 * D)
    mods = [jnp.stack([mod_full[l, 8], lax.dynamic_index_in_dim(mod_full[l], dev, 0, keepdims=False)])[:, None, :]
            for l in range(depth)]

    small = [{k: w[k][l] for k in SMALL_NAMES} for l in range(depth)]
    n_big = len(BIG_NAMES)
    n_ff = len(FF_WEIGHTS)
    later = shard_halves(0, FF_WEIGHTS) + [e for l in range(1, depth) for e in shard_halves(l, BIG_NAMES)]
    core = ci.astype(F32).reshape(1, 1)

    def on_side(res):
        return (full_weights(FF_WEIGHTS, res[:n_ff]),
                [full_weights(BIG_NAMES, res[n_ff + i * n_big:n_ff + (i + 1) * n_big]) for i in range(depth - 1)])

    def half_major(k, g):
        ch = _chip_chunks(k, g)
        return ch.reshape(4, 2, ch.shape[1] // 2, ch.shape[2]).transpose(1, 0, 2, 3)

    def swap_entries(gb, names):
        hm = [half_major(k, gb[k]) for k in names]
        return hm, [("swap", "sib", [h]) for h in hm]

    def scatter_entries(tag, names, hm, recv):
        out = []
        for k, h, r in zip(names, hm, recv):
            s2 = pair_sum(f"pair_sum_{tag}_{k}", h.reshape(2, -1, h.shape[-1]), r.reshape(-1, r.shape[-1]), core)
            out.append(("scatter", "chip", [s2.reshape(r.shape)]))
        return out

    def grad_side(gb_later):
        hms, entries = [], []
        for gb in gb_later:
            hm, e = swap_entries(gb, BIG_NAMES)
            hms.append(hm)
            entries += e

        def make_scatter(recv):
            return [e for i, hm in enumerate(hms)
                    for e in scatter_entries(f"l{i + 1}", BIG_NAMES, hm, recv[i * n_big:(i + 1) * n_big])]

        return entries, make_scatter

    def ff_side(g_ff):
        hm, entries = swap_entries(g_ff, FF_WEIGHTS)
        return entries, lambda recv: scatter_entries("l0", FF_WEIGHTS, hm, recv)

    loss, grad_x, dmods, gbig, gsmall, dfnw, early_pieces = local_step(
        x[0], ctx[0], loss_target[0], mods, [full_weights(ATTN_WEIGHTS, got[1:])], small, final_norm_w, side=later,
        on_side=on_side, grad_side=grad_side, ff_side=ff_side)

    dm_lat = jnp.stack([dmods[l][1, 0] for l in range(depth)])
    dm_ctx = jnp.stack([dmods[l][0, 0] for l in range(depth)])
    small_pack = _pack([dm_lat, dm_ctx] + [jnp.stack([gsmall[l][k] for l in range(depth)]) for k in SMALL_NAMES] + [dfnw, loss])
    hm0, swap0 = swap_entries(gbig[0], ATTN_WEIGHTS)
    got = xchg("exchange_grads", [("gather", "all", [small_pack])] + swap0)
    small_all, recv0 = got[0], got[1:]
    small_sum = tree_sum("small_grad_sum", small_all)

    n_dm = depth * 6 * D
    dm_rows = n_dm // LANES
    dm_lat_all = small_all[:, :dm_rows].reshape(8, depth, 6 * D)
    dm_ctx_sum = small_sum[dm_rows:2 * dm_rows].reshape(depth, 6 * D)
    take = lambda a: lax.dynamic_slice_in_dim(a, chip * W_MOD_COLS, W_MOD_COLS, axis=-1)
    dmc_loc = take(dm_ctx_sum)
    cc_part = cctx_partial(jnp.pad(dmc_loc[:, None, :], ((0, 0), (0, 7), (0, 0))), w_mod)
    got = xchg("scatter_grads", [("gather", "chip", [cc_part])] + scatter_entries("l0", ATTN_WEIGHTS, hm0, recv0))
    cc_parts = got[0]
    keys = ([(0, k) for k in ATTN_WEIGHTS] + [(l, k) for l in range(1, depth) for k in BIG_NAMES] + [(0, k) for k in FF_WEIGHTS])
    pieces = dict(zip(keys, list(got[1:]) + list(early_pieces)))
    keys = [(l, k) for l in range(depth) for k in BIG_NAMES]
    reduced = {lk: tree_sum(f"chip_sum_l{lk[0]}_{lk[1]}", pieces[lk]) for lk in keys}
    g_c_ctx = cctx_grad(cc_parts, jnp.broadcast_to(c_ctx[None], (8, D)))[0]

    others = dict(zip(keys, xchg("share_halves", [("swap", "sib", [reduced[lk], reduced[lk]]) for lk in keys])))
    shard = {lk: jnp.where(ci == 0, jnp.concatenate([reduced[lk], others[lk]], axis=0),
                           jnp.concatenate([others[lk], reduced[lk]], axis=0)) for lk in keys}
    grads = {k: jnp.stack([shard[(l, k)] for l in range(depth)]) for k in BIG_NAMES}

    dm16 = jnp.concatenate([take(dm_lat_all).transpose(1, 0, 2), dmc_loc[:, None, :], jnp.zeros((depth, 7, W_MOD_COLS), F32)], axis=1)
    grads["w_mod"] = mod_weight_grad(c16, dm16)
    flat_sum = small_sum.reshape(-1)
    g_b_mod = (flat_sum[:n_dm] + flat_sum[n_dm:2 * n_dm]).reshape(depth, 6 * D)
    rest_shapes = [w[k].shape for k in PACKED[2:]]
    n_rest = sum(int(np.prod(s)) for s in rest_shapes)
    for k, g in zip(PACKED, [g_c_ctx, g_b_mod] + _unpack(flat_sum[2 * n_dm:2 * n_dm + n_rest], rest_shapes)):
        grads[k] = g
    loss = flat_sum[2 * n_dm + n_rest]

    delta, new_m, new_v = {}, {}, {}
    for k in BIG_NAMES + ("w_mod",):
        view = lambda a: a.reshape(-1, a.shape[-1])
        d_, m_, v_ = adamw(f"adamw_{k}", view(w[k]), view(grads[k]), view(m1[k]), view(m2[k]))
        delta[k], new_m[k], new_v[k] = d_.reshape(w[k].shape), m_.reshape(w[k].shape), v_.reshape(w[k].shape)
    shapes = [w[k].shape for k in PACKED]
    d_, m_, v_ = adamw("adamw_small", _pack([w[k] for k in PACKED]), _pack([grads[k] for k in PACKED]),
                       _pack([m1[k] for k in PACKED]), _pack([m2[k] for k in PACKED]))
    for k, dk, mk, vk in zip(PACKED, _unpack(d_, shapes), _unpack(m_, shapes), _unpack(v_, shapes)):
        delta[k], new_m[k], new_v[k] = dk, mk, vk
    return (loss, grad_x[None], *[grads[k] for k in WEIGHT_NAMES], *[delta[k] for k in WEIGHT_NAMES],
            *[new_m[k] for k in WEIGHT_NAMES], *[new_v[k] for k in WEIGHT_NAMES])
```

```python
import functools
import math

import numpy as np
import jax
import jax.numpy as jnp
from jax import lax
from jax.experimental import pallas as pl
from jax.experimental.pallas import tpu as pltpu

F32 = jnp.float32
BF16 = jnp.bfloat16
HI = lax.Precision.HIGHEST
EPS = 1e-6
VMEM_LIMIT_BYTES = 56 * 1024 * 1024
LANES = 128

D = 1024
D_FF = 4096
CTX = 256
GRID_W = 64
SGU_CHUNK = 128
GLA_CHUNK = 64
GLA_TAU = 16.0
GLA_DK = 32
MLA_SCALE = (128 + 64) ** -0.5
SCORE_SCALE = MLA_SCALE * math.log2(math.e)
LN2 = math.log(2.0)
ROPE_BASE = 10000.0
TM = 256
NCTXB = CTX // TM
P_GV, P_CKV, P_SU, P_SV, P_GR, P_DQ, P_GK, P_GATE, P_KR, P_GQ = 0, 256, 512, 768, 1024, 1280, 1536, 1664, 1792, 1920
P_COLS = 2048
IN_GROUPS = ((0, 128, P_GK), (128, 256, P_GV), (384, 32, P_GATE), (416, 256, P_CKV), (672, 64, P_KR),
             (736, 256, P_SU), (992, 256, P_SV), (1248, 128, P_GQ), (1376, 256, P_GR), (1632, 256, P_DQ))
ADAM_LR, ADAM_B1, ADAM_B2, ADAM_EPS, ADAM_WD, ADAM_STEP = 0.001, 0.9, 0.999, 1e-08, 0.01, 10
MESH = pl.DeviceIdType.MESH


def _params(sem):
    return pltpu.CompilerParams(dimension_semantics=sem, vmem_limit_bytes=VMEM_LIMIT_BYTES)


def _pick(n, cands):
    for c in cands:
        if n % c == 0:
            return c
    return n


class Op:
    def __init__(self, arr, blk, idx, gshape, gidx, acc):
        self.arr, self.blk, self.idx, self.gshape, self.gidx, self.acc = arr, blk, idx, gshape, gidx, acc

    def spec(self):
        return pl.BlockSpec(self.blk, self.idx)


def rows(arr, width=None, cb=0, off=0, tm=TM):
    w = arr.shape[1] if width is None else width
    n = arr.shape[0] - off * tm
    return Op(arr, (tm, w), lambda i: (i + off, cb), (n, w), lambda i: (i, 0), False)


def blank_rows(n, w, dtype, tm=TM):
    return Op(jnp.zeros((tm, w), dtype), (tm, w), lambda i: (0, 0), (n, w), lambda i: (i, 0), False)


def chunks(arr, per_tile):
    z = (0,) * (arr.ndim - 1)
    return Op(arr, (per_tile,) + arr.shape[1:], lambda i: (i,) + z, arr.shape, lambda i: (i,) + z, False)


def const(arr):
    z = (0,) * arr.ndim
    return Op(arr, arr.shape, lambda i: z, arr.shape, lambda i: z, True)


def rw(name, fn, ins, outs, grid):
    nin = len(ins)

    def body(*refs):
        vals = [r[...] for r in refs[:nin]]
        res = fn(pl.program_id(0), *vals)
        for o, r in zip(refs[nin:], res):
            o[...] = r.astype(o.dtype)

    return pl.pallas_call(
        body, name=name, grid=(grid,),
        in_specs=[o.spec() for o in ins],
        out_specs=[pl.BlockSpec(b, ix) for (_, _, b, ix) in outs],
        out_shape=[jax.ShapeDtypeStruct(s, d) for (s, d, _, _) in outs],
        compiler_params=_params(("arbitrary",)),
    )(*[o.arr for o in ins])


def rowout(n, w, dtype, tm=TM):
    return ((n, w), dtype, (tm, w), lambda i: (i, 0))


def chunkout(shape, dtype, per_tile):
    z = (0,) * (len(shape) - 1)
    return (shape, dtype, (per_tile,) + tuple(shape[1:]), lambda i: (i,) + z)


def rw_vjp(name, fn, ins, cots, wrt, grid, gdt=None, adds=None):
    nin = len(ins)
    cot_ops = [c for c in cots if c is not None]
    add_items = sorted((adds or {}).items())
    gdt = gdt or [F32] * len(wrt)
    ncot, nadd = len(cot_ops), len(add_items)

    def body(*refs):
        i = pl.program_id(0)
        vals = [r[...] for r in refs[:nin]]
        cvals = [r[...] for r in refs[nin:nin + ncot]]
        avals = [r[...] for r in refs[nin + ncot:nin + ncot + nadd]]
        grefs = refs[nin + ncot + nadd:]

        def f(*d):
            a = list(vals)
            for k, dv in zip(wrt, d):
                a[k] = dv
            return tuple(fn(i, *a))

        outs, vf = jax.vjp(f, *[vals[k] for k in wrt])
        it = iter(cvals)
        ct = tuple(jnp.zeros_like(o) if c is None else next(it).astype(o.dtype) for c, o in zip(cots, outs))
        gs = list(vf(ct))
        for (pos, _), av in zip(add_items, avals):
            gs[pos] = gs[pos].astype(F32) + av.astype(F32)
        for pos, (k, g, gref) in enumerate(zip(wrt, gs, grefs)):
            if ins[k].acc:
                @pl.when(i == 0)
                def _():
                    gref[...] = jnp.zeros_like(gref)
                gref[...] += g.astype(gref.dtype)
            else:
                gref[...] = g.astype(gref.dtype)

    all_in = list(ins) + cot_ops + [op for _, op in add_items]
    return pl.pallas_call(
        body, name=name, grid=(grid,),
        in_specs=[o.spec() for o in all_in],
        out_specs=[pl.BlockSpec(ins[k].blk, ins[k].gidx) for k in wrt],
        out_shape=[jax.ShapeDtypeStruct(ins[k].gshape, dt) for k, dt in zip(wrt, gdt)],
        compiler_params=_params(("arbitrary",)),
    )(*[o.arr for o in all_in])


MM_VMEM_BUDGET = 40 * 1024 * 1024
MM_COLS = 1024


def _square_bf16(a):
    a = a.astype(F32)
    return (a * a).astype(BF16)


def mm(name, a, b, out_dtype, pre=None, post=None, extras=(), bt=False):
    m, k = a.shape
    n = b.shape[0] if bt else b.shape[1]
    nc = min(n, MM_COLS)
    row_bytes = k * a.dtype.itemsize + n * jnp.dtype(out_dtype).itemsize + sum(n * e.dtype.itemsize for e in extras)
    tm = next(t for t in (768, 512, 384, 256, 128, 64)
              if m % t == 0 and 2 * t * row_bytes + 2 * k * n * b.dtype.itemsize + t * nc * 4 <= MM_VMEM_BUDGET)

    def body(a_ref, b_ref, *rest):
        o_ref = rest[-1]
        av = a_ref[...]
        if pre is not None:
            av = pre(av)
        for j in range(n // nc):
            cs = slice(j * nc, (j + 1) * nc)
            if bt:
                acc = lax.dot_general(av, b_ref[cs, :], (((1,), (1,)), ((), ())), preferred_element_type=F32)
            else:
                acc = lax.dot_general(av, b_ref[:, cs], (((1,), (0,)), ((), ())), preferred_element_type=F32)
            if post is not None:
                acc = post(acc, *[e[:, cs] for e in rest[:-1]])
            o_ref[:, cs] = acc.astype(o_ref.dtype)

    row = lambda w: pl.BlockSpec((tm, w), lambda i: (i, 0))
    return pl.pallas_call(
        body, name=name, grid=(m // tm,),
        in_specs=[row(k), pl.BlockSpec(b.shape, lambda i: (0, 0))] + [row(n) for _ in extras],
        out_specs=row(n),
        out_shape=jax.ShapeDtypeStruct((m, n), out_dtype),
        compiler_params=_params(("arbitrary",)),
    )(a, b, *extras)


def mm_tn(name, a, b, pre=None, side=None):
    m, ka = a.shape
    _, nb = b.shape
    tm = _pick(m, (768, 512, 256))
    ta = _pick(ka, (2048, 1024, 512, 256, 128))
    tb = _pick(nb, tuple(t for t in (4096, 2048, 1024, 512, 256, 128) if ta * t * 4 <= 8 * 1024 * 1024))
    sd = side or Side(())

    def body(a_ref, b_ref, *rest):
        o_ref = rest[sd.n_in]
        finish = sd.start(rest[:sd.n_in], rest[sd.n_in + 1:sd.n_in + 1 + sd.n_out], rest[sd.n_in + 1 + sd.n_out:])

        @pl.when(pl.program_id(2) == 0)
        def _():
            o_ref[...] = jnp.zeros_like(o_ref)
        av = a_ref[...] if pre is None else pre(a_ref[...])
        o_ref[...] += lax.dot_general(av, b_ref[...], (((0,), (0,)), ((), ())), preferred_element_type=F32)
        finish()

    res = pl.pallas_call(
        body, name=name, grid=(ka // ta, nb // tb, m // tm),
        in_specs=[pl.BlockSpec((tm, ta), lambda i, j, k: (k, i)), pl.BlockSpec((tm, tb), lambda i, j, k: (k, j))] + sd.in_specs,
        out_specs=[pl.BlockSpec((ta, tb), lambda i, j, k: (i, j))] + sd.out_specs,
        out_shape=[jax.ShapeDtypeStruct((ka, nb), F32)] + sd.shapes,
        scratch_shapes=sd.sems,
        compiler_params=_params(("arbitrary", "arbitrary", "arbitrary")),
    )(a, b, *sd.arrays)
    return res[0] if side is None else (res[0], list(res[1:]))


def _rms(x, w):
    return x * lax.rsqrt(jnp.mean(x * x, axis=-1, keepdims=True) + EPS) * w


def _mod_of(blk, m, n_rows):
    if n_rows == CTX:
        mv = jnp.where(blk == 0, m[0], m[1])
        return lambda lo, hi: mv[:, lo:hi]
    is_ctx = blk * n_rows + lax.broadcasted_iota(jnp.int32, (n_rows, 1), 0) < CTX
    return lambda lo, hi: jnp.where(is_ctx, m[0][:, lo:hi], m[1][:, lo:hi])


def _gelu(x):
    return x * (0.5 * (1.0 + jnp.tanh(math.sqrt(2.0 / math.pi) * (x + 0.044715 * (x * x * x)))))


def _sigmoid(x):
    return 1.0 / (1.0 + jnp.exp(-x))


def _log_sigmoid(z):
    return jnp.minimum(z, 0.0) - jnp.log(1.0 + jnp.exp(-jnp.abs(z)))


def _dot(a, b, dims=((1,), (0,)), precision=None):
    return lax.dot_general(a, b, (dims, ((), ())), precision=precision, preferred_element_type=F32)


def _lane_group_mask(width, group, h):
    lane = lax.broadcasted_iota(jnp.int32, (1, width), 1)
    return (lane >= h * group) & (lane < (h + 1) * group)


def fn_norm1(blk, x, m, nw):
    mv = _mod_of(blk, m, x.shape[0])
    return ((_rms(x, nw) * (1.0 + mv(D, 2 * D)) + mv(0, D)),)


def fn_res_norm2(blk, x, yo, m, nw):
    mv = _mod_of(blk, m, x.shape[0])
    x1 = x + mv(2 * D, 3 * D) * yo
    return x1, _rms(x1, nw) * (1.0 + mv(4 * D, 5 * D)) + mv(3 * D, 4 * D)


def fn_res2(blk, x1, f, m):
    mv = _mod_of(blk, m, x1.shape[0])
    return (x1 + mv(5 * D, 6 * D) * f,)


def fn_sgu(blk, su, sv, nw, nb, ws, bm):
    u = _gelu(su)
    g = _gelu(sv)
    mu = jnp.mean(g, axis=-1, keepdims=True)
    var = jnp.mean(jnp.square(g - mu), axis=-1, keepdims=True)
    v = (g - mu) * lax.rsqrt(var + EPS) * nw + nb
    out = []
    for c in range(su.shape[0] // SGU_CHUNK):
        vc = v[c * SGU_CHUNK:(c + 1) * SGU_CHUNK]
        s = bm
        for h in range(4):
            vh = jnp.where(_lane_group_mask(256, 64, h), vc, 0.0)
            s = s + _dot(ws[h].astype(BF16), vh.astype(BF16))
        out.append(u[c * SGU_CHUNK:(c + 1) * SGU_CHUNK] * s)
    return (jnp.concatenate(out, axis=0),)


def fn_gates(blk, pg, wg, bg):
    z = _dot(pg.astype(BF16), wg.astype(BF16)) + bg
    g = _log_sigmoid(z) * (1.0 / GLA_TAU)
    return g[:, :128], g[:, 128:]


def _scan_rows(x, rev):
    n = x.shape[0]
    row = lax.broadcasted_iota(jnp.int32, x.shape, 0)
    d = 1
    while d < n:
        if rev:
            x = x + jnp.where(row < n - d, pltpu.roll(x, n - d, 0), 0.0)
        else:
            x = x + jnp.where(row >= d, pltpu.roll(x, d, 0), 0.0)
        d *= 2
    return x


@functools.partial(jax.custom_vjp, nondiff_argnums=(1,))
def _cumsum_rows(x, rev):
    return _scan_rows(x, rev)


def _cumsum_rows_fwd(x, rev):
    return _scan_rows(x, rev), None


def _cumsum_rows_bwd(rev, _, dy):
    return (_scan_rows(dy, not rev),)


_cumsum_rows.defvjp(_cumsum_rows_fwd, _cumsum_rows_bwd)


def _gla_chunk_terms(g, rev):
    return _cumsum_rows(g, rev), jnp.sum(g, axis=0, keepdims=True)


def _bd_mask():
    r = lax.broadcasted_iota(jnp.int32, (128, 256), 0)
    c = lax.broadcasted_iota(jnp.int32, (128, 256), 1)
    return (r // GLA_DK) == (c // 64)


def _gla_kv_chunk(k, v, g, rev):
    b, tot = _gla_chunk_terms(g, rev)
    kd = k * jnp.exp(tot - b)
    u = jnp.where(_bd_mask(), _dot(kd.astype(BF16), v.astype(BF16), ((0,), (0,))), 0.0)
    r = lax.broadcasted_iota(jnp.int32, (128, 128), 0)
    c = lax.broadcasted_iota(jnp.int32, (128, 128), 1)
    col = jnp.sum(jnp.where(r == c, jnp.broadcast_to(jnp.exp(tot), (128, 128)), 0.0), axis=1, keepdims=True)
    return u, jnp.broadcast_to(col, (128, 128))


def _gla_o_chunk(q, k, v, g, s, rev):
    b, _ = _gla_chunk_terms(g, rev)
    qe = q * jnp.exp(b) * (GLA_DK ** -0.5)
    ke = k * jnp.exp(-b)
    o = _dot(qe.astype(BF16), jnp.where(_bd_mask(), s, 0.0).astype(BF16))
    qs = jnp.concatenate([jnp.where(_lane_group_mask(128, GLA_DK, h), qe, 0.0) for h in range(4)], axis=0)
    a = _dot(qs.astype(BF16), ke.astype(BF16), ((1,), (1,)))
    i = lax.broadcasted_iota(jnp.int32, a.shape, 0) % GLA_CHUNK
    j = lax.broadcasted_iota(jnp.int32, a.shape, 1)
    a = jnp.where((j >= i) if rev else (j <= i), a, 0.0)
    av = _dot(a.astype(BF16), v.astype(BF16))
    for h in range(4):
        o = o + jnp.where(_lane_group_mask(256, 64, h), av[GLA_CHUNK * h:GLA_CHUNK * (h + 1)], 0.0)
    return o


def fn_gla_kv(blk, k, v, gf, gb):
    uf, ef, ub, eb = [], [], [], []
    for c in range(k.shape[0] // GLA_CHUNK):
        sl = slice(c * GLA_CHUNK, (c + 1) * GLA_CHUNK)
        u, e = _gla_kv_chunk(k[sl], v[sl], gf[sl], False)
        uf.append(u[None]); ef.append(e[None])
        u, e = _gla_kv_chunk(k[sl], v[sl], gb[sl], True)
        ub.append(u[None]); eb.append(e[None])
    cat = lambda t: jnp.concatenate(t, axis=0)
    return cat(uf), cat(ef), cat(ub), cat(eb)


def fn_gla_o(blk, q, k, v, gf, gb, gr, sf, sb, nwt):
    out = []
    for c in range(q.shape[0] // GLA_CHUNK):
        sl = slice(c * GLA_CHUNK, (c + 1) * GLA_CHUNK)
        out.append(_gla_o_chunk(q[sl], k[sl], v[sl], gf[sl], sf[c], False)
                   + _gla_o_chunk(q[sl], k[sl], v[sl], gb[sl], sb[c], True))
    o = jnp.concatenate(out, axis=0)
    r = lax.broadcasted_iota(jnp.int32, (256, 256), 0)
    c = lax.broadcasted_iota(jnp.int32, (256, 256), 1)
    head_mean = jnp.where((r // 64) == (c // 64), 1.0 / 64.0, 0.0).astype(F32)
    ms = _dot(o * o, head_mean, precision=HI)
    on = o * lax.rsqrt(ms + EPS) * nwt
    return (on * (gr * _sigmoid(gr)),)


def _rope_partner(x):
    lane = lax.broadcasted_iota(jnp.int32, x.shape, 1)
    return jnp.where((lane // 16) % 2 == 0, pltpu.roll(x, LANES - 16, 1), pltpu.roll(x, 16, 1))


@jax.custom_vjp
def _rope(x, cs, sn):
    return x * cs + _rope_partner(x) * sn


def _rope_fwd(x, cs, sn):
    return _rope(x, cs, sn), (cs, sn)


def _rope_bwd(res, dy):
    cs, sn = res
    return dy * cs + _rope_partner(dy * sn), jnp.zeros_like(cs), jnp.zeros_like(sn)


_rope.defvjp(_rope_fwd, _rope_bwd)


def fn_mla_pre(blk, ckv, dq, kvw, qw):
    return _rms(ckv, kvw), _rms(dq, qw)


def fn_mla_post(blk, kk, qu, kr, cs, sn):
    kro = _rope(kr, cs, sn)
    kcat, q = [], []
    for h in range(4):
        kcat += [kk[:, 128 * h:128 * (h + 1)].astype(F32), kro]
        q += [qu[:, 256 * h:256 * h + 128], _rope(qu[:, 256 * h + 128:256 * (h + 1)], cs, sn)]
    return jnp.concatenate(kcat, axis=1), jnp.concatenate(q, axis=1) * SCORE_SCALE


ATTN_ROWS = 256
NEG = -1e30


def _scores(q, k, k0, context_queries):
    s = _dot(q, k, ((1,), (1,)))
    if context_queries is not None:
        col = k0 + lax.broadcasted_iota(jnp.int32, s.shape, 1)
        s = jnp.where(context_queries & (col >= CTX), NEG, s)
    return s


def flash_fwd(q, kcat, kvu, side=()):
    t = q.shape[0]
    tq = _pick(t, (768, 512, 256))
    tk = _pick(t, (2816, 1536, 768, 512, 256))
    nsub = tq // ATTN_ROWS
    n_side_in, _, _, side_shapes = _exchange_shapes(side)
    n_side = len(side)

    def body(q_ref, k_ref, v_ref, *rest):
        side_in, rest = rest[:n_side_in], rest[n_side_in:]
        o_ref, lse_ref = rest[:2]
        side_out, (m_sc, l_sc, acc_sc), side_sems = rest[2:2 + n_side], rest[2 + n_side:5 + n_side], rest[5 + n_side:]
        h, qi, ki = pl.program_id(0), pl.program_id(1), pl.program_id(2)
        if side:
            starts, forwards, finals = _exchange_phases(side, side_in, side_out, *side_sems)
            at_tile0 = (qi == 0) & (ki == 0)
            last = (h == pl.num_programs(0) - 1) & (qi == pl.num_programs(1) - 1) & (ki == pl.num_programs(2) - 1)
            for when, phase in (((h == 0) & at_tile0, starts), ((h == 2) & at_tile0, forwards)):
                @pl.when(when)
                def _(phase=phase):
                    for run in phase:
                        run()

        @pl.when(ki == 0)
        def _():
            m_sc[...] = jnp.full_like(m_sc, NEG)
            l_sc[...] = jnp.zeros_like(l_sc)
            acc_sc[...] = jnp.zeros_like(acc_sc)

        k, v = k_ref[...], v_ref[...]
        chains = [pl.ds(r * ATTN_ROWS, ATTN_ROWS) for r in range(nsub)]
        scores = [_scores(q_ref[rs, :], k, ki * tk, (qi == 0) if r == 0 else None) for r, rs in enumerate(chains)]
        probs = []
        for rs, s in zip(chains, scores):
            m_old = m_sc[rs, :]
            m_new = jnp.maximum(m_old, jnp.max(s, axis=-1, keepdims=True))
            alpha = jnp.exp2(m_old - m_new)
            p = jnp.exp2(s - m_new)
            l_sc[rs, :] = alpha * l_sc[rs, :] + jnp.sum(p, axis=-1, keepdims=True)
            m_sc[rs, :] = m_new
            probs.append((alpha, p.astype(BF16)))
        for rs, (alpha, p) in zip(chains, probs):
            acc_sc[rs, :] = alpha * acc_sc[rs, :] + _dot(p, v)

        @pl.when(ki == pl.num_programs(2) - 1)
        def _():
            o_ref[...] = acc_sc[...] / l_sc[...]
            lse_ref[...] = jnp.broadcast_to(m_sc[...] + jnp.log2(l_sc[...]), lse_ref.shape)

        if side:
            @pl.when(last)
            def _():
                for run in finals:
                    run()

    any_spec = pl.BlockSpec(memory_space=pl.ANY)
    res = pl.pallas_call(
        body, name="mla_flash_fwd", grid=(4, t // tq, t // tk),
        in_specs=[pl.BlockSpec((tq, 256), lambda h, i, j: (i, h)), pl.BlockSpec((tk, 256), lambda h, i, j: (j, h)),
                  pl.BlockSpec((tk, 128), lambda h, i, j: (j, 4 + h))] + [any_spec] * n_side_in,
        out_specs=[pl.BlockSpec((tq, 128), lambda h, i, j: (i, h)), pl.BlockSpec((tq, 128), lambda h, i, j: (i, h))]
        + [any_spec] * n_side,
        out_shape=[jax.ShapeDtypeStruct((t, 512), F32), jax.ShapeDtypeStruct((t, 512), F32)] + side_shapes,
        scratch_shapes=[pltpu.VMEM((tq, 1), F32), pltpu.VMEM((tq, 1), F32), pltpu.VMEM((tq, 128), F32)]
        + (_exchange_sems(side) if side else []),
        compiler_params=_params(("arbitrary", "arbitrary", "arbitrary")),
    )(q, kcat, kvu, *[a for _, _, arrs in side for a in arrs])
    return res[0], res[1], list(res[2:])


def fn_attn_stats(blk, do, o, lse):
    blocks = []
    for u in range(do.shape[0] // TM):
        rs = slice(u * TM, (u + 1) * TM)
        out = []
        for h in range(4):
            hs = slice(128 * h, 128 * (h + 1))
            d = jnp.sum(do[rs, hs] * o[rs, hs], axis=-1, keepdims=True)
            out.append(lse[rs, hs].T[0:8])
            out.append(jnp.broadcast_to(d, (TM, 128)).T[0:8])
        blocks.append(jnp.concatenate(out, axis=0)[None])
    return (jnp.concatenate(blocks, axis=0),)


def flash_bwd(q, kcat, kvu, dy, stats, side=None):
    t = q.shape[0]
    tq = _pick(t, (2816, 768, 512, 256))
    tk = _pick(t, (768, 512, 256))
    nst = tq // TM
    sd = side or Side(())

    def body(q_ref, k_ref, v_ref, do_ref, st_ref, *rest):
        dq_ref, dk_ref, dv_ref = rest[sd.n_in:sd.n_in + 3]
        finish = sd.start(rest[:sd.n_in], rest[sd.n_in + 3:sd.n_in + 3 + sd.n_out], rest[sd.n_in + 3 + sd.n_out:])
        kj, qi = pl.program_id(1), pl.program_id(2)

        @pl.when(qi == 0)
        def _():
            dk_ref[...] = jnp.zeros_like(dk_ref)
            dv_ref[...] = jnp.zeros_like(dv_ref)

        def step(has_context_queries):
            q_, k, v, do = q_ref[...], k_ref[...], v_ref[...], do_ref[...].astype(BF16)
            lse_row = jnp.concatenate([st_ref[u, 0:1, :] for u in range(nst)], axis=1)
            delta_row = jnp.concatenate([st_ref[u, 8:9, :] for u in range(nst)], axis=1)
            s = _dot(k, q_, ((1,), (1,)))
            if has_context_queries:
                key = kj * tk + lax.broadcasted_iota(jnp.int32, s.shape, 0)
                qry = lax.broadcasted_iota(jnp.int32, s.shape, 1)
                s = jnp.where((qry < CTX) & (key >= CTX), NEG, s)
            p = jnp.exp2(s - lse_row)
            dp = _dot(v, do, ((1,), (1,)))
            ds = (p * (dp - delta_row)).astype(BF16)
            dv_ref[...] += _dot(p.astype(BF16), do)
            dk_ref[...] += LN2 * _dot(ds, q_)
            dq_new = LN2 * _dot(ds, k, ((0,), (0,)))
            rows_ = pl.ds(pl.multiple_of(qi * tq, TM), tq)

            @pl.when(kj == 0)
            def _():
                dq_ref[rows_, :] = dq_new

            @pl.when(kj != 0)
            def _():
                dq_ref[rows_, :] += dq_new

        pl.when(qi == 0)(lambda: step(True))
        pl.when(qi != 0)(lambda: step(False))
        finish()

    res = pl.pallas_call(
        body, name="mla_flash_bwd", grid=(4, t // tk, t // tq),
        in_specs=[pl.BlockSpec((tq, 256), lambda h, j, i: (i, h)), pl.BlockSpec((tk, 256), lambda h, j, i: (j, h)),
                  pl.BlockSpec((tk, 128), lambda h, j, i: (j, 4 + h)), pl.BlockSpec((tq, 128), lambda h, j, i: (i, 4 + h)),
                  pl.BlockSpec((nst, 16, 256), lambda h, j, i: (i, h, 0))] + sd.in_specs,
        out_specs=[pl.BlockSpec((t, 256), lambda h, j, i: (0, h)), pl.BlockSpec((tk, 256), lambda h, j, i: (j, h)),
                   pl.BlockSpec((tk, 128), lambda h, j, i: (j, h))] + sd.out_specs,
        out_shape=[jax.ShapeDtypeStruct((t, 1024), F32), jax.ShapeDtypeStruct((t, 1024), F32),
                   jax.ShapeDtypeStruct((t, 512), F32)] + sd.shapes,
        scratch_shapes=sd.sems,
        compiler_params=_params(("arbitrary", "arbitrary", "arbitrary")),
    )(q, kcat, kvu, dy, stats, *sd.arrays)
    return res[0], res[1], res[2], list(res[3:])


SCAN_BLOCK = CTX // GLA_CHUNK


def _scan_block(t, nb, rev):
    if not rev:
        return t
    return jnp.where(t < 1, 0, nb - t)


def _scan_order(rev):
    return tuple(reversed(range(SCAN_BLOCK))) if rev else tuple(range(SCAN_BLOCK))


def _both_halves(e):
    return jnp.concatenate([e, e], axis=1)


def gla_states(uf, ef, ub, eb):
    nb = uf.shape[0] // SCAN_BLOCK

    def body(uf_ref, ef_ref, ub_ref, eb_ref, sf_ref, sb_ref, sf_sc, sb_sc):
        @pl.when(pl.program_id(0) == 0)
        def _():
            sf_sc[...] = jnp.zeros_like(sf_sc)
            sb_sc[...] = jnp.zeros_like(sb_sc)

        for u_ref, e_ref, s_ref, sc, rev in ((uf_ref, ef_ref, sf_ref, sf_sc, False), (ub_ref, eb_ref, sb_ref, sb_sc, True)):
            s = sc[...]
            for c in _scan_order(rev):
                s_ref[c] = s
                s = _both_halves(e_ref[c]) * s + u_ref[c]
            sc[...] = s

    big = lambda rev: pl.BlockSpec((SCAN_BLOCK, 128, 256), lambda t: (_scan_block(t, nb, rev), 0, 0))
    small = lambda rev: pl.BlockSpec((SCAN_BLOCK, 128, 128), lambda t: (_scan_block(t, nb, rev), 0, 0))
    return pl.pallas_call(
        body, name="gla_states", grid=(nb,),
        in_specs=[big(False), small(False), big(True), small(True)],
        out_specs=[big(False), big(True)],
        out_shape=[jax.ShapeDtypeStruct(uf.shape, F32)] * 2,
        scratch_shapes=[pltpu.VMEM((128, 256), F32)] * 2,
        compiler_params=_params(("arbitrary",)),
    )(uf, ef, ub, eb)


def gla_states_bwd(ef, eb, sf, sb, dsf, dsb):
    nb = ef.shape[0] // SCAN_BLOCK

    def body(ef_ref, eb_ref, sf_ref, sb_ref, dsf_ref, dsb_ref, duf_ref, def_ref, dub_ref, deb_ref, gf_sc, gb_sc):
        @pl.when(pl.program_id(0) == 0)
        def _():
            gf_sc[...] = jnp.zeros_like(gf_sc)
            gb_sc[...] = jnp.zeros_like(gb_sc)

        for e_ref, s_ref, ds_ref, du_ref, de_ref, g_sc, rev in ((ef_ref, sf_ref, dsf_ref, duf_ref, def_ref, gf_sc, False),
                                                                 (eb_ref, sb_ref, dsb_ref, dub_ref, deb_ref, gb_sc, True)):
            g = g_sc[...]
            for k in reversed(_scan_order(rev)):
                du_ref[k] = g
                gs = g * s_ref[k]
                de_ref[k] = gs[:, :128] + gs[:, 128:]
                g = _both_halves(e_ref[k]) * g + ds_ref[k]
            g_sc[...] = g

    big = lambda rev: pl.BlockSpec((SCAN_BLOCK, 128, 256), lambda t: (_scan_block(nb - 1 - t, nb, rev), 0, 0))
    small = lambda rev: pl.BlockSpec((SCAN_BLOCK, 128, 128), lambda t: (_scan_block(nb - 1 - t, nb, rev), 0, 0))
    return pl.pallas_call(
        body, name="gla_states_bwd", grid=(nb,),
        in_specs=[small(False), small(True), big(False), big(True), big(False), big(True)],
        out_specs=[big(False), small(False), big(True), small(True)],
        out_shape=[jax.ShapeDtypeStruct(sf.shape, F32), jax.ShapeDtypeStruct(ef.shape, F32)] * 2,
        scratch_shapes=[pltpu.VMEM((128, 256), F32)] * 2,
        compiler_params=_params(("arbitrary",)),
    )(ef, eb, sf, sb, dsf, dsb)


def loss_head(xt, target, fnw):
    t = xt.shape[0]

    def f(x, tg, w):
        y = _rms(x, w)
        return 0.5 * jnp.sum(jnp.square(y - tg)) * (1.0 / D)

    def body(x_ref, t_ref, w_ref, loss_ref, dx_ref, dw_ref):
        i = pl.program_id(0)

        @pl.when(i == 0)
        def _():
            loss_ref[...] = jnp.zeros_like(loss_ref)
            dw_ref[...] = jnp.zeros_like(dw_ref)

        @pl.when(i < NCTXB)
        def _():
            dx_ref[...] = jnp.zeros_like(dx_ref)

        @pl.when(i >= NCTXB)
        def _():
            val, (dx, dw) = jax.value_and_grad(f, argnums=(0, 2))(x_ref[...], t_ref[...], w_ref[...])
            loss_ref[...] += jnp.broadcast_to(val, loss_ref.shape)
            dx_ref[...] = dx
            dw_ref[...] += dw

    return pl.pallas_call(
        body, name="loss_head", grid=(t // TM,),
        in_specs=[pl.BlockSpec((TM, D), lambda i: (i, 0)), pl.BlockSpec((TM, D), lambda i: (jnp.maximum(i - NCTXB, 0), 0)),
                  pl.BlockSpec((1, D), lambda i: (0, 0))],
        out_specs=[pl.BlockSpec((1, 128), lambda i: (0, 0)), pl.BlockSpec((TM, D), lambda i: (i, 0)),
                   pl.BlockSpec((1, D), lambda i: (0, 0))],
        out_shape=[jax.ShapeDtypeStruct((1, 128), F32), jax.ShapeDtypeStruct((t, D), F32), jax.ShapeDtypeStruct((1, D), F32)],
        compiler_params=_params(("arbitrary",)),
    )(xt, target, fnw)


def _in_to_padded(w):
    out, pos = [], 0
    for src, wd, dst in sorted(IN_GROUPS, key=lambda g: g[2]):
        if dst > pos:
            out.append(jnp.zeros((w.shape[0], dst - pos), w.dtype))
        out.append(w[:, src:src + wd])
        pos = dst + wd
    if pos < P_COLS:
        out.append(jnp.zeros((w.shape[0], P_COLS - pos), w.dtype))
    return jnp.concatenate(out, axis=1)


def _in_from_padded(g):
    return jnp.concatenate([g[:, dst:dst + wd] for _, wd, dst in IN_GROUPS], axis=1)


def _uq_to_padded(w):
    return jnp.pad(w.reshape(256, 4, 192), ((0, 0), (0, 0), (0, 64))).reshape(256, 1024)


def _uq_from_padded(g):
    return g.reshape(256, 4, 256)[:, :, :192].reshape(256, 768)


def _ukv_to_padded(w):
    return w.reshape(256, 4, 2, 128).transpose(0, 2, 1, 3).reshape(256, 1024)


def _ukv_from_padded(g):
    return g.reshape(256, 2, 4, 128).transpose(0, 2, 1, 3).reshape(256, 1024)


def _rope_tables(n):
    freq = ROPE_BASE ** (-jnp.arange(16, dtype=F32) * 2.0 / 32.0)
    grid_h = n // GRID_W
    ar = jnp.repeat(jnp.arange(grid_h, dtype=F32)[:, None] * freq[None, :], GRID_W, axis=0)
    ac = jnp.tile(jnp.arange(GRID_W, dtype=F32)[:, None] * freq[None, :], (grid_h, 1))
    z = jnp.zeros((n, 64), F32)
    cs = jnp.concatenate([jnp.cos(ar), jnp.cos(ar), jnp.cos(ac), jnp.cos(ac), z], axis=1)
    sn = jnp.concatenate([-jnp.sin(ar), jnp.sin(ar), -jnp.sin(ac), jnp.sin(ac), z], axis=1)
    cs_c = jnp.concatenate([jnp.ones((CTX, 64), F32), jnp.zeros((CTX, 64), F32)], axis=1)
    return jnp.concatenate([cs_c, cs], axis=0), jnp.concatenate([jnp.zeros((CTX, 128), F32), sn], axis=0)


def _small_views(sp):
    wg = jnp.concatenate([jnp.pad(sp["gla_wg_fwd"], ((0, 112), (0, 0))), jnp.pad(sp["gla_wg_bwd"], ((16, 96), (0, 0)))], axis=1)
    return dict(
        n1w=sp["norm1_w"][None], n2w=sp["norm2_w"][None],
        sgu_nw=sp["sgu_norm_w"][None], sgu_nb=sp["sgu_norm_b"][None], sgu_w=sp["sgu_w"],
        sgu_bm=jnp.repeat(sp["sgu_b"].T, 64, axis=1),
        wg=wg, bg=jnp.concatenate([sp["gla_bg_fwd"], sp["gla_bg_bwd"]])[None],
        gla_nwt=jnp.tile(sp["gla_norm_w"], 4)[None],
        kvw=sp["mla_kv_norm_w"][None], qw=sp["mla_q_norm_w"][None])


def _small_grads(g):
    return dict(
        norm1_w=g["n1w"][0], norm2_w=g["n2w"][0],
        sgu_norm_w=g["sgu_nw"][0], sgu_norm_b=g["sgu_nb"][0], sgu_w=g["sgu_w"],
        sgu_b=g["sgu_bm"].reshape(128, 4, 64).sum(-1).T,
        gla_wg_fwd=g["wg"][0:16, 0:128], gla_wg_bwd=g["wg"][16:32, 128:256],
        gla_bg_fwd=g["bg"][0, 0:128], gla_bg_bwd=g["bg"][0, 128:256],
        gla_norm_w=g["gla_nwt"].reshape(4, 64).sum(0),
        mla_kv_norm_w=g["kvw"][0], mla_q_norm_w=g["qw"][0])


def _big_views(full):
    views = {}
    if "w_in" in full:
        views.update(win=_in_to_padded(full["w_in"]), wuq=_uq_to_padded(full["mla_w_uq"]),
                     wukv=_ukv_to_padded(full["mla_w_ukv"]), wout=full["w_out"])
    if "w_ff1" in full:
        views.update(w1=full["w_ff1"], w2=full["w_ff2"])
    return views


def _gla_tile(t):
    return _pick(t, (768, 512, 256))


def _layer_ops(p, sv, a):
    gt = _gla_tile(p.shape[0])
    pc = lambda off, w: rows(p, w, off // w, tm=gt)
    gr = lambda arr, w=None, cb=0: rows(arr, w, cb, tm=gt)
    return dict(
        sgu=[pc(P_SU, 256), pc(P_SV, 256), const(sv["sgu_nw"]), const(sv["sgu_nb"]), const(sv["sgu_w"]), const(sv["sgu_bm"])],
        gates=[pc(P_GATE, 128), const(sv["wg"]), const(sv["bg"])],
        mla_pre=[pc(P_CKV, 256), pc(P_DQ, 256), const(sv["kvw"]), const(sv["qw"])],
        gla_kv=lambda: [pc(P_GK, 128), pc(P_GV, 256), gr(a["gf"]), gr(a["gb"])],
        gla_o=lambda: [pc(P_GQ, 128), pc(P_GK, 128), pc(P_GV, 256), gr(a["gf"]), gr(a["gb"]), pc(P_GR, 256),
                       chunks(a["sf"], gt // GLA_CHUNK), chunks(a["sb"], gt // GLA_CHUNK), const(sv["gla_nwt"])],
        mla_post=lambda: [gr(a["kvu"], 512, 0), gr(a["qu"]), pc(P_KR, 128), gr(a["cs"]), gr(a["sn"])])


def layer_fwd(l, xt, modl, bw, sv, tabs, side=(), late=None):
    t = xt.shape[0]
    g, nc, gt = t // TM, t // GLA_CHUNK, _gla_tile(t)
    gg, cpt = t // gt, gt // GLA_CHUNK
    nm = lambda s: f"l{l}_{s}"
    a = dict(x=xt, cs=tabs[0], sn=tabs[1])
    a["h"], = rw(nm("norm1"), fn_norm1, [rows(xt, tm=gt), const(modl), const(sv["n1w"])], [rowout(t, D, BF16, tm=gt)], gg)
    p = a["p"] = mm(nm("in_proj"), a["h"], bw["win"], F32)
    ops = _layer_ops(p, sv, a)
    y_sgu, = rw(nm("sgu"), fn_sgu, ops["sgu"], [rowout(t, 256, BF16, tm=gt)], gg)
    a["gf"], a["gb"] = rw(nm("gates"), fn_gates, ops["gates"], [rowout(t, 128, F32, tm=gt)] * 2, gg)
    a["uf"], a["ef"], a["ub"], a["eb"] = rw(nm("gla_kv"), fn_gla_kv, ops["gla_kv"](),
                                           [chunkout((nc, 128, 256), F32, cpt), chunkout((nc, 128, 128), F32, cpt)] * 2, gg)
    a["sf"], a["sb"] = gla_states(a["uf"], a["ef"], a["ub"], a["eb"])
    y_gla, = rw(nm("gla_o"), fn_gla_o, ops["gla_o"](), [rowout(t, 256, BF16, tm=gt)], gg)
    a["ckvn"], a["dqn"] = rw(nm("mla_pre"), fn_mla_pre, ops["mla_pre"], [rowout(t, 256, BF16, tm=gt)] * 2, gg)
    a["kvu"] = mm(nm("kv_up"), a["ckvn"], bw["wukv"], BF16)
    a["qu"] = mm(nm("q_up"), a["dqn"], bw["wuq"], F32)
    a["kcat"], a["q"] = rw(nm("mla_post"), fn_mla_post, ops["mla_post"](), [rowout(t, 1024, BF16, tm=gt)] * 2, gg)
    a["o"], a["lse"], side_out = flash_fwd(a["q"], a["kcat"], a["kvu"], side)
    if late is not None:
        late(side_out)
    a["y"] = jnp.concatenate([y_sgu, y_gla, a["o"].astype(BF16)], axis=1)
    a["yo"] = mm(nm("out_proj"), a["y"], bw["wout"], F32)
    a["x1"], a["h2"] = rw(nm("res_norm2"), fn_res_norm2,
                          [rows(xt, tm=gt), rows(a["yo"], tm=gt), const(modl), const(sv["n2w"])],
                          [rowout(t, D, F32, tm=gt), rowout(t, D, BF16, tm=gt)], gg)
    a["act"] = mm(nm("ff1"), a["h2"], bw["w1"], BF16, post=lambda acc: jnp.maximum(acc, 0.0))
    a["f"] = mm(nm("ff2"), a["act"], bw["w2"], F32, pre=_square_bf16)
    x2, = rw(nm("res2"), fn_res2, [rows(a["x1"], tm=gt), rows(a["f"], tm=gt), const(modl)], [rowout(t, D, F32, tm=gt)], gg)
    return x2, a


def fn_assemble(blk, gv1, gv2, ckv, su, sv_, gr, dq, gk1, gk2, pg, kr, gq):
    return (jnp.concatenate([gv1 + gv2, ckv, su, sv_, gr, dq, gk1 + gk2, pg, kr, gq], axis=1),)


def layer_bwd(l, dx2, a, modl, bw, sv, side_a=None, make_side_b=None, ff_side=None):
    t = dx2.shape[0]
    g, gt = t // TM, _gla_tile(t)
    gg, cpt = t // gt, gt // GLA_CHUNK
    ht = TM
    nm = lambda s: f"l{l}_{s}_bwd"
    p = a["p"]
    ops = _layer_ops(p, sv, a)
    gw, gs = {}, {}
    df, dm_a = rw_vjp(nm("res2"), fn_res2, [rows(a["x1"], tm=gt), rows(a["f"], tm=gt), const(modl)], [rows(dx2, tm=gt)],
                      [1, 2], gg, gdt=[BF16, F32])
    during_attention = []
    if side_a:
        gw["w2"], got_a = mm_tn(nm("ff2_w"), a["act"], df, pre=_square_bf16, side=Side(side_a))
        during_attention += make_side_b(got_a)
    else:
        gw["w2"] = mm_tn(nm("ff2_w"), a["act"], df, pre=_square_bf16)
    du = mm(nm("ff2_x"), df, bw["w2"], BF16, post=lambda acc, act: acc * (2.0 * act.astype(F32)), extras=(a["act"],), bt=True)
    gw["w1"] = mm_tn(nm("ff1_w"), a["h2"], du)
    dh2 = mm(nm("ff1_x"), du, bw["w1"], F32, bt=True)
    dxa, dyo, dm_b, gs["n2w"] = rw_vjp(nm("res_norm2"), fn_res_norm2,
                                       [rows(a["x"], tm=ht), rows(a["yo"], tm=ht), const(modl), const(sv["n2w"])],
                                       [rows(dx2, tm=ht), rows(dh2, tm=ht)], [0, 1, 2, 3], t // ht, gdt=[F32, BF16, F32, F32])
    if ff_side is not None:
        ff_entries, make_ff_next = ff_side(dict(w_ff1=gw["w1"], w_ff2=gw["w2"]))
        gw["wout"], got_ff = mm_tn(nm("out_w"), a["y"], dyo, side=Side(ff_entries))
        during_attention += make_ff_next(got_ff)
    else:
        gw["wout"] = mm_tn(nm("out_w"), a["y"], dyo)
    dy = mm(nm("out_x"), dyo, bw["wout"], F32, bt=True)
    dsu, dsv, gs["sgu_nw"], gs["sgu_nb"], gs["sgu_w"], gs["sgu_bm"] = rw_vjp(
        nm("sgu"), fn_sgu, ops["sgu"], [rows(dy, 256, 0, tm=gt)], [0, 1, 2, 3, 4, 5], gg)
    dgq, dgk1, dgv1, dgf1, dgb1, dgr, dsf, dsb, gs["gla_nwt"] = rw_vjp(
        nm("gla_o"), fn_gla_o, ops["gla_o"](), [rows(dy, 256, 1, tm=gt)], list(range(9)), gg)
    duf, def_, dub, deb = gla_states_bwd(a["ef"], a["eb"], a["sf"], a["sb"], dsf, dsb)
    dgk2, dgv2, dgf, dgb = rw_vjp(nm("gla_kv"), fn_gla_kv, ops["gla_kv"](),
                                  [chunks(duf, cpt), chunks(def_, cpt), chunks(dub, cpt), chunks(deb, cpt)], [0, 1, 2, 3], gg,
                                  adds={2: rows(dgf1, tm=gt), 3: rows(dgb1, tm=gt)})
    dpg, gs["wg"], gs["bg"] = rw_vjp(nm("gates"), fn_gates, ops["gates"], [rows(dgf, tm=gt), rows(dgb, tm=gt)], [0, 1, 2], gg)
    stats, = rw(nm("attn_stats"), fn_attn_stats, [rows(dy, 512, 1, tm=gt), rows(a["o"], tm=gt), rows(a["lse"], tm=gt)],
                [chunkout((g, 64, TM), F32, gt // TM)], gg)
    dq, dkcat, dv, got_b = flash_bwd(a["q"], a["kcat"], a["kvu"], dy, stats,
                                     side=Side(during_attention) if during_attention else None)
    post_ins = [blank_rows(t, 512, BF16, gt), blank_rows(t, 1024, F32, gt), blank_rows(t, 128, F32, gt)] + ops["mla_post"]()[3:]
    dkk, dqu, dkr = rw_vjp(nm("mla_post"), fn_mla_post, post_ins, [rows(dkcat, tm=gt), rows(dq, tm=gt)], [0, 1, 2],
                           gg, gdt=[BF16, BF16, F32])
    dkvu = jnp.concatenate([dkk, dv.astype(BF16)], axis=1)
    gw["wukv"] = mm_tn(nm("kv_up_w"), a["ckvn"], dkvu)
    gw["wuq"] = mm_tn(nm("q_up_w"), a["dqn"], dqu)
    dckvn = mm(nm("kv_up_x"), dkvu, bw["wukv"], F32, bt=True)
    ddqn = mm(nm("q_up_x"), dqu, bw["wuq"], F32, bt=True)
    dckv, ddq, gs["kvw"], gs["qw"] = rw_vjp(nm("mla_pre"), fn_mla_pre, ops["mla_pre"],
                                            [rows(dckvn, tm=gt), rows(ddqn, tm=gt)], [0, 1, 2, 3], gg)
    dp, = rw(nm("assemble"), fn_assemble,
             [rows(x_, tm=gt) for x_ in (dgv1, dgv2, dckv, dsu, dsv, dgr, ddq, dgk1, dgk2, dpg, dkr, dgq)],
             [rowout(t, P_COLS, BF16, tm=gt)], gg)
    gw["win"] = mm_tn(nm("in_w"), a["h"], dp)
    dh = mm(nm("in_x"), dp, bw["win"], F32, bt=True)
    dx, dm_c, gs["n1w"] = rw_vjp(nm("norm1"), fn_norm1, [rows(a["x"], tm=gt), const(modl), const(sv["n1w"])],
                                 [rows(dh, tm=gt)], [0, 1, 2], gg, adds={0: rows(dxa, tm=gt)})
    big = dict(w_in=_in_from_padded(gw["win"]), w_out=gw["wout"], mla_w_uq=_uq_from_padded(gw["wuq"]),
               mla_w_ukv=_ukv_from_padded(gw["wukv"]), w_ff1=gw["w1"], w_ff2=gw["w2"])
    return dx, dm_a + dm_b + dm_c, big, _small_grads(gs), got_b


SMALL_NAMES = ("norm1_w", "sgu_norm_w", "sgu_norm_b", "sgu_w", "sgu_b", "gla_wg_fwd", "gla_bg_fwd", "gla_wg_bwd",
               "gla_bg_bwd", "gla_norm_w", "mla_q_norm_w", "mla_kv_norm_w", "norm2_w")
BIG_NAMES = ("w_in", "w_out", "mla_w_uq", "mla_w_ukv", "w_ff1", "w_ff2")
ATTN_WEIGHTS, FF_WEIGHTS = BIG_NAMES[:4], BIG_NAMES[4:]


def local_step(x, ctx, target, mods, big, small, final_norm_w, side=(), on_side=None, grad_side=None, ff_side=None):
    n = x.shape[0]
    xt = jnp.concatenate([ctx, x], axis=0)
    tabs = _rope_tables(n)
    depth = len(mods)
    big = list(big)
    svs = [_small_views(small[l]) for l in range(depth)]
    acts, bws = [], []
    for l in range(depth):
        bws.append(_big_views(big[l]))
        late = None
        if l == 0 and on_side is not None:
            def late(results):
                rest0, later = on_side(results)
                bws[0].update(_big_views(rest0))
                big.extend(later)
        xt, a = layer_fwd(l, xt, mods[l], bws[l], svs[l], tabs, side if l == 0 else (), late)
        acts.append(a)
    loss, dxt, dfnw = loss_head(xt, target, final_norm_w[None])
    dmods, gbig, gsmall = [None] * depth, [None] * depth, [None] * depth
    got = []
    for l in reversed(range(depth)):
        hooks = (None, None, None)
        if l == 0 and grad_side is not None and depth > 1:
            hooks = (*grad_side(gbig[1:]), ff_side)
        dxt, dmods[l], gbig[l], gsmall[l], got = layer_bwd(l, dxt, acts[l], mods[l], bws[l], svs[l], *hooks)
    return loss, dxt[CTX:], dmods, gbig, gsmall, dfnw, got


def _group(group):
    x, y, c = lax.axis_index("x"), lax.axis_index("y"), lax.axis_index("c")
    if group == "sib":
        return 2, c, [((x, y, 1 - c), 1 - c)]
    if group == "chip":
        flips = [(1, 0), (0, 1), (1, 1)]
        return 4, 2 * x + y, [((x ^ fx, y ^ fy, c), 2 * (x ^ fx) + (y ^ fy)) for fx, fy in flips]
    flips = [(fx, fy, fc) for fx in (0, 1) for fy in (0, 1) for fc in (0, 1)][1:]
    return 8, 4 * x + 2 * y + c, [((x ^ fx, y ^ fy, c ^ fc), 4 * (x ^ fx) + 2 * (y ^ fy) + (c ^ fc)) for fx, fy, fc in flips]


def _group_size(group):
    return {"sib": 2, "chip": 4, "all": 8}[group]


REMOTE_COPIES = {"gather": None, "scatter": None, "swap": 1, "gather2": 6, "gather8": 7}


def _exchange_shapes(entries):
    n_in = sum(len(arrs) for _, _, arrs in entries)
    n_remote = sum(REMOTE_COPIES[k] or _group_size(g) - 1 for k, g, _ in entries)
    n_local = sum(1 for k, _, _ in entries if k != "swap")
    out_shape = []
    for kind, group, arrs in entries:
        a = arrs[0]
        if kind in ("gather", "gather2", "gather8"):
            out_shape.append(jax.ShapeDtypeStruct((_group_size(group),) + a.shape, a.dtype))
        elif kind == "swap" and len(arrs) == 1:
            out_shape.append(jax.ShapeDtypeStruct(a.shape[1:], a.dtype))
        else:
            out_shape.append(jax.ShapeDtypeStruct(a.shape, a.dtype))
    return n_in, n_remote, n_local, out_shape


def _exchange_sems(entries):
    _, n_remote, n_local, _ = _exchange_shapes(entries)
    return [pltpu.SemaphoreType.DMA((n_remote,)), pltpu.SemaphoreType.DMA((n_remote,)), pltpu.SemaphoreType.DMA((max(n_local, 1),))]


def _exchange_phases(entries, in_refs, out_refs, send_sems, recv_sems, local_sems):
    x, y, c = lax.axis_index("x"), lax.axis_index("y"), lax.axis_index("c")

    def remote(src, dst, k, dev):
        return pltpu.make_async_remote_copy(src_ref=src, dst_ref=dst, send_sem=send_sems.at[k], recv_sem=recv_sems.at[k],
                                            device_id=dev, device_id_type=MESH)

    pos, k, kl = 0, 0, 0
    starts, forwards, finals = [], [], []
    for (kind, group, arrs), out in zip(entries, out_refs):
        srcs = in_refs[pos:pos + len(arrs)]
        pos += len(arrs)
        _, mine, peers = _group(group)
        if kind == "swap":
            (dev, _), = peers
            if len(srcs) == 1:
                starts.append(remote(srcs[0].at[1 - c], out, k, dev).start)
                finals.append(remote(srcs[0].at[0], out, k, dev).wait)
            else:
                def start_swap(srcs=srcs, k=k, dev=dev, out=out):
                    for core, src in ((0, srcs[1]), (1, srcs[0])):
                        @pl.when(c == core)
                        def _(src=src):
                            remote(src, out, k, dev).start()

                starts.append(start_swap)
                finals.append(remote(srcs[0], out, k, dev).wait)
            k += 1
            continue
        src = srcs[0]
        own = pltpu.make_async_copy(src if kind != "scatter" else src.at[mine], out.at[mine], local_sems.at[kl])
        starts.append(own.start)
        finals.append(own.wait)
        kl += 1
        if kind == "gather8":
            sibling = (x, y, 1 - c)
            starts.append(remote(src, out.at[mine], k, sibling).start)
            finals.append(remote(src, out.at[4 * x + 2 * y + (1 - c)], k, sibling).wait)
            for f, (dev, chip_slot) in enumerate(_group("chip")[2]):
                starts.append(remote(src, out.at[mine], k + 1 + f, dev).start)
                arrival = remote(src, out.at[2 * chip_slot + c], k + 1 + f, dev)

                def forward8(arrival=arrival, slot=2 * chip_slot + c, kf=k + 4 + f, out=out):
                    arrival.wait_recv()
                    remote(out.at[slot], out.at[slot], kf, sibling).start()

                forwards.append(forward8)
                finals.append(arrival.wait_send)
                finals.append(remote(out.at[2 * chip_slot + c], out.at[2 * chip_slot + (1 - c)], k + 4 + f, sibling).wait)
            k += 7
            continue
        if kind == "gather2":
            sibling = (x, y, 1 - c)
            for f, (dev, slot) in enumerate(peers):
                starts.append(remote(src.at[c], out.at[mine, c], k + f, dev).start)
                arrival = remote(src.at[c], out.at[slot, c], k + f, dev)

                def forward(arrival=arrival, slot=slot, kf=k + 3 + f, out=out):
                    arrival.wait_recv()
                    remote(out.at[slot, c], out.at[slot, c], kf, sibling).start()

                forwards.append(forward)
                finals.append(arrival.wait_send)
                finals.append(remote(out.at[slot, c], out.at[slot, 1 - c], k + 3 + f, sibling).wait)
            k += 6
            continue
        for dev, slot in peers:
            piece = src if kind == "gather" else src.at[slot]
            starts.append(remote(piece, out.at[mine], k, dev).start)
            finals.append(remote(piece, out.at[slot], k, dev).wait)
            k += 1
    return starts, forwards, finals


class Side:
    def __init__(self, entries):
        self.entries = tuple(entries)
        self.n_in, _, _, self.shapes = _exchange_shapes(self.entries)
        self.n_out = len(self.entries)
        self.arrays = [a for _, _, arrs in self.entries for a in arrs]
        any_spec = pl.BlockSpec(memory_space=pl.ANY)
        self.in_specs, self.out_specs = [any_spec] * self.n_in, [any_spec] * self.n_out
        self.sems = _exchange_sems(self.entries) if self.entries else []

    def start(self, in_refs, out_refs, sem_refs):
        if not self.entries:
            return lambda: None
        ids = [pl.program_id(d) for d in range(3)]
        first = (ids[0] == 0) & (ids[1] == 0) & (ids[2] == 0)
        last = ((ids[0] == pl.num_programs(0) - 1) & (ids[1] == pl.num_programs(1) - 1) & (ids[2] == pl.num_programs(2) - 1))
        starts, forwards, finals = _exchange_phases(self.entries, in_refs, out_refs, *sem_refs)
        assert not forwards

        @pl.when(first)
        def _():
            for run in starts:
                run()

        def finish():
            @pl.when(last)
            def _():
                for run in finals:
                    run()

        return finish


def xchg(name, entries):
    n_in, _, _, out_shape = _exchange_shapes(entries)

    def body(*refs):
        in_refs, out_refs = refs[:n_in], refs[n_in:n_in + len(entries)]
        for phase in _exchange_phases(entries, in_refs, out_refs, *refs[n_in + len(entries):]):
            for run in phase:
                run()

    any_spec = pl.BlockSpec(memory_space=pl.ANY)
    return pl.pallas_call(
        body, name=name,
        in_specs=[any_spec] * n_in, out_specs=[any_spec] * len(entries), out_shape=out_shape,
        scratch_shapes=_exchange_sems(entries),
    )(*[a for _, _, arrs in entries for a in arrs])


def _block_rows(r, c, budget=131072):
    tr = 8
    while tr * 2 * c <= budget and r % (tr * 2) == 0:
        tr *= 2
    return tr if r % tr == 0 else r


def tree_sum(name, parts):
    g, r, c = parts.shape
    tr = _block_rows(r, c)

    def body(p_ref, o_ref):
        p = [p_ref[i].astype(F32) for i in range(g)]
        while len(p) > 1:
            p = [p[i] + p[i + 1] for i in range(0, len(p), 2)]
        o_ref[...] = p[0]

    return pl.pallas_call(
        body, name=name, grid=(r // tr,),
        in_specs=[pl.BlockSpec((g, tr, c), lambda i: (0, i, 0))], out_specs=pl.BlockSpec((tr, c), lambda i: (i, 0)),
        out_shape=jax.ShapeDtypeStruct((r, c), F32), compiler_params=_params(("arbitrary",)),
    )(parts)


def pair_sum(name, halves, recv, core):
    r, c = recv.shape
    tr = _block_rows(r, c)

    def body(h_ref, r_ref, k_ref, o_ref):
        o_ref[...] = (jnp.where(k_ref[...] > 0.5, h_ref[1], h_ref[0]) + r_ref[...]).astype(o_ref.dtype)

    blk = pl.BlockSpec((tr, c), lambda i: (i, 0))
    return pl.pallas_call(
        body, name=name, grid=(r // tr,),
        in_specs=[pl.BlockSpec((2, tr, c), lambda i: (0, i, 0)), blk, pl.BlockSpec((1, 1), lambda i: (0, 0))],
        out_specs=blk, out_shape=jax.ShapeDtypeStruct((r, c), BF16), compiler_params=_params(("arbitrary",)),
    )(halves, recv, core)


def adamw(name, w, g, m, v):
    r, c = w.shape
    tr = _block_rows(r, c)

    def body(w_ref, g_ref, m_ref, v_ref, d_ref, nm_ref, nv_ref):
        gg = g_ref[...]
        nm = ADAM_B1 * m_ref[...] + (1.0 - ADAM_B1) * gg
        nv = ADAM_B2 * v_ref[...] + (1.0 - ADAM_B2) * jnp.square(gg)
        m_hat = nm / (1.0 - ADAM_B1 ** ADAM_STEP)
        v_hat = nv / (1.0 - ADAM_B2 ** ADAM_STEP)
        d_ref[...] = -ADAM_LR * (m_hat / (jnp.sqrt(v_hat) + ADAM_EPS) + ADAM_WD * w_ref[...])
        nm_ref[...] = nm
        nv_ref[...] = nv

    blk = pl.BlockSpec((tr, c), lambda i: (i, 0))
    return pl.pallas_call(
        body, name=name, grid=(r // tr,), in_specs=[blk] * 4, out_specs=[blk] * 3,
        out_shape=[jax.ShapeDtypeStruct((r, c), F32)] * 3, compiler_params=_params(("arbitrary",)),
    )(w, g, m, v)


W_MOD_COLS = 6 * D // 4
MOD_TN = 512


def mod_project(c16, w_mod, b_loc):
    def body(c_ref, w_ref, b_ref, o_ref):
        cv = c_ref[...]
        s = (cv * _sigmoid(cv)).astype(BF16)
        o_ref[0] = _dot(s, w_ref[0].astype(BF16)) + b_ref[0]

    return pl.pallas_call(
        body, name="mod_project", grid=(2, W_MOD_COLS // MOD_TN),
        in_specs=[pl.BlockSpec((16, D), lambda l, j: (0, 0)), pl.BlockSpec((1, D, MOD_TN), lambda l, j: (l, 0, j)),
                  pl.BlockSpec((1, 1, MOD_TN), lambda l, j: (l, 0, j))],
        out_specs=pl.BlockSpec((1, 16, MOD_TN), lambda l, j: (l, 0, j)),
        out_shape=jax.ShapeDtypeStruct((2, 16, W_MOD_COLS), F32), compiler_params=_params(("arbitrary", "arbitrary")),
    )(c16, w_mod, b_loc)


def mod_weight_grad(c16, dm16):
    def body(c_ref, d_ref, o_ref):
        cv = c_ref[...]
        o_ref[0] = _dot(cv * _sigmoid(cv), d_ref[0], ((0,), (0,)), precision=HI)

    return pl.pallas_call(
        body, name="mod_weight_grad", grid=(2, W_MOD_COLS // MOD_TN),
        in_specs=[pl.BlockSpec((16, D), lambda l, j: (0, 0)), pl.BlockSpec((1, 16, MOD_TN), lambda l, j: (l, 0, j))],
        out_specs=pl.BlockSpec((1, D, MOD_TN), lambda l, j: (l, 0, j)),
        out_shape=jax.ShapeDtypeStruct((2, D, W_MOD_COLS), F32), compiler_params=_params(("arbitrary", "arbitrary")),
    )(c16, dm16)


def cctx_partial(dmc, w_mod):
    def body(d_ref, w_ref, o_ref):
        @pl.when(pl.program_id(0) == 0)
        def _():
            o_ref[...] = jnp.zeros_like(o_ref)
        o_ref[...] += _dot(d_ref[0], w_ref[0], ((1,), (1,)), precision=HI)

    return pl.pallas_call(
        body, name="cctx_partial", grid=(2,),
        in_specs=[pl.BlockSpec((1, 8, W_MOD_COLS), lambda l: (l, 0, 0)), pl.BlockSpec((1, D, W_MOD_COLS), lambda l: (l, 0, 0))],
        out_specs=pl.BlockSpec((8, D), lambda l: (0, 0)),
        out_shape=jax.ShapeDtypeStruct((8, D), F32), compiler_params=_params(("arbitrary",)),
    )(dmc, w_mod)


def cctx_grad(parts, c_ctx8):
    def body(p_ref, c_ref, o_ref):
        ds = (p_ref[0] + p_ref[1]) + (p_ref[2] + p_ref[3])
        _, vf = jax.vjp(lambda z: z * _sigmoid(z), c_ref[...])
        o_ref[...] = vf(ds)[0]

    return pl.pallas_call(
        body, name="cctx_grad", out_shape=jax.ShapeDtypeStruct((8, D), F32),
    )(parts, c_ctx8)


ARG_NAMES = ("x", "c", "ctx", "c_ctx", "w_mod", "b_mod", "norm1_w", "w_in", "w_out", "sgu_norm_w", "sgu_norm_b", "sgu_w",
             "sgu_b", "gla_wg_fwd", "gla_bg_fwd", "gla_wg_bwd", "gla_bg_bwd", "gla_norm_w", "mla_q_norm_w", "mla_w_uq",
             "mla_kv_norm_w", "mla_w_ukv", "norm2_w", "w_ff1", "w_ff2", "final_norm_w")
WEIGHT_NAMES = ARG_NAMES[3:]
PACKED = ("c_ctx", "b_mod") + SMALL_NAMES + ("final_norm_w",)
ROW_SHARDED = ("w_out", "w_ff2")
PACK_ROWS = 256


def _pack(vectors):
    flat = jnp.concatenate([v.reshape(-1) for v in vectors])
    n = flat.shape[0]
    total = -(-n // (PACK_ROWS * LANES)) * PACK_ROWS * LANES
    return jnp.pad(flat, (0, total - n)).reshape(-1, LANES)


def _unpack(buf, shapes):
    flat, out, pos = buf.reshape(-1), [], 0
    for shp in shapes:
        n = int(np.prod(shp))
        out.append(flat[pos:pos + n].reshape(shp))
        pos += n
    return out


def _full_weight(name, g):
    if name in ROW_SHARDED:
        return g.reshape(-1, g.shape[-1])
    return g.transpose(1, 0, 2).reshape(g.shape[1], -1)


def _chip_chunks(name, a):
    if name in ROW_SHARDED:
        return a.reshape(4, a.shape[0] // 4, a.shape[1])
    return a.reshape(a.shape[0], 4, a.shape[1] // 4).transpose(1, 0, 2)


def kernel(x, c, ctx, c_ctx, w_mod, b_mod, norm1_w, w_in, w_out, sgu_norm_w, sgu_norm_b, sgu_w, sgu_b, gla_wg_fwd, gla_bg_fwd, gla_wg_bwd, gla_bg_bwd, gla_norm_w, mla_q_norm_w, mla_w_uq, mla_kv_norm_w, mla_w_ukv, norm2_w, w_ff1, w_ff2, final_norm_w, loss_target, m_c_ctx, m_w_mod, m_b_mod, m_norm1_w, m_w_in, m_w_out, m_sgu_norm_w, m_sgu_norm_b, m_sgu_w, m_sgu_b, m_gla_wg_fwd, m_gla_bg_fwd, m_gla_wg_bwd, m_gla_bg_bwd, m_gla_norm_w, m_mla_q_norm_w, m_mla_w_uq, m_mla_kv_norm_w, m_mla_w_ukv, m_norm2_w, m_w_ff1, m_w_ff2, m_final_norm_w, v_c_ctx, v_w_mod, v_b_mod, v_norm1_w, v_w_in, v_w_out, v_sgu_norm_w, v_sgu_norm_b, v_sgu_w, v_sgu_b, v_gla_wg_fwd, v_gla_bg_fwd, v_gla_wg_bwd, v_gla_bg_bwd, v_gla_norm_w, v_mla_q_norm_w, v_mla_w_uq, v_mla_kv_norm_w, v_mla_w_ukv, v_norm2_w, v_w_ff1, v_w_ff2, v_final_norm_w):
    args = (x, c, ctx, c_ctx, w_mod, b_mod, norm1_w, w_in, w_out, sgu_norm_w, sgu_norm_b, sgu_w, sgu_b, gla_wg_fwd, gla_bg_fwd, gla_wg_bwd, gla_bg_bwd, gla_norm_w, mla_q_norm_w, mla_w_uq, mla_kv_norm_w, mla_w_ukv, norm2_w, w_ff1, w_ff2, final_norm_w)
    w = dict(zip(ARG_NAMES, args))
    moms = (m_c_ctx, m_w_mod, m_b_mod, m_norm1_w, m_w_in, m_w_out, m_sgu_norm_w, m_sgu_norm_b, m_sgu_w, m_sgu_b, m_gla_wg_fwd, m_gla_bg_fwd, m_gla_wg_bwd, m_gla_bg_bwd, m_gla_norm_w, m_mla_q_norm_w, m_mla_w_uq, m_mla_kv_norm_w, m_mla_w_ukv, m_norm2_w, m_w_ff1, m_w_ff2, m_final_norm_w)
    vars_ = (v_c_ctx, v_w_mod, v_b_mod, v_norm1_w, v_w_in, v_w_out, v_sgu_norm_w, v_sgu_norm_b, v_sgu_w, v_sgu_b, v_gla_wg_fwd, v_gla_bg_fwd, v_gla_wg_bwd, v_gla_bg_bwd, v_gla_norm_w, v_mla_q_norm_w, v_mla_w_uq, v_mla_kv_norm_w, v_mla_w_ukv, v_norm2_w, v_w_ff1, v_w_ff2, v_final_norm_w)
    m1 = dict(zip(WEIGHT_NAMES, moms))
    m2 = dict(zip(WEIGHT_NAMES, vars_))
    xi, yi, ci = lax.axis_index("x"), lax.axis_index("y"), lax.axis_index("c")
    chip, dev = 2 * xi + yi, 4 * xi + 2 * yi + ci
    depth = w_mod.shape[0]

    def shard_halves(l, names):
        return [("gather2", "chip", [w[k][l].astype(BF16).reshape(2, w[k].shape[1] // 2, w[k].shape[2])]) for k in names]

    def full_weights(names, gathered):
        return {k: _full_weight(k, g.reshape(4, *w[k].shape[1:])) for k, g in zip(names, gathered)}

    got = xchg("gather_inputs", [("gather", "all", [c])] + shard_halves(0, ATTN_WEIGHTS))
    c_all = got[0]
    c16 = jnp.concatenate([c_all.reshape(8, D), c_ctx[None], jnp.zeros((7, D), F32)], axis=0)
    b_loc = lax.dynamic_slice_in_dim(b_mod, chip * W_MOD_COLS, W_MOD_COLS, axis=1)[:, None, :]
    mod_part = mod_project(c16, w_mod, b_loc)
    mod_all, = xchg("gather_mod", [("gather", "chip", [mod_part])])
    mod_full = mod_all.transpose(1, 2, 0, 3).reshape(depth, 16, 6 * D)
    mods = [jnp.stack([mod_full[l, 8], lax.dynamic_index_in_dim(mod_full[l], dev, 0, keepdims=False)])[:, None, :]
            for l in range(depth)]

    small = [{k: w[k][l] for k in SMALL_NAMES} for l in range(depth)]
    n_big = len(BIG_NAMES)
    n_ff = len(FF_WEIGHTS)
    later = shard_halves(0, FF_WEIGHTS) + [e for l in range(1, depth) for e in shard_halves(l, BIG_NAMES)]
    core = ci.astype(F32).reshape(1, 1)

    def on_side(res):
        return (full_weights(FF_WEIGHTS, res[:n_ff]),
                [full_weights(BIG_NAMES, res[n_ff + i * n_big:n_ff + (i + 1) * n_big]) for i in range(depth - 1)])

    def half_major(k, g):
        ch = _chip_chunks(k, g)
        return ch.reshape(4, 2, ch.shape[1] // 2, ch.shape[2]).transpose(1, 0, 2, 3)

    def swap_entries(gb, names):
        hm = [half_major(k, gb[k]) for k in names]
        return hm, [("swap", "sib", [h]) for h in hm]

    def scatter_entries(tag, names, hm, recv):
        out = []
        for k, h, r in zip(names, hm, recv):
            s2 = pair_sum(f"pair_sum_{tag}_{k}", h.reshape(2, -1, h.shape[-1]), r.reshape(-1, r.shape[-1]), core)
            out.append(("scatter", "chip", [s2.reshape(r.shape)]))
        return out

    def grad_side(gb_later):
        hms, entries = [], []
        for gb in gb_later:
            hm, e = swap_entries(gb, BIG_NAMES)
            hms.append(hm)
            entries += e

        def make_scatter(recv):
            return [e for i, hm in enumerate(hms)
                    for e in scatter_entries(f"l{i + 1}", BIG_NAMES, hm, recv[i * n_big:(i + 1) * n_big])]

        return entries, make_scatter

    def ff_side(g_ff):
        hm, entries = swap_entries(g_ff, FF_WEIGHTS)
        return entries, lambda recv: scatter_entries("l0", FF_WEIGHTS, hm, recv)

    loss, grad_x, dmods, gbig, gsmall, dfnw, early_pieces = local_step(
        x[0], ctx[0], loss_target[0], mods, [full_weights(ATTN_WEIGHTS, got[1:])], small, final_norm_w, side=later,
        on_side=on_side, grad_side=grad_side, ff_side=ff_side)

    dm_lat = jnp.stack([dmods[l][1, 0] for l in range(depth)])
    dm_ctx = jnp.stack([dmods[l][0, 0] for l in range(depth)])
    small_pack = _pack([dm_lat, dm_ctx] + [jnp.stack([gsmall[l][k] for l in range(depth)]) for k in SMALL_NAMES] + [dfnw, loss])
    hm0, swap0 = swap_entries(gbig[0], ATTN_WEIGHTS)
    got = xchg("exchange_grads", [("gather8", "all", [small_pack])] + swap0)
    small_all, recv0 = got[0], got[1:]
    small_sum = tree_sum("small_grad_sum", small_all)

    n_dm = depth * 6 * D
    dm_rows = n_dm // LANES
    dm_lat_all = small_all[:, :dm_rows].reshape(8, depth, 6 * D)
    dm_ctx_sum = small_sum[dm_rows:2 * dm_rows].reshape(depth, 6 * D)
    take = lambda a: lax.dynamic_slice_in_dim(a, chip * W_MOD_COLS, W_MOD_COLS, axis=-1)
    dmc_loc = take(dm_ctx_sum)
    cc_part = cctx_partial(jnp.pad(dmc_loc[:, None, :], ((0, 0), (0, 7), (0, 0))), w_mod)
    got = xchg("scatter_grads", [("gather", "chip", [cc_part])] + scatter_entries("l0", ATTN_WEIGHTS, hm0, recv0))
    cc_parts = got[0]
    keys = ([(0, k) for k in ATTN_WEIGHTS] + [(l, k) for l in range(1, depth) for k in BIG_NAMES] + [(0, k) for k in FF_WEIGHTS])
    pieces = dict(zip(keys, list(got[1:]) + list(early_pieces)))
    keys = [(l, k) for l in range(depth) for k in BIG_NAMES]
    reduced = {lk: tree_sum(f"chip_sum_l{lk[0]}_{lk[1]}", pieces[lk]) for lk in keys}
    g_c_ctx = cctx_grad(cc_parts, jnp.broadcast_to(c_ctx[None], (8, D)))[0]

    others = dict(zip(keys, xchg("share_halves", [("swap", "sib", [reduced[lk], reduced[lk]]) for lk in keys])))
    shard = {lk: jnp.where(ci == 0, jnp.concatenate([reduced[lk], others[lk]], axis=0),
                           jnp.concatenate([others[lk], reduced[lk]], axis=0)) for lk in keys}
    grads = {k: jnp.stack([shard[(l, k)] for l in range(depth)]) for k in BIG_NAMES}

    dm16 = jnp.concatenate([take(dm_lat_all).transpose(1, 0, 2), dmc_loc[:, None, :], jnp.zeros((depth, 7, W_MOD_COLS), F32)], axis=1)
    grads["w_mod"] = mod_weight_grad(c16, dm16)
    flat_sum = small_sum.reshape(-1)
    g_b_mod = (flat_sum[:n_dm] + flat_sum[n_dm:2 * n_dm]).reshape(depth, 6 * D)
    rest_shapes = [w[k].shape for k in PACKED[2:]]
    n_rest = sum(int(np.prod(s)) for s in rest_shapes)
    for k, g in zip(PACKED, [g_c_ctx, g_b_mod] + _unpack(flat_sum[2 * n_dm:2 * n_dm + n_rest], rest_shapes)):
        grads[k] = g
    loss = flat_sum[2 * n_dm + n_rest]

    delta, new_m, new_v = {}, {}, {}
    for k in BIG_NAMES + ("w_mod",):
        view = lambda a: a.reshape(-1, a.shape[-1])
        d_, m_, v_ = adamw(f"adamw_{k}", view(w[k]), view(grads[k]), view(m1[k]), view(m2[k]))
        delta[k], new_m[k], new_v[k] = d_.reshape(w[k].shape), m_.reshape(w[k].shape), v_.reshape(w[k].shape)
    shapes = [w[k].shape for k in PACKED]
    d_, m_, v_ = adamw("adamw_small", _pack([w[k] for k in PACKED]), _pack([grads[k] for k in PACKED]),
                       _pack([m1[k] for k in PACKED]), _pack([m2[k] for k in PACKED]))
    for k, dk, mk, vk in zip(PACKED, _unpack(d_, shapes), _unpack(m_, shapes), _unpack(v_, shapes)):
        delta[k], new_m[k], new_v[k] = dk, mk, vk
    return (loss, grad_x[None], *[grads[k] for k in WEIGHT_NAMES], *[delta[k] for k in WEIGHT_NAMES],
            *[new_m[k] for k in WEIGHT_NAMES], *[new_v[k] for k in WEIGHT_NAMES])
```
